```python
import jax, jax.numpy as jnp
from jax import lax
import numpy as np

D_MODEL = 2048
BATCH = 1
SEQ = 16384
DEPTH = 1

M_HEADS = 4
M_HEAD_DIM = 256
M_WIDTH = M_HEADS * M_HEAD_DIM
M_CHUNK = 128
CONV_WIDTH = 4
A_HEADS = 8
A_HEAD_DIM = 128
A_WIDTH = A_HEADS * A_HEAD_DIM
MOBA_BLOCK = 256
MOBA_TOPK = 3
Q_BLOCK = 128
ROPE_THETA = 10000.0
D_FF = 4 * D_MODEL
NORM_EPS = 1e-6

SPLITS = (M_WIDTH, M_WIDTH, M_WIDTH, M_WIDTH, M_HEADS, M_HEADS,
          A_WIDTH, A_WIDTH, A_WIDTH, D_MODEL, D_MODEL)
IN_COLS = 4 * M_WIDTH + 2 * M_HEADS + 3 * A_WIDTH + 2 * D_MODEL

kernel_name = 'mlstm_moba_gated_hybrid_block'


def rmsnorm(x, w):
    xf = x.astype(jnp.float32)
    y = xf * lax.rsqrt(jnp.mean(xf * xf, axis=-1, keepdims=True) + NORM_EPS)
    return (y * w.astype(jnp.float32)).astype(x.dtype)


def split_cols(u):
    outs, start = [], 0
    for width in SPLITS:
        outs.append(u[..., start:start + width])
        start += width
    return outs


def to_heads(u, n_heads):
    B, S, C = u.shape
    return u.reshape(B, S, n_heads, C // n_heads).transpose(0, 2, 1, 3)


def causal_depthwise_conv(u, w, b):
    S = u.shape[1]
    up = jnp.pad(u, ((0, 0), (CONV_WIDTH - 1, 0), (0, 0)))
    out = b
    for j in range(CONV_WIDTH):
        out = out + w[j] * up[:, j:j + S]
    return out


def apply_rope(x, positions):
    D = x.shape[-1]
    inv_freq = 1.0 / (ROPE_THETA ** (jnp.arange(0, D, 2, dtype=jnp.float32) / D))
    ang = positions.astype(jnp.float32)[..., None] * inv_freq
    ang = jnp.concatenate([ang, ang], axis=-1)[:, None]
    xf = x.astype(jnp.float32)
    x1, x2 = xf[..., :D // 2], xf[..., D // 2:]
    rot = jnp.concatenate([-x2, x1], axis=-1)
    return (xf * jnp.cos(ang) + rot * jnp.sin(ang)).astype(x.dtype)


def mlstm_chunkwise(q, k, v, i_pre, f_pre):
    B, H, S, D = q.shape
    L = M_CHUNK
    NC = S // L
    f32 = jnp.float32
    q = q.astype(f32).reshape(B, H, NC, L, D)
    k = (k.astype(f32) * (D ** -0.5)).reshape(B, H, NC, L, D)
    v = v.astype(f32).reshape(B, H, NC, L, D)
    log_f = jax.nn.log_sigmoid(f_pre.astype(f32)).reshape(B, H, NC, L)
    log_i = i_pre.astype(f32).reshape(B, H, NC, L)
    b = jnp.cumsum(log_f, axis=-1)
    F = b[..., -1]

    w = F[..., None] - b + log_i
    m_loc = jnp.max(w, axis=-1)
    e = jnp.exp(w - m_loc[..., None])
    C_loc = jnp.einsum('bhclv,bhclk->bhcvk', v * e[..., None], k)
    n_loc = jnp.einsum('bhcl,bhclk->bhck', e, k)

    def step(carry, xs):
        C, n, m = carry
        Fc, Cl, nl, ml = xs
        m_new = jnp.maximum(Fc + m, ml)
        a = jnp.exp(Fc + m - m_new)
        g = jnp.exp(ml - m_new)
        C_new = a[..., None, None] * C + g[..., None, None] * Cl
        n_new = a[..., None] * n + g[..., None] * nl
        return (C_new, n_new, m_new), (C, n, m)

    init = (jnp.zeros((B, H, D, D), f32), jnp.zeros((B, H, D), f32), jnp.zeros((B, H), f32))
    xs = (jnp.moveaxis(F, 2, 0), jnp.moveaxis(C_loc, 2, 0),
          jnp.moveaxis(n_loc, 2, 0), jnp.moveaxis(m_loc, 2, 0))
    _, (C_prev, n_prev, m_prev) = lax.scan(step, init, xs)
    C_prev = jnp.moveaxis(C_prev, 0, 2)
    n_prev = jnp.moveaxis(n_prev, 0, 2)
    m_prev = jnp.moveaxis(m_prev, 0, 2)

    a_log = b + m_prev[..., None]
    d_log = b[..., :, None] - b[..., None, :] + log_i[..., None, :]
    causal = jnp.tril(jnp.ones((L, L), dtype=bool))
    d_log = jnp.where(causal, d_log, -jnp.inf)
    m_t = jnp.maximum(a_log, jnp.max(d_log, axis=-1))
    s_ts = jnp.einsum('bhctd,bhcsd->bhcts', q, k) * jnp.exp(d_log - m_t[..., None])
    inter = jnp.exp(a_log - m_t)
    num = (inter[..., None] * jnp.einsum('bhcvk,bhctk->bhctv', C_prev, q)
           + jnp.einsum('bhcts,bhcsv->bhctv', s_ts, v))
    den = inter * jnp.einsum('bhck,bhctk->bhct', n_prev, q) + jnp.sum(s_ts, axis=-1)
    h = num / jnp.maximum(jnp.abs(den), jnp.exp(-m_t))[..., None]
    return h.reshape(B, H, S, D)


def moba_attention(q, k, v):
    B, H, S, D = q.shape
    BS = MOBA_BLOCK
    NB = -(-S // BS)
    K_SEL = min(MOBA_TOPK, NB)
    pad = NB * BS - S
    kp = jnp.pad(k, ((0, 0), (0, 0), (0, pad), (0, 0))).reshape(B, H, NB, BS, D)
    vp = jnp.pad(v, ((0, 0), (0, 0), (0, pad), (0, 0))).reshape(B, H, NB, BS, D)
    k_mean = jnp.mean(kp.astype(jnp.float32), axis=3)
    scale = D ** -0.5
    NQB = S // Q_BLOCK
    blk_ids = jnp.arange(NB)
    b_idx = jnp.arange(B)[:, None, None, None]
    h_idx = jnp.arange(H)[None, :, None, None]
    n_sel = K_SEL * BS

    def one_block(qb):
        start = qb * Q_BLOCK
        qf = lax.dynamic_slice_in_dim(q, start, Q_BLOCK, axis=2).astype(jnp.float32)
        own = start // BS
        gate = jnp.einsum('bhqd,bhnd->bhqn', qf, k_mean)
        gate = jnp.where(blk_ids < own, gate, -jnp.inf)
        _, sel = lax.top_k(gate, K_SEL)
        valid = jnp.arange(K_SEL) < own
        k_sel = kp[b_idx, h_idx, sel].astype(jnp.float32)
        v_sel = vp[b_idx, h_idx, sel].astype(jnp.float32)
        s_sel = jnp.einsum('bhqd,bhqnkd->bhqnk', qf, k_sel) * scale
        s_sel = jnp.where(valid[:, None], s_sel, -jnp.inf).reshape(B, H, Q_BLOCK, n_sel)
        k_own = lax.dynamic_index_in_dim(kp, own, axis=2, keepdims=False).astype(jnp.float32)
        v_own = lax.dynamic_index_in_dim(vp, own, axis=2, keepdims=False).astype(jnp.float32)
        s_own = jnp.einsum('bhqd,bhkd->bhqk', qf, k_own) * scale
        q_pos = start + jnp.arange(Q_BLOCK)
        k_pos = own * BS + jnp.arange(BS)
        s_own = jnp.where(k_pos[None, :] <= q_pos[:, None], s_own, -jnp.inf)
        p = jax.nn.softmax(jnp.concatenate([s_sel, s_own], axis=-1), axis=-1)
        o = (jnp.einsum('bhqm,bhqmd->bhqd', p[..., :n_sel],
                        v_sel.reshape(B, H, Q_BLOCK, n_sel, D))
             + jnp.einsum('bhqk,bhkd->bhqd', p[..., n_sel:], v_own))
        return o.astype(q.dtype)

    out = lax.map(one_block, jnp.arange(NQB))
    return jnp.moveaxis(out, 0, 2).reshape(B, H, S, D)


def setup_inputs(seed: int = 0) -> dict:
    key = jax.random.key(seed)
    ks = jax.random.split(key, 16)
    f32 = jnp.float32

    def gain(k, n):
        return 1.0 + 0.05 * jax.random.normal(k, (DEPTH, n), f32)

    def dense(k, fan_in, fan_out):
        return jax.random.normal(k, (DEPTH, fan_in, fan_out), f32) * (fan_in ** -0.5)

    x = jax.random.normal(ks[0], (BATCH, SEQ, D_MODEL), f32)
    positions = jnp.broadcast_to(jnp.arange(SEQ, dtype=jnp.int32), (BATCH, SEQ))
    f_bias = (jnp.linspace(3.0, 6.0, M_HEADS, dtype=f32)[None, :]
              + 0.1 * jax.random.normal(ks[6], (DEPTH, M_HEADS), f32))
    return {
        'x': x,
        'positions': positions,
        'norm_mix_pre': gain(ks[1], D_MODEL),
        'w_in': dense(ks[2], D_MODEL, IN_COLS),
        'conv_w': jax.random.normal(ks[3], (DEPTH, CONV_WIDTH, 2 * M_WIDTH), f32) * (CONV_WIDTH ** -0.5),
        'conv_b': 0.01 * jax.random.normal(ks[4], (DEPTH, 2 * M_WIDTH), f32),
        'i_bias': 0.1 * jax.random.normal(ks[5], (DEPTH, M_HEADS), f32),
        'f_bias': f_bias,
        'mlstm_norm': gain(ks[7], M_WIDTH),
        'w_branch_m': dense(ks[8], M_WIDTH, D_MODEL),
        'w_branch_a': dense(ks[9], A_WIDTH, D_MODEL),
        'w_out': dense(ks[10], D_MODEL, D_MODEL),
        'norm_mix_post': gain(ks[11], D_MODEL),
        'norm_ffn_pre': gain(ks[12], D_MODEL),
        'w_up': dense(ks[13], D_MODEL, D_FF),
        'w_down': dense(ks[14], D_FF, D_MODEL),
        'norm_ffn_post': gain(ks[15], D_MODEL),
    }


def reference(x, positions, norm_mix_pre, w_in, conv_w, conv_b, i_bias, f_bias, mlstm_norm,
              w_branch_m, w_branch_a, w_out, norm_mix_post, norm_ffn_pre, w_up, w_down,
              norm_ffn_post):
    dt = x.dtype
    B, S, _ = x.shape
    for l in range(DEPTH):
        xn = rmsnorm(x, norm_mix_pre[l])
        proj = jnp.einsum('bsd,dc->bsc', xn, w_in[l])
        mq, mk, mv, mo, mi, mf, aq, ak, av, g_m, g_a = split_cols(proj)

        qk = jax.nn.silu(causal_depthwise_conv(jnp.concatenate([mq, mk], axis=-1),
                                               conv_w[l], conv_b[l]))
        mq_h = to_heads(qk[..., :M_WIDTH], M_HEADS)
        mk_h = to_heads(qk[..., M_WIDTH:], M_HEADS)
        mv_h = to_heads(mv, M_HEADS)
        i_pre = jnp.transpose(mi + i_bias[l], (0, 2, 1))
        f_pre = jnp.transpose(mf + f_bias[l], (0, 2, 1))
        h = mlstm_chunkwise(mq_h, mk_h, mv_h, i_pre, f_pre)
        h = h * lax.rsqrt(jnp.mean(h * h, axis=-1, keepdims=True) + NORM_EPS)
        h = h.transpose(0, 2, 1, 3).reshape(B, S, M_WIDTH) * mlstm_norm[l].astype(jnp.float32)
        h_m = (h * jax.nn.sigmoid(mo.astype(jnp.float32))).astype(dt)

        aq_h = apply_rope(to_heads(aq, A_HEADS), positions)
        ak_h = apply_rope(to_heads(ak, A_HEADS), positions)
        av_h = to_heads(av, A_HEADS)
        o_a = moba_attention(aq_h, ak_h, av_h)
        h_a = o_a.transpose(0, 2, 1, 3).reshape(B, S, A_WIDTH)

        y_m = jnp.einsum('bsc,cd->bsd', h_m, w_branch_m[l])
        y_a = jnp.einsum('bsc,cd->bsd', h_a, w_branch_a[l])
        merged = jax.nn.sigmoid(g_m) * y_m + jax.nn.sigmoid(g_a) * y_a
        mix_out = jnp.einsum('bsd,de->bse', merged, w_out[l])
        x = x + rmsnorm(mix_out, norm_mix_post[l])

        hn = rmsnorm(x, norm_ffn_pre[l])
        u = jnp.square(jax.nn.relu(jnp.einsum('bsd,df->bsf', hn, w_up[l])))
        y = jnp.einsum('bsf,fd->bsd', u, w_down[l])
        x = x + rmsnorm(y, norm_ffn_post[l])
    return x
```

```python
import functools

import jax
import jax.numpy as jnp
from jax import lax
from jax.experimental import pallas as pl
from jax.experimental.pallas import tpu as pltpu

F32 = jnp.float32
BF16 = jnp.bfloat16

M_HEADS = 4
M_HEAD_DIM = 256
M_WIDTH = M_HEADS * M_HEAD_DIM
M_CHUNK = 128
CONV_WIDTH = 4
A_HEADS = 8
A_HEAD_DIM = 128
A_WIDTH = A_HEADS * A_HEAD_DIM
MOBA_BLOCK = 256
MOBA_TOPK = 3
ROPE_THETA = 10000.0
NORM_EPS = 1e-6

LANES = 128
SUBLANES = 8
VMEM_LIMIT = 56 * 1024 * 1024
NEG_BIG = -1e30

COL_GM = 0
COL_GA = 2048
COL_MQK = 4096
COL_AQK = 6144
COL_MV = 8192
COL_MO = 9216
COL_AV = 10240
P_COLS = 11264


def _cparams(sem):
    return pltpu.CompilerParams(dimension_semantics=sem, vmem_limit_bytes=VMEM_LIMIT)


def _rms(x, gain):
    ms = jnp.mean(x * x, axis=-1, keepdims=True)
    return x * lax.rsqrt(ms + NORM_EPS) * gain


def _sigmoid(x):
    return 1.0 / (1.0 + jnp.exp(-x))


def _split3(x):
    hi = x.astype(BF16)
    r1 = x - hi.astype(F32)
    mid = r1.astype(BF16)
    lo = (r1 - mid.astype(F32)).astype(BF16)
    return hi, mid, lo


def _proj_kernel(x_ref, g_ref, w_ref, wg_ref, p_ref, gate_ref, xn_ref):
    @pl.when(pl.program_id(1) == 0)
    def _():
        xn = _rms(x_ref[...], g_ref[...]).astype(BF16)
        xn_ref[...] = xn
        gate_ref[...] = jnp.dot(xn, wg_ref[...], preferred_element_type=F32)

    p_ref[...] = jnp.dot(xn_ref[...], w_ref[...], preferred_element_type=F32)


def _proj(x, gain, w_all, w_gate, tm=512, tn=1024):
    S, D = x.shape
    N = w_all.shape[1]
    return pl.pallas_call(
        _proj_kernel,
        grid=(S // tm, N // tn),
        in_specs=[
            pl.BlockSpec((tm, D), lambda i, j: (i, 0)),
            pl.BlockSpec((1, D), lambda i, j: (0, 0)),
            pl.BlockSpec((D, tn), lambda i, j: (0, j)),
            pl.BlockSpec((D, LANES), lambda i, j: (0, 0)),
        ],
        out_specs=[
            pl.BlockSpec((tm, tn), lambda i, j: (i, j)),
            pl.BlockSpec((tm, LANES), lambda i, j: (i, 0)),
        ],
        out_shape=[
            jax.ShapeDtypeStruct((S, N), F32),
            jax.ShapeDtypeStruct((S, LANES), F32),
        ],
        scratch_shapes=[pltpu.VMEM((tm, D), BF16)],
        compiler_params=_cparams(("parallel", "arbitrary")),
        name="proj",
    )(x, gain, w_all, w_gate)


PREP_ROWS = MOBA_BLOCK
PREP_COLS = 512


def _prep_kernel(pmk_ref, halo_ref, paqk_ref, pav_ref, gate_ref, pos_ref, invf_ref,
                 cw_ref, cb_ref, gb_ref,
                 mq_ref, mkT_ref, aq_ref, ak_ref, avT_ref, kmean_ref, gc_ref, gt_ref,
                 buf_ref):
    i = pl.program_id(0)
    R = PREP_ROWS

    halo = halo_ref[...]
    buf_ref[0:SUBLANES, :] = jnp.where(i == 0, jnp.zeros_like(halo), halo)
    buf_ref[SUBLANES:SUBLANES + R, :] = pmk_ref[...]
    k_scale = M_HEAD_DIM ** -0.5
    for c0 in range(0, 2 * M_WIDTH, PREP_COLS):
        cs = slice(c0, c0 + PREP_COLS)
        acc = cb_ref[:, cs]
        for j in range(CONV_WIDTH):
            off = SUBLANES - (CONV_WIDTH - 1) + j
            acc = acc + cw_ref[j:j + 1, cs] * buf_ref[off:off + R, cs]
        y = acc * _sigmoid(acc)
        if c0 < M_WIDTH:
            mq_ref[:, cs] = y.astype(BF16)
        else:
            ks = slice(c0 - M_WIDTH, c0 - M_WIDTH + PREP_COLS)
            mkT_ref[ks, :] = (y * k_scale).T.astype(BF16)

    ang = pos_ref[...].astype(F32) * invf_ref[...]
    cos = jnp.cos(ang)
    lane = lax.broadcasted_iota(jnp.int32, (R, A_HEAD_DIM), 1)
    sin_signed = jnp.where(lane < A_HEAD_DIM // 2, -1.0, 1.0) * jnp.sin(ang)
    for h in range(2 * A_HEADS):
        hs = slice(h * A_HEAD_DIM, (h + 1) * A_HEAD_DIM)
        xh = paqk_ref[:, hs]
        yh = xh * cos + pltpu.roll(xh, A_HEAD_DIM // 2, 1) * sin_signed
        if h < A_HEADS:
            aq_ref[:, hs] = yh
        else:
            ko = slice((h - A_HEADS) * A_HEAD_DIM, (h - A_HEADS + 1) * A_HEAD_DIM)
            ak_ref[:, ko] = yh.astype(BF16)
            kmean_ref[0, :, ko] = jnp.mean(yh, axis=0, keepdims=True)
    for c0 in range(0, A_WIDTH, PREP_COLS):
        avT_ref[c0:c0 + PREP_COLS, :] = pav_ref[:, c0:c0 + PREP_COLS].T.astype(BF16)

    g = gate_ref[...] + gb_ref[...]
    log_f = jnp.minimum(g, 0.0) - jnp.log1p(jnp.exp(-jnp.abs(g)))
    r_i = lax.broadcasted_iota(jnp.int32, (R, R), 0)
    c_i = lax.broadcasted_iota(jnp.int32, (R, R), 1)
    tri = ((r_i >= c_i) & ((r_i // M_CHUNK) == (c_i // M_CHUNK))).astype(BF16)
    hi, mid, lo = _split3(log_f)
    csum = (jnp.dot(tri, hi, preferred_element_type=F32)
            + jnp.dot(tri, mid, preferred_element_type=F32)
            + jnp.dot(tri, lo, preferred_element_type=F32))
    glane = lax.broadcasted_iota(jnp.int32, (R, LANES), 1)
    gc = jnp.where(glane < M_HEADS, g, csum)
    gc_ref[...] = gc
    gt_ref[...] = gc.T[0:SUBLANES, :]


def _prep(P, gate, pos, invf, conv_w, conv_b, gbias):
    S = P.shape[0]
    R = PREP_ROWS
    nb = S // R
    halo_blocks = R // SUBLANES
    return pl.pallas_call(
        _prep_kernel,
        grid=(nb,),
        in_specs=[
            pl.BlockSpec((R, 2 * M_WIDTH), lambda i: (i, COL_MQK // (2 * M_WIDTH))),
            pl.BlockSpec((SUBLANES, 2 * M_WIDTH),
                         lambda i: (jnp.maximum(i * halo_blocks - 1, 0), COL_MQK // (2 * M_WIDTH))),
            pl.BlockSpec((R, 2 * A_WIDTH), lambda i: (i, COL_AQK // (2 * A_WIDTH))),
            pl.BlockSpec((R, A_WIDTH), lambda i: (i, COL_AV // A_WIDTH)),
            pl.BlockSpec((R, LANES), lambda i: (i, 0)),
            pl.BlockSpec((R, 1), lambda i: (i, 0)),
            pl.BlockSpec((1, A_HEAD_DIM), lambda i: (0, 0)),
            pl.BlockSpec((CONV_WIDTH, 2 * M_WIDTH), lambda i: (0, 0)),
            pl.BlockSpec((1, 2 * M_WIDTH), lambda i: (0, 0)),
            pl.BlockSpec((1, LANES), lambda i: (0, 0)),
        ],
        out_specs=[
            pl.BlockSpec((R, M_WIDTH), lambda i: (i, 0)),
            pl.BlockSpec((M_WIDTH, R), lambda i: (0, i)),
            pl.BlockSpec((R, A_WIDTH), lambda i: (i, 0)),
            pl.BlockSpec((R, A_WIDTH), lambda i: (i, 0)),
            pl.BlockSpec((A_WIDTH, R), lambda i: (0, i)),
            pl.BlockSpec((1, 1, A_WIDTH), lambda i: (i, 0, 0)),
            pl.BlockSpec((R, LANES), lambda i: (i, 0)),
            pl.BlockSpec((SUBLANES, R), lambda i: (0, i)),
        ],
        out_shape=[
            jax.ShapeDtypeStruct((S, M_WIDTH), BF16),
            jax.ShapeDtypeStruct((M_WIDTH, S), BF16),
            jax.ShapeDtypeStruct((S, A_WIDTH), F32),
            jax.ShapeDtypeStruct((S, A_WIDTH), BF16),
            jax.ShapeDtypeStruct((A_WIDTH, S), BF16),
            jax.ShapeDtypeStruct((nb, 1, A_WIDTH), F32),
            jax.ShapeDtypeStruct((S, LANES), F32),
            jax.ShapeDtypeStruct((SUBLANES, S), F32),
        ],
        scratch_shapes=[pltpu.VMEM((SUBLANES + R, 2 * M_WIDTH), F32)],
        compiler_params=_cparams(("parallel",)),
        name="prep",
    )(P, P, P, P, gate, pos, invf, conv_w, conv_b, gbias)


M_AUG = M_HEAD_DIM + LANES


def _mlstm_kernel(q_ref, kT_ref, v_ref, mo_ref, gc_ref, gt_ref, gain_ref, out_ref,
                  c_ref, m_ref):
    @pl.when(pl.program_id(0) == 0)
    def _():
        c_ref[...] = jnp.zeros_like(c_ref)
        m_ref[...] = jnp.zeros_like(m_ref)

    L = M_CHUNK
    D = M_HEAD_DIM
    row = lax.broadcasted_iota(jnp.int32, (L, L), 0)
    col = lax.broadcasted_iota(jnp.int32, (L, L), 1)
    causal = row >= col
    ones_col = (lax.broadcasted_iota(jnp.int32, (L, LANES), 1) == 0).astype(BF16)

    for h in range(M_HEADS):
        hs = slice(h * D, (h + 1) * D)
        q = q_ref[:, hs]
        kT = kT_ref[hs, :]
        v_aug = jnp.concatenate([v_ref[:, hs].astype(BF16), ones_col], axis=1)
        b_c = gc_ref[:, M_HEADS + h:M_HEADS + h + 1]
        li_r = gt_ref[h:h + 1, :]
        b_r = gt_ref[M_HEADS + h:M_HEADS + h + 1, :]
        f_tot = b_r[:, L - 1:L]
        m_prev = m_ref[h:h + 1, 0:1]

        u_r = li_r - b_r
        d_log = jnp.where(causal, b_c + u_r, -jnp.inf)
        a_log = b_c + m_prev
        m_t = jnp.maximum(a_log, jnp.max(d_log, axis=1, keepdims=True))
        s_qk = jnp.dot(q, kT, preferred_element_type=F32)
        s_ts = s_qk * jnp.exp(d_log - m_t)
        inter = jnp.exp(a_log - m_t)
        c_old = c_ref[h]
        r = (inter * jnp.dot(q, c_old.astype(BF16), preferred_element_type=F32)
             + jnp.dot(s_ts.astype(BF16), v_aug, preferred_element_type=F32))
        num = r[:, :D]
        den = r[:, D:D + 1]
        hh = num / jnp.maximum(jnp.abs(den), jnp.exp(-m_t))
        hn = hh * lax.rsqrt(jnp.mean(hh * hh, axis=-1, keepdims=True) + NORM_EPS)
        out_ref[:, hs] = (hn * gain_ref[:, hs] * _sigmoid(mo_ref[:, hs])).astype(BF16)

        w_r = f_tot + u_r
        m_loc = jnp.max(w_r, axis=1, keepdims=True)
        m_new = jnp.maximum(f_tot + m_prev, m_loc)
        a = jnp.exp(f_tot + m_prev - m_new)
        e_r = jnp.exp(w_r - m_new)
        keT = (kT.astype(F32) * e_r).astype(BF16)
        c_ref[h] = a * c_old + jnp.dot(keT, v_aug, preferred_element_type=F32)
        m_ref[h:h + 1, :] = jnp.broadcast_to(m_new, (1, LANES))


def _mlstm(mq, mkT, P, gc, gt, gain):
    S = mq.shape[0]
    L = M_CHUNK
    return pl.pallas_call(
        _mlstm_kernel,
        grid=(S // L,),
        in_specs=[
            pl.BlockSpec((L, M_WIDTH), lambda c: (c, 0)),
            pl.BlockSpec((M_WIDTH, L), lambda c: (0, c)),
            pl.BlockSpec((L, M_WIDTH), lambda c: (c, COL_MV // M_WIDTH)),
            pl.BlockSpec((L, M_WIDTH), lambda c: (c, COL_MO // M_WIDTH)),
            pl.BlockSpec((L, LANES), lambda c: (c, 0)),
            pl.BlockSpec((SUBLANES, L), lambda c: (0, c)),
            pl.BlockSpec((1, M_WIDTH), lambda c: (0, 0)),
        ],
        out_specs=pl.BlockSpec((L, M_WIDTH), lambda c: (c, 0)),
        out_shape=jax.ShapeDtypeStruct((S, M_WIDTH), BF16),
        scratch_shapes=[
            pltpu.VMEM((M_HEADS, M_HEAD_DIM, M_AUG), F32),
            pltpu.VMEM((SUBLANES, LANES), F32),
        ],
        compiler_params=_cparams(("arbitrary",)),
        name="mlstm",
    )(mq, mkT, P, P, gc, gt, gain)


def _moba_kernel(q_ref, k_ref, vT_ref, km_ref, o_ref, bias_ref, m_ref, l_ref, acc_ref):
    i = pl.program_id(1)
    BS = MOBA_BLOCK
    NB = km_ref.shape[0]

    qT = q_ref[...].T

    km = km_ref[...]
    kh = km.astype(BF16)
    kl = (km - kh.astype(F32)).astype(BF16)
    qh = qT.astype(BF16)
    ql = (qT - qh.astype(F32)).astype(BF16)
    gate = (jnp.dot(kh, qh, preferred_element_type=F32)
            + jnp.dot(kh, ql, preferred_element_type=F32)
            + jnp.dot(kl, qh, preferred_element_type=F32))
    blk = lax.broadcasted_iota(jnp.int32, (NB, BS), 0)
    gate = jnp.where(blk < i, gate, -jnp.inf)
    bias = jnp.full((NB, BS), NEG_BIG, F32)
    for r in range(MOBA_TOPK):
        mx = jnp.max(gate, axis=0, keepdims=True)
        idx = jnp.min(jnp.where(gate == mx, blk, NB), axis=0, keepdims=True)
        idx = jnp.where(r < i, idx, -1)
        pick = blk == idx
        bias = jnp.where(pick, 0.0, bias)
        gate = jnp.where(pick, -jnp.inf, gate)
    bias_ref[...] = bias

    qs = (qT * (A_HEAD_DIM ** -0.5)).astype(BF16)

    own = pl.multiple_of(i * BS, BS)
    sT = jnp.dot(k_ref[pl.ds(own, BS), :], qs, preferred_element_type=F32)
    kpos = lax.broadcasted_iota(jnp.int32, (BS, BS), 0)
    qpos = lax.broadcasted_iota(jnp.int32, (BS, BS), 1)
    sT = jnp.where(kpos <= qpos, sT, NEG_BIG)
    m0 = jnp.max(sT, axis=0, keepdims=True)
    p0 = jnp.exp(sT - m0)
    m_ref[...] = m0
    l_ref[...] = jnp.sum(p0, axis=0, keepdims=True)
    acc_ref[...] = jnp.dot(vT_ref[:, pl.ds(own, BS)], p0.astype(BF16), preferred_element_type=F32)

    def body(j, carry):
        start = pl.multiple_of(j * BS, BS)
        s = jnp.dot(k_ref[pl.ds(start, BS), :], qs, preferred_element_type=F32)
        s = s + bias_ref[pl.ds(j, 1), :]
        m_old = m_ref[...]
        m_new = jnp.maximum(m_old, jnp.max(s, axis=0, keepdims=True))
        alpha = jnp.exp(m_old - m_new)
        p = jnp.exp(s - m_new)
        l_ref[...] = alpha * l_ref[...] + jnp.sum(p, axis=0, keepdims=True)
        acc_ref[...] = alpha * acc_ref[...] + jnp.dot(
            vT_ref[:, pl.ds(start, BS)], p.astype(BF16), preferred_element_type=F32)
        m_ref[...] = m_new
        return carry

    lax.fori_loop(0, i, body, 0)

    o_ref[...] = (acc_ref[...] / l_ref[...]).T.astype(BF16)


def _moba(aq, ak, avT, kmean):
    S = aq.shape[0]
    BS = MOBA_BLOCK
    NB = S // BS
    Dh = A_HEAD_DIM
    return pl.pallas_call(
        _moba_kernel,
        grid=(A_HEADS, NB),
        in_specs=[
            pl.BlockSpec((BS, Dh), lambda h, i: (i, h)),
            pl.BlockSpec((S, Dh), lambda h, i: (0, h)),
            pl.BlockSpec((Dh, S), lambda h, i: (h, 0)),
            pl.BlockSpec((NB, Dh), lambda h, i: (0, h)),
        ],
        out_specs=pl.BlockSpec((BS, Dh), lambda h, i: (i, h)),
        out_shape=jax.ShapeDtypeStruct((S, A_WIDTH), BF16),
        scratch_shapes=[
            pltpu.VMEM((NB, BS), F32),
            pltpu.VMEM((1, BS), F32),
            pltpu.VMEM((1, BS), F32),
            pltpu.VMEM((Dh, BS), F32),
        ],
        compiler_params=_cparams(("parallel", "arbitrary")),
        name="moba",
    )(aq, ak, avT, kmean)


def _merge_kernel(hm_ref, ha_ref, gm_ref, ga_ref, x_ref, wm_ref, wa_ref, wo_ref, gain_ref,
                  out_ref):
    ym = jnp.dot(hm_ref[...], wm_ref[...], preferred_element_type=F32)
    ya = jnp.dot(ha_ref[...], wa_ref[...], preferred_element_type=F32)
    merged = _sigmoid(gm_ref[...]) * ym + _sigmoid(ga_ref[...]) * ya
    mix = jnp.dot(merged.astype(BF16), wo_ref[...], preferred_element_type=F32)
    out_ref[...] = x_ref[...] + _rms(mix, gain_ref[...])


def _merge(hm, ha, P, x, wm, wa, wo, gain, tm=256):
    S, D = x.shape
    const = pl.Buffered(1)
    return pl.pallas_call(
        _merge_kernel,
        grid=(S // tm,),
        in_specs=[
            pl.BlockSpec((tm, M_WIDTH), lambda i: (i, 0)),
            pl.BlockSpec((tm, A_WIDTH), lambda i: (i, 0)),
            pl.BlockSpec((tm, D), lambda i: (i, COL_GM // D)),
            pl.BlockSpec((tm, D), lambda i: (i, COL_GA // D)),
            pl.BlockSpec((tm, D), lambda i: (i, 0)),
            pl.BlockSpec((M_WIDTH, D), lambda i: (0, 0), pipeline_mode=const),
            pl.BlockSpec((A_WIDTH, D), lambda i: (0, 0), pipeline_mode=const),
            pl.BlockSpec((D, D), lambda i: (0, 0), pipeline_mode=const),
            pl.BlockSpec((1, D), lambda i: (0, 0)),
        ],
        out_specs=pl.BlockSpec((tm, D), lambda i: (i, 0)),
        out_shape=jax.ShapeDtypeStruct((S, D), F32),
        compiler_params=_cparams(("parallel",)),
        name="merge",
    )(hm, ha, P, P, x, wm, wa, wo, gain)


def _ffn_kernel(x_ref, gpre_ref, wu_ref, wd_ref, gpost_ref, out_ref, hn_ref, acc_ref):
    f = pl.program_id(1)

    @pl.when(f == 0)
    def _():
        hn_ref[...] = _rms(x_ref[...], gpre_ref[...]).astype(BF16)
        acc_ref[...] = jnp.zeros_like(acc_ref)

    u = jnp.dot(hn_ref[...], wu_ref[...], preferred_element_type=F32)
    u = jnp.square(jnp.maximum(u, 0.0)).astype(BF16)
    acc_ref[...] += jnp.dot(u, wd_ref[...], preferred_element_type=F32)

    @pl.when(f == pl.num_programs(1) - 1)
    def _():
        out_ref[...] = x_ref[...] + _rms(acc_ref[...], gpost_ref[...])


def _ffn(x, gpre, wu, wd, gpost, tm=512, tf=1024):
    S, D = x.shape
    Fd = wu.shape[1]
    return pl.pallas_call(
        _ffn_kernel,
        grid=(S // tm, Fd // tf),
        in_specs=[
            pl.BlockSpec((tm, D), lambda i, f: (i, 0)),
            pl.BlockSpec((1, D), lambda i, f: (0, 0)),
            pl.BlockSpec((D, tf), lambda i, f: (0, f)),
            pl.BlockSpec((tf, D), lambda i, f: (f, 0)),
            pl.BlockSpec((1, D), lambda i, f: (0, 0)),
        ],
        out_specs=pl.BlockSpec((tm, D), lambda i, f: (i, 0)),
        out_shape=jax.ShapeDtypeStruct((S, D), F32),
        scratch_shapes=[pltpu.VMEM((tm, D), BF16), pltpu.VMEM((tm, D), F32)],
        compiler_params=_cparams(("parallel", "arbitrary")),
        name="ffn",
    )(x, gpre, wu, wd, gpost)


def _layer(x, pos, norm_mix_pre, w_in, conv_w, conv_b, i_bias, f_bias, mlstm_norm,
           w_branch_m, w_branch_a, w_out, norm_mix_post, norm_ffn_pre, w_up, w_down,
           norm_ffn_post):
    S, D = x.shape
    o = 0
    pieces = {}
    for name, width in (("mq", M_WIDTH), ("mk", M_WIDTH), ("mv", M_WIDTH), ("mo", M_WIDTH),
                        ("mi", M_HEADS), ("mf", M_HEADS), ("aq", A_WIDTH), ("ak", A_WIDTH),
                        ("av", A_WIDTH), ("gm", D), ("ga", D)):
        pieces[name] = w_in[:, o:o + width]
        o += width
    w_all = jnp.concatenate([pieces[n] for n in ("gm", "ga", "mq", "mk", "aq", "ak", "mv", "mo", "av")],
                            axis=1).astype(BF16)
    w_gate = jnp.concatenate(
        [pieces["mi"], pieces["mf"], jnp.zeros((D, LANES - 2 * M_HEADS), w_in.dtype)], axis=1).astype(BF16)
    gbias = jnp.concatenate([i_bias, f_bias, jnp.zeros((LANES - 2 * M_HEADS,), F32)])[None, :]
    half = jnp.arange(0, A_HEAD_DIM, 2, dtype=F32) / A_HEAD_DIM
    inv_freq = 1.0 / (ROPE_THETA ** half)
    invf = jnp.concatenate([inv_freq, inv_freq])[None, :]

    P, gate = _proj(x, norm_mix_pre[None, :], w_all, w_gate)
    mq, mkT, aq, ak, avT, kmean, gc, gt = _prep(
        P, gate, pos.reshape(S, 1), invf, conv_w, conv_b[None, :], gbias)
    hm = _mlstm(mq, mkT, P, gc, gt, mlstm_norm[None, :])
    ha = _moba(aq, ak, avT, kmean.reshape(S // MOBA_BLOCK, A_WIDTH))
    x1 = _merge(hm, ha, P, x, w_branch_m.astype(BF16), w_branch_a.astype(BF16),
                w_out.astype(BF16), norm_mix_post[None, :])
    return _ffn(x1, norm_ffn_pre[None, :], w_up.astype(BF16), w_down.astype(BF16),
                norm_ffn_post[None, :])


def kernel(x, positions, norm_mix_pre, w_in, conv_w, conv_b, i_bias, f_bias, mlstm_norm,
           w_branch_m, w_branch_a, w_out, norm_mix_post, norm_ffn_pre, w_up, w_down,
           norm_ffn_post):
    B = x.shape[0]
    depth = w_in.shape[0]
    outs = []
    for b in range(B):
        xb = x[b]
        for l in range(depth):
            xb = _layer(xb, positions[b], norm_mix_pre[l], w_in[l], conv_w[l], conv_b[l],
                        i_bias[l], f_bias[l], mlstm_norm[l], w_branch_m[l], w_branch_a[l],
                        w_out[l], norm_mix_post[l], norm_ffn_pre[l], w_up[l], w_down[l],
                        norm_ffn_post[l])
        outs.append(xb)
    return outs[0][None] if B == 1 else jnp.stack(outs, axis=0)
```

```python
import functools

import jax
import jax.numpy as jnp
from jax import lax
from jax.experimental import pallas as pl
from jax.experimental.pallas import tpu as pltpu

F32 = jnp.float32
BF16 = jnp.bfloat16

M_HEADS = 4
M_HEAD_DIM = 256
M_WIDTH = M_HEADS * M_HEAD_DIM
M_CHUNK = 128
CONV_WIDTH = 4
A_HEADS = 8
A_HEAD_DIM = 128
A_WIDTH = A_HEADS * A_HEAD_DIM
MOBA_BLOCK = 256
MOBA_TOPK = 3
ROPE_THETA = 10000.0
NORM_EPS = 1e-6

LANES = 128
SUBLANES = 8
VMEM_LIMIT = 56 * 1024 * 1024
NEG_BIG = -1e30
LOG2_E = 1.4426950408889634

COL_GM = 0
COL_GA = 2048
COL_MQK = 4096
COL_AQK = 6144
COL_MV = 8192
COL_MO = 9216
COL_AV = 10240
P_COLS = 11264


def _cparams(sem):
    return pltpu.CompilerParams(dimension_semantics=sem, vmem_limit_bytes=VMEM_LIMIT)


def _rms(x, gain):
    ms = jnp.mean(x * x, axis=-1, keepdims=True)
    return x * lax.rsqrt(ms + NORM_EPS) * gain


def _sigmoid(x):
    return 1.0 / (1.0 + jnp.exp(-x))


def _split3(x):
    hi = x.astype(BF16)
    r1 = x - hi.astype(F32)
    mid = r1.astype(BF16)
    lo = (r1 - mid.astype(F32)).astype(BF16)
    return hi, mid, lo


def _proj_kernel(x_ref, g_ref, w_ref, wg_ref, p_ref, gate_ref, xn_ref):
    @pl.when(pl.program_id(1) == 0)
    def _():
        xn = _rms(x_ref[...], g_ref[...]).astype(BF16)
        xn_ref[...] = xn
        gate_ref[...] = jnp.dot(xn, wg_ref[...], preferred_element_type=F32)

    p_ref[...] = jnp.dot(xn_ref[...], w_ref[...], preferred_element_type=F32)


def _proj(x, gain, w_all, w_gate, tm=512, tn=1024):
    S, D = x.shape
    N = w_all.shape[1]
    return pl.pallas_call(
        _proj_kernel,
        grid=(S // tm, N // tn),
        in_specs=[
            pl.BlockSpec((tm, D), lambda i, j: (i, 0)),
            pl.BlockSpec((1, D), lambda i, j: (0, 0)),
            pl.BlockSpec((D, tn), lambda i, j: (0, j)),
            pl.BlockSpec((D, LANES), lambda i, j: (0, 0)),
        ],
        out_specs=[
            pl.BlockSpec((tm, tn), lambda i, j: (i, j)),
            pl.BlockSpec((tm, LANES), lambda i, j: (i, 0)),
        ],
        out_shape=[
            jax.ShapeDtypeStruct((S, N), F32),
            jax.ShapeDtypeStruct((S, LANES), F32),
        ],
        scratch_shapes=[pltpu.VMEM((tm, D), BF16)],
        compiler_params=_cparams(("parallel", "arbitrary")),
        name="proj",
    )(x, gain, w_all, w_gate)


PREP_ROWS = MOBA_BLOCK
PREP_COLS = 512


def _prep_kernel(pmk_ref, halo_ref, paqk_ref, pav_ref, gate_ref, pos_ref, invf_ref,
                 cw_ref, cb_ref, gb_ref,
                 mq_ref, mkT_ref, aq_ref, ak_ref, avT_ref, kmean_ref, gc_ref, gt_ref,
                 buf_ref):
    i = pl.program_id(0)
    R = PREP_ROWS

    halo = halo_ref[...]
    buf_ref[0:SUBLANES, :] = jnp.where(i == 0, jnp.zeros_like(halo), halo)
    buf_ref[SUBLANES:SUBLANES + R, :] = pmk_ref[...]
    k_scale = M_HEAD_DIM ** -0.5
    for c0 in range(0, 2 * M_WIDTH, PREP_COLS):
        cs = slice(c0, c0 + PREP_COLS)
        acc = cb_ref[:, cs]
        for j in range(CONV_WIDTH):
            off = SUBLANES - (CONV_WIDTH - 1) + j
            acc = acc + cw_ref[j:j + 1, cs] * buf_ref[off:off + R, cs]
        y = acc * _sigmoid(acc)
        if c0 < M_WIDTH:
            mq_ref[:, cs] = y.astype(BF16)
        else:
            ks = slice(c0 - M_WIDTH, c0 - M_WIDTH + PREP_COLS)
            mkT_ref[ks, :] = (y * k_scale).T.astype(BF16)

    ang = pos_ref[...].astype(F32) * invf_ref[...]
    cos = jnp.cos(ang)
    lane = lax.broadcasted_iota(jnp.int32, (R, A_HEAD_DIM), 1)
    sin_signed = jnp.where(lane < A_HEAD_DIM // 2, -1.0, 1.0) * jnp.sin(ang)
    for h in range(2 * A_HEADS):
        hs = slice(h * A_HEAD_DIM, (h + 1) * A_HEAD_DIM)
        xh = paqk_ref[:, hs]
        yh = xh * cos + pltpu.roll(xh, A_HEAD_DIM // 2, 1) * sin_signed
        if h < A_HEADS:
            aq_ref[:, hs] = yh
        else:
            ko = slice((h - A_HEADS) * A_HEAD_DIM, (h - A_HEADS + 1) * A_HEAD_DIM)
            ak_ref[:, ko] = yh.astype(BF16)
            kmean_ref[0, :, ko] = jnp.mean(yh, axis=0, keepdims=True)
    for c0 in range(0, A_WIDTH, PREP_COLS):
        avT_ref[c0:c0 + PREP_COLS, :] = pav_ref[:, c0:c0 + PREP_COLS].T.astype(BF16)

    g = gate_ref[...] + gb_ref[...]
    log_f = jnp.minimum(g, 0.0) - jnp.log1p(jnp.exp(-jnp.abs(g)))
    r_i = lax.broadcasted_iota(jnp.int32, (R, R), 0)
    c_i = lax.broadcasted_iota(jnp.int32, (R, R), 1)
    tri = ((r_i >= c_i) & ((r_i // M_CHUNK) == (c_i // M_CHUNK))).astype(BF16)
    hi, mid, lo = _split3(log_f)
    csum = (jnp.dot(tri, hi, preferred_element_type=F32)
            + jnp.dot(tri, mid, preferred_element_type=F32)
            + jnp.dot(tri, lo, preferred_element_type=F32))
    glane = lax.broadcasted_iota(jnp.int32, (R, LANES), 1)
    gc = jnp.where(glane < M_HEADS, g, csum)
    gc_ref[...] = gc
    gt_ref[...] = gc.T[0:SUBLANES, :]


def _prep(P, gate, pos, invf, conv_w, conv_b, gbias):
    S = P.shape[0]
    R = PREP_ROWS
    nb = S // R
    halo_blocks = R // SUBLANES
    return pl.pallas_call(
        _prep_kernel,
        grid=(nb,),
        in_specs=[
            pl.BlockSpec((R, 2 * M_WIDTH), lambda i: (i, COL_MQK // (2 * M_WIDTH))),
            pl.BlockSpec((SUBLANES, 2 * M_WIDTH),
                         lambda i: (jnp.maximum(i * halo_blocks - 1, 0), COL_MQK // (2 * M_WIDTH))),
            pl.BlockSpec((R, 2 * A_WIDTH), lambda i: (i, COL_AQK // (2 * A_WIDTH))),
            pl.BlockSpec((R, A_WIDTH), lambda i: (i, COL_AV // A_WIDTH)),
            pl.BlockSpec((R, LANES), lambda i: (i, 0)),
            pl.BlockSpec((R, 1), lambda i: (i, 0)),
            pl.BlockSpec((1, A_HEAD_DIM), lambda i: (0, 0)),
            pl.BlockSpec((CONV_WIDTH, 2 * M_WIDTH), lambda i: (0, 0)),
            pl.BlockSpec((1, 2 * M_WIDTH), lambda i: (0, 0)),
            pl.BlockSpec((1, LANES), lambda i: (0, 0)),
        ],
        out_specs=[
            pl.BlockSpec((R, M_WIDTH), lambda i: (i, 0)),
            pl.BlockSpec((M_WIDTH, R), lambda i: (0, i)),
            pl.BlockSpec((R, A_WIDTH), lambda i: (i, 0)),
            pl.BlockSpec((R, A_WIDTH), lambda i: (i, 0)),
            pl.BlockSpec((A_WIDTH, R), lambda i: (0, i)),
            pl.BlockSpec((1, 1, A_WIDTH), lambda i: (i, 0, 0)),
            pl.BlockSpec((R, LANES), lambda i: (i, 0)),
            pl.BlockSpec((SUBLANES, R), lambda i: (0, i)),
        ],
        out_shape=[
            jax.ShapeDtypeStruct((S, M_WIDTH), BF16),
            jax.ShapeDtypeStruct((M_WIDTH, S), BF16),
            jax.ShapeDtypeStruct((S, A_WIDTH), F32),
            jax.ShapeDtypeStruct((S, A_WIDTH), BF16),
            jax.ShapeDtypeStruct((A_WIDTH, S), BF16),
            jax.ShapeDtypeStruct((nb, 1, A_WIDTH), F32),
            jax.ShapeDtypeStruct((S, LANES), F32),
            jax.ShapeDtypeStruct((SUBLANES, S), F32),
        ],
        scratch_shapes=[pltpu.VMEM((SUBLANES + R, 2 * M_WIDTH), F32)],
        compiler_params=_cparams(("parallel",)),
        name="prep",
    )(P, P, P, P, gate, pos, invf, conv_w, conv_b, gbias)


M_AUG = M_HEAD_DIM + LANES


def _mlstm_kernel(q_ref, kT_ref, v_ref, mo_ref, gc_ref, gt_ref, gain_ref, out_ref,
                  c_ref, m_ref):
    @pl.when(pl.program_id(0) == 0)
    def _():
        c_ref[...] = jnp.zeros_like(c_ref)
        m_ref[...] = jnp.zeros_like(m_ref)

    L = M_CHUNK
    D = M_HEAD_DIM
    row = lax.broadcasted_iota(jnp.int32, (L, L), 0)
    col = lax.broadcasted_iota(jnp.int32, (L, L), 1)
    causal = row >= col
    ones_col = (lax.broadcasted_iota(jnp.int32, (L, LANES), 1) == 0).astype(BF16)

    for h in range(M_HEADS):
        hs = slice(h * D, (h + 1) * D)
        q = q_ref[:, hs]
        kT = kT_ref[hs, :]
        v_aug = jnp.concatenate([v_ref[:, hs].astype(BF16), ones_col], axis=1)
        b_c = gc_ref[:, M_HEADS + h:M_HEADS + h + 1]
        li_r = gt_ref[h:h + 1, :]
        b_r = gt_ref[M_HEADS + h:M_HEADS + h + 1, :]
        f_tot = b_r[:, L - 1:L]
        m_prev = m_ref[h:h + 1, 0:1]

        u_r = li_r - b_r
        d_log = jnp.where(causal, b_c + u_r, -jnp.inf)
        a_log = b_c + m_prev
        m_t = jnp.maximum(a_log, jnp.max(d_log, axis=1, keepdims=True))
        s_qk = jnp.dot(q, kT, preferred_element_type=F32)
        s_ts = s_qk * jnp.exp(d_log - m_t)
        inter = jnp.exp(a_log - m_t)
        c_old = c_ref[h]
        r = (inter * jnp.dot(q, c_old.astype(BF16), preferred_element_type=F32)
             + jnp.dot(s_ts.astype(BF16), v_aug, preferred_element_type=F32))
        num = r[:, :D]
        den = r[:, D:D + 1]
        hh = num / jnp.maximum(jnp.abs(den), jnp.exp(-m_t))
        hn = hh * lax.rsqrt(jnp.mean(hh * hh, axis=-1, keepdims=True) + NORM_EPS)
        out_ref[:, hs] = (hn * gain_ref[:, hs] * _sigmoid(mo_ref[:, hs])).astype(BF16)

        w_r = f_tot + u_r
        m_loc = jnp.max(w_r, axis=1, keepdims=True)
        m_new = jnp.maximum(f_tot + m_prev, m_loc)
        a = jnp.exp(f_tot + m_prev - m_new)
        e_r = jnp.exp(w_r - m_new)
        keT = (kT.astype(F32) * e_r).astype(BF16)
        c_ref[h] = a * c_old + jnp.dot(keT, v_aug, preferred_element_type=F32)
        m_ref[h:h + 1, :] = jnp.broadcast_to(m_new, (1, LANES))


def _mlstm(mq, mkT, P, gc, gt, gain):
    S = mq.shape[0]
    L = M_CHUNK
    return pl.pallas_call(
        _mlstm_kernel,
        grid=(S // L,),
        in_specs=[
            pl.BlockSpec((L, M_WIDTH), lambda c: (c, 0)),
            pl.BlockSpec((M_WIDTH, L), lambda c: (0, c)),
            pl.BlockSpec((L, M_WIDTH), lambda c: (c, COL_MV // M_WIDTH)),
            pl.BlockSpec((L, M_WIDTH), lambda c: (c, COL_MO // M_WIDTH)),
            pl.BlockSpec((L, LANES), lambda c: (c, 0)),
            pl.BlockSpec((SUBLANES, L), lambda c: (0, c)),
            pl.BlockSpec((1, M_WIDTH), lambda c: (0, 0)),
        ],
        out_specs=pl.BlockSpec((L, M_WIDTH), lambda c: (c, 0)),
        out_shape=jax.ShapeDtypeStruct((S, M_WIDTH), BF16),
        scratch_shapes=[
            pltpu.VMEM((M_HEADS, M_HEAD_DIM, M_AUG), F32),
            pltpu.VMEM((SUBLANES, LANES), F32),
        ],
        compiler_params=_cparams(("arbitrary",)),
        name="mlstm",
    )(mq, mkT, P, P, gc, gt, gain)


MOBA_HEADS_PER_STEP = 4


def _moba_kernel(q_ref, k_ref, vT_ref, km_ref, o_ref, sel_ref, qs_ref, m_ref, l_ref, acc_ref):
    i = pl.program_id(1)
    BS = MOBA_BLOCK
    Dh = A_HEAD_DIM
    NB = km_ref.shape[0]
    G = MOBA_HEADS_PER_STEP
    own = pl.multiple_of(i * BS, BS)
    blk = lax.broadcasted_iota(jnp.int32, (NB, BS), 0)
    kpos = lax.broadcasted_iota(jnp.int32, (BS, BS), 0)
    qpos = lax.broadcasted_iota(jnp.int32, (BS, BS), 1)

    def scores(g, start):
        hs = slice(g * Dh, (g + 1) * Dh)
        return jnp.dot(k_ref[pl.ds(start, BS), hs], qs_ref[g], preferred_element_type=F32)

    def softmax_part(s):
        m = jnp.max(s, axis=0, keepdims=True)
        p = jnp.exp2(s - m)
        return m, jnp.sum(p, axis=0, keepdims=True), p.astype(BF16)

    def weighted_values(g, start, p):
        hs = slice(g * Dh, (g + 1) * Dh)
        return jnp.dot(vT_ref[hs, pl.ds(start, BS)], p, preferred_element_type=F32)

    def pipelined(start, finish, causal):
        s = {0: scores(0, start)}
        pend = None
        for g in range(G):
            if g + 1 < G:
                s[g + 1] = scores(g + 1, start)
            sg = s.pop(g)
            if causal:
                sg = jnp.where(kpos <= qpos, sg, NEG_BIG)
            m, l, p = softmax_part(sg)
            o = weighted_values(g, start, p)
            if pend is not None:
                finish(*pend)
            pend = (g, m, l, o)
        finish(*pend)

    for g in range(G):
        hs = slice(g * Dh, (g + 1) * Dh)
        qT = q_ref[:, hs].T
        km = km_ref[:, hs]
        kh = km.astype(BF16)
        kl = (km - kh.astype(F32)).astype(BF16)
        qh = qT.astype(BF16)
        ql = (qT - qh.astype(F32)).astype(BF16)
        gate = (jnp.dot(kh, qh, preferred_element_type=F32)
                + jnp.dot(kh, ql, preferred_element_type=F32)
                + jnp.dot(kl, qh, preferred_element_type=F32))
        gate = jnp.where(blk < i, gate, -jnp.inf)
        sel = jnp.zeros((NB, BS), F32)
        for r in range(MOBA_TOPK):
            mx = jnp.max(gate, axis=0, keepdims=True)
            idx = jnp.min(jnp.where(gate == mx, blk, NB), axis=0, keepdims=True)
            idx = jnp.where(r < i, idx, -1)
            pick = blk == idx
            sel = jnp.where(pick, 1.0, sel)
            gate = jnp.where(pick, -jnp.inf, gate)
        sel_ref[g] = sel
        qs_ref[g] = (qT * (A_HEAD_DIM ** -0.5 * LOG2_E)).astype(BF16)

    def init_stats(g, m, l, o):
        m_ref[g] = m
        l_ref[g] = l
        acc_ref[g] = o

    pipelined(own, init_stats, True)

    def body(j, carry):
        start = pl.multiple_of(j * BS, BS)

        def combine(g, m, l, o):
            picked = sel_ref[g, pl.ds(j, 1), :] > 0.0
            m_eff = jnp.where(picked, m, NEG_BIG)
            m_old = m_ref[g]
            m_new = jnp.maximum(m_old, m_eff)
            a_old = jnp.exp2(m_old - m_new)
            a_blk = jnp.exp2(m_eff - m_new)
            l_ref[g] = a_old * l_ref[g] + a_blk * l
            acc_ref[g] = a_old * acc_ref[g] + a_blk * o
            m_ref[g] = m_new

        pipelined(start, combine, False)
        return carry

    lax.fori_loop(0, i, body, 0)

    for g in range(G):
        o_ref[:, g * Dh:(g + 1) * Dh] = (acc_ref[g] / l_ref[g]).T.astype(BF16)


def _moba(aq, ak, avT, kmean):
    S = aq.shape[0]
    BS = MOBA_BLOCK
    NB = S // BS
    G = MOBA_HEADS_PER_STEP
    W = G * A_HEAD_DIM
    once = pl.Buffered(1)
    return pl.pallas_call(
        _moba_kernel,
        grid=(A_HEADS // G, NB),
        in_specs=[
            pl.BlockSpec((BS, W), lambda h, i: (i, h)),
            pl.BlockSpec((S, W), lambda h, i: (0, h), pipeline_mode=once),
            pl.BlockSpec((W, S), lambda h, i: (h, 0), pipeline_mode=once),
            pl.BlockSpec((NB, W), lambda h, i: (0, h)),
        ],
        out_specs=pl.BlockSpec((BS, W), lambda h, i: (i, h)),
        out_shape=jax.ShapeDtypeStruct((S, A_WIDTH), BF16),
        scratch_shapes=[
            pltpu.VMEM((G, NB, BS), F32),
            pltpu.VMEM((G, A_HEAD_DIM, BS), BF16),
            pltpu.VMEM((G, 1, BS), F32),
            pltpu.VMEM((G, 1, BS), F32),
            pltpu.VMEM((G, A_HEAD_DIM, BS), F32),
        ],
        compiler_params=_cparams(("parallel", "arbitrary")),
        name="moba",
    )(aq, ak, avT, kmean)


def _merge_kernel(hm_ref, ha_ref, gm_ref, ga_ref, x_ref, wm_ref, wa_ref, wo_ref, gain_ref,
                  out_ref):
    ym = jnp.dot(hm_ref[...], wm_ref[...], preferred_element_type=F32)
    ya = jnp.dot(ha_ref[...], wa_ref[...], preferred_element_type=F32)
    merged = _sigmoid(gm_ref[...]) * ym + _sigmoid(ga_ref[...]) * ya
    mix = jnp.dot(merged.astype(BF16), wo_ref[...], preferred_element_type=F32)
    out_ref[...] = x_ref[...] + _rms(mix, gain_ref[...])


def _merge(hm, ha, P, x, wm, wa, wo, gain, tm=256):
    S, D = x.shape
    const = pl.Buffered(1)
    return pl.pallas_call(
        _merge_kernel,
        grid=(S // tm,),
        in_specs=[
            pl.BlockSpec((tm, M_WIDTH), lambda i: (i, 0)),
            pl.BlockSpec((tm, A_WIDTH), lambda i: (i, 0)),
            pl.BlockSpec((tm, D), lambda i: (i, COL_GM // D)),
            pl.BlockSpec((tm, D), lambda i: (i, COL_GA // D)),
            pl.BlockSpec((tm, D), lambda i: (i, 0)),
            pl.BlockSpec((M_WIDTH, D), lambda i: (0, 0), pipeline_mode=const),
            pl.BlockSpec((A_WIDTH, D), lambda i: (0, 0), pipeline_mode=const),
            pl.BlockSpec((D, D), lambda i: (0, 0), pipeline_mode=const),
            pl.BlockSpec((1, D), lambda i: (0, 0)),
        ],
        out_specs=pl.BlockSpec((tm, D), lambda i: (i, 0)),
        out_shape=jax.ShapeDtypeStruct((S, D), F32),
        compiler_params=_cparams(("parallel",)),
        name="merge",
    )(hm, ha, P, P, x, wm, wa, wo, gain)


def _ffn_kernel(x_ref, gpre_ref, wu_ref, wd_ref, gpost_ref, out_ref, hn_ref, acc_ref):
    f = pl.program_id(1)

    @pl.when(f == 0)
    def _():
        hn_ref[...] = _rms(x_ref[...], gpre_ref[...]).astype(BF16)
        acc_ref[...] = jnp.zeros_like(acc_ref)

    u = jnp.dot(hn_ref[...], wu_ref[...], preferred_element_type=F32)
    u = jnp.square(jnp.maximum(u, 0.0)).astype(BF16)
    acc_ref[...] += jnp.dot(u, wd_ref[...], preferred_element_type=F32)

    @pl.when(f == pl.num_programs(1) - 1)
    def _():
        out_ref[...] = x_ref[...] + _rms(acc_ref[...], gpost_ref[...])


def _ffn(x, gpre, wu, wd, gpost, tm=512, tf=1024):
    S, D = x.shape
    Fd = wu.shape[1]
    return pl.pallas_call(
        _ffn_kernel,
        grid=(S // tm, Fd // tf),
        in_specs=[
            pl.BlockSpec((tm, D), lambda i, f: (i, 0)),
            pl.BlockSpec((1, D), lambda i, f: (0, 0)),
            pl.BlockSpec((D, tf), lambda i, f: (0, f)),
            pl.BlockSpec((tf, D), lambda i, f: (f, 0)),
            pl.BlockSpec((1, D), lambda i, f: (0, 0)),
        ],
        out_specs=pl.BlockSpec((tm, D), lambda i, f: (i, 0)),
        out_shape=jax.ShapeDtypeStruct((S, D), F32),
        scratch_shapes=[pltpu.VMEM((tm, D), BF16), pltpu.VMEM((tm, D), F32)],
        compiler_params=_cparams(("parallel", "arbitrary")),
        name="ffn",
    )(x, gpre, wu, wd, gpost)


def _layer(x, pos, norm_mix_pre, w_in, conv_w, conv_b, i_bias, f_bias, mlstm_norm,
           w_branch_m, w_branch_a, w_out, norm_mix_post, norm_ffn_pre, w_up, w_down,
           norm_ffn_post):
    S, D = x.shape
    o = 0
    pieces = {}
    for name, width in (("mq", M_WIDTH), ("mk", M_WIDTH), ("mv", M_WIDTH), ("mo", M_WIDTH),
                        ("mi", M_HEADS), ("mf", M_HEADS), ("aq", A_WIDTH), ("ak", A_WIDTH),
                        ("av", A_WIDTH), ("gm", D), ("ga", D)):
        pieces[name] = w_in[:, o:o + width]
        o += width
    w_all = jnp.concatenate([pieces[n] for n in ("gm", "ga", "mq", "mk", "aq", "ak", "mv", "mo", "av")],
                            axis=1).astype(BF16)
    w_gate = jnp.concatenate(
        [pieces["mi"], pieces["mf"], jnp.zeros((D, LANES - 2 * M_HEADS), w_in.dtype)], axis=1).astype(BF16)
    gbias = jnp.concatenate([i_bias, f_bias, jnp.zeros((LANES - 2 * M_HEADS,), F32)])[None, :]
    half = jnp.arange(0, A_HEAD_DIM, 2, dtype=F32) / A_HEAD_DIM
    inv_freq = 1.0 / (ROPE_THETA ** half)
    invf = jnp.concatenate([inv_freq, inv_freq])[None, :]

    P, gate = _proj(x, norm_mix_pre[None, :], w_all, w_gate)
    mq, mkT, aq, ak, avT, kmean, gc, gt = _prep(
        P, gate, pos.reshape(S, 1), invf, conv_w, conv_b[None, :], gbias)
    hm = _mlstm(mq, mkT, P, gc, gt, mlstm_norm[None, :])
    ha = _moba(aq, ak, avT, kmean.reshape(S // MOBA_BLOCK, A_WIDTH))
    x1 = _merge(hm, ha, P, x, w_branch_m.astype(BF16), w_branch_a.astype(BF16),
                w_out.astype(BF16), norm_mix_post[None, :])
    return _ffn(x1, norm_ffn_pre[None, :], w_up.astype(BF16), w_down.astype(BF16),
                norm_ffn_post[None, :])


def kernel(x, positions, norm_mix_pre, w_in, conv_w, conv_b, i_bias, f_bias, mlstm_norm,
           w_branch_m, w_branch_a, w_out, norm_mix_post, norm_ffn_pre, w_up, w_down,
           norm_ffn_post):
    B = x.shape[0]
    depth = w_in.shape[0]
    outs = []
    for b in range(B):
        xb = x[b]
        for l in range(depth):
            xb = _layer(xb, positions[b], norm_mix_pre[l], w_in[l], conv_w[l], conv_b[l],
                        i_bias[l], f_bias[l], mlstm_norm[l], w_branch_m[l], w_branch_a[l],
                        w_out[l], norm_mix_post[l], norm_ffn_pre[l], w_up[l], w_down[l],
                        norm_ffn_post[l])
        outs.append(xb)
    return outs[0][None] if B == 1 else jnp.stack(outs, axis=0)
```

```python
import functools

import jax
import jax.numpy as jnp
from jax import lax
from jax.experimental import pallas as pl
from jax.experimental.pallas import tpu as pltpu

F32 = jnp.float32
BF16 = jnp.bfloat16

M_HEADS = 4
M_HEAD_DIM = 256
M_WIDTH = M_HEADS * M_HEAD_DIM
M_CHUNK = 128
CONV_WIDTH = 4
A_HEADS = 8
A_HEAD_DIM = 128
A_WIDTH = A_HEADS * A_HEAD_DIM
MOBA_BLOCK = 256
MOBA_TOPK = 3
ROPE_THETA = 10000.0
NORM_EPS = 1e-6

LANES = 128
SUBLANES = 8
VMEM_LIMIT = 56 * 1024 * 1024
NEG_BIG = -1e30
LOG2_E = 1.4426950408889634

COL_GM = 0
COL_GA = 2048
COL_MQK = 4096
COL_AQK = 6144
COL_MV = 8192
COL_MO = 9216
COL_AV = 10240
P_COLS = 11264


def _cparams(sem):
    return pltpu.CompilerParams(dimension_semantics=sem, vmem_limit_bytes=VMEM_LIMIT)


def _rms(x, gain):
    ms = jnp.mean(x * x, axis=-1, keepdims=True)
    return x * lax.rsqrt(ms + NORM_EPS) * gain


def _sigmoid(x):
    return 1.0 / (1.0 + jnp.exp(-x))


def _split3(x):
    hi = x.astype(BF16)
    r1 = x - hi.astype(F32)
    mid = r1.astype(BF16)
    lo = (r1 - mid.astype(F32)).astype(BF16)
    return hi, mid, lo


def _proj_kernel(x_ref, g_ref, w_ref, wg_ref, p_ref, gate_ref, xn_ref):
    @pl.when(pl.program_id(1) == 0)
    def _():
        xn = _rms(x_ref[...], g_ref[...]).astype(BF16)
        xn_ref[...] = xn
        gate_ref[...] = jnp.dot(xn, wg_ref[...], preferred_element_type=F32)

    p_ref[...] = jnp.dot(xn_ref[...], w_ref[...], preferred_element_type=F32)


def _proj(x, gain, w_all, w_gate, tm=512, tn=1024):
    S, D = x.shape
    N = w_all.shape[1]
    return pl.pallas_call(
        _proj_kernel,
        grid=(S // tm, N // tn),
        in_specs=[
            pl.BlockSpec((tm, D), lambda i, j: (i, 0)),
            pl.BlockSpec((1, D), lambda i, j: (0, 0)),
            pl.BlockSpec((D, tn), lambda i, j: (0, j)),
            pl.BlockSpec((D, LANES), lambda i, j: (0, 0)),
        ],
        out_specs=[
            pl.BlockSpec((tm, tn), lambda i, j: (i, j)),
            pl.BlockSpec((tm, LANES), lambda i, j: (i, 0)),
        ],
        out_shape=[
            jax.ShapeDtypeStruct((S, N), F32),
            jax.ShapeDtypeStruct((S, LANES), F32),
        ],
        scratch_shapes=[pltpu.VMEM((tm, D), BF16)],
        compiler_params=_cparams(("parallel", "arbitrary")),
        name="proj",
    )(x, gain, w_all, w_gate)


PREP_ROWS = MOBA_BLOCK
PREP_COLS = 512


def _prep_kernel(pmk_ref, halo_ref, paqk_ref, pav_ref, gate_ref, pos_ref, invf_ref,
                 cw_ref, cb_ref, gb_ref,
                 mq_ref, mkT_ref, aq_ref, ak_ref, avT_ref, kmean_ref, gc_ref, gt_ref,
                 buf_ref):
    i = pl.program_id(0)
    R = PREP_ROWS

    halo = halo_ref[...]
    buf_ref[0:SUBLANES, :] = jnp.where(i == 0, jnp.zeros_like(halo), halo)
    buf_ref[SUBLANES:SUBLANES + R, :] = pmk_ref[...]
    k_scale = M_HEAD_DIM ** -0.5
    for c0 in range(0, 2 * M_WIDTH, PREP_COLS):
        cs = slice(c0, c0 + PREP_COLS)
        acc = cb_ref[:, cs]
        for j in range(CONV_WIDTH):
            off = SUBLANES - (CONV_WIDTH - 1) + j
            acc = acc + cw_ref[j:j + 1, cs] * buf_ref[off:off + R, cs]
        y = acc * _sigmoid(acc)
        if c0 < M_WIDTH:
            mq_ref[:, cs] = y.astype(BF16)
        else:
            ks = slice(c0 - M_WIDTH, c0 - M_WIDTH + PREP_COLS)
            mkT_ref[ks, :] = (y * k_scale).T.astype(BF16)

    ang = pos_ref[...].astype(F32) * invf_ref[...]
    cos = jnp.cos(ang)
    lane = lax.broadcasted_iota(jnp.int32, (R, A_HEAD_DIM), 1)
    sin_signed = jnp.where(lane < A_HEAD_DIM // 2, -1.0, 1.0) * jnp.sin(ang)
    for h in range(2 * A_HEADS):
        hs = slice(h * A_HEAD_DIM, (h + 1) * A_HEAD_DIM)
        xh = paqk_ref[:, hs]
        yh = xh * cos + pltpu.roll(xh, A_HEAD_DIM // 2, 1) * sin_signed
        if h < A_HEADS:
            aq_ref[:, hs] = yh
        else:
            ko = slice((h - A_HEADS) * A_HEAD_DIM, (h - A_HEADS + 1) * A_HEAD_DIM)
            ak_ref[:, ko] = yh.astype(BF16)
            kmean_ref[0, :, ko] = jnp.mean(yh, axis=0, keepdims=True)
    for c0 in range(0, A_WIDTH, PREP_COLS):
        avT_ref[c0:c0 + PREP_COLS, :] = pav_ref[:, c0:c0 + PREP_COLS].T.astype(BF16)

    g = gate_ref[...] + gb_ref[...]
    log_f = jnp.minimum(g, 0.0) - jnp.log1p(jnp.exp(-jnp.abs(g)))
    r_i = lax.broadcasted_iota(jnp.int32, (R, R), 0)
    c_i = lax.broadcasted_iota(jnp.int32, (R, R), 1)
    tri = ((r_i >= c_i) & ((r_i // M_CHUNK) == (c_i // M_CHUNK))).astype(BF16)
    hi, mid, lo = _split3(log_f)
    csum = (jnp.dot(tri, hi, preferred_element_type=F32)
            + jnp.dot(tri, mid, preferred_element_type=F32)
            + jnp.dot(tri, lo, preferred_element_type=F32))
    glane = lax.broadcasted_iota(jnp.int32, (R, LANES), 1)
    gc = jnp.where(glane < M_HEADS, g, csum)
    gc_ref[...] = gc
    gt_ref[...] = gc.T[0:SUBLANES, :]


def _prep(P, gate, pos, invf, conv_w, conv_b, gbias):
    S = P.shape[0]
    R = PREP_ROWS
    nb = S // R
    halo_blocks = R // SUBLANES
    return pl.pallas_call(
        _prep_kernel,
        grid=(nb,),
        in_specs=[
            pl.BlockSpec((R, 2 * M_WIDTH), lambda i: (i, COL_MQK // (2 * M_WIDTH))),
            pl.BlockSpec((SUBLANES, 2 * M_WIDTH),
                         lambda i: (jnp.maximum(i * halo_blocks - 1, 0), COL_MQK // (2 * M_WIDTH))),
            pl.BlockSpec((R, 2 * A_WIDTH), lambda i: (i, COL_AQK // (2 * A_WIDTH))),
            pl.BlockSpec((R, A_WIDTH), lambda i: (i, COL_AV // A_WIDTH)),
            pl.BlockSpec((R, LANES), lambda i: (i, 0)),
            pl.BlockSpec((R, 1), lambda i: (i, 0)),
            pl.BlockSpec((1, A_HEAD_DIM), lambda i: (0, 0)),
            pl.BlockSpec((CONV_WIDTH, 2 * M_WIDTH), lambda i: (0, 0)),
            pl.BlockSpec((1, 2 * M_WIDTH), lambda i: (0, 0)),
            pl.BlockSpec((1, LANES), lambda i: (0, 0)),
        ],
        out_specs=[
            pl.BlockSpec((R, M_WIDTH), lambda i: (i, 0)),
            pl.BlockSpec((M_WIDTH, R), lambda i: (0, i)),
            pl.BlockSpec((R, A_WIDTH), lambda i: (i, 0)),
            pl.BlockSpec((R, A_WIDTH), lambda i: (i, 0)),
            pl.BlockSpec((A_WIDTH, R), lambda i: (0, i)),
            pl.BlockSpec((1, 1, A_WIDTH), lambda i: (i, 0, 0)),
            pl.BlockSpec((R, LANES), lambda i: (i, 0)),
            pl.BlockSpec((SUBLANES, R), lambda i: (0, i)),
        ],
        out_shape=[
            jax.ShapeDtypeStruct((S, M_WIDTH), BF16),
            jax.ShapeDtypeStruct((M_WIDTH, S), BF16),
            jax.ShapeDtypeStruct((S, A_WIDTH), F32),
            jax.ShapeDtypeStruct((S, A_WIDTH), BF16),
            jax.ShapeDtypeStruct((A_WIDTH, S), BF16),
            jax.ShapeDtypeStruct((nb, 1, A_WIDTH), F32),
            jax.ShapeDtypeStruct((S, LANES), F32),
            jax.ShapeDtypeStruct((SUBLANES, S), F32),
        ],
        scratch_shapes=[pltpu.VMEM((SUBLANES + R, 2 * M_WIDTH), F32)],
        compiler_params=_cparams(("parallel",)),
        name="prep",
    )(P, P, P, P, gate, pos, invf, conv_w, conv_b, gbias)


M_AUG = M_HEAD_DIM + LANES


def _mlstm_kernel(q_ref, kT_ref, v_ref, mo_ref, gc_ref, gt_ref, gain_ref, out_ref,
                  c_ref, m_ref):
    @pl.when(pl.program_id(0) == 0)
    def _():
        c_ref[...] = jnp.zeros_like(c_ref)
        m_ref[...] = jnp.zeros_like(m_ref)

    L = M_CHUNK
    D = M_HEAD_DIM
    row = lax.broadcasted_iota(jnp.int32, (L, L), 0)
    col = lax.broadcasted_iota(jnp.int32, (L, L), 1)
    causal = row >= col
    ones_col = (lax.broadcasted_iota(jnp.int32, (L, LANES), 1) == 0).astype(BF16)

    for h in range(M_HEADS):
        hs = slice(h * D, (h + 1) * D)
        q = q_ref[:, hs]
        kT = kT_ref[hs, :]
        v_aug = jnp.concatenate([v_ref[:, hs].astype(BF16), ones_col], axis=1)
        b_c = gc_ref[:, M_HEADS + h:M_HEADS + h + 1]
        li_r = gt_ref[h:h + 1, :]
        b_r = gt_ref[M_HEADS + h:M_HEADS + h + 1, :]
        f_tot = b_r[:, L - 1:L]
        m_prev = m_ref[h:h + 1, 0:1]

        u_r = li_r - b_r
        d_log = jnp.where(causal, b_c + u_r, -jnp.inf)
        a_log = b_c + m_prev
        m_t = jnp.maximum(a_log, jnp.max(d_log, axis=1, keepdims=True))
        s_qk = jnp.dot(q, kT, preferred_element_type=F32)
        s_ts = s_qk * jnp.exp(d_log - m_t)
        inter = jnp.exp(a_log - m_t)
        c_old = c_ref[h]
        r = (inter * jnp.dot(q, c_old.astype(BF16), preferred_element_type=F32)
             + jnp.dot(s_ts.astype(BF16), v_aug, preferred_element_type=F32))
        num = r[:, :D]
        den = r[:, D:D + 1]
        hh = num / jnp.maximum(jnp.abs(den), jnp.exp(-m_t))
        hn = hh * lax.rsqrt(jnp.mean(hh * hh, axis=-1, keepdims=True) + NORM_EPS)
        out_ref[:, hs] = (hn * gain_ref[:, hs] * _sigmoid(mo_ref[:, hs])).astype(BF16)

        w_r = f_tot + u_r
        m_loc = jnp.max(w_r, axis=1, keepdims=True)
        m_new = jnp.maximum(f_tot + m_prev, m_loc)
        a = jnp.exp(f_tot + m_prev - m_new)
        e_r = jnp.exp(w_r - m_new)
        keT = (kT.astype(F32) * e_r).astype(BF16)
        c_ref[h] = a * c_old + jnp.dot(keT, v_aug, preferred_element_type=F32)
        m_ref[h:h + 1, :] = jnp.broadcast_to(m_new, (1, LANES))


def _mlstm(mq, mkT, P, gc, gt, gain):
    S = mq.shape[0]
    L = M_CHUNK
    return pl.pallas_call(
        _mlstm_kernel,
        grid=(S // L,),
        in_specs=[
            pl.BlockSpec((L, M_WIDTH), lambda c: (c, 0)),
            pl.BlockSpec((M_WIDTH, L), lambda c: (0, c)),
            pl.BlockSpec((L, M_WIDTH), lambda c: (c, COL_MV // M_WIDTH)),
            pl.BlockSpec((L, M_WIDTH), lambda c: (c, COL_MO // M_WIDTH)),
            pl.BlockSpec((L, LANES), lambda c: (c, 0)),
            pl.BlockSpec((SUBLANES, L), lambda c: (0, c)),
            pl.BlockSpec((1, M_WIDTH), lambda c: (0, 0)),
        ],
        out_specs=pl.BlockSpec((L, M_WIDTH), lambda c: (c, 0)),
        out_shape=jax.ShapeDtypeStruct((S, M_WIDTH), BF16),
        scratch_shapes=[
            pltpu.VMEM((M_HEADS, M_HEAD_DIM, M_AUG), F32),
            pltpu.VMEM((SUBLANES, LANES), F32),
        ],
        compiler_params=_cparams(("arbitrary",)),
        name="mlstm",
    )(mq, mkT, P, P, gc, gt, gain)


MOBA_HEADS_PER_STEP = 4


def _moba_kernel(q_ref, k_ref, vT_ref, km_ref, o_ref, sel_ref, qs_ref, m_ref, l_ref, acc_ref,
                 s0_ref, s1_ref, mb0_ref, mb1_ref):
    i = pl.program_id(1)
    BS = MOBA_BLOCK
    Dh = A_HEAD_DIM
    NB = km_ref.shape[0]
    G = MOBA_HEADS_PER_STEP
    own = pl.multiple_of(i * BS, BS)
    blk = lax.broadcasted_iota(jnp.int32, (NB, BS), 0)
    kpos = lax.broadcasted_iota(jnp.int32, (BS, BS), 0)
    qpos = lax.broadcasted_iota(jnp.int32, (BS, BS), 1)

    ones_rows = jnp.ones((2 * SUBLANES, BS), BF16)

    def stage_scores(start, s_buf, mb_buf, causal):
        for g in range(G):
            hs = slice(g * Dh, (g + 1) * Dh)
            s = jnp.dot(k_ref[pl.ds(start, BS), hs], qs_ref[g], preferred_element_type=F32)
            if causal:
                s = jnp.where(kpos <= qpos, s, NEG_BIG)
            sb = s.astype(BF16)
            s_buf[g] = sb
            mb_buf[g] = jnp.max(sb, axis=0, keepdims=True).astype(F32)

    def stage_apply(start, picked_of, s_buf, mb_buf):
        for g in range(G):
            hs = slice(g * Dh, (g + 1) * Dh)
            picked = picked_of(g)
            m_old = m_ref[g]
            m_new = jnp.maximum(m_old, jnp.where(picked, mb_buf[g], NEG_BIG))
            a_old = jnp.exp2(m_old - m_new)
            shift = jnp.where(picked, m_new, -NEG_BIG)
            p = jnp.exp2(s_buf[g] - shift.astype(BF16))
            v_aug = jnp.concatenate([vT_ref[hs, pl.ds(start, BS)], ones_rows], axis=0)
            r = jnp.dot(v_aug, p, preferred_element_type=F32)
            l_ref[g] = a_old * l_ref[g] + r[Dh:Dh + 1, :]
            acc_ref[g] = a_old * acc_ref[g] + r[:Dh, :]
            m_ref[g] = m_new

    for g in range(G):
        hs = slice(g * Dh, (g + 1) * Dh)
        qT = q_ref[:, hs].T
        km = km_ref[:, hs]
        kh = km.astype(BF16)
        kl = (km - kh.astype(F32)).astype(BF16)
        qh = qT.astype(BF16)
        ql = (qT - qh.astype(F32)).astype(BF16)
        gate = (jnp.dot(kh, qh, preferred_element_type=F32)
                + jnp.dot(kh, ql, preferred_element_type=F32)
                + jnp.dot(kl, qh, preferred_element_type=F32))
        gate = jnp.where(blk < i, gate, -jnp.inf)
        sel = jnp.zeros((NB, BS), F32)
        for r in range(MOBA_TOPK):
            mx = jnp.max(gate, axis=0, keepdims=True)
            idx = jnp.min(jnp.where(gate == mx, blk, NB), axis=0, keepdims=True)
            idx = jnp.where(r < i, idx, -1)
            pick = blk == idx
            sel = jnp.where(pick, 1.0, sel)
            gate = jnp.where(pick, -jnp.inf, gate)
        sel_ref[g] = sel
        qs_ref[g] = (qT * (A_HEAD_DIM ** -0.5 * LOG2_E)).astype(BF16)

    m_ref[...] = jnp.full(m_ref.shape, NEG_BIG, F32)
    l_ref[...] = jnp.zeros(l_ref.shape, F32)
    acc_ref[...] = jnp.zeros(acc_ref.shape, F32)

    def block_start(j):
        return pl.multiple_of(jnp.minimum(j, NB - 1) * BS, BS)

    def picked_past(j):
        return lambda g: sel_ref[g, pl.ds(j, 1), :] > 0.0

    stage_scores(own, s1_ref, mb1_ref, True)
    stage_scores(block_start(0), s0_ref, mb0_ref, False)
    stage_apply(own, lambda g: jnp.full((1, BS), True), s1_ref, mb1_ref)

    def body(t, carry):
        j = 2 * t
        stage_scores(block_start(j + 1), s1_ref, mb1_ref, False)
        stage_apply(block_start(j), picked_past(j), s0_ref, mb0_ref)
        stage_scores(block_start(j + 2), s0_ref, mb0_ref, False)
        stage_apply(block_start(j + 1), picked_past(j + 1), s1_ref, mb1_ref)
        return carry

    lax.fori_loop(0, (i + 1) // 2, body, 0)

    for g in range(G):
        o_ref[:, g * Dh:(g + 1) * Dh] = (acc_ref[g] / l_ref[g]).T.astype(BF16)


def _moba(aq, ak, avT, kmean):
    S = aq.shape[0]
    BS = MOBA_BLOCK
    NB = S // BS
    G = MOBA_HEADS_PER_STEP
    W = G * A_HEAD_DIM
    once = pl.Buffered(1)
    return pl.pallas_call(
        _moba_kernel,
        grid=(A_HEADS // G, NB),
        in_specs=[
            pl.BlockSpec((BS, W), lambda h, i: (i, h)),
            pl.BlockSpec((S, W), lambda h, i: (0, h), pipeline_mode=once),
            pl.BlockSpec((W, S), lambda h, i: (h, 0), pipeline_mode=once),
            pl.BlockSpec((NB, W), lambda h, i: (0, h)),
        ],
        out_specs=pl.BlockSpec((BS, W), lambda h, i: (i, h)),
        out_shape=jax.ShapeDtypeStruct((S, A_WIDTH), BF16),
        scratch_shapes=[
            pltpu.VMEM((G, NB, BS), F32),
            pltpu.VMEM((G, A_HEAD_DIM, BS), BF16),
            pltpu.VMEM((G, 1, BS), F32),
            pltpu.VMEM((G, 1, BS), F32),
            pltpu.VMEM((G, A_HEAD_DIM, BS), F32),
            pltpu.VMEM((G, BS, BS), BF16),
            pltpu.VMEM((G, BS, BS), BF16),
            pltpu.VMEM((G, 1, BS), F32),
            pltpu.VMEM((G, 1, BS), F32),
        ],
        compiler_params=_cparams(("parallel", "arbitrary")),
        name="moba",
    )(aq, ak, avT, kmean)


def _merge_kernel(hm_ref, ha_ref, gm_ref, ga_ref, x_ref, wm_ref, wa_ref, wo_ref, gain_ref,
                  out_ref):
    ym = jnp.dot(hm_ref[...], wm_ref[...], preferred_element_type=F32)
    ya = jnp.dot(ha_ref[...], wa_ref[...], preferred_element_type=F32)
    merged = _sigmoid(gm_ref[...]) * ym + _sigmoid(ga_ref[...]) * ya
    mix = jnp.dot(merged.astype(BF16), wo_ref[...], preferred_element_type=F32)
    out_ref[...] = x_ref[...] + _rms(mix, gain_ref[...])


def _merge(hm, ha, P, x, wm, wa, wo, gain, tm=256):
    S, D = x.shape
    const = pl.Buffered(1)
    return pl.pallas_call(
        _merge_kernel,
        grid=(S // tm,),
        in_specs=[
            pl.BlockSpec((tm, M_WIDTH), lambda i: (i, 0)),
            pl.BlockSpec((tm, A_WIDTH), lambda i: (i, 0)),
            pl.BlockSpec((tm, D), lambda i: (i, COL_GM // D)),
            pl.BlockSpec((tm, D), lambda i: (i, COL_GA // D)),
            pl.BlockSpec((tm, D), lambda i: (i, 0)),
            pl.BlockSpec((M_WIDTH, D), lambda i: (0, 0), pipeline_mode=const),
            pl.BlockSpec((A_WIDTH, D), lambda i: (0, 0), pipeline_mode=const),
            pl.BlockSpec((D, D), lambda i: (0, 0), pipeline_mode=const),
            pl.BlockSpec((1, D), lambda i: (0, 0)),
        ],
        out_specs=pl.BlockSpec((tm, D), lambda i: (i, 0)),
        out_shape=jax.ShapeDtypeStruct((S, D), F32),
        compiler_params=_cparams(("parallel",)),
        name="merge",
    )(hm, ha, P, P, x, wm, wa, wo, gain)


def _ffn_kernel(x_ref, gpre_ref, wu_ref, wd_ref, gpost_ref, out_ref, hn_ref, acc_ref):
    f = pl.program_id(1)

    @pl.when(f == 0)
    def _():
        hn_ref[...] = _rms(x_ref[...], gpre_ref[...]).astype(BF16)
        acc_ref[...] = jnp.zeros_like(acc_ref)

    u = jnp.dot(hn_ref[...], wu_ref[...], preferred_element_type=F32)
    u = jnp.square(jnp.maximum(u, 0.0)).astype(BF16)
    acc_ref[...] += jnp.dot(u, wd_ref[...], preferred_element_type=F32)

    @pl.when(f == pl.num_programs(1) - 1)
    def _():
        out_ref[...] = x_ref[...] + _rms(acc_ref[...], gpost_ref[...])


def _ffn(x, gpre, wu, wd, gpost, tm=512, tf=1024):
    S, D = x.shape
    Fd = wu.shape[1]
    return pl.pallas_call(
        _ffn_kernel,
        grid=(S // tm, Fd // tf),
        in_specs=[
            pl.BlockSpec((tm, D), lambda i, f: (i, 0)),
            pl.BlockSpec((1, D), lambda i, f: (0, 0)),
            pl.BlockSpec((D, tf), lambda i, f: (0, f)),
            pl.BlockSpec((tf, D), lambda i, f: (f, 0)),
            pl.BlockSpec((1, D), lambda i, f: (0, 0)),
        ],
        out_specs=pl.BlockSpec((tm, D), lambda i, f: (i, 0)),
        out_shape=jax.ShapeDtypeStruct((S, D), F32),
        scratch_shapes=[pltpu.VMEM((tm, D), BF16), pltpu.VMEM((tm, D), F32)],
        compiler_params=_cparams(("parallel", "arbitrary")),
        name="ffn",
    )(x, gpre, wu, wd, gpost)


def _layer(x, pos, norm_mix_pre, w_in, conv_w, conv_b, i_bias, f_bias, mlstm_norm,
           w_branch_m, w_branch_a, w_out, norm_mix_post, norm_ffn_pre, w_up, w_down,
           norm_ffn_post):
    S, D = x.shape
    o = 0
    pieces = {}
    for name, width in (("mq", M_WIDTH), ("mk", M_WIDTH), ("mv", M_WIDTH), ("mo", M_WIDTH),
                        ("mi", M_HEADS), ("mf", M_HEADS), ("aq", A_WIDTH), ("ak", A_WIDTH),
                        ("av", A_WIDTH), ("gm", D), ("ga", D)):
        pieces[name] = w_in[:, o:o + width]
        o += width
    w_all = jnp.concatenate([pieces[n] for n in ("gm", "ga", "mq", "mk", "aq", "ak", "mv", "mo", "av")],
                            axis=1).astype(BF16)
    w_gate = jnp.concatenate(
        [pieces["mi"], pieces["mf"], jnp.zeros((D, LANES - 2 * M_HEADS), w_in.dtype)], axis=1).astype(BF16)
    gbias = jnp.concatenate([i_bias, f_bias, jnp.zeros((LANES - 2 * M_HEADS,), F32)])[None, :]
    half = jnp.arange(0, A_HEAD_DIM, 2, dtype=F32) / A_HEAD_DIM
    inv_freq = 1.0 / (ROPE_THETA ** half)
    invf = jnp.concatenate([inv_freq, inv_freq])[None, :]

    P, gate = _proj(x, norm_mix_pre[None, :], w_all, w_gate)
    mq, mkT, aq, ak, avT, kmean, gc, gt = _prep(
        P, gate, pos.reshape(S, 1), invf, conv_w, conv_b[None, :], gbias)
    hm = _mlstm(mq, mkT, P, gc, gt, mlstm_norm[None, :])
    ha = _moba(aq, ak, avT, kmean.reshape(S // MOBA_BLOCK, A_WIDTH))
    x1 = _merge(hm, ha, P, x, w_branch_m.astype(BF16), w_branch_a.astype(BF16),
                w_out.astype(BF16), norm_mix_post[None, :])
    return _ffn(x1, norm_ffn_pre[None, :], w_up.astype(BF16), w_down.astype(BF16),
                norm_ffn_post[None, :])


def kernel(x, positions, norm_mix_pre, w_in, conv_w, conv_b, i_bias, f_bias, mlstm_norm,
           w_branch_m, w_branch_a, w_out, norm_mix_post, norm_ffn_pre, w_up, w_down,
           norm_ffn_post):
    B = x.shape[0]
    depth = w_in.shape[0]
    outs = []
    for b in range(B):
        xb = x[b]
        for l in range(depth):
            xb = _layer(xb, positions[b], norm_mix_pre[l], w_in[l], conv_w[l], conv_b[l],
                        i_bias[l], f_bias[l], mlstm_norm[l], w_branch_m[l], w_branch_a[l],
                        w_out[l], norm_mix_post[l], norm_ffn_pre[l], w_up[l], w_down[l],
                        norm_ffn_post[l])
        outs.append(xb)
    return outs[0][None] if B == 1 else jnp.stack(outs, axis=0)
```

```python
import functools

import jax
import jax.numpy as jnp
from jax import lax
from jax.experimental import pallas as pl
from jax.experimental.pallas import tpu as pltpu

F32 = jnp.float32
BF16 = jnp.bfloat16

M_HEADS = 4
M_HEAD_DIM = 256
M_WIDTH = M_HEADS * M_HEAD_DIM
M_CHUNK = 128
CONV_WIDTH = 4
A_HEADS = 8
A_HEAD_DIM = 128
A_WIDTH = A_HEADS * A_HEAD_DIM
MOBA_BLOCK = 256
MOBA_TOPK = 3
ROPE_THETA = 10000.0
NORM_EPS = 1e-6

LANES = 128
SUBLANES = 8
VMEM_LIMIT = 56 * 1024 * 1024
NEG_BIG = -1e30
LOG2_E = 1.4426950408889634

COL_GM = 0
COL_GA = 2048
COL_MQK = 4096
COL_AQK = 6144
COL_MV = 8192
COL_MO = 9216
COL_AV = 10240
P_COLS = 11264


def _cparams(sem):
    return pltpu.CompilerParams(dimension_semantics=sem, vmem_limit_bytes=VMEM_LIMIT)


def _rms(x, gain):
    ms = jnp.mean(x * x, axis=-1, keepdims=True)
    return x * lax.rsqrt(ms + NORM_EPS) * gain


def _sigmoid(x):
    return 1.0 / (1.0 + jnp.exp(-x))


def _split3(x):
    hi = x.astype(BF16)
    r1 = x - hi.astype(F32)
    mid = r1.astype(BF16)
    lo = (r1 - mid.astype(F32)).astype(BF16)
    return hi, mid, lo


def _proj_kernel(x_ref, g_ref, w_ref, wg_ref, p_ref, gate_ref, xn_ref):
    @pl.when(pl.program_id(1) == 0)
    def _():
        xn = _rms(x_ref[...], g_ref[...]).astype(BF16)
        xn_ref[...] = xn
        gate_ref[...] = jnp.dot(xn, wg_ref[...], preferred_element_type=F32)

    p_ref[...] = jnp.dot(xn_ref[...], w_ref[...], preferred_element_type=F32).astype(BF16)


def _proj(x, gain, w_all, w_gate, tm=1024, tn=1024):
    S, D = x.shape
    N = w_all.shape[1]
    return pl.pallas_call(
        _proj_kernel,
        grid=(S // tm, N // tn),
        in_specs=[
            pl.BlockSpec((tm, D), lambda i, j: (i, 0)),
            pl.BlockSpec((1, D), lambda i, j: (0, 0)),
            pl.BlockSpec((D, tn), lambda i, j: (0, j)),
            pl.BlockSpec((D, LANES), lambda i, j: (0, 0)),
        ],
        out_specs=[
            pl.BlockSpec((tm, tn), lambda i, j: (i, j)),
            pl.BlockSpec((tm, LANES), lambda i, j: (i, 0)),
        ],
        out_shape=[
            jax.ShapeDtypeStruct((S, N), BF16),
            jax.ShapeDtypeStruct((S, LANES), F32),
        ],
        scratch_shapes=[pltpu.VMEM((tm, D), BF16)],
        compiler_params=_cparams(("parallel", "arbitrary")),
        name="proj",
    )(x, gain, w_all, w_gate)


PREP_ROWS = MOBA_BLOCK
PREP_COLS = 512
HALO_ROWS = 2 * SUBLANES


def _prep_kernel(pmk_ref, halo_ref, paqk_ref, pav_ref, gate_ref, pos_ref, invf_ref,
                 cw_ref, cb_ref, gb_ref,
                 mq_ref, mkT_ref, aq_ref, ak_ref, avT_ref, kmean_ref, gc_ref, gt_ref):
    i = pl.program_id(0)
    R = PREP_ROWS

    k_scale = M_HEAD_DIM ** -0.5
    for c0 in range(0, 2 * M_WIDTH, PREP_COLS):
        cs = slice(c0, c0 + PREP_COLS)
        prev = halo_ref[:, cs].astype(F32)[HALO_ROWS - SUBLANES:, :]
        prev = jnp.where(i == 0, jnp.zeros_like(prev), prev)
        ext = jnp.concatenate([pmk_ref[:, cs].astype(F32), prev], axis=0)
        acc = cw_ref[0:1, cs] * ext
        for j in range(1, CONV_WIDTH):
            acc = pltpu.roll(acc, 1, 0) + cw_ref[j:j + 1, cs] * ext
        acc = acc[0:R, :] + cb_ref[:, cs]
        y = acc * _sigmoid(acc)
        if c0 < M_WIDTH:
            mq_ref[:, cs] = y.astype(BF16)
        else:
            ks = slice(c0 - M_WIDTH, c0 - M_WIDTH + PREP_COLS)
            mkT_ref[ks, :] = (y * k_scale).T.astype(BF16)

    ang = pos_ref[...].astype(F32) * invf_ref[...]
    cos = jnp.cos(ang)
    lane = lax.broadcasted_iota(jnp.int32, (R, A_HEAD_DIM), 1)
    sin_signed = jnp.where(lane < A_HEAD_DIM // 2, -1.0, 1.0) * jnp.sin(ang)
    for h in range(2 * A_HEADS):
        hs = slice(h * A_HEAD_DIM, (h + 1) * A_HEAD_DIM)
        xh = paqk_ref[:, hs].astype(F32)
        yh = xh * cos + pltpu.roll(xh, A_HEAD_DIM // 2, 1) * sin_signed
        if h < A_HEADS:
            aq_ref[:, hs] = yh
        else:
            ko = slice((h - A_HEADS) * A_HEAD_DIM, (h - A_HEADS + 1) * A_HEAD_DIM)
            ak_ref[:, ko] = yh.astype(BF16)
            kmean_ref[0, :, ko] = jnp.mean(yh, axis=0, keepdims=True)
    for c0 in range(0, A_WIDTH, PREP_COLS):
        avT_ref[c0:c0 + PREP_COLS, :] = pav_ref[:, c0:c0 + PREP_COLS].astype(F32).T.astype(BF16)

    g = gate_ref[...] + gb_ref[...]
    log_f = jnp.minimum(g, 0.0) - jnp.log1p(jnp.exp(-jnp.abs(g)))
    r_i = lax.broadcasted_iota(jnp.int32, (R, R), 0)
    c_i = lax.broadcasted_iota(jnp.int32, (R, R), 1)
    tri = ((r_i >= c_i) & ((r_i // M_CHUNK) == (c_i // M_CHUNK))).astype(BF16)
    hi, mid, lo = _split3(log_f)
    csum = (jnp.dot(tri, hi, preferred_element_type=F32)
            + jnp.dot(tri, mid, preferred_element_type=F32)
            + jnp.dot(tri, lo, preferred_element_type=F32))
    glane = lax.broadcasted_iota(jnp.int32, (R, LANES), 1)
    gc = jnp.where(glane < M_HEADS, g, csum)
    gc_ref[...] = gc
    gt_ref[...] = gc.T[0:SUBLANES, :]


def _prep(P, gate, pos, invf, conv_w, conv_b, gbias):
    S = P.shape[0]
    R = PREP_ROWS
    nb = S // R
    halo_blocks = R // HALO_ROWS
    return pl.pallas_call(
        _prep_kernel,
        grid=(nb,),
        in_specs=[
            pl.BlockSpec((R, 2 * M_WIDTH), lambda i: (i, COL_MQK // (2 * M_WIDTH))),
            pl.BlockSpec((HALO_ROWS, 2 * M_WIDTH),
                         lambda i: (jnp.maximum(i * halo_blocks - 1, 0), COL_MQK // (2 * M_WIDTH))),
            pl.BlockSpec((R, 2 * A_WIDTH), lambda i: (i, COL_AQK // (2 * A_WIDTH))),
            pl.BlockSpec((R, A_WIDTH), lambda i: (i, COL_AV // A_WIDTH)),
            pl.BlockSpec((R, LANES), lambda i: (i, 0)),
            pl.BlockSpec((R, 1), lambda i: (i, 0)),
            pl.BlockSpec((1, A_HEAD_DIM), lambda i: (0, 0)),
            pl.BlockSpec((CONV_WIDTH, 2 * M_WIDTH), lambda i: (0, 0)),
            pl.BlockSpec((1, 2 * M_WIDTH), lambda i: (0, 0)),
            pl.BlockSpec((1, LANES), lambda i: (0, 0)),
        ],
        out_specs=[
            pl.BlockSpec((R, M_WIDTH), lambda i: (i, 0)),
            pl.BlockSpec((M_WIDTH, R), lambda i: (0, i)),
            pl.BlockSpec((R, A_WIDTH), lambda i: (i, 0)),
            pl.BlockSpec((R, A_WIDTH), lambda i: (i, 0)),
            pl.BlockSpec((A_WIDTH, R), lambda i: (0, i)),
            pl.BlockSpec((1, 1, A_WIDTH), lambda i: (i, 0, 0)),
            pl.BlockSpec((R, LANES), lambda i: (i, 0)),
            pl.BlockSpec((SUBLANES, R), lambda i: (0, i)),
        ],
        out_shape=[
            jax.ShapeDtypeStruct((S, M_WIDTH), BF16),
            jax.ShapeDtypeStruct((M_WIDTH, S), BF16),
            jax.ShapeDtypeStruct((S, A_WIDTH), F32),
            jax.ShapeDtypeStruct((S, A_WIDTH), BF16),
            jax.ShapeDtypeStruct((A_WIDTH, S), BF16),
            jax.ShapeDtypeStruct((nb, 1, A_WIDTH), F32),
            jax.ShapeDtypeStruct((S, LANES), F32),
            jax.ShapeDtypeStruct((SUBLANES, S), F32),
        ],
        compiler_params=_cparams(("parallel",)),
        name="prep",
    )(P, P, P, P, gate, pos, invf, conv_w, conv_b, gbias)


M_AUG = M_HEAD_DIM + LANES


def _mlstm_kernel(q_ref, kT_ref, v_ref, mo_ref, gc_ref, gt_ref, gain_ref, out_ref,
                  c_ref, m_ref):
    @pl.when(pl.program_id(0) == 0)
    def _():
        c_ref[...] = jnp.zeros_like(c_ref)
        m_ref[...] = jnp.zeros_like(m_ref)

    L = M_CHUNK
    D = M_HEAD_DIM
    row = lax.broadcasted_iota(jnp.int32, (L, L), 0)
    col = lax.broadcasted_iota(jnp.int32, (L, L), 1)
    causal = row >= col
    ones_col = (lax.broadcasted_iota(jnp.int32, (L, LANES), 1) == 0).astype(BF16)

    for h in range(M_HEADS):
        hs = slice(h * D, (h + 1) * D)
        q = q_ref[:, hs]
        kT = kT_ref[hs, :]
        v_aug = jnp.concatenate([v_ref[:, hs], ones_col], axis=1)
        b_c = gc_ref[:, M_HEADS + h:M_HEADS + h + 1]
        li_r = gt_ref[h:h + 1, :]
        b_r = gt_ref[M_HEADS + h:M_HEADS + h + 1, :]
        f_tot = b_r[:, L - 1:L]
        m_prev = m_ref[h:h + 1, 0:1]

        u_r = li_r - b_r
        d_log = jnp.where(causal, b_c + u_r, -jnp.inf)
        a_log = b_c + m_prev
        m_t = jnp.maximum(a_log, jnp.max(d_log, axis=1, keepdims=True))
        s_qk = jnp.dot(q, kT, preferred_element_type=F32)
        s_ts = s_qk * jnp.exp(d_log - m_t)
        inter = jnp.exp(a_log - m_t)
        c_old = c_ref[h]
        r = (inter * jnp.dot(q, c_old.astype(BF16), preferred_element_type=F32)
             + jnp.dot(s_ts.astype(BF16), v_aug, preferred_element_type=F32))
        num = r[:, :D]
        den = r[:, D:D + 1]
        hh = num / jnp.maximum(jnp.abs(den), jnp.exp(-m_t))
        hn = hh * lax.rsqrt(jnp.mean(hh * hh, axis=-1, keepdims=True) + NORM_EPS)
        out_ref[:, hs] = (hn * gain_ref[:, hs] * _sigmoid(mo_ref[:, hs].astype(F32))).astype(BF16)

        w_r = f_tot + u_r
        m_loc = jnp.max(w_r, axis=1, keepdims=True)
        m_new = jnp.maximum(f_tot + m_prev, m_loc)
        a = jnp.exp(f_tot + m_prev - m_new)
        e_r = jnp.exp(w_r - m_new)
        keT = (kT.astype(F32) * e_r).astype(BF16)
        c_ref[h] = a * c_old + jnp.dot(keT, v_aug, preferred_element_type=F32)
        m_ref[h:h + 1, :] = jnp.broadcast_to(m_new, (1, LANES))


def _mlstm(mq, mkT, P, gc, gt, gain):
    S = mq.shape[0]
    L = M_CHUNK
    return pl.pallas_call(
        _mlstm_kernel,
        grid=(S // L,),
        in_specs=[
            pl.BlockSpec((L, M_WIDTH), lambda c: (c, 0)),
            pl.BlockSpec((M_WIDTH, L), lambda c: (0, c)),
            pl.BlockSpec((L, M_WIDTH), lambda c: (c, COL_MV // M_WIDTH)),
            pl.BlockSpec((L, M_WIDTH), lambda c: (c, COL_MO // M_WIDTH)),
            pl.BlockSpec((L, LANES), lambda c: (c, 0)),
            pl.BlockSpec((SUBLANES, L), lambda c: (0, c)),
            pl.BlockSpec((1, M_WIDTH), lambda c: (0, 0)),
        ],
        out_specs=pl.BlockSpec((L, M_WIDTH), lambda c: (c, 0)),
        out_shape=jax.ShapeDtypeStruct((S, M_WIDTH), BF16),
        scratch_shapes=[
            pltpu.VMEM((M_HEADS, M_HEAD_DIM, M_AUG), F32),
            pltpu.VMEM((SUBLANES, LANES), F32),
        ],
        compiler_params=_cparams(("arbitrary",)),
        name="mlstm",
    )(mq, mkT, P, P, gc, gt, gain)


MOBA_HEADS_PER_STEP = 4


def _moba_kernel(q_ref, k_ref, vT_ref, km_ref, o_ref, sel_ref, qs_ref, m_ref, l_ref, acc_ref,
                 s0_ref, s1_ref, mb0_ref, mb1_ref):
    i = pl.program_id(1)
    BS = MOBA_BLOCK
    Dh = A_HEAD_DIM
    NB = km_ref.shape[0]
    G = MOBA_HEADS_PER_STEP
    own = pl.multiple_of(i * BS, BS)
    blk = lax.broadcasted_iota(jnp.int32, (NB, BS), 0)
    kpos = lax.broadcasted_iota(jnp.int32, (BS, BS), 0)
    qpos = lax.broadcasted_iota(jnp.int32, (BS, BS), 1)

    ones_rows = jnp.ones((2 * SUBLANES, BS), BF16)

    def stage_scores(start, s_buf, mb_buf, causal):
        for g in range(G):
            hs = slice(g * Dh, (g + 1) * Dh)
            s = jnp.dot(k_ref[pl.ds(start, BS), hs], qs_ref[g], preferred_element_type=F32)
            if causal:
                s = jnp.where(kpos <= qpos, s, NEG_BIG)
            sb = s.astype(BF16)
            s_buf[g] = sb
            mb_buf[g] = jnp.max(sb, axis=0, keepdims=True).astype(F32)

    def stage_apply(start, picked_of, s_buf, mb_buf):
        for g in range(G):
            hs = slice(g * Dh, (g + 1) * Dh)
            picked = picked_of(g)
            m_old = m_ref[g]
            m_new = jnp.maximum(m_old, jnp.where(picked, mb_buf[g], NEG_BIG))
            a_old = jnp.exp2(m_old - m_new)
            shift = jnp.where(picked, m_new, -NEG_BIG)
            p = jnp.exp2(s_buf[g] - shift.astype(BF16))
            v_aug = jnp.concatenate([vT_ref[hs, pl.ds(start, BS)], ones_rows], axis=0)
            r = jnp.dot(v_aug, p, preferred_element_type=F32)
            l_ref[g] = a_old * l_ref[g] + r[Dh:Dh + 1, :]
            acc_ref[g] = a_old * acc_ref[g] + r[:Dh, :]
            m_ref[g] = m_new

    for g in range(G):
        hs = slice(g * Dh, (g + 1) * Dh)
        qT = q_ref[:, hs].T
        km = km_ref[:, hs]
        kh = km.astype(BF16)
        kl = (km - kh.astype(F32)).astype(BF16)
        qh = qT.astype(BF16)
        ql = (qT - qh.astype(F32)).astype(BF16)
        gate = (jnp.dot(kh, qh, preferred_element_type=F32)
                + jnp.dot(kh, ql, preferred_element_type=F32)
                + jnp.dot(kl, qh, preferred_element_type=F32))
        gate = jnp.where(blk < i, gate, -jnp.inf)
        sel = jnp.zeros((NB, BS), F32)
        for r in range(MOBA_TOPK):
            mx = jnp.max(gate, axis=0, keepdims=True)
            idx = jnp.min(jnp.where(gate == mx, blk, NB), axis=0, keepdims=True)
            idx = jnp.where(r < i, idx, -1)
            pick = blk == idx
            sel = jnp.where(pick, 1.0, sel)
            gate = jnp.where(pick, -jnp.inf, gate)
        sel_ref[g] = sel
        qs_ref[g] = (qT * (A_HEAD_DIM ** -0.5 * LOG2_E)).astype(BF16)

    m_ref[...] = jnp.full(m_ref.shape, NEG_BIG, F32)
    l_ref[...] = jnp.zeros(l_ref.shape, F32)
    acc_ref[...] = jnp.zeros(acc_ref.shape, F32)

    def block_start(j):
        return pl.multiple_of(jnp.minimum(j, NB - 1) * BS, BS)

    def picked_past(j):
        return lambda g: sel_ref[g, pl.ds(j, 1), :] > 0.0

    stage_scores(own, s1_ref, mb1_ref, True)
    stage_scores(block_start(0), s0_ref, mb0_ref, False)
    stage_apply(own, lambda g: jnp.full((1, BS), True), s1_ref, mb1_ref)

    def body(t, carry):
        j = 2 * t
        stage_scores(block_start(j + 1), s1_ref, mb1_ref, False)
        stage_apply(block_start(j), picked_past(j), s0_ref, mb0_ref)
        stage_scores(block_start(j + 2), s0_ref, mb0_ref, False)
        stage_apply(block_start(j + 1), picked_past(j + 1), s1_ref, mb1_ref)
        return carry

    lax.fori_loop(0, (i + 1) // 2, body, 0)

    for g in range(G):
        o_ref[:, g * Dh:(g + 1) * Dh] = (acc_ref[g] / l_ref[g]).T.astype(BF16)


def _moba(aq, ak, avT, kmean):
    S = aq.shape[0]
    BS = MOBA_BLOCK
    NB = S // BS
    G = MOBA_HEADS_PER_STEP
    W = G * A_HEAD_DIM
    once = pl.Buffered(1)
    return pl.pallas_call(
        _moba_kernel,
        grid=(A_HEADS // G, NB),
        in_specs=[
            pl.BlockSpec((BS, W), lambda h, i: (i, h)),
            pl.BlockSpec((S, W), lambda h, i: (0, h), pipeline_mode=once),
            pl.BlockSpec((W, S), lambda h, i: (h, 0), pipeline_mode=once),
            pl.BlockSpec((NB, W), lambda h, i: (0, h)),
        ],
        out_specs=pl.BlockSpec((BS, W), lambda h, i: (i, h)),
        out_shape=jax.ShapeDtypeStruct((S, A_WIDTH), BF16),
        scratch_shapes=[
            pltpu.VMEM((G, NB, BS), F32),
            pltpu.VMEM((G, A_HEAD_DIM, BS), BF16),
            pltpu.VMEM((G, 1, BS), F32),
            pltpu.VMEM((G, 1, BS), F32),
            pltpu.VMEM((G, A_HEAD_DIM, BS), F32),
            pltpu.VMEM((G, BS, BS), BF16),
            pltpu.VMEM((G, BS, BS), BF16),
            pltpu.VMEM((G, 1, BS), F32),
            pltpu.VMEM((G, 1, BS), F32),
        ],
        compiler_params=_cparams(("parallel", "arbitrary")),
        name="moba",
    )(aq, ak, avT, kmean)


def _merge_kernel(hm_ref, ha_ref, gm_ref, ga_ref, x_ref, wm_ref, wa_ref, wo_ref, gain_ref,
                  out_ref):
    ym = jnp.dot(hm_ref[...], wm_ref[...], preferred_element_type=F32)
    ya = jnp.dot(ha_ref[...], wa_ref[...], preferred_element_type=F32)
    merged = _sigmoid(gm_ref[...].astype(F32)) * ym + _sigmoid(ga_ref[...].astype(F32)) * ya
    mix = jnp.dot(merged.astype(BF16), wo_ref[...], preferred_element_type=F32)
    out_ref[...] = x_ref[...] + _rms(mix, gain_ref[...])


def _merge(hm, ha, P, x, wm, wa, wo, gain, tm=256):
    S, D = x.shape
    const = pl.Buffered(1)
    return pl.pallas_call(
        _merge_kernel,
        grid=(S // tm,),
        in_specs=[
            pl.BlockSpec((tm, M_WIDTH), lambda i: (i, 0)),
            pl.BlockSpec((tm, A_WIDTH), lambda i: (i, 0)),
            pl.BlockSpec((tm, D), lambda i: (i, COL_GM // D)),
            pl.BlockSpec((tm, D), lambda i: (i, COL_GA // D)),
            pl.BlockSpec((tm, D), lambda i: (i, 0)),
            pl.BlockSpec((M_WIDTH, D), lambda i: (0, 0), pipeline_mode=const),
            pl.BlockSpec((A_WIDTH, D), lambda i: (0, 0), pipeline_mode=const),
            pl.BlockSpec((D, D), lambda i: (0, 0), pipeline_mode=const),
            pl.BlockSpec((1, D), lambda i: (0, 0)),
        ],
        out_specs=pl.BlockSpec((tm, D), lambda i: (i, 0)),
        out_shape=jax.ShapeDtypeStruct((S, D), F32),
        compiler_params=_cparams(("parallel",)),
        name="merge",
    )(hm, ha, P, P, x, wm, wa, wo, gain)


def _ffn_kernel(x_ref, gpre_ref, wu_ref, wd_ref, gpost_ref, out_ref, hn_ref, acc_ref):
    f = pl.program_id(1)

    @pl.when(f == 0)
    def _():
        hn_ref[...] = _rms(x_ref[...], gpre_ref[...]).astype(BF16)
        acc_ref[...] = jnp.zeros_like(acc_ref)

    u = jnp.dot(hn_ref[...], wu_ref[...], preferred_element_type=F32)
    u = jnp.square(jnp.maximum(u, 0.0)).astype(BF16)
    acc_ref[...] += jnp.dot(u, wd_ref[...], preferred_element_type=F32)

    @pl.when(f == pl.num_programs(1) - 1)
    def _():
        out_ref[...] = x_ref[...] + _rms(acc_ref[...], gpost_ref[...])


def _ffn(x, gpre, wu, wd, gpost, tm=512, tf=1024):
    S, D = x.shape
    Fd = wu.shape[1]
    return pl.pallas_call(
        _ffn_kernel,
        grid=(S // tm, Fd // tf),
        in_specs=[
            pl.BlockSpec((tm, D), lambda i, f: (i, 0)),
            pl.BlockSpec((1, D), lambda i, f: (0, 0)),
            pl.BlockSpec((D, tf), lambda i, f: (0, f)),
            pl.BlockSpec((tf, D), lambda i, f: (f, 0)),
            pl.BlockSpec((1, D), lambda i, f: (0, 0)),
        ],
        out_specs=pl.BlockSpec((tm, D), lambda i, f: (i, 0)),
        out_shape=jax.ShapeDtypeStruct((S, D), F32),
        scratch_shapes=[pltpu.VMEM((tm, D), BF16), pltpu.VMEM((tm, D), F32)],
        compiler_params=_cparams(("parallel", "arbitrary")),
        name="ffn",
    )(x, gpre, wu, wd, gpost)


def _layer(x, pos, norm_mix_pre, w_in, conv_w, conv_b, i_bias, f_bias, mlstm_norm,
           w_branch_m, w_branch_a, w_out, norm_mix_post, norm_ffn_pre, w_up, w_down,
           norm_ffn_post):
    S, D = x.shape
    o = 0
    pieces = {}
    w_in = w_in.astype(BF16)
    for name, width in (("mq", M_WIDTH), ("mk", M_WIDTH), ("mv", M_WIDTH), ("mo", M_WIDTH),
                        ("mi", M_HEADS), ("mf", M_HEADS), ("aq", A_WIDTH), ("ak", A_WIDTH),
                        ("av", A_WIDTH), ("gm", D), ("ga", D)):
        pieces[name] = w_in[:, o:o + width]
        o += width
    w_all = jnp.concatenate([pieces[n] for n in ("gm", "ga", "mq", "mk", "aq", "ak", "mv", "mo", "av")],
                            axis=1)
    w_gate = jnp.concatenate(
        [pieces["mi"], pieces["mf"], jnp.zeros((D, LANES - 2 * M_HEADS), BF16)], axis=1)
    gbias = jnp.concatenate([i_bias, f_bias, jnp.zeros((LANES - 2 * M_HEADS,), F32)])[None, :]
    half = jnp.arange(0, A_HEAD_DIM, 2, dtype=F32) / A_HEAD_DIM
    inv_freq = 1.0 / (ROPE_THETA ** half)
    invf = jnp.concatenate([inv_freq, inv_freq])[None, :]

    P, gate = _proj(x, norm_mix_pre[None, :], w_all, w_gate)
    mq, mkT, aq, ak, avT, kmean, gc, gt = _prep(
        P, gate, pos.reshape(S, 1), invf, conv_w, conv_b[None, :], gbias)
    hm = _mlstm(mq, mkT, P, gc, gt, mlstm_norm[None, :])
    ha = _moba(aq, ak, avT, kmean.reshape(S // MOBA_BLOCK, A_WIDTH))
    x1 = _merge(hm, ha, P, x, w_branch_m.astype(BF16), w_branch_a.astype(BF16),
                w_out.astype(BF16), norm_mix_post[None, :])
    return _ffn(x1, norm_ffn_pre[None, :], w_up.astype(BF16), w_down.astype(BF16),
                norm_ffn_post[None, :])


def kernel(x, positions, norm_mix_pre, w_in, conv_w, conv_b, i_bias, f_bias, mlstm_norm,
           w_branch_m, w_branch_a, w_out, norm_mix_post, norm_ffn_pre, w_up, w_down,
           norm_ffn_post):
    B = x.shape[0]
    depth = w_in.shape[0]
    outs = []
    for b in range(B):
        xb = x[b]
        for l in range(depth):
            xb = _layer(xb, positions[b], norm_mix_pre[l], w_in[l], conv_w[l], conv_b[l],
                        i_bias[l], f_bias[l], mlstm_norm[l], w_branch_m[l], w_branch_a[l],
                        w_out[l], norm_mix_post[l], norm_ffn_pre[l], w_up[l], w_down[l],
                        norm_ffn_post[l])
        outs.append(xb)
    return outs[0][None] if B == 1 else jnp.stack(outs, axis=0)
```

```python
import functools

import jax
import jax.numpy as jnp
from jax import lax
from jax.experimental import pallas as pl
from jax.experimental.pallas import tpu as pltpu

F32 = jnp.float32
BF16 = jnp.bfloat16

M_HEADS = 4
M_HEAD_DIM = 256
M_WIDTH = M_HEADS * M_HEAD_DIM
M_CHUNK = 128
CONV_WIDTH = 4
A_HEADS = 8
A_HEAD_DIM = 128
A_WIDTH = A_HEADS * A_HEAD_DIM
MOBA_BLOCK = 256
MOBA_TOPK = 3
ROPE_THETA = 10000.0
NORM_EPS = 1e-6

LANES = 128
SUBLANES = 8
VMEM_LIMIT = 56 * 1024 * 1024
NEG_BIG = -1e30
LOG2_E = 1.4426950408889634

COL_GM = 0
COL_GA = 2048
COL_MQK = 4096
COL_AQK = 6144
COL_MV = 8192
COL_MO = 9216
COL_AV = 10240
P_COLS = 11264


def _cparams(sem):
    return pltpu.CompilerParams(dimension_semantics=sem, vmem_limit_bytes=VMEM_LIMIT)


def _rms(x, gain):
    ms = jnp.mean(x * x, axis=-1, keepdims=True)
    return x * lax.rsqrt(ms + NORM_EPS) * gain


def _sigmoid(x):
    return 1.0 / (1.0 + jnp.exp(-x))


def _split3(x):
    hi = x.astype(BF16)
    r1 = x - hi.astype(F32)
    mid = r1.astype(BF16)
    lo = (r1 - mid.astype(F32)).astype(BF16)
    return hi, mid, lo


def _proj_kernel(x_ref, g_ref, w_ref, wg_ref, p_ref, gate_ref, xn_ref):
    @pl.when(pl.program_id(1) == 0)
    def _():
        xn = _rms(x_ref[...], g_ref[...]).astype(BF16)
        xn_ref[...] = xn
        gate_ref[...] = jnp.dot(xn, wg_ref[...], preferred_element_type=F32)

    p_ref[...] = jnp.dot(xn_ref[...], w_ref[...], preferred_element_type=F32).astype(BF16)


def _proj(x, gain, w_all, w_gate, tm=1024, tn=1024):
    S, D = x.shape
    N = w_all.shape[1]
    return pl.pallas_call(
        _proj_kernel,
        grid=(S // tm, N // tn),
        in_specs=[
            pl.BlockSpec((tm, D), lambda i, j: (i, 0)),
            pl.BlockSpec((1, D), lambda i, j: (0, 0)),
            pl.BlockSpec((D, tn), lambda i, j: (0, j)),
            pl.BlockSpec((D, LANES), lambda i, j: (0, 0)),
        ],
        out_specs=[
            pl.BlockSpec((tm, tn), lambda i, j: (i, j)),
            pl.BlockSpec((tm, LANES), lambda i, j: (i, 0)),
        ],
        out_shape=[
            jax.ShapeDtypeStruct((S, N), BF16),
            jax.ShapeDtypeStruct((S, LANES), F32),
        ],
        scratch_shapes=[pltpu.VMEM((tm, D), BF16)],
        compiler_params=_cparams(("parallel", "arbitrary")),
        name="proj",
    )(x, gain, w_all, w_gate)


PREP_ROWS = MOBA_BLOCK
PREP_COLS = 512
HALO_ROWS = 2 * SUBLANES


def _prep_kernel(pmk_ref, halo_ref, paqk_ref, pav_ref, gate_ref, pos_ref, invf_ref,
                 cw_ref, cb_ref, gb_ref,
                 mq_ref, mkT_ref, aq_ref, ak_ref, avT_ref, kmean_ref, gc_ref, gt_ref):
    i = pl.program_id(0)
    R = PREP_ROWS

    k_scale = M_HEAD_DIM ** -0.5
    for c0 in range(0, 2 * M_WIDTH, PREP_COLS):
        cs = slice(c0, c0 + PREP_COLS)
        prev = halo_ref[:, cs].astype(F32)[HALO_ROWS - SUBLANES:, :]
        prev = jnp.where(i == 0, jnp.zeros_like(prev), prev)
        ext = jnp.concatenate([pmk_ref[:, cs].astype(F32), prev], axis=0)
        acc = cw_ref[0:1, cs] * ext
        for j in range(1, CONV_WIDTH):
            acc = pltpu.roll(acc, 1, 0) + cw_ref[j:j + 1, cs] * ext
        acc = acc[0:R, :] + cb_ref[:, cs]
        y = acc * _sigmoid(acc)
        if c0 < M_WIDTH:
            mq_ref[:, cs] = y.astype(BF16)
        else:
            ks = slice(c0 - M_WIDTH, c0 - M_WIDTH + PREP_COLS)
            mkT_ref[ks, :] = (y * k_scale).T.astype(BF16)

    ang = pos_ref[...].astype(F32) * invf_ref[...]
    cos = jnp.cos(ang)
    lane = lax.broadcasted_iota(jnp.int32, (R, A_HEAD_DIM), 1)
    sin_signed = jnp.where(lane < A_HEAD_DIM // 2, -1.0, 1.0) * jnp.sin(ang)
    for h in range(2 * A_HEADS):
        hs = slice(h * A_HEAD_DIM, (h + 1) * A_HEAD_DIM)
        xh = paqk_ref[:, hs].astype(F32)
        yh = xh * cos + pltpu.roll(xh, A_HEAD_DIM // 2, 1) * sin_signed
        if h < A_HEADS:
            aq_ref[:, hs] = yh
        else:
            ko = slice((h - A_HEADS) * A_HEAD_DIM, (h - A_HEADS + 1) * A_HEAD_DIM)
            ak_ref[:, ko] = yh.astype(BF16)
            kmean_ref[0, :, ko] = jnp.mean(yh, axis=0, keepdims=True)
    for c0 in range(0, A_WIDTH, PREP_COLS):
        avT_ref[c0:c0 + PREP_COLS, :] = pav_ref[:, c0:c0 + PREP_COLS].astype(F32).T.astype(BF16)

    g = gate_ref[...] + gb_ref[...]
    log_f = jnp.minimum(g, 0.0) - jnp.log1p(jnp.exp(-jnp.abs(g)))
    r_i = lax.broadcasted_iota(jnp.int32, (R, R), 0)
    c_i = lax.broadcasted_iota(jnp.int32, (R, R), 1)
    tri = ((r_i >= c_i) & ((r_i // M_CHUNK) == (c_i // M_CHUNK))).astype(BF16)
    hi, mid, lo = _split3(log_f)
    csum = (jnp.dot(tri, hi, preferred_element_type=F32)
            + jnp.dot(tri, mid, preferred_element_type=F32)
            + jnp.dot(tri, lo, preferred_element_type=F32))
    glane = lax.broadcasted_iota(jnp.int32, (R, LANES), 1)
    gc = jnp.where(glane < M_HEADS, g, csum)
    gc_ref[...] = gc
    gt_ref[...] = gc.T[0:SUBLANES, :]


def _prep(P, gate, pos, invf, conv_w, conv_b, gbias):
    S = P.shape[0]
    R = PREP_ROWS
    nb = S // R
    halo_blocks = R // HALO_ROWS
    return pl.pallas_call(
        _prep_kernel,
        grid=(nb,),
        in_specs=[
            pl.BlockSpec((R, 2 * M_WIDTH), lambda i: (i, COL_MQK // (2 * M_WIDTH))),
            pl.BlockSpec((HALO_ROWS, 2 * M_WIDTH),
                         lambda i: (jnp.maximum(i * halo_blocks - 1, 0), COL_MQK // (2 * M_WIDTH))),
            pl.BlockSpec((R, 2 * A_WIDTH), lambda i: (i, COL_AQK // (2 * A_WIDTH))),
            pl.BlockSpec((R, A_WIDTH), lambda i: (i, COL_AV // A_WIDTH)),
            pl.BlockSpec((R, LANES), lambda i: (i, 0)),
            pl.BlockSpec((R, 1), lambda i: (i, 0)),
            pl.BlockSpec((1, A_HEAD_DIM), lambda i: (0, 0)),
            pl.BlockSpec((CONV_WIDTH, 2 * M_WIDTH), lambda i: (0, 0)),
            pl.BlockSpec((1, 2 * M_WIDTH), lambda i: (0, 0)),
            pl.BlockSpec((1, LANES), lambda i: (0, 0)),
        ],
        out_specs=[
            pl.BlockSpec((R, M_WIDTH), lambda i: (i, 0)),
            pl.BlockSpec((M_WIDTH, R), lambda i: (0, i)),
            pl.BlockSpec((R, A_WIDTH), lambda i: (i, 0)),
            pl.BlockSpec((R, A_WIDTH), lambda i: (i, 0)),
            pl.BlockSpec((A_WIDTH, R), lambda i: (0, i)),
            pl.BlockSpec((1, 1, A_WIDTH), lambda i: (i, 0, 0)),
            pl.BlockSpec((R, LANES), lambda i: (i, 0)),
            pl.BlockSpec((SUBLANES, R), lambda i: (0, i)),
        ],
        out_shape=[
            jax.ShapeDtypeStruct((S, M_WIDTH), BF16),
            jax.ShapeDtypeStruct((M_WIDTH, S), BF16),
            jax.ShapeDtypeStruct((S, A_WIDTH), F32),
            jax.ShapeDtypeStruct((S, A_WIDTH), BF16),
            jax.ShapeDtypeStruct((A_WIDTH, S), BF16),
            jax.ShapeDtypeStruct((nb, 1, A_WIDTH), F32),
            jax.ShapeDtypeStruct((S, LANES), F32),
            jax.ShapeDtypeStruct((SUBLANES, S), F32),
        ],
        compiler_params=_cparams(("parallel",)),
        name="prep",
    )(P, P, P, P, gate, pos, invf, conv_w, conv_b, gbias)


M_AUG = M_HEAD_DIM + LANES


def _mlstm_kernel(q_ref, kT_ref, v_ref, mo_ref, gc_ref, gt_ref, gain_ref, out_ref,
                  c_ref, m_ref):
    @pl.when(pl.program_id(0) == 0)
    def _():
        c_ref[...] = jnp.zeros_like(c_ref)
        m_ref[...] = jnp.zeros_like(m_ref)

    L = M_CHUNK
    D = M_HEAD_DIM
    row = lax.broadcasted_iota(jnp.int32, (L, L), 0)
    col = lax.broadcasted_iota(jnp.int32, (L, L), 1)
    causal = row >= col
    ones_col = (lax.broadcasted_iota(jnp.int32, (L, LANES), 1) == 0).astype(BF16)

    heads = range(M_HEADS)
    hsl = [slice(h * D, (h + 1) * D) for h in heads]
    q = [q_ref[:, hsl[h]] for h in heads]
    kT = [kT_ref[hsl[h], :] for h in heads]
    v_aug = [jnp.concatenate([v_ref[:, hsl[h]], ones_col], axis=1) for h in heads]
    b_c = [gc_ref[:, M_HEADS + h:M_HEADS + h + 1] for h in heads]
    b_r = [gt_ref[M_HEADS + h:M_HEADS + h + 1, :] for h in heads]
    u_r = [gt_ref[h:h + 1, :] - b_r[h] for h in heads]
    f_tot = [b_r[h][:, L - 1:L] for h in heads]
    m_prev = [m_ref[h:h + 1, 0:1] for h in heads]

    s_qk = [jnp.dot(q[h], kT[h], preferred_element_type=F32) for h in heads]
    q_c = [jnp.dot(q[h], c_ref[h].astype(BF16), preferred_element_type=F32) for h in heads]

    for h in heads:
        w_r = f_tot[h] + u_r[h]
        m_loc = jnp.max(w_r, axis=1, keepdims=True)
        m_new = jnp.maximum(f_tot[h] + m_prev[h], m_loc)
        a = jnp.exp(f_tot[h] + m_prev[h] - m_new)
        e_r = jnp.exp(w_r - m_new)
        keT = (kT[h].astype(F32) * e_r).astype(BF16)
        c_ref[h] = a * c_ref[h] + jnp.dot(keT, v_aug[h], preferred_element_type=F32)
        m_ref[h:h + 1, :] = jnp.broadcast_to(m_new, (1, LANES))

    for h in heads:
        d_log = jnp.where(causal, b_c[h] + u_r[h], -jnp.inf)
        a_log = b_c[h] + m_prev[h]
        m_t = jnp.maximum(a_log, jnp.max(d_log, axis=1, keepdims=True))
        s_ts = s_qk[h] * jnp.exp(d_log - m_t)
        inter = jnp.exp(a_log - m_t)
        r = inter * q_c[h] + jnp.dot(s_ts.astype(BF16), v_aug[h], preferred_element_type=F32)
        num = r[:, :D]
        den = r[:, D:D + 1]
        hh = num / jnp.maximum(jnp.abs(den), jnp.exp(-m_t))
        hn = hh * lax.rsqrt(jnp.mean(hh * hh, axis=-1, keepdims=True) + NORM_EPS)
        out_ref[:, hsl[h]] = (hn * gain_ref[:, hsl[h]]
                              * _sigmoid(mo_ref[:, hsl[h]].astype(F32))).astype(BF16)


def _mlstm(mq, mkT, P, gc, gt, gain):
    S = mq.shape[0]
    L = M_CHUNK
    return pl.pallas_call(
        _mlstm_kernel,
        grid=(S // L,),
        in_specs=[
            pl.BlockSpec((L, M_WIDTH), lambda c: (c, 0)),
            pl.BlockSpec((M_WIDTH, L), lambda c: (0, c)),
            pl.BlockSpec((L, M_WIDTH), lambda c: (c, COL_MV // M_WIDTH)),
            pl.BlockSpec((L, M_WIDTH), lambda c: (c, COL_MO // M_WIDTH)),
            pl.BlockSpec((L, LANES), lambda c: (c, 0)),
            pl.BlockSpec((SUBLANES, L), lambda c: (0, c)),
            pl.BlockSpec((1, M_WIDTH), lambda c: (0, 0)),
        ],
        out_specs=pl.BlockSpec((L, M_WIDTH), lambda c: (c, 0)),
        out_shape=jax.ShapeDtypeStruct((S, M_WIDTH), BF16),
        scratch_shapes=[
            pltpu.VMEM((M_HEADS, M_HEAD_DIM, M_AUG), F32),
            pltpu.VMEM((SUBLANES, LANES), F32),
        ],
        compiler_params=_cparams(("arbitrary",)),
        name="mlstm",
    )(mq, mkT, P, P, gc, gt, gain)


MOBA_HEADS_PER_STEP = 4


def _moba_kernel(q_ref, k_ref, vT_ref, km_ref, o_ref, sel_ref, qs_ref, m_ref, l_ref, acc_ref,
                 s0_ref, s1_ref, mb0_ref, mb1_ref):
    i = pl.program_id(1)
    BS = MOBA_BLOCK
    Dh = A_HEAD_DIM
    NB = km_ref.shape[0]
    G = MOBA_HEADS_PER_STEP
    own = pl.multiple_of(i * BS, BS)
    blk = lax.broadcasted_iota(jnp.int32, (NB, BS), 0)
    kpos = lax.broadcasted_iota(jnp.int32, (BS, BS), 0)
    qpos = lax.broadcasted_iota(jnp.int32, (BS, BS), 1)

    ones_rows = jnp.ones((2 * SUBLANES, BS), BF16)

    def stage_scores(start, s_buf, mb_buf, causal):
        for g in range(G):
            hs = slice(g * Dh, (g + 1) * Dh)
            s = jnp.dot(k_ref[pl.ds(start, BS), hs], qs_ref[g], preferred_element_type=F32)
            if causal:
                s = jnp.where(kpos <= qpos, s, NEG_BIG)
            sb = s.astype(BF16)
            s_buf[g] = sb
            mb_buf[g] = jnp.max(sb, axis=0, keepdims=True).astype(F32)

    def stage_apply(start, picked_of, s_buf, mb_buf):
        for g in range(G):
            hs = slice(g * Dh, (g + 1) * Dh)
            picked = picked_of(g)
            m_old = m_ref[g]
            m_new = jnp.maximum(m_old, jnp.where(picked, mb_buf[g], NEG_BIG))
            a_old = jnp.exp2(m_old - m_new)
            shift = jnp.where(picked, m_new, -NEG_BIG)
            p = jnp.exp2(s_buf[g] - shift.astype(BF16))
            v_aug = jnp.concatenate([vT_ref[hs, pl.ds(start, BS)], ones_rows], axis=0)
            r = jnp.dot(v_aug, p, preferred_element_type=F32)
            l_ref[g] = a_old * l_ref[g] + r[Dh:Dh + 1, :]
            acc_ref[g] = a_old * acc_ref[g] + r[:Dh, :]
            m_ref[g] = m_new

    for g in range(G):
        hs = slice(g * Dh, (g + 1) * Dh)
        qT = q_ref[:, hs].T
        km = km_ref[:, hs]
        kh = km.astype(BF16)
        kl = (km - kh.astype(F32)).astype(BF16)
        qh = qT.astype(BF16)
        ql = (qT - qh.astype(F32)).astype(BF16)
        gate = (jnp.dot(kh, qh, preferred_element_type=F32)
                + jnp.dot(kh, ql, preferred_element_type=F32)
                + jnp.dot(kl, qh, preferred_element_type=F32))
        gate = jnp.where(blk < i, gate, -jnp.inf)
        sel = jnp.zeros((NB, BS), F32)
        for r in range(MOBA_TOPK):
            mx = jnp.max(gate, axis=0, keepdims=True)
            idx = jnp.min(jnp.where(gate == mx, blk, NB), axis=0, keepdims=True)
            idx = jnp.where(r < i, idx, -1)
            pick = blk == idx
            sel = jnp.where(pick, 1.0, sel)
            gate = jnp.where(pick, -jnp.inf, gate)
        sel_ref[g] = sel
        qs_ref[g] = (qT * (A_HEAD_DIM ** -0.5 * LOG2_E)).astype(BF16)

    m_ref[...] = jnp.full(m_ref.shape, NEG_BIG, F32)
    l_ref[...] = jnp.zeros(l_ref.shape, F32)
    acc_ref[...] = jnp.zeros(acc_ref.shape, F32)

    def block_start(j):
        return pl.multiple_of(jnp.minimum(j, NB - 1) * BS, BS)

    def picked_past(j):
        return lambda g: sel_ref[g, pl.ds(j, 1), :] > 0.0

    stage_scores(own, s1_ref, mb1_ref, True)
    stage_scores(block_start(0), s0_ref, mb0_ref, False)
    stage_apply(own, lambda g: jnp.full((1, BS), True), s1_ref, mb1_ref)

    def pair(j):
        stage_scores(block_start(j + 1), s1_ref, mb1_ref, False)
        stage_apply(block_start(j), picked_past(j), s0_ref, mb0_ref)
        stage_scores(block_start(j + 2), s0_ref, mb0_ref, False)
        stage_apply(block_start(j + 1), picked_past(j + 1), s1_ref, mb1_ref)

    def quad_body(t, carry):
        pair(4 * t)
        pair(4 * t + 2)
        return carry

    def pair_body(t, carry):
        pair(4 * (i // 4) + 2 * t)
        return carry

    lax.fori_loop(0, i // 4, quad_body, 0)
    lax.fori_loop(0, (i % 4 + 1) // 2, pair_body, 0)

    for g in range(G):
        o_ref[:, g * Dh:(g + 1) * Dh] = (acc_ref[g] / l_ref[g]).T.astype(BF16)


def _moba(aq, ak, avT, kmean):
    S = aq.shape[0]
    BS = MOBA_BLOCK
    NB = S // BS
    G = MOBA_HEADS_PER_STEP
    W = G * A_HEAD_DIM
    once = pl.Buffered(1)
    return pl.pallas_call(
        _moba_kernel,
        grid=(A_HEADS // G, NB),
        in_specs=[
            pl.BlockSpec((BS, W), lambda h, i: (i, h)),
            pl.BlockSpec((S, W), lambda h, i: (0, h), pipeline_mode=once),
            pl.BlockSpec((W, S), lambda h, i: (h, 0), pipeline_mode=once),
            pl.BlockSpec((NB, W), lambda h, i: (0, h)),
        ],
        out_specs=pl.BlockSpec((BS, W), lambda h, i: (i, h)),
        out_shape=jax.ShapeDtypeStruct((S, A_WIDTH), BF16),
        scratch_shapes=[
            pltpu.VMEM((G, NB, BS), F32),
            pltpu.VMEM((G, A_HEAD_DIM, BS), BF16),
            pltpu.VMEM((G, 1, BS), F32),
            pltpu.VMEM((G, 1, BS), F32),
            pltpu.VMEM((G, A_HEAD_DIM, BS), F32),
            pltpu.VMEM((G, BS, BS), BF16),
            pltpu.VMEM((G, BS, BS), BF16),
            pltpu.VMEM((G, 1, BS), F32),
            pltpu.VMEM((G, 1, BS), F32),
        ],
        compiler_params=_cparams(("parallel", "arbitrary")),
        name="moba",
    )(aq, ak, avT, kmean)


def _merge_kernel(hm_ref, ha_ref, gm_ref, ga_ref, x_ref, wm_ref, wa_ref, wo_ref, gain_ref,
                  out_ref):
    ym = jnp.dot(hm_ref[...], wm_ref[...], preferred_element_type=F32)
    ya = jnp.dot(ha_ref[...], wa_ref[...], preferred_element_type=F32)
    merged = _sigmoid(gm_ref[...].astype(F32)) * ym + _sigmoid(ga_ref[...].astype(F32)) * ya
    mix = jnp.dot(merged.astype(BF16), wo_ref[...], preferred_element_type=F32)
    out_ref[...] = x_ref[...] + _rms(mix, gain_ref[...])


def _merge(hm, ha, P, x, wm, wa, wo, gain, tm=256):
    S, D = x.shape
    const = pl.Buffered(1)
    return pl.pallas_call(
        _merge_kernel,
        grid=(S // tm,),
        in_specs=[
            pl.BlockSpec((tm, M_WIDTH), lambda i: (i, 0)),
            pl.BlockSpec((tm, A_WIDTH), lambda i: (i, 0)),
            pl.BlockSpec((tm, D), lambda i: (i, COL_GM // D)),
            pl.BlockSpec((tm, D), lambda i: (i, COL_GA // D)),
            pl.BlockSpec((tm, D), lambda i: (i, 0)),
            pl.BlockSpec((M_WIDTH, D), lambda i: (0, 0), pipeline_mode=const),
            pl.BlockSpec((A_WIDTH, D), lambda i: (0, 0), pipeline_mode=const),
            pl.BlockSpec((D, D), lambda i: (0, 0), pipeline_mode=const),
            pl.BlockSpec((1, D), lambda i: (0, 0)),
        ],
        out_specs=pl.BlockSpec((tm, D), lambda i: (i, 0)),
        out_shape=jax.ShapeDtypeStruct((S, D), F32),
        compiler_params=_cparams(("parallel",)),
        name="merge",
    )(hm, ha, P, P, x, wm, wa, wo, gain)


def _ffn_kernel(x_ref, gpre_ref, wu_ref, wd_ref, gpost_ref, out_ref, hn_ref, acc_ref):
    f = pl.program_id(1)

    @pl.when(f == 0)
    def _():
        hn_ref[...] = _rms(x_ref[...], gpre_ref[...]).astype(BF16)
        acc_ref[...] = jnp.zeros_like(acc_ref)

    u = jnp.dot(hn_ref[...], wu_ref[...], preferred_element_type=F32)
    u = jnp.square(jnp.maximum(u, 0.0)).astype(BF16)
    acc_ref[...] += jnp.dot(u, wd_ref[...], preferred_element_type=F32)

    @pl.when(f == pl.num_programs(1) - 1)
    def _():
        out_ref[...] = x_ref[...] + _rms(acc_ref[...], gpost_ref[...])


def _ffn(x, gpre, wu, wd, gpost, tm=512, tf=1024):
    S, D = x.shape
    Fd = wu.shape[1]
    return pl.pallas_call(
        _ffn_kernel,
        grid=(S // tm, Fd // tf),
        in_specs=[
            pl.BlockSpec((tm, D), lambda i, f: (i, 0)),
            pl.BlockSpec((1, D), lambda i, f: (0, 0)),
            pl.BlockSpec((D, tf), lambda i, f: (0, f)),
            pl.BlockSpec((tf, D), lambda i, f: (f, 0)),
            pl.BlockSpec((1, D), lambda i, f: (0, 0)),
        ],
        out_specs=pl.BlockSpec((tm, D), lambda i, f: (i, 0)),
        out_shape=jax.ShapeDtypeStruct((S, D), F32),
        scratch_shapes=[pltpu.VMEM((tm, D), BF16), pltpu.VMEM((tm, D), F32)],
        compiler_params=_cparams(("parallel", "arbitrary")),
        name="ffn",
    )(x, gpre, wu, wd, gpost)


def _layer(x, pos, norm_mix_pre, w_in, conv_w, conv_b, i_bias, f_bias, mlstm_norm,
           w_branch_m, w_branch_a, w_out, norm_mix_post, norm_ffn_pre, w_up, w_down,
           norm_ffn_post):
    S, D = x.shape
    o = 0
    pieces = {}
    w_in = w_in.astype(BF16)
    for name, width in (("mq", M_WIDTH), ("mk", M_WIDTH), ("mv", M_WIDTH), ("mo", M_WIDTH),
                        ("mi", M_HEADS), ("mf", M_HEADS), ("aq", A_WIDTH), ("ak", A_WIDTH),
                        ("av", A_WIDTH), ("gm", D), ("ga", D)):
        pieces[name] = w_in[:, o:o + width]
        o += width
    w_all = jnp.concatenate([pieces[n] for n in ("gm", "ga", "mq", "mk", "aq", "ak", "mv", "mo", "av")],
                            axis=1)
    w_gate = jnp.concatenate(
        [pieces["mi"], pieces["mf"], jnp.zeros((D, LANES - 2 * M_HEADS), BF16)], axis=1)
    gbias = jnp.concatenate([i_bias, f_bias, jnp.zeros((LANES - 2 * M_HEADS,), F32)])[None, :]
    half = jnp.arange(0, A_HEAD_DIM, 2, dtype=F32) / A_HEAD_DIM
    inv_freq = 1.0 / (ROPE_THETA ** half)
    invf = jnp.concatenate([inv_freq, inv_freq])[None, :]

    P, gate = _proj(x, norm_mix_pre[None, :], w_all, w_gate)
    mq, mkT, aq, ak, avT, kmean, gc, gt = _prep(
        P, gate, pos.reshape(S, 1), invf, conv_w, conv_b[None, :], gbias)
    hm = _mlstm(mq, mkT, P, gc, gt, mlstm_norm[None, :])
    ha = _moba(aq, ak, avT, kmean.reshape(S // MOBA_BLOCK, A_WIDTH))
    x1 = _merge(hm, ha, P, x, w_branch_m.astype(BF16), w_branch_a.astype(BF16),
                w_out.astype(BF16), norm_mix_post[None, :])
    return _ffn(x1, norm_ffn_pre[None, :], w_up.astype(BF16), w_down.astype(BF16),
                norm_ffn_post[None, :])


def kernel(x, positions, norm_mix_pre, w_in, conv_w, conv_b, i_bias, f_bias, mlstm_norm,
           w_branch_m, w_branch_a, w_out, norm_mix_post, norm_ffn_pre, w_up, w_down,
           norm_ffn_post):
    B = x.shape[0]
    depth = w_in.shape[0]
    outs = []
    for b in range(B):
        xb = x[b]
        for l in range(depth):
            xb = _layer(xb, positions[b], norm_mix_pre[l], w_in[l], conv_w[l], conv_b[l],
                        i_bias[l], f_bias[l], mlstm_norm[l], w_branch_m[l], w_branch_a[l],
                        w_out[l], norm_mix_post[l], norm_ffn_pre[l], w_up[l], w_down[l],
                        norm_ffn_post[l])
        outs.append(xb)
    return outs[0][None] if B == 1 else jnp.stack(outs, axis=0)
```

```python
import functools

import jax
import jax.numpy as jnp
from jax import lax
from jax.experimental import pallas as pl
from jax.experimental.pallas import tpu as pltpu
from jax.experimental.pallas import tpu_sc as plsc

F32 = jnp.float32
BF16 = jnp.bfloat16

M_HEADS = 4
M_HEAD_DIM = 256
M_WIDTH = M_HEADS * M_HEAD_DIM
M_CHUNK = 128
CONV_WIDTH = 4
A_HEADS = 8
A_HEAD_DIM = 128
A_WIDTH = A_HEADS * A_HEAD_DIM
MOBA_BLOCK = 256
MOBA_TOPK = 3
ROPE_THETA = 10000.0
NORM_EPS = 1e-6

LANES = 128
SUBLANES = 8
VMEM_LIMIT = 56 * 1024 * 1024
NEG_BIG = -1e30
LOG2_E = 1.4426950408889634

COL_GM = 0
COL_GA = 2048
COL_MQK = 4096
COL_AQK = 6144
COL_MV = 8192
COL_MO = 9216
COL_AV = 10240
P_COLS = 11264


def _cparams(sem):
    return pltpu.CompilerParams(dimension_semantics=sem, vmem_limit_bytes=VMEM_LIMIT)


def _rms(x, gain):
    ms = jnp.mean(x * x, axis=-1, keepdims=True)
    return x * lax.rsqrt(ms + NORM_EPS) * gain


def _sigmoid(x):
    return 1.0 / (1.0 + jnp.exp(-x))


def _split3(x):
    hi = x.astype(BF16)
    r1 = x - hi.astype(F32)
    mid = r1.astype(BF16)
    lo = (r1 - mid.astype(F32)).astype(BF16)
    return hi, mid, lo


def _proj_kernel(x_ref, g_ref, w_ref, wg_ref, p_ref, gate_ref, xn_ref):
    @pl.when(pl.program_id(1) == 0)
    def _():
        xn = _rms(x_ref[...], g_ref[...]).astype(BF16)
        xn_ref[...] = xn
        gate_ref[...] = jnp.dot(xn, wg_ref[...], preferred_element_type=F32)

    p_ref[...] = jnp.dot(xn_ref[...], w_ref[...], preferred_element_type=F32).astype(BF16)


def _proj(x, gain, w_all, w_gate, tm=1024, tn=1024):
    S, D = x.shape
    N = w_all.shape[1]
    return pl.pallas_call(
        _proj_kernel,
        grid=(S // tm, N // tn),
        in_specs=[
            pl.BlockSpec((tm, D), lambda i, j: (i, 0)),
            pl.BlockSpec((1, D), lambda i, j: (0, 0)),
            pl.BlockSpec((D, tn), lambda i, j: (0, j)),
            pl.BlockSpec((D, LANES), lambda i, j: (0, 0)),
        ],
        out_specs=[
            pl.BlockSpec((tm, tn), lambda i, j: (i, j)),
            pl.BlockSpec((tm, LANES), lambda i, j: (i, 0)),
        ],
        out_shape=[
            jax.ShapeDtypeStruct((S, N), BF16),
            jax.ShapeDtypeStruct((S, LANES), F32),
        ],
        scratch_shapes=[pltpu.VMEM((tm, D), BF16)],
        compiler_params=_cparams(("parallel", "arbitrary")),
        name="proj",
    )(x, gain, w_all, w_gate)


PREP_ROWS = MOBA_BLOCK
PREP_COLS = 512
HALO_ROWS = 2 * SUBLANES


def _prep_kernel(pmk_ref, halo_ref, paqk_ref, pav_ref, gate_ref, pos_ref, invf_ref,
                 cw_ref, cb_ref, gb_ref,
                 mq_ref, mkT_ref, aq_ref, ak_ref, avT_ref, kmean_ref, gc_ref, gt_ref):
    i = pl.program_id(0)
    R = PREP_ROWS

    k_scale = M_HEAD_DIM ** -0.5
    for c0 in range(0, 2 * M_WIDTH, PREP_COLS):
        cs = slice(c0, c0 + PREP_COLS)
        prev = halo_ref[:, cs].astype(F32)[HALO_ROWS - SUBLANES:, :]
        prev = jnp.where(i == 0, jnp.zeros_like(prev), prev)
        ext = jnp.concatenate([pmk_ref[:, cs].astype(F32), prev], axis=0)
        acc = cw_ref[0:1, cs] * ext
        for j in range(1, CONV_WIDTH):
            acc = pltpu.roll(acc, 1, 0) + cw_ref[j:j + 1, cs] * ext
        acc = acc[0:R, :] + cb_ref[:, cs]
        y = acc * _sigmoid(acc)
        if c0 < M_WIDTH:
            mq_ref[:, cs] = y.astype(BF16)
        else:
            ks = slice(c0 - M_WIDTH, c0 - M_WIDTH + PREP_COLS)
            mkT_ref[ks, :] = (y * k_scale).T.astype(BF16)

    ang = pos_ref[...].astype(F32) * invf_ref[...]
    cos = jnp.cos(ang)
    lane = lax.broadcasted_iota(jnp.int32, (R, A_HEAD_DIM), 1)
    sin_signed = jnp.where(lane < A_HEAD_DIM // 2, -1.0, 1.0) * jnp.sin(ang)
    for h in range(2 * A_HEADS):
        hs = slice(h * A_HEAD_DIM, (h + 1) * A_HEAD_DIM)
        xh = paqk_ref[:, hs].astype(F32)
        yh = xh * cos + pltpu.roll(xh, A_HEAD_DIM // 2, 1) * sin_signed
        if h < A_HEADS:
            aq_ref[h] = yh
        else:
            ko = slice((h - A_HEADS) * A_HEAD_DIM, (h - A_HEADS + 1) * A_HEAD_DIM)
            ak_ref[:, ko] = yh.astype(BF16)
            kmean_ref[0, :, ko] = jnp.mean(yh, axis=0, keepdims=True)
    for c0 in range(0, A_WIDTH, PREP_COLS):
        avT_ref[c0:c0 + PREP_COLS, :] = pav_ref[:, c0:c0 + PREP_COLS].astype(F32).T.astype(BF16)

    g = gate_ref[...] + gb_ref[...]
    log_f = jnp.minimum(g, 0.0) - jnp.log1p(jnp.exp(-jnp.abs(g)))
    r_i = lax.broadcasted_iota(jnp.int32, (R, R), 0)
    c_i = lax.broadcasted_iota(jnp.int32, (R, R), 1)
    tri = ((r_i >= c_i) & ((r_i // M_CHUNK) == (c_i // M_CHUNK))).astype(BF16)
    hi, mid, lo = _split3(log_f)
    csum = (jnp.dot(tri, hi, preferred_element_type=F32)
            + jnp.dot(tri, mid, preferred_element_type=F32)
            + jnp.dot(tri, lo, preferred_element_type=F32))
    glane = lax.broadcasted_iota(jnp.int32, (R, LANES), 1)
    gc = jnp.where(glane < M_HEADS, g, csum)
    gc_ref[...] = gc
    gt_ref[...] = gc.T[0:SUBLANES, :]


def _prep(P, gate, pos, invf, conv_w, conv_b, gbias):
    S = P.shape[0]
    R = PREP_ROWS
    nb = S // R
    halo_blocks = R // HALO_ROWS
    return pl.pallas_call(
        _prep_kernel,
        grid=(nb,),
        in_specs=[
            pl.BlockSpec((R, 2 * M_WIDTH), lambda i: (i, COL_MQK // (2 * M_WIDTH))),
            pl.BlockSpec((HALO_ROWS, 2 * M_WIDTH),
                         lambda i: (jnp.maximum(i * halo_blocks - 1, 0), COL_MQK // (2 * M_WIDTH))),
            pl.BlockSpec((R, 2 * A_WIDTH), lambda i: (i, COL_AQK // (2 * A_WIDTH))),
            pl.BlockSpec((R, A_WIDTH), lambda i: (i, COL_AV // A_WIDTH)),
            pl.BlockSpec((R, LANES), lambda i: (i, 0)),
            pl.BlockSpec((R, 1), lambda i: (i, 0)),
            pl.BlockSpec((1, A_HEAD_DIM), lambda i: (0, 0)),
            pl.BlockSpec((CONV_WIDTH, 2 * M_WIDTH), lambda i: (0, 0)),
            pl.BlockSpec((1, 2 * M_WIDTH), lambda i: (0, 0)),
            pl.BlockSpec((1, LANES), lambda i: (0, 0)),
        ],
        out_specs=[
            pl.BlockSpec((R, M_WIDTH), lambda i: (i, 0)),
            pl.BlockSpec((M_WIDTH, R), lambda i: (0, i)),
            pl.BlockSpec((A_HEADS, R, A_HEAD_DIM), lambda i: (0, i, 0)),
            pl.BlockSpec((R, A_WIDTH), lambda i: (i, 0)),
            pl.BlockSpec((A_WIDTH, R), lambda i: (0, i)),
            pl.BlockSpec((1, 1, A_WIDTH), lambda i: (i, 0, 0)),
            pl.BlockSpec((R, LANES), lambda i: (i, 0)),
            pl.BlockSpec((SUBLANES, R), lambda i: (0, i)),
        ],
        out_shape=[
            jax.ShapeDtypeStruct((S, M_WIDTH), BF16),
            jax.ShapeDtypeStruct((M_WIDTH, S), BF16),
            jax.ShapeDtypeStruct((A_HEADS, S, A_HEAD_DIM), F32),
            jax.ShapeDtypeStruct((S, A_WIDTH), BF16),
            jax.ShapeDtypeStruct((A_WIDTH, S), BF16),
            jax.ShapeDtypeStruct((nb, 1, A_WIDTH), F32),
            jax.ShapeDtypeStruct((S, LANES), F32),
            jax.ShapeDtypeStruct((SUBLANES, S), F32),
        ],
        compiler_params=_cparams(("parallel",)),
        name="prep",
    )(P, P, P, P, gate, pos, invf, conv_w, conv_b, gbias)


M_AUG = M_HEAD_DIM + LANES


def _mlstm_kernel(q_ref, kT_ref, v_ref, mo_ref, gc_ref, gt_ref, gain_ref, out_ref,
                  c_ref, m_ref):
    @pl.when(pl.program_id(0) == 0)
    def _():
        c_ref[...] = jnp.zeros_like(c_ref)
        m_ref[...] = jnp.zeros_like(m_ref)

    L = M_CHUNK
    D = M_HEAD_DIM
    row = lax.broadcasted_iota(jnp.int32, (L, L), 0)
    col = lax.broadcasted_iota(jnp.int32, (L, L), 1)
    causal = row >= col
    ones_col = (lax.broadcasted_iota(jnp.int32, (L, LANES), 1) == 0).astype(BF16)

    heads = range(M_HEADS)
    hsl = [slice(h * D, (h + 1) * D) for h in heads]
    q = [q_ref[:, hsl[h]] for h in heads]
    kT = [kT_ref[hsl[h], :] for h in heads]
    v_aug = [jnp.concatenate([v_ref[:, hsl[h]], ones_col], axis=1) for h in heads]
    b_c = [gc_ref[:, M_HEADS + h:M_HEADS + h + 1] for h in heads]
    b_r = [gt_ref[M_HEADS + h:M_HEADS + h + 1, :] for h in heads]
    u_r = [gt_ref[h:h + 1, :] - b_r[h] for h in heads]
    f_tot = [b_r[h][:, L - 1:L] for h in heads]
    m_prev = [m_ref[h:h + 1, 0:1] for h in heads]

    s_qk = [jnp.dot(q[h], kT[h], preferred_element_type=F32) for h in heads]
    q_c = [jnp.dot(q[h], c_ref[h].astype(BF16), preferred_element_type=F32) for h in heads]

    for h in heads:
        w_r = f_tot[h] + u_r[h]
        m_loc = jnp.max(w_r, axis=1, keepdims=True)
        m_new = jnp.maximum(f_tot[h] + m_prev[h], m_loc)
        a = jnp.exp(f_tot[h] + m_prev[h] - m_new)
        e_r = jnp.exp(w_r - m_new)
        keT = (kT[h].astype(F32) * e_r).astype(BF16)
        c_ref[h] = a * c_ref[h] + jnp.dot(keT, v_aug[h], preferred_element_type=F32)
        m_ref[h:h + 1, :] = jnp.broadcast_to(m_new, (1, LANES))

    for h in heads:
        d_log = jnp.where(causal, b_c[h] + u_r[h], -jnp.inf)
        a_log = b_c[h] + m_prev[h]
        m_t = jnp.maximum(a_log, jnp.max(d_log, axis=1, keepdims=True))
        s_ts = s_qk[h] * jnp.exp(d_log - m_t)
        inter = jnp.exp(a_log - m_t)
        r = inter * q_c[h] + jnp.dot(s_ts.astype(BF16), v_aug[h], preferred_element_type=F32)
        num = r[:, :D]
        den = r[:, D:D + 1]
        hh = num / jnp.maximum(jnp.abs(den), jnp.exp(-m_t))
        hn = hh * lax.rsqrt(jnp.mean(hh * hh, axis=-1, keepdims=True) + NORM_EPS)
        out_ref[:, hsl[h]] = (hn * gain_ref[:, hsl[h]]
                              * _sigmoid(mo_ref[:, hsl[h]].astype(F32))).astype(BF16)


def _mlstm(mq, mkT, P, gc, gt, gain):
    S = mq.shape[0]
    L = M_CHUNK
    return pl.pallas_call(
        _mlstm_kernel,
        grid=(S // L,),
        in_specs=[
            pl.BlockSpec((L, M_WIDTH), lambda c: (c, 0)),
            pl.BlockSpec((M_WIDTH, L), lambda c: (0, c)),
            pl.BlockSpec((L, M_WIDTH), lambda c: (c, COL_MV // M_WIDTH)),
            pl.BlockSpec((L, M_WIDTH), lambda c: (c, COL_MO // M_WIDTH)),
            pl.BlockSpec((L, LANES), lambda c: (c, 0)),
            pl.BlockSpec((SUBLANES, L), lambda c: (0, c)),
            pl.BlockSpec((1, M_WIDTH), lambda c: (0, 0)),
        ],
        out_specs=pl.BlockSpec((L, M_WIDTH), lambda c: (c, 0)),
        out_shape=jax.ShapeDtypeStruct((S, M_WIDTH), BF16),
        scratch_shapes=[
            pltpu.VMEM((M_HEADS, M_HEAD_DIM, M_AUG), F32),
            pltpu.VMEM((SUBLANES, LANES), F32),
        ],
        compiler_params=_cparams(("arbitrary",)),
        name="mlstm",
    )(mq, mkT, P, P, gc, gt, gain)


MOBA_HEADS_PER_STEP = 4


def _moba_kernel(q_ref, k_ref, vT_ref, km_ref, o_ref, sel_ref, qs_ref, m_ref, l_ref, acc_ref,
                 s0_ref, s1_ref, mb0_ref, mb1_ref):
    i = pl.program_id(1)
    BS = MOBA_BLOCK
    Dh = A_HEAD_DIM
    NB = km_ref.shape[0]
    G = MOBA_HEADS_PER_STEP
    own = pl.multiple_of(i * BS, BS)
    blk = lax.broadcasted_iota(jnp.int32, (NB, BS), 0)
    kpos = lax.broadcasted_iota(jnp.int32, (BS, BS), 0)
    qpos = lax.broadcasted_iota(jnp.int32, (BS, BS), 1)

    ones_rows = jnp.ones((2 * SUBLANES, BS), BF16)

    def stage_scores(start, s_buf, mb_buf, causal):
        for g in range(G):
            hs = slice(g * Dh, (g + 1) * Dh)
            s = jnp.dot(k_ref[pl.ds(start, BS), hs], qs_ref[g], preferred_element_type=F32)
            if causal:
                s = jnp.where(kpos <= qpos, s, NEG_BIG)
            sb = s.astype(BF16)
            s_buf[g] = sb
            mb_buf[g] = jnp.max(sb, axis=0, keepdims=True).astype(F32)

    def stage_apply(start, picked_of, s_buf, mb_buf):
        for g in range(G):
            hs = slice(g * Dh, (g + 1) * Dh)
            picked = picked_of(g)
            m_old = m_ref[g]
            m_new = jnp.maximum(m_old, jnp.where(picked, mb_buf[g], NEG_BIG))
            a_old = jnp.exp2(m_old - m_new)
            shift = jnp.where(picked, m_new, -NEG_BIG)
            p = jnp.exp2(s_buf[g] - shift.astype(BF16))
            v_aug = jnp.concatenate([vT_ref[hs, pl.ds(start, BS)], ones_rows], axis=0)
            r = jnp.dot(v_aug, p, preferred_element_type=F32)
            l_ref[g] = a_old * l_ref[g] + r[Dh:Dh + 1, :]
            acc_ref[g] = a_old * acc_ref[g] + r[:Dh, :]
            m_ref[g] = m_new

    for g in range(G):
        hs = slice(g * Dh, (g + 1) * Dh)
        qT = q_ref[:, hs].T
        km = km_ref[:, hs]
        kh = km.astype(BF16)
        kl = (km - kh.astype(F32)).astype(BF16)
        qh = qT.astype(BF16)
        ql = (qT - qh.astype(F32)).astype(BF16)
        gate = (jnp.dot(kh, qh, preferred_element_type=F32)
                + jnp.dot(kh, ql, preferred_element_type=F32)
                + jnp.dot(kl, qh, preferred_element_type=F32))
        gate = jnp.where(blk < i, gate, -jnp.inf)
        sel = jnp.zeros((NB, BS), F32)
        for r in range(MOBA_TOPK):
            mx = jnp.max(gate, axis=0, keepdims=True)
            idx = jnp.min(jnp.where(gate == mx, blk, NB), axis=0, keepdims=True)
            idx = jnp.where(r < i, idx, -1)
            pick = blk == idx
            sel = jnp.where(pick, 1.0, sel)
            gate = jnp.where(pick, -jnp.inf, gate)
        sel_ref[g] = sel
        qs_ref[g] = (qT * (A_HEAD_DIM ** -0.5 * LOG2_E)).astype(BF16)

    m_ref[...] = jnp.full(m_ref.shape, NEG_BIG, F32)
    l_ref[...] = jnp.zeros(l_ref.shape, F32)
    acc_ref[...] = jnp.zeros(acc_ref.shape, F32)

    def block_start(j):
        return pl.multiple_of(jnp.minimum(j, NB - 1) * BS, BS)

    def picked_past(j):
        return lambda g: sel_ref[g, pl.ds(j, 1), :] > 0.0

    stage_scores(own, s1_ref, mb1_ref, True)
    stage_scores(block_start(0), s0_ref, mb0_ref, False)
    stage_apply(own, lambda g: jnp.full((1, BS), True), s1_ref, mb1_ref)

    def pair(j):
        stage_scores(block_start(j + 1), s1_ref, mb1_ref, False)
        stage_apply(block_start(j), picked_past(j), s0_ref, mb0_ref)
        stage_scores(block_start(j + 2), s0_ref, mb0_ref, False)
        stage_apply(block_start(j + 1), picked_past(j + 1), s1_ref, mb1_ref)

    def quad_body(t, carry):
        pair(4 * t)
        pair(4 * t + 2)
        return carry

    def pair_body(t, carry):
        pair(4 * (i // 4) + 2 * t)
        return carry

    lax.fori_loop(0, i // 4, quad_body, 0)
    lax.fori_loop(0, (i % 4 + 1) // 2, pair_body, 0)

    for g in range(G):
        o_ref[:, g * Dh:(g + 1) * Dh] = (acc_ref[g] / l_ref[g]).T.astype(BF16)


def _moba(aq, ak, avT, kmean):
    S = aq.shape[0]
    BS = MOBA_BLOCK
    NB = S // BS
    G = MOBA_HEADS_PER_STEP
    W = G * A_HEAD_DIM
    once = pl.Buffered(1)
    return pl.pallas_call(
        _moba_kernel,
        grid=(A_HEADS // G, NB),
        in_specs=[
            pl.BlockSpec((BS, W), lambda h, i: (i, h)),
            pl.BlockSpec((S, W), lambda h, i: (0, h), pipeline_mode=once),
            pl.BlockSpec((W, S), lambda h, i: (h, 0), pipeline_mode=once),
            pl.BlockSpec((NB, W), lambda h, i: (0, h)),
        ],
        out_specs=pl.BlockSpec((BS, W), lambda h, i: (i, h)),
        out_shape=jax.ShapeDtypeStruct((S, A_WIDTH), BF16),
        scratch_shapes=[
            pltpu.VMEM((G, NB, BS), F32),
            pltpu.VMEM((G, A_HEAD_DIM, BS), BF16),
            pltpu.VMEM((G, 1, BS), F32),
            pltpu.VMEM((G, 1, BS), F32),
            pltpu.VMEM((G, A_HEAD_DIM, BS), F32),
            pltpu.VMEM((G, BS, BS), BF16),
            pltpu.VMEM((G, BS, BS), BF16),
            pltpu.VMEM((G, 1, BS), F32),
            pltpu.VMEM((G, 1, BS), F32),
        ],
        compiler_params=_cparams(("parallel", "arbitrary")),
        name="moba",
    )(aq, ak, avT, kmean)


ROUTE_TILES = 260
ROUTE_ROWS = ROUTE_TILES * MOBA_BLOCK
GROUP_TILES_PER_STEP = 4
ROW_WORDS = A_HEAD_DIM // 2


def _pack_rows(o_t, lse):
    q = o_t.shape[1]
    hi = pltpu.bitcast(o_t[:ROW_WORDS].astype(BF16).astype(F32), jnp.uint32)
    lo = pltpu.bitcast(o_t[ROW_WORDS:].astype(BF16).astype(F32), jnp.uint32)
    words = pltpu.bitcast(hi | (lo >> 16), F32)
    tail = jnp.concatenate([jnp.broadcast_to(lse, (SUBLANES, q)),
                            jnp.zeros((LANES - ROW_WORDS - SUBLANES, q), F32)], axis=0)
    return jnp.concatenate([words, tail], axis=0).T


def _unpack_rows(rows):
    words = pltpu.bitcast(rows[:, :ROW_WORDS], jnp.uint32)
    first = pltpu.bitcast(words & jnp.uint32(0xFFFF0000), F32)
    second = pltpu.bitcast(words << 16, F32)
    return first, second, rows[:, ROW_WORDS:ROW_WORDS + 1]


def _block_partial(k_blk, v_t_blk, qs, ones_rows, causal_mask=None):
    s = jnp.dot(k_blk, qs, preferred_element_type=F32)
    if causal_mask is not None:
        s = jnp.where(causal_mask, s, NEG_BIG)
    sb = s.astype(BF16)
    m = jnp.max(sb, axis=0, keepdims=True)
    p = jnp.exp2(sb - m)
    r = jnp.dot(jnp.concatenate([v_t_blk, ones_rows], axis=0), p, preferred_element_type=F32)
    dh = v_t_blk.shape[0]
    l = r[dh:dh + 1, :]
    return r[:dh, :] / l, m.astype(F32) + jnp.log2(l)


def _route_kernel(q_ref, k_ref, vT_ref, km_ref, route_ref, cnt_ref, own_ref, cnt_acc):
    i = pl.program_id(1)
    BS = MOBA_BLOCK
    Dh = A_HEAD_DIM
    NB = km_ref.shape[0]
    G = MOBA_HEADS_PER_STEP
    own = pl.multiple_of(i * BS, BS)
    blk = lax.broadcasted_iota(jnp.int32, (NB, BS), 0)
    kpos = lax.broadcasted_iota(jnp.int32, (BS, BS), 0)
    qpos = lax.broadcasted_iota(jnp.int32, (BS, BS), 1)
    earlier = (kpos < qpos).astype(BF16)
    ones_rows = jnp.ones((2 * SUBLANES, BS), BF16)

    @pl.when(i == 0)
    def _():
        cnt_acc[...] = jnp.zeros_like(cnt_acc)

    q_t = [q_ref[g].T for g in range(G)]
    qs = [(q_t[g] * (Dh ** -0.5 * LOG2_E)).astype(BF16) for g in range(G)]
    parts = [_block_partial(k_ref[pl.ds(own, BS), g * Dh:(g + 1) * Dh],
                            vT_ref[g * Dh:(g + 1) * Dh, pl.ds(own, BS)], qs[g], ones_rows,
                            kpos <= qpos) for g in range(G)]

    for g in range(G):
        hs = slice(g * Dh, (g + 1) * Dh)
        km = km_ref[:, hs]
        kh = km.astype(BF16)
        kl = (km - kh.astype(F32)).astype(BF16)
        qh = q_t[g].astype(BF16)
        ql = (q_t[g] - qh.astype(F32)).astype(BF16)
        gate = (jnp.dot(kh, qh, preferred_element_type=F32)
                + jnp.dot(kh, ql, preferred_element_type=F32)
                + jnp.dot(kl, qh, preferred_element_type=F32))
        gate = jnp.where(blk < i, gate, -jnp.inf)
        picks, rows = [], []
        for r in range(MOBA_TOPK):
            mx = jnp.max(gate, axis=0, keepdims=True)
            idx = jnp.min(jnp.where(gate == mx, blk, NB), axis=0, keepdims=True)
            idx = jnp.where(r < i, idx, -1)
            pick = blk == idx
            gate = jnp.where(pick, -jnp.inf, gate)
            picks.append(pick)
            rows.append(idx)
        onehot = sum(p.astype(F32) for p in picks)
        before = cnt_acc[g][:, 0:1] + jnp.dot(onehot.astype(BF16), earlier, preferred_element_type=F32)
        for r in range(MOBA_TOPK):
            rank = jnp.sum(jnp.where(picks[r], before, 0.0), axis=0, keepdims=True)
            rows.append(rank.astype(jnp.int32))
        rows.append(jnp.zeros((SUBLANES - 2 * MOBA_TOPK, BS), jnp.int32))
        route_ref[g] = jnp.concatenate(rows, axis=0)
        cnt_new = cnt_acc[g] + jnp.sum(onehot, axis=1, keepdims=True)
        cnt_acc[g] = cnt_new
        cnt_ref[g] = cnt_new
        own_ref[g] = _pack_rows(*parts[g])


def _route(aq_hm, ak, avT, kmean):
    H, S, Dh = aq_hm.shape
    BS = MOBA_BLOCK
    NB = S // BS
    G = MOBA_HEADS_PER_STEP
    W = G * Dh
    once = pl.Buffered(1)
    return pl.pallas_call(
        _route_kernel,
        grid=(H // G, NB),
        in_specs=[
            pl.BlockSpec((G, BS, Dh), lambda h, i: (h, i, 0)),
            pl.BlockSpec((S, W), lambda h, i: (0, h), pipeline_mode=once),
            pl.BlockSpec((W, S), lambda h, i: (h, 0), pipeline_mode=once),
            pl.BlockSpec((NB, W), lambda h, i: (0, h)),
        ],
        out_specs=[
            pl.BlockSpec((G, SUBLANES, BS), lambda h, i: (h, 0, i)),
            pl.BlockSpec((G, NB, LANES), lambda h, i: (h, 0, 0)),
            pl.BlockSpec((G, BS, LANES), lambda h, i: (h, i, 0)),
        ],
        out_shape=[
            jax.ShapeDtypeStruct((H, SUBLANES, S), jnp.int32),
            jax.ShapeDtypeStruct((H, NB, LANES), F32),
            jax.ShapeDtypeStruct((H, S, LANES), F32),
        ],
        scratch_shapes=[pltpu.VMEM((G, NB, LANES), F32)],
        compiler_params=_cparams(("parallel", "arbitrary")),
        name="moba_route",
    )(aq_hm, ak, avT, kmean)


def _dest_kernel(seg_ref, route_ref, dest_ref):
    h = pl.program_id(0)
    NB = seg_ref.shape[1]
    r = route_ref[0]
    blk = r[0:MOBA_TOPK, :]
    rank = r[MOBA_TOPK:2 * MOBA_TOPK, :]
    base = h * ROUTE_ROWS
    dest = jnp.full(blk.shape, base + ROUTE_ROWS - 1, jnp.int32)
    for j in range(NB):
        dest = jnp.where(blk == j, base + seg_ref[h, j] + rank, dest)
    dest_ref[0] = jnp.concatenate(
        [dest, jnp.zeros((SUBLANES - MOBA_TOPK, dest.shape[1]), jnp.int32)], axis=0)


def _dest(seg_start, route):
    H, _, S = route.shape
    return pl.pallas_call(
        _dest_kernel,
        grid_spec=pltpu.PrefetchScalarGridSpec(
            num_scalar_prefetch=1,
            grid=(H,),
            in_specs=[pl.BlockSpec((1, SUBLANES, S), lambda h, seg: (h, 0, 0))],
            out_specs=pl.BlockSpec((1, SUBLANES, S), lambda h, seg: (h, 0, 0)),
        ),
        out_shape=jax.ShapeDtypeStruct((H, SUBLANES, S), jnp.int32),
        compiler_params=_cparams(("parallel",)),
        name="moba_dest",
    )(seg_start, route)


SC_WINDOW = 128


def _sc_mesh():
    return plsc.VectorSubcoreMesh(core_axis_name="c", subcore_axis_name="s")


def _sc_scatter(rows, idx, n_out):
    N, W = rows.shape
    M = idx.shape[0]
    n_win = N // SC_WINDOW

    @pl.kernel(out_type=jax.ShapeDtypeStruct((n_out, W), rows.dtype), mesh=_sc_mesh())
    def k(x_hbm, i_hbm, o_hbm):
        def body(x_vmem, i_vmem):
            pltpu.sync_copy(x_vmem, o_hbm.at[i_vmem.at[0]])

        pltpu.emit_pipeline(
            body,
            grid=(M // SC_WINDOW,),
            in_specs=[pl.BlockSpec((SC_WINDOW, W), lambda w: (lax.rem(w, n_win), 0)),
                      pl.BlockSpec((1, SC_WINDOW), lambda w: (0, w))],
            out_specs=[],
            core_axis_name=("c", "s"),
            dimension_semantics=(pltpu.PARALLEL,),
        )(x_hbm, i_hbm)

    return k(rows, idx.reshape(1, M))


def _sc_gather(table, idx):
    M = idx.shape[0]
    W = table.shape[1]

    @pl.kernel(out_type=jax.ShapeDtypeStruct((M, W), table.dtype), mesh=_sc_mesh())
    def k(x_hbm, i_hbm, o_hbm):
        def body(i_vmem, o_vmem):
            pltpu.sync_copy(x_hbm.at[i_vmem.at[0]], o_vmem)

        pltpu.emit_pipeline(
            body,
            grid=(M // SC_WINDOW,),
            in_specs=[pl.BlockSpec((1, SC_WINDOW), lambda w: (0, w))],
            out_specs=[pl.BlockSpec((SC_WINDOW, W), lambda w: (w, 0))],
            core_axis_name=("c", "s"),
            dimension_semantics=(pltpu.PARALLEL,),
        )(i_hbm, o_hbm)

    return k(table, idx.reshape(1, M))


def _group_kernel(tile_blk_ref, q_ref, k_ref, vT_ref, out_ref):
    h = pl.program_id(0)
    t = pl.program_id(1)
    BS = MOBA_BLOCK
    Dh = A_HEAD_DIM
    T = GROUP_TILES_PER_STEP
    ones_rows = jnp.ones((2 * SUBLANES, BS), BF16)
    starts = []
    for c in range(T):
        j = jnp.maximum(tile_blk_ref[h * ROUTE_TILES + t * T + c], 0)
        starts.append(pl.multiple_of(j * BS, BS))
    qs = [(q_ref[c * BS:(c + 1) * BS, :].T * (Dh ** -0.5 * LOG2_E)).astype(BF16) for c in range(T)]
    s = [jnp.dot(k_ref[pl.ds(starts[c], BS), :], qs[c], preferred_element_type=F32) for c in range(T)]
    sb = [x.astype(BF16) for x in s]
    m = [jnp.max(x, axis=0, keepdims=True) for x in sb]
    for c in range(T):
        p = jnp.exp2(sb[c] - m[c])
        r = jnp.dot(jnp.concatenate([vT_ref[:, pl.ds(starts[c], BS)], ones_rows], axis=0), p,
                    preferred_element_type=F32)
        l = r[Dh:Dh + 1, :]
        out_ref[c * BS:(c + 1) * BS, :] = _pack_rows(r[:Dh, :] / l, m[c].astype(F32) + jnp.log2(l))


def _group(tile_blk, q_sorted, ak, avT):
    S = ak.shape[0]
    H = A_HEADS
    Dh = A_HEAD_DIM
    T = GROUP_TILES_PER_STEP
    steps = ROUTE_TILES // T
    rows = T * MOBA_BLOCK
    return pl.pallas_call(
        _group_kernel,
        grid_spec=pltpu.PrefetchScalarGridSpec(
            num_scalar_prefetch=1,
            grid=(H, steps),
            in_specs=[
                pl.BlockSpec((rows, LANES), lambda h, t, tb: (h * steps + t, 0)),
                pl.BlockSpec((S, Dh), lambda h, t, tb: (0, h)),
                pl.BlockSpec((Dh, S), lambda h, t, tb: (h, 0)),
            ],
            out_specs=pl.BlockSpec((rows, LANES), lambda h, t, tb: (h * steps + t, 0)),
        ),
        out_shape=jax.ShapeDtypeStruct((H * ROUTE_ROWS, LANES), F32),
        compiler_params=_cparams(("parallel", "arbitrary")),
        name="moba_group",
    )(tile_blk, q_sorted, ak, avT)


def _combine_kernel(got_ref, own_ref, o_ref):
    i = pl.program_id(0)
    H = own_ref.shape[0]
    for h in range(H):
        parts = [(own_ref[h], None)] + [(got_ref[h, r], r) for r in range(MOBA_TOPK)]
        firsts, seconds, lses = [], [], []
        for rows, r in parts:
            a, b, lse = _unpack_rows(rows)
            if r is not None:
                ok = r < i
                a = jnp.where(ok, a, 0.0)
                b = jnp.where(ok, b, 0.0)
                lse = jnp.where(ok, lse, NEG_BIG)
            firsts.append(a)
            seconds.append(b)
            lses.append(lse)
        top = functools.reduce(jnp.maximum, lses)
        w = [jnp.exp2(x - top) for x in lses]
        den = sum(w)
        a = sum(wk * ak for wk, ak in zip(w, firsts)) / den
        b = sum(wk * bk for wk, bk in zip(w, seconds)) / den
        o_ref[:, h * A_HEAD_DIM:(h + 1) * A_HEAD_DIM] = jnp.concatenate([a, b], axis=1).astype(BF16)


def _combine(got, own):
    H, S, _ = own.shape
    BS = MOBA_BLOCK
    return pl.pallas_call(
        _combine_kernel,
        grid=(S // BS,),
        in_specs=[
            pl.BlockSpec((H, MOBA_TOPK, BS, LANES), lambda i: (0, 0, i, 0)),
            pl.BlockSpec((H, BS, LANES), lambda i: (0, i, 0)),
        ],
        out_specs=pl.BlockSpec((BS, A_WIDTH), lambda i: (i, 0)),
        out_shape=jax.ShapeDtypeStruct((S, A_WIDTH), BF16),
        compiler_params=_cparams(("parallel",)),
        name="moba_combine",
    )(got, own)


def _moba_routed(aq_hm, ak, avT, kmean):
    H, S, Dh = aq_hm.shape
    NB = S // MOBA_BLOCK
    route, cnt, own = _route(aq_hm, ak, avT, kmean)
    cnt = cnt[:, :, 0].astype(jnp.int32)
    seg_tiles = (cnt + MOBA_BLOCK - 1) // MOBA_BLOCK
    seg_end = jnp.cumsum(seg_tiles, axis=1)
    seg_start = (seg_end - seg_tiles) * MOBA_BLOCK
    tile_ids = jnp.arange(ROUTE_TILES, dtype=jnp.int32)
    tile_blk = jnp.sum(tile_ids[None, :, None] >= seg_end[:, None, :], axis=2).astype(jnp.int32)
    tile_blk = jnp.where(tile_blk < NB, tile_blk, -1).reshape(H * ROUTE_TILES)
    dest = _dest(seg_start, route)[:, :MOBA_TOPK, :]
    q_rows = aq_hm.reshape(H * S, Dh)
    q_sorted = _sc_scatter(q_rows, dest.transpose(1, 0, 2).reshape(-1), H * ROUTE_ROWS)
    results = _group(tile_blk, q_sorted, ak, avT)
    got = _sc_gather(results, dest.reshape(-1)).reshape(H, MOBA_TOPK, S, LANES)
    return _combine(got, own)


def _merge_kernel(hm_ref, ha_ref, gm_ref, ga_ref, x_ref, wm_ref, wa_ref, wo_ref, gain_ref,
                  out_ref):
    ym = jnp.dot(hm_ref[...], wm_ref[...], preferred_element_type=F32)
    ya = jnp.dot(ha_ref[...], wa_ref[...], preferred_element_type=F32)
    merged = _sigmoid(gm_ref[...].astype(F32)) * ym + _sigmoid(ga_ref[...].astype(F32)) * ya
    mix = jnp.dot(merged.astype(BF16), wo_ref[...], preferred_element_type=F32)
    out_ref[...] = x_ref[...] + _rms(mix, gain_ref[...])


def _merge(hm, ha, P, x, wm, wa, wo, gain, tm=256):
    S, D = x.shape
    const = pl.Buffered(1)
    return pl.pallas_call(
        _merge_kernel,
        grid=(S // tm,),
        in_specs=[
            pl.BlockSpec((tm, M_WIDTH), lambda i: (i, 0)),
            pl.BlockSpec((tm, A_WIDTH), lambda i: (i, 0)),
            pl.BlockSpec((tm, D), lambda i: (i, COL_GM // D)),
            pl.BlockSpec((tm, D), lambda i: (i, COL_GA // D)),
            pl.BlockSpec((tm, D), lambda i: (i, 0)),
            pl.BlockSpec((M_WIDTH, D), lambda i: (0, 0), pipeline_mode=const),
            pl.BlockSpec((A_WIDTH, D), lambda i: (0, 0), pipeline_mode=const),
            pl.BlockSpec((D, D), lambda i: (0, 0), pipeline_mode=const),
            pl.BlockSpec((1, D), lambda i: (0, 0)),
        ],
        out_specs=pl.BlockSpec((tm, D), lambda i: (i, 0)),
        out_shape=jax.ShapeDtypeStruct((S, D), F32),
        compiler_params=_cparams(("parallel",)),
        name="merge",
    )(hm, ha, P, P, x, wm, wa, wo, gain)


def _ffn_kernel(x_ref, gpre_ref, wu_ref, wd_ref, gpost_ref, out_ref, hn_ref, acc_ref):
    f = pl.program_id(1)

    @pl.when(f == 0)
    def _():
        hn_ref[...] = _rms(x_ref[...], gpre_ref[...]).astype(BF16)
        acc_ref[...] = jnp.zeros_like(acc_ref)

    u = jnp.dot(hn_ref[...], wu_ref[...], preferred_element_type=F32)
    u = jnp.square(jnp.maximum(u, 0.0)).astype(BF16)
    acc_ref[...] += jnp.dot(u, wd_ref[...], preferred_element_type=F32)

    @pl.when(f == pl.num_programs(1) - 1)
    def _():
        out_ref[...] = x_ref[...] + _rms(acc_ref[...], gpost_ref[...])


def _ffn(x, gpre, wu, wd, gpost, tm=512, tf=1024):
    S, D = x.shape
    Fd = wu.shape[1]
    return pl.pallas_call(
        _ffn_kernel,
        grid=(S // tm, Fd // tf),
        in_specs=[
            pl.BlockSpec((tm, D), lambda i, f: (i, 0)),
            pl.BlockSpec((1, D), lambda i, f: (0, 0)),
            pl.BlockSpec((D, tf), lambda i, f: (0, f)),
            pl.BlockSpec((tf, D), lambda i, f: (f, 0)),
            pl.BlockSpec((1, D), lambda i, f: (0, 0)),
        ],
        out_specs=pl.BlockSpec((tm, D), lambda i, f: (i, 0)),
        out_shape=jax.ShapeDtypeStruct((S, D), F32),
        scratch_shapes=[pltpu.VMEM((tm, D), BF16), pltpu.VMEM((tm, D), F32)],
        compiler_params=_cparams(("parallel", "arbitrary")),
        name="ffn",
    )(x, gpre, wu, wd, gpost)


def _layer(x, pos, norm_mix_pre, w_in, conv_w, conv_b, i_bias, f_bias, mlstm_norm,
           w_branch_m, w_branch_a, w_out, norm_mix_post, norm_ffn_pre, w_up, w_down,
           norm_ffn_post):
    S, D = x.shape
    o = 0
    pieces = {}
    w_in = w_in.astype(BF16)
    for name, width in (("mq", M_WIDTH), ("mk", M_WIDTH), ("mv", M_WIDTH), ("mo", M_WIDTH),
                        ("mi", M_HEADS), ("mf", M_HEADS), ("aq", A_WIDTH), ("ak", A_WIDTH),
                        ("av", A_WIDTH), ("gm", D), ("ga", D)):
        pieces[name] = w_in[:, o:o + width]
        o += width
    w_all = jnp.concatenate([pieces[n] for n in ("gm", "ga", "mq", "mk", "aq", "ak", "mv", "mo", "av")],
                            axis=1)
    w_gate = jnp.concatenate(
        [pieces["mi"], pieces["mf"], jnp.zeros((D, LANES - 2 * M_HEADS), BF16)], axis=1)
    gbias = jnp.concatenate([i_bias, f_bias, jnp.zeros((LANES - 2 * M_HEADS,), F32)])[None, :]
    half = jnp.arange(0, A_HEAD_DIM, 2, dtype=F32) / A_HEAD_DIM
    inv_freq = 1.0 / (ROPE_THETA ** half)
    invf = jnp.concatenate([inv_freq, inv_freq])[None, :]

    P, gate = _proj(x, norm_mix_pre[None, :], w_all, w_gate)
    mq, mkT, aq, ak, avT, kmean, gc, gt = _prep(
        P, gate, pos.reshape(S, 1), invf, conv_w, conv_b[None, :], gbias)
    hm = _mlstm(mq, mkT, P, gc, gt, mlstm_norm[None, :])
    ha = _moba_routed(aq, ak, avT, kmean.reshape(S // MOBA_BLOCK, A_WIDTH))
    x1 = _merge(hm, ha, P, x, w_branch_m.astype(BF16), w_branch_a.astype(BF16),
                w_out.astype(BF16), norm_mix_post[None, :])
    return _ffn(x1, norm_ffn_pre[None, :], w_up.astype(BF16), w_down.astype(BF16),
                norm_ffn_post[None, :])


def kernel(x, positions, norm_mix_pre, w_in, conv_w, conv_b, i_bias, f_bias, mlstm_norm,
           w_branch_m, w_branch_a, w_out, norm_mix_post, norm_ffn_pre, w_up, w_down,
           norm_ffn_post):
    B = x.shape[0]
    depth = w_in.shape[0]
    outs = []
    for b in range(B):
        xb = x[b]
        for l in range(depth):
            xb = _layer(xb, positions[b], norm_mix_pre[l], w_in[l], conv_w[l], conv_b[l],
                        i_bias[l], f_bias[l], mlstm_norm[l], w_branch_m[l], w_branch_a[l],
                        w_out[l], norm_mix_post[l], norm_ffn_pre[l], w_up[l], w_down[l],
                        norm_ffn_post[l])
        outs.append(xb)
    return outs[0][None] if B == 1 else jnp.stack(outs, axis=0)
```

```python
import functools

import jax
import jax.numpy as jnp
from jax import lax
from jax.experimental import pallas as pl
from jax.experimental.pallas import tpu as pltpu
from jax.experimental.pallas import tpu_sc as plsc

F32 = jnp.float32
BF16 = jnp.bfloat16

M_HEADS = 4
M_HEAD_DIM = 256
M_WIDTH = M_HEADS * M_HEAD_DIM
M_CHUNK = 128
CONV_WIDTH = 4
A_HEADS = 8
A_HEAD_DIM = 128
A_WIDTH = A_HEADS * A_HEAD_DIM
MOBA_BLOCK = 256
MOBA_TOPK = 3
ROPE_THETA = 10000.0
NORM_EPS = 1e-6

LANES = 128
SUBLANES = 8
VMEM_LIMIT = 56 * 1024 * 1024
NEG_BIG = -1e30
LOG2_E = 1.4426950408889634

COL_GM = 0
COL_GA = 2048
COL_MQK = 4096
COL_AQK = 6144
COL_MV = 8192
COL_MO = 9216
COL_AV = 10240
P_COLS = 11264


def _cparams(sem):
    return pltpu.CompilerParams(dimension_semantics=sem, vmem_limit_bytes=VMEM_LIMIT)


def _rms(x, gain):
    ms = jnp.mean(x * x, axis=-1, keepdims=True)
    return x * lax.rsqrt(ms + NORM_EPS) * gain


def _sigmoid(x):
    return 1.0 / (1.0 + jnp.exp(-x))


def _split3(x):
    hi = x.astype(BF16)
    r1 = x - hi.astype(F32)
    mid = r1.astype(BF16)
    lo = (r1 - mid.astype(F32)).astype(BF16)
    return hi, mid, lo


def _proj_kernel(x_ref, g_ref, w_ref, wg_ref, p_ref, gate_ref, xn_ref):
    @pl.when(pl.program_id(1) == 0)
    def _():
        xn = _rms(x_ref[...], g_ref[...]).astype(BF16)
        xn_ref[...] = xn
        gate_ref[...] = jnp.dot(xn, wg_ref[...], preferred_element_type=F32)

    p_ref[...] = jnp.dot(xn_ref[...], w_ref[...], preferred_element_type=F32).astype(BF16)


def _proj(x, gain, w_all, w_gate, tm=1024, tn=1024):
    S, D = x.shape
    N = w_all.shape[1]
    return pl.pallas_call(
        _proj_kernel,
        grid=(S // tm, N // tn),
        in_specs=[
            pl.BlockSpec((tm, D), lambda i, j: (i, 0)),
            pl.BlockSpec((1, D), lambda i, j: (0, 0)),
            pl.BlockSpec((D, tn), lambda i, j: (0, j)),
            pl.BlockSpec((D, LANES), lambda i, j: (0, 0)),
        ],
        out_specs=[
            pl.BlockSpec((tm, tn), lambda i, j: (i, j)),
            pl.BlockSpec((tm, LANES), lambda i, j: (i, 0)),
        ],
        out_shape=[
            jax.ShapeDtypeStruct((S, N), BF16),
            jax.ShapeDtypeStruct((S, LANES), F32),
        ],
        scratch_shapes=[pltpu.VMEM((tm, D), BF16)],
        compiler_params=_cparams(("parallel", "arbitrary")),
        name="proj",
    )(x, gain, w_all, w_gate)


PREP_ROWS = MOBA_BLOCK
PREP_COLS = 512
HALO_ROWS = 2 * SUBLANES


def _prep_kernel(pmk_ref, halo_ref, paqk_ref, pav_ref, gate_ref, pos_ref, invf_ref,
                 cw_ref, cb_ref, gb_ref,
                 mq_ref, mkT_ref, aq_ref, ak_ref, avT_ref, kmean_ref, gc_ref, gt_ref):
    i = pl.program_id(0)
    R = PREP_ROWS

    k_scale = M_HEAD_DIM ** -0.5
    for c0 in range(0, 2 * M_WIDTH, PREP_COLS):
        cs = slice(c0, c0 + PREP_COLS)
        prev = halo_ref[:, cs].astype(F32)[HALO_ROWS - SUBLANES:, :]
        prev = jnp.where(i == 0, jnp.zeros_like(prev), prev)
        ext = jnp.concatenate([pmk_ref[:, cs].astype(F32), prev], axis=0)
        acc = cw_ref[0:1, cs] * ext
        for j in range(1, CONV_WIDTH):
            acc = pltpu.roll(acc, 1, 0) + cw_ref[j:j + 1, cs] * ext
        acc = acc[0:R, :] + cb_ref[:, cs]
        y = acc * _sigmoid(acc)
        if c0 < M_WIDTH:
            mq_ref[:, cs] = y.astype(BF16)
        else:
            ks = slice(c0 - M_WIDTH, c0 - M_WIDTH + PREP_COLS)
            mkT_ref[ks, :] = (y * k_scale).T.astype(BF16)

    ang = pos_ref[...].astype(F32) * invf_ref[...]
    cos = jnp.cos(ang)
    lane = lax.broadcasted_iota(jnp.int32, (R, A_HEAD_DIM), 1)
    sin_signed = jnp.where(lane < A_HEAD_DIM // 2, -1.0, 1.0) * jnp.sin(ang)
    for h in range(2 * A_HEADS):
        hs = slice(h * A_HEAD_DIM, (h + 1) * A_HEAD_DIM)
        xh = paqk_ref[:, hs].astype(F32)
        yh = xh * cos + pltpu.roll(xh, A_HEAD_DIM // 2, 1) * sin_signed
        if h < A_HEADS:
            aq_ref[h] = yh
        else:
            ko = slice((h - A_HEADS) * A_HEAD_DIM, (h - A_HEADS + 1) * A_HEAD_DIM)
            ak_ref[:, ko] = yh.astype(BF16)
            kmean_ref[0, :, ko] = jnp.mean(yh, axis=0, keepdims=True)
    for c0 in range(0, A_WIDTH, PREP_COLS):
        avT_ref[c0:c0 + PREP_COLS, :] = pav_ref[:, c0:c0 + PREP_COLS].astype(F32).T.astype(BF16)

    g = gate_ref[...] + gb_ref[...]
    log_f = jnp.minimum(g, 0.0) - jnp.log1p(jnp.exp(-jnp.abs(g)))
    r_i = lax.broadcasted_iota(jnp.int32, (R, R), 0)
    c_i = lax.broadcasted_iota(jnp.int32, (R, R), 1)
    tri = ((r_i >= c_i) & ((r_i // M_CHUNK) == (c_i // M_CHUNK))).astype(BF16)
    hi, mid, lo = _split3(log_f)
    csum = (jnp.dot(tri, hi, preferred_element_type=F32)
            + jnp.dot(tri, mid, preferred_element_type=F32)
            + jnp.dot(tri, lo, preferred_element_type=F32))
    glane = lax.broadcasted_iota(jnp.int32, (R, LANES), 1)
    gc = jnp.where(glane < M_HEADS, g, csum)
    gc_ref[...] = gc
    gt_ref[...] = gc.T[0:SUBLANES, :]


def _prep(P, gate, pos, invf, conv_w, conv_b, gbias):
    S = P.shape[0]
    R = PREP_ROWS
    nb = S // R
    halo_blocks = R // HALO_ROWS
    return pl.pallas_call(
        _prep_kernel,
        grid=(nb,),
        in_specs=[
            pl.BlockSpec((R, 2 * M_WIDTH), lambda i: (i, COL_MQK // (2 * M_WIDTH))),
            pl.BlockSpec((HALO_ROWS, 2 * M_WIDTH),
                         lambda i: (jnp.maximum(i * halo_blocks - 1, 0), COL_MQK // (2 * M_WIDTH))),
            pl.BlockSpec((R, 2 * A_WIDTH), lambda i: (i, COL_AQK // (2 * A_WIDTH))),
            pl.BlockSpec((R, A_WIDTH), lambda i: (i, COL_AV // A_WIDTH)),
            pl.BlockSpec((R, LANES), lambda i: (i, 0)),
            pl.BlockSpec((R, 1), lambda i: (i, 0)),
            pl.BlockSpec((1, A_HEAD_DIM), lambda i: (0, 0)),
            pl.BlockSpec((CONV_WIDTH, 2 * M_WIDTH), lambda i: (0, 0)),
            pl.BlockSpec((1, 2 * M_WIDTH), lambda i: (0, 0)),
            pl.BlockSpec((1, LANES), lambda i: (0, 0)),
        ],
        out_specs=[
            pl.BlockSpec((R, M_WIDTH), lambda i: (i, 0)),
            pl.BlockSpec((M_WIDTH, R), lambda i: (0, i)),
            pl.BlockSpec((A_HEADS, R, A_HEAD_DIM), lambda i: (0, i, 0)),
            pl.BlockSpec((R, A_WIDTH), lambda i: (i, 0)),
            pl.BlockSpec((A_WIDTH, R), lambda i: (0, i)),
            pl.BlockSpec((1, 1, A_WIDTH), lambda i: (i, 0, 0)),
            pl.BlockSpec((R, LANES), lambda i: (i, 0)),
            pl.BlockSpec((SUBLANES, R), lambda i: (0, i)),
        ],
        out_shape=[
            jax.ShapeDtypeStruct((S, M_WIDTH), BF16),
            jax.ShapeDtypeStruct((M_WIDTH, S), BF16),
            jax.ShapeDtypeStruct((A_HEADS, S, A_HEAD_DIM), F32),
            jax.ShapeDtypeStruct((S, A_WIDTH), BF16),
            jax.ShapeDtypeStruct((A_WIDTH, S), BF16),
            jax.ShapeDtypeStruct((nb, 1, A_WIDTH), F32),
            jax.ShapeDtypeStruct((S, LANES), F32),
            jax.ShapeDtypeStruct((SUBLANES, S), F32),
        ],
        compiler_params=_cparams(("parallel",)),
        name="prep",
    )(P, P, P, P, gate, pos, invf, conv_w, conv_b, gbias)


M_AUG = M_HEAD_DIM + LANES


def _mlstm_kernel(q_ref, kT_ref, v_ref, mo_ref, gc_ref, gt_ref, gain_ref, out_ref,
                  c_ref, m_ref):
    @pl.when(pl.program_id(0) == 0)
    def _():
        c_ref[...] = jnp.zeros_like(c_ref)
        m_ref[...] = jnp.zeros_like(m_ref)

    L = M_CHUNK
    D = M_HEAD_DIM
    row = lax.broadcasted_iota(jnp.int32, (L, L), 0)
    col = lax.broadcasted_iota(jnp.int32, (L, L), 1)
    causal = row >= col
    ones_col = (lax.broadcasted_iota(jnp.int32, (L, LANES), 1) == 0).astype(BF16)

    heads = range(M_HEADS)
    hsl = [slice(h * D, (h + 1) * D) for h in heads]
    q = [q_ref[:, hsl[h]] for h in heads]
    kT = [kT_ref[hsl[h], :] for h in heads]
    v_aug = [jnp.concatenate([v_ref[:, hsl[h]], ones_col], axis=1) for h in heads]
    b_c = [gc_ref[:, M_HEADS + h:M_HEADS + h + 1] for h in heads]
    b_r = [gt_ref[M_HEADS + h:M_HEADS + h + 1, :] for h in heads]
    u_r = [gt_ref[h:h + 1, :] - b_r[h] for h in heads]
    f_tot = [b_r[h][:, L - 1:L] for h in heads]
    m_prev = [m_ref[h:h + 1, 0:1] for h in heads]

    s_qk = [jnp.dot(q[h], kT[h], preferred_element_type=F32) for h in heads]
    q_c = [jnp.dot(q[h], c_ref[h].astype(BF16), preferred_element_type=F32) for h in heads]

    for h in heads:
        w_r = f_tot[h] + u_r[h]
        m_loc = jnp.max(w_r, axis=1, keepdims=True)
        m_new = jnp.maximum(f_tot[h] + m_prev[h], m_loc)
        a = jnp.exp(f_tot[h] + m_prev[h] - m_new)
        e_r = jnp.exp(w_r - m_new)
        keT = (kT[h].astype(F32) * e_r).astype(BF16)
        c_ref[h] = a * c_ref[h] + jnp.dot(keT, v_aug[h], preferred_element_type=F32)
        m_ref[h:h + 1, :] = jnp.broadcast_to(m_new, (1, LANES))

    for h in heads:
        d_log = jnp.where(causal, b_c[h] + u_r[h], -jnp.inf)
        a_log = b_c[h] + m_prev[h]
        m_t = jnp.maximum(a_log, jnp.max(d_log, axis=1, keepdims=True))
        s_ts = s_qk[h] * jnp.exp(d_log - m_t)
        inter = jnp.exp(a_log - m_t)
        r = inter * q_c[h] + jnp.dot(s_ts.astype(BF16), v_aug[h], preferred_element_type=F32)
        num = r[:, :D]
        den = r[:, D:D + 1]
        hh = num / jnp.maximum(jnp.abs(den), jnp.exp(-m_t))
        hn = hh * lax.rsqrt(jnp.mean(hh * hh, axis=-1, keepdims=True) + NORM_EPS)
        out_ref[:, hsl[h]] = (hn * gain_ref[:, hsl[h]]
                              * _sigmoid(mo_ref[:, hsl[h]].astype(F32))).astype(BF16)


def _mlstm(mq, mkT, P, gc, gt, gain):
    S = mq.shape[0]
    L = M_CHUNK
    return pl.pallas_call(
        _mlstm_kernel,
        grid=(S // L,),
        in_specs=[
            pl.BlockSpec((L, M_WIDTH), lambda c: (c, 0)),
            pl.BlockSpec((M_WIDTH, L), lambda c: (0, c)),
            pl.BlockSpec((L, M_WIDTH), lambda c: (c, COL_MV // M_WIDTH)),
            pl.BlockSpec((L, M_WIDTH), lambda c: (c, COL_MO // M_WIDTH)),
            pl.BlockSpec((L, LANES), lambda c: (c, 0)),
            pl.BlockSpec((SUBLANES, L), lambda c: (0, c)),
            pl.BlockSpec((1, M_WIDTH), lambda c: (0, 0)),
        ],
        out_specs=pl.BlockSpec((L, M_WIDTH), lambda c: (c, 0)),
        out_shape=jax.ShapeDtypeStruct((S, M_WIDTH), BF16),
        scratch_shapes=[
            pltpu.VMEM((M_HEADS, M_HEAD_DIM, M_AUG), F32),
            pltpu.VMEM((SUBLANES, LANES), F32),
        ],
        compiler_params=_cparams(("arbitrary",)),
        name="mlstm",
    )(mq, mkT, P, P, gc, gt, gain)


MOBA_HEADS_PER_STEP = 4


def _moba_kernel(q_ref, k_ref, vT_ref, km_ref, o_ref, sel_ref, qs_ref, m_ref, l_ref, acc_ref,
                 s0_ref, s1_ref, mb0_ref, mb1_ref):
    i = pl.program_id(1)
    BS = MOBA_BLOCK
    Dh = A_HEAD_DIM
    NB = km_ref.shape[0]
    G = MOBA_HEADS_PER_STEP
    own = pl.multiple_of(i * BS, BS)
    blk = lax.broadcasted_iota(jnp.int32, (NB, BS), 0)
    kpos = lax.broadcasted_iota(jnp.int32, (BS, BS), 0)
    qpos = lax.broadcasted_iota(jnp.int32, (BS, BS), 1)

    ones_rows = jnp.ones((2 * SUBLANES, BS), BF16)

    def stage_scores(start, s_buf, mb_buf, causal):
        for g in range(G):
            hs = slice(g * Dh, (g + 1) * Dh)
            s = jnp.dot(k_ref[pl.ds(start, BS), hs], qs_ref[g], preferred_element_type=F32)
            if causal:
                s = jnp.where(kpos <= qpos, s, NEG_BIG)
            sb = s.astype(BF16)
            s_buf[g] = sb
            mb_buf[g] = jnp.max(sb, axis=0, keepdims=True).astype(F32)

    def stage_apply(start, picked_of, s_buf, mb_buf):
        for g in range(G):
            hs = slice(g * Dh, (g + 1) * Dh)
            picked = picked_of(g)
            m_old = m_ref[g]
            m_new = jnp.maximum(m_old, jnp.where(picked, mb_buf[g], NEG_BIG))
            a_old = jnp.exp2(m_old - m_new)
            shift = jnp.where(picked, m_new, -NEG_BIG)
            p = jnp.exp2(s_buf[g] - shift.astype(BF16))
            v_aug = jnp.concatenate([vT_ref[hs, pl.ds(start, BS)], ones_rows], axis=0)
            r = jnp.dot(v_aug, p, preferred_element_type=F32)
            l_ref[g] = a_old * l_ref[g] + r[Dh:Dh + 1, :]
            acc_ref[g] = a_old * acc_ref[g] + r[:Dh, :]
            m_ref[g] = m_new

    for g in range(G):
        hs = slice(g * Dh, (g + 1) * Dh)
        qT = q_ref[:, hs].T
        km = km_ref[:, hs]
        kh = km.astype(BF16)
        kl = (km - kh.astype(F32)).astype(BF16)
        qh = qT.astype(BF16)
        ql = (qT - qh.astype(F32)).astype(BF16)
        gate = (jnp.dot(kh, qh, preferred_element_type=F32)
                + jnp.dot(kh, ql, preferred_element_type=F32)
                + jnp.dot(kl, qh, preferred_element_type=F32))
        gate = jnp.where(blk < i, gate, -jnp.inf)
        sel = jnp.zeros((NB, BS), F32)
        for r in range(MOBA_TOPK):
            mx = jnp.max(gate, axis=0, keepdims=True)
            idx = jnp.min(jnp.where(gate == mx, blk, NB), axis=0, keepdims=True)
            idx = jnp.where(r < i, idx, -1)
            pick = blk == idx
            sel = jnp.where(pick, 1.0, sel)
            gate = jnp.where(pick, -jnp.inf, gate)
        sel_ref[g] = sel
        qs_ref[g] = (qT * (A_HEAD_DIM ** -0.5 * LOG2_E)).astype(BF16)

    m_ref[...] = jnp.full(m_ref.shape, NEG_BIG, F32)
    l_ref[...] = jnp.zeros(l_ref.shape, F32)
    acc_ref[...] = jnp.zeros(acc_ref.shape, F32)

    def block_start(j):
        return pl.multiple_of(jnp.minimum(j, NB - 1) * BS, BS)

    def picked_past(j):
        return lambda g: sel_ref[g, pl.ds(j, 1), :] > 0.0

    stage_scores(own, s1_ref, mb1_ref, True)
    stage_scores(block_start(0), s0_ref, mb0_ref, False)
    stage_apply(own, lambda g: jnp.full((1, BS), True), s1_ref, mb1_ref)

    def pair(j):
        stage_scores(block_start(j + 1), s1_ref, mb1_ref, False)
        stage_apply(block_start(j), picked_past(j), s0_ref, mb0_ref)
        stage_scores(block_start(j + 2), s0_ref, mb0_ref, False)
        stage_apply(block_start(j + 1), picked_past(j + 1), s1_ref, mb1_ref)

    def quad_body(t, carry):
        pair(4 * t)
        pair(4 * t + 2)
        return carry

    def pair_body(t, carry):
        pair(4 * (i // 4) + 2 * t)
        return carry

    lax.fori_loop(0, i // 4, quad_body, 0)
    lax.fori_loop(0, (i % 4 + 1) // 2, pair_body, 0)

    for g in range(G):
        o_ref[:, g * Dh:(g + 1) * Dh] = (acc_ref[g] / l_ref[g]).T.astype(BF16)


def _moba(aq, ak, avT, kmean):
    S = aq.shape[0]
    BS = MOBA_BLOCK
    NB = S // BS
    G = MOBA_HEADS_PER_STEP
    W = G * A_HEAD_DIM
    once = pl.Buffered(1)
    return pl.pallas_call(
        _moba_kernel,
        grid=(A_HEADS // G, NB),
        in_specs=[
            pl.BlockSpec((BS, W), lambda h, i: (i, h)),
            pl.BlockSpec((S, W), lambda h, i: (0, h), pipeline_mode=once),
            pl.BlockSpec((W, S), lambda h, i: (h, 0), pipeline_mode=once),
            pl.BlockSpec((NB, W), lambda h, i: (0, h)),
        ],
        out_specs=pl.BlockSpec((BS, W), lambda h, i: (i, h)),
        out_shape=jax.ShapeDtypeStruct((S, A_WIDTH), BF16),
        scratch_shapes=[
            pltpu.VMEM((G, NB, BS), F32),
            pltpu.VMEM((G, A_HEAD_DIM, BS), BF16),
            pltpu.VMEM((G, 1, BS), F32),
            pltpu.VMEM((G, 1, BS), F32),
            pltpu.VMEM((G, A_HEAD_DIM, BS), F32),
            pltpu.VMEM((G, BS, BS), BF16),
            pltpu.VMEM((G, BS, BS), BF16),
            pltpu.VMEM((G, 1, BS), F32),
            pltpu.VMEM((G, 1, BS), F32),
        ],
        compiler_params=_cparams(("parallel", "arbitrary")),
        name="moba",
    )(aq, ak, avT, kmean)


ROUTE_TILES = 264
ROUTE_ROWS = ROUTE_TILES * MOBA_BLOCK
GROUP_TILES_PER_STEP = 8
ROW_WORDS = A_HEAD_DIM // 2


def _pack_rows(o_t, lse):
    q = o_t.shape[1]
    hi = pltpu.bitcast(o_t[:ROW_WORDS].astype(BF16).astype(F32), jnp.uint32)
    lo = pltpu.bitcast(o_t[ROW_WORDS:].astype(BF16).astype(F32), jnp.uint32)
    words = pltpu.bitcast(hi | (lo >> 16), F32)
    tail = jnp.concatenate([jnp.broadcast_to(lse, (SUBLANES, q)),
                            jnp.zeros((LANES - ROW_WORDS - SUBLANES, q), F32)], axis=0)
    return jnp.concatenate([words, tail], axis=0).T


def _unpack_rows(rows):
    words = pltpu.bitcast(rows[:, :ROW_WORDS], jnp.uint32)
    first = pltpu.bitcast(words & jnp.uint32(0xFFFF0000), F32)
    second = pltpu.bitcast(words << 16, F32)
    return first, second, rows[:, ROW_WORDS:ROW_WORDS + 1]


def _dot_nt(a, b):
    return lax.dot_general(a, b, (((1,), (1,)), ((), ())), preferred_element_type=F32)


def _block_partial(k_blk, v_t_blk, qs, ones_rows, causal_mask=None):
    s = _dot_nt(k_blk, qs)
    if causal_mask is not None:
        s = jnp.where(causal_mask, s, NEG_BIG)
    sb = s.astype(BF16)
    m = jnp.max(sb, axis=0, keepdims=True)
    p = jnp.exp2(sb - m)
    r = jnp.dot(jnp.concatenate([v_t_blk, ones_rows], axis=0), p, preferred_element_type=F32)
    dh = v_t_blk.shape[0]
    l = r[dh:dh + 1, :]
    return r[:dh, :] / l, m.astype(F32) + jnp.log2(l)


def _route_kernel(q_ref, k_ref, vT_ref, km_ref, route_ref, cnt_ref, own_ref, cnt_acc):
    i = pl.program_id(1)
    BS = MOBA_BLOCK
    Dh = A_HEAD_DIM
    NB = km_ref.shape[0]
    G = MOBA_HEADS_PER_STEP
    own = pl.multiple_of(i * BS, BS)
    blk = lax.broadcasted_iota(jnp.int32, (NB, BS), 0)
    kpos = lax.broadcasted_iota(jnp.int32, (BS, BS), 0)
    qpos = lax.broadcasted_iota(jnp.int32, (BS, BS), 1)
    earlier = (kpos < qpos).astype(BF16)
    ones_rows = jnp.ones((2 * SUBLANES, BS), BF16)

    @pl.when(i == 0)
    def _():
        cnt_acc[...] = jnp.zeros_like(cnt_acc)

    q = [q_ref[g] for g in range(G)]
    qs = [(q[g] * (Dh ** -0.5 * LOG2_E)).astype(BF16) for g in range(G)]
    parts = [_block_partial(k_ref[pl.ds(own, BS), g * Dh:(g + 1) * Dh],
                            vT_ref[g * Dh:(g + 1) * Dh, pl.ds(own, BS)], qs[g], ones_rows,
                            kpos <= qpos) for g in range(G)]

    gates = []
    for g in range(G):
        hs = slice(g * Dh, (g + 1) * Dh)
        km = km_ref[:, hs]
        kh = km.astype(BF16)
        kl = (km - kh.astype(F32)).astype(BF16)
        qh = q[g].astype(BF16)
        ql = (q[g] - qh.astype(F32)).astype(BF16)
        gate = _dot_nt(kh, qh) + _dot_nt(kh, ql) + _dot_nt(kl, qh)
        gates.append(jnp.where(blk < i, gate, -jnp.inf))
    picks = [[] for _ in range(G)]
    rows = [[] for _ in range(G)]
    for r in range(MOBA_TOPK):
        for g in range(G):
            mx = jnp.max(gates[g], axis=0, keepdims=True)
            idx = jnp.min(jnp.where(gates[g] == mx, blk, NB), axis=0, keepdims=True)
            idx = jnp.where(r < i, idx, -1)
            pick = blk == idx
            gates[g] = jnp.where(pick, -jnp.inf, gates[g])
            picks[g].append(pick)
            rows[g].append(idx)
    for g in range(G):
        onehot = sum(p.astype(F32) for p in picks[g])
        before = cnt_acc[g][:, 0:1] + jnp.dot(onehot.astype(BF16), earlier, preferred_element_type=F32)
        for r in range(MOBA_TOPK):
            rank = jnp.sum(jnp.where(picks[g][r], before, 0.0), axis=0, keepdims=True)
            rows[g].append(rank.astype(jnp.int32))
        rows[g].append(jnp.zeros((SUBLANES - 2 * MOBA_TOPK, BS), jnp.int32))
        route_ref[g] = jnp.concatenate(rows[g], axis=0)
        cnt_new = cnt_acc[g] + jnp.sum(onehot, axis=1, keepdims=True)
        cnt_acc[g] = cnt_new
        cnt_ref[g] = cnt_new
        own_ref[g] = _pack_rows(*parts[g])


def _route(aq_hm, ak, avT, kmean):
    H, S, Dh = aq_hm.shape
    BS = MOBA_BLOCK
    NB = S // BS
    G = MOBA_HEADS_PER_STEP
    W = G * Dh
    once = pl.Buffered(1)
    return pl.pallas_call(
        _route_kernel,
        grid=(H // G, NB),
        in_specs=[
            pl.BlockSpec((G, BS, Dh), lambda h, i: (h, i, 0)),
            pl.BlockSpec((S, W), lambda h, i: (0, h), pipeline_mode=once),
            pl.BlockSpec((W, S), lambda h, i: (h, 0), pipeline_mode=once),
            pl.BlockSpec((NB, W), lambda h, i: (0, h)),
        ],
        out_specs=[
            pl.BlockSpec((G, SUBLANES, BS), lambda h, i: (h, 0, i)),
            pl.BlockSpec((G, NB, LANES), lambda h, i: (h, 0, 0)),
            pl.BlockSpec((G, BS, LANES), lambda h, i: (h, i, 0)),
        ],
        out_shape=[
            jax.ShapeDtypeStruct((H, SUBLANES, S), jnp.int32),
            jax.ShapeDtypeStruct((H, NB, LANES), F32),
            jax.ShapeDtypeStruct((H, S, LANES), F32),
        ],
        scratch_shapes=[pltpu.VMEM((G, NB, LANES), F32)],
        compiler_params=_cparams(("parallel", "arbitrary")),
        name="moba_route",
    )(aq_hm, ak, avT, kmean)


def _dest_kernel(seg_ref, route_ref, dest_ref):
    h = pl.program_id(0)
    NB = seg_ref.shape[1]
    r = route_ref[0]
    blk = r[0:MOBA_TOPK, :]
    rank = r[MOBA_TOPK:2 * MOBA_TOPK, :]
    base = h * ROUTE_ROWS
    dest = jnp.full(blk.shape, base + ROUTE_ROWS - 1, jnp.int32)
    for j in range(NB):
        dest = jnp.where(blk == j, base + seg_ref[h, j] + rank, dest)
    dest_ref[0] = jnp.concatenate(
        [dest, jnp.zeros((SUBLANES - MOBA_TOPK, dest.shape[1]), jnp.int32)], axis=0)


def _dest(seg_start, route):
    H, _, S = route.shape
    return pl.pallas_call(
        _dest_kernel,
        grid_spec=pltpu.PrefetchScalarGridSpec(
            num_scalar_prefetch=1,
            grid=(H,),
            in_specs=[pl.BlockSpec((1, SUBLANES, S), lambda h, seg: (h, 0, 0))],
            out_specs=pl.BlockSpec((1, SUBLANES, S), lambda h, seg: (h, 0, 0)),
        ),
        out_shape=jax.ShapeDtypeStruct((H, SUBLANES, S), jnp.int32),
        compiler_params=_cparams(("parallel",)),
        name="moba_dest",
    )(seg_start, route)


SC_WINDOW = 256


def _sc_mesh():
    return plsc.VectorSubcoreMesh(core_axis_name="c", subcore_axis_name="s")


def _sc_scatter(rows, idx, n_out):
    N, W = rows.shape
    M = idx.shape[0]
    n_win = N // SC_WINDOW

    @pl.kernel(out_type=jax.ShapeDtypeStruct((n_out, W), rows.dtype), mesh=_sc_mesh())
    def k(x_hbm, i_hbm, o_hbm):
        def body(x_vmem, i_vmem):
            pltpu.sync_copy(x_vmem, o_hbm.at[i_vmem.at[0]])

        pltpu.emit_pipeline(
            body,
            grid=(M // SC_WINDOW,),
            in_specs=[pl.BlockSpec((SC_WINDOW, W), lambda w: (lax.rem(w, n_win), 0)),
                      pl.BlockSpec((1, SC_WINDOW), lambda w: (0, w))],
            out_specs=[],
            core_axis_name=("c", "s"),
            dimension_semantics=(pltpu.PARALLEL,),
        )(x_hbm, i_hbm)

    return k(rows, idx.reshape(1, M))


def _sc_gather(table, idx):
    M = idx.shape[0]
    W = table.shape[1]

    @pl.kernel(out_type=jax.ShapeDtypeStruct((M, W), table.dtype), mesh=_sc_mesh())
    def k(x_hbm, i_hbm, o_hbm):
        def body(i_vmem, o_vmem):
            pltpu.sync_copy(x_hbm.at[i_vmem.at[0]], o_vmem)

        pltpu.emit_pipeline(
            body,
            grid=(M // SC_WINDOW,),
            in_specs=[pl.BlockSpec((1, SC_WINDOW), lambda w: (0, w))],
            out_specs=[pl.BlockSpec((SC_WINDOW, W), lambda w: (w, 0))],
            core_axis_name=("c", "s"),
            dimension_semantics=(pltpu.PARALLEL,),
        )(i_hbm, o_hbm)

    return k(table, idx.reshape(1, M))


def _group_kernel(tile_blk_ref, q_ref, k_ref, vT_ref, out_ref, s_ref, m_ref):
    h = pl.program_id(0)
    u = pl.program_id(1)
    last = pl.num_programs(1) - 2
    BS = MOBA_BLOCK
    Dh = A_HEAD_DIM
    T = GROUP_TILES_PER_STEP
    ones_rows = jnp.ones((2 * SUBLANES, BS), BF16)
    new = lax.rem(u, 2)
    old = 1 - new

    @pl.when((h == 0) & (u == 0))
    def _():
        s_ref[...] = jnp.zeros_like(s_ref)
        m_ref[...] = jnp.zeros_like(m_ref)

    def block_start(group, c):
        j = jnp.maximum(tile_blk_ref[h * ROUTE_TILES + group * T + c], 0)
        return pl.multiple_of(j * BS, BS)

    g_new = jnp.minimum(u, last)
    g_old = jnp.maximum(u - 1, 0)
    ms, rs = [], []
    for c in range(T):
        m = m_ref[old, c]
        p = jnp.exp2(s_ref[old, c] - m.astype(BF16))
        v_aug = jnp.concatenate([vT_ref[:, pl.ds(block_start(g_old, c), BS)], ones_rows], axis=0)
        ms.append(m)
        rs.append(jnp.dot(v_aug, p, preferred_element_type=F32))
    s_new = []
    for c in range(T):
        qs = (q_ref[c * BS:(c + 1) * BS, :] * (Dh ** -0.5 * LOG2_E)).astype(BF16)
        s_new.append(_dot_nt(k_ref[pl.ds(block_start(g_new, c), BS), :], qs))
    for c in range(T):
        l = rs[c][Dh:Dh + 1, :]
        out_ref[c * BS:(c + 1) * BS, :] = _pack_rows(rs[c][:Dh, :] / l, ms[c] + jnp.log2(l))
        sb = s_new[c].astype(BF16)
        s_ref[new, c] = sb
        m_ref[new, c] = jnp.max(sb, axis=0, keepdims=True).astype(F32)


def _group(tile_blk, q_sorted, ak, avT):
    S = ak.shape[0]
    H = A_HEADS
    Dh = A_HEAD_DIM
    BS = MOBA_BLOCK
    T = GROUP_TILES_PER_STEP
    steps = ROUTE_TILES // T
    rows = T * BS
    return pl.pallas_call(
        _group_kernel,
        grid_spec=pltpu.PrefetchScalarGridSpec(
            num_scalar_prefetch=1,
            grid=(H, steps + 1),
            in_specs=[
                pl.BlockSpec((rows, LANES), lambda h, u, tb: (h * steps + jnp.minimum(u, steps - 1), 0)),
                pl.BlockSpec((S, Dh), lambda h, u, tb: (0, h)),
                pl.BlockSpec((Dh, S), lambda h, u, tb: (h, 0)),
            ],
            out_specs=pl.BlockSpec((rows, LANES), lambda h, u, tb: (h * steps + jnp.maximum(u - 1, 0), 0)),
            scratch_shapes=[pltpu.VMEM((2, T, BS, BS), BF16), pltpu.VMEM((2, T, 1, BS), F32)],
        ),
        out_shape=jax.ShapeDtypeStruct((H * ROUTE_ROWS, LANES), F32),
        compiler_params=_cparams(("arbitrary", "arbitrary")),
        name="moba_group",
    )(tile_blk, q_sorted, ak, avT)


def _combine_kernel(got_ref, own_ref, o_ref):
    i = pl.program_id(0)
    H = own_ref.shape[0]
    BS = own_ref.shape[1]
    low_half = lax.broadcasted_iota(jnp.int32, (BS, LANES), 1) < ROW_WORDS
    for h in range(H):
        tiles = [own_ref[h]]
        for r in range(MOBA_TOPK):
            tiles.append(jnp.where(r < i, got_ref[h, r], 0.0))
        lses = [tiles[0][:, ROW_WORDS:ROW_WORDS + 1]]
        lses += [jnp.where(r < i, tiles[r + 1][:, ROW_WORDS:ROW_WORDS + 1], NEG_BIG)
                 for r in range(MOBA_TOPK)]
        top = functools.reduce(jnp.maximum, lses)
        w = [jnp.exp2(x - top) for x in lses]
        inv = 1.0 / sum(w)
        first = jnp.zeros((BS, LANES), F32)
        second = jnp.zeros((BS, LANES), F32)
        for wk, tile in zip(w, tiles):
            words = pltpu.bitcast(tile, jnp.uint32)
            first = first + wk * pltpu.bitcast(words & jnp.uint32(0xFFFF0000), F32)
            second = second + wk * pltpu.bitcast(words << 16, F32)
        o = jnp.where(low_half, first, pltpu.roll(second, ROW_WORDS, 1)) * inv
        o_ref[:, h * A_HEAD_DIM:(h + 1) * A_HEAD_DIM] = o.astype(BF16)


def _combine(got, own):
    H, S, _ = own.shape
    BS = MOBA_BLOCK
    return pl.pallas_call(
        _combine_kernel,
        grid=(S // BS,),
        in_specs=[
            pl.BlockSpec((H, MOBA_TOPK, BS, LANES), lambda i: (0, 0, i, 0)),
            pl.BlockSpec((H, BS, LANES), lambda i: (0, i, 0)),
        ],
        out_specs=pl.BlockSpec((BS, A_WIDTH), lambda i: (i, 0)),
        out_shape=jax.ShapeDtypeStruct((S, A_WIDTH), BF16),
        compiler_params=_cparams(("parallel",)),
        name="moba_combine",
    )(got, own)


def _moba_routed(aq_hm, ak, avT, kmean):
    H, S, Dh = aq_hm.shape
    NB = S // MOBA_BLOCK
    route, cnt, own = _route(aq_hm, ak, avT, kmean)
    cnt = cnt[:, :, 0].astype(jnp.int32)
    seg_tiles = (cnt + MOBA_BLOCK - 1) // MOBA_BLOCK
    seg_end = jnp.cumsum(seg_tiles, axis=1)
    seg_start = (seg_end - seg_tiles) * MOBA_BLOCK
    tile_ids = jnp.arange(ROUTE_TILES, dtype=jnp.int32)
    tile_blk = jnp.sum(tile_ids[None, :, None] >= seg_end[:, None, :], axis=2).astype(jnp.int32)
    tile_blk = jnp.where(tile_blk < NB, tile_blk, -1).reshape(H * ROUTE_TILES)
    dest = _dest(seg_start, route)[:, :MOBA_TOPK, :]
    q_rows = aq_hm.reshape(H * S, Dh)
    q_sorted = _sc_scatter(q_rows, dest.transpose(1, 0, 2).reshape(-1), H * ROUTE_ROWS)
    results = _group(tile_blk, q_sorted, ak, avT)
    got = _sc_gather(results, dest.reshape(-1)).reshape(H, MOBA_TOPK, S, LANES)
    return _combine(got, own)


def _merge_kernel(hm_ref, ha_ref, gm_ref, ga_ref, x_ref, wm_ref, wa_ref, wo_ref, gain_ref,
                  out_ref):
    ym = jnp.dot(hm_ref[...], wm_ref[...], preferred_element_type=F32)
    ya = jnp.dot(ha_ref[...], wa_ref[...], preferred_element_type=F32)
    merged = _sigmoid(gm_ref[...].astype(F32)) * ym + _sigmoid(ga_ref[...].astype(F32)) * ya
    mix = jnp.dot(merged.astype(BF16), wo_ref[...], preferred_element_type=F32)
    out_ref[...] = x_ref[...] + _rms(mix, gain_ref[...])


def _merge(hm, ha, P, x, wm, wa, wo, gain, tm=256):
    S, D = x.shape
    const = pl.Buffered(1)
    return pl.pallas_call(
        _merge_kernel,
        grid=(S // tm,),
        in_specs=[
            pl.BlockSpec((tm, M_WIDTH), lambda i: (i, 0)),
            pl.BlockSpec((tm, A_WIDTH), lambda i: (i, 0)),
            pl.BlockSpec((tm, D), lambda i: (i, COL_GM // D)),
            pl.BlockSpec((tm, D), lambda i: (i, COL_GA // D)),
            pl.BlockSpec((tm, D), lambda i: (i, 0)),
            pl.BlockSpec((M_WIDTH, D), lambda i: (0, 0), pipeline_mode=const),
            pl.BlockSpec((A_WIDTH, D), lambda i: (0, 0), pipeline_mode=const),
            pl.BlockSpec((D, D), lambda i: (0, 0), pipeline_mode=const),
            pl.BlockSpec((1, D), lambda i: (0, 0)),
        ],
        out_specs=pl.BlockSpec((tm, D), lambda i: (i, 0)),
        out_shape=jax.ShapeDtypeStruct((S, D), F32),
        compiler_params=_cparams(("parallel",)),
        name="merge",
    )(hm, ha, P, P, x, wm, wa, wo, gain)


def _ffn_kernel(x_ref, gpre_ref, wu_ref, wd_ref, gpost_ref, out_ref, hn_ref, acc_ref):
    f = pl.program_id(1)

    @pl.when(f == 0)
    def _():
        hn_ref[...] = _rms(x_ref[...], gpre_ref[...]).astype(BF16)
        acc_ref[...] = jnp.zeros_like(acc_ref)

    u = jnp.dot(hn_ref[...], wu_ref[...], preferred_element_type=F32)
    u = jnp.square(jnp.maximum(u, 0.0)).astype(BF16)
    acc_ref[...] += jnp.dot(u, wd_ref[...], preferred_element_type=F32)

    @pl.when(f == pl.num_programs(1) - 1)
    def _():
        out_ref[...] = x_ref[...] + _rms(acc_ref[...], gpost_ref[...])


def _ffn(x, gpre, wu, wd, gpost, tm=512, tf=1024):
    S, D = x.shape
    Fd = wu.shape[1]
    return pl.pallas_call(
        _ffn_kernel,
        grid=(S // tm, Fd // tf),
        in_specs=[
            pl.BlockSpec((tm, D), lambda i, f: (i, 0)),
            pl.BlockSpec((1, D), lambda i, f: (0, 0)),
            pl.BlockSpec((D, tf), lambda i, f: (0, f)),
            pl.BlockSpec((tf, D), lambda i, f: (f, 0)),
            pl.BlockSpec((1, D), lambda i, f: (0, 0)),
        ],
        out_specs=pl.BlockSpec((tm, D), lambda i, f: (i, 0)),
        out_shape=jax.ShapeDtypeStruct((S, D), F32),
        scratch_shapes=[pltpu.VMEM((tm, D), BF16), pltpu.VMEM((tm, D), F32)],
        compiler_params=_cparams(("parallel", "arbitrary")),
        name="ffn",
    )(x, gpre, wu, wd, gpost)


def _layer(x, pos, norm_mix_pre, w_in, conv_w, conv_b, i_bias, f_bias, mlstm_norm,
           w_branch_m, w_branch_a, w_out, norm_mix_post, norm_ffn_pre, w_up, w_down,
           norm_ffn_post):
    S, D = x.shape
    o = 0
    pieces = {}
    w_in = w_in.astype(BF16)
    for name, width in (("mq", M_WIDTH), ("mk", M_WIDTH), ("mv", M_WIDTH), ("mo", M_WIDTH),
                        ("mi", M_HEADS), ("mf", M_HEADS), ("aq", A_WIDTH), ("ak", A_WIDTH),
                        ("av", A_WIDTH), ("gm", D), ("ga", D)):
        pieces[name] = w_in[:, o:o + width]
        o += width
    w_all = jnp.concatenate([pieces[n] for n in ("gm", "ga", "mq", "mk", "aq", "ak", "mv", "mo", "av")],
                            axis=1)
    w_gate = jnp.concatenate(
        [pieces["mi"], pieces["mf"], jnp.zeros((D, LANES - 2 * M_HEADS), BF16)], axis=1)
    gbias = jnp.concatenate([i_bias, f_bias, jnp.zeros((LANES - 2 * M_HEADS,), F32)])[None, :]
    half = jnp.arange(0, A_HEAD_DIM, 2, dtype=F32) / A_HEAD_DIM
    inv_freq = 1.0 / (ROPE_THETA ** half)
    invf = jnp.concatenate([inv_freq, inv_freq])[None, :]

    P, gate = _proj(x, norm_mix_pre[None, :], w_all, w_gate)
    mq, mkT, aq, ak, avT, kmean, gc, gt = _prep(
        P, gate, pos.reshape(S, 1), invf, conv_w, conv_b[None, :], gbias)
    hm = _mlstm(mq, mkT, P, gc, gt, mlstm_norm[None, :])
    ha = _moba_routed(aq, ak, avT, kmean.reshape(S // MOBA_BLOCK, A_WIDTH))
    x1 = _merge(hm, ha, P, x, w_branch_m.astype(BF16), w_branch_a.astype(BF16),
                w_out.astype(BF16), norm_mix_post[None, :])
    return _ffn(x1, norm_ffn_pre[None, :], w_up.astype(BF16), w_down.astype(BF16),
                norm_ffn_post[None, :])


def kernel(x, positions, norm_mix_pre, w_in, conv_w, conv_b, i_bias, f_bias, mlstm_norm,
           w_branch_m, w_branch_a, w_out, norm_mix_post, norm_ffn_pre, w_up, w_down,
           norm_ffn_post):
    B = x.shape[0]
    depth = w_in.shape[0]
    outs = []
    for b in range(B):
        xb = x[b]
        for l in range(depth):
            xb = _layer(xb, positions[b], norm_mix_pre[l], w_in[l], conv_w[l], conv_b[l],
                        i_bias[l], f_bias[l], mlstm_norm[l], w_branch_m[l], w_branch_a[l],
                        w_out[l], norm_mix_post[l], norm_ffn_pre[l], w_up[l], w_down[l],
                        norm_ffn_post[l])
        outs.append(xb)
    return outs[0][None] if B == 1 else jnp.stack(outs, axis=0)
```

```python
import functools

import jax
import jax.numpy as jnp
from jax import lax
from jax.experimental import pallas as pl
from jax.experimental.pallas import tpu as pltpu
from jax.experimental.pallas import tpu_sc as plsc

F32 = jnp.float32
BF16 = jnp.bfloat16

M_HEADS = 4
M_HEAD_DIM = 256
M_WIDTH = M_HEADS * M_HEAD_DIM
M_CHUNK = 128
CONV_WIDTH = 4
A_HEADS = 8
A_HEAD_DIM = 128
A_WIDTH = A_HEADS * A_HEAD_DIM
MOBA_BLOCK = 256
MOBA_TOPK = 3
ROPE_THETA = 10000.0
NORM_EPS = 1e-6

LANES = 128
SUBLANES = 8
VMEM_LIMIT = 56 * 1024 * 1024
NEG_BIG = -1e30
LOG2_E = 1.4426950408889634

COL_GM = 0
COL_GA = 2048
COL_MQK = 4096
COL_AQK = 6144
COL_MV = 8192
COL_MO = 9216
COL_AV = 10240
P_COLS = 11264


def _cparams(sem):
    return pltpu.CompilerParams(dimension_semantics=sem, vmem_limit_bytes=VMEM_LIMIT)


def _rms(x, gain):
    ms = jnp.mean(x * x, axis=-1, keepdims=True)
    return x * lax.rsqrt(ms + NORM_EPS) * gain


def _sigmoid(x):
    return 1.0 / (1.0 + jnp.exp(-x))


def _split3(x):
    hi = x.astype(BF16)
    r1 = x - hi.astype(F32)
    mid = r1.astype(BF16)
    lo = (r1 - mid.astype(F32)).astype(BF16)
    return hi, mid, lo


def _proj_kernel(x_ref, g_ref, w_ref, wg_ref, p_ref, gate_ref, xn_ref):
    @pl.when(pl.program_id(1) == 0)
    def _():
        xn = _rms(x_ref[...], g_ref[...]).astype(BF16)
        xn_ref[...] = xn
        gate_ref[...] = jnp.dot(xn, wg_ref[...], preferred_element_type=F32)

    p_ref[...] = jnp.dot(xn_ref[...], w_ref[...], preferred_element_type=F32).astype(BF16)


def _proj(x, gain, w_all, w_gate, tm=1024, tn=1024):
    S, D = x.shape
    N = w_all.shape[1]
    return pl.pallas_call(
        _proj_kernel,
        grid=(S // tm, N // tn),
        in_specs=[
            pl.BlockSpec((tm, D), lambda i, j: (i, 0)),
            pl.BlockSpec((1, D), lambda i, j: (0, 0)),
            pl.BlockSpec((D, tn), lambda i, j: (0, j)),
            pl.BlockSpec((D, LANES), lambda i, j: (0, 0)),
        ],
        out_specs=[
            pl.BlockSpec((tm, tn), lambda i, j: (i, j)),
            pl.BlockSpec((tm, LANES), lambda i, j: (i, 0)),
        ],
        out_shape=[
            jax.ShapeDtypeStruct((S, N), BF16),
            jax.ShapeDtypeStruct((S, LANES), F32),
        ],
        scratch_shapes=[pltpu.VMEM((tm, D), BF16)],
        compiler_params=_cparams(("parallel", "arbitrary")),
        name="proj",
    )(x, gain, w_all, w_gate)


PREP_ROWS = MOBA_BLOCK
PREP_COLS = 512
HALO_ROWS = 2 * SUBLANES


def _prep_kernel(pmk_ref, halo_ref, paqk_ref, pav_ref, gate_ref, pos_ref, invf_ref,
                 cw_ref, cb_ref, gb_ref,
                 mq_ref, mkT_ref, aq_ref, ak_ref, avT_ref, kmean_ref, gc_ref, gt_ref):
    i = pl.program_id(0)
    R = PREP_ROWS

    k_scale = M_HEAD_DIM ** -0.5
    for c0 in range(0, 2 * M_WIDTH, PREP_COLS):
        cs = slice(c0, c0 + PREP_COLS)
        prev = halo_ref[:, cs].astype(F32)[HALO_ROWS - SUBLANES:, :]
        prev = jnp.where(i == 0, jnp.zeros_like(prev), prev)
        ext = jnp.concatenate([pmk_ref[:, cs].astype(F32), prev], axis=0)
        acc = cw_ref[0:1, cs] * ext
        for j in range(1, CONV_WIDTH):
            acc = pltpu.roll(acc, 1, 0) + cw_ref[j:j + 1, cs] * ext
        acc = acc[0:R, :] + cb_ref[:, cs]
        y = acc * _sigmoid(acc)
        if c0 < M_WIDTH:
            mq_ref[:, cs] = y.astype(BF16)
        else:
            ks = slice(c0 - M_WIDTH, c0 - M_WIDTH + PREP_COLS)
            mkT_ref[ks, :] = (y * k_scale).T.astype(BF16)

    ang = pos_ref[...].astype(F32) * invf_ref[...]
    cos = jnp.cos(ang)
    lane = lax.broadcasted_iota(jnp.int32, (R, A_HEAD_DIM), 1)
    sin_signed = jnp.where(lane < A_HEAD_DIM // 2, -1.0, 1.0) * jnp.sin(ang)
    for h in range(2 * A_HEADS):
        hs = slice(h * A_HEAD_DIM, (h + 1) * A_HEAD_DIM)
        xh = paqk_ref[:, hs].astype(F32)
        yh = xh * cos + pltpu.roll(xh, A_HEAD_DIM // 2, 1) * sin_signed
        if h < A_HEADS:
            aq_ref[h] = yh
        else:
            ko = slice((h - A_HEADS) * A_HEAD_DIM, (h - A_HEADS + 1) * A_HEAD_DIM)
            ak_ref[:, ko] = yh.astype(BF16)
            kmean_ref[0, :, ko] = jnp.mean(yh, axis=0, keepdims=True)
    for c0 in range(0, A_WIDTH, PREP_COLS):
        avT_ref[c0:c0 + PREP_COLS, :] = pav_ref[:, c0:c0 + PREP_COLS].astype(F32).T.astype(BF16)

    g = gate_ref[...] + gb_ref[...]
    log_f = jnp.minimum(g, 0.0) - jnp.log1p(jnp.exp(-jnp.abs(g)))
    r_i = lax.broadcasted_iota(jnp.int32, (R, R), 0)
    c_i = lax.broadcasted_iota(jnp.int32, (R, R), 1)
    tri = ((r_i >= c_i) & ((r_i // M_CHUNK) == (c_i // M_CHUNK))).astype(BF16)
    hi, mid, lo = _split3(log_f)
    csum = (jnp.dot(tri, hi, preferred_element_type=F32)
            + jnp.dot(tri, mid, preferred_element_type=F32)
            + jnp.dot(tri, lo, preferred_element_type=F32))
    glane = lax.broadcasted_iota(jnp.int32, (R, LANES), 1)
    gc = jnp.where(glane < M_HEADS, g, csum)
    gc_ref[...] = gc
    gt_ref[...] = gc.T[0:SUBLANES, :]


def _prep(P, gate, pos, invf, conv_w, conv_b, gbias):
    S = P.shape[0]
    R = PREP_ROWS
    nb = S // R
    halo_blocks = R // HALO_ROWS
    return pl.pallas_call(
        _prep_kernel,
        grid=(nb,),
        in_specs=[
            pl.BlockSpec((R, 2 * M_WIDTH), lambda i: (i, COL_MQK // (2 * M_WIDTH))),
            pl.BlockSpec((HALO_ROWS, 2 * M_WIDTH),
                         lambda i: (jnp.maximum(i * halo_blocks - 1, 0), COL_MQK // (2 * M_WIDTH))),
            pl.BlockSpec((R, 2 * A_WIDTH), lambda i: (i, COL_AQK // (2 * A_WIDTH))),
            pl.BlockSpec((R, A_WIDTH), lambda i: (i, COL_AV // A_WIDTH)),
            pl.BlockSpec((R, LANES), lambda i: (i, 0)),
            pl.BlockSpec((R, 1), lambda i: (i, 0)),
            pl.BlockSpec((1, A_HEAD_DIM), lambda i: (0, 0)),
            pl.BlockSpec((CONV_WIDTH, 2 * M_WIDTH), lambda i: (0, 0)),
            pl.BlockSpec((1, 2 * M_WIDTH), lambda i: (0, 0)),
            pl.BlockSpec((1, LANES), lambda i: (0, 0)),
        ],
        out_specs=[
            pl.BlockSpec((R, M_WIDTH), lambda i: (i, 0)),
            pl.BlockSpec((M_WIDTH, R), lambda i: (0, i)),
            pl.BlockSpec((A_HEADS, R, A_HEAD_DIM), lambda i: (0, i, 0)),
            pl.BlockSpec((R, A_WIDTH), lambda i: (i, 0)),
            pl.BlockSpec((A_WIDTH, R), lambda i: (0, i)),
            pl.BlockSpec((1, 1, A_WIDTH), lambda i: (i, 0, 0)),
            pl.BlockSpec((R, LANES), lambda i: (i, 0)),
            pl.BlockSpec((SUBLANES, R), lambda i: (0, i)),
        ],
        out_shape=[
            jax.ShapeDtypeStruct((S, M_WIDTH), BF16),
            jax.ShapeDtypeStruct((M_WIDTH, S), BF16),
            jax.ShapeDtypeStruct((A_HEADS, S, A_HEAD_DIM), F32),
            jax.ShapeDtypeStruct((S, A_WIDTH), BF16),
            jax.ShapeDtypeStruct((A_WIDTH, S), BF16),
            jax.ShapeDtypeStruct((nb, 1, A_WIDTH), F32),
            jax.ShapeDtypeStruct((S, LANES), F32),
            jax.ShapeDtypeStruct((SUBLANES, S), F32),
        ],
        compiler_params=_cparams(("parallel",)),
        name="prep",
    )(P, P, P, P, gate, pos, invf, conv_w, conv_b, gbias)


M_AUG = M_HEAD_DIM + LANES


def _mlstm_kernel(q_ref, kT_ref, v_ref, mo_ref, gc_ref, gt_ref, gain_ref, out_ref,
                  c_ref, m_ref):
    @pl.when(pl.program_id(0) == 0)
    def _():
        c_ref[...] = jnp.zeros_like(c_ref)
        m_ref[...] = jnp.zeros_like(m_ref)

    L = M_CHUNK
    D = M_HEAD_DIM
    row = lax.broadcasted_iota(jnp.int32, (L, L), 0)
    col = lax.broadcasted_iota(jnp.int32, (L, L), 1)
    causal = row >= col
    ones_col = (lax.broadcasted_iota(jnp.int32, (L, LANES), 1) == 0).astype(BF16)

    heads = range(M_HEADS)
    hsl = [slice(h * D, (h + 1) * D) for h in heads]
    q = [q_ref[:, hsl[h]] for h in heads]
    kT = [kT_ref[hsl[h], :] for h in heads]
    v_aug = [jnp.concatenate([v_ref[:, hsl[h]], ones_col], axis=1) for h in heads]
    b_c = [gc_ref[:, M_HEADS + h:M_HEADS + h + 1] for h in heads]
    b_r = [gt_ref[M_HEADS + h:M_HEADS + h + 1, :] for h in heads]
    u_r = [gt_ref[h:h + 1, :] - b_r[h] for h in heads]
    f_tot = [b_r[h][:, L - 1:L] for h in heads]
    m_prev = [m_ref[h:h + 1, 0:1] for h in heads]

    s_qk = [jnp.dot(q[h], kT[h], preferred_element_type=F32) for h in heads]
    q_c = [jnp.dot(q[h], c_ref[h].astype(BF16), preferred_element_type=F32) for h in heads]

    for h in heads:
        w_r = f_tot[h] + u_r[h]
        m_loc = jnp.max(w_r, axis=1, keepdims=True)
        m_new = jnp.maximum(f_tot[h] + m_prev[h], m_loc)
        a = jnp.exp(f_tot[h] + m_prev[h] - m_new)
        e_r = jnp.exp(w_r - m_new)
        keT = (kT[h].astype(F32) * e_r).astype(BF16)
        c_ref[h] = a * c_ref[h] + jnp.dot(keT, v_aug[h], preferred_element_type=F32)
        m_ref[h:h + 1, :] = jnp.broadcast_to(m_new, (1, LANES))

    for h in heads:
        d_log = jnp.where(causal, b_c[h] + u_r[h], -jnp.inf)
        a_log = b_c[h] + m_prev[h]
        m_t = jnp.maximum(a_log, jnp.max(d_log, axis=1, keepdims=True))
        s_ts = s_qk[h] * jnp.exp(d_log - m_t)
        inter = jnp.exp(a_log - m_t)
        r = inter * q_c[h] + jnp.dot(s_ts.astype(BF16), v_aug[h], preferred_element_type=F32)
        num = r[:, :D]
        den = r[:, D:D + 1]
        hh = num / jnp.maximum(jnp.abs(den), jnp.exp(-m_t))
        hn = hh * lax.rsqrt(jnp.mean(hh * hh, axis=-1, keepdims=True) + NORM_EPS)
        out_ref[:, hsl[h]] = (hn * gain_ref[:, hsl[h]]
                              * _sigmoid(mo_ref[:, hsl[h]].astype(F32))).astype(BF16)


def _mlstm(mq, mkT, P, gc, gt, gain):
    S = mq.shape[0]
    L = M_CHUNK
    return pl.pallas_call(
        _mlstm_kernel,
        grid=(S // L,),
        in_specs=[
            pl.BlockSpec((L, M_WIDTH), lambda c: (c, 0)),
            pl.BlockSpec((M_WIDTH, L), lambda c: (0, c)),
            pl.BlockSpec((L, M_WIDTH), lambda c: (c, COL_MV // M_WIDTH)),
            pl.BlockSpec((L, M_WIDTH), lambda c: (c, COL_MO // M_WIDTH)),
            pl.BlockSpec((L, LANES), lambda c: (c, 0)),
            pl.BlockSpec((SUBLANES, L), lambda c: (0, c)),
            pl.BlockSpec((1, M_WIDTH), lambda c: (0, 0)),
        ],
        out_specs=pl.BlockSpec((L, M_WIDTH), lambda c: (c, 0)),
        out_shape=jax.ShapeDtypeStruct((S, M_WIDTH), BF16),
        scratch_shapes=[
            pltpu.VMEM((M_HEADS, M_HEAD_DIM, M_AUG), F32),
            pltpu.VMEM((SUBLANES, LANES), F32),
        ],
        compiler_params=_cparams(("arbitrary",)),
        name="mlstm",
    )(mq, mkT, P, P, gc, gt, gain)


MOBA_HEADS_PER_STEP = 4


def _moba_kernel(q_ref, k_ref, vT_ref, km_ref, o_ref, sel_ref, qs_ref, m_ref, l_ref, acc_ref,
                 s0_ref, s1_ref, mb0_ref, mb1_ref):
    i = pl.program_id(1)
    BS = MOBA_BLOCK
    Dh = A_HEAD_DIM
    NB = km_ref.shape[0]
    G = MOBA_HEADS_PER_STEP
    own = pl.multiple_of(i * BS, BS)
    blk = lax.broadcasted_iota(jnp.int32, (NB, BS), 0)
    kpos = lax.broadcasted_iota(jnp.int32, (BS, BS), 0)
    qpos = lax.broadcasted_iota(jnp.int32, (BS, BS), 1)

    ones_rows = jnp.ones((2 * SUBLANES, BS), BF16)

    def stage_scores(start, s_buf, mb_buf, causal):
        for g in range(G):
            hs = slice(g * Dh, (g + 1) * Dh)
            s = jnp.dot(k_ref[pl.ds(start, BS), hs], qs_ref[g], preferred_element_type=F32)
            if causal:
                s = jnp.where(kpos <= qpos, s, NEG_BIG)
            sb = s.astype(BF16)
            s_buf[g] = sb
            mb_buf[g] = jnp.max(sb, axis=0, keepdims=True).astype(F32)

    def stage_apply(start, picked_of, s_buf, mb_buf):
        for g in range(G):
            hs = slice(g * Dh, (g + 1) * Dh)
            picked = picked_of(g)
            m_old = m_ref[g]
            m_new = jnp.maximum(m_old, jnp.where(picked, mb_buf[g], NEG_BIG))
            a_old = jnp.exp2(m_old - m_new)
            shift = jnp.where(picked, m_new, -NEG_BIG)
            p = jnp.exp2(s_buf[g] - shift.astype(BF16))
            v_aug = jnp.concatenate([vT_ref[hs, pl.ds(start, BS)], ones_rows], axis=0)
            r = jnp.dot(v_aug, p, preferred_element_type=F32)
            l_ref[g] = a_old * l_ref[g] + r[Dh:Dh + 1, :]
            acc_ref[g] = a_old * acc_ref[g] + r[:Dh, :]
            m_ref[g] = m_new

    for g in range(G):
        hs = slice(g * Dh, (g + 1) * Dh)
        qT = q_ref[:, hs].T
        km = km_ref[:, hs]
        kh = km.astype(BF16)
        kl = (km - kh.astype(F32)).astype(BF16)
        qh = qT.astype(BF16)
        ql = (qT - qh.astype(F32)).astype(BF16)
        gate = (jnp.dot(kh, qh, preferred_element_type=F32)
                + jnp.dot(kh, ql, preferred_element_type=F32)
                + jnp.dot(kl, qh, preferred_element_type=F32))
        gate = jnp.where(blk < i, gate, -jnp.inf)
        sel = jnp.zeros((NB, BS), F32)
        for r in range(MOBA_TOPK):
            mx = jnp.max(gate, axis=0, keepdims=True)
            idx = jnp.min(jnp.where(gate == mx, blk, NB), axis=0, keepdims=True)
            idx = jnp.where(r < i, idx, -1)
            pick = blk == idx
            sel = jnp.where(pick, 1.0, sel)
            gate = jnp.where(pick, -jnp.inf, gate)
        sel_ref[g] = sel
        qs_ref[g] = (qT * (A_HEAD_DIM ** -0.5 * LOG2_E)).astype(BF16)

    m_ref[...] = jnp.full(m_ref.shape, NEG_BIG, F32)
    l_ref[...] = jnp.zeros(l_ref.shape, F32)
    acc_ref[...] = jnp.zeros(acc_ref.shape, F32)

    def block_start(j):
        return pl.multiple_of(jnp.minimum(j, NB - 1) * BS, BS)

    def picked_past(j):
        return lambda g: sel_ref[g, pl.ds(j, 1), :] > 0.0

    stage_scores(own, s1_ref, mb1_ref, True)
    stage_scores(block_start(0), s0_ref, mb0_ref, False)
    stage_apply(own, lambda g: jnp.full((1, BS), True), s1_ref, mb1_ref)

    def pair(j):
        stage_scores(block_start(j + 1), s1_ref, mb1_ref, False)
        stage_apply(block_start(j), picked_past(j), s0_ref, mb0_ref)
        stage_scores(block_start(j + 2), s0_ref, mb0_ref, False)
        stage_apply(block_start(j + 1), picked_past(j + 1), s1_ref, mb1_ref)

    def quad_body(t, carry):
        pair(4 * t)
        pair(4 * t + 2)
        return carry

    def pair_body(t, carry):
        pair(4 * (i // 4) + 2 * t)
        return carry

    lax.fori_loop(0, i // 4, quad_body, 0)
    lax.fori_loop(0, (i % 4 + 1) // 2, pair_body, 0)

    for g in range(G):
        o_ref[:, g * Dh:(g + 1) * Dh] = (acc_ref[g] / l_ref[g]).T.astype(BF16)


def _moba(aq, ak, avT, kmean):
    S = aq.shape[0]
    BS = MOBA_BLOCK
    NB = S // BS
    G = MOBA_HEADS_PER_STEP
    W = G * A_HEAD_DIM
    once = pl.Buffered(1)
    return pl.pallas_call(
        _moba_kernel,
        grid=(A_HEADS // G, NB),
        in_specs=[
            pl.BlockSpec((BS, W), lambda h, i: (i, h)),
            pl.BlockSpec((S, W), lambda h, i: (0, h), pipeline_mode=once),
            pl.BlockSpec((W, S), lambda h, i: (h, 0), pipeline_mode=once),
            pl.BlockSpec((NB, W), lambda h, i: (0, h)),
        ],
        out_specs=pl.BlockSpec((BS, W), lambda h, i: (i, h)),
        out_shape=jax.ShapeDtypeStruct((S, A_WIDTH), BF16),
        scratch_shapes=[
            pltpu.VMEM((G, NB, BS), F32),
            pltpu.VMEM((G, A_HEAD_DIM, BS), BF16),
            pltpu.VMEM((G, 1, BS), F32),
            pltpu.VMEM((G, 1, BS), F32),
            pltpu.VMEM((G, A_HEAD_DIM, BS), F32),
            pltpu.VMEM((G, BS, BS), BF16),
            pltpu.VMEM((G, BS, BS), BF16),
            pltpu.VMEM((G, 1, BS), F32),
            pltpu.VMEM((G, 1, BS), F32),
        ],
        compiler_params=_cparams(("parallel", "arbitrary")),
        name="moba",
    )(aq, ak, avT, kmean)


ROUTE_TILES = 264
ROUTE_ROWS = ROUTE_TILES * MOBA_BLOCK
GROUP_TILES_PER_STEP = 8
ROW_WORDS = A_HEAD_DIM // 2


def _pack_rows(o_t, lse):
    q = o_t.shape[1]
    hi = pltpu.bitcast(o_t[:ROW_WORDS].astype(BF16).astype(F32), jnp.uint32)
    lo = pltpu.bitcast(o_t[ROW_WORDS:].astype(BF16).astype(F32), jnp.uint32)
    words = pltpu.bitcast(hi | (lo >> 16), F32)
    tail = jnp.concatenate([jnp.broadcast_to(lse, (SUBLANES, q)),
                            jnp.zeros((LANES - ROW_WORDS - SUBLANES, q), F32)], axis=0)
    return jnp.concatenate([words, tail], axis=0).T


def _unpack_rows(rows):
    words = pltpu.bitcast(rows[:, :ROW_WORDS], jnp.uint32)
    first = pltpu.bitcast(words & jnp.uint32(0xFFFF0000), F32)
    second = pltpu.bitcast(words << 16, F32)
    return first, second, rows[:, ROW_WORDS:ROW_WORDS + 1]


def _dot_nt(a, b):
    return lax.dot_general(a, b, (((1,), (1,)), ((), ())), preferred_element_type=F32)


def _block_partial(k_blk, v_t_blk, qs, ones_rows, causal_mask=None):
    s = _dot_nt(k_blk, qs)
    if causal_mask is not None:
        s = jnp.where(causal_mask, s, NEG_BIG)
    sb = s.astype(BF16)
    m = jnp.max(sb, axis=0, keepdims=True)
    p = jnp.exp2(sb - m)
    r = jnp.dot(jnp.concatenate([v_t_blk, ones_rows], axis=0), p, preferred_element_type=F32)
    dh = v_t_blk.shape[0]
    l = r[dh:dh + 1, :]
    return r[:dh, :] / l, m.astype(F32) + jnp.log2(l)


def _route_kernel(q_ref, k_ref, vT_ref, km_ref, route_ref, cnt_ref, own_ref, cnt_acc):
    i = pl.program_id(1)
    BS = MOBA_BLOCK
    Dh = A_HEAD_DIM
    NB = km_ref.shape[0]
    G = MOBA_HEADS_PER_STEP
    own = pl.multiple_of(i * BS, BS)
    blk = lax.broadcasted_iota(jnp.int32, (NB, BS), 0)
    kpos = lax.broadcasted_iota(jnp.int32, (BS, BS), 0)
    qpos = lax.broadcasted_iota(jnp.int32, (BS, BS), 1)
    earlier = (kpos < qpos).astype(BF16)
    ones_rows = jnp.ones((2 * SUBLANES, BS), BF16)

    @pl.when(i == 0)
    def _():
        cnt_acc[...] = jnp.zeros_like(cnt_acc)

    q = [q_ref[g] for g in range(G)]
    qs = [(q[g] * (Dh ** -0.5 * LOG2_E)).astype(BF16) for g in range(G)]
    parts = [_block_partial(k_ref[pl.ds(own, BS), g * Dh:(g + 1) * Dh],
                            vT_ref[g * Dh:(g + 1) * Dh, pl.ds(own, BS)], qs[g], ones_rows,
                            kpos <= qpos) for g in range(G)]

    gates = []
    for g in range(G):
        hs = slice(g * Dh, (g + 1) * Dh)
        km = km_ref[:, hs]
        kh = km.astype(BF16)
        kl = (km - kh.astype(F32)).astype(BF16)
        qh = q[g].astype(BF16)
        ql = (q[g] - qh.astype(F32)).astype(BF16)
        gate = _dot_nt(kh, qh) + _dot_nt(kh, ql) + _dot_nt(kl, qh)
        gates.append(jnp.where(blk < i, gate, -jnp.inf))
    picks = [[] for _ in range(G)]
    rows = [[] for _ in range(G)]
    for r in range(MOBA_TOPK):
        for g in range(G):
            mx = jnp.max(gates[g], axis=0, keepdims=True)
            idx = jnp.min(jnp.where(gates[g] == mx, blk, NB), axis=0, keepdims=True)
            idx = jnp.where(r < i, idx, -1)
            pick = blk == idx
            gates[g] = jnp.where(pick, -jnp.inf, gates[g])
            picks[g].append(pick)
            rows[g].append(idx)
    for g in range(G):
        onehot = sum(p.astype(F32) for p in picks[g])
        before = cnt_acc[g][:, 0:1] + jnp.dot(onehot.astype(BF16), earlier, preferred_element_type=F32)
        for r in range(MOBA_TOPK):
            rank = jnp.sum(jnp.where(picks[g][r], before, 0.0), axis=0, keepdims=True)
            rows[g].append(rank.astype(jnp.int32))
        rows[g].append(jnp.zeros((SUBLANES - 2 * MOBA_TOPK, BS), jnp.int32))
        route_ref[g] = jnp.concatenate(rows[g], axis=0)
        cnt_new = cnt_acc[g] + jnp.sum(onehot, axis=1, keepdims=True)
        cnt_acc[g] = cnt_new
        cnt_ref[g] = cnt_new
        own_ref[g] = _pack_rows(*parts[g])


def _route(aq_hm, ak, avT, kmean, h0, H):
    _, S, Dh = aq_hm.shape
    BS = MOBA_BLOCK
    NB = S // BS
    G = MOBA_HEADS_PER_STEP
    W = G * Dh
    hb = h0 // G
    once = pl.Buffered(1)
    return pl.pallas_call(
        _route_kernel,
        grid=(H // G, NB),
        in_specs=[
            pl.BlockSpec((G, BS, Dh), lambda h, i: (hb + h, i, 0)),
            pl.BlockSpec((S, W), lambda h, i: (0, hb + h), pipeline_mode=once),
            pl.BlockSpec((W, S), lambda h, i: (hb + h, 0), pipeline_mode=once),
            pl.BlockSpec((NB, W), lambda h, i: (0, hb + h)),
        ],
        out_specs=[
            pl.BlockSpec((G, SUBLANES, BS), lambda h, i: (h, 0, i)),
            pl.BlockSpec((G, NB, LANES), lambda h, i: (h, 0, 0)),
            pl.BlockSpec((G, BS, LANES), lambda h, i: (h, i, 0)),
        ],
        out_shape=[
            jax.ShapeDtypeStruct((H, SUBLANES, S), jnp.int32),
            jax.ShapeDtypeStruct((H, NB, LANES), F32),
            jax.ShapeDtypeStruct((H, S, LANES), F32),
        ],
        scratch_shapes=[pltpu.VMEM((G, NB, LANES), F32)],
        compiler_params=_cparams(("parallel", "arbitrary")),
        name="moba_route",
    )(aq_hm, ak, avT, kmean)


def _dest_kernel(seg_ref, route_ref, dest_ref):
    h = pl.program_id(0)
    NB = seg_ref.shape[1]
    r = route_ref[0]
    blk = r[0:MOBA_TOPK, :]
    rank = r[MOBA_TOPK:2 * MOBA_TOPK, :]
    base = h * ROUTE_ROWS
    dest = jnp.full(blk.shape, base + ROUTE_ROWS - 1, jnp.int32)
    for j in range(NB):
        dest = jnp.where(blk == j, base + seg_ref[h, j] + rank, dest)
    dest_ref[0] = jnp.concatenate(
        [dest, jnp.zeros((SUBLANES - MOBA_TOPK, dest.shape[1]), jnp.int32)], axis=0)


def _dest(seg_start, route):
    H, _, S = route.shape
    return pl.pallas_call(
        _dest_kernel,
        grid_spec=pltpu.PrefetchScalarGridSpec(
            num_scalar_prefetch=1,
            grid=(H,),
            in_specs=[pl.BlockSpec((1, SUBLANES, S), lambda h, seg: (h, 0, 0))],
            out_specs=pl.BlockSpec((1, SUBLANES, S), lambda h, seg: (h, 0, 0)),
        ),
        out_shape=jax.ShapeDtypeStruct((H, SUBLANES, S), jnp.int32),
        compiler_params=_cparams(("parallel",)),
        name="moba_dest",
    )(seg_start, route)


SC_WINDOW = 256


def _sc_mesh():
    return plsc.VectorSubcoreMesh(core_axis_name="c", subcore_axis_name="s")


def _sc_scatter(rows, row0, n_rows, idx, n_out):
    W = rows.shape[1]
    M = idx.shape[0]
    n_win = n_rows // SC_WINDOW
    win0 = row0 // SC_WINDOW

    @pl.kernel(out_type=jax.ShapeDtypeStruct((n_out, W), rows.dtype), mesh=_sc_mesh())
    def k(x_hbm, i_hbm, o_hbm):
        def body(x_vmem, i_vmem):
            pltpu.sync_copy(x_vmem, o_hbm.at[i_vmem.at[0]])

        pltpu.emit_pipeline(
            body,
            grid=(M // SC_WINDOW,),
            in_specs=[pl.BlockSpec((SC_WINDOW, W), lambda w: (win0 + lax.rem(w, n_win), 0)),
                      pl.BlockSpec((1, SC_WINDOW), lambda w: (0, w))],
            out_specs=[],
            core_axis_name=("c", "s"),
            dimension_semantics=(pltpu.PARALLEL,),
        )(x_hbm, i_hbm)

    return k(rows, idx.reshape(1, M))


def _sc_gather(table, idx):
    M = idx.shape[0]
    W = table.shape[1]

    @pl.kernel(out_type=jax.ShapeDtypeStruct((M, W), table.dtype), mesh=_sc_mesh())
    def k(x_hbm, i_hbm, o_hbm):
        def body(i_vmem, o_vmem):
            pltpu.sync_copy(x_hbm.at[i_vmem.at[0]], o_vmem)

        pltpu.emit_pipeline(
            body,
            grid=(M // SC_WINDOW,),
            in_specs=[pl.BlockSpec((1, SC_WINDOW), lambda w: (0, w))],
            out_specs=[pl.BlockSpec((SC_WINDOW, W), lambda w: (w, 0))],
            core_axis_name=("c", "s"),
            dimension_semantics=(pltpu.PARALLEL,),
        )(i_hbm, o_hbm)

    return k(table, idx.reshape(1, M))


def _group_kernel(tile_blk_ref, q_ref, k_ref, vT_ref, out_ref, s_ref, m_ref):
    h = pl.program_id(0)
    u = pl.program_id(1)
    last = pl.num_programs(1) - 2
    BS = MOBA_BLOCK
    Dh = A_HEAD_DIM
    T = GROUP_TILES_PER_STEP
    ones_rows = jnp.ones((2 * SUBLANES, BS), BF16)
    new = lax.rem(u, 2)
    old = 1 - new

    @pl.when((h == 0) & (u == 0))
    def _():
        s_ref[...] = jnp.zeros_like(s_ref)
        m_ref[...] = jnp.zeros_like(m_ref)

    def block_start(group, c):
        j = jnp.maximum(tile_blk_ref[h * ROUTE_TILES + group * T + c], 0)
        return pl.multiple_of(j * BS, BS)

    g_new = jnp.minimum(u, last)
    g_old = jnp.maximum(u - 1, 0)
    ms, rs = [], []
    for c in range(T):
        m = m_ref[old, c]
        p = jnp.exp2(s_ref[old, c] - m.astype(BF16))
        v_aug = jnp.concatenate([vT_ref[:, pl.ds(block_start(g_old, c), BS)], ones_rows], axis=0)
        ms.append(m)
        rs.append(jnp.dot(v_aug, p, preferred_element_type=F32))
    s_new = []
    for c in range(T):
        qs = (q_ref[c * BS:(c + 1) * BS, :] * (Dh ** -0.5 * LOG2_E)).astype(BF16)
        s_new.append(_dot_nt(k_ref[pl.ds(block_start(g_new, c), BS), :], qs))
    for c in range(T):
        l = rs[c][Dh:Dh + 1, :]
        out_ref[c * BS:(c + 1) * BS, :] = _pack_rows(rs[c][:Dh, :] / l, ms[c] + jnp.log2(l))
        sb = s_new[c].astype(BF16)
        s_ref[new, c] = sb
        m_ref[new, c] = jnp.max(sb, axis=0, keepdims=True).astype(F32)


def _group(tile_blk, q_sorted, ak, avT, h0):
    S = ak.shape[0]
    H = q_sorted.shape[0] // ROUTE_ROWS
    Dh = A_HEAD_DIM
    BS = MOBA_BLOCK
    T = GROUP_TILES_PER_STEP
    steps = ROUTE_TILES // T
    rows = T * BS
    return pl.pallas_call(
        _group_kernel,
        grid_spec=pltpu.PrefetchScalarGridSpec(
            num_scalar_prefetch=1,
            grid=(H, steps + 1),
            in_specs=[
                pl.BlockSpec((rows, LANES), lambda h, u, tb: (h * steps + jnp.minimum(u, steps - 1), 0)),
                pl.BlockSpec((S, Dh), lambda h, u, tb: (0, h0 + h)),
                pl.BlockSpec((Dh, S), lambda h, u, tb: (h0 + h, 0)),
            ],
            out_specs=pl.BlockSpec((rows, LANES), lambda h, u, tb: (h * steps + jnp.maximum(u - 1, 0), 0)),
            scratch_shapes=[pltpu.VMEM((2, T, BS, BS), BF16), pltpu.VMEM((2, T, 1, BS), F32)],
        ),
        out_shape=jax.ShapeDtypeStruct((H * ROUTE_ROWS, LANES), F32),
        compiler_params=_cparams(("arbitrary", "arbitrary")),
        name="moba_group",
    )(tile_blk, q_sorted, ak, avT)


def _combine_kernel(*refs):
    i = pl.program_id(0)
    o_ref = refs[-1]
    n_groups = (len(refs) - 1) // 2
    H = refs[n_groups].shape[0]
    BS = refs[n_groups].shape[1]
    low_half = lax.broadcasted_iota(jnp.int32, (BS, LANES), 1) < ROW_WORDS
    for head in range(n_groups * H):
        got_ref, own_ref, h = refs[head // H], refs[n_groups + head // H], head % H
        tiles = [own_ref[h]]
        for r in range(MOBA_TOPK):
            tiles.append(jnp.where(r < i, got_ref[h, r], 0.0))
        lses = [tiles[0][:, ROW_WORDS:ROW_WORDS + 1]]
        lses += [jnp.where(r < i, tiles[r + 1][:, ROW_WORDS:ROW_WORDS + 1], NEG_BIG)
                 for r in range(MOBA_TOPK)]
        top = functools.reduce(jnp.maximum, lses)
        w = [jnp.exp2(x - top) for x in lses]
        inv = 1.0 / sum(w)
        first = jnp.zeros((BS, LANES), F32)
        second = jnp.zeros((BS, LANES), F32)
        for wk, tile in zip(w, tiles):
            words = pltpu.bitcast(tile, jnp.uint32)
            first = first + wk * pltpu.bitcast(words & jnp.uint32(0xFFFF0000), F32)
            second = second + wk * pltpu.bitcast(words << 16, F32)
        o = jnp.where(low_half, first, pltpu.roll(second, ROW_WORDS, 1)) * inv
        o_ref[:, head * A_HEAD_DIM:(head + 1) * A_HEAD_DIM] = o.astype(BF16)


def _combine(gots, owns):
    H, S, _ = owns[0].shape
    BS = MOBA_BLOCK
    return pl.pallas_call(
        _combine_kernel,
        grid=(S // BS,),
        in_specs=([pl.BlockSpec((H, MOBA_TOPK, BS, LANES), lambda i: (0, 0, i, 0))] * len(gots)
                  + [pl.BlockSpec((H, BS, LANES), lambda i: (0, i, 0))] * len(owns)),
        out_specs=pl.BlockSpec((BS, A_WIDTH), lambda i: (i, 0)),
        out_shape=jax.ShapeDtypeStruct((S, A_WIDTH), BF16),
        compiler_params=_cparams(("parallel",)),
        name="moba_combine",
    )(*gots, *owns)


def _moba_routed(aq_hm, ak, avT, kmean):
    H, S, Dh = aq_hm.shape
    NB = S // MOBA_BLOCK
    hn = MOBA_HEADS_PER_STEP
    q_rows = aq_hm.reshape(H * S, Dh)
    gots, owns = [], []
    for h0 in range(0, H, hn):
        route, cnt, own = _route(aq_hm, ak, avT, kmean, h0, hn)
        cnt = cnt[:, :, 0].astype(jnp.int32)
        seg_tiles = (cnt + MOBA_BLOCK - 1) // MOBA_BLOCK
        seg_end = jnp.cumsum(seg_tiles, axis=1)
        seg_start = (seg_end - seg_tiles) * MOBA_BLOCK
        tile_ids = jnp.arange(ROUTE_TILES, dtype=jnp.int32)
        tile_blk = jnp.sum(tile_ids[None, :, None] >= seg_end[:, None, :], axis=2).astype(jnp.int32)
        tile_blk = jnp.where(tile_blk < NB, tile_blk, -1).reshape(hn * ROUTE_TILES)
        dest = _dest(seg_start, route)[:, :MOBA_TOPK, :]
        q_sorted = _sc_scatter(q_rows, h0 * S, hn * S, dest.transpose(1, 0, 2).reshape(-1),
                               hn * ROUTE_ROWS)
        results = _group(tile_blk, q_sorted, ak, avT, h0)
        gots.append(_sc_gather(results, dest.reshape(-1)).reshape(hn, MOBA_TOPK, S, LANES))
        owns.append(own)
    return _combine(gots, owns)


def _merge_kernel(hm_ref, ha_ref, gm_ref, ga_ref, x_ref, wm_ref, wa_ref, wo_ref, gain_ref,
                  out_ref):
    ym = jnp.dot(hm_ref[...], wm_ref[...], preferred_element_type=F32)
    ya = jnp.dot(ha_ref[...], wa_ref[...], preferred_element_type=F32)
    merged = _sigmoid(gm_ref[...].astype(F32)) * ym + _sigmoid(ga_ref[...].astype(F32)) * ya
    mix = jnp.dot(merged.astype(BF16), wo_ref[...], preferred_element_type=F32)
    out_ref[...] = x_ref[...] + _rms(mix, gain_ref[...])


def _merge(hm, ha, P, x, wm, wa, wo, gain, tm=256):
    S, D = x.shape
    const = pl.Buffered(1)
    return pl.pallas_call(
        _merge_kernel,
        grid=(S // tm,),
        in_specs=[
            pl.BlockSpec((tm, M_WIDTH), lambda i: (i, 0)),
            pl.BlockSpec((tm, A_WIDTH), lambda i: (i, 0)),
            pl.BlockSpec((tm, D), lambda i: (i, COL_GM // D)),
            pl.BlockSpec((tm, D), lambda i: (i, COL_GA // D)),
            pl.BlockSpec((tm, D), lambda i: (i, 0)),
            pl.BlockSpec((M_WIDTH, D), lambda i: (0, 0), pipeline_mode=const),
            pl.BlockSpec((A_WIDTH, D), lambda i: (0, 0), pipeline_mode=const),
            pl.BlockSpec((D, D), lambda i: (0, 0), pipeline_mode=const),
            pl.BlockSpec((1, D), lambda i: (0, 0)),
        ],
        out_specs=pl.BlockSpec((tm, D), lambda i: (i, 0)),
        out_shape=jax.ShapeDtypeStruct((S, D), F32),
        compiler_params=_cparams(("parallel",)),
        name="merge",
    )(hm, ha, P, P, x, wm, wa, wo, gain)


def _ffn_kernel(x_ref, gpre_ref, wu_ref, wd_ref, gpost_ref, out_ref, hn_ref, acc_ref):
    f = pl.program_id(1)

    @pl.when(f == 0)
    def _():
        hn_ref[...] = _rms(x_ref[...], gpre_ref[...]).astype(BF16)
        acc_ref[...] = jnp.zeros_like(acc_ref)

    u = jnp.dot(hn_ref[...], wu_ref[...], preferred_element_type=F32)
    u = jnp.square(jnp.maximum(u, 0.0)).astype(BF16)
    acc_ref[...] += jnp.dot(u, wd_ref[...], preferred_element_type=F32)

    @pl.when(f == pl.num_programs(1) - 1)
    def _():
        out_ref[...] = x_ref[...] + _rms(acc_ref[...], gpost_ref[...])


def _ffn(x, gpre, wu, wd, gpost, tm=512, tf=1024):
    S, D = x.shape
    Fd = wu.shape[1]
    return pl.pallas_call(
        _ffn_kernel,
        grid=(S // tm, Fd // tf),
        in_specs=[
            pl.BlockSpec((tm, D), lambda i, f: (i, 0)),
            pl.BlockSpec((1, D), lambda i, f: (0, 0)),
            pl.BlockSpec((D, tf), lambda i, f: (0, f)),
            pl.BlockSpec((tf, D), lambda i, f: (f, 0)),
            pl.BlockSpec((1, D), lambda i, f: (0, 0)),
        ],
        out_specs=pl.BlockSpec((tm, D), lambda i, f: (i, 0)),
        out_shape=jax.ShapeDtypeStruct((S, D), F32),
        scratch_shapes=[pltpu.VMEM((tm, D), BF16), pltpu.VMEM((tm, D), F32)],
        compiler_params=_cparams(("parallel", "arbitrary")),
        name="ffn",
    )(x, gpre, wu, wd, gpost)


def _layer(x, pos, norm_mix_pre, w_in, conv_w, conv_b, i_bias, f_bias, mlstm_norm,
           w_branch_m, w_branch_a, w_out, norm_mix_post, norm_ffn_pre, w_up, w_down,
           norm_ffn_post):
    S, D = x.shape
    o = 0
    pieces = {}
    w_in = w_in.astype(BF16)
    for name, width in (("mq", M_WIDTH), ("mk", M_WIDTH), ("mv", M_WIDTH), ("mo", M_WIDTH),
                        ("mi", M_HEADS), ("mf", M_HEADS), ("aq", A_WIDTH), ("ak", A_WIDTH),
                        ("av", A_WIDTH), ("gm", D), ("ga", D)):
        pieces[name] = w_in[:, o:o + width]
        o += width
    w_all = jnp.concatenate([pieces[n] for n in ("gm", "ga", "mq", "mk", "aq", "ak", "mv", "mo", "av")],
                            axis=1)
    w_gate = jnp.concatenate(
        [pieces["mi"], pieces["mf"], jnp.zeros((D, LANES - 2 * M_HEADS), BF16)], axis=1)
    gbias = jnp.concatenate([i_bias, f_bias, jnp.zeros((LANES - 2 * M_HEADS,), F32)])[None, :]
    half = jnp.arange(0, A_HEAD_DIM, 2, dtype=F32) / A_HEAD_DIM
    inv_freq = 1.0 / (ROPE_THETA ** half)
    invf = jnp.concatenate([inv_freq, inv_freq])[None, :]

    P, gate = _proj(x, norm_mix_pre[None, :], w_all, w_gate)
    mq, mkT, aq, ak, avT, kmean, gc, gt = _prep(
        P, gate, pos.reshape(S, 1), invf, conv_w, conv_b[None, :], gbias)
    hm = _mlstm(mq, mkT, P, gc, gt, mlstm_norm[None, :])
    ha = _moba_routed(aq, ak, avT, kmean.reshape(S // MOBA_BLOCK, A_WIDTH))
    x1 = _merge(hm, ha, P, x, w_branch_m.astype(BF16), w_branch_a.astype(BF16),
                w_out.astype(BF16), norm_mix_post[None, :])
    return _ffn(x1, norm_ffn_pre[None, :], w_up.astype(BF16), w_down.astype(BF16),
                norm_ffn_post[None, :])


def kernel(x, positions, norm_mix_pre, w_in, conv_w, conv_b, i_bias, f_bias, mlstm_norm,
           w_branch_m, w_branch_a, w_out, norm_mix_post, norm_ffn_pre, w_up, w_down,
           norm_ffn_post):
    B = x.shape[0]
    depth = w_in.shape[0]
    outs = []
    for b in range(B):
        xb = x[b]
        for l in range(depth):
            xb = _layer(xb, positions[b], norm_mix_pre[l], w_in[l], conv_w[l], conv_b[l],
                        i_bias[l], f_bias[l], mlstm_norm[l], w_branch_m[l], w_branch_a[l],
                        w_out[l], norm_mix_post[l], norm_ffn_pre[l], w_up[l], w_down[l],
                        norm_ffn_post[l])
        outs.append(xb)
    return outs[0][None] if B == 1 else jnp.stack(outs, axis=0)
```

```python
import functools

import jax
import jax.numpy as jnp
from jax import lax
from jax.experimental import pallas as pl
from jax.experimental.pallas import tpu as pltpu
from jax.experimental.pallas import tpu_sc as plsc

F32 = jnp.float32
BF16 = jnp.bfloat16

M_HEADS = 4
M_HEAD_DIM = 256
M_WIDTH = M_HEADS * M_HEAD_DIM
M_CHUNK = 128
CONV_WIDTH = 4
A_HEADS = 8
A_HEAD_DIM = 128
A_WIDTH = A_HEADS * A_HEAD_DIM
MOBA_BLOCK = 256
MOBA_TOPK = 3
ROPE_THETA = 10000.0
NORM_EPS = 1e-6

LANES = 128
SUBLANES = 8
VMEM_LIMIT = 56 * 1024 * 1024
NEG_BIG = -1e30
LOG2_E = 1.4426950408889634

COL_GM = 0
COL_GA = 2048
COL_MQK = 4096
COL_AQK = 6144
COL_MV = 8192
COL_MO = 9216
COL_AV = 10240
P_COLS = 11264


def _cparams(sem):
    return pltpu.CompilerParams(dimension_semantics=sem, vmem_limit_bytes=VMEM_LIMIT)


def _rms(x, gain):
    ms = jnp.mean(x * x, axis=-1, keepdims=True)
    return x * lax.rsqrt(ms + NORM_EPS) * gain


def _sigmoid(x):
    return 1.0 / (1.0 + jnp.exp(-x))


def _split3(x):
    hi = x.astype(BF16)
    r1 = x - hi.astype(F32)
    mid = r1.astype(BF16)
    lo = (r1 - mid.astype(F32)).astype(BF16)
    return hi, mid, lo


def _proj_kernel(x_ref, g_ref, w_ref, wg_ref, p_ref, gate_ref, xn_ref):
    @pl.when(pl.program_id(1) == 0)
    def _():
        xn = _rms(x_ref[...], g_ref[...]).astype(BF16)
        xn_ref[...] = xn
        gate_ref[...] = jnp.dot(xn, wg_ref[...], preferred_element_type=F32)

    p_ref[...] = jnp.dot(xn_ref[...], w_ref[...], preferred_element_type=F32).astype(BF16)


def _proj(x, gain, w_all, w_gate, tm=1024, tn=1024):
    S, D = x.shape
    N = w_all.shape[1]
    return pl.pallas_call(
        _proj_kernel,
        grid=(S // tm, N // tn),
        in_specs=[
            pl.BlockSpec((tm, D), lambda i, j: (i, 0)),
            pl.BlockSpec((1, D), lambda i, j: (0, 0)),
            pl.BlockSpec((D, tn), lambda i, j: (0, j)),
            pl.BlockSpec((D, LANES), lambda i, j: (0, 0)),
        ],
        out_specs=[
            pl.BlockSpec((tm, tn), lambda i, j: (i, j)),
            pl.BlockSpec((tm, LANES), lambda i, j: (i, 0)),
        ],
        out_shape=[
            jax.ShapeDtypeStruct((S, N), BF16),
            jax.ShapeDtypeStruct((S, LANES), F32),
        ],
        scratch_shapes=[pltpu.VMEM((tm, D), BF16)],
        compiler_params=_cparams(("parallel", "arbitrary")),
        name="proj",
    )(x, gain, w_all, w_gate)


PREP_ROWS = MOBA_BLOCK
PREP_COLS = 512
HALO_ROWS = 2 * SUBLANES


def _prep_kernel(pmk_ref, halo_ref, paqk_ref, pav_ref, gate_ref, pos_ref, invf_ref,
                 cw_ref, cb_ref, gb_ref,
                 mq_ref, mkT_ref, aq_ref, ak_ref, avT_ref, kmean_ref, gc_ref, gt_ref):
    i = pl.program_id(0)
    R = PREP_ROWS

    k_scale = M_HEAD_DIM ** -0.5
    for c0 in range(0, 2 * M_WIDTH, PREP_COLS):
        cs = slice(c0, c0 + PREP_COLS)
        prev = halo_ref[:, cs].astype(F32)[HALO_ROWS - SUBLANES:, :]
        prev = jnp.where(i == 0, jnp.zeros_like(prev), prev)
        ext = jnp.concatenate([pmk_ref[:, cs].astype(F32), prev], axis=0)
        acc = cw_ref[0:1, cs] * ext
        for j in range(1, CONV_WIDTH):
            acc = pltpu.roll(acc, 1, 0) + cw_ref[j:j + 1, cs] * ext
        acc = acc[0:R, :] + cb_ref[:, cs]
        y = acc * _sigmoid(acc)
        if c0 < M_WIDTH:
            mq_ref[:, cs] = y.astype(BF16)
        else:
            ks = slice(c0 - M_WIDTH, c0 - M_WIDTH + PREP_COLS)
            mkT_ref[ks, :] = (y * k_scale).T.astype(BF16)

    ang = pos_ref[...].astype(F32) * invf_ref[...]
    cos = jnp.cos(ang)
    lane = lax.broadcasted_iota(jnp.int32, (R, A_HEAD_DIM), 1)
    sin_signed = jnp.where(lane < A_HEAD_DIM // 2, -1.0, 1.0) * jnp.sin(ang)
    for h in range(2 * A_HEADS):
        hs = slice(h * A_HEAD_DIM, (h + 1) * A_HEAD_DIM)
        xh = paqk_ref[:, hs].astype(F32)
        yh = xh * cos + pltpu.roll(xh, A_HEAD_DIM // 2, 1) * sin_signed
        if h < A_HEADS:
            aq_ref[h] = yh
        else:
            ko = slice((h - A_HEADS) * A_HEAD_DIM, (h - A_HEADS + 1) * A_HEAD_DIM)
            ak_ref[:, ko] = yh.astype(BF16)
            kmean_ref[0, :, ko] = jnp.mean(yh, axis=0, keepdims=True)
    for c0 in range(0, A_WIDTH, PREP_COLS):
        avT_ref[c0:c0 + PREP_COLS, :] = pav_ref[:, c0:c0 + PREP_COLS].astype(F32).T.astype(BF16)

    g = gate_ref[...] + gb_ref[...]
    log_f = jnp.minimum(g, 0.0) - jnp.log1p(jnp.exp(-jnp.abs(g)))
    r_i = lax.broadcasted_iota(jnp.int32, (R, R), 0)
    c_i = lax.broadcasted_iota(jnp.int32, (R, R), 1)
    tri = ((r_i >= c_i) & ((r_i // M_CHUNK) == (c_i // M_CHUNK))).astype(BF16)
    hi, mid, lo = _split3(log_f)
    csum = (jnp.dot(tri, hi, preferred_element_type=F32)
            + jnp.dot(tri, mid, preferred_element_type=F32)
            + jnp.dot(tri, lo, preferred_element_type=F32))
    glane = lax.broadcasted_iota(jnp.int32, (R, LANES), 1)
    gc = jnp.where(glane < M_HEADS, g, csum)
    gc_ref[...] = gc
    gt_ref[...] = gc.T[0:SUBLANES, :]


def _prep(P, gate, pos, invf, conv_w, conv_b, gbias):
    S = P.shape[0]
    R = PREP_ROWS
    nb = S // R
    halo_blocks = R // HALO_ROWS
    return pl.pallas_call(
        _prep_kernel,
        grid=(nb,),
        in_specs=[
            pl.BlockSpec((R, 2 * M_WIDTH), lambda i: (i, COL_MQK // (2 * M_WIDTH))),
            pl.BlockSpec((HALO_ROWS, 2 * M_WIDTH),
                         lambda i: (jnp.maximum(i * halo_blocks - 1, 0), COL_MQK // (2 * M_WIDTH))),
            pl.BlockSpec((R, 2 * A_WIDTH), lambda i: (i, COL_AQK // (2 * A_WIDTH))),
            pl.BlockSpec((R, A_WIDTH), lambda i: (i, COL_AV // A_WIDTH)),
            pl.BlockSpec((R, LANES), lambda i: (i, 0)),
            pl.BlockSpec((R, 1), lambda i: (i, 0)),
            pl.BlockSpec((1, A_HEAD_DIM), lambda i: (0, 0)),
            pl.BlockSpec((CONV_WIDTH, 2 * M_WIDTH), lambda i: (0, 0)),
            pl.BlockSpec((1, 2 * M_WIDTH), lambda i: (0, 0)),
            pl.BlockSpec((1, LANES), lambda i: (0, 0)),
        ],
        out_specs=[
            pl.BlockSpec((R, M_WIDTH), lambda i: (i, 0)),
            pl.BlockSpec((M_WIDTH, R), lambda i: (0, i)),
            pl.BlockSpec((A_HEADS, R, A_HEAD_DIM), lambda i: (0, i, 0)),
            pl.BlockSpec((R, A_WIDTH), lambda i: (i, 0)),
            pl.BlockSpec((A_WIDTH, R), lambda i: (0, i)),
            pl.BlockSpec((1, 1, A_WIDTH), lambda i: (i, 0, 0)),
            pl.BlockSpec((R, LANES), lambda i: (i, 0)),
            pl.BlockSpec((SUBLANES, R), lambda i: (0, i)),
        ],
        out_shape=[
            jax.ShapeDtypeStruct((S, M_WIDTH), BF16),
            jax.ShapeDtypeStruct((M_WIDTH, S), BF16),
            jax.ShapeDtypeStruct((A_HEADS, S, A_HEAD_DIM), F32),
            jax.ShapeDtypeStruct((S, A_WIDTH), BF16),
            jax.ShapeDtypeStruct((A_WIDTH, S), BF16),
            jax.ShapeDtypeStruct((nb, 1, A_WIDTH), F32),
            jax.ShapeDtypeStruct((S, LANES), F32),
            jax.ShapeDtypeStruct((SUBLANES, S), F32),
        ],
        compiler_params=_cparams(("parallel",)),
        name="prep",
    )(P, P, P, P, gate, pos, invf, conv_w, conv_b, gbias)


M_AUG = M_HEAD_DIM + LANES


def _mlstm_kernel(q_ref, kT_ref, v_ref, mo_ref, gc_ref, gt_ref, gain_ref, out_ref,
                  c_ref, m_ref):
    @pl.when(pl.program_id(0) == 0)
    def _():
        c_ref[...] = jnp.zeros_like(c_ref)
        m_ref[...] = jnp.zeros_like(m_ref)

    L = M_CHUNK
    D = M_HEAD_DIM
    row = lax.broadcasted_iota(jnp.int32, (L, L), 0)
    col = lax.broadcasted_iota(jnp.int32, (L, L), 1)
    causal = row >= col
    ones_col = (lax.broadcasted_iota(jnp.int32, (L, LANES), 1) == 0).astype(BF16)

    heads = range(M_HEADS)
    hsl = [slice(h * D, (h + 1) * D) for h in heads]
    q = [q_ref[:, hsl[h]] for h in heads]
    kT = [kT_ref[hsl[h], :] for h in heads]
    v_aug = [jnp.concatenate([v_ref[:, hsl[h]], ones_col], axis=1) for h in heads]
    b_c = [gc_ref[:, M_HEADS + h:M_HEADS + h + 1] for h in heads]
    b_r = [gt_ref[M_HEADS + h:M_HEADS + h + 1, :] for h in heads]
    u_r = [gt_ref[h:h + 1, :] - b_r[h] for h in heads]
    f_tot = [b_r[h][:, L - 1:L] for h in heads]
    m_prev = [m_ref[h:h + 1, 0:1] for h in heads]

    s_qk = [jnp.dot(q[h], kT[h], preferred_element_type=F32) for h in heads]
    q_c = [jnp.dot(q[h], c_ref[h].astype(BF16), preferred_element_type=F32) for h in heads]

    for h in heads:
        w_r = f_tot[h] + u_r[h]
        m_loc = jnp.max(w_r, axis=1, keepdims=True)
        m_new = jnp.maximum(f_tot[h] + m_prev[h], m_loc)
        a = jnp.exp(f_tot[h] + m_prev[h] - m_new)
        e_r = jnp.exp(w_r - m_new)
        keT = (kT[h].astype(F32) * e_r).astype(BF16)
        c_ref[h] = a * c_ref[h] + jnp.dot(keT, v_aug[h], preferred_element_type=F32)
        m_ref[h:h + 1, :] = jnp.broadcast_to(m_new, (1, LANES))

    for h in heads:
        d_log = jnp.where(causal, b_c[h] + u_r[h], -jnp.inf)
        a_log = b_c[h] + m_prev[h]
        m_t = jnp.maximum(a_log, jnp.max(d_log, axis=1, keepdims=True))
        s_ts = s_qk[h] * jnp.exp(d_log - m_t)
        inter = jnp.exp(a_log - m_t)
        r = inter * q_c[h] + jnp.dot(s_ts.astype(BF16), v_aug[h], preferred_element_type=F32)
        num = r[:, :D]
        den = r[:, D:D + 1]
        hh = num / jnp.maximum(jnp.abs(den), jnp.exp(-m_t))
        hn = hh * lax.rsqrt(jnp.mean(hh * hh, axis=-1, keepdims=True) + NORM_EPS)
        out_ref[:, hsl[h]] = (hn * gain_ref[:, hsl[h]]
                              * _sigmoid(mo_ref[:, hsl[h]].astype(F32))).astype(BF16)


def _mlstm(mq, mkT, P, gc, gt, gain):
    S = mq.shape[0]
    L = M_CHUNK
    return pl.pallas_call(
        _mlstm_kernel,
        grid=(S // L,),
        in_specs=[
            pl.BlockSpec((L, M_WIDTH), lambda c: (c, 0)),
            pl.BlockSpec((M_WIDTH, L), lambda c: (0, c)),
            pl.BlockSpec((L, M_WIDTH), lambda c: (c, COL_MV // M_WIDTH)),
            pl.BlockSpec((L, M_WIDTH), lambda c: (c, COL_MO // M_WIDTH)),
            pl.BlockSpec((L, LANES), lambda c: (c, 0)),
            pl.BlockSpec((SUBLANES, L), lambda c: (0, c)),
            pl.BlockSpec((1, M_WIDTH), lambda c: (0, 0)),
        ],
        out_specs=pl.BlockSpec((L, M_WIDTH), lambda c: (c, 0)),
        out_shape=jax.ShapeDtypeStruct((S, M_WIDTH), BF16),
        scratch_shapes=[
            pltpu.VMEM((M_HEADS, M_HEAD_DIM, M_AUG), F32),
            pltpu.VMEM((SUBLANES, LANES), F32),
        ],
        compiler_params=_cparams(("arbitrary",)),
        name="mlstm",
    )(mq, mkT, P, P, gc, gt, gain)


MOBA_HEADS_PER_STEP = 4


ROUTE_TILES = 264
ROUTE_ROWS = ROUTE_TILES * MOBA_BLOCK
GROUP_TILES_PER_STEP = 8
ROW_WORDS = A_HEAD_DIM // 2
COMBINE_ROWS = 256


def _pack_rows(o_t, lse):
    q = o_t.shape[1]
    hi = pltpu.bitcast(o_t[:ROW_WORDS].astype(BF16).astype(F32), jnp.uint32)
    lo = pltpu.bitcast(o_t[ROW_WORDS:].astype(BF16).astype(F32), jnp.uint32)
    words = pltpu.bitcast(hi | (lo >> 16), F32)
    tail = jnp.concatenate([jnp.broadcast_to(lse, (SUBLANES, q)),
                            jnp.zeros((LANES - ROW_WORDS - SUBLANES, q), F32)], axis=0)
    return jnp.concatenate([words, tail], axis=0).T


def _dot_nt(a, b):
    return lax.dot_general(a, b, (((1,), (1,)), ((), ())), preferred_element_type=F32)


def _block_partial(k_blk, v_t_blk, qs, ones_rows, causal_mask=None):
    s = _dot_nt(k_blk, qs)
    if causal_mask is not None:
        s = jnp.where(causal_mask, s, NEG_BIG)
    sb = s.astype(BF16)
    m = jnp.max(sb, axis=0, keepdims=True)
    p = jnp.exp2(sb - m)
    r = jnp.dot(jnp.concatenate([v_t_blk, ones_rows], axis=0), p, preferred_element_type=F32)
    dh = v_t_blk.shape[0]
    l = r[dh:dh + 1, :]
    return r[:dh, :] / l, m.astype(F32) + jnp.log2(l)


def _route_kernel(q_ref, k_ref, vT_ref, km_ref, route_ref, cnt_ref, own_ref, cnt_acc):
    i = pl.program_id(1)
    BS = MOBA_BLOCK
    Dh = A_HEAD_DIM
    NB = km_ref.shape[0]
    G = MOBA_HEADS_PER_STEP
    own = pl.multiple_of(i * BS, BS)
    blk = lax.broadcasted_iota(jnp.int32, (NB, BS), 0)
    kpos = lax.broadcasted_iota(jnp.int32, (BS, BS), 0)
    qpos = lax.broadcasted_iota(jnp.int32, (BS, BS), 1)
    earlier = (kpos < qpos).astype(BF16)
    ones_rows = jnp.ones((2 * SUBLANES, BS), BF16)

    @pl.when(i == 0)
    def _():
        cnt_acc[...] = jnp.zeros_like(cnt_acc)

    q = [q_ref[g] for g in range(G)]
    qs = [(q[g] * (Dh ** -0.5 * LOG2_E)).astype(BF16) for g in range(G)]
    parts = [_block_partial(k_ref[pl.ds(own, BS), g * Dh:(g + 1) * Dh],
                            vT_ref[g * Dh:(g + 1) * Dh, pl.ds(own, BS)], qs[g], ones_rows,
                            kpos <= qpos) for g in range(G)]

    gates = []
    for g in range(G):
        hs = slice(g * Dh, (g + 1) * Dh)
        km = km_ref[:, hs]
        kh = km.astype(BF16)
        kl = (km - kh.astype(F32)).astype(BF16)
        qh = q[g].astype(BF16)
        ql = (q[g] - qh.astype(F32)).astype(BF16)
        gate = _dot_nt(kh, qh) + _dot_nt(kh, ql) + _dot_nt(kl, qh)
        gates.append(jnp.where(blk < i, gate, -jnp.inf))
    picks = [[] for _ in range(G)]
    rows = [[] for _ in range(G)]
    for r in range(MOBA_TOPK):
        for g in range(G):
            mx = jnp.max(gates[g], axis=0, keepdims=True)
            idx = jnp.min(jnp.where(gates[g] == mx, blk, NB), axis=0, keepdims=True)
            idx = jnp.where(r < i, idx, -1)
            pick = blk == idx
            gates[g] = jnp.where(pick, -jnp.inf, gates[g])
            picks[g].append(pick)
            rows[g].append(idx)
    for g in range(G):
        onehot = sum(p.astype(F32) for p in picks[g])
        before = cnt_acc[g][:, 0:1] + jnp.dot(onehot.astype(BF16), earlier, preferred_element_type=F32)
        for r in range(MOBA_TOPK):
            rank = jnp.sum(jnp.where(picks[g][r], before, 0.0), axis=0, keepdims=True)
            rows[g].append(rank.astype(jnp.int32))
        rows[g].append(jnp.zeros((SUBLANES - 2 * MOBA_TOPK, BS), jnp.int32))
        route_ref[g] = jnp.concatenate(rows[g], axis=0)
        cnt_new = cnt_acc[g] + jnp.sum(onehot, axis=1, keepdims=True)
        cnt_acc[g] = cnt_new
        cnt_ref[g] = cnt_new
        own_ref[g] = _pack_rows(*parts[g])


def _route(aq_hm, ak, avT, kmean, h0, H):
    _, S, Dh = aq_hm.shape
    BS = MOBA_BLOCK
    NB = S // BS
    G = MOBA_HEADS_PER_STEP
    W = G * Dh
    hb = h0 // G
    once = pl.Buffered(1)
    return pl.pallas_call(
        _route_kernel,
        grid=(H // G, NB),
        in_specs=[
            pl.BlockSpec((G, BS, Dh), lambda h, i: (hb + h, i, 0)),
            pl.BlockSpec((S, W), lambda h, i: (0, hb + h), pipeline_mode=once),
            pl.BlockSpec((W, S), lambda h, i: (hb + h, 0), pipeline_mode=once),
            pl.BlockSpec((NB, W), lambda h, i: (0, hb + h)),
        ],
        out_specs=[
            pl.BlockSpec((G, SUBLANES, BS), lambda h, i: (h, 0, i)),
            pl.BlockSpec((G, NB, LANES), lambda h, i: (h, 0, 0)),
            pl.BlockSpec((G, BS, LANES), lambda h, i: (h, i, 0)),
        ],
        out_shape=[
            jax.ShapeDtypeStruct((H, SUBLANES, S), jnp.int32),
            jax.ShapeDtypeStruct((H, NB, LANES), F32),
            jax.ShapeDtypeStruct((H, S, LANES), F32),
        ],
        scratch_shapes=[pltpu.VMEM((G, NB, LANES), F32)],
        compiler_params=_cparams(("parallel", "arbitrary")),
        name="moba_route",
    )(aq_hm, ak, avT, kmean)


def _dest_kernel(seg_ref, route_ref, dest_ref):
    h = pl.program_id(0)
    NB = seg_ref.shape[1]
    r = route_ref[0]
    blk = r[0:MOBA_TOPK, :]
    rank = r[MOBA_TOPK:2 * MOBA_TOPK, :]
    base = h * ROUTE_ROWS
    dest = jnp.full(blk.shape, base + ROUTE_ROWS - 1, jnp.int32)
    for j in range(NB):
        dest = jnp.where(blk == j, base + seg_ref[h, j] + rank, dest)
    dest_ref[0] = jnp.concatenate(
        [dest, jnp.zeros((SUBLANES - MOBA_TOPK, dest.shape[1]), jnp.int32)], axis=0)


def _dest(seg_start, route):
    H, _, S = route.shape
    return pl.pallas_call(
        _dest_kernel,
        grid_spec=pltpu.PrefetchScalarGridSpec(
            num_scalar_prefetch=1,
            grid=(H,),
            in_specs=[pl.BlockSpec((1, SUBLANES, S), lambda h, seg: (h, 0, 0))],
            out_specs=pl.BlockSpec((1, SUBLANES, S), lambda h, seg: (h, 0, 0)),
        ),
        out_shape=jax.ShapeDtypeStruct((H, SUBLANES, S), jnp.int32),
        compiler_params=_cparams(("parallel",)),
        name="moba_dest",
    )(seg_start, route)


SC_WINDOW = 256


def _sc_mesh():
    return plsc.VectorSubcoreMesh(core_axis_name="c", subcore_axis_name="s")


def _sc_scatter(rows, row0, n_rows, idx, n_out):
    W = rows.shape[1]
    M = idx.shape[0]
    n_win = n_rows // SC_WINDOW
    win0 = row0 // SC_WINDOW

    @pl.kernel(out_type=jax.ShapeDtypeStruct((n_out, W), rows.dtype), mesh=_sc_mesh())
    def k(x_hbm, i_hbm, o_hbm):
        def body(x_vmem, i_vmem):
            pltpu.sync_copy(x_vmem, o_hbm.at[i_vmem.at[0]])

        pltpu.emit_pipeline(
            body,
            grid=(M // SC_WINDOW,),
            in_specs=[pl.BlockSpec((SC_WINDOW, W), lambda w: (win0 + lax.rem(w, n_win), 0)),
                      pl.BlockSpec((1, SC_WINDOW), lambda w: (0, w))],
            out_specs=[],
            core_axis_name=("c", "s"),
            dimension_semantics=(pltpu.PARALLEL,),
        )(x_hbm, i_hbm)

    return k(rows, idx.reshape(1, M))


def _sc_gather(table, idx):
    M = idx.shape[0]
    W = table.shape[1]

    @pl.kernel(out_type=jax.ShapeDtypeStruct((M, W), table.dtype), mesh=_sc_mesh())
    def k(x_hbm, i_hbm, o_hbm):
        def body(i_vmem, o_vmem):
            pltpu.sync_copy(x_hbm.at[i_vmem.at[0]], o_vmem)

        pltpu.emit_pipeline(
            body,
            grid=(M // SC_WINDOW,),
            in_specs=[pl.BlockSpec((1, SC_WINDOW), lambda w: (0, w))],
            out_specs=[pl.BlockSpec((SC_WINDOW, W), lambda w: (w, 0))],
            core_axis_name=("c", "s"),
            dimension_semantics=(pltpu.PARALLEL,),
        )(i_hbm, o_hbm)

    return k(table, idx.reshape(1, M))


def _group_kernel(tile_blk_ref, q_ref, k_ref, vT_ref, out_ref, s_ref, m_ref):
    h = pl.program_id(0)
    u = pl.program_id(1)
    last = pl.num_programs(1) - 2
    BS = MOBA_BLOCK
    Dh = A_HEAD_DIM
    T = GROUP_TILES_PER_STEP
    ones_rows = jnp.ones((2 * SUBLANES, BS), BF16)
    new = lax.rem(u, 2)
    old = 1 - new

    @pl.when((h == 0) & (u == 0))
    def _():
        s_ref[...] = jnp.zeros_like(s_ref)
        m_ref[...] = jnp.zeros_like(m_ref)

    def block_start(group, c):
        j = jnp.maximum(tile_blk_ref[h * ROUTE_TILES + group * T + c], 0)
        return pl.multiple_of(j * BS, BS)

    g_new = jnp.minimum(u, last)
    g_old = jnp.maximum(u - 1, 0)
    @pl.when(tile_blk_ref[h * ROUTE_TILES + g_old * T] >= 0)
    def _():
        ms, rs = [], []
        for c in range(T):
            m = m_ref[old, c]
            p = jnp.exp2(s_ref[old, c] - m.astype(BF16))
            v_aug = jnp.concatenate([vT_ref[:, pl.ds(block_start(g_old, c), BS)], ones_rows], axis=0)
            ms.append(m)
            rs.append(jnp.dot(v_aug, p, preferred_element_type=F32))
        s_new = []
        for c in range(T):
            qs = (q_ref[c * BS:(c + 1) * BS, :] * (Dh ** -0.5 * LOG2_E)).astype(BF16)
            s_new.append(_dot_nt(k_ref[pl.ds(block_start(g_new, c), BS), :], qs))
        for c in range(T):
            l = rs[c][Dh:Dh + 1, :]
            out_ref[c * BS:(c + 1) * BS, :] = _pack_rows(rs[c][:Dh, :] / l, ms[c] + jnp.log2(l))
            sb = s_new[c].astype(BF16)
            s_ref[new, c] = sb
            m_ref[new, c] = jnp.max(sb, axis=0, keepdims=True).astype(F32)


def _group(tile_blk, q_sorted, ak, avT, h0):
    S = ak.shape[0]
    H = q_sorted.shape[0] // ROUTE_ROWS
    Dh = A_HEAD_DIM
    BS = MOBA_BLOCK
    T = GROUP_TILES_PER_STEP
    steps = ROUTE_TILES // T
    rows = T * BS
    return pl.pallas_call(
        _group_kernel,
        grid_spec=pltpu.PrefetchScalarGridSpec(
            num_scalar_prefetch=1,
            grid=(H, steps + 1),
            in_specs=[
                pl.BlockSpec((rows, LANES), lambda h, u, tb: (h * steps + jnp.minimum(u, steps - 1), 0)),
                pl.BlockSpec((S, Dh), lambda h, u, tb: (0, h0 + h)),
                pl.BlockSpec((Dh, S), lambda h, u, tb: (h0 + h, 0)),
            ],
            out_specs=pl.BlockSpec((rows, LANES), lambda h, u, tb: (h * steps + jnp.maximum(u - 1, 0), 0)),
            scratch_shapes=[pltpu.VMEM((2, T, BS, BS), BF16), pltpu.VMEM((2, T, 1, BS), F32)],
        ),
        out_shape=jax.ShapeDtypeStruct((H * ROUTE_ROWS, LANES), F32),
        compiler_params=_cparams(("arbitrary", "arbitrary")),
        name="moba_group",
    )(tile_blk, q_sorted, ak, avT)


def _combine_kernel(*refs):
    i = pl.program_id(0)
    o_ref = refs[-1]
    n_groups = (len(refs) - 1) // 2
    H = refs[n_groups].shape[0]
    BS = refs[n_groups].shape[1]
    RC = COMBINE_ROWS
    low_half = lax.broadcasted_iota(jnp.int32, (RC, LANES), 1) < ROW_WORDS
    for head in range(n_groups * H):
        got_ref, own_ref, h = refs[head // H], refs[n_groups + head // H], head % H
        for r0 in range(0, BS, RC):
            rs = slice(r0, r0 + RC)
            tiles = [own_ref[h, rs, :]]
            for r in range(MOBA_TOPK):
                tiles.append(jnp.where(r < i, got_ref[h, r, rs, :], 0.0))
            lses = [tiles[0][:, ROW_WORDS:ROW_WORDS + 1]]
            lses += [jnp.where(r < i, tiles[r + 1][:, ROW_WORDS:ROW_WORDS + 1], NEG_BIG)
                     for r in range(MOBA_TOPK)]
            top = functools.reduce(jnp.maximum, lses)
            w = [jnp.exp2(x - top) for x in lses]
            inv = 1.0 / sum(w)
            first = jnp.zeros((RC, LANES), F32)
            second = jnp.zeros((RC, LANES), F32)
            for wk, tile in zip(w, tiles):
                words = pltpu.bitcast(tile, jnp.uint32)
                first = first + wk * pltpu.bitcast(words & jnp.uint32(0xFFFF0000), F32)
                second = second + wk * pltpu.bitcast(words << 16, F32)
            o = jnp.where(low_half, first, pltpu.roll(second, ROW_WORDS, 1)) * inv
            o_ref[rs, head * A_HEAD_DIM:(head + 1) * A_HEAD_DIM] = o.astype(BF16)


def _combine(gots, owns):
    H, S, _ = owns[0].shape
    BS = MOBA_BLOCK
    return pl.pallas_call(
        _combine_kernel,
        grid=(S // BS,),
        in_specs=([pl.BlockSpec((H, MOBA_TOPK, BS, LANES), lambda i: (0, 0, i, 0))] * len(gots)
                  + [pl.BlockSpec((H, BS, LANES), lambda i: (0, i, 0))] * len(owns)),
        out_specs=pl.BlockSpec((BS, A_WIDTH), lambda i: (i, 0)),
        out_shape=jax.ShapeDtypeStruct((S, A_WIDTH), BF16),
        compiler_params=_cparams(("parallel",)),
        name="moba_combine",
    )(*gots, *owns)


def _moba_routed(aq_hm, ak, avT, kmean):
    H, S, Dh = aq_hm.shape
    NB = S // MOBA_BLOCK
    hn = MOBA_HEADS_PER_STEP
    q_rows = aq_hm.reshape(H * S, Dh)
    gots, owns = [], []
    for h0 in range(0, H, hn):
        route, cnt, own = _route(aq_hm, ak, avT, kmean, h0, hn)
        cnt = cnt[:, :, 0].astype(jnp.int32)
        seg_tiles = (cnt + MOBA_BLOCK - 1) // MOBA_BLOCK
        seg_end = jnp.cumsum(seg_tiles, axis=1)
        seg_start = (seg_end - seg_tiles) * MOBA_BLOCK
        tile_ids = jnp.arange(ROUTE_TILES, dtype=jnp.int32)
        tile_blk = jnp.sum(tile_ids[None, :, None] >= seg_end[:, None, :], axis=2).astype(jnp.int32)
        tile_blk = jnp.where(tile_blk < NB, tile_blk, -1).reshape(hn * ROUTE_TILES)
        dest = _dest(seg_start, route)[:, :MOBA_TOPK, :]
        q_sorted = _sc_scatter(q_rows, h0 * S, hn * S, dest.transpose(1, 0, 2).reshape(-1),
                               hn * ROUTE_ROWS)
        results = _group(tile_blk, q_sorted, ak, avT, h0)
        gots.append(_sc_gather(results, dest.reshape(-1)).reshape(hn, MOBA_TOPK, S, LANES))
        owns.append(own)
    return _combine(gots, owns)


def _merge_kernel(hm_ref, ha_ref, gm_ref, ga_ref, x_ref, wm_ref, wa_ref, wo_ref, gain_ref,
                  out_ref):
    ym = jnp.dot(hm_ref[...], wm_ref[...], preferred_element_type=F32)
    ya = jnp.dot(ha_ref[...], wa_ref[...], preferred_element_type=F32)
    merged = _sigmoid(gm_ref[...].astype(F32)) * ym + _sigmoid(ga_ref[...].astype(F32)) * ya
    mix = jnp.dot(merged.astype(BF16), wo_ref[...], preferred_element_type=F32)
    out_ref[...] = x_ref[...] + _rms(mix, gain_ref[...])


def _merge(hm, ha, P, x, wm, wa, wo, gain, tm=256):
    S, D = x.shape
    const = pl.Buffered(1)
    return pl.pallas_call(
        _merge_kernel,
        grid=(S // tm,),
        in_specs=[
            pl.BlockSpec((tm, M_WIDTH), lambda i: (i, 0)),
            pl.BlockSpec((tm, A_WIDTH), lambda i: (i, 0)),
            pl.BlockSpec((tm, D), lambda i: (i, COL_GM // D)),
            pl.BlockSpec((tm, D), lambda i: (i, COL_GA // D)),
            pl.BlockSpec((tm, D), lambda i: (i, 0)),
            pl.BlockSpec((M_WIDTH, D), lambda i: (0, 0), pipeline_mode=const),
            pl.BlockSpec((A_WIDTH, D), lambda i: (0, 0), pipeline_mode=const),
            pl.BlockSpec((D, D), lambda i: (0, 0), pipeline_mode=const),
            pl.BlockSpec((1, D), lambda i: (0, 0)),
        ],
        out_specs=pl.BlockSpec((tm, D), lambda i: (i, 0)),
        out_shape=jax.ShapeDtypeStruct((S, D), F32),
        compiler_params=_cparams(("parallel",)),
        name="merge",
    )(hm, ha, P, P, x, wm, wa, wo, gain)


def _ffn_kernel(x_ref, gpre_ref, wu_ref, wd_ref, gpost_ref, out_ref, hn_ref, acc_ref):
    f = pl.program_id(1)

    @pl.when(f == 0)
    def _():
        hn_ref[...] = _rms(x_ref[...], gpre_ref[...]).astype(BF16)
        acc_ref[...] = jnp.zeros_like(acc_ref)

    u = jnp.dot(hn_ref[...], wu_ref[...], preferred_element_type=F32)
    u = jnp.square(jnp.maximum(u, 0.0)).astype(BF16)
    acc_ref[...] += jnp.dot(u, wd_ref[...], preferred_element_type=F32)

    @pl.when(f == pl.num_programs(1) - 1)
    def _():
        out_ref[...] = x_ref[...] + _rms(acc_ref[...], gpost_ref[...])


def _ffn(x, gpre, wu, wd, gpost, tm=512, tf=1024):
    S, D = x.shape
    Fd = wu.shape[1]
    return pl.pallas_call(
        _ffn_kernel,
        grid=(S // tm, Fd // tf),
        in_specs=[
            pl.BlockSpec((tm, D), lambda i, f: (i, 0)),
            pl.BlockSpec((1, D), lambda i, f: (0, 0)),
            pl.BlockSpec((D, tf), lambda i, f: (0, f)),
            pl.BlockSpec((tf, D), lambda i, f: (f, 0)),
            pl.BlockSpec((1, D), lambda i, f: (0, 0)),
        ],
        out_specs=pl.BlockSpec((tm, D), lambda i, f: (i, 0)),
        out_shape=jax.ShapeDtypeStruct((S, D), F32),
        scratch_shapes=[pltpu.VMEM((tm, D), BF16), pltpu.VMEM((tm, D), F32)],
        compiler_params=_cparams(("parallel", "arbitrary")),
        name="ffn",
    )(x, gpre, wu, wd, gpost)


def _layer(x, pos, norm_mix_pre, w_in, conv_w, conv_b, i_bias, f_bias, mlstm_norm,
           w_branch_m, w_branch_a, w_out, norm_mix_post, norm_ffn_pre, w_up, w_down,
           norm_ffn_post):
    S, D = x.shape
    o = 0
    pieces = {}
    w_in = w_in.astype(BF16)
    for name, width in (("mq", M_WIDTH), ("mk", M_WIDTH), ("mv", M_WIDTH), ("mo", M_WIDTH),
                        ("mi", M_HEADS), ("mf", M_HEADS), ("aq", A_WIDTH), ("ak", A_WIDTH),
                        ("av", A_WIDTH), ("gm", D), ("ga", D)):
        pieces[name] = w_in[:, o:o + width]
        o += width
    w_all = jnp.concatenate([pieces[n] for n in ("gm", "ga", "mq", "mk", "aq", "ak", "mv", "mo", "av")],
                            axis=1)
    w_gate = jnp.concatenate(
        [pieces["mi"], pieces["mf"], jnp.zeros((D, LANES - 2 * M_HEADS), BF16)], axis=1)
    gbias = jnp.concatenate([i_bias, f_bias, jnp.zeros((LANES - 2 * M_HEADS,), F32)])[None, :]
    half = jnp.arange(0, A_HEAD_DIM, 2, dtype=F32) / A_HEAD_DIM
    inv_freq = 1.0 / (ROPE_THETA ** half)
    invf = jnp.concatenate([inv_freq, inv_freq])[None, :]

    P, gate = _proj(x, norm_mix_pre[None, :], w_all, w_gate)
    mq, mkT, aq, ak, avT, kmean, gc, gt = _prep(
        P, gate, pos.reshape(S, 1), invf, conv_w, conv_b[None, :], gbias)
    hm = _mlstm(mq, mkT, P, gc, gt, mlstm_norm[None, :])
    ha = _moba_routed(aq, ak, avT, kmean.reshape(S // MOBA_BLOCK, A_WIDTH))
    x1 = _merge(hm, ha, P, x, w_branch_m.astype(BF16), w_branch_a.astype(BF16),
                w_out.astype(BF16), norm_mix_post[None, :])
    return _ffn(x1, norm_ffn_pre[None, :], w_up.astype(BF16), w_down.astype(BF16),
                norm_ffn_post[None, :])


def kernel(x, positions, norm_mix_pre, w_in, conv_w, conv_b, i_bias, f_bias, mlstm_norm,
           w_branch_m, w_branch_a, w_out, norm_mix_post, norm_ffn_pre, w_up, w_down,
           norm_ffn_post):
    B = x.shape[0]
    depth = w_in.shape[0]
    outs = []
    for b in range(B):
        xb = x[b]
        for l in range(depth):
            xb = _layer(xb, positions[b], norm_mix_pre[l], w_in[l], conv_w[l], conv_b[l],
                        i_bias[l], f_bias[l], mlstm_norm[l], w_branch_m[l], w_branch_a[l],
                        w_out[l], norm_mix_post[l], norm_ffn_pre[l], w_up[l], w_down[l],
                        norm_ffn_post[l])
        outs.append(xb)
    return outs[0][None] if B == 1 else jnp.stack(outs, axis=0)
```

```python
import functools

import jax
import jax.numpy as jnp
from jax import lax
from jax.experimental import pallas as pl
from jax.experimental.pallas import tpu as pltpu
from jax.experimental.pallas import tpu_sc as plsc

F32 = jnp.float32
BF16 = jnp.bfloat16

M_HEADS = 4
M_HEAD_DIM = 256
M_WIDTH = M_HEADS * M_HEAD_DIM
M_CHUNK = 128
CONV_WIDTH = 4
A_HEADS = 8
A_HEAD_DIM = 128
A_WIDTH = A_HEADS * A_HEAD_DIM
MOBA_BLOCK = 256
MOBA_TOPK = 3
ROPE_THETA = 10000.0
NORM_EPS = 1e-6

LANES = 128
SUBLANES = 8
VMEM_LIMIT = 56 * 1024 * 1024
NEG_BIG = -1e30
LOG2_E = 1.4426950408889634

COL_GM = 0
COL_GA = 2048
COL_MQK = 4096
COL_AQK = 6144
COL_MV = 8192
COL_MO = 9216
COL_AV = 10240
P_COLS = 11264


def _cparams(sem):
    return pltpu.CompilerParams(dimension_semantics=sem, vmem_limit_bytes=VMEM_LIMIT)


def _rms(x, gain):
    ms = jnp.mean(x * x, axis=-1, keepdims=True)
    return x * lax.rsqrt(ms + NORM_EPS) * gain


def _sigmoid(x):
    return 1.0 / (1.0 + jnp.exp(-x))


def _split3(x):
    hi = x.astype(BF16)
    r1 = x - hi.astype(F32)
    mid = r1.astype(BF16)
    lo = (r1 - mid.astype(F32)).astype(BF16)
    return hi, mid, lo


def _proj_kernel(x_ref, g_ref, w_ref, wg_ref, p_ref, gate_ref, xn_ref):
    @pl.when(pl.program_id(1) == 0)
    def _():
        xn = _rms(x_ref[...], g_ref[...]).astype(BF16)
        xn_ref[...] = xn
        gate_ref[...] = jnp.dot(xn, wg_ref[...], preferred_element_type=F32)

    p_ref[...] = jnp.dot(xn_ref[...], w_ref[...], preferred_element_type=F32).astype(BF16)


def _proj(x, gain, w_all, w_gate, tm=1024, tn=1024):
    S, D = x.shape
    N = w_all.shape[1]
    return pl.pallas_call(
        _proj_kernel,
        grid=(S // tm, N // tn),
        in_specs=[
            pl.BlockSpec((tm, D), lambda i, j: (i, 0)),
            pl.BlockSpec((1, D), lambda i, j: (0, 0)),
            pl.BlockSpec((D, tn), lambda i, j: (0, j)),
            pl.BlockSpec((D, LANES), lambda i, j: (0, 0)),
        ],
        out_specs=[
            pl.BlockSpec((tm, tn), lambda i, j: (i, j)),
            pl.BlockSpec((tm, LANES), lambda i, j: (i, 0)),
        ],
        out_shape=[
            jax.ShapeDtypeStruct((S, N), BF16),
            jax.ShapeDtypeStruct((S, LANES), F32),
        ],
        scratch_shapes=[pltpu.VMEM((tm, D), BF16)],
        compiler_params=_cparams(("parallel", "arbitrary")),
        name="proj",
    )(x, gain, w_all, w_gate)


PREP_ROWS = MOBA_BLOCK
PREP_COLS = 512
HALO_ROWS = 2 * SUBLANES


def _prep_kernel(pmk_ref, halo_ref, paqk_ref, pav_ref, gate_ref, pos_ref, invf_ref,
                 cw_ref, cb_ref, gb_ref,
                 mq_ref, mkT_ref, aq_ref, ak_ref, avT_ref, kmean_ref, gc_ref, gt_ref):
    i = pl.program_id(0)
    R = PREP_ROWS

    k_scale = M_HEAD_DIM ** -0.5
    for c0 in range(0, 2 * M_WIDTH, PREP_COLS):
        cs = slice(c0, c0 + PREP_COLS)
        prev = halo_ref[:, cs].astype(F32)[HALO_ROWS - SUBLANES:, :]
        prev = jnp.where(i == 0, jnp.zeros_like(prev), prev)
        ext = jnp.concatenate([pmk_ref[:, cs].astype(F32), prev], axis=0)
        acc = cw_ref[0:1, cs] * ext
        for j in range(1, CONV_WIDTH):
            acc = pltpu.roll(acc, 1, 0) + cw_ref[j:j + 1, cs] * ext
        acc = acc[0:R, :] + cb_ref[:, cs]
        y = acc * _sigmoid(acc)
        if c0 < M_WIDTH:
            mq_ref[:, cs] = y.astype(BF16)
        else:
            ks = slice(c0 - M_WIDTH, c0 - M_WIDTH + PREP_COLS)
            mkT_ref[ks, :] = (y * k_scale).T.astype(BF16)

    ang = pos_ref[...].astype(F32) * invf_ref[...]
    cos = jnp.cos(ang)
    lane = lax.broadcasted_iota(jnp.int32, (R, A_HEAD_DIM), 1)
    sin_signed = jnp.where(lane < A_HEAD_DIM // 2, -1.0, 1.0) * jnp.sin(ang)
    for h in range(2 * A_HEADS):
        hs = slice(h * A_HEAD_DIM, (h + 1) * A_HEAD_DIM)
        xh = paqk_ref[:, hs].astype(F32)
        yh = xh * cos + pltpu.roll(xh, A_HEAD_DIM // 2, 1) * sin_signed
        if h < A_HEADS:
            aq_ref[h] = yh
        else:
            ko = slice((h - A_HEADS) * A_HEAD_DIM, (h - A_HEADS + 1) * A_HEAD_DIM)
            ak_ref[:, ko] = yh.astype(BF16)
            kmean_ref[0, :, ko] = jnp.mean(yh, axis=0, keepdims=True)
    for c0 in range(0, A_WIDTH, PREP_COLS):
        avT_ref[c0:c0 + PREP_COLS, :] = pav_ref[:, c0:c0 + PREP_COLS].astype(F32).T.astype(BF16)

    g = gate_ref[...] + gb_ref[...]
    log_f = jnp.minimum(g, 0.0) - jnp.log1p(jnp.exp(-jnp.abs(g)))
    r_i = lax.broadcasted_iota(jnp.int32, (R, R), 0)
    c_i = lax.broadcasted_iota(jnp.int32, (R, R), 1)
    tri = ((r_i >= c_i) & ((r_i // M_CHUNK) == (c_i // M_CHUNK))).astype(BF16)
    hi, mid, lo = _split3(log_f)
    csum = (jnp.dot(tri, hi, preferred_element_type=F32)
            + jnp.dot(tri, mid, preferred_element_type=F32)
            + jnp.dot(tri, lo, preferred_element_type=F32))
    glane = lax.broadcasted_iota(jnp.int32, (R, LANES), 1)
    gc = jnp.where(glane < M_HEADS, g, csum)
    gc_ref[...] = gc
    gt_ref[...] = gc.T[0:SUBLANES, :]


def _prep(P, gate, pos, invf, conv_w, conv_b, gbias):
    S = P.shape[0]
    R = PREP_ROWS
    nb = S // R
    halo_blocks = R // HALO_ROWS
    return pl.pallas_call(
        _prep_kernel,
        grid=(nb,),
        in_specs=[
            pl.BlockSpec((R, 2 * M_WIDTH), lambda i: (i, COL_MQK // (2 * M_WIDTH))),
            pl.BlockSpec((HALO_ROWS, 2 * M_WIDTH),
                         lambda i: (jnp.maximum(i * halo_blocks - 1, 0), COL_MQK // (2 * M_WIDTH))),
            pl.BlockSpec((R, 2 * A_WIDTH), lambda i: (i, COL_AQK // (2 * A_WIDTH))),
            pl.BlockSpec((R, A_WIDTH), lambda i: (i, COL_AV // A_WIDTH)),
            pl.BlockSpec((R, LANES), lambda i: (i, 0)),
            pl.BlockSpec((R, 1), lambda i: (i, 0)),
            pl.BlockSpec((1, A_HEAD_DIM), lambda i: (0, 0)),
            pl.BlockSpec((CONV_WIDTH, 2 * M_WIDTH), lambda i: (0, 0)),
            pl.BlockSpec((1, 2 * M_WIDTH), lambda i: (0, 0)),
            pl.BlockSpec((1, LANES), lambda i: (0, 0)),
        ],
        out_specs=[
            pl.BlockSpec((R, M_WIDTH), lambda i: (i, 0)),
            pl.BlockSpec((M_WIDTH, R), lambda i: (0, i)),
            pl.BlockSpec((A_HEADS, R, A_HEAD_DIM), lambda i: (0, i, 0)),
            pl.BlockSpec((R, A_WIDTH), lambda i: (i, 0)),
            pl.BlockSpec((A_WIDTH, R), lambda i: (0, i)),
            pl.BlockSpec((1, 1, A_WIDTH), lambda i: (i, 0, 0)),
            pl.BlockSpec((R, LANES), lambda i: (i, 0)),
            pl.BlockSpec((SUBLANES, R), lambda i: (0, i)),
        ],
        out_shape=[
            jax.ShapeDtypeStruct((S, M_WIDTH), BF16),
            jax.ShapeDtypeStruct((M_WIDTH, S), BF16),
            jax.ShapeDtypeStruct((A_HEADS, S, A_HEAD_DIM), F32),
            jax.ShapeDtypeStruct((S, A_WIDTH), BF16),
            jax.ShapeDtypeStruct((A_WIDTH, S), BF16),
            jax.ShapeDtypeStruct((nb, 1, A_WIDTH), F32),
            jax.ShapeDtypeStruct((S, LANES), F32),
            jax.ShapeDtypeStruct((SUBLANES, S), F32),
        ],
        compiler_params=_cparams(("parallel",)),
        name="prep",
    )(P, P, P, P, gate, pos, invf, conv_w, conv_b, gbias)


M_AUG = M_HEAD_DIM + LANES


def _mlstm_kernel(q_ref, kT_ref, v_ref, mo_ref, gc_ref, gt_ref, gain_ref, out_ref,
                  c_ref, m_ref):
    @pl.when(pl.program_id(0) == 0)
    def _():
        c_ref[...] = jnp.zeros_like(c_ref)
        m_ref[...] = jnp.zeros_like(m_ref)

    L = M_CHUNK
    D = M_HEAD_DIM
    row = lax.broadcasted_iota(jnp.int32, (L, L), 0)
    col = lax.broadcasted_iota(jnp.int32, (L, L), 1)
    causal = row >= col
    ones_col = (lax.broadcasted_iota(jnp.int32, (L, LANES), 1) == 0).astype(BF16)

    heads = range(M_HEADS)
    hsl = [slice(h * D, (h + 1) * D) for h in heads]
    q = [q_ref[:, hsl[h]] for h in heads]
    kT = [kT_ref[hsl[h], :] for h in heads]
    v_aug = [jnp.concatenate([v_ref[:, hsl[h]], ones_col], axis=1) for h in heads]
    b_c = [gc_ref[:, M_HEADS + h:M_HEADS + h + 1] for h in heads]
    b_r = [gt_ref[M_HEADS + h:M_HEADS + h + 1, :] for h in heads]
    u_r = [gt_ref[h:h + 1, :] - b_r[h] for h in heads]
    f_tot = [b_r[h][:, L - 1:L] for h in heads]
    m_prev = [m_ref[h:h + 1, 0:1] for h in heads]

    s_qk = [jnp.dot(q[h], kT[h], preferred_element_type=F32) for h in heads]
    q_c = [jnp.dot(q[h], c_ref[h].astype(BF16), preferred_element_type=F32) for h in heads]

    for h in heads:
        w_r = f_tot[h] + u_r[h]
        m_loc = jnp.max(w_r, axis=1, keepdims=True)
        m_new = jnp.maximum(f_tot[h] + m_prev[h], m_loc)
        a = jnp.exp(f_tot[h] + m_prev[h] - m_new)
        e_r = jnp.exp(w_r - m_new)
        keT = (kT[h].astype(F32) * e_r).astype(BF16)
        c_ref[h] = a * c_ref[h] + jnp.dot(keT, v_aug[h], preferred_element_type=F32)
        m_ref[h:h + 1, :] = jnp.broadcast_to(m_new, (1, LANES))

    for h in heads:
        d_log = jnp.where(causal, b_c[h] + u_r[h], -jnp.inf)
        a_log = b_c[h] + m_prev[h]
        m_t = jnp.maximum(a_log, jnp.max(d_log, axis=1, keepdims=True))
        s_ts = s_qk[h] * jnp.exp(d_log - m_t)
        inter = jnp.exp(a_log - m_t)
        r = inter * q_c[h] + jnp.dot(s_ts.astype(BF16), v_aug[h], preferred_element_type=F32)
        num = r[:, :D]
        den = r[:, D:D + 1]
        hh = num / jnp.maximum(jnp.abs(den), jnp.exp(-m_t))
        hn = hh * lax.rsqrt(jnp.mean(hh * hh, axis=-1, keepdims=True) + NORM_EPS)
        out_ref[:, hsl[h]] = (hn * gain_ref[:, hsl[h]]
                              * _sigmoid(mo_ref[:, hsl[h]].astype(F32))).astype(BF16)


def _mlstm(mq, mkT, P, gc, gt, gain):
    S = mq.shape[0]
    L = M_CHUNK
    return pl.pallas_call(
        _mlstm_kernel,
        grid=(S // L,),
        in_specs=[
            pl.BlockSpec((L, M_WIDTH), lambda c: (c, 0)),
            pl.BlockSpec((M_WIDTH, L), lambda c: (0, c)),
            pl.BlockSpec((L, M_WIDTH), lambda c: (c, COL_MV // M_WIDTH)),
            pl.BlockSpec((L, M_WIDTH), lambda c: (c, COL_MO // M_WIDTH)),
            pl.BlockSpec((L, LANES), lambda c: (c, 0)),
            pl.BlockSpec((SUBLANES, L), lambda c: (0, c)),
            pl.BlockSpec((1, M_WIDTH), lambda c: (0, 0)),
        ],
        out_specs=pl.BlockSpec((L, M_WIDTH), lambda c: (c, 0)),
        out_shape=jax.ShapeDtypeStruct((S, M_WIDTH), BF16),
        scratch_shapes=[
            pltpu.VMEM((M_HEADS, M_HEAD_DIM, M_AUG), F32),
            pltpu.VMEM((SUBLANES, LANES), F32),
        ],
        compiler_params=_cparams(("arbitrary",)),
        name="mlstm",
    )(mq, mkT, P, P, gc, gt, gain)


MOBA_HEADS_PER_STEP = 4


ROUTE_TILES = 264
ROUTE_ROWS = ROUTE_TILES * MOBA_BLOCK
GROUP_TILES_PER_STEP = 8
ROW_WORDS = A_HEAD_DIM // 2
COMBINE_ROWS = 256


def _pack_rows(o_t, lse):
    q = o_t.shape[1]
    hi = pltpu.bitcast(o_t[:ROW_WORDS].astype(BF16).astype(F32), jnp.uint32)
    lo = pltpu.bitcast(o_t[ROW_WORDS:].astype(BF16).astype(F32), jnp.uint32)
    words = pltpu.bitcast(hi | (lo >> 16), F32)
    tail = jnp.concatenate([jnp.broadcast_to(lse, (SUBLANES, q)),
                            jnp.zeros((LANES - ROW_WORDS - SUBLANES, q), F32)], axis=0)
    return jnp.concatenate([words, tail], axis=0).T


def _dot_nt(a, b):
    return lax.dot_general(a, b, (((1,), (1,)), ((), ())), preferred_element_type=F32)


def _block_partial(k_blk, v_t_blk, qs, ones_rows, causal_mask=None):
    s = _dot_nt(k_blk, qs)
    if causal_mask is not None:
        s = jnp.where(causal_mask, s, NEG_BIG)
    sb = s.astype(BF16)
    m = jnp.max(sb, axis=0, keepdims=True)
    p = jnp.exp2(sb - m)
    r = jnp.dot(jnp.concatenate([v_t_blk, ones_rows], axis=0), p, preferred_element_type=F32)
    dh = v_t_blk.shape[0]
    l = r[dh:dh + 1, :]
    return r[:dh, :] / l, m.astype(F32) + jnp.log2(l)


def _route_kernel(q_ref, k_ref, vT_ref, km_ref, route_ref, cnt_ref, own_ref, cnt_acc):
    i = pl.program_id(0)
    BS = MOBA_BLOCK
    Dh = A_HEAD_DIM
    NB = km_ref.shape[0]
    G = q_ref.shape[0]
    blk = lax.broadcasted_iota(jnp.int32, (NB, BS), 0)
    kpos = lax.broadcasted_iota(jnp.int32, (BS, BS), 0)
    qpos = lax.broadcasted_iota(jnp.int32, (BS, BS), 1)
    earlier = (kpos < qpos).astype(BF16)
    ones_rows = jnp.ones((2 * SUBLANES, BS), BF16)

    @pl.when(i == 0)
    def _():
        cnt_acc[...] = jnp.zeros_like(cnt_acc)

    q = [q_ref[g] for g in range(G)]
    qs = [(q[g] * (Dh ** -0.5 * LOG2_E)).astype(BF16) for g in range(G)]
    parts = [_block_partial(k_ref[:, g * Dh:(g + 1) * Dh], vT_ref[g * Dh:(g + 1) * Dh, :], qs[g],
                            ones_rows, kpos <= qpos) for g in range(G)]

    gates = []
    for g in range(G):
        hs = slice(g * Dh, (g + 1) * Dh)
        km = km_ref[:, hs]
        kh = km.astype(BF16)
        kl = (km - kh.astype(F32)).astype(BF16)
        qh = q[g].astype(BF16)
        ql = (q[g] - qh.astype(F32)).astype(BF16)
        gate = _dot_nt(kh, qh) + _dot_nt(kh, ql) + _dot_nt(kl, qh)
        gates.append(jnp.where(blk < i, gate, -jnp.inf))
    picks = [[] for _ in range(G)]
    rows = [[] for _ in range(G)]
    for r in range(MOBA_TOPK):
        for g in range(G):
            mx = jnp.max(gates[g], axis=0, keepdims=True)
            idx = jnp.min(jnp.where(gates[g] == mx, blk, NB), axis=0, keepdims=True)
            idx = jnp.where(r < i, idx, -1)
            pick = blk == idx
            gates[g] = jnp.where(pick, -jnp.inf, gates[g])
            picks[g].append(pick)
            rows[g].append(idx)
    for g in range(G):
        onehot = sum(p.astype(F32) for p in picks[g])
        before = cnt_acc[g][:, 0:1] + jnp.dot(onehot.astype(BF16), earlier, preferred_element_type=F32)
        for r in range(MOBA_TOPK):
            rank = jnp.sum(jnp.where(picks[g][r], before, 0.0), axis=0, keepdims=True)
            rows[g].append(rank.astype(jnp.int32))
        rows[g].append(jnp.zeros((SUBLANES - 2 * MOBA_TOPK, BS), jnp.int32))
        route_ref[g] = jnp.concatenate(rows[g], axis=0)
        cnt_new = cnt_acc[g] + jnp.sum(onehot, axis=1, keepdims=True)
        cnt_acc[g] = cnt_new
        cnt_ref[g] = cnt_new
        own_ref[g] = _pack_rows(*parts[g])


def _route(aq_hm, ak, avT, kmean):
    H, S, Dh = aq_hm.shape
    BS = MOBA_BLOCK
    NB = S // BS
    W = H * Dh
    return pl.pallas_call(
        _route_kernel,
        grid=(NB,),
        in_specs=[
            pl.BlockSpec((H, BS, Dh), lambda i: (0, i, 0)),
            pl.BlockSpec((BS, W), lambda i: (i, 0)),
            pl.BlockSpec((W, BS), lambda i: (0, i)),
            pl.BlockSpec((NB, W), lambda i: (0, 0)),
        ],
        out_specs=[
            pl.BlockSpec((H, SUBLANES, BS), lambda i: (0, 0, i)),
            pl.BlockSpec((H, NB, LANES), lambda i: (0, 0, 0)),
            pl.BlockSpec((H, BS, LANES), lambda i: (0, i, 0)),
        ],
        out_shape=[
            jax.ShapeDtypeStruct((H, SUBLANES, S), jnp.int32),
            jax.ShapeDtypeStruct((H, NB, LANES), F32),
            jax.ShapeDtypeStruct((H, S, LANES), F32),
        ],
        scratch_shapes=[pltpu.VMEM((H, NB, LANES), F32)],
        compiler_params=_cparams(("arbitrary",)),
        name="moba_route",
    )(aq_hm, ak, avT, kmean)


def _dest_kernel(seg_ref, route_ref, dest_ref):
    h = pl.program_id(0)
    NB = seg_ref.shape[1]
    r = route_ref[0]
    blk = r[0:MOBA_TOPK, :]
    rank = r[MOBA_TOPK:2 * MOBA_TOPK, :]
    base = h * ROUTE_ROWS
    dest = jnp.full(blk.shape, base + ROUTE_ROWS - 1, jnp.int32)
    for j in range(NB):
        dest = jnp.where(blk == j, base + seg_ref[h, j] + rank, dest)
    dest_ref[0] = jnp.concatenate(
        [dest, jnp.zeros((SUBLANES - MOBA_TOPK, dest.shape[1]), jnp.int32)], axis=0)


def _dest(seg_start, route, h0):
    H = seg_start.shape[0]
    S = route.shape[2]
    return pl.pallas_call(
        _dest_kernel,
        grid_spec=pltpu.PrefetchScalarGridSpec(
            num_scalar_prefetch=1,
            grid=(H,),
            in_specs=[pl.BlockSpec((1, SUBLANES, S), lambda h, seg: (h0 + h, 0, 0))],
            out_specs=pl.BlockSpec((1, SUBLANES, S), lambda h, seg: (h, 0, 0)),
        ),
        out_shape=jax.ShapeDtypeStruct((H, SUBLANES, S), jnp.int32),
        compiler_params=_cparams(("parallel",)),
        name="moba_dest",
    )(seg_start, route)


SC_WINDOW = 256


def _sc_mesh():
    return plsc.VectorSubcoreMesh(core_axis_name="c", subcore_axis_name="s")


def _sc_scatter(rows, row0, n_rows, idx, n_out):
    W = rows.shape[1]
    M = idx.shape[0]
    n_win = n_rows // SC_WINDOW
    win0 = row0 // SC_WINDOW

    @pl.kernel(out_type=jax.ShapeDtypeStruct((n_out, W), rows.dtype), mesh=_sc_mesh())
    def k(x_hbm, i_hbm, o_hbm):
        def body(x_vmem, i_vmem):
            pltpu.sync_copy(x_vmem, o_hbm.at[i_vmem.at[0]])

        pltpu.emit_pipeline(
            body,
            grid=(M // SC_WINDOW,),
            in_specs=[pl.BlockSpec((SC_WINDOW, W), lambda w: (win0 + lax.rem(w, n_win), 0)),
                      pl.BlockSpec((1, SC_WINDOW), lambda w: (0, w))],
            out_specs=[],
            core_axis_name=("c", "s"),
            dimension_semantics=(pltpu.PARALLEL,),
        )(x_hbm, i_hbm)

    return k(rows, idx.reshape(1, M))


def _sc_gather(table, idx):
    M = idx.shape[0]
    W = table.shape[1]

    @pl.kernel(out_type=jax.ShapeDtypeStruct((M, W), table.dtype), mesh=_sc_mesh())
    def k(x_hbm, i_hbm, o_hbm):
        def body(i_vmem, o_vmem):
            pltpu.sync_copy(x_hbm.at[i_vmem.at[0]], o_vmem)

        pltpu.emit_pipeline(
            body,
            grid=(M // SC_WINDOW,),
            in_specs=[pl.BlockSpec((1, SC_WINDOW), lambda w: (0, w))],
            out_specs=[pl.BlockSpec((SC_WINDOW, W), lambda w: (w, 0))],
            core_axis_name=("c", "s"),
            dimension_semantics=(pltpu.PARALLEL,),
        )(i_hbm, o_hbm)

    return k(table, idx.reshape(1, M))


def _group_kernel(tile_blk_ref, q_ref, k_ref, vT_ref, out_ref, s_ref, m_ref):
    h = pl.program_id(0)
    u = pl.program_id(1)
    last = pl.num_programs(1) - 2
    BS = MOBA_BLOCK
    Dh = A_HEAD_DIM
    T = GROUP_TILES_PER_STEP
    ones_rows = jnp.ones((2 * SUBLANES, BS), BF16)
    new = lax.rem(u, 2)
    old = 1 - new

    @pl.when((h == 0) & (u == 0))
    def _():
        s_ref[...] = jnp.zeros_like(s_ref)
        m_ref[...] = jnp.zeros_like(m_ref)

    def block_start(group, c):
        j = jnp.maximum(tile_blk_ref[h * ROUTE_TILES + group * T + c], 0)
        return pl.multiple_of(j * BS, BS)

    g_new = jnp.minimum(u, last)
    g_old = jnp.maximum(u - 1, 0)
    @pl.when(tile_blk_ref[h * ROUTE_TILES + g_old * T] >= 0)
    def _():
        ms, rs = [], []
        for c in range(T):
            m = m_ref[old, c]
            p = jnp.exp2(s_ref[old, c] - m.astype(BF16))
            v_aug = jnp.concatenate([vT_ref[:, pl.ds(block_start(g_old, c), BS)], ones_rows], axis=0)
            ms.append(m)
            rs.append(jnp.dot(v_aug, p, preferred_element_type=F32))
        s_new = []
        for c in range(T):
            qs = (q_ref[c * BS:(c + 1) * BS, :] * (Dh ** -0.5 * LOG2_E)).astype(BF16)
            s_new.append(_dot_nt(k_ref[pl.ds(block_start(g_new, c), BS), :], qs))
        for c in range(T):
            l = rs[c][Dh:Dh + 1, :]
            out_ref[c * BS:(c + 1) * BS, :] = _pack_rows(rs[c][:Dh, :] / l, ms[c] + jnp.log2(l))
            sb = s_new[c].astype(BF16)
            s_ref[new, c] = sb
            m_ref[new, c] = jnp.max(sb, axis=0, keepdims=True).astype(F32)


def _group(tile_blk, q_sorted, ak, avT, h0):
    S = ak.shape[0]
    H = q_sorted.shape[0] // ROUTE_ROWS
    Dh = A_HEAD_DIM
    BS = MOBA_BLOCK
    T = GROUP_TILES_PER_STEP
    steps = ROUTE_TILES // T
    rows = T * BS
    return pl.pallas_call(
        _group_kernel,
        grid_spec=pltpu.PrefetchScalarGridSpec(
            num_scalar_prefetch=1,
            grid=(H, steps + 1),
            in_specs=[
                pl.BlockSpec((rows, LANES), lambda h, u, tb: (h * steps + jnp.minimum(u, steps - 1), 0)),
                pl.BlockSpec((S, Dh), lambda h, u, tb: (0, h0 + h)),
                pl.BlockSpec((Dh, S), lambda h, u, tb: (h0 + h, 0)),
            ],
            out_specs=pl.BlockSpec((rows, LANES), lambda h, u, tb: (h * steps + jnp.maximum(u - 1, 0), 0)),
            scratch_shapes=[pltpu.VMEM((2, T, BS, BS), BF16), pltpu.VMEM((2, T, 1, BS), F32)],
        ),
        out_shape=jax.ShapeDtypeStruct((H * ROUTE_ROWS, LANES), F32),
        compiler_params=_cparams(("arbitrary", "arbitrary")),
        name="moba_group",
    )(tile_blk, q_sorted, ak, avT)


def _combine_kernel(*refs):
    i = pl.program_id(0)
    own_ref, o_ref = refs[-2], refs[-1]
    n_groups = len(refs) - 2
    H = refs[0].shape[0]
    BS = own_ref.shape[1]
    RC = COMBINE_ROWS
    low_half = lax.broadcasted_iota(jnp.int32, (RC, LANES), 1) < ROW_WORDS
    for head in range(n_groups * H):
        got_ref, h = refs[head // H], head % H
        for r0 in range(0, BS, RC):
            rs = slice(r0, r0 + RC)
            tiles = [own_ref[head, rs, :]]
            for r in range(MOBA_TOPK):
                tiles.append(jnp.where(r < i, got_ref[h, r, rs, :], 0.0))
            lses = [tiles[0][:, ROW_WORDS:ROW_WORDS + 1]]
            lses += [jnp.where(r < i, tiles[r + 1][:, ROW_WORDS:ROW_WORDS + 1], NEG_BIG)
                     for r in range(MOBA_TOPK)]
            top = functools.reduce(jnp.maximum, lses)
            w = [jnp.exp2(x - top) for x in lses]
            inv = 1.0 / sum(w)
            first = jnp.zeros((RC, LANES), F32)
            second = jnp.zeros((RC, LANES), F32)
            for wk, tile in zip(w, tiles):
                words = pltpu.bitcast(tile, jnp.uint32)
                first = first + wk * pltpu.bitcast(words & jnp.uint32(0xFFFF0000), F32)
                second = second + wk * pltpu.bitcast(words << 16, F32)
            o = jnp.where(low_half, first, pltpu.roll(second, ROW_WORDS, 1)) * inv
            o_ref[rs, head * A_HEAD_DIM:(head + 1) * A_HEAD_DIM] = o.astype(BF16)


def _combine(gots, own):
    H = gots[0].shape[0]
    S = own.shape[1]
    BS = MOBA_BLOCK
    return pl.pallas_call(
        _combine_kernel,
        grid=(S // BS,),
        in_specs=([pl.BlockSpec((H, MOBA_TOPK, BS, LANES), lambda i: (0, 0, i, 0))] * len(gots)
                  + [pl.BlockSpec((own.shape[0], BS, LANES), lambda i: (0, i, 0))]),
        out_specs=pl.BlockSpec((BS, A_WIDTH), lambda i: (i, 0)),
        out_shape=jax.ShapeDtypeStruct((S, A_WIDTH), BF16),
        compiler_params=_cparams(("parallel",)),
        name="moba_combine",
    )(*gots, own)


def _moba_routed(aq_hm, ak, avT, kmean):
    H, S, Dh = aq_hm.shape
    NB = S // MOBA_BLOCK
    hn = MOBA_HEADS_PER_STEP
    q_rows = aq_hm.reshape(H * S, Dh)
    route, cnt_all, own = _route(aq_hm, ak, avT, kmean)
    gots = []
    for h0 in range(0, H, hn):
        cnt = cnt_all[h0:h0 + hn, :, 0].astype(jnp.int32)
        seg_tiles = (cnt + MOBA_BLOCK - 1) // MOBA_BLOCK
        seg_end = jnp.cumsum(seg_tiles, axis=1)
        seg_start = (seg_end - seg_tiles) * MOBA_BLOCK
        tile_ids = jnp.arange(ROUTE_TILES, dtype=jnp.int32)
        tile_blk = jnp.sum(tile_ids[None, :, None] >= seg_end[:, None, :], axis=2).astype(jnp.int32)
        tile_blk = jnp.where(tile_blk < NB, tile_blk, -1).reshape(hn * ROUTE_TILES)
        dest = _dest(seg_start, route, h0)[:, :MOBA_TOPK, :]
        q_sorted = _sc_scatter(q_rows, h0 * S, hn * S, dest.transpose(1, 0, 2).reshape(-1),
                               hn * ROUTE_ROWS)
        results = _group(tile_blk, q_sorted, ak, avT, h0)
        gots.append(_sc_gather(results, dest.reshape(-1)).reshape(hn, MOBA_TOPK, S, LANES))
    return _combine(gots, own)


def _merge_kernel(hm_ref, ha_ref, gm_ref, ga_ref, x_ref, wm_ref, wa_ref, wo_ref, gain_ref,
                  out_ref):
    ym = jnp.dot(hm_ref[...], wm_ref[...], preferred_element_type=F32)
    ya = jnp.dot(ha_ref[...], wa_ref[...], preferred_element_type=F32)
    merged = _sigmoid(gm_ref[...].astype(F32)) * ym + _sigmoid(ga_ref[...].astype(F32)) * ya
    mix = jnp.dot(merged.astype(BF16), wo_ref[...], preferred_element_type=F32)
    out_ref[...] = x_ref[...] + _rms(mix, gain_ref[...])


def _merge(hm, ha, P, x, wm, wa, wo, gain, tm=256):
    S, D = x.shape
    const = pl.Buffered(1)
    return pl.pallas_call(
        _merge_kernel,
        grid=(S // tm,),
        in_specs=[
            pl.BlockSpec((tm, M_WIDTH), lambda i: (i, 0)),
            pl.BlockSpec((tm, A_WIDTH), lambda i: (i, 0)),
            pl.BlockSpec((tm, D), lambda i: (i, COL_GM // D)),
            pl.BlockSpec((tm, D), lambda i: (i, COL_GA // D)),
            pl.BlockSpec((tm, D), lambda i: (i, 0)),
            pl.BlockSpec((M_WIDTH, D), lambda i: (0, 0), pipeline_mode=const),
            pl.BlockSpec((A_WIDTH, D), lambda i: (0, 0), pipeline_mode=const),
            pl.BlockSpec((D, D), lambda i: (0, 0), pipeline_mode=const),
            pl.BlockSpec((1, D), lambda i: (0, 0)),
        ],
        out_specs=pl.BlockSpec((tm, D), lambda i: (i, 0)),
        out_shape=jax.ShapeDtypeStruct((S, D), F32),
        compiler_params=_cparams(("parallel",)),
        name="merge",
    )(hm, ha, P, P, x, wm, wa, wo, gain)


def _ffn_kernel(x_ref, gpre_ref, wu_ref, wd_ref, gpost_ref, out_ref, hn_ref, acc_ref):
    f = pl.program_id(1)

    @pl.when(f == 0)
    def _():
        hn_ref[...] = _rms(x_ref[...], gpre_ref[...]).astype(BF16)
        acc_ref[...] = jnp.zeros_like(acc_ref)

    u = jnp.dot(hn_ref[...], wu_ref[...], preferred_element_type=F32)
    u = jnp.square(jnp.maximum(u, 0.0)).astype(BF16)
    acc_ref[...] += jnp.dot(u, wd_ref[...], preferred_element_type=F32)

    @pl.when(f == pl.num_programs(1) - 1)
    def _():
        out_ref[...] = x_ref[...] + _rms(acc_ref[...], gpost_ref[...])


def _ffn(x, gpre, wu, wd, gpost, tm=512, tf=1024):
    S, D = x.shape
    Fd = wu.shape[1]
    return pl.pallas_call(
        _ffn_kernel,
        grid=(S // tm, Fd // tf),
        in_specs=[
            pl.BlockSpec((tm, D), lambda i, f: (i, 0)),
            pl.BlockSpec((1, D), lambda i, f: (0, 0)),
            pl.BlockSpec((D, tf), lambda i, f: (0, f)),
            pl.BlockSpec((tf, D), lambda i, f: (f, 0)),
            pl.BlockSpec((1, D), lambda i, f: (0, 0)),
        ],
        out_specs=pl.BlockSpec((tm, D), lambda i, f: (i, 0)),
        out_shape=jax.ShapeDtypeStruct((S, D), F32),
        scratch_shapes=[pltpu.VMEM((tm, D), BF16), pltpu.VMEM((tm, D), F32)],
        compiler_params=_cparams(("parallel", "arbitrary")),
        name="ffn",
    )(x, gpre, wu, wd, gpost)


def _layer(x, pos, norm_mix_pre, w_in, conv_w, conv_b, i_bias, f_bias, mlstm_norm,
           w_branch_m, w_branch_a, w_out, norm_mix_post, norm_ffn_pre, w_up, w_down,
           norm_ffn_post):
    S, D = x.shape
    o = 0
    pieces = {}
    w_in = w_in.astype(BF16)
    for name, width in (("mq", M_WIDTH), ("mk", M_WIDTH), ("mv", M_WIDTH), ("mo", M_WIDTH),
                        ("mi", M_HEADS), ("mf", M_HEADS), ("aq", A_WIDTH), ("ak", A_WIDTH),
                        ("av", A_WIDTH), ("gm", D), ("ga", D)):
        pieces[name] = w_in[:, o:o + width]
        o += width
    w_all = jnp.concatenate([pieces[n] for n in ("gm", "ga", "mq", "mk", "aq", "ak", "mv", "mo", "av")],
                            axis=1)
    w_gate = jnp.concatenate(
        [pieces["mi"], pieces["mf"], jnp.zeros((D, LANES - 2 * M_HEADS), BF16)], axis=1)
    gbias = jnp.concatenate([i_bias, f_bias, jnp.zeros((LANES - 2 * M_HEADS,), F32)])[None, :]
    half = jnp.arange(0, A_HEAD_DIM, 2, dtype=F32) / A_HEAD_DIM
    inv_freq = 1.0 / (ROPE_THETA ** half)
    invf = jnp.concatenate([inv_freq, inv_freq])[None, :]

    P, gate = _proj(x, norm_mix_pre[None, :], w_all, w_gate)
    mq, mkT, aq, ak, avT, kmean, gc, gt = _prep(
        P, gate, pos.reshape(S, 1), invf, conv_w, conv_b[None, :], gbias)
    hm = _mlstm(mq, mkT, P, gc, gt, mlstm_norm[None, :])
    ha = _moba_routed(aq, ak, avT, kmean.reshape(S // MOBA_BLOCK, A_WIDTH))
    x1 = _merge(hm, ha, P, x, w_branch_m.astype(BF16), w_branch_a.astype(BF16),
                w_out.astype(BF16), norm_mix_post[None, :])
    return _ffn(x1, norm_ffn_pre[None, :], w_up.astype(BF16), w_down.astype(BF16),
                norm_ffn_post[None, :])


def kernel(x, positions, norm_mix_pre, w_in, conv_w, conv_b, i_bias, f_bias, mlstm_norm,
           w_branch_m, w_branch_a, w_out, norm_mix_post, norm_ffn_pre, w_up, w_down,
           norm_ffn_post):
    B = x.shape[0]
    depth = w_in.shape[0]
    outs = []
    for b in range(B):
        xb = x[b]
        for l in range(depth):
            xb = _layer(xb, positions[b], norm_mix_pre[l], w_in[l], conv_w[l], conv_b[l],
                        i_bias[l], f_bias[l], mlstm_norm[l], w_branch_m[l], w_branch_a[l],
                        w_out[l], norm_mix_post[l], norm_ffn_pre[l], w_up[l], w_down[l],
                        norm_ffn_post[l])
        outs.append(xb)
    return outs[0][None] if B == 1 else jnp.stack(outs, axis=0)
```

```python
import functools

import jax
import jax.numpy as jnp
from jax import lax
from jax.experimental import pallas as pl
from jax.experimental.pallas import tpu as pltpu
from jax.experimental.pallas import tpu_sc as plsc

F32 = jnp.float32
BF16 = jnp.bfloat16

M_HEADS = 4
M_HEAD_DIM = 256
M_WIDTH = M_HEADS * M_HEAD_DIM
M_CHUNK = 128
CONV_WIDTH = 4
A_HEADS = 8
A_HEAD_DIM = 128
A_WIDTH = A_HEADS * A_HEAD_DIM
MOBA_BLOCK = 256
MOBA_TOPK = 3
ROPE_THETA = 10000.0
NORM_EPS = 1e-6

LANES = 128
SUBLANES = 8
VMEM_LIMIT = 56 * 1024 * 1024
NEG_BIG = -1e30
LOG2_E = 1.4426950408889634

COL_AQK = 0
COL_AV = 2048
COL_GM = 0
COL_GA = 2048
COL_MQK = 4096
COL_MV = 6144
COL_MO = 7168


def _cparams(sem):
    return pltpu.CompilerParams(dimension_semantics=sem, vmem_limit_bytes=VMEM_LIMIT)


def _rms(x, gain):
    ms = jnp.mean(x * x, axis=-1, keepdims=True)
    return x * lax.rsqrt(ms + NORM_EPS) * gain


def _sigmoid(x):
    return 1.0 / (1.0 + jnp.exp(-x))


def _split3(x):
    hi = x.astype(BF16)
    r1 = x - hi.astype(F32)
    mid = r1.astype(BF16)
    lo = (r1 - mid.astype(F32)).astype(BF16)
    return hi, mid, lo


def _proj_kernel(x_ref, g_ref, w_ref, wg_ref, p_ref, gate_ref, xn_ref):
    @pl.when(pl.program_id(1) == 0)
    def _():
        xn = _rms(x_ref[...], g_ref[...]).astype(BF16)
        xn_ref[...] = xn
        gate_ref[...] = jnp.dot(xn, wg_ref[...], preferred_element_type=F32)

    p_ref[...] = jnp.dot(xn_ref[...], w_ref[...], preferred_element_type=F32).astype(BF16)


def _proj(x, gain, w_all, w_gate, tm=1024, tn=1024):
    S, D = x.shape
    N = w_all.shape[1]
    return pl.pallas_call(
        _proj_kernel,
        grid=(S // tm, N // tn),
        in_specs=[
            pl.BlockSpec((tm, D), lambda i, j: (i, 0)),
            pl.BlockSpec((1, D), lambda i, j: (0, 0)),
            pl.BlockSpec((D, tn), lambda i, j: (0, j)),
            pl.BlockSpec((D, LANES), lambda i, j: (0, 0)),
        ],
        out_specs=[
            pl.BlockSpec((tm, tn), lambda i, j: (i, j)),
            pl.BlockSpec((tm, LANES), lambda i, j: (i, 0)),
        ],
        out_shape=[
            jax.ShapeDtypeStruct((S, N), BF16),
            jax.ShapeDtypeStruct((S, LANES), F32),
        ],
        scratch_shapes=[pltpu.VMEM((tm, D), BF16)],
        compiler_params=_cparams(("parallel", "arbitrary")),
        name="proj",
    )(x, gain, w_all, w_gate)


PREP_ROWS = MOBA_BLOCK
PREP_COLS = 512
HALO_ROWS = 2 * SUBLANES


def _prep_conv_kernel(pmk_ref, halo_ref, gate_ref, cw_ref, cb_ref, gb_ref,
                      mq_ref, mkT_ref, gc_ref, gt_ref):
    i = pl.program_id(0)
    R = PREP_ROWS

    k_scale = M_HEAD_DIM ** -0.5
    for c0 in range(0, 2 * M_WIDTH, PREP_COLS):
        cs = slice(c0, c0 + PREP_COLS)
        prev = halo_ref[:, cs].astype(F32)[HALO_ROWS - SUBLANES:, :]
        prev = jnp.where(i == 0, jnp.zeros_like(prev), prev)
        ext = jnp.concatenate([pmk_ref[:, cs].astype(F32), prev], axis=0)
        acc = cw_ref[0:1, cs] * ext
        for j in range(1, CONV_WIDTH):
            acc = pltpu.roll(acc, 1, 0) + cw_ref[j:j + 1, cs] * ext
        acc = acc[0:R, :] + cb_ref[:, cs]
        y = acc * _sigmoid(acc)
        if c0 < M_WIDTH:
            mq_ref[:, cs] = y.astype(BF16)
        else:
            ks = slice(c0 - M_WIDTH, c0 - M_WIDTH + PREP_COLS)
            mkT_ref[ks, :] = (y * k_scale).T.astype(BF16)

    g = gate_ref[...] + gb_ref[...]
    log_f = jnp.minimum(g, 0.0) - jnp.log1p(jnp.exp(-jnp.abs(g)))
    r_i = lax.broadcasted_iota(jnp.int32, (R, R), 0)
    c_i = lax.broadcasted_iota(jnp.int32, (R, R), 1)
    tri = ((r_i >= c_i) & ((r_i // M_CHUNK) == (c_i // M_CHUNK))).astype(BF16)
    hi, mid, lo = _split3(log_f)
    csum = (jnp.dot(tri, hi, preferred_element_type=F32)
            + jnp.dot(tri, mid, preferred_element_type=F32)
            + jnp.dot(tri, lo, preferred_element_type=F32))
    glane = lax.broadcasted_iota(jnp.int32, (R, LANES), 1)
    gc = jnp.where(glane < M_HEADS, g, csum)
    gc_ref[...] = gc
    gt_ref[...] = gc.T[0:SUBLANES, :]


def _prep_conv(P, gate, conv_w, conv_b, gbias):
    S = P.shape[0]
    R = PREP_ROWS
    halo_blocks = R // HALO_ROWS
    return pl.pallas_call(
        _prep_conv_kernel,
        grid=(S // R,),
        in_specs=[
            pl.BlockSpec((R, 2 * M_WIDTH), lambda i: (i, COL_MQK // (2 * M_WIDTH))),
            pl.BlockSpec((HALO_ROWS, 2 * M_WIDTH),
                         lambda i: (jnp.maximum(i * halo_blocks - 1, 0), COL_MQK // (2 * M_WIDTH))),
            pl.BlockSpec((R, LANES), lambda i: (i, 0)),
            pl.BlockSpec((CONV_WIDTH, 2 * M_WIDTH), lambda i: (0, 0)),
            pl.BlockSpec((1, 2 * M_WIDTH), lambda i: (0, 0)),
            pl.BlockSpec((1, LANES), lambda i: (0, 0)),
        ],
        out_specs=[
            pl.BlockSpec((R, M_WIDTH), lambda i: (i, 0)),
            pl.BlockSpec((M_WIDTH, R), lambda i: (0, i)),
            pl.BlockSpec((R, LANES), lambda i: (i, 0)),
            pl.BlockSpec((SUBLANES, R), lambda i: (0, i)),
        ],
        out_shape=[
            jax.ShapeDtypeStruct((S, M_WIDTH), BF16),
            jax.ShapeDtypeStruct((M_WIDTH, S), BF16),
            jax.ShapeDtypeStruct((S, LANES), F32),
            jax.ShapeDtypeStruct((SUBLANES, S), F32),
        ],
        compiler_params=_cparams(("parallel",)),
        name="prep_conv",
    )(P, P, gate, conv_w, conv_b, gbias)


def _prep_rope_kernel(paqk_ref, pav_ref, pos_ref, invf_ref, aq_ref, ak_ref, avT_ref, kmean_ref):
    R = PREP_ROWS
    ang = pos_ref[...].astype(F32) * invf_ref[...]
    cos = jnp.cos(ang)
    lane = lax.broadcasted_iota(jnp.int32, (R, A_HEAD_DIM), 1)
    sin_signed = jnp.where(lane < A_HEAD_DIM // 2, -1.0, 1.0) * jnp.sin(ang)
    for h in range(2 * A_HEADS):
        hs = slice(h * A_HEAD_DIM, (h + 1) * A_HEAD_DIM)
        xh = paqk_ref[:, hs].astype(F32)
        yh = xh * cos + pltpu.roll(xh, A_HEAD_DIM // 2, 1) * sin_signed
        if h < A_HEADS:
            aq_ref[h] = yh
        else:
            ko = slice((h - A_HEADS) * A_HEAD_DIM, (h - A_HEADS + 1) * A_HEAD_DIM)
            ak_ref[:, ko] = yh.astype(BF16)
            kmean_ref[0, :, ko] = jnp.mean(yh, axis=0, keepdims=True)
    for c0 in range(0, A_WIDTH, PREP_COLS):
        avT_ref[c0:c0 + PREP_COLS, :] = pav_ref[:, c0:c0 + PREP_COLS].astype(F32).T.astype(BF16)


def _prep_rope(P, pos, invf):
    S = P.shape[0]
    R = PREP_ROWS
    nb = S // R
    return pl.pallas_call(
        _prep_rope_kernel,
        grid=(nb,),
        in_specs=[
            pl.BlockSpec((R, 2 * A_WIDTH), lambda i: (i, COL_AQK // (2 * A_WIDTH))),
            pl.BlockSpec((R, A_WIDTH), lambda i: (i, COL_AV // A_WIDTH)),
            pl.BlockSpec((R, 1), lambda i: (i, 0)),
            pl.BlockSpec((1, A_HEAD_DIM), lambda i: (0, 0)),
        ],
        out_specs=[
            pl.BlockSpec((A_HEADS, R, A_HEAD_DIM), lambda i: (0, i, 0)),
            pl.BlockSpec((R, A_WIDTH), lambda i: (i, 0)),
            pl.BlockSpec((A_WIDTH, R), lambda i: (0, i)),
            pl.BlockSpec((1, 1, A_WIDTH), lambda i: (i, 0, 0)),
        ],
        out_shape=[
            jax.ShapeDtypeStruct((A_HEADS, S, A_HEAD_DIM), F32),
            jax.ShapeDtypeStruct((S, A_WIDTH), BF16),
            jax.ShapeDtypeStruct((A_WIDTH, S), BF16),
            jax.ShapeDtypeStruct((nb, 1, A_WIDTH), F32),
        ],
        compiler_params=_cparams(("parallel",)),
        name="prep_rope",
    )(P, P, pos, invf)


M_AUG = M_HEAD_DIM + LANES


def _mlstm_kernel(q_ref, kT_ref, v_ref, mo_ref, gc_ref, gt_ref, gain_ref, out_ref,
                  c_ref, m_ref):
    @pl.when(pl.program_id(0) == 0)
    def _():
        c_ref[...] = jnp.zeros_like(c_ref)
        m_ref[...] = jnp.zeros_like(m_ref)

    L = M_CHUNK
    D = M_HEAD_DIM
    row = lax.broadcasted_iota(jnp.int32, (L, L), 0)
    col = lax.broadcasted_iota(jnp.int32, (L, L), 1)
    causal = row >= col
    ones_col = (lax.broadcasted_iota(jnp.int32, (L, LANES), 1) == 0).astype(BF16)

    heads = range(M_HEADS)
    hsl = [slice(h * D, (h + 1) * D) for h in heads]
    q = [q_ref[:, hsl[h]] for h in heads]
    kT = [kT_ref[hsl[h], :] for h in heads]
    v_aug = [jnp.concatenate([v_ref[:, hsl[h]], ones_col], axis=1) for h in heads]
    b_c = [gc_ref[:, M_HEADS + h:M_HEADS + h + 1] for h in heads]
    b_r = [gt_ref[M_HEADS + h:M_HEADS + h + 1, :] for h in heads]
    u_r = [gt_ref[h:h + 1, :] - b_r[h] for h in heads]
    f_tot = [b_r[h][:, L - 1:L] for h in heads]
    m_prev = [m_ref[h:h + 1, 0:1] for h in heads]

    s_qk = [jnp.dot(q[h], kT[h], preferred_element_type=F32) for h in heads]
    q_c = [jnp.dot(q[h], c_ref[h].astype(BF16), preferred_element_type=F32) for h in heads]

    for h in heads:
        w_r = f_tot[h] + u_r[h]
        m_loc = jnp.max(w_r, axis=1, keepdims=True)
        m_new = jnp.maximum(f_tot[h] + m_prev[h], m_loc)
        a = jnp.exp(f_tot[h] + m_prev[h] - m_new)
        e_r = jnp.exp(w_r - m_new)
        keT = (kT[h].astype(F32) * e_r).astype(BF16)
        c_ref[h] = a * c_ref[h] + jnp.dot(keT, v_aug[h], preferred_element_type=F32)
        m_ref[h:h + 1, :] = jnp.broadcast_to(m_new, (1, LANES))

    for h in heads:
        d_log = jnp.where(causal, b_c[h] + u_r[h], -jnp.inf)
        a_log = b_c[h] + m_prev[h]
        m_t = jnp.maximum(a_log, jnp.max(d_log, axis=1, keepdims=True))
        s_ts = s_qk[h] * jnp.exp(d_log - m_t)
        inter = jnp.exp(a_log - m_t)
        r = inter * q_c[h] + jnp.dot(s_ts.astype(BF16), v_aug[h], preferred_element_type=F32)
        num = r[:, :D]
        den = r[:, D:D + 1]
        hh = num / jnp.maximum(jnp.abs(den), jnp.exp(-m_t))
        hn = hh * lax.rsqrt(jnp.mean(hh * hh, axis=-1, keepdims=True) + NORM_EPS)
        out_ref[:, hsl[h]] = (hn * gain_ref[:, hsl[h]]
                              * _sigmoid(mo_ref[:, hsl[h]].astype(F32))).astype(BF16)


def _mlstm(mq, mkT, P, gc, gt, gain):
    S = mq.shape[0]
    L = M_CHUNK
    return pl.pallas_call(
        _mlstm_kernel,
        grid=(S // L,),
        in_specs=[
            pl.BlockSpec((L, M_WIDTH), lambda c: (c, 0)),
            pl.BlockSpec((M_WIDTH, L), lambda c: (0, c)),
            pl.BlockSpec((L, M_WIDTH), lambda c: (c, COL_MV // M_WIDTH)),
            pl.BlockSpec((L, M_WIDTH), lambda c: (c, COL_MO // M_WIDTH)),
            pl.BlockSpec((L, LANES), lambda c: (c, 0)),
            pl.BlockSpec((SUBLANES, L), lambda c: (0, c)),
            pl.BlockSpec((1, M_WIDTH), lambda c: (0, 0)),
        ],
        out_specs=pl.BlockSpec((L, M_WIDTH), lambda c: (c, 0)),
        out_shape=jax.ShapeDtypeStruct((S, M_WIDTH), BF16),
        scratch_shapes=[
            pltpu.VMEM((M_HEADS, M_HEAD_DIM, M_AUG), F32),
            pltpu.VMEM((SUBLANES, LANES), F32),
        ],
        compiler_params=_cparams(("arbitrary",)),
        name="mlstm",
    )(mq, mkT, P, P, gc, gt, gain)


MOBA_HEADS_PER_STEP = 4


ROUTE_TILES = 264
ROUTE_ROWS = ROUTE_TILES * MOBA_BLOCK
GROUP_TILES_PER_STEP = 8
ROW_WORDS = A_HEAD_DIM // 2
COMBINE_ROWS = 256


def _pack_rows(o_t, lse):
    q = o_t.shape[1]
    hi = pltpu.bitcast(o_t[:ROW_WORDS].astype(BF16).astype(F32), jnp.uint32)
    lo = pltpu.bitcast(o_t[ROW_WORDS:].astype(BF16).astype(F32), jnp.uint32)
    words = pltpu.bitcast(hi | (lo >> 16), F32)
    tail = jnp.concatenate([jnp.broadcast_to(lse, (SUBLANES, q)),
                            jnp.zeros((LANES - ROW_WORDS - SUBLANES, q), F32)], axis=0)
    return jnp.concatenate([words, tail], axis=0).T


def _dot_nt(a, b):
    return lax.dot_general(a, b, (((1,), (1,)), ((), ())), preferred_element_type=F32)


def _block_partial(k_blk, v_t_blk, qs, ones_rows, causal_mask=None):
    s = _dot_nt(k_blk, qs)
    if causal_mask is not None:
        s = jnp.where(causal_mask, s, NEG_BIG)
    sb = s.astype(BF16)
    m = jnp.max(sb, axis=0, keepdims=True)
    p = jnp.exp2(sb - m)
    r = jnp.dot(jnp.concatenate([v_t_blk, ones_rows], axis=0), p, preferred_element_type=F32)
    dh = v_t_blk.shape[0]
    l = r[dh:dh + 1, :]
    return r[:dh, :] / l, m.astype(F32) + jnp.log2(l)


def _route_kernel(q_ref, k_ref, vT_ref, km_ref, route_ref, cnt_ref, own_ref, cnt_acc):
    i = pl.program_id(0)
    BS = MOBA_BLOCK
    Dh = A_HEAD_DIM
    NB = km_ref.shape[0]
    G = q_ref.shape[0]
    blk = lax.broadcasted_iota(jnp.int32, (NB, BS), 0)
    kpos = lax.broadcasted_iota(jnp.int32, (BS, BS), 0)
    qpos = lax.broadcasted_iota(jnp.int32, (BS, BS), 1)
    earlier = (kpos < qpos).astype(BF16)
    ones_rows = jnp.ones((2 * SUBLANES, BS), BF16)

    @pl.when(i == 0)
    def _():
        cnt_acc[...] = jnp.zeros_like(cnt_acc)

    q = [q_ref[g] for g in range(G)]
    qs = [(q[g] * (Dh ** -0.5 * LOG2_E)).astype(BF16) for g in range(G)]
    parts = [_block_partial(k_ref[:, g * Dh:(g + 1) * Dh], vT_ref[g * Dh:(g + 1) * Dh, :], qs[g],
                            ones_rows, kpos <= qpos) for g in range(G)]

    gates = []
    for g in range(G):
        hs = slice(g * Dh, (g + 1) * Dh)
        km = km_ref[:, hs]
        kh = km.astype(BF16)
        kl = (km - kh.astype(F32)).astype(BF16)
        qh = q[g].astype(BF16)
        ql = (q[g] - qh.astype(F32)).astype(BF16)
        gate = _dot_nt(kh, qh) + _dot_nt(kh, ql) + _dot_nt(kl, qh)
        gates.append(jnp.where(blk < i, gate, -jnp.inf))
    picks = [[] for _ in range(G)]
    rows = [[] for _ in range(G)]
    for r in range(MOBA_TOPK):
        for g in range(G):
            mx = jnp.max(gates[g], axis=0, keepdims=True)
            idx = jnp.min(jnp.where(gates[g] == mx, blk, NB), axis=0, keepdims=True)
            idx = jnp.where(r < i, idx, -1)
            pick = blk == idx
            gates[g] = jnp.where(pick, -jnp.inf, gates[g])
            picks[g].append(pick)
            rows[g].append(idx)
    for g in range(G):
        onehot = sum(p.astype(F32) for p in picks[g])
        before = cnt_acc[g][:, 0:1] + jnp.dot(onehot.astype(BF16), earlier, preferred_element_type=F32)
        for r in range(MOBA_TOPK):
            rank = jnp.sum(jnp.where(picks[g][r], before, 0.0), axis=0, keepdims=True)
            rows[g].append(rank.astype(jnp.int32))
        rows[g].append(jnp.zeros((SUBLANES - 2 * MOBA_TOPK, BS), jnp.int32))
        route_ref[g] = jnp.concatenate(rows[g], axis=0)
        cnt_new = cnt_acc[g] + jnp.sum(onehot, axis=1, keepdims=True)
        cnt_acc[g] = cnt_new
        cnt_ref[g] = cnt_new
        own_ref[g] = _pack_rows(*parts[g])


def _route(aq_hm, ak, avT, kmean):
    H, S, Dh = aq_hm.shape
    BS = MOBA_BLOCK
    NB = S // BS
    W = H * Dh
    return pl.pallas_call(
        _route_kernel,
        grid=(NB,),
        in_specs=[
            pl.BlockSpec((H, BS, Dh), lambda i: (0, i, 0)),
            pl.BlockSpec((BS, W), lambda i: (i, 0)),
            pl.BlockSpec((W, BS), lambda i: (0, i)),
            pl.BlockSpec((NB, W), lambda i: (0, 0)),
        ],
        out_specs=[
            pl.BlockSpec((H, SUBLANES, BS), lambda i: (0, 0, i)),
            pl.BlockSpec((H, NB, LANES), lambda i: (0, 0, 0)),
            pl.BlockSpec((H, BS, LANES), lambda i: (0, i, 0)),
        ],
        out_shape=[
            jax.ShapeDtypeStruct((H, SUBLANES, S), jnp.int32),
            jax.ShapeDtypeStruct((H, NB, LANES), F32),
            jax.ShapeDtypeStruct((H, S, LANES), F32),
        ],
        scratch_shapes=[pltpu.VMEM((H, NB, LANES), F32)],
        compiler_params=_cparams(("arbitrary",)),
        name="moba_route",
    )(aq_hm, ak, avT, kmean)


def _dest_kernel(seg_ref, route_ref, dest_ref):
    h = pl.program_id(0)
    NB = seg_ref.shape[1]
    r = route_ref[0]
    blk = r[0:MOBA_TOPK, :]
    rank = r[MOBA_TOPK:2 * MOBA_TOPK, :]
    base = h * ROUTE_ROWS
    dest = jnp.full(blk.shape, base + ROUTE_ROWS - 1, jnp.int32)
    for j in range(NB):
        dest = jnp.where(blk == j, base + seg_ref[h, j] + rank, dest)
    dest_ref[0] = jnp.concatenate(
        [dest, jnp.zeros((SUBLANES - MOBA_TOPK, dest.shape[1]), jnp.int32)], axis=0)


def _dest(seg_start, route, h0):
    H = seg_start.shape[0]
    S = route.shape[2]
    return pl.pallas_call(
        _dest_kernel,
        grid_spec=pltpu.PrefetchScalarGridSpec(
            num_scalar_prefetch=1,
            grid=(H,),
            in_specs=[pl.BlockSpec((1, SUBLANES, S), lambda h, seg: (h0 + h, 0, 0))],
            out_specs=pl.BlockSpec((1, SUBLANES, S), lambda h, seg: (h, 0, 0)),
        ),
        out_shape=jax.ShapeDtypeStruct((H, SUBLANES, S), jnp.int32),
        compiler_params=_cparams(("parallel",)),
        name="moba_dest",
    )(seg_start, route)


SC_WINDOW = 256


def _sc_mesh():
    return plsc.VectorSubcoreMesh(core_axis_name="c", subcore_axis_name="s")


def _sc_scatter(rows, row0, n_rows, idx, n_out):
    W = rows.shape[1]
    M = idx.shape[0]
    n_win = n_rows // SC_WINDOW
    win0 = row0 // SC_WINDOW

    @pl.kernel(out_type=jax.ShapeDtypeStruct((n_out, W), rows.dtype), mesh=_sc_mesh())
    def k(x_hbm, i_hbm, o_hbm):
        def body(x_vmem, i_vmem):
            pltpu.sync_copy(x_vmem, o_hbm.at[i_vmem.at[0]])

        pltpu.emit_pipeline(
            body,
            grid=(M // SC_WINDOW,),
            in_specs=[pl.BlockSpec((SC_WINDOW, W), lambda w: (win0 + lax.rem(w, n_win), 0)),
                      pl.BlockSpec((1, SC_WINDOW), lambda w: (0, w))],
            out_specs=[],
            core_axis_name=("c", "s"),
            dimension_semantics=(pltpu.PARALLEL,),
        )(x_hbm, i_hbm)

    return k(rows, idx.reshape(1, M))


def _sc_gather(table, idx):
    M = idx.shape[0]
    W = table.shape[1]

    @pl.kernel(out_type=jax.ShapeDtypeStruct((M, W), table.dtype), mesh=_sc_mesh())
    def k(x_hbm, i_hbm, o_hbm):
        def body(i_vmem, o_vmem):
            pltpu.sync_copy(x_hbm.at[i_vmem.at[0]], o_vmem)

        pltpu.emit_pipeline(
            body,
            grid=(M // SC_WINDOW,),
            in_specs=[pl.BlockSpec((1, SC_WINDOW), lambda w: (0, w))],
            out_specs=[pl.BlockSpec((SC_WINDOW, W), lambda w: (w, 0))],
            core_axis_name=("c", "s"),
            dimension_semantics=(pltpu.PARALLEL,),
        )(i_hbm, o_hbm)

    return k(table, idx.reshape(1, M))


def _group_kernel(tile_blk_ref, q_ref, k_ref, vT_ref, out_ref, s_ref, m_ref):
    h = pl.program_id(0)
    u = pl.program_id(1)
    last = pl.num_programs(1) - 2
    BS = MOBA_BLOCK
    Dh = A_HEAD_DIM
    T = GROUP_TILES_PER_STEP
    ones_rows = jnp.ones((2 * SUBLANES, BS), BF16)
    new = lax.rem(u, 2)
    old = 1 - new

    @pl.when((h == 0) & (u == 0))
    def _():
        s_ref[...] = jnp.zeros_like(s_ref)
        m_ref[...] = jnp.zeros_like(m_ref)

    def block_start(group, c):
        j = jnp.maximum(tile_blk_ref[h * ROUTE_TILES + group * T + c], 0)
        return pl.multiple_of(j * BS, BS)

    g_new = jnp.minimum(u, last)
    g_old = jnp.maximum(u - 1, 0)
    @pl.when(tile_blk_ref[h * ROUTE_TILES + g_old * T] >= 0)
    def _():
        ms, rs = [], []
        for c in range(T):
            m = m_ref[old, c]
            p = jnp.exp2(s_ref[old, c] - m.astype(BF16))
            v_aug = jnp.concatenate([vT_ref[:, pl.ds(block_start(g_old, c), BS)], ones_rows], axis=0)
            ms.append(m)
            rs.append(jnp.dot(v_aug, p, preferred_element_type=F32))
        s_new = []
        for c in range(T):
            qs = (q_ref[c * BS:(c + 1) * BS, :] * (Dh ** -0.5 * LOG2_E)).astype(BF16)
            s_new.append(_dot_nt(k_ref[pl.ds(block_start(g_new, c), BS), :], qs))
        for c in range(T):
            l = rs[c][Dh:Dh + 1, :]
            out_ref[c * BS:(c + 1) * BS, :] = _pack_rows(rs[c][:Dh, :] / l, ms[c] + jnp.log2(l))
            sb = s_new[c].astype(BF16)
            s_ref[new, c] = sb
            m_ref[new, c] = jnp.max(sb, axis=0, keepdims=True).astype(F32)


def _group(tile_blk, q_sorted, ak, avT, h0):
    S = ak.shape[0]
    H = q_sorted.shape[0] // ROUTE_ROWS
    Dh = A_HEAD_DIM
    BS = MOBA_BLOCK
    T = GROUP_TILES_PER_STEP
    steps = ROUTE_TILES // T
    rows = T * BS
    return pl.pallas_call(
        _group_kernel,
        grid_spec=pltpu.PrefetchScalarGridSpec(
            num_scalar_prefetch=1,
            grid=(H, steps + 1),
            in_specs=[
                pl.BlockSpec((rows, LANES), lambda h, u, tb: (h * steps + jnp.minimum(u, steps - 1), 0)),
                pl.BlockSpec((S, Dh), lambda h, u, tb: (0, h0 + h)),
                pl.BlockSpec((Dh, S), lambda h, u, tb: (h0 + h, 0)),
            ],
            out_specs=pl.BlockSpec((rows, LANES), lambda h, u, tb: (h * steps + jnp.maximum(u - 1, 0), 0)),
            scratch_shapes=[pltpu.VMEM((2, T, BS, BS), BF16), pltpu.VMEM((2, T, 1, BS), F32)],
        ),
        out_shape=jax.ShapeDtypeStruct((H * ROUTE_ROWS, LANES), F32),
        compiler_params=_cparams(("arbitrary", "arbitrary")),
        name="moba_group",
    )(tile_blk, q_sorted, ak, avT)


def _combine_kernel(*refs):
    i = pl.program_id(0)
    own_ref, o_ref = refs[-2], refs[-1]
    n_groups = len(refs) - 2
    H = refs[0].shape[0]
    BS = own_ref.shape[1]
    RC = COMBINE_ROWS
    low_half = lax.broadcasted_iota(jnp.int32, (RC, LANES), 1) < ROW_WORDS
    for head in range(n_groups * H):
        got_ref, h = refs[head // H], head % H
        for r0 in range(0, BS, RC):
            rs = slice(r0, r0 + RC)
            tiles = [own_ref[head, rs, :]]
            for r in range(MOBA_TOPK):
                tiles.append(jnp.where(r < i, got_ref[h, r, rs, :], 0.0))
            lses = [tiles[0][:, ROW_WORDS:ROW_WORDS + 1]]
            lses += [jnp.where(r < i, tiles[r + 1][:, ROW_WORDS:ROW_WORDS + 1], NEG_BIG)
                     for r in range(MOBA_TOPK)]
            top = functools.reduce(jnp.maximum, lses)
            w = [jnp.exp2(x - top) for x in lses]
            inv = 1.0 / sum(w)
            first = jnp.zeros((RC, LANES), F32)
            second = jnp.zeros((RC, LANES), F32)
            for wk, tile in zip(w, tiles):
                words = pltpu.bitcast(tile, jnp.uint32)
                first = first + wk * pltpu.bitcast(words & jnp.uint32(0xFFFF0000), F32)
                second = second + wk * pltpu.bitcast(words << 16, F32)
            o = jnp.where(low_half, first, pltpu.roll(second, ROW_WORDS, 1)) * inv
            o_ref[rs, head * A_HEAD_DIM:(head + 1) * A_HEAD_DIM] = o.astype(BF16)


def _combine(gots, own):
    H = gots[0].shape[0]
    S = own.shape[1]
    BS = MOBA_BLOCK
    return pl.pallas_call(
        _combine_kernel,
        grid=(S // BS,),
        in_specs=([pl.BlockSpec((H, MOBA_TOPK, BS, LANES), lambda i: (0, 0, i, 0))] * len(gots)
                  + [pl.BlockSpec((own.shape[0], BS, LANES), lambda i: (0, i, 0))]),
        out_specs=pl.BlockSpec((BS, A_WIDTH), lambda i: (i, 0)),
        out_shape=jax.ShapeDtypeStruct((S, A_WIDTH), BF16),
        compiler_params=_cparams(("parallel",)),
        name="moba_combine",
    )(*gots, own)


def _moba_dispatch(aq_hm, ak, avT, kmean):
    H, S, Dh = aq_hm.shape
    NB = S // MOBA_BLOCK
    hn = MOBA_HEADS_PER_STEP
    q_rows = aq_hm.reshape(H * S, Dh)
    route, cnt_all, own = _route(aq_hm, ak, avT, kmean)
    groups = []
    for h0 in range(0, H, hn):
        cnt = cnt_all[h0:h0 + hn, :, 0].astype(jnp.int32)
        seg_tiles = (cnt + MOBA_BLOCK - 1) // MOBA_BLOCK
        seg_end = jnp.cumsum(seg_tiles, axis=1)
        seg_start = (seg_end - seg_tiles) * MOBA_BLOCK
        tile_ids = jnp.arange(ROUTE_TILES, dtype=jnp.int32)
        tile_blk = jnp.sum(tile_ids[None, :, None] >= seg_end[:, None, :], axis=2).astype(jnp.int32)
        tile_blk = jnp.where(tile_blk < NB, tile_blk, -1).reshape(hn * ROUTE_TILES)
        dest = _dest(seg_start, route, h0)[:, :MOBA_TOPK, :]
        q_sorted = _sc_scatter(q_rows, h0 * S, hn * S, dest.transpose(1, 0, 2).reshape(-1),
                               hn * ROUTE_ROWS)
        groups.append((h0, tile_blk, dest, q_sorted))
    return groups, own


def _moba_attend(groups, own, ak, avT):
    S = ak.shape[0]
    hn = MOBA_HEADS_PER_STEP
    gots = []
    for h0, tile_blk, dest, q_sorted in groups:
        results = _group(tile_blk, q_sorted, ak, avT, h0)
        gots.append(_sc_gather(results, dest.reshape(-1)).reshape(hn, MOBA_TOPK, S, LANES))
    return _combine(gots, own)


def _merge_kernel(hm_ref, ha_ref, gm_ref, ga_ref, x_ref, wm_ref, wa_ref, wo_ref, gain_ref,
                  out_ref):
    ym = jnp.dot(hm_ref[...], wm_ref[...], preferred_element_type=F32)
    ya = jnp.dot(ha_ref[...], wa_ref[...], preferred_element_type=F32)
    merged = _sigmoid(gm_ref[...].astype(F32)) * ym + _sigmoid(ga_ref[...].astype(F32)) * ya
    mix = jnp.dot(merged.astype(BF16), wo_ref[...], preferred_element_type=F32)
    out_ref[...] = x_ref[...] + _rms(mix, gain_ref[...])


def _merge(hm, ha, P, x, wm, wa, wo, gain, tm=256):
    S, D = x.shape
    const = pl.Buffered(1)
    return pl.pallas_call(
        _merge_kernel,
        grid=(S // tm,),
        in_specs=[
            pl.BlockSpec((tm, M_WIDTH), lambda i: (i, 0)),
            pl.BlockSpec((tm, A_WIDTH), lambda i: (i, 0)),
            pl.BlockSpec((tm, D), lambda i: (i, COL_GM // D)),
            pl.BlockSpec((tm, D), lambda i: (i, COL_GA // D)),
            pl.BlockSpec((tm, D), lambda i: (i, 0)),
            pl.BlockSpec((M_WIDTH, D), lambda i: (0, 0), pipeline_mode=const),
            pl.BlockSpec((A_WIDTH, D), lambda i: (0, 0), pipeline_mode=const),
            pl.BlockSpec((D, D), lambda i: (0, 0), pipeline_mode=const),
            pl.BlockSpec((1, D), lambda i: (0, 0)),
        ],
        out_specs=pl.BlockSpec((tm, D), lambda i: (i, 0)),
        out_shape=jax.ShapeDtypeStruct((S, D), F32),
        compiler_params=_cparams(("parallel",)),
        name="merge",
    )(hm, ha, P, P, x, wm, wa, wo, gain)


def _ffn_kernel(x_ref, gpre_ref, wu_ref, wd_ref, gpost_ref, out_ref, hn_ref, acc_ref):
    f = pl.program_id(1)

    @pl.when(f == 0)
    def _():
        hn_ref[...] = _rms(x_ref[...], gpre_ref[...]).astype(BF16)
        acc_ref[...] = jnp.zeros_like(acc_ref)

    u = jnp.dot(hn_ref[...], wu_ref[...], preferred_element_type=F32)
    u = jnp.square(jnp.maximum(u, 0.0)).astype(BF16)
    acc_ref[...] += jnp.dot(u, wd_ref[...], preferred_element_type=F32)

    @pl.when(f == pl.num_programs(1) - 1)
    def _():
        out_ref[...] = x_ref[...] + _rms(acc_ref[...], gpost_ref[...])


def _ffn(x, gpre, wu, wd, gpost, tm=512, tf=1024):
    S, D = x.shape
    Fd = wu.shape[1]
    return pl.pallas_call(
        _ffn_kernel,
        grid=(S // tm, Fd // tf),
        in_specs=[
            pl.BlockSpec((tm, D), lambda i, f: (i, 0)),
            pl.BlockSpec((1, D), lambda i, f: (0, 0)),
            pl.BlockSpec((D, tf), lambda i, f: (0, f)),
            pl.BlockSpec((tf, D), lambda i, f: (f, 0)),
            pl.BlockSpec((1, D), lambda i, f: (0, 0)),
        ],
        out_specs=pl.BlockSpec((tm, D), lambda i, f: (i, 0)),
        out_shape=jax.ShapeDtypeStruct((S, D), F32),
        scratch_shapes=[pltpu.VMEM((tm, D), BF16), pltpu.VMEM((tm, D), F32)],
        compiler_params=_cparams(("parallel", "arbitrary")),
        name="ffn",
    )(x, gpre, wu, wd, gpost)


def _layer(x, pos, norm_mix_pre, w_in, conv_w, conv_b, i_bias, f_bias, mlstm_norm,
           w_branch_m, w_branch_a, w_out, norm_mix_post, norm_ffn_pre, w_up, w_down,
           norm_ffn_post):
    S, D = x.shape
    o = 0
    pieces = {}
    w_in = w_in.astype(BF16)
    for name, width in (("mq", M_WIDTH), ("mk", M_WIDTH), ("mv", M_WIDTH), ("mo", M_WIDTH),
                        ("mi", M_HEADS), ("mf", M_HEADS), ("aq", A_WIDTH), ("ak", A_WIDTH),
                        ("av", A_WIDTH), ("gm", D), ("ga", D)):
        pieces[name] = w_in[:, o:o + width]
        o += width
    w_moba = jnp.concatenate([pieces[n] for n in ("aq", "ak", "av")], axis=1)
    w_rest = jnp.concatenate([pieces[n] for n in ("gm", "ga", "mq", "mk", "mv", "mo")], axis=1)
    w_gate = jnp.concatenate(
        [pieces["mi"], pieces["mf"], jnp.zeros((D, LANES - 2 * M_HEADS), BF16)], axis=1)
    gbias = jnp.concatenate([i_bias, f_bias, jnp.zeros((LANES - 2 * M_HEADS,), F32)])[None, :]
    half = jnp.arange(0, A_HEAD_DIM, 2, dtype=F32) / A_HEAD_DIM
    inv_freq = 1.0 / (ROPE_THETA ** half)
    invf = jnp.concatenate([inv_freq, inv_freq])[None, :]

    gain = norm_mix_pre[None, :]
    p_moba, _ = _proj(x, gain, w_moba, w_gate)
    aq, ak, avT, kmean = _prep_rope(p_moba, pos.reshape(S, 1), invf)
    groups, own = _moba_dispatch(aq, ak, avT, kmean.reshape(S // MOBA_BLOCK, A_WIDTH))
    P, gate = _proj(x, gain, w_rest, w_gate)
    mq, mkT, gc, gt = _prep_conv(P, gate, conv_w, conv_b[None, :], gbias)
    hm = _mlstm(mq, mkT, P, gc, gt, mlstm_norm[None, :])
    ha = _moba_attend(groups, own, ak, avT)
    x1 = _merge(hm, ha, P, x, w_branch_m.astype(BF16), w_branch_a.astype(BF16),
                w_out.astype(BF16), norm_mix_post[None, :])
    return _ffn(x1, norm_ffn_pre[None, :], w_up.astype(BF16), w_down.astype(BF16),
                norm_ffn_post[None, :])


def kernel(x, positions, norm_mix_pre, w_in, conv_w, conv_b, i_bias, f_bias, mlstm_norm,
           w_branch_m, w_branch_a, w_out, norm_mix_post, norm_ffn_pre, w_up, w_down,
           norm_ffn_post):
    B = x.shape[0]
    depth = w_in.shape[0]
    outs = []
    for b in range(B):
        xb = x[b]
        for l in range(depth):
            xb = _layer(xb, positions[b], norm_mix_pre[l], w_in[l], conv_w[l], conv_b[l],
                        i_bias[l], f_bias[l], mlstm_norm[l], w_branch_m[l], w_branch_a[l],
                        w_out[l], norm_mix_post[l], norm_ffn_pre[l], w_up[l], w_down[l],
                        norm_ffn_post[l])
        outs.append(xb)
    return outs[0][None] if B == 1 else jnp.stack(outs, axis=0)
```

```python
import functools

import jax
import jax.numpy as jnp
from jax import lax
from jax.experimental import pallas as pl
from jax.experimental.pallas import tpu as pltpu
from jax.experimental.pallas import tpu_sc as plsc

F32 = jnp.float32
BF16 = jnp.bfloat16

M_HEADS = 4
M_HEAD_DIM = 256
M_WIDTH = M_HEADS * M_HEAD_DIM
M_CHUNK = 128
CONV_WIDTH = 4
A_HEADS = 8
A_HEAD_DIM = 128
A_WIDTH = A_HEADS * A_HEAD_DIM
MOBA_BLOCK = 256
MOBA_TOPK = 3
ROPE_THETA = 10000.0
NORM_EPS = 1e-6

LANES = 128
SUBLANES = 8
VMEM_LIMIT = 56 * 1024 * 1024
NEG_BIG = -1e30
LOG2_E = 1.4426950408889634

COL_GM = 0
COL_GA = 2048
COL_MQK = 4096
COL_AQK = 6144
COL_MV = 8192
COL_MO = 9216
COL_AV = 10240


def _cparams(sem):
    return pltpu.CompilerParams(dimension_semantics=sem, vmem_limit_bytes=VMEM_LIMIT)


def _rms(x, gain):
    ms = jnp.mean(x * x, axis=-1, keepdims=True)
    return x * lax.rsqrt(ms + NORM_EPS) * gain


def _sigmoid(x):
    return 1.0 / (1.0 + jnp.exp(-x))


def _split3(x):
    hi = x.astype(BF16)
    r1 = x - hi.astype(F32)
    mid = r1.astype(BF16)
    lo = (r1 - mid.astype(F32)).astype(BF16)
    return hi, mid, lo


def _proj_kernel(x_ref, g_ref, w_ref, wg_ref, p_ref, gate_ref, xn_ref):
    @pl.when(pl.program_id(1) == 0)
    def _():
        xn = _rms(x_ref[...], g_ref[...]).astype(BF16)
        xn_ref[...] = xn
        gate_ref[...] = jnp.dot(xn, wg_ref[...], preferred_element_type=F32)

    p_ref[...] = jnp.dot(xn_ref[...], w_ref[...], preferred_element_type=F32).astype(BF16)


def _proj(x, gain, w_all, w_gate, tm=1024, tn=1024):
    S, D = x.shape
    N = w_all.shape[1]
    return pl.pallas_call(
        _proj_kernel,
        grid=(S // tm, N // tn),
        in_specs=[
            pl.BlockSpec((tm, D), lambda i, j: (i, 0)),
            pl.BlockSpec((1, D), lambda i, j: (0, 0)),
            pl.BlockSpec((D, tn), lambda i, j: (0, j)),
            pl.BlockSpec((D, LANES), lambda i, j: (0, 0)),
        ],
        out_specs=[
            pl.BlockSpec((tm, tn), lambda i, j: (i, j)),
            pl.BlockSpec((tm, LANES), lambda i, j: (i, 0)),
        ],
        out_shape=[
            jax.ShapeDtypeStruct((S, N), BF16),
            jax.ShapeDtypeStruct((S, LANES), F32),
        ],
        scratch_shapes=[pltpu.VMEM((tm, D), BF16)],
        compiler_params=_cparams(("parallel", "arbitrary")),
        name="proj",
    )(x, gain, w_all, w_gate)


PREP_ROWS = MOBA_BLOCK
PREP_COLS = 512
HALO_ROWS = 2 * SUBLANES


def _prep_kernel(pmk_ref, halo_ref, paqk_ref, pav_ref, gate_ref, pos_ref, invf_ref,
                 cw_ref, cb_ref, gb_ref,
                 mq_ref, mkT_ref, aq_ref, ak_ref, avT_ref, kmean_ref, gc_ref, gt_ref):
    i = pl.program_id(0)
    R = PREP_ROWS

    k_scale = M_HEAD_DIM ** -0.5
    for c0 in range(0, 2 * M_WIDTH, PREP_COLS):
        cs = slice(c0, c0 + PREP_COLS)
        prev = halo_ref[:, cs].astype(F32)[HALO_ROWS - SUBLANES:, :]
        prev = jnp.where(i == 0, jnp.zeros_like(prev), prev)
        ext = jnp.concatenate([pmk_ref[:, cs].astype(F32), prev], axis=0)
        acc = cw_ref[0:1, cs] * ext
        for j in range(1, CONV_WIDTH):
            acc = pltpu.roll(acc, 1, 0) + cw_ref[j:j + 1, cs] * ext
        acc = acc[0:R, :] + cb_ref[:, cs]
        y = acc * _sigmoid(acc)
        if c0 < M_WIDTH:
            mq_ref[:, cs] = y.astype(BF16)
        else:
            ks = slice(c0 - M_WIDTH, c0 - M_WIDTH + PREP_COLS)
            mkT_ref[ks, :] = (y * k_scale).T.astype(BF16)

    ang = pos_ref[...].astype(F32) * invf_ref[...]
    cos = jnp.cos(ang)
    lane = lax.broadcasted_iota(jnp.int32, (R, A_HEAD_DIM), 1)
    sin_signed = jnp.where(lane < A_HEAD_DIM // 2, -1.0, 1.0) * jnp.sin(ang)
    for h in range(2 * A_HEADS):
        hs = slice(h * A_HEAD_DIM, (h + 1) * A_HEAD_DIM)
        xh = paqk_ref[:, hs].astype(F32)
        yh = xh * cos + pltpu.roll(xh, A_HEAD_DIM // 2, 1) * sin_signed
        if h < A_HEADS:
            aq_ref[h] = yh
        else:
            ko = slice((h - A_HEADS) * A_HEAD_DIM, (h - A_HEADS + 1) * A_HEAD_DIM)
            ak_ref[:, ko] = yh.astype(BF16)
            kmean_ref[0, :, ko] = jnp.mean(yh, axis=0, keepdims=True)
    for c0 in range(0, A_WIDTH, PREP_COLS):
        avT_ref[c0:c0 + PREP_COLS, :] = pav_ref[:, c0:c0 + PREP_COLS].astype(F32).T.astype(BF16)

    g = gate_ref[...] + gb_ref[...]
    log_f = jnp.minimum(g, 0.0) - jnp.log1p(jnp.exp(-jnp.abs(g)))
    r_i = lax.broadcasted_iota(jnp.int32, (R, R), 0)
    c_i = lax.broadcasted_iota(jnp.int32, (R, R), 1)
    tri = ((r_i >= c_i) & ((r_i // M_CHUNK) == (c_i // M_CHUNK))).astype(BF16)
    hi, mid, lo = _split3(log_f)
    csum = (jnp.dot(tri, hi, preferred_element_type=F32)
            + jnp.dot(tri, mid, preferred_element_type=F32)
            + jnp.dot(tri, lo, preferred_element_type=F32))
    glane = lax.broadcasted_iota(jnp.int32, (R, LANES), 1)
    gc = jnp.where(glane < M_HEADS, g, csum)
    gc_ref[...] = gc
    gt_ref[...] = gc.T[0:SUBLANES, :]


def _prep(P, gate, pos, invf, conv_w, conv_b, gbias):
    S = P.shape[0]
    R = PREP_ROWS
    nb = S // R
    halo_blocks = R // HALO_ROWS
    return pl.pallas_call(
        _prep_kernel,
        grid=(nb,),
        in_specs=[
            pl.BlockSpec((R, 2 * M_WIDTH), lambda i: (i, COL_MQK // (2 * M_WIDTH))),
            pl.BlockSpec((HALO_ROWS, 2 * M_WIDTH),
                         lambda i: (jnp.maximum(i * halo_blocks - 1, 0), COL_MQK // (2 * M_WIDTH))),
            pl.BlockSpec((R, 2 * A_WIDTH), lambda i: (i, COL_AQK // (2 * A_WIDTH))),
            pl.BlockSpec((R, A_WIDTH), lambda i: (i, COL_AV // A_WIDTH)),
            pl.BlockSpec((R, LANES), lambda i: (i, 0)),
            pl.BlockSpec((R, 1), lambda i: (i, 0)),
            pl.BlockSpec((1, A_HEAD_DIM), lambda i: (0, 0)),
            pl.BlockSpec((CONV_WIDTH, 2 * M_WIDTH), lambda i: (0, 0)),
            pl.BlockSpec((1, 2 * M_WIDTH), lambda i: (0, 0)),
            pl.BlockSpec((1, LANES), lambda i: (0, 0)),
        ],
        out_specs=[
            pl.BlockSpec((R, M_WIDTH), lambda i: (i, 0)),
            pl.BlockSpec((M_WIDTH, R), lambda i: (0, i)),
            pl.BlockSpec((A_HEADS, R, A_HEAD_DIM), lambda i: (0, i, 0)),
            pl.BlockSpec((R, A_WIDTH), lambda i: (i, 0)),
            pl.BlockSpec((A_WIDTH, R), lambda i: (0, i)),
            pl.BlockSpec((1, 1, A_WIDTH), lambda i: (i, 0, 0)),
            pl.BlockSpec((R, LANES), lambda i: (i, 0)),
            pl.BlockSpec((SUBLANES, R), lambda i: (0, i)),
        ],
        out_shape=[
            jax.ShapeDtypeStruct((S, M_WIDTH), BF16),
            jax.ShapeDtypeStruct((M_WIDTH, S), BF16),
            jax.ShapeDtypeStruct((A_HEADS, S, A_HEAD_DIM), F32),
            jax.ShapeDtypeStruct((S, A_WIDTH), BF16),
            jax.ShapeDtypeStruct((A_WIDTH, S), BF16),
            jax.ShapeDtypeStruct((nb, 1, A_WIDTH), F32),
            jax.ShapeDtypeStruct((S, LANES), F32),
            jax.ShapeDtypeStruct((SUBLANES, S), F32),
        ],
        compiler_params=_cparams(("parallel",)),
        name="prep",
    )(P, P, P, P, gate, pos, invf, conv_w, conv_b, gbias)


M_AUG = M_HEAD_DIM + LANES


def _mlstm_kernel(q_ref, kT_ref, v_ref, mo_ref, gc_ref, gt_ref, gain_ref, out_ref,
                  c_ref, m_ref):
    @pl.when(pl.program_id(0) == 0)
    def _():
        c_ref[...] = jnp.zeros_like(c_ref)
        m_ref[...] = jnp.zeros_like(m_ref)

    L = M_CHUNK
    D = M_HEAD_DIM
    row = lax.broadcasted_iota(jnp.int32, (L, L), 0)
    col = lax.broadcasted_iota(jnp.int32, (L, L), 1)
    causal = row >= col
    ones_col = (lax.broadcasted_iota(jnp.int32, (L, LANES), 1) == 0).astype(BF16)

    heads = range(M_HEADS)
    hsl = [slice(h * D, (h + 1) * D) for h in heads]
    q = [q_ref[:, hsl[h]] for h in heads]
    kT = [kT_ref[hsl[h], :] for h in heads]
    v_aug = [jnp.concatenate([v_ref[:, hsl[h]], ones_col], axis=1) for h in heads]
    b_c = [gc_ref[:, M_HEADS + h:M_HEADS + h + 1] for h in heads]
    b_r = [gt_ref[M_HEADS + h:M_HEADS + h + 1, :] for h in heads]
    u_r = [gt_ref[h:h + 1, :] - b_r[h] for h in heads]
    f_tot = [b_r[h][:, L - 1:L] for h in heads]
    m_prev = [m_ref[h:h + 1, 0:1] for h in heads]

    s_qk = [jnp.dot(q[h], kT[h], preferred_element_type=F32) for h in heads]
    q_c = [jnp.dot(q[h], c_ref[h].astype(BF16), preferred_element_type=F32) for h in heads]

    for h in heads:
        w_r = f_tot[h] + u_r[h]
        m_loc = jnp.max(w_r, axis=1, keepdims=True)
        m_new = jnp.maximum(f_tot[h] + m_prev[h], m_loc)
        a = jnp.exp(f_tot[h] + m_prev[h] - m_new)
        e_r = jnp.exp(w_r - m_new)
        keT = (kT[h].astype(F32) * e_r).astype(BF16)
        c_ref[h] = a * c_ref[h] + jnp.dot(keT, v_aug[h], preferred_element_type=F32)
        m_ref[h:h + 1, :] = jnp.broadcast_to(m_new, (1, LANES))

    for h in heads:
        d_log = jnp.where(causal, b_c[h] + u_r[h], -jnp.inf)
        a_log = b_c[h] + m_prev[h]
        m_t = jnp.maximum(a_log, jnp.max(d_log, axis=1, keepdims=True))
        s_ts = s_qk[h] * jnp.exp(d_log - m_t)
        inter = jnp.exp(a_log - m_t)
        r = inter * q_c[h] + jnp.dot(s_ts.astype(BF16), v_aug[h], preferred_element_type=F32)
        num = r[:, :D]
        den = r[:, D:D + 1]
        hh = num / jnp.maximum(jnp.abs(den), jnp.exp(-m_t))
        hn = hh * lax.rsqrt(jnp.mean(hh * hh, axis=-1, keepdims=True) + NORM_EPS)
        out_ref[:, hsl[h]] = (hn * gain_ref[:, hsl[h]]
                              * _sigmoid(mo_ref[:, hsl[h]].astype(F32))).astype(BF16)


def _mlstm(mq, mkT, P, gc, gt, gain):
    S = mq.shape[0]
    L = M_CHUNK
    return pl.pallas_call(
        _mlstm_kernel,
        grid=(S // L,),
        in_specs=[
            pl.BlockSpec((L, M_WIDTH), lambda c: (c, 0)),
            pl.BlockSpec((M_WIDTH, L), lambda c: (0, c)),
            pl.BlockSpec((L, M_WIDTH), lambda c: (c, COL_MV // M_WIDTH)),
            pl.BlockSpec((L, M_WIDTH), lambda c: (c, COL_MO // M_WIDTH)),
            pl.BlockSpec((L, LANES), lambda c: (c, 0)),
            pl.BlockSpec((SUBLANES, L), lambda c: (0, c)),
            pl.BlockSpec((1, M_WIDTH), lambda c: (0, 0)),
        ],
        out_specs=pl.BlockSpec((L, M_WIDTH), lambda c: (c, 0)),
        out_shape=jax.ShapeDtypeStruct((S, M_WIDTH), BF16),
        scratch_shapes=[
            pltpu.VMEM((M_HEADS, M_HEAD_DIM, M_AUG), F32),
            pltpu.VMEM((SUBLANES, LANES), F32),
        ],
        compiler_params=_cparams(("arbitrary",)),
        name="mlstm",
    )(mq, mkT, P, P, gc, gt, gain)


MOBA_HEADS_PER_GROUP = 4
ROUTE_TILES = 264
ROUTE_ROWS = ROUTE_TILES * MOBA_BLOCK
GROUP_TILES_PER_STEP = 8
ROW_WORDS = A_HEAD_DIM // 2
SC_WINDOW = 256


def _pack_rows(o_t, lse):
    q = o_t.shape[1]
    hi = pltpu.bitcast(o_t[:ROW_WORDS].astype(BF16).astype(F32), jnp.uint32)
    lo = pltpu.bitcast(o_t[ROW_WORDS:].astype(BF16).astype(F32), jnp.uint32)
    words = pltpu.bitcast(hi | (lo >> 16), F32)
    tail = jnp.concatenate([jnp.broadcast_to(lse, (SUBLANES, q)),
                            jnp.zeros((LANES - ROW_WORDS - SUBLANES, q), F32)], axis=0)
    return jnp.concatenate([words, tail], axis=0).T


def _dot_nt(a, b):
    return lax.dot_general(a, b, (((1,), (1,)), ((), ())), preferred_element_type=F32)


def _block_partial(k_blk, v_t_blk, qs, ones_rows, causal_mask=None):
    s = _dot_nt(k_blk, qs)
    if causal_mask is not None:
        s = jnp.where(causal_mask, s, NEG_BIG)
    sb = s.astype(BF16)
    m = jnp.max(sb, axis=0, keepdims=True)
    p = jnp.exp2(sb - m)
    r = jnp.dot(jnp.concatenate([v_t_blk, ones_rows], axis=0), p, preferred_element_type=F32)
    dh = v_t_blk.shape[0]
    l = r[dh:dh + 1, :]
    return r[:dh, :] / l, m.astype(F32) + jnp.log2(l)


def _route_kernel(q_ref, k_ref, vT_ref, km_ref, route_ref, cnt_ref, own_ref, cnt_acc):
    i = pl.program_id(1)
    BS = MOBA_BLOCK
    Dh = A_HEAD_DIM
    NB = km_ref.shape[0]
    G = q_ref.shape[0]
    blk = lax.broadcasted_iota(jnp.int32, (NB, BS), 0)
    kpos = lax.broadcasted_iota(jnp.int32, (BS, BS), 0)
    qpos = lax.broadcasted_iota(jnp.int32, (BS, BS), 1)
    earlier = (kpos < qpos).astype(BF16)
    ones_rows = jnp.ones((2 * SUBLANES, BS), BF16)

    @pl.when(i == 0)
    def _():
        cnt_acc[...] = jnp.zeros_like(cnt_acc)

    q = [q_ref[g] for g in range(G)]
    qs = [(q[g] * (Dh ** -0.5 * LOG2_E)).astype(BF16) for g in range(G)]
    parts = [_block_partial(k_ref[:, g * Dh:(g + 1) * Dh], vT_ref[g * Dh:(g + 1) * Dh, :], qs[g],
                            ones_rows, kpos <= qpos) for g in range(G)]

    gates = []
    for g in range(G):
        hs = slice(g * Dh, (g + 1) * Dh)
        km = km_ref[:, hs]
        kh = km.astype(BF16)
        kl = (km - kh.astype(F32)).astype(BF16)
        qh = q[g].astype(BF16)
        ql = (q[g] - qh.astype(F32)).astype(BF16)
        gate = _dot_nt(kh, qh) + _dot_nt(kh, ql) + _dot_nt(kl, qh)
        gates.append(jnp.where(blk < i, gate, -jnp.inf))
    picks = [[] for _ in range(G)]
    rows = [[] for _ in range(G)]
    for r in range(MOBA_TOPK):
        for g in range(G):
            mx = jnp.max(gates[g], axis=0, keepdims=True)
            idx = jnp.min(jnp.where(gates[g] == mx, blk, NB), axis=0, keepdims=True)
            idx = jnp.where(r < i, idx, -1)
            pick = blk == idx
            gates[g] = jnp.where(pick, -jnp.inf, gates[g])
            picks[g].append(pick)
            rows[g].append(idx)
    for g in range(G):
        onehot = sum(p.astype(F32) for p in picks[g])
        before = cnt_acc[g][:, 0:1] + jnp.dot(onehot.astype(BF16), earlier, preferred_element_type=F32)
        for r in range(MOBA_TOPK):
            rank = jnp.sum(jnp.where(picks[g][r], before, 0.0), axis=0, keepdims=True)
            rows[g].append(rank.astype(jnp.int32))
        rows[g].append(jnp.zeros((SUBLANES - 2 * MOBA_TOPK, BS), jnp.int32))
        route_ref[g] = jnp.concatenate(rows[g], axis=0)
        cnt_new = cnt_acc[g] + jnp.sum(onehot, axis=1, keepdims=True)
        cnt_acc[g] = cnt_new
        cnt_ref[g] = cnt_new
        own_ref[g] = _pack_rows(*parts[g])


def _route(aq_hm, ak, avT, kmean, h0, H):
    _, S, Dh = aq_hm.shape
    BS = MOBA_BLOCK
    NB = S // BS
    G = MOBA_HEADS_PER_GROUP
    W = G * Dh
    hb = h0 // G
    return pl.pallas_call(
        _route_kernel,
        grid=(H // G, NB),
        in_specs=[
            pl.BlockSpec((G, BS, Dh), lambda h, i: (hb + h, i, 0)),
            pl.BlockSpec((BS, W), lambda h, i: (i, hb + h)),
            pl.BlockSpec((W, BS), lambda h, i: (hb + h, i)),
            pl.BlockSpec((NB, W), lambda h, i: (0, hb + h)),
        ],
        out_specs=[
            pl.BlockSpec((G, SUBLANES, BS), lambda h, i: (h, 0, i)),
            pl.BlockSpec((G, NB, LANES), lambda h, i: (h, 0, 0)),
            pl.BlockSpec((G, BS, LANES), lambda h, i: (h, i, 0)),
        ],
        out_shape=[
            jax.ShapeDtypeStruct((H, SUBLANES, S), jnp.int32),
            jax.ShapeDtypeStruct((H, NB, LANES), F32),
            jax.ShapeDtypeStruct((H, S, LANES), F32),
        ],
        scratch_shapes=[pltpu.VMEM((G, NB, LANES), F32)],
        compiler_params=_cparams(("parallel", "arbitrary")),
        name="moba_route",
    )(aq_hm, ak, avT, kmean)


def _dest_kernel(seg_ref, route_ref, dest_ref):
    h = pl.program_id(0)
    NB = seg_ref.shape[1]
    r = route_ref[0]
    blk = r[0:MOBA_TOPK, :]
    rank = r[MOBA_TOPK:2 * MOBA_TOPK, :]
    base = h * ROUTE_ROWS
    dest = jnp.full(blk.shape, base + ROUTE_ROWS - 1, jnp.int32)
    for j in range(NB):
        dest = jnp.where(blk == j, base + seg_ref[h, j] + rank, dest)
    dest_ref[0] = jnp.concatenate(
        [dest, jnp.zeros((SUBLANES - MOBA_TOPK, dest.shape[1]), jnp.int32)], axis=0)


def _dest(seg_start, route):
    H, _, S = route.shape
    return pl.pallas_call(
        _dest_kernel,
        grid_spec=pltpu.PrefetchScalarGridSpec(
            num_scalar_prefetch=1,
            grid=(H,),
            in_specs=[pl.BlockSpec((1, SUBLANES, S), lambda h, seg: (h, 0, 0))],
            out_specs=pl.BlockSpec((1, SUBLANES, S), lambda h, seg: (h, 0, 0)),
        ),
        out_shape=jax.ShapeDtypeStruct((H, SUBLANES, S), jnp.int32),
        compiler_params=_cparams(("parallel",)),
        name="moba_dest",
    )(seg_start, route)


def _sc_mesh():
    return plsc.VectorSubcoreMesh(core_axis_name="c", subcore_axis_name="s")


def _sc_scatter(rows, row0, n_rows, idx, n_out):
    W = rows.shape[1]
    M = idx.shape[0]
    n_win = n_rows // SC_WINDOW
    win0 = row0 // SC_WINDOW

    @pl.kernel(out_type=jax.ShapeDtypeStruct((n_out, W), rows.dtype), mesh=_sc_mesh())
    def k(x_hbm, i_hbm, o_hbm):
        def body(x_vmem, i_vmem):
            pltpu.sync_copy(x_vmem, o_hbm.at[i_vmem.at[0]])

        pltpu.emit_pipeline(
            body,
            grid=(M // SC_WINDOW,),
            in_specs=[pl.BlockSpec((SC_WINDOW, W), lambda w: (win0 + lax.rem(w, n_win), 0)),
                      pl.BlockSpec((1, SC_WINDOW), lambda w: (0, w))],
            out_specs=[],
            core_axis_name=("c", "s"),
            dimension_semantics=(pltpu.PARALLEL,),
        )(x_hbm, i_hbm)

    return k(rows, idx.reshape(1, M))


def _sc_gather(table, idx):
    M = idx.shape[0]
    W = table.shape[1]

    @pl.kernel(out_type=jax.ShapeDtypeStruct((M, W), table.dtype), mesh=_sc_mesh())
    def k(x_hbm, i_hbm, o_hbm):
        def body(i_vmem, o_vmem):
            pltpu.sync_copy(x_hbm.at[i_vmem.at[0]], o_vmem)

        pltpu.emit_pipeline(
            body,
            grid=(M // SC_WINDOW,),
            in_specs=[pl.BlockSpec((1, SC_WINDOW), lambda w: (0, w))],
            out_specs=[pl.BlockSpec((SC_WINDOW, W), lambda w: (w, 0))],
            core_axis_name=("c", "s"),
            dimension_semantics=(pltpu.PARALLEL,),
        )(i_hbm, o_hbm)

    return k(table, idx.reshape(1, M))


def _group_kernel(tile_blk_ref, q_ref, k_ref, vT_ref, out_ref, s_ref, m_ref):
    h = pl.program_id(0)
    u = pl.program_id(1)
    last = pl.num_programs(1) - 2
    BS = MOBA_BLOCK
    Dh = A_HEAD_DIM
    T = GROUP_TILES_PER_STEP
    ones_rows = jnp.ones((2 * SUBLANES, BS), BF16)
    new = lax.rem(u, 2)
    old = 1 - new

    @pl.when((h == 0) & (u == 0))
    def _():
        s_ref[...] = jnp.zeros_like(s_ref)
        m_ref[...] = jnp.zeros_like(m_ref)

    def block_start(group, c):
        j = jnp.maximum(tile_blk_ref[h * ROUTE_TILES + group * T + c], 0)
        return pl.multiple_of(j * BS, BS)

    g_new = jnp.minimum(u, last)
    g_old = jnp.maximum(u - 1, 0)
    @pl.when(tile_blk_ref[h * ROUTE_TILES + g_old * T] >= 0)
    def _():
        ms, rs = [], []
        for c in range(T):
            m = m_ref[old, c]
            p = jnp.exp2(s_ref[old, c] - m.astype(BF16))
            v_aug = jnp.concatenate([vT_ref[:, pl.ds(block_start(g_old, c), BS)], ones_rows], axis=0)
            ms.append(m)
            rs.append(jnp.dot(v_aug, p, preferred_element_type=F32))
        s_new = []
        for c in range(T):
            qs = (q_ref[c * BS:(c + 1) * BS, :] * (Dh ** -0.5 * LOG2_E)).astype(BF16)
            s_new.append(_dot_nt(k_ref[pl.ds(block_start(g_new, c), BS), :], qs))
        for c in range(T):
            l = rs[c][Dh:Dh + 1, :]
            out_ref[c * BS:(c + 1) * BS, :] = _pack_rows(rs[c][:Dh, :] / l, ms[c] + jnp.log2(l))
            sb = s_new[c].astype(BF16)
            s_ref[new, c] = sb
            m_ref[new, c] = jnp.max(sb, axis=0, keepdims=True).astype(F32)


def _group(tile_blk, q_sorted, ak, avT, h0):
    S = ak.shape[0]
    H = q_sorted.shape[0] // ROUTE_ROWS
    Dh = A_HEAD_DIM
    BS = MOBA_BLOCK
    T = GROUP_TILES_PER_STEP
    steps = ROUTE_TILES // T
    rows = T * BS
    return pl.pallas_call(
        _group_kernel,
        grid_spec=pltpu.PrefetchScalarGridSpec(
            num_scalar_prefetch=1,
            grid=(H, steps + 1),
            in_specs=[
                pl.BlockSpec((rows, LANES), lambda h, u, tb: (h * steps + jnp.minimum(u, steps - 1), 0)),
                pl.BlockSpec((S, Dh), lambda h, u, tb: (0, h0 + h)),
                pl.BlockSpec((Dh, S), lambda h, u, tb: (h0 + h, 0)),
            ],
            out_specs=pl.BlockSpec((rows, LANES), lambda h, u, tb: (h * steps + jnp.maximum(u - 1, 0), 0)),
            scratch_shapes=[pltpu.VMEM((2, T, BS, BS), BF16), pltpu.VMEM((2, T, 1, BS), F32)],
        ),
        out_shape=jax.ShapeDtypeStruct((H * ROUTE_ROWS, LANES), F32),
        compiler_params=_cparams(("arbitrary", "arbitrary")),
        name="moba_group",
    )(tile_blk, q_sorted, ak, avT)


def _combine_kernel(*refs):
    i = pl.program_id(0)
    o_ref = refs[-1]
    n_groups = (len(refs) - 1) // 2
    H = refs[n_groups].shape[0]
    BS = refs[n_groups].shape[1]
    low_half = lax.broadcasted_iota(jnp.int32, (BS, LANES), 1) < ROW_WORDS
    for head in range(n_groups * H):
        got_ref, own_ref, h = refs[head // H], refs[n_groups + head // H], head % H
        tiles = [own_ref[h]]
        for r in range(MOBA_TOPK):
            tiles.append(jnp.where(r < i, got_ref[h, r], 0.0))
        lses = [tiles[0][:, ROW_WORDS:ROW_WORDS + 1]]
        lses += [jnp.where(r < i, tiles[r + 1][:, ROW_WORDS:ROW_WORDS + 1], NEG_BIG)
                 for r in range(MOBA_TOPK)]
        top = functools.reduce(jnp.maximum, lses)
        w = [jnp.exp2(x - top) for x in lses]
        inv = 1.0 / sum(w)
        first = jnp.zeros((BS, LANES), F32)
        second = jnp.zeros((BS, LANES), F32)
        for wk, tile in zip(w, tiles):
            words = pltpu.bitcast(tile, jnp.uint32)
            first = first + wk * pltpu.bitcast(words & jnp.uint32(0xFFFF0000), F32)
            second = second + wk * pltpu.bitcast(words << 16, F32)
        o = jnp.where(low_half, first, pltpu.roll(second, ROW_WORDS, 1)) * inv
        o_ref[:, head * A_HEAD_DIM:(head + 1) * A_HEAD_DIM] = o.astype(BF16)


def _combine(gots, owns):
    H, S, _ = owns[0].shape
    BS = MOBA_BLOCK
    return pl.pallas_call(
        _combine_kernel,
        grid=(S // BS,),
        in_specs=([pl.BlockSpec((H, MOBA_TOPK, BS, LANES), lambda i: (0, 0, i, 0))] * len(gots)
                  + [pl.BlockSpec((H, BS, LANES), lambda i: (0, i, 0))] * len(owns)),
        out_specs=pl.BlockSpec((BS, A_WIDTH), lambda i: (i, 0)),
        out_shape=jax.ShapeDtypeStruct((S, A_WIDTH), BF16),
        compiler_params=_cparams(("parallel",)),
        name="moba_combine",
    )(*gots, *owns)


def _moba_routed(aq_hm, ak, avT, kmean):
    H, S, Dh = aq_hm.shape
    NB = S // MOBA_BLOCK
    hn = MOBA_HEADS_PER_GROUP
    q_rows = aq_hm.reshape(H * S, Dh)
    gots, owns = [], []
    for h0 in range(0, H, hn):
        route, cnt, own = _route(aq_hm, ak, avT, kmean, h0, hn)
        cnt = cnt[:, :, 0].astype(jnp.int32)
        seg_tiles = (cnt + MOBA_BLOCK - 1) // MOBA_BLOCK
        seg_end = jnp.cumsum(seg_tiles, axis=1)
        seg_start = (seg_end - seg_tiles) * MOBA_BLOCK
        tile_ids = jnp.arange(ROUTE_TILES, dtype=jnp.int32)
        tile_blk = jnp.sum(tile_ids[None, :, None] >= seg_end[:, None, :], axis=2).astype(jnp.int32)
        tile_blk = jnp.where(tile_blk < NB, tile_blk, -1).reshape(hn * ROUTE_TILES)
        dest = _dest(seg_start, route)[:, :MOBA_TOPK, :]
        q_sorted = _sc_scatter(q_rows, h0 * S, hn * S, dest.transpose(1, 0, 2).reshape(-1),
                               hn * ROUTE_ROWS)
        results = _group(tile_blk, q_sorted, ak, avT, h0)
        gots.append(_sc_gather(results, dest.reshape(-1)).reshape(hn, MOBA_TOPK, S, LANES))
        owns.append(own)
    return _combine(gots, owns)


def _merge_kernel(hm_ref, ha_ref, gm_ref, ga_ref, x_ref, wm_ref, wa_ref, wo_ref, gain_ref,
                  out_ref):
    ym = jnp.dot(hm_ref[...], wm_ref[...], preferred_element_type=F32)
    ya = jnp.dot(ha_ref[...], wa_ref[...], preferred_element_type=F32)
    merged = _sigmoid(gm_ref[...].astype(F32)) * ym + _sigmoid(ga_ref[...].astype(F32)) * ya
    mix = jnp.dot(merged.astype(BF16), wo_ref[...], preferred_element_type=F32)
    out_ref[...] = x_ref[...] + _rms(mix, gain_ref[...])


def _merge(hm, ha, P, x, wm, wa, wo, gain, tm=256):
    S, D = x.shape
    const = pl.Buffered(1)
    return pl.pallas_call(
        _merge_kernel,
        grid=(S // tm,),
        in_specs=[
            pl.BlockSpec((tm, M_WIDTH), lambda i: (i, 0)),
            pl.BlockSpec((tm, A_WIDTH), lambda i: (i, 0)),
            pl.BlockSpec((tm, D), lambda i: (i, COL_GM // D)),
            pl.BlockSpec((tm, D), lambda i: (i, COL_GA // D)),
            pl.BlockSpec((tm, D), lambda i: (i, 0)),
            pl.BlockSpec((M_WIDTH, D), lambda i: (0, 0), pipeline_mode=const),
            pl.BlockSpec((A_WIDTH, D), lambda i: (0, 0), pipeline_mode=const),
            pl.BlockSpec((D, D), lambda i: (0, 0), pipeline_mode=const),
            pl.BlockSpec((1, D), lambda i: (0, 0)),
        ],
        out_specs=pl.BlockSpec((tm, D), lambda i: (i, 0)),
        out_shape=jax.ShapeDtypeStruct((S, D), F32),
        compiler_params=_cparams(("parallel",)),
        name="merge",
    )(hm, ha, P, P, x, wm, wa, wo, gain)


def _ffn_kernel(x_ref, gpre_ref, wu_ref, wd_ref, gpost_ref, out_ref, hn_ref, acc_ref):
    f = pl.program_id(1)

    @pl.when(f == 0)
    def _():
        hn_ref[...] = _rms(x_ref[...], gpre_ref[...]).astype(BF16)
        acc_ref[...] = jnp.zeros_like(acc_ref)

    u = jnp.dot(hn_ref[...], wu_ref[...], preferred_element_type=F32)
    u = jnp.square(jnp.maximum(u, 0.0)).astype(BF16)
    acc_ref[...] += jnp.dot(u, wd_ref[...], preferred_element_type=F32)

    @pl.when(f == pl.num_programs(1) - 1)
    def _():
        out_ref[...] = x_ref[...] + _rms(acc_ref[...], gpost_ref[...])


def _ffn(x, gpre, wu, wd, gpost, tm=512, tf=1024):
    S, D = x.shape
    Fd = wu.shape[1]
    return pl.pallas_call(
        _ffn_kernel,
        grid=(S // tm, Fd // tf),
        in_specs=[
            pl.BlockSpec((tm, D), lambda i, f: (i, 0)),
            pl.BlockSpec((1, D), lambda i, f: (0, 0)),
            pl.BlockSpec((D, tf), lambda i, f: (0, f)),
            pl.BlockSpec((tf, D), lambda i, f: (f, 0)),
            pl.BlockSpec((1, D), lambda i, f: (0, 0)),
        ],
        out_specs=pl.BlockSpec((tm, D), lambda i, f: (i, 0)),
        out_shape=jax.ShapeDtypeStruct((S, D), F32),
        scratch_shapes=[pltpu.VMEM((tm, D), BF16), pltpu.VMEM((tm, D), F32)],
        compiler_params=_cparams(("parallel", "arbitrary")),
        name="ffn",
    )(x, gpre, wu, wd, gpost)


def _layer(x, pos, norm_mix_pre, w_in, conv_w, conv_b, i_bias, f_bias, mlstm_norm,
           w_branch_m, w_branch_a, w_out, norm_mix_post, norm_ffn_pre, w_up, w_down,
           norm_ffn_post):
    S, D = x.shape
    o = 0
    pieces = {}
    w_in = w_in.astype(BF16)
    for name, width in (("mq", M_WIDTH), ("mk", M_WIDTH), ("mv", M_WIDTH), ("mo", M_WIDTH),
                        ("mi", M_HEADS), ("mf", M_HEADS), ("aq", A_WIDTH), ("ak", A_WIDTH),
                        ("av", A_WIDTH), ("gm", D), ("ga", D)):
        pieces[name] = w_in[:, o:o + width]
        o += width
    w_all = jnp.concatenate([pieces[n] for n in ("gm", "ga", "mq", "mk", "aq", "ak", "mv", "mo", "av")],
                            axis=1)
    w_gate = jnp.concatenate(
        [pieces["mi"], pieces["mf"], jnp.zeros((D, LANES - 2 * M_HEADS), BF16)], axis=1)
    gbias = jnp.concatenate([i_bias, f_bias, jnp.zeros((LANES - 2 * M_HEADS,), F32)])[None, :]
    half = jnp.arange(0, A_HEAD_DIM, 2, dtype=F32) / A_HEAD_DIM
    inv_freq = 1.0 / (ROPE_THETA ** half)
    invf = jnp.concatenate([inv_freq, inv_freq])[None, :]

    P, gate = _proj(x, norm_mix_pre[None, :], w_all, w_gate)
    mq, mkT, aq, ak, avT, kmean, gc, gt = _prep(
        P, gate, pos.reshape(S, 1), invf, conv_w, conv_b[None, :], gbias)
    hm = _mlstm(mq, mkT, P, gc, gt, mlstm_norm[None, :])
    ha = _moba_routed(aq, ak, avT, kmean.reshape(S // MOBA_BLOCK, A_WIDTH))
    x1 = _merge(hm, ha, P, x, w_branch_m.astype(BF16), w_branch_a.astype(BF16),
                w_out.astype(BF16), norm_mix_post[None, :])
    return _ffn(x1, norm_ffn_pre[None, :], w_up.astype(BF16), w_down.astype(BF16),
                norm_ffn_post[None, :])


def kernel(x, positions, norm_mix_pre, w_in, conv_w, conv_b, i_bias, f_bias, mlstm_norm,
           w_branch_m, w_branch_a, w_out, norm_mix_post, norm_ffn_pre, w_up, w_down,
           norm_ffn_post):
    B = x.shape[0]
    depth = w_in.shape[0]
    outs = []
    for b in range(B):
        xb = x[b]
        for l in range(depth):
            xb = _layer(xb, positions[b], norm_mix_pre[l], w_in[l], conv_w[l], conv_b[l],
                        i_bias[l], f_bias[l], mlstm_norm[l], w_branch_m[l], w_branch_a[l],
                        w_out[l], norm_mix_post[l], norm_ffn_pre[l], w_up[l], w_down[l],
                        norm_ffn_post[l])
        outs.append(xb)
    return outs[0][None] if B == 1 else jnp.stack(outs, axis=0)
```

```python
import functools

import jax
import jax.numpy as jnp
from jax import lax
from jax.experimental import pallas as pl
from jax.experimental.pallas import tpu as pltpu
from jax.experimental.pallas import tpu_sc as plsc

F32 = jnp.float32
BF16 = jnp.bfloat16

M_HEADS = 4
M_HEAD_DIM = 256
M_WIDTH = M_HEADS * M_HEAD_DIM
M_CHUNK = 128
CONV_WIDTH = 4
A_HEADS = 8
A_HEAD_DIM = 128
A_WIDTH = A_HEADS * A_HEAD_DIM
MOBA_BLOCK = 256
MOBA_TOPK = 3
ROPE_THETA = 10000.0
NORM_EPS = 1e-6

LANES = 128
SUBLANES = 8
VMEM_LIMIT = 56 * 1024 * 1024
NEG_BIG = -1e30
LOG2_E = 1.4426950408889634

COL_GM = 0
COL_GA = 2048
COL_MQK = 4096
COL_AQK = 6144
COL_MV = 8192
COL_MO = 9216
COL_AV = 10240


def _cparams(sem):
    return pltpu.CompilerParams(dimension_semantics=sem, vmem_limit_bytes=VMEM_LIMIT)


def _rms(x, gain):
    ms = jnp.mean(x * x, axis=-1, keepdims=True)
    return x * lax.rsqrt(ms + NORM_EPS) * gain


def _sigmoid(x):
    return 1.0 / (1.0 + jnp.exp(-x))


def _split3(x):
    hi = x.astype(BF16)
    r1 = x - hi.astype(F32)
    mid = r1.astype(BF16)
    lo = (r1 - mid.astype(F32)).astype(BF16)
    return hi, mid, lo


REPACK_COLS = 1024
REPACK_ROWS = 512
GATE_COLS = 2 * M_HEADS
REPACK_SRC = (7, 8, 9, 10, 0, 1, 4, 5, 2, 3, 6)
REPACK_FIRST_SHIFTED = 4


def _repack_kernel(src_ref, a_ref, b_ref, o_ref):
    j = pl.program_id(1)

    @pl.when(src_ref[j] < REPACK_FIRST_SHIFTED)
    def _():
        o_ref[...] = a_ref[...].astype(BF16)

    @pl.when(src_ref[j] >= REPACK_FIRST_SHIFTED)
    def _():
        wide = jnp.concatenate([a_ref[...], b_ref[...]], axis=1)
        left = pltpu.roll(wide, wide.shape[1] - GATE_COLS, 1)
        o_ref[...] = left[:, :REPACK_COLS].astype(BF16)


def _repack(w_in):
    D = w_in.shape[0]
    nblk = len(REPACK_SRC)
    per = REPACK_COLS // LANES
    return pl.pallas_call(
        _repack_kernel,
        grid_spec=pltpu.PrefetchScalarGridSpec(
            num_scalar_prefetch=1,
            grid=(D // REPACK_ROWS, nblk),
            in_specs=[
                pl.BlockSpec((REPACK_ROWS, REPACK_COLS), lambda r, j, src: (r, src[j])),
                pl.BlockSpec((REPACK_ROWS, LANES), lambda r, j, src: (r, (src[j] + 1) * per)),
            ],
            out_specs=pl.BlockSpec((REPACK_ROWS, REPACK_COLS), lambda r, j, src: (r, j)),
        ),
        out_shape=jax.ShapeDtypeStruct((D, nblk * REPACK_COLS), BF16),
        compiler_params=_cparams(("parallel", "arbitrary")),
        name="repack",
    )(jnp.asarray(REPACK_SRC, jnp.int32), w_in, w_in)


def _proj_kernel(x_ref, g_ref, w_ref, wg_ref, p_ref, gate_ref, xn_ref):
    @pl.when(pl.program_id(1) == 0)
    def _():
        xn = _rms(x_ref[...], g_ref[...]).astype(BF16)
        xn_ref[...] = xn
        gate_ref[...] = jnp.dot(xn, wg_ref[...], preferred_element_type=F32)

    p_ref[...] = jnp.dot(xn_ref[...], w_ref[...], preferred_element_type=F32).astype(BF16)


def _proj(x, gain, w_all, w_gate, tm=1024, tn=1024):
    S, D = x.shape
    N = w_all.shape[1]
    return pl.pallas_call(
        _proj_kernel,
        grid=(S // tm, N // tn),
        in_specs=[
            pl.BlockSpec((tm, D), lambda i, j: (i, 0)),
            pl.BlockSpec((1, D), lambda i, j: (0, 0)),
            pl.BlockSpec((D, tn), lambda i, j: (0, j)),
            pl.BlockSpec((D, LANES), lambda i, j: (0, 0)),
        ],
        out_specs=[
            pl.BlockSpec((tm, tn), lambda i, j: (i, j)),
            pl.BlockSpec((tm, LANES), lambda i, j: (i, 0)),
        ],
        out_shape=[
            jax.ShapeDtypeStruct((S, N), BF16),
            jax.ShapeDtypeStruct((S, LANES), F32),
        ],
        scratch_shapes=[pltpu.VMEM((tm, D), BF16)],
        compiler_params=_cparams(("parallel", "arbitrary")),
        name="proj",
    )(x, gain, w_all, w_gate)


PREP_ROWS = MOBA_BLOCK
PREP_COLS = 512
HALO_ROWS = 2 * SUBLANES


def _prep_kernel(pmk_ref, halo_ref, paqk_ref, pav_ref, gate_ref, pos_ref, invf_ref,
                 cw_ref, cb_ref, gb_ref,
                 mq_ref, mkT_ref, aq_ref, ak_ref, avT_ref, kmean_ref, gc_ref, gt_ref):
    i = pl.program_id(0)
    R = PREP_ROWS

    k_scale = M_HEAD_DIM ** -0.5
    for c0 in range(0, 2 * M_WIDTH, PREP_COLS):
        cs = slice(c0, c0 + PREP_COLS)
        prev = halo_ref[:, cs].astype(F32)[HALO_ROWS - SUBLANES:, :]
        prev = jnp.where(i == 0, jnp.zeros_like(prev), prev)
        ext = jnp.concatenate([pmk_ref[:, cs].astype(F32), prev], axis=0)
        acc = cw_ref[0:1, cs] * ext
        for j in range(1, CONV_WIDTH):
            acc = pltpu.roll(acc, 1, 0) + cw_ref[j:j + 1, cs] * ext
        acc = acc[0:R, :] + cb_ref[:, cs]
        y = acc * _sigmoid(acc)
        if c0 < M_WIDTH:
            mq_ref[:, cs] = y.astype(BF16)
        else:
            ks = slice(c0 - M_WIDTH, c0 - M_WIDTH + PREP_COLS)
            mkT_ref[ks, :] = (y * k_scale).T.astype(BF16)

    ang = pos_ref[...].astype(F32) * invf_ref[...]
    cos = jnp.cos(ang)
    lane = lax.broadcasted_iota(jnp.int32, (R, A_HEAD_DIM), 1)
    sin_signed = jnp.where(lane < A_HEAD_DIM // 2, -1.0, 1.0) * jnp.sin(ang)
    for h in range(2 * A_HEADS):
        hs = slice(h * A_HEAD_DIM, (h + 1) * A_HEAD_DIM)
        xh = paqk_ref[:, hs].astype(F32)
        yh = xh * cos + pltpu.roll(xh, A_HEAD_DIM // 2, 1) * sin_signed
        if h < A_HEADS:
            aq_ref[h] = yh
        else:
            ko = slice((h - A_HEADS) * A_HEAD_DIM, (h - A_HEADS + 1) * A_HEAD_DIM)
            ak_ref[:, ko] = yh.astype(BF16)
            kmean_ref[0, :, ko] = jnp.mean(yh, axis=0, keepdims=True)
    for c0 in range(0, A_WIDTH, PREP_COLS):
        avT_ref[c0:c0 + PREP_COLS, :] = pav_ref[:, c0:c0 + PREP_COLS].astype(F32).T.astype(BF16)

    g = gate_ref[...] + gb_ref[...]
    log_f = jnp.minimum(g, 0.0) - jnp.log1p(jnp.exp(-jnp.abs(g)))
    r_i = lax.broadcasted_iota(jnp.int32, (R, R), 0)
    c_i = lax.broadcasted_iota(jnp.int32, (R, R), 1)
    tri = ((r_i >= c_i) & ((r_i // M_CHUNK) == (c_i // M_CHUNK))).astype(BF16)
    hi, mid, lo = _split3(log_f)
    csum = (jnp.dot(tri, hi, preferred_element_type=F32)
            + jnp.dot(tri, mid, preferred_element_type=F32)
            + jnp.dot(tri, lo, preferred_element_type=F32))
    glane = lax.broadcasted_iota(jnp.int32, (R, LANES), 1)
    gc = jnp.where(glane < M_HEADS, g, csum)
    gc_ref[...] = gc
    gt_ref[...] = gc.T[0:SUBLANES, :]


def _prep(P, gate, pos, invf, conv_w, conv_b, gbias):
    S = P.shape[0]
    R = PREP_ROWS
    nb = S // R
    halo_blocks = R // HALO_ROWS
    return pl.pallas_call(
        _prep_kernel,
        grid=(nb,),
        in_specs=[
            pl.BlockSpec((R, 2 * M_WIDTH), lambda i: (i, COL_MQK // (2 * M_WIDTH))),
            pl.BlockSpec((HALO_ROWS, 2 * M_WIDTH),
                         lambda i: (jnp.maximum(i * halo_blocks - 1, 0), COL_MQK // (2 * M_WIDTH))),
            pl.BlockSpec((R, 2 * A_WIDTH), lambda i: (i, COL_AQK // (2 * A_WIDTH))),
            pl.BlockSpec((R, A_WIDTH), lambda i: (i, COL_AV // A_WIDTH)),
            pl.BlockSpec((R, LANES), lambda i: (i, 0)),
            pl.BlockSpec((R, 1), lambda i: (i, 0)),
            pl.BlockSpec((1, A_HEAD_DIM), lambda i: (0, 0)),
            pl.BlockSpec((CONV_WIDTH, 2 * M_WIDTH), lambda i: (0, 0)),
            pl.BlockSpec((1, 2 * M_WIDTH), lambda i: (0, 0)),
            pl.BlockSpec((1, LANES), lambda i: (0, 0)),
        ],
        out_specs=[
            pl.BlockSpec((R, M_WIDTH), lambda i: (i, 0)),
            pl.BlockSpec((M_WIDTH, R), lambda i: (0, i)),
            pl.BlockSpec((A_HEADS, R, A_HEAD_DIM), lambda i: (0, i, 0)),
            pl.BlockSpec((R, A_WIDTH), lambda i: (i, 0)),
            pl.BlockSpec((A_WIDTH, R), lambda i: (0, i)),
            pl.BlockSpec((1, 1, A_WIDTH), lambda i: (i, 0, 0)),
            pl.BlockSpec((R, LANES), lambda i: (i, 0)),
            pl.BlockSpec((SUBLANES, R), lambda i: (0, i)),
        ],
        out_shape=[
            jax.ShapeDtypeStruct((S, M_WIDTH), BF16),
            jax.ShapeDtypeStruct((M_WIDTH, S), BF16),
            jax.ShapeDtypeStruct((A_HEADS, S, A_HEAD_DIM), F32),
            jax.ShapeDtypeStruct((S, A_WIDTH), BF16),
            jax.ShapeDtypeStruct((A_WIDTH, S), BF16),
            jax.ShapeDtypeStruct((nb, 1, A_WIDTH), F32),
            jax.ShapeDtypeStruct((S, LANES), F32),
            jax.ShapeDtypeStruct((SUBLANES, S), F32),
        ],
        compiler_params=_cparams(("parallel",)),
        name="prep",
    )(P, P, P, P, gate, pos, invf, conv_w, conv_b, gbias)


M_AUG = M_HEAD_DIM + LANES


def _mlstm_kernel(q_ref, kT_ref, v_ref, mo_ref, gc_ref, gt_ref, gain_ref, out_ref,
                  c_ref, m_ref):
    @pl.when(pl.program_id(0) == 0)
    def _():
        c_ref[...] = jnp.zeros_like(c_ref)
        m_ref[...] = jnp.zeros_like(m_ref)

    L = M_CHUNK
    D = M_HEAD_DIM
    row = lax.broadcasted_iota(jnp.int32, (L, L), 0)
    col = lax.broadcasted_iota(jnp.int32, (L, L), 1)
    causal = row >= col
    ones_col = (lax.broadcasted_iota(jnp.int32, (L, LANES), 1) == 0).astype(BF16)

    heads = range(M_HEADS)
    hsl = [slice(h * D, (h + 1) * D) for h in heads]
    q = [q_ref[:, hsl[h]] for h in heads]
    kT = [kT_ref[hsl[h], :] for h in heads]
    v_aug = [jnp.concatenate([v_ref[:, hsl[h]], ones_col], axis=1) for h in heads]
    b_c = [gc_ref[:, M_HEADS + h:M_HEADS + h + 1] for h in heads]
    b_r = [gt_ref[M_HEADS + h:M_HEADS + h + 1, :] for h in heads]
    u_r = [gt_ref[h:h + 1, :] - b_r[h] for h in heads]
    f_tot = [b_r[h][:, L - 1:L] for h in heads]
    m_prev = [m_ref[h:h + 1, 0:1] for h in heads]

    s_qk = [jnp.dot(q[h], kT[h], preferred_element_type=F32) for h in heads]
    q_c = [jnp.dot(q[h], c_ref[h].astype(BF16), preferred_element_type=F32) for h in heads]

    for h in heads:
        w_r = f_tot[h] + u_r[h]
        m_loc = jnp.max(w_r, axis=1, keepdims=True)
        m_new = jnp.maximum(f_tot[h] + m_prev[h], m_loc)
        a = jnp.exp(f_tot[h] + m_prev[h] - m_new)
        e_r = jnp.exp(w_r - m_new)
        keT = (kT[h].astype(F32) * e_r).astype(BF16)
        c_ref[h] = a * c_ref[h] + jnp.dot(keT, v_aug[h], preferred_element_type=F32)
        m_ref[h:h + 1, :] = jnp.broadcast_to(m_new, (1, LANES))

    for h in heads:
        d_log = jnp.where(causal, b_c[h] + u_r[h], -jnp.inf)
        a_log = b_c[h] + m_prev[h]
        m_t = jnp.maximum(a_log, jnp.max(d_log, axis=1, keepdims=True))
        s_ts = s_qk[h] * jnp.exp(d_log - m_t)
        inter = jnp.exp(a_log - m_t)
        r = inter * q_c[h] + jnp.dot(s_ts.astype(BF16), v_aug[h], preferred_element_type=F32)
        num = r[:, :D]
        den = r[:, D:D + 1]
        hh = num / jnp.maximum(jnp.abs(den), jnp.exp(-m_t))
        hn = hh * lax.rsqrt(jnp.mean(hh * hh, axis=-1, keepdims=True) + NORM_EPS)
        out_ref[:, hsl[h]] = (hn * gain_ref[:, hsl[h]]
                              * _sigmoid(mo_ref[:, hsl[h]].astype(F32))).astype(BF16)


def _mlstm(mq, mkT, P, gc, gt, gain):
    S = mq.shape[0]
    L = M_CHUNK
    return pl.pallas_call(
        _mlstm_kernel,
        grid=(S // L,),
        in_specs=[
            pl.BlockSpec((L, M_WIDTH), lambda c: (c, 0)),
            pl.BlockSpec((M_WIDTH, L), lambda c: (0, c)),
            pl.BlockSpec((L, M_WIDTH), lambda c: (c, COL_MV // M_WIDTH)),
            pl.BlockSpec((L, M_WIDTH), lambda c: (c, COL_MO // M_WIDTH)),
            pl.BlockSpec((L, LANES), lambda c: (c, 0)),
            pl.BlockSpec((SUBLANES, L), lambda c: (0, c)),
            pl.BlockSpec((1, M_WIDTH), lambda c: (0, 0)),
        ],
        out_specs=pl.BlockSpec((L, M_WIDTH), lambda c: (c, 0)),
        out_shape=jax.ShapeDtypeStruct((S, M_WIDTH), BF16),
        scratch_shapes=[
            pltpu.VMEM((M_HEADS, M_HEAD_DIM, M_AUG), F32),
            pltpu.VMEM((SUBLANES, LANES), F32),
        ],
        compiler_params=_cparams(("arbitrary",)),
        name="mlstm",
    )(mq, mkT, P, P, gc, gt, gain)


MOBA_HEADS_PER_GROUP = 4
ROUTE_TILES = 264
ROUTE_ROWS = ROUTE_TILES * MOBA_BLOCK
GROUP_TILES_PER_STEP = 8
ROW_WORDS = A_HEAD_DIM // 2
SC_WINDOW = 256


def _pack_rows(o_t, lse):
    q = o_t.shape[1]
    hi = pltpu.bitcast(o_t[:ROW_WORDS].astype(BF16).astype(F32), jnp.uint32)
    lo = pltpu.bitcast(o_t[ROW_WORDS:].astype(BF16).astype(F32), jnp.uint32)
    words = pltpu.bitcast(hi | (lo >> 16), F32)
    tail = jnp.concatenate([jnp.broadcast_to(lse, (SUBLANES, q)),
                            jnp.zeros((LANES - ROW_WORDS - SUBLANES, q), F32)], axis=0)
    return jnp.concatenate([words, tail], axis=0).T


def _dot_nt(a, b):
    return lax.dot_general(a, b, (((1,), (1,)), ((), ())), preferred_element_type=F32)


def _block_partial(k_blk, v_t_blk, qs, ones_rows, causal_mask=None):
    s = _dot_nt(k_blk, qs)
    if causal_mask is not None:
        s = jnp.where(causal_mask, s, NEG_BIG)
    sb = s.astype(BF16)
    m = jnp.max(sb, axis=0, keepdims=True)
    p = jnp.exp2(sb - m)
    r = jnp.dot(jnp.concatenate([v_t_blk, ones_rows], axis=0), p, preferred_element_type=F32)
    dh = v_t_blk.shape[0]
    l = r[dh:dh + 1, :]
    return r[:dh, :] / l, m.astype(F32) + jnp.log2(l)


def _route_kernel(q_ref, k_ref, vT_ref, km_ref, route_ref, cnt_ref, own_ref, cnt_acc):
    i = pl.program_id(1)
    BS = MOBA_BLOCK
    Dh = A_HEAD_DIM
    NB = km_ref.shape[0]
    G = q_ref.shape[0]
    blk = lax.broadcasted_iota(jnp.int32, (NB, BS), 0)
    kpos = lax.broadcasted_iota(jnp.int32, (BS, BS), 0)
    qpos = lax.broadcasted_iota(jnp.int32, (BS, BS), 1)
    earlier = (kpos < qpos).astype(BF16)
    ones_rows = jnp.ones((2 * SUBLANES, BS), BF16)

    @pl.when(i == 0)
    def _():
        cnt_acc[...] = jnp.zeros_like(cnt_acc)

    q = [q_ref[g] for g in range(G)]
    qs = [(q[g] * (Dh ** -0.5 * LOG2_E)).astype(BF16) for g in range(G)]
    parts = [_block_partial(k_ref[:, g * Dh:(g + 1) * Dh], vT_ref[g * Dh:(g + 1) * Dh, :], qs[g],
                            ones_rows, kpos <= qpos) for g in range(G)]

    gates = []
    for g in range(G):
        hs = slice(g * Dh, (g + 1) * Dh)
        km = km_ref[:, hs]
        kh = km.astype(BF16)
        kl = (km - kh.astype(F32)).astype(BF16)
        qh = q[g].astype(BF16)
        ql = (q[g] - qh.astype(F32)).astype(BF16)
        gate = _dot_nt(kh, qh) + _dot_nt(kh, ql) + _dot_nt(kl, qh)
        gates.append(jnp.where(blk < i, gate, -jnp.inf))
    picks = [[] for _ in range(G)]
    rows = [[] for _ in range(G)]
    for r in range(MOBA_TOPK):
        for g in range(G):
            mx = jnp.max(gates[g], axis=0, keepdims=True)
            idx = jnp.min(jnp.where(gates[g] == mx, blk, NB), axis=0, keepdims=True)
            idx = jnp.where(r < i, idx, -1)
            pick = blk == idx
            gates[g] = jnp.where(pick, -jnp.inf, gates[g])
            picks[g].append(pick)
            rows[g].append(idx)
    for g in range(G):
        onehot = sum(p.astype(F32) for p in picks[g])
        before = cnt_acc[g][:, 0:1] + jnp.dot(onehot.astype(BF16), earlier, preferred_element_type=F32)
        for r in range(MOBA_TOPK):
            rank = jnp.sum(jnp.where(picks[g][r], before, 0.0), axis=0, keepdims=True)
            rows[g].append(rank.astype(jnp.int32))
        rows[g].append(jnp.zeros((SUBLANES - 2 * MOBA_TOPK, BS), jnp.int32))
        route_ref[g] = jnp.concatenate(rows[g], axis=0)
        cnt_new = cnt_acc[g] + jnp.sum(onehot, axis=1, keepdims=True)
        cnt_acc[g] = cnt_new
        cnt_ref[g] = cnt_new
        own_ref[g] = _pack_rows(*parts[g])


def _route(aq_hm, ak, avT, kmean, h0, H):
    _, S, Dh = aq_hm.shape
    BS = MOBA_BLOCK
    NB = S // BS
    G = MOBA_HEADS_PER_GROUP
    W = G * Dh
    hb = h0 // G
    return pl.pallas_call(
        _route_kernel,
        grid=(H // G, NB),
        in_specs=[
            pl.BlockSpec((G, BS, Dh), lambda h, i: (hb + h, i, 0)),
            pl.BlockSpec((BS, W), lambda h, i: (i, hb + h)),
            pl.BlockSpec((W, BS), lambda h, i: (hb + h, i)),
            pl.BlockSpec((NB, W), lambda h, i: (0, hb + h)),
        ],
        out_specs=[
            pl.BlockSpec((G, SUBLANES, BS), lambda h, i: (h, 0, i)),
            pl.BlockSpec((G, NB, LANES), lambda h, i: (h, 0, 0)),
            pl.BlockSpec((G, BS, LANES), lambda h, i: (h, i, 0)),
        ],
        out_shape=[
            jax.ShapeDtypeStruct((H, SUBLANES, S), jnp.int32),
            jax.ShapeDtypeStruct((H, NB, LANES), F32),
            jax.ShapeDtypeStruct((H, S, LANES), F32),
        ],
        scratch_shapes=[pltpu.VMEM((G, NB, LANES), F32)],
        compiler_params=_cparams(("parallel", "arbitrary")),
        name="moba_route",
    )(aq_hm, ak, avT, kmean)


def _dest_kernel(seg_ref, route_ref, dest_ref):
    h = pl.program_id(0)
    NB = seg_ref.shape[1]
    r = route_ref[0]
    blk = r[0:MOBA_TOPK, :]
    rank = r[MOBA_TOPK:2 * MOBA_TOPK, :]
    base = h * ROUTE_ROWS
    dest = jnp.full(blk.shape, base + ROUTE_ROWS - 1, jnp.int32)
    for j in range(NB):
        dest = jnp.where(blk == j, base + seg_ref[h, j] + rank, dest)
    dest_ref[0] = jnp.concatenate(
        [dest, jnp.zeros((SUBLANES - MOBA_TOPK, dest.shape[1]), jnp.int32)], axis=0)


def _dest(seg_start, route):
    H, _, S = route.shape
    return pl.pallas_call(
        _dest_kernel,
        grid_spec=pltpu.PrefetchScalarGridSpec(
            num_scalar_prefetch=1,
            grid=(H,),
            in_specs=[pl.BlockSpec((1, SUBLANES, S), lambda h, seg: (h, 0, 0))],
            out_specs=pl.BlockSpec((1, SUBLANES, S), lambda h, seg: (h, 0, 0)),
        ),
        out_shape=jax.ShapeDtypeStruct((H, SUBLANES, S), jnp.int32),
        compiler_params=_cparams(("parallel",)),
        name="moba_dest",
    )(seg_start, route)


def _sc_mesh():
    return plsc.VectorSubcoreMesh(core_axis_name="c", subcore_axis_name="s")


def _sc_scatter(rows, row0, n_rows, idx, n_out):
    W = rows.shape[1]
    M = idx.shape[0]
    n_win = n_rows // SC_WINDOW
    win0 = row0 // SC_WINDOW

    @pl.kernel(out_type=jax.ShapeDtypeStruct((n_out, W), rows.dtype), mesh=_sc_mesh())
    def k(x_hbm, i_hbm, o_hbm):
        def body(x_vmem, i_vmem):
            pltpu.sync_copy(x_vmem, o_hbm.at[i_vmem.at[0]])

        pltpu.emit_pipeline(
            body,
            grid=(M // SC_WINDOW,),
            in_specs=[pl.BlockSpec((SC_WINDOW, W), lambda w: (win0 + lax.rem(w, n_win), 0)),
                      pl.BlockSpec((1, SC_WINDOW), lambda w: (0, w))],
            out_specs=[],
            core_axis_name=("c", "s"),
            dimension_semantics=(pltpu.PARALLEL,),
        )(x_hbm, i_hbm)

    return k(rows, idx.reshape(1, M))


def _sc_gather(table, idx):
    M = idx.shape[0]
    W = table.shape[1]

    @pl.kernel(out_type=jax.ShapeDtypeStruct((M, W), table.dtype), mesh=_sc_mesh())
    def k(x_hbm, i_hbm, o_hbm):
        def body(i_vmem, o_vmem):
            pltpu.sync_copy(x_hbm.at[i_vmem.at[0]], o_vmem)

        pltpu.emit_pipeline(
            body,
            grid=(M // SC_WINDOW,),
            in_specs=[pl.BlockSpec((1, SC_WINDOW), lambda w: (0, w))],
            out_specs=[pl.BlockSpec((SC_WINDOW, W), lambda w: (w, 0))],
            core_axis_name=("c", "s"),
            dimension_semantics=(pltpu.PARALLEL,),
        )(i_hbm, o_hbm)

    return k(table, idx.reshape(1, M))


def _group_kernel(tile_blk_ref, q_ref, k_ref, vT_ref, out_ref, s_ref, m_ref):
    h = pl.program_id(0)
    u = pl.program_id(1)
    last = pl.num_programs(1) - 2
    BS = MOBA_BLOCK
    Dh = A_HEAD_DIM
    T = GROUP_TILES_PER_STEP
    ones_rows = jnp.ones((2 * SUBLANES, BS), BF16)
    new = lax.rem(u, 2)
    old = 1 - new

    @pl.when((h == 0) & (u == 0))
    def _():
        s_ref[...] = jnp.zeros_like(s_ref)
        m_ref[...] = jnp.zeros_like(m_ref)

    def block_start(group, c):
        j = jnp.maximum(tile_blk_ref[h * ROUTE_TILES + group * T + c], 0)
        return pl.multiple_of(j * BS, BS)

    g_new = jnp.minimum(u, last)
    g_old = jnp.maximum(u - 1, 0)
    @pl.when(tile_blk_ref[h * ROUTE_TILES + g_old * T] >= 0)
    def _():
        ms, rs = [], []
        for c in range(T):
            m = m_ref[old, c]
            p = jnp.exp2(s_ref[old, c] - m.astype(BF16))
            v_aug = jnp.concatenate([vT_ref[:, pl.ds(block_start(g_old, c), BS)], ones_rows], axis=0)
            ms.append(m)
            rs.append(jnp.dot(v_aug, p, preferred_element_type=F32))
        s_new = []
        for c in range(T):
            qs = (q_ref[c * BS:(c + 1) * BS, :] * (Dh ** -0.5 * LOG2_E)).astype(BF16)
            s_new.append(_dot_nt(k_ref[pl.ds(block_start(g_new, c), BS), :], qs))
        for c in range(T):
            l = rs[c][Dh:Dh + 1, :]
            out_ref[c * BS:(c + 1) * BS, :] = _pack_rows(rs[c][:Dh, :] / l, ms[c] + jnp.log2(l))
            sb = s_new[c].astype(BF16)
            s_ref[new, c] = sb
            m_ref[new, c] = jnp.max(sb, axis=0, keepdims=True).astype(F32)


def _group(tile_blk, q_sorted, ak, avT, h0):
    S = ak.shape[0]
    H = q_sorted.shape[0] // ROUTE_ROWS
    Dh = A_HEAD_DIM
    BS = MOBA_BLOCK
    T = GROUP_TILES_PER_STEP
    steps = ROUTE_TILES // T
    rows = T * BS
    return pl.pallas_call(
        _group_kernel,
        grid_spec=pltpu.PrefetchScalarGridSpec(
            num_scalar_prefetch=1,
            grid=(H, steps + 1),
            in_specs=[
                pl.BlockSpec((rows, LANES), lambda h, u, tb: (h * steps + jnp.minimum(u, steps - 1), 0)),
                pl.BlockSpec((S, Dh), lambda h, u, tb: (0, h0 + h)),
                pl.BlockSpec((Dh, S), lambda h, u, tb: (h0 + h, 0)),
            ],
            out_specs=pl.BlockSpec((rows, LANES), lambda h, u, tb: (h * steps + jnp.maximum(u - 1, 0), 0)),
            scratch_shapes=[pltpu.VMEM((2, T, BS, BS), BF16), pltpu.VMEM((2, T, 1, BS), F32)],
        ),
        out_shape=jax.ShapeDtypeStruct((H * ROUTE_ROWS, LANES), F32),
        compiler_params=_cparams(("arbitrary", "arbitrary")),
        name="moba_group",
    )(tile_blk, q_sorted, ak, avT)


def _combine_kernel(*refs):
    i = pl.program_id(0)
    o_ref = refs[-1]
    n_groups = (len(refs) - 1) // 2
    H = refs[n_groups].shape[0]
    BS = refs[n_groups].shape[1]
    low_half = lax.broadcasted_iota(jnp.int32, (BS, LANES), 1) < ROW_WORDS
    for head in range(n_groups * H):
        got_ref, own_ref, h = refs[head // H], refs[n_groups + head // H], head % H
        tiles = [own_ref[h]]
        for r in range(MOBA_TOPK):
            tiles.append(jnp.where(r < i, got_ref[h, r], 0.0))
        lses = [tiles[0][:, ROW_WORDS:ROW_WORDS + 1]]
        lses += [jnp.where(r < i, tiles[r + 1][:, ROW_WORDS:ROW_WORDS + 1], NEG_BIG)
                 for r in range(MOBA_TOPK)]
        top = functools.reduce(jnp.maximum, lses)
        w = [jnp.exp2(x - top) for x in lses]
        inv = 1.0 / sum(w)
        first = jnp.zeros((BS, LANES), F32)
        second = jnp.zeros((BS, LANES), F32)
        for wk, tile in zip(w, tiles):
            words = pltpu.bitcast(tile, jnp.uint32)
            first = first + wk * pltpu.bitcast(words & jnp.uint32(0xFFFF0000), F32)
            second = second + wk * pltpu.bitcast(words << 16, F32)
        o = jnp.where(low_half, first, pltpu.roll(second, ROW_WORDS, 1)) * inv
        o_ref[:, head * A_HEAD_DIM:(head + 1) * A_HEAD_DIM] = o.astype(BF16)


def _combine(gots, owns):
    H, S, _ = owns[0].shape
    BS = MOBA_BLOCK
    return pl.pallas_call(
        _combine_kernel,
        grid=(S // BS,),
        in_specs=([pl.BlockSpec((H, MOBA_TOPK, BS, LANES), lambda i: (0, 0, i, 0))] * len(gots)
                  + [pl.BlockSpec((H, BS, LANES), lambda i: (0, i, 0))] * len(owns)),
        out_specs=pl.BlockSpec((BS, A_WIDTH), lambda i: (i, 0)),
        out_shape=jax.ShapeDtypeStruct((S, A_WIDTH), BF16),
        compiler_params=_cparams(("parallel",)),
        name="moba_combine",
    )(*gots, *owns)


def _moba_routed(aq_hm, ak, avT, kmean):
    H, S, Dh = aq_hm.shape
    NB = S // MOBA_BLOCK
    hn = MOBA_HEADS_PER_GROUP
    q_rows = aq_hm.reshape(H * S, Dh)
    gots, owns = [], []
    for h0 in range(0, H, hn):
        route, cnt, own = _route(aq_hm, ak, avT, kmean, h0, hn)
        cnt = cnt[:, :, 0].astype(jnp.int32)
        seg_tiles = (cnt + MOBA_BLOCK - 1) // MOBA_BLOCK
        seg_end = jnp.cumsum(seg_tiles, axis=1)
        seg_start = (seg_end - seg_tiles) * MOBA_BLOCK
        tile_ids = jnp.arange(ROUTE_TILES, dtype=jnp.int32)
        tile_blk = jnp.sum(tile_ids[None, :, None] >= seg_end[:, None, :], axis=2).astype(jnp.int32)
        tile_blk = jnp.where(tile_blk < NB, tile_blk, -1).reshape(hn * ROUTE_TILES)
        dest = _dest(seg_start, route)[:, :MOBA_TOPK, :]
        q_sorted = _sc_scatter(q_rows, h0 * S, hn * S, dest.transpose(1, 0, 2).reshape(-1),
                               hn * ROUTE_ROWS)
        results = _group(tile_blk, q_sorted, ak, avT, h0)
        gots.append(_sc_gather(results, dest.reshape(-1)).reshape(hn, MOBA_TOPK, S, LANES))
        owns.append(own)
    return _combine(gots, owns)


def _merge_kernel(hm_ref, ha_ref, gm_ref, ga_ref, x_ref, wm_ref, wa_ref, wo_ref, gain_ref,
                  out_ref):
    ym = jnp.dot(hm_ref[...], wm_ref[...], preferred_element_type=F32)
    ya = jnp.dot(ha_ref[...], wa_ref[...], preferred_element_type=F32)
    merged = _sigmoid(gm_ref[...].astype(F32)) * ym + _sigmoid(ga_ref[...].astype(F32)) * ya
    mix = jnp.dot(merged.astype(BF16), wo_ref[...], preferred_element_type=F32)
    out_ref[...] = x_ref[...] + _rms(mix, gain_ref[...])


def _merge(hm, ha, P, x, wm, wa, wo, gain, tm=256):
    S, D = x.shape
    const = pl.Buffered(1)
    return pl.pallas_call(
        _merge_kernel,
        grid=(S // tm,),
        in_specs=[
            pl.BlockSpec((tm, M_WIDTH), lambda i: (i, 0)),
            pl.BlockSpec((tm, A_WIDTH), lambda i: (i, 0)),
            pl.BlockSpec((tm, D), lambda i: (i, COL_GM // D)),
            pl.BlockSpec((tm, D), lambda i: (i, COL_GA // D)),
            pl.BlockSpec((tm, D), lambda i: (i, 0)),
            pl.BlockSpec((M_WIDTH, D), lambda i: (0, 0), pipeline_mode=const),
            pl.BlockSpec((A_WIDTH, D), lambda i: (0, 0), pipeline_mode=const),
            pl.BlockSpec((D, D), lambda i: (0, 0), pipeline_mode=const),
            pl.BlockSpec((1, D), lambda i: (0, 0)),
        ],
        out_specs=pl.BlockSpec((tm, D), lambda i: (i, 0)),
        out_shape=jax.ShapeDtypeStruct((S, D), F32),
        compiler_params=_cparams(("parallel",)),
        name="merge",
    )(hm, ha, P, P, x, wm, wa, wo, gain)


def _ffn_kernel(x_ref, gpre_ref, wu_ref, wd_ref, gpost_ref, out_ref, hn_ref, acc_ref):
    f = pl.program_id(1)

    @pl.when(f == 0)
    def _():
        hn_ref[...] = _rms(x_ref[...], gpre_ref[...]).astype(BF16)
        acc_ref[...] = jnp.zeros_like(acc_ref)

    u = jnp.dot(hn_ref[...], wu_ref[...], preferred_element_type=F32)
    u = jnp.square(jnp.maximum(u, 0.0)).astype(BF16)
    acc_ref[...] += jnp.dot(u, wd_ref[...], preferred_element_type=F32)

    @pl.when(f == pl.num_programs(1) - 1)
    def _():
        out_ref[...] = x_ref[...] + _rms(acc_ref[...], gpost_ref[...])


def _ffn(x, gpre, wu, wd, gpost, tm=512, tf=1024):
    S, D = x.shape
    Fd = wu.shape[1]
    return pl.pallas_call(
        _ffn_kernel,
        grid=(S // tm, Fd // tf),
        in_specs=[
            pl.BlockSpec((tm, D), lambda i, f: (i, 0)),
            pl.BlockSpec((1, D), lambda i, f: (0, 0)),
            pl.BlockSpec((D, tf), lambda i, f: (0, f)),
            pl.BlockSpec((tf, D), lambda i, f: (f, 0)),
            pl.BlockSpec((1, D), lambda i, f: (0, 0)),
        ],
        out_specs=pl.BlockSpec((tm, D), lambda i, f: (i, 0)),
        out_shape=jax.ShapeDtypeStruct((S, D), F32),
        scratch_shapes=[pltpu.VMEM((tm, D), BF16), pltpu.VMEM((tm, D), F32)],
        compiler_params=_cparams(("parallel", "arbitrary")),
        name="ffn",
    )(x, gpre, wu, wd, gpost)


def _layer(x, pos, norm_mix_pre, w_in, conv_w, conv_b, i_bias, f_bias, mlstm_norm,
           w_branch_m, w_branch_a, w_out, norm_mix_post, norm_ffn_pre, w_up, w_down,
           norm_ffn_post):
    S, D = x.shape
    w_all = _repack(w_in)
    gate0 = 4 * M_WIDTH
    w_gate = jnp.concatenate(
        [w_in[:, gate0:gate0 + GATE_COLS].astype(BF16), jnp.zeros((D, LANES - GATE_COLS), BF16)], axis=1)
    gbias = jnp.concatenate([i_bias, f_bias, jnp.zeros((LANES - 2 * M_HEADS,), F32)])[None, :]
    half = jnp.arange(0, A_HEAD_DIM, 2, dtype=F32) / A_HEAD_DIM
    inv_freq = 1.0 / (ROPE_THETA ** half)
    invf = jnp.concatenate([inv_freq, inv_freq])[None, :]

    P, gate = _proj(x, norm_mix_pre[None, :], w_all, w_gate)
    mq, mkT, aq, ak, avT, kmean, gc, gt = _prep(
        P, gate, pos.reshape(S, 1), invf, conv_w, conv_b[None, :], gbias)
    hm = _mlstm(mq, mkT, P, gc, gt, mlstm_norm[None, :])
    ha = _moba_routed(aq, ak, avT, kmean.reshape(S // MOBA_BLOCK, A_WIDTH))
    x1 = _merge(hm, ha, P, x, w_branch_m.astype(BF16), w_branch_a.astype(BF16),
                w_out.astype(BF16), norm_mix_post[None, :])
    return _ffn(x1, norm_ffn_pre[None, :], w_up.astype(BF16), w_down.astype(BF16),
                norm_ffn_post[None, :])


def kernel(x, positions, norm_mix_pre, w_in, conv_w, conv_b, i_bias, f_bias, mlstm_norm,
           w_branch_m, w_branch_a, w_out, norm_mix_post, norm_ffn_pre, w_up, w_down,
           norm_ffn_post):
    B = x.shape[0]
    depth = w_in.shape[0]
    outs = []
    for b in range(B):
        xb = x[b]
        for l in range(depth):
            xb = _layer(xb, positions[b], norm_mix_pre[l], w_in[l], conv_w[l], conv_b[l],
                        i_bias[l], f_bias[l], mlstm_norm[l], w_branch_m[l], w_branch_a[l],
                        w_out[l], norm_mix_post[l], norm_ffn_pre[l], w_up[l], w_down[l],
                        norm_ffn_post[l])
        outs.append(xb)
    return outs[0][None] if B == 1 else jnp.stack(outs, axis=0)
```

```python
import functools

import jax
import jax.numpy as jnp
from jax import lax
from jax.experimental import pallas as pl
from jax.experimental.pallas import tpu as pltpu
from jax.experimental.pallas import tpu_sc as plsc

F32 = jnp.float32
BF16 = jnp.bfloat16

M_HEADS = 4
M_HEAD_DIM = 256
M_WIDTH = M_HEADS * M_HEAD_DIM
M_CHUNK = 128
CONV_WIDTH = 4
A_HEADS = 8
A_HEAD_DIM = 128
A_WIDTH = A_HEADS * A_HEAD_DIM
MOBA_BLOCK = 256
MOBA_TOPK = 3
ROPE_THETA = 10000.0
NORM_EPS = 1e-6

LANES = 128
SUBLANES = 8
VMEM_LIMIT = 56 * 1024 * 1024
NEG_BIG = -1e30
LOG2_E = 1.4426950408889634

COL_GM = 0
COL_GA = 2048
COL_MQK = 4096
COL_AQK = 6144
COL_MV = 8192
COL_MO = 9216
COL_AV = 10240


def _cparams(sem):
    return pltpu.CompilerParams(dimension_semantics=sem, vmem_limit_bytes=VMEM_LIMIT)


def _rms(x, gain):
    ms = jnp.mean(x * x, axis=-1, keepdims=True)
    return x * lax.rsqrt(ms + NORM_EPS) * gain


def _sigmoid(x):
    return 1.0 / (1.0 + jnp.exp(-x))


def _split3(x):
    hi = x.astype(BF16)
    r1 = x - hi.astype(F32)
    mid = r1.astype(BF16)
    lo = (r1 - mid.astype(F32)).astype(BF16)
    return hi, mid, lo


REPACK_COLS = 1024
REPACK_ROWS = 512
GATE_COLS = 2 * M_HEADS
REPACK_SRC = (7, 8, 9, 10, 0, 1, 4, 5, 2, 3, 6)
REPACK_FIRST_SHIFTED = 4


def _repack_kernel(src_ref, a_ref, b_ref, o_ref):
    j = pl.program_id(1)

    @pl.when(src_ref[j] < REPACK_FIRST_SHIFTED)
    def _():
        o_ref[...] = a_ref[...].astype(BF16)

    @pl.when(src_ref[j] >= REPACK_FIRST_SHIFTED)
    def _():
        wide = jnp.concatenate([a_ref[...], b_ref[...]], axis=1)
        left = pltpu.roll(wide, wide.shape[1] - GATE_COLS, 1)
        o_ref[...] = left[:, :REPACK_COLS].astype(BF16)


def _repack(w_in, layer):
    D = w_in.shape[1]
    nblk = len(REPACK_SRC)
    per = REPACK_COLS // LANES
    return pl.pallas_call(
        _repack_kernel,
        grid_spec=pltpu.PrefetchScalarGridSpec(
            num_scalar_prefetch=1,
            grid=(D // REPACK_ROWS, nblk),
            in_specs=[
                pl.BlockSpec((None, REPACK_ROWS, REPACK_COLS), lambda r, j, src: (layer, r, src[j])),
                pl.BlockSpec((None, REPACK_ROWS, LANES), lambda r, j, src: (layer, r, (src[j] + 1) * per)),
            ],
            out_specs=pl.BlockSpec((REPACK_ROWS, REPACK_COLS), lambda r, j, src: (r, j)),
        ),
        out_shape=jax.ShapeDtypeStruct((D, nblk * REPACK_COLS), BF16),
        compiler_params=_cparams(("parallel", "arbitrary")),
        name="repack",
    )(jnp.asarray(REPACK_SRC, jnp.int32), w_in, w_in)


def _proj_kernel(x_ref, g_ref, w_ref, wg_ref, p_ref, gate_ref, xn_ref):
    @pl.when(pl.program_id(1) == 0)
    def _():
        xn = _rms(x_ref[...], g_ref[...]).astype(BF16)
        xn_ref[...] = xn
        gate_ref[...] = jnp.dot(xn, wg_ref[...], preferred_element_type=F32)

    p_ref[...] = jnp.dot(xn_ref[...], w_ref[...], preferred_element_type=F32).astype(BF16)


def _proj(x, gain, w_all, w_gate, tm=1024, tn=1024):
    S, D = x.shape
    N = w_all.shape[1]
    return pl.pallas_call(
        _proj_kernel,
        grid=(S // tm, N // tn),
        in_specs=[
            pl.BlockSpec((tm, D), lambda i, j: (i, 0)),
            pl.BlockSpec((1, D), lambda i, j: (0, 0)),
            pl.BlockSpec((D, tn), lambda i, j: (0, j)),
            pl.BlockSpec((D, LANES), lambda i, j: (0, 0)),
        ],
        out_specs=[
            pl.BlockSpec((tm, tn), lambda i, j: (i, j)),
            pl.BlockSpec((tm, LANES), lambda i, j: (i, 0)),
        ],
        out_shape=[
            jax.ShapeDtypeStruct((S, N), BF16),
            jax.ShapeDtypeStruct((S, LANES), F32),
        ],
        scratch_shapes=[pltpu.VMEM((tm, D), BF16)],
        compiler_params=_cparams(("parallel", "arbitrary")),
        name="proj",
    )(x, gain, w_all, w_gate)


PREP_ROWS = MOBA_BLOCK
PREP_COLS = 512
HALO_ROWS = 2 * SUBLANES


def _prep_kernel(pmk_ref, halo_ref, paqk_ref, pav_ref, gate_ref, pos_ref, invf_ref,
                 cw_ref, cb_ref, gb_ref,
                 mq_ref, mkT_ref, aq_ref, ak_ref, avT_ref, kmean_ref, gc_ref, gt_ref):
    i = pl.program_id(0)
    R = PREP_ROWS

    k_scale = M_HEAD_DIM ** -0.5
    for c0 in range(0, 2 * M_WIDTH, PREP_COLS):
        cs = slice(c0, c0 + PREP_COLS)
        prev = halo_ref[:, cs].astype(F32)[HALO_ROWS - SUBLANES:, :]
        prev = jnp.where(i == 0, jnp.zeros_like(prev), prev)
        ext = jnp.concatenate([pmk_ref[:, cs].astype(F32), prev], axis=0)
        acc = cw_ref[0:1, cs] * ext
        for j in range(1, CONV_WIDTH):
            acc = pltpu.roll(acc, 1, 0) + cw_ref[j:j + 1, cs] * ext
        acc = acc[0:R, :] + cb_ref[:, cs]
        y = acc * _sigmoid(acc)
        if c0 < M_WIDTH:
            mq_ref[:, cs] = y.astype(BF16)
        else:
            ks = slice(c0 - M_WIDTH, c0 - M_WIDTH + PREP_COLS)
            mkT_ref[ks, :] = (y * k_scale).T.astype(BF16)

    ang = pos_ref[...].astype(F32) * invf_ref[...]
    cos = jnp.cos(ang)
    lane = lax.broadcasted_iota(jnp.int32, (R, A_HEAD_DIM), 1)
    sin_signed = jnp.where(lane < A_HEAD_DIM // 2, -1.0, 1.0) * jnp.sin(ang)
    for h in range(2 * A_HEADS):
        hs = slice(h * A_HEAD_DIM, (h + 1) * A_HEAD_DIM)
        xh = paqk_ref[:, hs].astype(F32)
        yh = xh * cos + pltpu.roll(xh, A_HEAD_DIM // 2, 1) * sin_signed
        if h < A_HEADS:
            aq_ref[h] = yh
        else:
            ko = slice((h - A_HEADS) * A_HEAD_DIM, (h - A_HEADS + 1) * A_HEAD_DIM)
            ak_ref[:, ko] = yh.astype(BF16)
            kmean_ref[0, :, ko] = jnp.mean(yh, axis=0, keepdims=True)
    for c0 in range(0, A_WIDTH, PREP_COLS):
        avT_ref[c0:c0 + PREP_COLS, :] = pav_ref[:, c0:c0 + PREP_COLS].astype(F32).T.astype(BF16)

    g = gate_ref[...] + gb_ref[...]
    log_f = jnp.minimum(g, 0.0) - jnp.log1p(jnp.exp(-jnp.abs(g)))
    r_i = lax.broadcasted_iota(jnp.int32, (R, R), 0)
    c_i = lax.broadcasted_iota(jnp.int32, (R, R), 1)
    tri = ((r_i >= c_i) & ((r_i // M_CHUNK) == (c_i // M_CHUNK))).astype(BF16)
    hi, mid, lo = _split3(log_f)
    csum = (jnp.dot(tri, hi, preferred_element_type=F32)
            + jnp.dot(tri, mid, preferred_element_type=F32)
            + jnp.dot(tri, lo, preferred_element_type=F32))
    glane = lax.broadcasted_iota(jnp.int32, (R, LANES), 1)
    gc = jnp.where(glane < M_HEADS, g, csum)
    gc_ref[...] = gc
    gt_ref[...] = gc.T[0:SUBLANES, :]


def _prep(P, gate, pos, invf, conv_w, conv_b, gbias):
    S = P.shape[0]
    R = PREP_ROWS
    nb = S // R
    halo_blocks = R // HALO_ROWS
    return pl.pallas_call(
        _prep_kernel,
        grid=(nb,),
        in_specs=[
            pl.BlockSpec((R, 2 * M_WIDTH), lambda i: (i, COL_MQK // (2 * M_WIDTH))),
            pl.BlockSpec((HALO_ROWS, 2 * M_WIDTH),
                         lambda i: (jnp.maximum(i * halo_blocks - 1, 0), COL_MQK // (2 * M_WIDTH))),
            pl.BlockSpec((R, 2 * A_WIDTH), lambda i: (i, COL_AQK // (2 * A_WIDTH))),
            pl.BlockSpec((R, A_WIDTH), lambda i: (i, COL_AV // A_WIDTH)),
            pl.BlockSpec((R, LANES), lambda i: (i, 0)),
            pl.BlockSpec((R, 1), lambda i: (i, 0)),
            pl.BlockSpec((1, A_HEAD_DIM), lambda i: (0, 0)),
            pl.BlockSpec((CONV_WIDTH, 2 * M_WIDTH), lambda i: (0, 0)),
            pl.BlockSpec((1, 2 * M_WIDTH), lambda i: (0, 0)),
            pl.BlockSpec((1, LANES), lambda i: (0, 0)),
        ],
        out_specs=[
            pl.BlockSpec((R, M_WIDTH), lambda i: (i, 0)),
            pl.BlockSpec((M_WIDTH, R), lambda i: (0, i)),
            pl.BlockSpec((A_HEADS, R, A_HEAD_DIM), lambda i: (0, i, 0)),
            pl.BlockSpec((R, A_WIDTH), lambda i: (i, 0)),
            pl.BlockSpec((A_WIDTH, R), lambda i: (0, i)),
            pl.BlockSpec((1, 1, A_WIDTH), lambda i: (i, 0, 0)),
            pl.BlockSpec((R, LANES), lambda i: (i, 0)),
            pl.BlockSpec((SUBLANES, R), lambda i: (0, i)),
        ],
        out_shape=[
            jax.ShapeDtypeStruct((S, M_WIDTH), BF16),
            jax.ShapeDtypeStruct((M_WIDTH, S), BF16),
            jax.ShapeDtypeStruct((A_HEADS, S, A_HEAD_DIM), F32),
            jax.ShapeDtypeStruct((S, A_WIDTH), BF16),
            jax.ShapeDtypeStruct((A_WIDTH, S), BF16),
            jax.ShapeDtypeStruct((nb, 1, A_WIDTH), F32),
            jax.ShapeDtypeStruct((S, LANES), F32),
            jax.ShapeDtypeStruct((SUBLANES, S), F32),
        ],
        compiler_params=_cparams(("parallel",)),
        name="prep",
    )(P, P, P, P, gate, pos, invf, conv_w, conv_b, gbias)


M_AUG = M_HEAD_DIM + LANES


def _mlstm_kernel(q_ref, kT_ref, v_ref, mo_ref, gc_ref, gt_ref, gain_ref, out_ref,
                  c_ref, m_ref):
    @pl.when(pl.program_id(0) == 0)
    def _():
        c_ref[...] = jnp.zeros_like(c_ref)
        m_ref[...] = jnp.zeros_like(m_ref)

    L = M_CHUNK
    D = M_HEAD_DIM
    row = lax.broadcasted_iota(jnp.int32, (L, L), 0)
    col = lax.broadcasted_iota(jnp.int32, (L, L), 1)
    causal = row >= col
    ones_col = (lax.broadcasted_iota(jnp.int32, (L, LANES), 1) == 0).astype(BF16)

    heads = range(M_HEADS)
    hsl = [slice(h * D, (h + 1) * D) for h in heads]
    q = [q_ref[:, hsl[h]] for h in heads]
    kT = [kT_ref[hsl[h], :] for h in heads]
    v_aug = [jnp.concatenate([v_ref[:, hsl[h]], ones_col], axis=1) for h in heads]
    b_c = [gc_ref[:, M_HEADS + h:M_HEADS + h + 1] for h in heads]
    b_r = [gt_ref[M_HEADS + h:M_HEADS + h + 1, :] for h in heads]
    u_r = [gt_ref[h:h + 1, :] - b_r[h] for h in heads]
    f_tot = [b_r[h][:, L - 1:L] for h in heads]
    m_prev = [m_ref[h:h + 1, 0:1] for h in heads]

    s_qk = [jnp.dot(q[h], kT[h], preferred_element_type=F32) for h in heads]
    q_c = [jnp.dot(q[h], c_ref[h].astype(BF16), preferred_element_type=F32) for h in heads]

    for h in heads:
        w_r = f_tot[h] + u_r[h]
        m_loc = jnp.max(w_r, axis=1, keepdims=True)
        m_new = jnp.maximum(f_tot[h] + m_prev[h], m_loc)
        a = jnp.exp(f_tot[h] + m_prev[h] - m_new)
        e_r = jnp.exp(w_r - m_new)
        keT = (kT[h].astype(F32) * e_r).astype(BF16)
        c_ref[h] = a * c_ref[h] + jnp.dot(keT, v_aug[h], preferred_element_type=F32)
        m_ref[h:h + 1, :] = jnp.broadcast_to(m_new, (1, LANES))

    for h in heads:
        d_log = jnp.where(causal, b_c[h] + u_r[h], -jnp.inf)
        a_log = b_c[h] + m_prev[h]
        m_t = jnp.maximum(a_log, jnp.max(d_log, axis=1, keepdims=True))
        s_ts = s_qk[h] * jnp.exp(d_log - m_t)
        inter = jnp.exp(a_log - m_t)
        r = inter * q_c[h] + jnp.dot(s_ts.astype(BF16), v_aug[h], preferred_element_type=F32)
        num = r[:, :D]
        den = r[:, D:D + 1]
        hh = num / jnp.maximum(jnp.abs(den), jnp.exp(-m_t))
        hn = hh * lax.rsqrt(jnp.mean(hh * hh, axis=-1, keepdims=True) + NORM_EPS)
        out_ref[:, hsl[h]] = (hn * gain_ref[:, hsl[h]]
                              * _sigmoid(mo_ref[:, hsl[h]].astype(F32))).astype(BF16)


def _mlstm(mq, mkT, P, gc, gt, gain):
    S = mq.shape[0]
    L = M_CHUNK
    return pl.pallas_call(
        _mlstm_kernel,
        grid=(S // L,),
        in_specs=[
            pl.BlockSpec((L, M_WIDTH), lambda c: (c, 0)),
            pl.BlockSpec((M_WIDTH, L), lambda c: (0, c)),
            pl.BlockSpec((L, M_WIDTH), lambda c: (c, COL_MV // M_WIDTH)),
            pl.BlockSpec((L, M_WIDTH), lambda c: (c, COL_MO // M_WIDTH)),
            pl.BlockSpec((L, LANES), lambda c: (c, 0)),
            pl.BlockSpec((SUBLANES, L), lambda c: (0, c)),
            pl.BlockSpec((1, M_WIDTH), lambda c: (0, 0)),
        ],
        out_specs=pl.BlockSpec((L, M_WIDTH), lambda c: (c, 0)),
        out_shape=jax.ShapeDtypeStruct((S, M_WIDTH), BF16),
        scratch_shapes=[
            pltpu.VMEM((M_HEADS, M_HEAD_DIM, M_AUG), F32),
            pltpu.VMEM((SUBLANES, LANES), F32),
        ],
        compiler_params=_cparams(("arbitrary",)),
        name="mlstm",
    )(mq, mkT, P, P, gc, gt, gain)


MOBA_HEADS_PER_GROUP = 4
ROUTE_TILES = 264
ROUTE_ROWS = ROUTE_TILES * MOBA_BLOCK
GROUP_TILES_PER_STEP = 8
ROW_WORDS = A_HEAD_DIM // 2
SC_WINDOW = 256


def _pack_rows(o_t, lse):
    q = o_t.shape[1]
    hi = pltpu.bitcast(o_t[:ROW_WORDS].astype(BF16).astype(F32), jnp.uint32)
    lo = pltpu.bitcast(o_t[ROW_WORDS:].astype(BF16).astype(F32), jnp.uint32)
    words = pltpu.bitcast(hi | (lo >> 16), F32)
    tail = jnp.concatenate([jnp.broadcast_to(lse, (SUBLANES, q)),
                            jnp.zeros((LANES - ROW_WORDS - SUBLANES, q), F32)], axis=0)
    return jnp.concatenate([words, tail], axis=0).T


def _dot_nt(a, b):
    return lax.dot_general(a, b, (((1,), (1,)), ((), ())), preferred_element_type=F32)


def _block_partial(k_blk, v_t_blk, qs, ones_rows, causal_mask=None):
    s = _dot_nt(k_blk, qs)
    if causal_mask is not None:
        s = jnp.where(causal_mask, s, NEG_BIG)
    sb = s.astype(BF16)
    m = jnp.max(sb, axis=0, keepdims=True)
    p = jnp.exp2(sb - m)
    r = jnp.dot(jnp.concatenate([v_t_blk, ones_rows], axis=0), p, preferred_element_type=F32)
    dh = v_t_blk.shape[0]
    l = r[dh:dh + 1, :]
    return r[:dh, :] / l, m.astype(F32) + jnp.log2(l)


def _route_kernel(q_ref, k_ref, vT_ref, km_ref, route_ref, cnt_ref, own_ref, cnt_acc):
    i = pl.program_id(1)
    BS = MOBA_BLOCK
    Dh = A_HEAD_DIM
    NB = km_ref.shape[0]
    G = q_ref.shape[0]
    blk = lax.broadcasted_iota(jnp.int32, (NB, BS), 0)
    kpos = lax.broadcasted_iota(jnp.int32, (BS, BS), 0)
    qpos = lax.broadcasted_iota(jnp.int32, (BS, BS), 1)
    earlier = (kpos < qpos).astype(BF16)
    ones_rows = jnp.ones((2 * SUBLANES, BS), BF16)

    @pl.when(i == 0)
    def _():
        cnt_acc[...] = jnp.zeros_like(cnt_acc)

    q = [q_ref[g] for g in range(G)]
    qs = [(q[g] * (Dh ** -0.5 * LOG2_E)).astype(BF16) for g in range(G)]
    parts = [_block_partial(k_ref[:, g * Dh:(g + 1) * Dh], vT_ref[g * Dh:(g + 1) * Dh, :], qs[g],
                            ones_rows, kpos <= qpos) for g in range(G)]

    gates = []
    for g in range(G):
        hs = slice(g * Dh, (g + 1) * Dh)
        km = km_ref[:, hs]
        kh = km.astype(BF16)
        kl = (km - kh.astype(F32)).astype(BF16)
        qh = q[g].astype(BF16)
        ql = (q[g] - qh.astype(F32)).astype(BF16)
        gate = _dot_nt(kh, qh) + _dot_nt(kh, ql) + _dot_nt(kl, qh)
        gates.append(jnp.where(blk < i, gate, -jnp.inf))
    picks = [[] for _ in range(G)]
    rows = [[] for _ in range(G)]
    for r in range(MOBA_TOPK):
        for g in range(G):
            mx = jnp.max(gates[g], axis=0, keepdims=True)
            idx = jnp.min(jnp.where(gates[g] == mx, blk, NB), axis=0, keepdims=True)
            idx = jnp.where(r < i, idx, -1)
            pick = blk == idx
            gates[g] = jnp.where(pick, -jnp.inf, gates[g])
            picks[g].append(pick)
            rows[g].append(idx)
    for g in range(G):
        onehot = sum(p.astype(F32) for p in picks[g])
        before = cnt_acc[g][:, 0:1] + jnp.dot(onehot.astype(BF16), earlier, preferred_element_type=F32)
        for r in range(MOBA_TOPK):
            rank = jnp.sum(jnp.where(picks[g][r], before, 0.0), axis=0, keepdims=True)
            rows[g].append(rank.astype(jnp.int32))
        rows[g].append(jnp.zeros((SUBLANES - 2 * MOBA_TOPK, BS), jnp.int32))
        route_ref[g] = jnp.concatenate(rows[g], axis=0)
        cnt_new = cnt_acc[g] + jnp.sum(onehot, axis=1, keepdims=True)
        cnt_acc[g] = cnt_new
        cnt_ref[g] = cnt_new
        own_ref[g] = _pack_rows(*parts[g])


def _route(aq_hm, ak, avT, kmean, h0, H):
    _, S, Dh = aq_hm.shape
    BS = MOBA_BLOCK
    NB = S // BS
    G = MOBA_HEADS_PER_GROUP
    W = G * Dh
    hb = h0 // G
    return pl.pallas_call(
        _route_kernel,
        grid=(H // G, NB),
        in_specs=[
            pl.BlockSpec((G, BS, Dh), lambda h, i: (hb + h, i, 0)),
            pl.BlockSpec((BS, W), lambda h, i: (i, hb + h)),
            pl.BlockSpec((W, BS), lambda h, i: (hb + h, i)),
            pl.BlockSpec((NB, W), lambda h, i: (0, hb + h)),
        ],
        out_specs=[
            pl.BlockSpec((G, SUBLANES, BS), lambda h, i: (h, 0, i)),
            pl.BlockSpec((G, NB, LANES), lambda h, i: (h, 0, 0)),
            pl.BlockSpec((G, BS, LANES), lambda h, i: (h, i, 0)),
        ],
        out_shape=[
            jax.ShapeDtypeStruct((H, SUBLANES, S), jnp.int32),
            jax.ShapeDtypeStruct((H, NB, LANES), F32),
            jax.ShapeDtypeStruct((H, S, LANES), F32),
        ],
        scratch_shapes=[pltpu.VMEM((G, NB, LANES), F32)],
        compiler_params=_cparams(("parallel", "arbitrary")),
        name="moba_route",
    )(aq_hm, ak, avT, kmean)


def _dest_kernel(seg_ref, route_ref, dest_ref):
    h = pl.program_id(0)
    NB = seg_ref.shape[1]
    r = route_ref[0]
    blk = r[0:MOBA_TOPK, :]
    rank = r[MOBA_TOPK:2 * MOBA_TOPK, :]
    base = h * ROUTE_ROWS
    dest = jnp.full(blk.shape, base + ROUTE_ROWS - 1, jnp.int32)
    for j in range(NB):
        dest = jnp.where(blk == j, base + seg_ref[h, j] + rank, dest)
    dest_ref[0] = jnp.concatenate(
        [dest, jnp.zeros((SUBLANES - MOBA_TOPK, dest.shape[1]), jnp.int32)], axis=0)


def _dest(seg_start, route):
    H, _, S = route.shape
    return pl.pallas_call(
        _dest_kernel,
        grid_spec=pltpu.PrefetchScalarGridSpec(
            num_scalar_prefetch=1,
            grid=(H,),
            in_specs=[pl.BlockSpec((1, SUBLANES, S), lambda h, seg: (h, 0, 0))],
            out_specs=pl.BlockSpec((1, SUBLANES, S), lambda h, seg: (h, 0, 0)),
        ),
        out_shape=jax.ShapeDtypeStruct((H, SUBLANES, S), jnp.int32),
        compiler_params=_cparams(("parallel",)),
        name="moba_dest",
    )(seg_start, route)


def _sc_mesh():
    return plsc.VectorSubcoreMesh(core_axis_name="c", subcore_axis_name="s")


def _sc_scatter(rows, row0, n_rows, idx, n_out):
    W = rows.shape[1]
    M = idx.shape[0]
    n_win = n_rows // SC_WINDOW
    win0 = row0 // SC_WINDOW

    @pl.kernel(out_type=jax.ShapeDtypeStruct((n_out, W), rows.dtype), mesh=_sc_mesh())
    def k(x_hbm, i_hbm, o_hbm):
        def body(x_vmem, i_vmem):
            pltpu.sync_copy(x_vmem, o_hbm.at[i_vmem.at[0]])

        pltpu.emit_pipeline(
            body,
            grid=(M // SC_WINDOW,),
            in_specs=[pl.BlockSpec((SC_WINDOW, W), lambda w: (win0 + lax.rem(w, n_win), 0)),
                      pl.BlockSpec((1, SC_WINDOW), lambda w: (0, w))],
            out_specs=[],
            core_axis_name=("c", "s"),
            dimension_semantics=(pltpu.PARALLEL,),
        )(x_hbm, i_hbm)

    return k(rows, idx.reshape(1, M))


def _sc_gather(table, idx):
    M = idx.shape[0]
    W = table.shape[1]

    @pl.kernel(out_type=jax.ShapeDtypeStruct((M, W), table.dtype), mesh=_sc_mesh())
    def k(x_hbm, i_hbm, o_hbm):
        def body(i_vmem, o_vmem):
            pltpu.sync_copy(x_hbm.at[i_vmem.at[0]], o_vmem)

        pltpu.emit_pipeline(
            body,
            grid=(M // SC_WINDOW,),
            in_specs=[pl.BlockSpec((1, SC_WINDOW), lambda w: (0, w))],
            out_specs=[pl.BlockSpec((SC_WINDOW, W), lambda w: (w, 0))],
            core_axis_name=("c", "s"),
            dimension_semantics=(pltpu.PARALLEL,),
        )(i_hbm, o_hbm)

    return k(table, idx.reshape(1, M))


def _group_kernel(tile_blk_ref, q_ref, k_ref, vT_ref, out_ref, s_ref, m_ref):
    h = pl.program_id(0)
    u = pl.program_id(1)
    last = pl.num_programs(1) - 2
    BS = MOBA_BLOCK
    Dh = A_HEAD_DIM
    T = GROUP_TILES_PER_STEP
    ones_rows = jnp.ones((2 * SUBLANES, BS), BF16)
    new = lax.rem(u, 2)
    old = 1 - new

    @pl.when((h == 0) & (u == 0))
    def _():
        s_ref[...] = jnp.zeros_like(s_ref)
        m_ref[...] = jnp.zeros_like(m_ref)

    def block_start(group, c):
        j = jnp.maximum(tile_blk_ref[h * ROUTE_TILES + group * T + c], 0)
        return pl.multiple_of(j * BS, BS)

    g_new = jnp.minimum(u, last)
    g_old = jnp.maximum(u - 1, 0)
    @pl.when(tile_blk_ref[h * ROUTE_TILES + g_old * T] >= 0)
    def _():
        ms, rs = [], []
        for c in range(T):
            m = m_ref[old, c]
            p = jnp.exp2(s_ref[old, c] - m.astype(BF16))
            v_aug = jnp.concatenate([vT_ref[:, pl.ds(block_start(g_old, c), BS)], ones_rows], axis=0)
            ms.append(m)
            rs.append(jnp.dot(v_aug, p, preferred_element_type=F32))
        s_new = []
        for c in range(T):
            qs = (q_ref[c * BS:(c + 1) * BS, :] * (Dh ** -0.5 * LOG2_E)).astype(BF16)
            s_new.append(_dot_nt(k_ref[pl.ds(block_start(g_new, c), BS), :], qs))
        for c in range(T):
            l = rs[c][Dh:Dh + 1, :]
            out_ref[c * BS:(c + 1) * BS, :] = _pack_rows(rs[c][:Dh, :] / l, ms[c] + jnp.log2(l))
            sb = s_new[c].astype(BF16)
            s_ref[new, c] = sb
            m_ref[new, c] = jnp.max(sb, axis=0, keepdims=True).astype(F32)


def _group(tile_blk, q_sorted, ak, avT, h0):
    S = ak.shape[0]
    H = q_sorted.shape[0] // ROUTE_ROWS
    Dh = A_HEAD_DIM
    BS = MOBA_BLOCK
    T = GROUP_TILES_PER_STEP
    steps = ROUTE_TILES // T
    rows = T * BS
    return pl.pallas_call(
        _group_kernel,
        grid_spec=pltpu.PrefetchScalarGridSpec(
            num_scalar_prefetch=1,
            grid=(H, steps + 1),
            in_specs=[
                pl.BlockSpec((rows, LANES), lambda h, u, tb: (h * steps + jnp.minimum(u, steps - 1), 0)),
                pl.BlockSpec((S, Dh), lambda h, u, tb: (0, h0 + h)),
                pl.BlockSpec((Dh, S), lambda h, u, tb: (h0 + h, 0)),
            ],
            out_specs=pl.BlockSpec((rows, LANES), lambda h, u, tb: (h * steps + jnp.maximum(u - 1, 0), 0)),
            scratch_shapes=[pltpu.VMEM((2, T, BS, BS), BF16), pltpu.VMEM((2, T, 1, BS), F32)],
        ),
        out_shape=jax.ShapeDtypeStruct((H * ROUTE_ROWS, LANES), F32),
        compiler_params=_cparams(("arbitrary", "arbitrary")),
        name="moba_group",
    )(tile_blk, q_sorted, ak, avT)


def _combine_kernel(*refs):
    i = pl.program_id(0)
    o_ref = refs[-1]
    n_groups = (len(refs) - 1) // 2
    H = refs[n_groups].shape[0]
    BS = refs[n_groups].shape[1]
    low_half = lax.broadcasted_iota(jnp.int32, (BS, LANES), 1) < ROW_WORDS
    for head in range(n_groups * H):
        got_ref, own_ref, h = refs[head // H], refs[n_groups + head // H], head % H
        tiles = [own_ref[h]]
        for r in range(MOBA_TOPK):
            tiles.append(jnp.where(r < i, got_ref[h, r], 0.0))
        lses = [tiles[0][:, ROW_WORDS:ROW_WORDS + 1]]
        lses += [jnp.where(r < i, tiles[r + 1][:, ROW_WORDS:ROW_WORDS + 1], NEG_BIG)
                 for r in range(MOBA_TOPK)]
        top = functools.reduce(jnp.maximum, lses)
        w = [jnp.exp2(x - top) for x in lses]
        inv = 1.0 / sum(w)
        first = jnp.zeros((BS, LANES), F32)
        second = jnp.zeros((BS, LANES), F32)
        for wk, tile in zip(w, tiles):
            words = pltpu.bitcast(tile, jnp.uint32)
            first = first + wk * pltpu.bitcast(words & jnp.uint32(0xFFFF0000), F32)
            second = second + wk * pltpu.bitcast(words << 16, F32)
        o = jnp.where(low_half, first, pltpu.roll(second, ROW_WORDS, 1)) * inv
        o_ref[:, head * A_HEAD_DIM:(head + 1) * A_HEAD_DIM] = o.astype(BF16)


def _combine(gots, owns):
    H, S, _ = owns[0].shape
    BS = MOBA_BLOCK
    return pl.pallas_call(
        _combine_kernel,
        grid=(S // BS,),
        in_specs=([pl.BlockSpec((H, MOBA_TOPK, BS, LANES), lambda i: (0, 0, i, 0))] * len(gots)
                  + [pl.BlockSpec((H, BS, LANES), lambda i: (0, i, 0))] * len(owns)),
        out_specs=pl.BlockSpec((BS, A_WIDTH), lambda i: (i, 0)),
        out_shape=jax.ShapeDtypeStruct((S, A_WIDTH), BF16),
        compiler_params=_cparams(("parallel",)),
        name="moba_combine",
    )(*gots, *owns)


def _moba_routed(aq_hm, ak, avT, kmean):
    H, S, Dh = aq_hm.shape
    NB = S // MOBA_BLOCK
    hn = MOBA_HEADS_PER_GROUP
    q_rows = aq_hm.reshape(H * S, Dh)
    gots, owns = [], []
    for h0 in range(0, H, hn):
        route, cnt, own = _route(aq_hm, ak, avT, kmean, h0, hn)
        cnt = cnt[:, :, 0].astype(jnp.int32)
        seg_tiles = (cnt + MOBA_BLOCK - 1) // MOBA_BLOCK
        seg_end = jnp.cumsum(seg_tiles, axis=1)
        seg_start = (seg_end - seg_tiles) * MOBA_BLOCK
        tile_ids = jnp.arange(ROUTE_TILES, dtype=jnp.int32)
        tile_blk = jnp.sum(tile_ids[None, :, None] >= seg_end[:, None, :], axis=2).astype(jnp.int32)
        tile_blk = jnp.where(tile_blk < NB, tile_blk, -1).reshape(hn * ROUTE_TILES)
        dest = _dest(seg_start, route)[:, :MOBA_TOPK, :]
        q_sorted = _sc_scatter(q_rows, h0 * S, hn * S, dest.transpose(1, 0, 2).reshape(-1),
                               hn * ROUTE_ROWS)
        results = _group(tile_blk, q_sorted, ak, avT, h0)
        gots.append(_sc_gather(results, dest.reshape(-1)).reshape(hn, MOBA_TOPK, S, LANES))
        owns.append(own)
    return _combine(gots, owns)


def _merge_kernel(hm_ref, ha_ref, gm_ref, ga_ref, x_ref, wm_ref, wa_ref, wo_ref, gain_ref,
                  out_ref):
    ym = jnp.dot(hm_ref[...], wm_ref[...], preferred_element_type=F32)
    ya = jnp.dot(ha_ref[...], wa_ref[...], preferred_element_type=F32)
    merged = _sigmoid(gm_ref[...].astype(F32)) * ym + _sigmoid(ga_ref[...].astype(F32)) * ya
    mix = jnp.dot(merged.astype(BF16), wo_ref[...], preferred_element_type=F32)
    out_ref[...] = x_ref[...] + _rms(mix, gain_ref[...])


def _merge(hm, ha, P, x, wm, wa, wo, gain, tm=256):
    S, D = x.shape
    const = pl.Buffered(1)
    return pl.pallas_call(
        _merge_kernel,
        grid=(S // tm,),
        in_specs=[
            pl.BlockSpec((tm, M_WIDTH), lambda i: (i, 0)),
            pl.BlockSpec((tm, A_WIDTH), lambda i: (i, 0)),
            pl.BlockSpec((tm, D), lambda i: (i, COL_GM // D)),
            pl.BlockSpec((tm, D), lambda i: (i, COL_GA // D)),
            pl.BlockSpec((tm, D), lambda i: (i, 0)),
            pl.BlockSpec((M_WIDTH, D), lambda i: (0, 0), pipeline_mode=const),
            pl.BlockSpec((A_WIDTH, D), lambda i: (0, 0), pipeline_mode=const),
            pl.BlockSpec((D, D), lambda i: (0, 0), pipeline_mode=const),
            pl.BlockSpec((1, D), lambda i: (0, 0)),
        ],
        out_specs=pl.BlockSpec((tm, D), lambda i: (i, 0)),
        out_shape=jax.ShapeDtypeStruct((S, D), F32),
        compiler_params=_cparams(("parallel",)),
        name="merge",
    )(hm, ha, P, P, x, wm, wa, wo, gain)


def _ffn_kernel(x_ref, gpre_ref, wu_ref, wd_ref, gpost_ref, out_ref, hn_ref, acc_ref):
    f = pl.program_id(1)

    @pl.when(f == 0)
    def _():
        hn_ref[...] = _rms(x_ref[...], gpre_ref[...]).astype(BF16)
        acc_ref[...] = jnp.zeros_like(acc_ref)

    u = jnp.dot(hn_ref[...], wu_ref[...], preferred_element_type=F32)
    u = jnp.square(jnp.maximum(u, 0.0)).astype(BF16)
    acc_ref[...] += jnp.dot(u, wd_ref[...], preferred_element_type=F32)

    @pl.when(f == pl.num_programs(1) - 1)
    def _():
        out_ref[...] = x_ref[...] + _rms(acc_ref[...], gpost_ref[...])


def _ffn(x, gpre, wu, wd, gpost, tm=512, tf=1024):
    S, D = x.shape
    Fd = wu.shape[1]
    return pl.pallas_call(
        _ffn_kernel,
        grid=(S // tm, Fd // tf),
        in_specs=[
            pl.BlockSpec((tm, D), lambda i, f: (i, 0)),
            pl.BlockSpec((1, D), lambda i, f: (0, 0)),
            pl.BlockSpec((D, tf), lambda i, f: (0, f)),
            pl.BlockSpec((tf, D), lambda i, f: (f, 0)),
            pl.BlockSpec((1, D), lambda i, f: (0, 0)),
        ],
        out_specs=pl.BlockSpec((tm, D), lambda i, f: (i, 0)),
        out_shape=jax.ShapeDtypeStruct((S, D), F32),
        scratch_shapes=[pltpu.VMEM((tm, D), BF16), pltpu.VMEM((tm, D), F32)],
        compiler_params=_cparams(("parallel", "arbitrary")),
        name="ffn",
    )(x, gpre, wu, wd, gpost)


def _layer(layer, x, pos, norm_mix_pre, w_in, conv_w, conv_b, i_bias, f_bias, mlstm_norm,
           w_branch_m, w_branch_a, w_out, norm_mix_post, norm_ffn_pre, w_up, w_down,
           norm_ffn_post):
    S, D = x.shape
    w_all = _repack(w_in, layer)
    gate0 = 4 * M_WIDTH
    w_gate = jnp.concatenate(
        [w_in[layer, :, gate0:gate0 + GATE_COLS].astype(BF16), jnp.zeros((D, LANES - GATE_COLS), BF16)],
        axis=1)
    gbias = jnp.concatenate([i_bias, f_bias, jnp.zeros((LANES - 2 * M_HEADS,), F32)])[None, :]
    half = jnp.arange(0, A_HEAD_DIM, 2, dtype=F32) / A_HEAD_DIM
    inv_freq = 1.0 / (ROPE_THETA ** half)
    invf = jnp.concatenate([inv_freq, inv_freq])[None, :]

    P, gate = _proj(x, norm_mix_pre[None, :], w_all, w_gate)
    mq, mkT, aq, ak, avT, kmean, gc, gt = _prep(
        P, gate, pos.reshape(S, 1), invf, conv_w, conv_b[None, :], gbias)
    hm = _mlstm(mq, mkT, P, gc, gt, mlstm_norm[None, :])
    ha = _moba_routed(aq, ak, avT, kmean.reshape(S // MOBA_BLOCK, A_WIDTH))
    x1 = _merge(hm, ha, P, x, w_branch_m.astype(BF16), w_branch_a.astype(BF16),
                w_out.astype(BF16), norm_mix_post[None, :])
    return _ffn(x1, norm_ffn_pre[None, :], w_up.astype(BF16), w_down.astype(BF16),
                norm_ffn_post[None, :])


def kernel(x, positions, norm_mix_pre, w_in, conv_w, conv_b, i_bias, f_bias, mlstm_norm,
           w_branch_m, w_branch_a, w_out, norm_mix_post, norm_ffn_pre, w_up, w_down,
           norm_ffn_post):
    B = x.shape[0]
    depth = w_in.shape[0]
    outs = []
    for b in range(B):
        xb = x[b]
        for l in range(depth):
            xb = _layer(l, xb, positions[b], norm_mix_pre[l], w_in, conv_w[l], conv_b[l],
                        i_bias[l], f_bias[l], mlstm_norm[l], w_branch_m[l], w_branch_a[l],
                        w_out[l], norm_mix_post[l], norm_ffn_pre[l], w_up[l], w_down[l],
                        norm_ffn_post[l])
        outs.append(xb)
    return outs[0][None] if B == 1 else jnp.stack(outs, axis=0)
```

```python
import functools

import jax
import jax.numpy as jnp
from jax import lax
from jax.experimental import pallas as pl
from jax.experimental.pallas import tpu as pltpu
from jax.experimental.pallas import tpu_sc as plsc

F32 = jnp.float32
BF16 = jnp.bfloat16

M_HEADS = 4
M_HEAD_DIM = 256
M_WIDTH = M_HEADS * M_HEAD_DIM
M_CHUNK = 128
CONV_WIDTH = 4
A_HEADS = 8
A_HEAD_DIM = 128
A_WIDTH = A_HEADS * A_HEAD_DIM
MOBA_BLOCK = 256
MOBA_TOPK = 3
ROPE_THETA = 10000.0
NORM_EPS = 1e-6

LANES = 128
SUBLANES = 8
VMEM_LIMIT = 56 * 1024 * 1024
NEG_BIG = -1e30
LOG2_E = 1.4426950408889634

COL_GM = 0
COL_GA = 2048
COL_MQK = 4096
COL_AQK = 6144
COL_MV = 8192
COL_MO = 9216
COL_AV = 10240


def _cparams(sem):
    return pltpu.CompilerParams(dimension_semantics=sem, vmem_limit_bytes=VMEM_LIMIT)


def _rms(x, gain):
    ms = jnp.mean(x * x, axis=-1, keepdims=True)
    return x * lax.rsqrt(ms + NORM_EPS) * gain


def _sigmoid(x):
    return 1.0 / (1.0 + jnp.exp(-x))


def _split3(x):
    hi = x.astype(BF16)
    r1 = x - hi.astype(F32)
    mid = r1.astype(BF16)
    lo = (r1 - mid.astype(F32)).astype(BF16)
    return hi, mid, lo


def _proj_kernel(x_ref, g_ref, w_ref, wg_ref, p_ref, gate_ref, xn_ref):
    @pl.when(pl.program_id(1) == 0)
    def _():
        xn = _rms(x_ref[...], g_ref[...]).astype(BF16)
        xn_ref[...] = xn
        gate_ref[...] = jnp.dot(xn, wg_ref[...], preferred_element_type=F32)

    p_ref[...] = jnp.dot(xn_ref[...], w_ref[...], preferred_element_type=F32).astype(BF16)


def _proj(x, b, gain, w_all, w_gate, tm=1024, tn=1024):
    _, S, D = x.shape
    N = w_all.shape[1]
    return pl.pallas_call(
        _proj_kernel,
        grid=(S // tm, N // tn),
        in_specs=[
            pl.BlockSpec((None, tm, D), lambda i, j: (b, i, 0)),
            pl.BlockSpec((1, D), lambda i, j: (0, 0)),
            pl.BlockSpec((D, tn), lambda i, j: (0, j)),
            pl.BlockSpec((D, LANES), lambda i, j: (0, 0)),
        ],
        out_specs=[
            pl.BlockSpec((tm, tn), lambda i, j: (i, j)),
            pl.BlockSpec((tm, LANES), lambda i, j: (i, 0)),
        ],
        out_shape=[
            jax.ShapeDtypeStruct((S, N), BF16),
            jax.ShapeDtypeStruct((S, LANES), F32),
        ],
        scratch_shapes=[pltpu.VMEM((tm, D), BF16)],
        compiler_params=_cparams(("parallel", "arbitrary")),
        name="proj",
    )(x, gain, w_all, w_gate)


PREP_ROWS = MOBA_BLOCK
PREP_COLS = 512
HALO_ROWS = 2 * SUBLANES


def _prep_kernel(pmk_ref, halo_ref, paqk_ref, pav_ref, gate_ref, pos_ref, invf_ref,
                 cw_ref, cb_ref, gb_ref,
                 mq_ref, mkT_ref, aq_ref, ak_ref, avT_ref, kmean_ref, gc_ref, gt_ref):
    i = pl.program_id(0)
    R = PREP_ROWS

    k_scale = M_HEAD_DIM ** -0.5
    for c0 in range(0, 2 * M_WIDTH, PREP_COLS):
        cs = slice(c0, c0 + PREP_COLS)
        prev = halo_ref[:, cs].astype(F32)[HALO_ROWS - SUBLANES:, :]
        prev = jnp.where(i == 0, jnp.zeros_like(prev), prev)
        ext = jnp.concatenate([pmk_ref[:, cs].astype(F32), prev], axis=0)
        acc = cw_ref[0:1, cs] * ext
        for j in range(1, CONV_WIDTH):
            acc = pltpu.roll(acc, 1, 0) + cw_ref[j:j + 1, cs] * ext
        acc = acc[0:R, :] + cb_ref[:, cs]
        y = acc * _sigmoid(acc)
        if c0 < M_WIDTH:
            mq_ref[:, cs] = y.astype(BF16)
        else:
            ks = slice(c0 - M_WIDTH, c0 - M_WIDTH + PREP_COLS)
            mkT_ref[ks, :] = (y * k_scale).T.astype(BF16)

    ang = pos_ref[...].astype(F32) * invf_ref[...]
    cos = jnp.cos(ang)
    lane = lax.broadcasted_iota(jnp.int32, (R, A_HEAD_DIM), 1)
    sin_signed = jnp.where(lane < A_HEAD_DIM // 2, -1.0, 1.0) * jnp.sin(ang)
    for h in range(2 * A_HEADS):
        hs = slice(h * A_HEAD_DIM, (h + 1) * A_HEAD_DIM)
        xh = paqk_ref[:, hs].astype(F32)
        yh = xh * cos + pltpu.roll(xh, A_HEAD_DIM // 2, 1) * sin_signed
        if h < A_HEADS:
            aq_ref[h] = yh
        else:
            ko = slice((h - A_HEADS) * A_HEAD_DIM, (h - A_HEADS + 1) * A_HEAD_DIM)
            ak_ref[:, ko] = yh.astype(BF16)
            kmean_ref[0, :, ko] = jnp.mean(yh, axis=0, keepdims=True)
    for c0 in range(0, A_WIDTH, PREP_COLS):
        avT_ref[c0:c0 + PREP_COLS, :] = pav_ref[:, c0:c0 + PREP_COLS].astype(F32).T.astype(BF16)

    g = gate_ref[...] + gb_ref[...]
    log_f = jnp.minimum(g, 0.0) - jnp.log1p(jnp.exp(-jnp.abs(g)))
    r_i = lax.broadcasted_iota(jnp.int32, (R, R), 0)
    c_i = lax.broadcasted_iota(jnp.int32, (R, R), 1)
    tri = ((r_i >= c_i) & ((r_i // M_CHUNK) == (c_i // M_CHUNK))).astype(BF16)
    hi, mid, lo = _split3(log_f)
    csum = (jnp.dot(tri, hi, preferred_element_type=F32)
            + jnp.dot(tri, mid, preferred_element_type=F32)
            + jnp.dot(tri, lo, preferred_element_type=F32))
    glane = lax.broadcasted_iota(jnp.int32, (R, LANES), 1)
    gc = jnp.where(glane < M_HEADS, g, csum)
    gc_ref[...] = gc
    gt_ref[...] = gc.T[0:SUBLANES, :]


def _prep(P, gate, pos, invf, conv_w, conv_b, gbias):
    S = P.shape[0]
    R = PREP_ROWS
    nb = S // R
    halo_blocks = R // HALO_ROWS
    return pl.pallas_call(
        _prep_kernel,
        grid=(nb,),
        in_specs=[
            pl.BlockSpec((R, 2 * M_WIDTH), lambda i: (i, COL_MQK // (2 * M_WIDTH))),
            pl.BlockSpec((HALO_ROWS, 2 * M_WIDTH),
                         lambda i: (jnp.maximum(i * halo_blocks - 1, 0), COL_MQK // (2 * M_WIDTH))),
            pl.BlockSpec((R, 2 * A_WIDTH), lambda i: (i, COL_AQK // (2 * A_WIDTH))),
            pl.BlockSpec((R, A_WIDTH), lambda i: (i, COL_AV // A_WIDTH)),
            pl.BlockSpec((R, LANES), lambda i: (i, 0)),
            pl.BlockSpec((R, 1), lambda i: (i, 0)),
            pl.BlockSpec((1, A_HEAD_DIM), lambda i: (0, 0)),
            pl.BlockSpec((CONV_WIDTH, 2 * M_WIDTH), lambda i: (0, 0)),
            pl.BlockSpec((1, 2 * M_WIDTH), lambda i: (0, 0)),
            pl.BlockSpec((1, LANES), lambda i: (0, 0)),
        ],
        out_specs=[
            pl.BlockSpec((R, M_WIDTH), lambda i: (i, 0)),
            pl.BlockSpec((M_WIDTH, R), lambda i: (0, i)),
            pl.BlockSpec((A_HEADS, R, A_HEAD_DIM), lambda i: (0, i, 0)),
            pl.BlockSpec((R, A_WIDTH), lambda i: (i, 0)),
            pl.BlockSpec((A_WIDTH, R), lambda i: (0, i)),
            pl.BlockSpec((1, 1, A_WIDTH), lambda i: (i, 0, 0)),
            pl.BlockSpec((R, LANES), lambda i: (i, 0)),
            pl.BlockSpec((SUBLANES, R), lambda i: (0, i)),
        ],
        out_shape=[
            jax.ShapeDtypeStruct((S, M_WIDTH), BF16),
            jax.ShapeDtypeStruct((M_WIDTH, S), BF16),
            jax.ShapeDtypeStruct((A_HEADS, S, A_HEAD_DIM), F32),
            jax.ShapeDtypeStruct((S, A_WIDTH), BF16),
            jax.ShapeDtypeStruct((A_WIDTH, S), BF16),
            jax.ShapeDtypeStruct((nb, 1, A_WIDTH), F32),
            jax.ShapeDtypeStruct((S, LANES), F32),
            jax.ShapeDtypeStruct((SUBLANES, S), F32),
        ],
        compiler_params=_cparams(("parallel",)),
        name="prep",
    )(P, P, P, P, gate, pos, invf, conv_w, conv_b, gbias)


M_AUG = M_HEAD_DIM + LANES


def _mlstm_kernel(q_ref, kT_ref, v_ref, mo_ref, gc_ref, gt_ref, gain_ref, out_ref,
                  c_ref, m_ref):
    @pl.when(pl.program_id(0) == 0)
    def _():
        c_ref[...] = jnp.zeros_like(c_ref)
        m_ref[...] = jnp.zeros_like(m_ref)

    L = M_CHUNK
    D = M_HEAD_DIM
    row = lax.broadcasted_iota(jnp.int32, (L, L), 0)
    col = lax.broadcasted_iota(jnp.int32, (L, L), 1)
    causal = row >= col
    ones_col = (lax.broadcasted_iota(jnp.int32, (L, LANES), 1) == 0).astype(BF16)

    heads = range(M_HEADS)
    hsl = [slice(h * D, (h + 1) * D) for h in heads]
    q = [q_ref[:, hsl[h]] for h in heads]
    kT = [kT_ref[hsl[h], :] for h in heads]
    v_aug = [jnp.concatenate([v_ref[:, hsl[h]], ones_col], axis=1) for h in heads]
    b_c = [gc_ref[:, M_HEADS + h:M_HEADS + h + 1] for h in heads]
    b_r = [gt_ref[M_HEADS + h:M_HEADS + h + 1, :] for h in heads]
    u_r = [gt_ref[h:h + 1, :] - b_r[h] for h in heads]
    f_tot = [b_r[h][:, L - 1:L] for h in heads]
    m_prev = [m_ref[h:h + 1, 0:1] for h in heads]

    s_qk = [jnp.dot(q[h], kT[h], preferred_element_type=F32) for h in heads]
    q_c = [jnp.dot(q[h], c_ref[h].astype(BF16), preferred_element_type=F32) for h in heads]

    for h in heads:
        w_r = f_tot[h] + u_r[h]
        m_loc = jnp.max(w_r, axis=1, keepdims=True)
        m_new = jnp.maximum(f_tot[h] + m_prev[h], m_loc)
        a = jnp.exp(f_tot[h] + m_prev[h] - m_new)
        e_r = jnp.exp(w_r - m_new)
        keT = (kT[h].astype(F32) * e_r).astype(BF16)
        c_ref[h] = a * c_ref[h] + jnp.dot(keT, v_aug[h], preferred_element_type=F32)
        m_ref[h:h + 1, :] = jnp.broadcast_to(m_new, (1, LANES))

    for h in heads:
        d_log = jnp.where(causal, b_c[h] + u_r[h], -jnp.inf)
        a_log = b_c[h] + m_prev[h]
        m_t = jnp.maximum(a_log, jnp.max(d_log, axis=1, keepdims=True))
        s_ts = s_qk[h] * jnp.exp(d_log - m_t)
        inter = jnp.exp(a_log - m_t)
        r = inter * q_c[h] + jnp.dot(s_ts.astype(BF16), v_aug[h], preferred_element_type=F32)
        num = r[:, :D]
        den = r[:, D:D + 1]
        hh = num / jnp.maximum(jnp.abs(den), jnp.exp(-m_t))
        hn = hh * lax.rsqrt(jnp.mean(hh * hh, axis=-1, keepdims=True) + NORM_EPS)
        out_ref[:, hsl[h]] = (hn * gain_ref[:, hsl[h]]
                              * _sigmoid(mo_ref[:, hsl[h]].astype(F32))).astype(BF16)


def _mlstm(mq, mkT, P, gc, gt, gain):
    S = mq.shape[0]
    L = M_CHUNK
    return pl.pallas_call(
        _mlstm_kernel,
        grid=(S // L,),
        in_specs=[
            pl.BlockSpec((L, M_WIDTH), lambda c: (c, 0)),
            pl.BlockSpec((M_WIDTH, L), lambda c: (0, c)),
            pl.BlockSpec((L, M_WIDTH), lambda c: (c, COL_MV // M_WIDTH)),
            pl.BlockSpec((L, M_WIDTH), lambda c: (c, COL_MO // M_WIDTH)),
            pl.BlockSpec((L, LANES), lambda c: (c, 0)),
            pl.BlockSpec((SUBLANES, L), lambda c: (0, c)),
            pl.BlockSpec((1, M_WIDTH), lambda c: (0, 0)),
        ],
        out_specs=pl.BlockSpec((L, M_WIDTH), lambda c: (c, 0)),
        out_shape=jax.ShapeDtypeStruct((S, M_WIDTH), BF16),
        scratch_shapes=[
            pltpu.VMEM((M_HEADS, M_HEAD_DIM, M_AUG), F32),
            pltpu.VMEM((SUBLANES, LANES), F32),
        ],
        compiler_params=_cparams(("arbitrary",)),
        name="mlstm",
    )(mq, mkT, P, P, gc, gt, gain)


MOBA_HEADS_PER_GROUP = 4
ROUTE_TILES = 264
ROUTE_ROWS = ROUTE_TILES * MOBA_BLOCK
GROUP_TILES_PER_STEP = 8
ROW_WORDS = A_HEAD_DIM // 2
SC_WINDOW = 256


def _pack_rows(o_t, lse):
    q = o_t.shape[1]
    hi = pltpu.bitcast(o_t[:ROW_WORDS].astype(BF16).astype(F32), jnp.uint32)
    lo = pltpu.bitcast(o_t[ROW_WORDS:].astype(BF16).astype(F32), jnp.uint32)
    words = pltpu.bitcast(hi | (lo >> 16), F32)
    tail = jnp.concatenate([jnp.broadcast_to(lse, (SUBLANES, q)),
                            jnp.zeros((LANES - ROW_WORDS - SUBLANES, q), F32)], axis=0)
    return jnp.concatenate([words, tail], axis=0).T


def _dot_nt(a, b):
    return lax.dot_general(a, b, (((1,), (1,)), ((), ())), preferred_element_type=F32)


def _block_partial(k_blk, v_t_blk, qs, ones_rows, causal_mask=None):
    s = _dot_nt(k_blk, qs)
    if causal_mask is not None:
        s = jnp.where(causal_mask, s, NEG_BIG)
    sb = s.astype(BF16)
    m = jnp.max(sb, axis=0, keepdims=True)
    p = jnp.exp2(sb - m)
    r = jnp.dot(jnp.concatenate([v_t_blk, ones_rows], axis=0), p, preferred_element_type=F32)
    dh = v_t_blk.shape[0]
    l = r[dh:dh + 1, :]
    return r[:dh, :] / l, m.astype(F32) + jnp.log2(l)


def _route_kernel(q_ref, k_ref, vT_ref, km_ref, route_ref, cnt_ref, own_ref, cnt_acc):
    i = pl.program_id(1)
    BS = MOBA_BLOCK
    Dh = A_HEAD_DIM
    NB = km_ref.shape[0]
    G = q_ref.shape[0]
    blk = lax.broadcasted_iota(jnp.int32, (NB, BS), 0)
    kpos = lax.broadcasted_iota(jnp.int32, (BS, BS), 0)
    qpos = lax.broadcasted_iota(jnp.int32, (BS, BS), 1)
    earlier = (kpos < qpos).astype(BF16)
    ones_rows = jnp.ones((2 * SUBLANES, BS), BF16)

    @pl.when(i == 0)
    def _():
        cnt_acc[...] = jnp.zeros_like(cnt_acc)

    q = [q_ref[g] for g in range(G)]
    qs = [(q[g] * (Dh ** -0.5 * LOG2_E)).astype(BF16) for g in range(G)]
    parts = [_block_partial(k_ref[:, g * Dh:(g + 1) * Dh], vT_ref[g * Dh:(g + 1) * Dh, :], qs[g],
                            ones_rows, kpos <= qpos) for g in range(G)]

    gates = []
    for g in range(G):
        hs = slice(g * Dh, (g + 1) * Dh)
        km = km_ref[:, hs]
        kh = km.astype(BF16)
        kl = (km - kh.astype(F32)).astype(BF16)
        qh = q[g].astype(BF16)
        ql = (q[g] - qh.astype(F32)).astype(BF16)
        gate = _dot_nt(kh, qh) + _dot_nt(kh, ql) + _dot_nt(kl, qh)
        gates.append(jnp.where(blk < i, gate, -jnp.inf))
    picks = [[] for _ in range(G)]
    rows = [[] for _ in range(G)]
    for r in range(MOBA_TOPK):
        for g in range(G):
            mx = jnp.max(gates[g], axis=0, keepdims=True)
            idx = jnp.min(jnp.where(gates[g] == mx, blk, NB), axis=0, keepdims=True)
            idx = jnp.where(r < i, idx, -1)
            pick = blk == idx
            gates[g] = jnp.where(pick, -jnp.inf, gates[g])
            picks[g].append(pick)
            rows[g].append(idx)
    for g in range(G):
        onehot = sum(p.astype(F32) for p in picks[g])
        before = cnt_acc[g][:, 0:1] + jnp.dot(onehot.astype(BF16), earlier, preferred_element_type=F32)
        for r in range(MOBA_TOPK):
            rank = jnp.sum(jnp.where(picks[g][r], before, 0.0), axis=0, keepdims=True)
            rows[g].append(rank.astype(jnp.int32))
        rows[g].append(jnp.zeros((SUBLANES - 2 * MOBA_TOPK, BS), jnp.int32))
        route_ref[g] = jnp.concatenate(rows[g], axis=0)
        cnt_new = cnt_acc[g] + jnp.sum(onehot, axis=1, keepdims=True)
        cnt_acc[g] = cnt_new
        cnt_ref[g] = cnt_new
        own_ref[g] = _pack_rows(*parts[g])


def _route(aq_hm, ak, avT, kmean, h0, H):
    _, S, Dh = aq_hm.shape
    BS = MOBA_BLOCK
    NB = S // BS
    G = MOBA_HEADS_PER_GROUP
    W = G * Dh
    hb = h0 // G
    return pl.pallas_call(
        _route_kernel,
        grid=(H // G, NB),
        in_specs=[
            pl.BlockSpec((G, BS, Dh), lambda h, i: (hb + h, i, 0)),
            pl.BlockSpec((BS, W), lambda h, i: (i, hb + h)),
            pl.BlockSpec((W, BS), lambda h, i: (hb + h, i)),
            pl.BlockSpec((NB, W), lambda h, i: (0, hb + h)),
        ],
        out_specs=[
            pl.BlockSpec((G, SUBLANES, BS), lambda h, i: (h, 0, i)),
            pl.BlockSpec((G, NB, LANES), lambda h, i: (h, 0, 0)),
            pl.BlockSpec((G, BS, LANES), lambda h, i: (h, i, 0)),
        ],
        out_shape=[
            jax.ShapeDtypeStruct((H, SUBLANES, S), jnp.int32),
            jax.ShapeDtypeStruct((H, NB, LANES), F32),
            jax.ShapeDtypeStruct((H, S, LANES), F32),
        ],
        scratch_shapes=[pltpu.VMEM((G, NB, LANES), F32)],
        compiler_params=_cparams(("parallel", "arbitrary")),
        name="moba_route",
    )(aq_hm, ak, avT, kmean)


def _dest_kernel(seg_ref, route_ref, dest_ref):
    h = pl.program_id(0)
    NB = seg_ref.shape[1]
    r = route_ref[0]
    blk = r[0:MOBA_TOPK, :]
    rank = r[MOBA_TOPK:2 * MOBA_TOPK, :]
    base = h * ROUTE_ROWS
    dest = jnp.full(blk.shape, base + ROUTE_ROWS - 1, jnp.int32)
    for j in range(NB):
        dest = jnp.where(blk == j, base + seg_ref[h, j] + rank, dest)
    dest_ref[0] = jnp.concatenate(
        [dest, jnp.zeros((SUBLANES - MOBA_TOPK, dest.shape[1]), jnp.int32)], axis=0)


def _dest(seg_start, route):
    H, _, S = route.shape
    return pl.pallas_call(
        _dest_kernel,
        grid_spec=pltpu.PrefetchScalarGridSpec(
            num_scalar_prefetch=1,
            grid=(H,),
            in_specs=[pl.BlockSpec((1, SUBLANES, S), lambda h, seg: (h, 0, 0))],
            out_specs=pl.BlockSpec((1, SUBLANES, S), lambda h, seg: (h, 0, 0)),
        ),
        out_shape=jax.ShapeDtypeStruct((H, SUBLANES, S), jnp.int32),
        compiler_params=_cparams(("parallel",)),
        name="moba_dest",
    )(seg_start, route)


def _sc_mesh():
    return plsc.VectorSubcoreMesh(core_axis_name="c", subcore_axis_name="s")


def _sc_scatter(rows, row0, n_rows, idx, n_out):
    W = rows.shape[1]
    M = idx.shape[0]
    n_win = n_rows // SC_WINDOW
    win0 = row0 // SC_WINDOW

    @pl.kernel(out_type=jax.ShapeDtypeStruct((n_out, W), rows.dtype), mesh=_sc_mesh())
    def k(x_hbm, i_hbm, o_hbm):
        def body(x_vmem, i_vmem):
            pltpu.sync_copy(x_vmem, o_hbm.at[i_vmem.at[0]])

        pltpu.emit_pipeline(
            body,
            grid=(M // SC_WINDOW,),
            in_specs=[pl.BlockSpec((SC_WINDOW, W), lambda w: (win0 + lax.rem(w, n_win), 0)),
                      pl.BlockSpec((1, SC_WINDOW), lambda w: (0, w))],
            out_specs=[],
            core_axis_name=("c", "s"),
            dimension_semantics=(pltpu.PARALLEL,),
        )(x_hbm, i_hbm)

    return k(rows, idx.reshape(1, M))


def _sc_gather(table, idx):
    M = idx.shape[0]
    W = table.shape[1]

    @pl.kernel(out_type=jax.ShapeDtypeStruct((M, W), table.dtype), mesh=_sc_mesh())
    def k(x_hbm, i_hbm, o_hbm):
        def body(i_vmem, o_vmem):
            pltpu.sync_copy(x_hbm.at[i_vmem.at[0]], o_vmem)

        pltpu.emit_pipeline(
            body,
            grid=(M // SC_WINDOW,),
            in_specs=[pl.BlockSpec((1, SC_WINDOW), lambda w: (0, w))],
            out_specs=[pl.BlockSpec((SC_WINDOW, W), lambda w: (w, 0))],
            core_axis_name=("c", "s"),
            dimension_semantics=(pltpu.PARALLEL,),
        )(i_hbm, o_hbm)

    return k(table, idx.reshape(1, M))


def _group_kernel(tile_blk_ref, q_ref, k_ref, vT_ref, out_ref, s_ref, m_ref):
    h = pl.program_id(0)
    u = pl.program_id(1)
    last = pl.num_programs(1) - 2
    BS = MOBA_BLOCK
    Dh = A_HEAD_DIM
    T = GROUP_TILES_PER_STEP
    ones_rows = jnp.ones((2 * SUBLANES, BS), BF16)
    new = lax.rem(u, 2)
    old = 1 - new

    @pl.when((h == 0) & (u == 0))
    def _():
        s_ref[...] = jnp.zeros_like(s_ref)
        m_ref[...] = jnp.zeros_like(m_ref)

    def block_start(group, c):
        j = jnp.maximum(tile_blk_ref[h * ROUTE_TILES + group * T + c], 0)
        return pl.multiple_of(j * BS, BS)

    g_new = jnp.minimum(u, last)
    g_old = jnp.maximum(u - 1, 0)
    @pl.when(tile_blk_ref[h * ROUTE_TILES + g_old * T] >= 0)
    def _():
        ms, rs = [], []
        for c in range(T):
            m = m_ref[old, c]
            p = jnp.exp2(s_ref[old, c] - m.astype(BF16))
            v_aug = jnp.concatenate([vT_ref[:, pl.ds(block_start(g_old, c), BS)], ones_rows], axis=0)
            ms.append(m)
            rs.append(jnp.dot(v_aug, p, preferred_element_type=F32))
        s_new = []
        for c in range(T):
            qs = (q_ref[c * BS:(c + 1) * BS, :] * (Dh ** -0.5 * LOG2_E)).astype(BF16)
            s_new.append(_dot_nt(k_ref[pl.ds(block_start(g_new, c), BS), :], qs))
        for c in range(T):
            l = rs[c][Dh:Dh + 1, :]
            out_ref[c * BS:(c + 1) * BS, :] = _pack_rows(rs[c][:Dh, :] / l, ms[c] + jnp.log2(l))
            sb = s_new[c].astype(BF16)
            s_ref[new, c] = sb
            m_ref[new, c] = jnp.max(sb, axis=0, keepdims=True).astype(F32)


def _group(tile_blk, q_sorted, ak, avT, h0):
    S = ak.shape[0]
    H = q_sorted.shape[0] // ROUTE_ROWS
    Dh = A_HEAD_DIM
    BS = MOBA_BLOCK
    T = GROUP_TILES_PER_STEP
    steps = ROUTE_TILES // T
    rows = T * BS
    return pl.pallas_call(
        _group_kernel,
        grid_spec=pltpu.PrefetchScalarGridSpec(
            num_scalar_prefetch=1,
            grid=(H, steps + 1),
            in_specs=[
                pl.BlockSpec((rows, LANES), lambda h, u, tb: (h * steps + jnp.minimum(u, steps - 1), 0)),
                pl.BlockSpec((S, Dh), lambda h, u, tb: (0, h0 + h)),
                pl.BlockSpec((Dh, S), lambda h, u, tb: (h0 + h, 0)),
            ],
            out_specs=pl.BlockSpec((rows, LANES), lambda h, u, tb: (h * steps + jnp.maximum(u - 1, 0), 0)),
            scratch_shapes=[pltpu.VMEM((2, T, BS, BS), BF16), pltpu.VMEM((2, T, 1, BS), F32)],
        ),
        out_shape=jax.ShapeDtypeStruct((H * ROUTE_ROWS, LANES), F32),
        compiler_params=_cparams(("arbitrary", "arbitrary")),
        name="moba_group",
    )(tile_blk, q_sorted, ak, avT)


def _combine_kernel(*refs):
    i = pl.program_id(0)
    o_ref = refs[-1]
    n_groups = (len(refs) - 1) // 2
    H = refs[n_groups].shape[0]
    BS = refs[n_groups].shape[1]
    low_half = lax.broadcasted_iota(jnp.int32, (BS, LANES), 1) < ROW_WORDS
    for head in range(n_groups * H):
        got_ref, own_ref, h = refs[head // H], refs[n_groups + head // H], head % H
        tiles = [own_ref[h]]
        for r in range(MOBA_TOPK):
            tiles.append(jnp.where(r < i, got_ref[h, r], 0.0))
        lses = [tiles[0][:, ROW_WORDS:ROW_WORDS + 1]]
        lses += [jnp.where(r < i, tiles[r + 1][:, ROW_WORDS:ROW_WORDS + 1], NEG_BIG)
                 for r in range(MOBA_TOPK)]
        top = functools.reduce(jnp.maximum, lses)
        w = [jnp.exp2(x - top) for x in lses]
        inv = 1.0 / sum(w)
        first = jnp.zeros((BS, LANES), F32)
        second = jnp.zeros((BS, LANES), F32)
        for wk, tile in zip(w, tiles):
            words = pltpu.bitcast(tile, jnp.uint32)
            first = first + wk * pltpu.bitcast(words & jnp.uint32(0xFFFF0000), F32)
            second = second + wk * pltpu.bitcast(words << 16, F32)
        o = jnp.where(low_half, first, pltpu.roll(second, ROW_WORDS, 1)) * inv
        o_ref[:, head * A_HEAD_DIM:(head + 1) * A_HEAD_DIM] = o.astype(BF16)


def _combine(gots, owns):
    H, S, _ = owns[0].shape
    BS = MOBA_BLOCK
    return pl.pallas_call(
        _combine_kernel,
        grid=(S // BS,),
        in_specs=([pl.BlockSpec((H, MOBA_TOPK, BS, LANES), lambda i: (0, 0, i, 0))] * len(gots)
                  + [pl.BlockSpec((H, BS, LANES), lambda i: (0, i, 0))] * len(owns)),
        out_specs=pl.BlockSpec((BS, A_WIDTH), lambda i: (i, 0)),
        out_shape=jax.ShapeDtypeStruct((S, A_WIDTH), BF16),
        compiler_params=_cparams(("parallel",)),
        name="moba_combine",
    )(*gots, *owns)


def _moba_routed(aq_hm, ak, avT, kmean):
    H, S, Dh = aq_hm.shape
    NB = S // MOBA_BLOCK
    hn = MOBA_HEADS_PER_GROUP
    q_rows = aq_hm.reshape(H * S, Dh)
    gots, owns = [], []
    for h0 in range(0, H, hn):
        route, cnt, own = _route(aq_hm, ak, avT, kmean, h0, hn)
        cnt = cnt[:, :, 0].astype(jnp.int32)
        seg_tiles = (cnt + MOBA_BLOCK - 1) // MOBA_BLOCK
        seg_end = jnp.cumsum(seg_tiles, axis=1)
        seg_start = (seg_end - seg_tiles) * MOBA_BLOCK
        tile_ids = jnp.arange(ROUTE_TILES, dtype=jnp.int32)
        tile_blk = jnp.sum(tile_ids[None, :, None] >= seg_end[:, None, :], axis=2).astype(jnp.int32)
        tile_blk = jnp.where(tile_blk < NB, tile_blk, -1).reshape(hn * ROUTE_TILES)
        dest = _dest(seg_start, route)[:, :MOBA_TOPK, :]
        q_sorted = _sc_scatter(q_rows, h0 * S, hn * S, dest.transpose(1, 0, 2).reshape(-1),
                               hn * ROUTE_ROWS)
        results = _group(tile_blk, q_sorted, ak, avT, h0)
        gots.append(_sc_gather(results, dest.reshape(-1)).reshape(hn, MOBA_TOPK, S, LANES))
        owns.append(own)
    return _combine(gots, owns)


def _merge_kernel(hm_ref, ha_ref, gm_ref, ga_ref, x_ref, wm_ref, wa_ref, wo_ref, gain_ref,
                  out_ref):
    ym = jnp.dot(hm_ref[...], wm_ref[...], preferred_element_type=F32)
    ya = jnp.dot(ha_ref[...], wa_ref[...], preferred_element_type=F32)
    merged = _sigmoid(gm_ref[...].astype(F32)) * ym + _sigmoid(ga_ref[...].astype(F32)) * ya
    mix = jnp.dot(merged.astype(BF16), wo_ref[...], preferred_element_type=F32)
    out_ref[...] = x_ref[...] + _rms(mix, gain_ref[...])


def _merge(hm, ha, P, x, b, wm, wa, wo, gain, tm=256):
    _, S, D = x.shape
    const = pl.Buffered(1)
    return pl.pallas_call(
        _merge_kernel,
        grid=(S // tm,),
        in_specs=[
            pl.BlockSpec((tm, M_WIDTH), lambda i: (i, 0)),
            pl.BlockSpec((tm, A_WIDTH), lambda i: (i, 0)),
            pl.BlockSpec((tm, D), lambda i: (i, COL_GM // D)),
            pl.BlockSpec((tm, D), lambda i: (i, COL_GA // D)),
            pl.BlockSpec((None, tm, D), lambda i: (b, i, 0)),
            pl.BlockSpec((M_WIDTH, D), lambda i: (0, 0), pipeline_mode=const),
            pl.BlockSpec((A_WIDTH, D), lambda i: (0, 0), pipeline_mode=const),
            pl.BlockSpec((D, D), lambda i: (0, 0), pipeline_mode=const),
            pl.BlockSpec((1, D), lambda i: (0, 0)),
        ],
        out_specs=pl.BlockSpec((tm, D), lambda i: (i, 0)),
        out_shape=jax.ShapeDtypeStruct((S, D), F32),
        compiler_params=_cparams(("parallel",)),
        name="merge",
    )(hm, ha, P, P, x, wm, wa, wo, gain)


def _ffn_kernel(x_ref, gpre_ref, wu_ref, wd_ref, gpost_ref, out_ref, hn_ref, acc_ref):
    f = pl.program_id(1)

    @pl.when(f == 0)
    def _():
        hn_ref[...] = _rms(x_ref[...], gpre_ref[...]).astype(BF16)
        acc_ref[...] = jnp.zeros_like(acc_ref)

    u = jnp.dot(hn_ref[...], wu_ref[...], preferred_element_type=F32)
    u = jnp.square(jnp.maximum(u, 0.0)).astype(BF16)
    acc_ref[...] += jnp.dot(u, wd_ref[...], preferred_element_type=F32)

    @pl.when(f == pl.num_programs(1) - 1)
    def _():
        out_ref[...] = x_ref[...] + _rms(acc_ref[...], gpost_ref[...])


def _ffn(x, gpre, wu, wd, gpost, tm=512, tf=1024):
    S, D = x.shape
    Fd = wu.shape[1]
    return pl.pallas_call(
        _ffn_kernel,
        grid=(S // tm, Fd // tf),
        in_specs=[
            pl.BlockSpec((tm, D), lambda i, f: (i, 0)),
            pl.BlockSpec((1, D), lambda i, f: (0, 0)),
            pl.BlockSpec((D, tf), lambda i, f: (0, f)),
            pl.BlockSpec((tf, D), lambda i, f: (f, 0)),
            pl.BlockSpec((1, D), lambda i, f: (0, 0)),
        ],
        out_specs=pl.BlockSpec((tm, D), lambda i, f: (i, 0)),
        out_shape=jax.ShapeDtypeStruct((S, D), F32),
        scratch_shapes=[pltpu.VMEM((tm, D), BF16), pltpu.VMEM((tm, D), F32)],
        compiler_params=_cparams(("parallel", "arbitrary")),
        name="ffn",
    )(x, gpre, wu, wd, gpost)


def _layer(x, b, pos, norm_mix_pre, w_in, conv_w, conv_b, i_bias, f_bias, mlstm_norm,
           w_branch_m, w_branch_a, w_out, norm_mix_post, norm_ffn_pre, w_up, w_down,
           norm_ffn_post):
    _, S, D = x.shape
    o = 0
    pieces = {}
    w_in = w_in.astype(BF16)
    for name, width in (("mq", M_WIDTH), ("mk", M_WIDTH), ("mv", M_WIDTH), ("mo", M_WIDTH),
                        ("mi", M_HEADS), ("mf", M_HEADS), ("aq", A_WIDTH), ("ak", A_WIDTH),
                        ("av", A_WIDTH), ("gm", D), ("ga", D)):
        pieces[name] = w_in[:, o:o + width]
        o += width
    w_all = jnp.concatenate([pieces[n] for n in ("gm", "ga", "mq", "mk", "aq", "ak", "mv", "mo", "av")],
                            axis=1)
    w_gate = jnp.concatenate(
        [pieces["mi"], pieces["mf"], jnp.zeros((D, LANES - 2 * M_HEADS), BF16)], axis=1)
    gbias = jnp.concatenate([i_bias, f_bias, jnp.zeros((LANES - 2 * M_HEADS,), F32)])[None, :]
    half = jnp.arange(0, A_HEAD_DIM, 2, dtype=F32) / A_HEAD_DIM
    inv_freq = 1.0 / (ROPE_THETA ** half)
    invf = jnp.concatenate([inv_freq, inv_freq])[None, :]

    P, gate = _proj(x, b, norm_mix_pre[None, :], w_all, w_gate)
    mq, mkT, aq, ak, avT, kmean, gc, gt = _prep(
        P, gate, pos.reshape(S, 1), invf, conv_w, conv_b[None, :], gbias)
    hm = _mlstm(mq, mkT, P, gc, gt, mlstm_norm[None, :])
    ha = _moba_routed(aq, ak, avT, kmean.reshape(S // MOBA_BLOCK, A_WIDTH))
    x1 = _merge(hm, ha, P, x, b, w_branch_m.astype(BF16), w_branch_a.astype(BF16),
                w_out.astype(BF16), norm_mix_post[None, :])
    return _ffn(x1, norm_ffn_pre[None, :], w_up.astype(BF16), w_down.astype(BF16),
                norm_ffn_post[None, :])


def kernel(x, positions, norm_mix_pre, w_in, conv_w, conv_b, i_bias, f_bias, mlstm_norm,
           w_branch_m, w_branch_a, w_out, norm_mix_post, norm_ffn_pre, w_up, w_down,
           norm_ffn_post):
    B = x.shape[0]
    depth = w_in.shape[0]
    outs = []
    for b in range(B):
        xin, bi = x, b
        for l in range(depth):
            xb = _layer(xin, bi, positions[b], norm_mix_pre[l], w_in[l], conv_w[l], conv_b[l],
                        i_bias[l], f_bias[l], mlstm_norm[l], w_branch_m[l], w_branch_a[l],
                        w_out[l], norm_mix_post[l], norm_ffn_pre[l], w_up[l], w_down[l],
                        norm_ffn_post[l])
            xin, bi = xb[None], 0
        outs.append(xb)
    return outs[0][None] if B == 1 else jnp.stack(outs, axis=0)
```

```python
import functools

import jax
import jax.numpy as jnp
from jax import lax
from jax.experimental import pallas as pl
from jax.experimental.pallas import tpu as pltpu
from jax.experimental.pallas import tpu_sc as plsc

F32 = jnp.float32
BF16 = jnp.bfloat16

M_HEADS = 4
M_HEAD_DIM = 256
M_WIDTH = M_HEADS * M_HEAD_DIM
M_CHUNK = 128
CONV_WIDTH = 4
A_HEADS = 8
A_HEAD_DIM = 128
A_WIDTH = A_HEADS * A_HEAD_DIM
MOBA_BLOCK = 256
MOBA_TOPK = 3
ROPE_THETA = 10000.0
NORM_EPS = 1e-6

LANES = 128
SUBLANES = 8
VMEM_LIMIT = 56 * 1024 * 1024
NEG_BIG = -1e30
LOG2_E = 1.4426950408889634

COL_GM = 0
COL_GA = 2048
COL_MQK = 4096
COL_AQK = 6144
COL_MV = 8192
COL_MO = 9216
COL_AV = 10240


def _cparams(sem):
    return pltpu.CompilerParams(dimension_semantics=sem, vmem_limit_bytes=VMEM_LIMIT)


def _rms(x, gain):
    ms = jnp.mean(x * x, axis=-1, keepdims=True)
    return x * lax.rsqrt(ms + NORM_EPS) * gain


def _sigmoid(x):
    return 1.0 / (1.0 + jnp.exp(-x))


def _split3(x):
    hi = x.astype(BF16)
    r1 = x - hi.astype(F32)
    mid = r1.astype(BF16)
    lo = (r1 - mid.astype(F32)).astype(BF16)
    return hi, mid, lo


def _proj_kernel(x_ref, g_ref, w_ref, wg_ref, p_ref, gate_ref, xn_ref):
    @pl.when(pl.program_id(1) == 0)
    def _():
        xn = _rms(x_ref[...], g_ref[...]).astype(BF16)
        xn_ref[...] = xn
        gate_ref[...] = jnp.dot(xn, wg_ref[...], preferred_element_type=F32)

    p_ref[...] = jnp.dot(xn_ref[...], w_ref[...], preferred_element_type=F32).astype(BF16)


def _proj(x, b, gain, w_all, w_gate, tm=1024, tn=1024):
    _, S, D = x.shape
    N = w_all.shape[1]
    return pl.pallas_call(
        _proj_kernel,
        grid=(S // tm, N // tn),
        in_specs=[
            pl.BlockSpec((None, tm, D), lambda i, j: (b, i, 0)),
            pl.BlockSpec((1, D), lambda i, j: (0, 0)),
            pl.BlockSpec((D, tn), lambda i, j: (0, j)),
            pl.BlockSpec((D, LANES), lambda i, j: (0, 0)),
        ],
        out_specs=[
            pl.BlockSpec((tm, tn), lambda i, j: (i, j)),
            pl.BlockSpec((tm, LANES), lambda i, j: (i, 0)),
        ],
        out_shape=[
            jax.ShapeDtypeStruct((S, N), BF16),
            jax.ShapeDtypeStruct((S, LANES), F32),
        ],
        scratch_shapes=[pltpu.VMEM((tm, D), BF16)],
        compiler_params=_cparams(("parallel", "arbitrary")),
        name="proj",
    )(x, gain, w_all, w_gate)


PREP_ROWS = MOBA_BLOCK
PREP_COLS = 512
HALO_ROWS = 2 * SUBLANES


def _prep_kernel(pmk_ref, halo_ref, paqk_ref, pav_ref, gate_ref, pos_ref, invf_ref,
                 cw_ref, cb_ref, gb_ref,
                 mq_ref, mkT_ref, aq_ref, ak_ref, avT_ref, kmean_ref, gc_ref, gt_ref):
    i = pl.program_id(0)
    R = PREP_ROWS

    k_scale = M_HEAD_DIM ** -0.5
    for c0 in range(0, 2 * M_WIDTH, PREP_COLS):
        cs = slice(c0, c0 + PREP_COLS)
        prev = halo_ref[:, cs].astype(F32)[HALO_ROWS - SUBLANES:, :]
        prev = jnp.where(i == 0, jnp.zeros_like(prev), prev)
        ext = jnp.concatenate([pmk_ref[:, cs].astype(F32), prev], axis=0)
        acc = cw_ref[0:1, cs] * ext
        for j in range(1, CONV_WIDTH):
            acc = pltpu.roll(acc, 1, 0) + cw_ref[j:j + 1, cs] * ext
        acc = acc[0:R, :] + cb_ref[:, cs]
        y = acc * _sigmoid(acc)
        if c0 < M_WIDTH:
            mq_ref[:, cs] = y.astype(BF16)
        else:
            ks = slice(c0 - M_WIDTH, c0 - M_WIDTH + PREP_COLS)
            mkT_ref[ks, :] = (y * k_scale).T.astype(BF16)

    ang = pos_ref[...].astype(F32) * invf_ref[...]
    cos = jnp.cos(ang)
    lane = lax.broadcasted_iota(jnp.int32, (R, A_HEAD_DIM), 1)
    sin_signed = jnp.where(lane < A_HEAD_DIM // 2, -1.0, 1.0) * jnp.sin(ang)
    for h in range(2 * A_HEADS):
        hs = slice(h * A_HEAD_DIM, (h + 1) * A_HEAD_DIM)
        xh = paqk_ref[:, hs].astype(F32)
        yh = xh * cos + pltpu.roll(xh, A_HEAD_DIM // 2, 1) * sin_signed
        if h < A_HEADS:
            aq_ref[h] = yh
        else:
            ko = slice((h - A_HEADS) * A_HEAD_DIM, (h - A_HEADS + 1) * A_HEAD_DIM)
            ak_ref[:, ko] = yh.astype(BF16)
            kmean_ref[0, :, ko] = jnp.mean(yh, axis=0, keepdims=True)
    for c0 in range(0, A_WIDTH, PREP_COLS):
        avT_ref[c0:c0 + PREP_COLS, :] = pav_ref[:, c0:c0 + PREP_COLS].astype(F32).T.astype(BF16)

    g = gate_ref[...] + gb_ref[...]
    log_f = jnp.minimum(g, 0.0) - jnp.log1p(jnp.exp(-jnp.abs(g)))
    r_i = lax.broadcasted_iota(jnp.int32, (R, R), 0)
    c_i = lax.broadcasted_iota(jnp.int32, (R, R), 1)
    tri = ((r_i >= c_i) & ((r_i // M_CHUNK) == (c_i // M_CHUNK))).astype(BF16)
    hi, mid, lo = _split3(log_f)
    csum = (jnp.dot(tri, hi, preferred_element_type=F32)
            + jnp.dot(tri, mid, preferred_element_type=F32)
            + jnp.dot(tri, lo, preferred_element_type=F32))
    glane = lax.broadcasted_iota(jnp.int32, (R, LANES), 1)
    gc = jnp.where(glane < M_HEADS, g, csum)
    gc_ref[...] = gc
    gt_ref[...] = gc.T[0:SUBLANES, :]


def _prep(P, gate, pos, invf, conv_w, conv_b, gbias):
    S = P.shape[0]
    R = PREP_ROWS
    nb = S // R
    halo_blocks = R // HALO_ROWS
    return pl.pallas_call(
        _prep_kernel,
        grid=(nb,),
        in_specs=[
            pl.BlockSpec((R, 2 * M_WIDTH), lambda i: (i, COL_MQK // (2 * M_WIDTH))),
            pl.BlockSpec((HALO_ROWS, 2 * M_WIDTH),
                         lambda i: (jnp.maximum(i * halo_blocks - 1, 0), COL_MQK // (2 * M_WIDTH))),
            pl.BlockSpec((R, 2 * A_WIDTH), lambda i: (i, COL_AQK // (2 * A_WIDTH))),
            pl.BlockSpec((R, A_WIDTH), lambda i: (i, COL_AV // A_WIDTH)),
            pl.BlockSpec((R, LANES), lambda i: (i, 0)),
            pl.BlockSpec((R, 1), lambda i: (i, 0)),
            pl.BlockSpec((1, A_HEAD_DIM), lambda i: (0, 0)),
            pl.BlockSpec((CONV_WIDTH, 2 * M_WIDTH), lambda i: (0, 0)),
            pl.BlockSpec((1, 2 * M_WIDTH), lambda i: (0, 0)),
            pl.BlockSpec((1, LANES), lambda i: (0, 0)),
        ],
        out_specs=[
            pl.BlockSpec((R, M_WIDTH), lambda i: (i, 0)),
            pl.BlockSpec((M_WIDTH, R), lambda i: (0, i)),
            pl.BlockSpec((A_HEADS, R, A_HEAD_DIM), lambda i: (0, i, 0)),
            pl.BlockSpec((R, A_WIDTH), lambda i: (i, 0)),
            pl.BlockSpec((A_WIDTH, R), lambda i: (0, i)),
            pl.BlockSpec((1, 1, A_WIDTH), lambda i: (i, 0, 0)),
            pl.BlockSpec((R, LANES), lambda i: (i, 0)),
            pl.BlockSpec((SUBLANES, R), lambda i: (0, i)),
        ],
        out_shape=[
            jax.ShapeDtypeStruct((S, M_WIDTH), BF16),
            jax.ShapeDtypeStruct((M_WIDTH, S), BF16),
            jax.ShapeDtypeStruct((A_HEADS, S, A_HEAD_DIM), F32),
            jax.ShapeDtypeStruct((S, A_WIDTH), BF16),
            jax.ShapeDtypeStruct((A_WIDTH, S), BF16),
            jax.ShapeDtypeStruct((nb, 1, A_WIDTH), F32),
            jax.ShapeDtypeStruct((S, LANES), F32),
            jax.ShapeDtypeStruct((SUBLANES, S), F32),
        ],
        compiler_params=_cparams(("parallel",)),
        name="prep",
    )(P, P, P, P, gate, pos, invf, conv_w, conv_b, gbias)


M_AUG = M_HEAD_DIM + LANES


def _mlstm_kernel(q_ref, kT_ref, v_ref, mo_ref, gc_ref, gt_ref, gain_ref, out_ref,
                  c_ref, m_ref):
    @pl.when(pl.program_id(0) == 0)
    def _():
        c_ref[...] = jnp.zeros_like(c_ref)
        m_ref[...] = jnp.zeros_like(m_ref)

    L = M_CHUNK
    D = M_HEAD_DIM
    row = lax.broadcasted_iota(jnp.int32, (L, L), 0)
    col = lax.broadcasted_iota(jnp.int32, (L, L), 1)
    causal = row >= col
    ones_col = (lax.broadcasted_iota(jnp.int32, (L, LANES), 1) == 0).astype(BF16)

    heads = range(M_HEADS)
    hsl = [slice(h * D, (h + 1) * D) for h in heads]
    q = [q_ref[:, hsl[h]] for h in heads]
    kT = [kT_ref[hsl[h], :] for h in heads]
    v_aug = [jnp.concatenate([v_ref[:, hsl[h]], ones_col], axis=1) for h in heads]
    b_c = [gc_ref[:, M_HEADS + h:M_HEADS + h + 1] for h in heads]
    b_r = [gt_ref[M_HEADS + h:M_HEADS + h + 1, :] for h in heads]
    u_r = [gt_ref[h:h + 1, :] - b_r[h] for h in heads]
    f_tot = [b_r[h][:, L - 1:L] for h in heads]
    m_prev = [m_ref[h:h + 1, 0:1] for h in heads]

    s_qk = [jnp.dot(q[h], kT[h], preferred_element_type=F32) for h in heads]
    q_c = [jnp.dot(q[h], c_ref[h].astype(BF16), preferred_element_type=F32) for h in heads]

    for h in heads:
        w_r = f_tot[h] + u_r[h]
        m_loc = jnp.max(w_r, axis=1, keepdims=True)
        m_new = jnp.maximum(f_tot[h] + m_prev[h], m_loc)
        a = jnp.exp(f_tot[h] + m_prev[h] - m_new)
        e_r = jnp.exp(w_r - m_new)
        keT = (kT[h].astype(F32) * e_r).astype(BF16)
        c_ref[h] = a * c_ref[h] + jnp.dot(keT, v_aug[h], preferred_element_type=F32)
        m_ref[h:h + 1, :] = jnp.broadcast_to(m_new, (1, LANES))

    for h in heads:
        d_log = jnp.where(causal, b_c[h] + u_r[h], -jnp.inf)
        a_log = b_c[h] + m_prev[h]
        m_t = jnp.maximum(a_log, jnp.max(d_log, axis=1, keepdims=True))
        s_ts = s_qk[h] * jnp.exp(d_log - m_t)
        inter = jnp.exp(a_log - m_t)
        r = inter * q_c[h] + jnp.dot(s_ts.astype(BF16), v_aug[h], preferred_element_type=F32)
        num = r[:, :D]
        den = r[:, D:D + 1]
        hh = num / jnp.maximum(jnp.abs(den), jnp.exp(-m_t))
        hn = hh * lax.rsqrt(jnp.mean(hh * hh, axis=-1, keepdims=True) + NORM_EPS)
        out_ref[:, hsl[h]] = (hn * gain_ref[:, hsl[h]]
                              * _sigmoid(mo_ref[:, hsl[h]].astype(F32))).astype(BF16)


def _mlstm(mq, mkT, P, gc, gt, gain):
    S = mq.shape[0]
    L = M_CHUNK
    return pl.pallas_call(
        _mlstm_kernel,
        grid=(S // L,),
        in_specs=[
            pl.BlockSpec((L, M_WIDTH), lambda c: (c, 0)),
            pl.BlockSpec((M_WIDTH, L), lambda c: (0, c)),
            pl.BlockSpec((L, M_WIDTH), lambda c: (c, COL_MV // M_WIDTH)),
            pl.BlockSpec((L, M_WIDTH), lambda c: (c, COL_MO // M_WIDTH)),
            pl.BlockSpec((L, LANES), lambda c: (c, 0)),
            pl.BlockSpec((SUBLANES, L), lambda c: (0, c)),
            pl.BlockSpec((1, M_WIDTH), lambda c: (0, 0)),
        ],
        out_specs=pl.BlockSpec((L, M_WIDTH), lambda c: (c, 0)),
        out_shape=jax.ShapeDtypeStruct((S, M_WIDTH), BF16),
        scratch_shapes=[
            pltpu.VMEM((M_HEADS, M_HEAD_DIM, M_AUG), F32),
            pltpu.VMEM((SUBLANES, LANES), F32),
        ],
        compiler_params=_cparams(("arbitrary",)),
        name="mlstm",
    )(mq, mkT, P, P, gc, gt, gain)


MOBA_HEADS_PER_GROUP = 4
ROUTE_TILES = 264
ROUTE_ROWS = ROUTE_TILES * MOBA_BLOCK
GROUP_TILES_PER_STEP = 8
ROW_WORDS = A_HEAD_DIM // 2
SC_WINDOW = 256


def _pack_rows(o_t, lse):
    q = o_t.shape[1]
    hi = pltpu.bitcast(o_t[:ROW_WORDS].astype(BF16).astype(F32), jnp.uint32)
    lo = pltpu.bitcast(o_t[ROW_WORDS:].astype(BF16).astype(F32), jnp.uint32)
    words = pltpu.bitcast(hi | (lo >> 16), F32)
    tail = jnp.concatenate([jnp.broadcast_to(lse, (SUBLANES, q)),
                            jnp.zeros((LANES - ROW_WORDS - SUBLANES, q), F32)], axis=0)
    return jnp.concatenate([words, tail], axis=0).T


def _dot_nt(a, b):
    return lax.dot_general(a, b, (((1,), (1,)), ((), ())), preferred_element_type=F32)


def _block_partial(k_blk, v_t_blk, qs, ones_rows, causal_mask=None):
    s = _dot_nt(k_blk, qs)
    if causal_mask is not None:
        s = jnp.where(causal_mask, s, NEG_BIG)
    sb = s.astype(BF16)
    m = jnp.max(sb, axis=0, keepdims=True)
    p = jnp.exp2(sb - m)
    r = jnp.dot(jnp.concatenate([v_t_blk, ones_rows], axis=0), p, preferred_element_type=F32)
    dh = v_t_blk.shape[0]
    l = r[dh:dh + 1, :]
    return r[:dh, :] / l, m.astype(F32) + jnp.log2(l)


def _route_kernel(q_ref, k_ref, vT_ref, km_ref, route_ref, cnt_ref, own_ref, cnt_acc):
    i = pl.program_id(1)
    BS = MOBA_BLOCK
    Dh = A_HEAD_DIM
    NB = km_ref.shape[0]
    G = q_ref.shape[0]
    blk = lax.broadcasted_iota(jnp.int32, (NB, BS), 0)
    kpos = lax.broadcasted_iota(jnp.int32, (BS, BS), 0)
    qpos = lax.broadcasted_iota(jnp.int32, (BS, BS), 1)
    earlier = (kpos < qpos).astype(BF16)
    ones_rows = jnp.ones((2 * SUBLANES, BS), BF16)

    @pl.when(i == 0)
    def _():
        cnt_acc[...] = jnp.zeros_like(cnt_acc)

    q = [q_ref[g] for g in range(G)]
    qs = [(q[g] * (Dh ** -0.5 * LOG2_E)).astype(BF16) for g in range(G)]
    parts = [_block_partial(k_ref[:, g * Dh:(g + 1) * Dh], vT_ref[g * Dh:(g + 1) * Dh, :], qs[g],
                            ones_rows, kpos <= qpos) for g in range(G)]

    gates = []
    for g in range(G):
        hs = slice(g * Dh, (g + 1) * Dh)
        km = km_ref[:, hs]
        kh = km.astype(BF16)
        kl = (km - kh.astype(F32)).astype(BF16)
        qh = q[g].astype(BF16)
        ql = (q[g] - qh.astype(F32)).astype(BF16)
        gate = _dot_nt(kh, qh) + _dot_nt(kh, ql) + _dot_nt(kl, qh)
        gates.append(jnp.where(blk < i, gate, -jnp.inf))
    picks = [[] for _ in range(G)]
    rows = [[] for _ in range(G)]
    for r in range(MOBA_TOPK):
        for g in range(G):
            mx = jnp.max(gates[g], axis=0, keepdims=True)
            idx = jnp.min(jnp.where(gates[g] == mx, blk, NB), axis=0, keepdims=True)
            idx = jnp.where(r < i, idx, -1)
            pick = blk == idx
            gates[g] = jnp.where(pick, -jnp.inf, gates[g])
            picks[g].append(pick)
            rows[g].append(idx)
    for g in range(G):
        onehot = sum(p.astype(F32) for p in picks[g])
        before = cnt_acc[g][:, 0:1] + jnp.dot(onehot.astype(BF16), earlier, preferred_element_type=F32)
        for r in range(MOBA_TOPK):
            rank = jnp.sum(jnp.where(picks[g][r], before, 0.0), axis=0, keepdims=True)
            rows[g].append(rank.astype(jnp.int32))
        rows[g].append(jnp.zeros((SUBLANES - 2 * MOBA_TOPK, BS), jnp.int32))
        route_ref[g] = jnp.concatenate(rows[g], axis=0)
        cnt_new = cnt_acc[g] + jnp.sum(onehot, axis=1, keepdims=True)
        cnt_acc[g] = cnt_new
        cnt_ref[g] = cnt_new
        own_ref[g] = _pack_rows(*parts[g])


def _route(aq_hm, ak, avT, kmean, h0, H):
    _, S, Dh = aq_hm.shape
    BS = MOBA_BLOCK
    NB = S // BS
    G = MOBA_HEADS_PER_GROUP
    W = G * Dh
    hb = h0 // G
    return pl.pallas_call(
        _route_kernel,
        grid=(H // G, NB),
        in_specs=[
            pl.BlockSpec((G, BS, Dh), lambda h, i: (hb + h, i, 0)),
            pl.BlockSpec((BS, W), lambda h, i: (i, hb + h)),
            pl.BlockSpec((W, BS), lambda h, i: (hb + h, i)),
            pl.BlockSpec((NB, W), lambda h, i: (0, hb + h)),
        ],
        out_specs=[
            pl.BlockSpec((G, SUBLANES, BS), lambda h, i: (h, 0, i)),
            pl.BlockSpec((G, NB, LANES), lambda h, i: (h, 0, 0)),
            pl.BlockSpec((G, BS, LANES), lambda h, i: (h, i, 0)),
        ],
        out_shape=[
            jax.ShapeDtypeStruct((H, SUBLANES, S), jnp.int32),
            jax.ShapeDtypeStruct((H, NB, LANES), F32),
            jax.ShapeDtypeStruct((H, S, LANES), F32),
        ],
        scratch_shapes=[pltpu.VMEM((G, NB, LANES), F32)],
        compiler_params=_cparams(("parallel", "arbitrary")),
        name="moba_route",
    )(aq_hm, ak, avT, kmean)


def _dest_kernel(seg_ref, route_ref, dest_ref):
    h = pl.program_id(0)
    NB = seg_ref.shape[1]
    r = route_ref[0]
    blk = r[0:MOBA_TOPK, :]
    rank = r[MOBA_TOPK:2 * MOBA_TOPK, :]
    base = h * ROUTE_ROWS
    dest = jnp.full(blk.shape, base + ROUTE_ROWS - 1, jnp.int32)
    for j in range(NB):
        dest = jnp.where(blk == j, base + seg_ref[h, j] + rank, dest)
    dest_ref[0] = jnp.concatenate(
        [dest, jnp.zeros((SUBLANES - MOBA_TOPK, dest.shape[1]), jnp.int32)], axis=0)


def _dest(seg_start, route):
    H, _, S = route.shape
    return pl.pallas_call(
        _dest_kernel,
        grid_spec=pltpu.PrefetchScalarGridSpec(
            num_scalar_prefetch=1,
            grid=(H,),
            in_specs=[pl.BlockSpec((1, SUBLANES, S), lambda h, seg: (h, 0, 0))],
            out_specs=pl.BlockSpec((1, SUBLANES, S), lambda h, seg: (h, 0, 0)),
        ),
        out_shape=jax.ShapeDtypeStruct((H, SUBLANES, S), jnp.int32),
        compiler_params=_cparams(("parallel",)),
        name="moba_dest",
    )(seg_start, route)


def _sc_mesh():
    return plsc.VectorSubcoreMesh(core_axis_name="c", subcore_axis_name="s")


def _sc_scatter(rows, row0, n_rows, idx, n_out):
    W = rows.shape[1]
    M = idx.shape[0]
    n_win = n_rows // SC_WINDOW
    win0 = row0 // SC_WINDOW

    @pl.kernel(out_type=jax.ShapeDtypeStruct((n_out, W), rows.dtype), mesh=_sc_mesh())
    def k(x_hbm, i_hbm, o_hbm):
        def body(x_vmem, i_vmem):
            pltpu.sync_copy(x_vmem, o_hbm.at[i_vmem.at[0]])

        pltpu.emit_pipeline(
            body,
            grid=(M // SC_WINDOW,),
            in_specs=[pl.BlockSpec((SC_WINDOW, W), lambda w: (win0 + lax.rem(w, n_win), 0)),
                      pl.BlockSpec((1, SC_WINDOW), lambda w: (0, w))],
            out_specs=[],
            core_axis_name=("c", "s"),
            dimension_semantics=(pltpu.PARALLEL,),
        )(x_hbm, i_hbm)

    return k(rows, idx.reshape(1, M))


def _sc_gather(table, idx):
    M = idx.shape[0]
    W = table.shape[1]

    @pl.kernel(out_type=jax.ShapeDtypeStruct((M, W), table.dtype), mesh=_sc_mesh())
    def k(x_hbm, i_hbm, o_hbm):
        def body(i_vmem, o_vmem):
            pltpu.sync_copy(x_hbm.at[i_vmem.at[0]], o_vmem)

        pltpu.emit_pipeline(
            body,
            grid=(M // SC_WINDOW,),
            in_specs=[pl.BlockSpec((1, SC_WINDOW), lambda w: (0, w))],
            out_specs=[pl.BlockSpec((SC_WINDOW, W), lambda w: (w, 0))],
            core_axis_name=("c", "s"),
            dimension_semantics=(pltpu.PARALLEL,),
        )(i_hbm, o_hbm)

    return k(table, idx.reshape(1, M))


def _group_kernel(tile_blk_ref, q_ref, k_ref, vT_ref, out_ref, s_ref, m_ref):
    h = pl.program_id(0)
    u = pl.program_id(1)
    last = pl.num_programs(1) - 2
    BS = MOBA_BLOCK
    Dh = A_HEAD_DIM
    T = GROUP_TILES_PER_STEP
    ones_rows = jnp.ones((2 * SUBLANES, BS), BF16)
    new = lax.rem(u, 2)
    old = 1 - new

    @pl.when((h == 0) & (u == 0))
    def _():
        s_ref[...] = jnp.zeros_like(s_ref)
        m_ref[...] = jnp.zeros_like(m_ref)

    def block_start(group, c):
        j = jnp.maximum(tile_blk_ref[h * ROUTE_TILES + group * T + c], 0)
        return pl.multiple_of(j * BS, BS)

    g_new = jnp.minimum(u, last)
    g_old = jnp.maximum(u - 1, 0)
    @pl.when(tile_blk_ref[h * ROUTE_TILES + g_old * T] >= 0)
    def _():
        ms, rs = [], []
        for c in range(T):
            m = m_ref[old, c]
            p = jnp.exp2(s_ref[old, c] - m.astype(BF16))
            v_aug = jnp.concatenate([vT_ref[:, pl.ds(block_start(g_old, c), BS)], ones_rows], axis=0)
            ms.append(m)
            rs.append(jnp.dot(v_aug, p, preferred_element_type=F32))
        s_new = []
        for c in range(T):
            qs = (q_ref[c * BS:(c + 1) * BS, :] * (Dh ** -0.5 * LOG2_E)).astype(BF16)
            s_new.append(_dot_nt(k_ref[pl.ds(block_start(g_new, c), BS), :], qs))
        for c in range(T):
            l = rs[c][Dh:Dh + 1, :]
            out_ref[c * BS:(c + 1) * BS, :] = _pack_rows(rs[c][:Dh, :] / l, ms[c] + jnp.log2(l))
            sb = s_new[c].astype(BF16)
            s_ref[new, c] = sb
            m_ref[new, c] = jnp.max(sb, axis=0, keepdims=True).astype(F32)


def _group(tile_blk, q_sorted, ak, avT, h0):
    S = ak.shape[0]
    H = q_sorted.shape[0] // ROUTE_ROWS
    Dh = A_HEAD_DIM
    BS = MOBA_BLOCK
    T = GROUP_TILES_PER_STEP
    steps = ROUTE_TILES // T
    rows = T * BS
    return pl.pallas_call(
        _group_kernel,
        grid_spec=pltpu.PrefetchScalarGridSpec(
            num_scalar_prefetch=1,
            grid=(H, steps + 1),
            in_specs=[
                pl.BlockSpec((rows, LANES), lambda h, u, tb: (h * steps + jnp.minimum(u, steps - 1), 0)),
                pl.BlockSpec((S, Dh), lambda h, u, tb: (0, h0 + h)),
                pl.BlockSpec((Dh, S), lambda h, u, tb: (h0 + h, 0)),
            ],
            out_specs=pl.BlockSpec((rows, LANES), lambda h, u, tb: (h * steps + jnp.maximum(u - 1, 0), 0)),
            scratch_shapes=[pltpu.VMEM((2, T, BS, BS), BF16), pltpu.VMEM((2, T, 1, BS), F32)],
        ),
        out_shape=jax.ShapeDtypeStruct((H * ROUTE_ROWS, LANES), F32),
        compiler_params=_cparams(("arbitrary", "arbitrary")),
        name="moba_group",
    )(tile_blk, q_sorted, ak, avT)


def _combine_kernel(*refs):
    i = pl.program_id(0)
    o_ref = refs[-1]
    n_groups = (len(refs) - 1) // 2
    H = refs[n_groups].shape[0]
    BS = refs[n_groups].shape[1]
    low_half = lax.broadcasted_iota(jnp.int32, (BS, LANES), 1) < ROW_WORDS
    for head in range(n_groups * H):
        got_ref, own_ref, h = refs[head // H], refs[n_groups + head // H], head % H
        tiles = [own_ref[h]]
        for r in range(MOBA_TOPK):
            tiles.append(jnp.where(r < i, got_ref[h, r], 0.0))
        lses = [tiles[0][:, ROW_WORDS:ROW_WORDS + 1]]
        lses += [jnp.where(r < i, tiles[r + 1][:, ROW_WORDS:ROW_WORDS + 1], NEG_BIG)
                 for r in range(MOBA_TOPK)]
        top = functools.reduce(jnp.maximum, lses)
        w = [jnp.exp2(x - top) for x in lses]
        inv = 1.0 / sum(w)
        first = jnp.zeros((BS, LANES), F32)
        second = jnp.zeros((BS, LANES), F32)
        for wk, tile in zip(w, tiles):
            words = pltpu.bitcast(tile, jnp.uint32)
            first = first + wk * pltpu.bitcast(words & jnp.uint32(0xFFFF0000), F32)
            second = second + wk * pltpu.bitcast(words << 16, F32)
        o = jnp.where(low_half, first, pltpu.roll(second, ROW_WORDS, 1)) * inv
        o_ref[:, head * A_HEAD_DIM:(head + 1) * A_HEAD_DIM] = o.astype(BF16)


def _combine(gots, owns):
    H, S, _ = owns[0].shape
    BS = MOBA_BLOCK
    return pl.pallas_call(
        _combine_kernel,
        grid=(S // BS,),
        in_specs=([pl.BlockSpec((H, MOBA_TOPK, BS, LANES), lambda i: (0, 0, i, 0))] * len(gots)
                  + [pl.BlockSpec((H, BS, LANES), lambda i: (0, i, 0))] * len(owns)),
        out_specs=pl.BlockSpec((BS, A_WIDTH), lambda i: (i, 0)),
        out_shape=jax.ShapeDtypeStruct((S, A_WIDTH), BF16),
        compiler_params=_cparams(("parallel",)),
        name="moba_combine",
    )(*gots, *owns)


def _moba_routed(aq_hm, ak, avT, kmean):
    H, S, Dh = aq_hm.shape
    NB = S // MOBA_BLOCK
    hn = MOBA_HEADS_PER_GROUP
    q_rows = aq_hm.reshape(H * S, Dh)
    gots, owns = [], []
    for h0 in range(0, H, hn):
        route, cnt, own = _route(aq_hm, ak, avT, kmean, h0, hn)
        cnt = cnt[:, :, 0].astype(jnp.int32)
        seg_tiles = (cnt + MOBA_BLOCK - 1) // MOBA_BLOCK
        seg_end = jnp.cumsum(seg_tiles, axis=1)
        seg_start = (seg_end - seg_tiles) * MOBA_BLOCK
        tile_ids = jnp.arange(ROUTE_TILES, dtype=jnp.int32)
        tile_blk = jnp.sum(tile_ids[None, :, None] >= seg_end[:, None, :], axis=2).astype(jnp.int32)
        tile_blk = jnp.where(tile_blk < NB, tile_blk, -1).reshape(hn * ROUTE_TILES)
        dest = _dest(seg_start, route)[:, :MOBA_TOPK, :]
        q_sorted = _sc_scatter(q_rows, h0 * S, hn * S, dest.transpose(1, 0, 2).reshape(-1),
                               hn * ROUTE_ROWS)
        results = _group(tile_blk, q_sorted, ak, avT, h0)
        gots.append(_sc_gather(results, dest.reshape(-1)).reshape(hn, MOBA_TOPK, S, LANES))
        owns.append(own)
    return _combine(gots, owns)


def _merge_kernel(hm_ref, ha_ref, gm_ref, ga_ref, x_ref, wm_ref, wa_ref, wo_ref, gain_ref,
                  out_ref):
    ym = jnp.dot(hm_ref[...], wm_ref[...], preferred_element_type=F32)
    ya = jnp.dot(ha_ref[...], wa_ref[...], preferred_element_type=F32)
    merged = _sigmoid(gm_ref[...].astype(F32)) * ym + _sigmoid(ga_ref[...].astype(F32)) * ya
    mix = jnp.dot(merged.astype(BF16), wo_ref[...], preferred_element_type=F32)
    out_ref[...] = x_ref[...] + _rms(mix, gain_ref[...])


def _merge(hm, ha, P, x, b, wm, wa, wo, gain, tm=256):
    _, S, D = x.shape
    const = pl.Buffered(1)
    return pl.pallas_call(
        _merge_kernel,
        grid=(S // tm,),
        in_specs=[
            pl.BlockSpec((tm, M_WIDTH), lambda i: (i, 0)),
            pl.BlockSpec((tm, A_WIDTH), lambda i: (i, 0)),
            pl.BlockSpec((tm, D), lambda i: (i, COL_GM // D)),
            pl.BlockSpec((tm, D), lambda i: (i, COL_GA // D)),
            pl.BlockSpec((None, tm, D), lambda i: (b, i, 0)),
            pl.BlockSpec((M_WIDTH, D), lambda i: (0, 0), pipeline_mode=const),
            pl.BlockSpec((A_WIDTH, D), lambda i: (0, 0), pipeline_mode=const),
            pl.BlockSpec((D, D), lambda i: (0, 0), pipeline_mode=const),
            pl.BlockSpec((1, D), lambda i: (0, 0)),
        ],
        out_specs=pl.BlockSpec((tm, D), lambda i: (i, 0)),
        out_shape=jax.ShapeDtypeStruct((S, D), F32),
        compiler_params=_cparams(("parallel",)),
        name="merge",
    )(hm, ha, P, P, x, wm, wa, wo, gain)


def _ffn_kernel(x_ref, gpre_ref, wu_ref, wd_ref, gpost_ref, out_ref, hn_ref, acc_ref):
    f = pl.program_id(1)

    @pl.when(f == 0)
    def _():
        hn_ref[...] = _rms(x_ref[...], gpre_ref[...]).astype(BF16)
        acc_ref[...] = jnp.zeros_like(acc_ref)

    u = jnp.dot(hn_ref[...], wu_ref[...], preferred_element_type=F32)
    u = jnp.square(jnp.maximum(u, 0.0)).astype(BF16)
    acc_ref[...] += jnp.dot(u, wd_ref[...], preferred_element_type=F32)

    @pl.when(f == pl.num_programs(1) - 1)
    def _():
        out_ref[...] = x_ref[...] + _rms(acc_ref[...], gpost_ref[...])


def _ffn(x, gpre, wu, wd, gpost, tm=512, tf=1024):
    S, D = x.shape
    Fd = wu.shape[1]
    return pl.pallas_call(
        _ffn_kernel,
        grid=(S // tm, Fd // tf),
        in_specs=[
            pl.BlockSpec((tm, D), lambda i, f: (i, 0)),
            pl.BlockSpec((1, D), lambda i, f: (0, 0)),
            pl.BlockSpec((D, tf), lambda i, f: (0, f)),
            pl.BlockSpec((tf, D), lambda i, f: (f, 0)),
            pl.BlockSpec((1, D), lambda i, f: (0, 0)),
        ],
        out_specs=pl.BlockSpec((tm, D), lambda i, f: (i, 0)),
        out_shape=jax.ShapeDtypeStruct((S, D), F32),
        scratch_shapes=[pltpu.VMEM((tm, D), BF16), pltpu.VMEM((tm, D), F32)],
        compiler_params=_cparams(("parallel", "arbitrary")),
        name="ffn",
    )(x, gpre, wu, wd, gpost)


def _layer(x, b, pos, norm_mix_pre, w_in, conv_w, conv_b, i_bias, f_bias, mlstm_norm,
           w_branch_m, w_branch_a, w_out, norm_mix_post, norm_ffn_pre, w_up, w_down,
           norm_ffn_post):
    _, S, D = x.shape
    o = 0
    pieces = {}
    w_in = w_in.astype(BF16)
    for name, width in (("mq", M_WIDTH), ("mk", M_WIDTH), ("mv", M_WIDTH), ("mo", M_WIDTH),
                        ("mi", M_HEADS), ("mf", M_HEADS), ("aq", A_WIDTH), ("ak", A_WIDTH),
                        ("av", A_WIDTH), ("gm", D), ("ga", D)):
        pieces[name] = w_in[:, o:o + width]
        o += width
    w_all = jnp.concatenate([pieces[n] for n in ("gm", "ga", "mq", "mk", "aq", "ak", "mv", "mo", "av")],
                            axis=1)
    w_gate = jnp.concatenate(
        [pieces["mi"], pieces["mf"], jnp.zeros((D, LANES - 2 * M_HEADS), BF16)], axis=1)
    gbias = jnp.concatenate([i_bias, f_bias, jnp.zeros((LANES - 2 * M_HEADS,), F32)])[None, :]
    half = jnp.arange(0, A_HEAD_DIM, 2, dtype=F32) / A_HEAD_DIM
    inv_freq = 1.0 / (ROPE_THETA ** half)
    invf = jnp.concatenate([inv_freq, inv_freq])[None, :]

    P, gate = _proj(x, b, norm_mix_pre[None, :], w_all, w_gate)
    mq, mkT, aq, ak, avT, kmean, gc, gt = _prep(
        P, gate, pos.reshape(S, 1), invf, conv_w, conv_b[None, :], gbias)
    hm = _mlstm(mq, mkT, P, gc, gt, mlstm_norm[None, :])
    ha = _moba_routed(aq, ak, avT, kmean.reshape(S // MOBA_BLOCK, A_WIDTH))
    x1 = _merge(hm, ha, P, x, b, w_branch_m.astype(BF16), w_branch_a.astype(BF16),
                w_out.astype(BF16), norm_mix_post[None, :])
    return _ffn(x1, norm_ffn_pre[None, :], w_up.astype(BF16), w_down.astype(BF16),
                norm_ffn_post[None, :])


def kernel(x, positions, norm_mix_pre, w_in, conv_w, conv_b, i_bias, f_bias, mlstm_norm,
           w_branch_m, w_branch_a, w_out, norm_mix_post, norm_ffn_pre, w_up, w_down,
           norm_ffn_post):
    B = x.shape[0]
    depth = w_in.shape[0]
    outs = []
    def take(a, i):
        return a.reshape(a.shape[1:]) if a.shape[0] == 1 else a[i]

    for b in range(B):
        xin, bi = x, b
        for l in range(depth):
            xb = _layer(xin, bi, take(positions, b), take(norm_mix_pre, l), take(w_in, l),
                        take(conv_w, l), take(conv_b, l), take(i_bias, l), take(f_bias, l),
                        take(mlstm_norm, l), take(w_branch_m, l), take(w_branch_a, l),
                        take(w_out, l), take(norm_mix_post, l), take(norm_ffn_pre, l),
                        take(w_up, l), take(w_down, l), take(norm_ffn_post, l))
            xin, bi = xb[None], 0
        outs.append(xb)
    return outs[0][None] if B == 1 else jnp.stack(outs, axis=0)
```

```python
import functools

import jax
import jax.numpy as jnp
from jax import lax
from jax.experimental import pallas as pl
from jax.experimental.pallas import tpu as pltpu
from jax.experimental.pallas import tpu_sc as plsc

F32 = jnp.float32
BF16 = jnp.bfloat16

M_HEADS = 4
M_HEAD_DIM = 256
M_WIDTH = M_HEADS * M_HEAD_DIM
M_CHUNK = 128
CONV_WIDTH = 4
A_HEADS = 8
A_HEAD_DIM = 128
A_WIDTH = A_HEADS * A_HEAD_DIM
MOBA_BLOCK = 256
MOBA_TOPK = 3
ROPE_THETA = 10000.0
NORM_EPS = 1e-6

LANES = 128
SUBLANES = 8
VMEM_LIMIT = 56 * 1024 * 1024
NEG_BIG = -1e30
LOG2_E = 1.4426950408889634

COL_GM = 0
COL_GA = 2048
COL_MQK = 4096
COL_AQK = 6144
COL_MV = 8192
COL_MO = 9216
COL_AV = 10240


def _cparams(sem):
    return pltpu.CompilerParams(dimension_semantics=sem, vmem_limit_bytes=VMEM_LIMIT)


def _rms(x, gain):
    ms = jnp.mean(x * x, axis=-1, keepdims=True)
    return x * lax.rsqrt(ms + NORM_EPS) * gain


def _sigmoid(x):
    return 1.0 / (1.0 + jnp.exp(-x))


def _split3(x):
    hi = x.astype(BF16)
    r1 = x - hi.astype(F32)
    mid = r1.astype(BF16)
    lo = (r1 - mid.astype(F32)).astype(BF16)
    return hi, mid, lo


def _proj_kernel(x_ref, g_ref, w_ref, wg_ref, p_ref, gate_ref, xn_ref):
    @pl.when(pl.program_id(1) == 0)
    def _():
        xn = _rms(x_ref[...], g_ref[...]).astype(BF16)
        xn_ref[...] = xn
        gate_ref[...] = jnp.dot(xn, wg_ref[...], preferred_element_type=F32)

    p_ref[...] = jnp.dot(xn_ref[...], w_ref[...], preferred_element_type=F32).astype(BF16)


def _proj(x, b, gain, w_all, w_gate, tm=1024, tn=1024):
    _, S, D = x.shape
    N = w_all.shape[1]
    return pl.pallas_call(
        _proj_kernel,
        grid=(S // tm, N // tn),
        in_specs=[
            pl.BlockSpec((None, tm, D), lambda i, j: (b, i, 0)),
            pl.BlockSpec((1, D), lambda i, j: (0, 0)),
            pl.BlockSpec((D, tn), lambda i, j: (0, j)),
            pl.BlockSpec((D, LANES), lambda i, j: (0, 0)),
        ],
        out_specs=[
            pl.BlockSpec((tm, tn), lambda i, j: (i, j)),
            pl.BlockSpec((tm, LANES), lambda i, j: (i, 0)),
        ],
        out_shape=[
            jax.ShapeDtypeStruct((S, N), BF16),
            jax.ShapeDtypeStruct((S, LANES), F32),
        ],
        scratch_shapes=[pltpu.VMEM((tm, D), BF16)],
        compiler_params=_cparams(("parallel", "arbitrary")),
        name="proj",
    )(x, gain, w_all, w_gate)


PREP_ROWS = MOBA_BLOCK
PREP_COLS = 512
HALO_ROWS = 2 * SUBLANES


def _prep_kernel(pmk_ref, halo_ref, paqk_ref, pav_ref, gate_ref, pos_ref, invf_ref,
                 cw_ref, cb_ref, gb_ref,
                 mq_ref, mkT_ref, aq_ref, ak_ref, avT_ref, kmean_ref, gc_ref, gt_ref):
    i = pl.program_id(0)
    R = PREP_ROWS

    k_scale = M_HEAD_DIM ** -0.5
    for c0 in range(0, 2 * M_WIDTH, PREP_COLS):
        cs = slice(c0, c0 + PREP_COLS)
        prev = halo_ref[:, cs].astype(F32)[HALO_ROWS - SUBLANES:, :]
        prev = jnp.where(i == 0, jnp.zeros_like(prev), prev)
        ext = jnp.concatenate([pmk_ref[:, cs].astype(F32), prev], axis=0)
        acc = cw_ref[0:1, cs] * ext
        for j in range(1, CONV_WIDTH):
            acc = pltpu.roll(acc, 1, 0) + cw_ref[j:j + 1, cs] * ext
        acc = acc[0:R, :] + cb_ref[:, cs]
        y = acc * _sigmoid(acc)
        if c0 < M_WIDTH:
            mq_ref[:, cs] = y.astype(BF16)
        else:
            ks = slice(c0 - M_WIDTH, c0 - M_WIDTH + PREP_COLS)
            mkT_ref[ks, :] = (y * k_scale).T.astype(BF16)

    ang = pos_ref[...].astype(F32) * invf_ref[...]
    cos = jnp.cos(ang)
    lane = lax.broadcasted_iota(jnp.int32, (R, A_HEAD_DIM), 1)
    sin_signed = jnp.where(lane < A_HEAD_DIM // 2, -1.0, 1.0) * jnp.sin(ang)
    for h in range(2 * A_HEADS):
        hs = slice(h * A_HEAD_DIM, (h + 1) * A_HEAD_DIM)
        xh = paqk_ref[:, hs].astype(F32)
        yh = xh * cos + pltpu.roll(xh, A_HEAD_DIM // 2, 1) * sin_signed
        if h < A_HEADS:
            aq_ref[h] = yh
        else:
            ko = slice((h - A_HEADS) * A_HEAD_DIM, (h - A_HEADS + 1) * A_HEAD_DIM)
            ak_ref[:, ko] = yh.astype(BF16)
            kmean_ref[0, :, ko] = jnp.mean(yh, axis=0, keepdims=True)
    for c0 in range(0, A_WIDTH, PREP_COLS):
        avT_ref[c0:c0 + PREP_COLS, :] = pav_ref[:, c0:c0 + PREP_COLS].astype(F32).T.astype(BF16)

    g = gate_ref[...] + gb_ref[...]
    log_f = jnp.minimum(g, 0.0) - jnp.log1p(jnp.exp(-jnp.abs(g)))
    r_i = lax.broadcasted_iota(jnp.int32, (R, R), 0)
    c_i = lax.broadcasted_iota(jnp.int32, (R, R), 1)
    tri = ((r_i >= c_i) & ((r_i // M_CHUNK) == (c_i // M_CHUNK))).astype(BF16)
    hi, mid, lo = _split3(log_f)
    csum = (jnp.dot(tri, hi, preferred_element_type=F32)
            + jnp.dot(tri, mid, preferred_element_type=F32)
            + jnp.dot(tri, lo, preferred_element_type=F32))
    glane = lax.broadcasted_iota(jnp.int32, (R, LANES), 1)
    gc = jnp.where(glane < M_HEADS, g, csum)
    gc_ref[...] = gc
    gt_ref[...] = gc.T[0:SUBLANES, :]


def _prep(P, gate, pos, invf, conv_w, conv_b, gbias):
    S = P.shape[0]
    R = PREP_ROWS
    nb = S // R
    halo_blocks = R // HALO_ROWS
    return pl.pallas_call(
        _prep_kernel,
        grid=(nb,),
        in_specs=[
            pl.BlockSpec((R, 2 * M_WIDTH), lambda i: (i, COL_MQK // (2 * M_WIDTH))),
            pl.BlockSpec((HALO_ROWS, 2 * M_WIDTH),
                         lambda i: (jnp.maximum(i * halo_blocks - 1, 0), COL_MQK // (2 * M_WIDTH))),
            pl.BlockSpec((R, 2 * A_WIDTH), lambda i: (i, COL_AQK // (2 * A_WIDTH))),
            pl.BlockSpec((R, A_WIDTH), lambda i: (i, COL_AV // A_WIDTH)),
            pl.BlockSpec((R, LANES), lambda i: (i, 0)),
            pl.BlockSpec((R, 1), lambda i: (i, 0)),
            pl.BlockSpec((1, A_HEAD_DIM), lambda i: (0, 0)),
            pl.BlockSpec((CONV_WIDTH, 2 * M_WIDTH), lambda i: (0, 0)),
            pl.BlockSpec((1, 2 * M_WIDTH), lambda i: (0, 0)),
            pl.BlockSpec((1, LANES), lambda i: (0, 0)),
        ],
        out_specs=[
            pl.BlockSpec((R, M_WIDTH), lambda i: (i, 0)),
            pl.BlockSpec((M_WIDTH, R), lambda i: (0, i)),
            pl.BlockSpec((A_HEADS, R, A_HEAD_DIM), lambda i: (0, i, 0)),
            pl.BlockSpec((R, A_WIDTH), lambda i: (i, 0)),
            pl.BlockSpec((A_WIDTH, R), lambda i: (0, i)),
            pl.BlockSpec((1, 1, A_WIDTH), lambda i: (i, 0, 0)),
            pl.BlockSpec((R, LANES), lambda i: (i, 0)),
            pl.BlockSpec((SUBLANES, R), lambda i: (0, i)),
        ],
        out_shape=[
            jax.ShapeDtypeStruct((S, M_WIDTH), BF16),
            jax.ShapeDtypeStruct((M_WIDTH, S), BF16),
            jax.ShapeDtypeStruct((A_HEADS, S, A_HEAD_DIM), F32),
            jax.ShapeDtypeStruct((S, A_WIDTH), BF16),
            jax.ShapeDtypeStruct((A_WIDTH, S), BF16),
            jax.ShapeDtypeStruct((nb, 1, A_WIDTH), F32),
            jax.ShapeDtypeStruct((S, LANES), F32),
            jax.ShapeDtypeStruct((SUBLANES, S), F32),
        ],
        compiler_params=_cparams(("parallel",)),
        name="prep",
    )(P, P, P, P, gate, pos, invf, conv_w, conv_b, gbias)


M_AUG = M_HEAD_DIM + LANES
MLSTM_CHUNKS_PER_STEP = 1


def _mlstm_kernel(q_ref, kT_ref, v_ref, mo_ref, gc_ref, gt_ref, gain_ref, out_ref,
                  c_ref, m_ref):
    @pl.when(pl.program_id(0) == 0)
    def _():
        c_ref[...] = jnp.zeros_like(c_ref)
        m_ref[...] = jnp.zeros_like(m_ref)

    L = M_CHUNK
    D = M_HEAD_DIM
    row = lax.broadcasted_iota(jnp.int32, (L, L), 0)
    col = lax.broadcasted_iota(jnp.int32, (L, L), 1)
    causal = row >= col
    ones_col = (lax.broadcasted_iota(jnp.int32, (L, LANES), 1) == 0).astype(BF16)

    heads = range(M_HEADS)
    hsl = [slice(h * D, (h + 1) * D) for h in heads]

    def state_stages(rs):
        q = [q_ref[rs, hsl[h]] for h in heads]
        kT = [kT_ref[hsl[h], rs] for h in heads]
        v_aug = [jnp.concatenate([v_ref[rs, hsl[h]], ones_col], axis=1) for h in heads]
        b_c = [gc_ref[rs, M_HEADS + h:M_HEADS + h + 1] for h in heads]
        b_r = [gt_ref[M_HEADS + h:M_HEADS + h + 1, rs] for h in heads]
        u_r = [gt_ref[h:h + 1, rs] - b_r[h] for h in heads]
        f_tot = [b_r[h][:, L - 1:L] for h in heads]
        m_prev = [m_ref[h:h + 1, 0:1] for h in heads]

        s_qk = [jnp.dot(q[h], kT[h], preferred_element_type=F32) for h in heads]
        q_c = [jnp.dot(q[h], c_ref[h].astype(BF16), preferred_element_type=F32) for h in heads]

        for h in heads:
            w_r = f_tot[h] + u_r[h]
            m_loc = jnp.max(w_r, axis=1, keepdims=True)
            m_new = jnp.maximum(f_tot[h] + m_prev[h], m_loc)
            a = jnp.exp(f_tot[h] + m_prev[h] - m_new)
            e_r = jnp.exp(w_r - m_new)
            keT = (kT[h].astype(F32) * e_r).astype(BF16)
            c_ref[h] = a * c_ref[h] + jnp.dot(keT, v_aug[h], preferred_element_type=F32)
            m_ref[h:h + 1, :] = jnp.broadcast_to(m_new, (1, LANES))
        return rs, v_aug, b_c, u_r, m_prev, s_qk, q_c

    def output_stage(rs, v_aug, b_c, u_r, m_prev, s_qk, q_c):
        for h in heads:
            d_log = jnp.where(causal, b_c[h] + u_r[h], -jnp.inf)
            a_log = b_c[h] + m_prev[h]
            m_t = jnp.maximum(a_log, jnp.max(d_log, axis=1, keepdims=True))
            s_ts = s_qk[h] * jnp.exp(d_log - m_t)
            inter = jnp.exp(a_log - m_t)
            r = inter * q_c[h] + jnp.dot(s_ts.astype(BF16), v_aug[h], preferred_element_type=F32)
            num = r[:, :D]
            den = r[:, D:D + 1]
            hh = num / jnp.maximum(jnp.abs(den), jnp.exp(-m_t))
            hn = hh * lax.rsqrt(jnp.mean(hh * hh, axis=-1, keepdims=True) + NORM_EPS)
            out_ref[rs, hsl[h]] = (hn * gain_ref[:, hsl[h]]
                                   * _sigmoid(mo_ref[rs, hsl[h]].astype(F32))).astype(BF16)

    pending = [state_stages(slice(c * L, (c + 1) * L)) for c in range(MLSTM_CHUNKS_PER_STEP)]
    for vals in pending:
        output_stage(*vals)


def _mlstm(mq, mkT, P, gc, gt, gain):
    S = mq.shape[0]
    L = M_CHUNK * MLSTM_CHUNKS_PER_STEP
    return pl.pallas_call(
        _mlstm_kernel,
        grid=(S // L,),
        in_specs=[
            pl.BlockSpec((L, M_WIDTH), lambda c: (c, 0)),
            pl.BlockSpec((M_WIDTH, L), lambda c: (0, c)),
            pl.BlockSpec((L, M_WIDTH), lambda c: (c, COL_MV // M_WIDTH)),
            pl.BlockSpec((L, M_WIDTH), lambda c: (c, COL_MO // M_WIDTH)),
            pl.BlockSpec((L, LANES), lambda c: (c, 0)),
            pl.BlockSpec((SUBLANES, L), lambda c: (0, c)),
            pl.BlockSpec((1, M_WIDTH), lambda c: (0, 0)),
        ],
        out_specs=pl.BlockSpec((L, M_WIDTH), lambda c: (c, 0)),
        out_shape=jax.ShapeDtypeStruct((S, M_WIDTH), BF16),
        scratch_shapes=[
            pltpu.VMEM((M_HEADS, M_HEAD_DIM, M_AUG), F32),
            pltpu.VMEM((SUBLANES, LANES), F32),
        ],
        compiler_params=_cparams(("arbitrary",)),
        name="mlstm",
    )(mq, mkT, P, P, gc, gt, gain)


MOBA_HEADS_PER_GROUP = 4
ROUTE_TILES = 264
ROUTE_ROWS = ROUTE_TILES * MOBA_BLOCK
GROUP_TILES_PER_STEP = 12
ROW_WORDS = A_HEAD_DIM // 2
SC_WINDOW = 256


def _pack_rows(o_t, lse):
    q = o_t.shape[1]
    hi = pltpu.bitcast(o_t[:ROW_WORDS].astype(BF16).astype(F32), jnp.uint32)
    lo = pltpu.bitcast(o_t[ROW_WORDS:].astype(BF16).astype(F32), jnp.uint32)
    words = pltpu.bitcast(hi | (lo >> 16), F32)
    tail = jnp.concatenate([jnp.broadcast_to(lse, (SUBLANES, q)),
                            jnp.zeros((LANES - ROW_WORDS - SUBLANES, q), F32)], axis=0)
    return jnp.concatenate([words, tail], axis=0).T


def _dot_nt(a, b):
    return lax.dot_general(a, b, (((1,), (1,)), ((), ())), preferred_element_type=F32)


def _block_partial(k_blk, v_t_blk, qs, ones_rows, causal_mask=None):
    s = _dot_nt(k_blk, qs)
    if causal_mask is not None:
        s = jnp.where(causal_mask, s, NEG_BIG)
    sb = s.astype(BF16)
    m = jnp.max(sb, axis=0, keepdims=True)
    p = jnp.exp2(sb - m)
    r = jnp.dot(jnp.concatenate([v_t_blk, ones_rows], axis=0), p, preferred_element_type=F32)
    dh = v_t_blk.shape[0]
    l = r[dh:dh + 1, :]
    return r[:dh, :] / l, m.astype(F32) + jnp.log2(l)


def _route_kernel(q_ref, k_ref, vT_ref, km_ref, route_ref, cnt_ref, own_ref, cnt_acc):
    i = pl.program_id(1)
    BS = MOBA_BLOCK
    Dh = A_HEAD_DIM
    NB = km_ref.shape[0]
    G = q_ref.shape[0]
    blk = lax.broadcasted_iota(jnp.int32, (NB, BS), 0)
    kpos = lax.broadcasted_iota(jnp.int32, (BS, BS), 0)
    qpos = lax.broadcasted_iota(jnp.int32, (BS, BS), 1)
    earlier = (kpos < qpos).astype(BF16)
    ones_rows = jnp.ones((2 * SUBLANES, BS), BF16)

    @pl.when(i == 0)
    def _():
        cnt_acc[...] = jnp.zeros_like(cnt_acc)

    q = [q_ref[g] for g in range(G)]
    qs = [(q[g] * (Dh ** -0.5 * LOG2_E)).astype(BF16) for g in range(G)]
    parts = [_block_partial(k_ref[:, g * Dh:(g + 1) * Dh], vT_ref[g * Dh:(g + 1) * Dh, :], qs[g],
                            ones_rows, kpos <= qpos) for g in range(G)]

    gates = []
    for g in range(G):
        hs = slice(g * Dh, (g + 1) * Dh)
        km = km_ref[:, hs]
        kh = km.astype(BF16)
        kl = (km - kh.astype(F32)).astype(BF16)
        qh = q[g].astype(BF16)
        ql = (q[g] - qh.astype(F32)).astype(BF16)
        gate = _dot_nt(kh, qh) + _dot_nt(kh, ql) + _dot_nt(kl, qh)
        gates.append(jnp.where(blk < i, gate, -jnp.inf))
    picks = [[] for _ in range(G)]
    rows = [[] for _ in range(G)]
    for r in range(MOBA_TOPK):
        for g in range(G):
            mx = jnp.max(gates[g], axis=0, keepdims=True)
            idx = jnp.min(jnp.where(gates[g] == mx, blk, NB), axis=0, keepdims=True)
            idx = jnp.where(r < i, idx, -1)
            pick = blk == idx
            gates[g] = jnp.where(pick, -jnp.inf, gates[g])
            picks[g].append(pick)
            rows[g].append(idx)
    for g in range(G):
        onehot = sum(p.astype(F32) for p in picks[g])
        before = cnt_acc[g][:, 0:1] + jnp.dot(onehot.astype(BF16), earlier, preferred_element_type=F32)
        for r in range(MOBA_TOPK):
            rank = jnp.sum(jnp.where(picks[g][r], before, 0.0), axis=0, keepdims=True)
            rows[g].append(rank.astype(jnp.int32))
        rows[g].append(jnp.zeros((SUBLANES - 2 * MOBA_TOPK, BS), jnp.int32))
        route_ref[g] = jnp.concatenate(rows[g], axis=0)
        cnt_new = cnt_acc[g] + jnp.sum(onehot, axis=1, keepdims=True)
        cnt_acc[g] = cnt_new
        cnt_ref[g] = cnt_new
        own_ref[g] = _pack_rows(*parts[g])


def _route(aq_hm, ak, avT, kmean, h0, H):
    _, S, Dh = aq_hm.shape
    BS = MOBA_BLOCK
    NB = S // BS
    G = MOBA_HEADS_PER_GROUP
    W = G * Dh
    hb = h0 // G
    return pl.pallas_call(
        _route_kernel,
        grid=(H // G, NB),
        in_specs=[
            pl.BlockSpec((G, BS, Dh), lambda h, i: (hb + h, i, 0)),
            pl.BlockSpec((BS, W), lambda h, i: (i, hb + h)),
            pl.BlockSpec((W, BS), lambda h, i: (hb + h, i)),
            pl.BlockSpec((NB, W), lambda h, i: (0, hb + h)),
        ],
        out_specs=[
            pl.BlockSpec((G, SUBLANES, BS), lambda h, i: (h, 0, i)),
            pl.BlockSpec((G, NB, LANES), lambda h, i: (h, 0, 0)),
            pl.BlockSpec((G, BS, LANES), lambda h, i: (h, i, 0)),
        ],
        out_shape=[
            jax.ShapeDtypeStruct((H, SUBLANES, S), jnp.int32),
            jax.ShapeDtypeStruct((H, NB, LANES), F32),
            jax.ShapeDtypeStruct((H, S, LANES), F32),
        ],
        scratch_shapes=[pltpu.VMEM((G, NB, LANES), F32)],
        compiler_params=_cparams(("parallel", "arbitrary")),
        name="moba_route",
    )(aq_hm, ak, avT, kmean)


def _dest_kernel(seg_ref, route_ref, dest_ref):
    h = pl.program_id(0)
    NB = seg_ref.shape[1]
    r = route_ref[0]
    blk = r[0:MOBA_TOPK, :]
    rank = r[MOBA_TOPK:2 * MOBA_TOPK, :]
    base = h * ROUTE_ROWS
    dest = jnp.full(blk.shape, base + ROUTE_ROWS - 1, jnp.int32)
    for j in range(NB):
        dest = jnp.where(blk == j, base + seg_ref[h, j] + rank, dest)
    dest_ref[0] = jnp.concatenate(
        [dest, jnp.zeros((SUBLANES - MOBA_TOPK, dest.shape[1]), jnp.int32)], axis=0)


def _dest(seg_start, route):
    H, _, S = route.shape
    return pl.pallas_call(
        _dest_kernel,
        grid_spec=pltpu.PrefetchScalarGridSpec(
            num_scalar_prefetch=1,
            grid=(H,),
            in_specs=[pl.BlockSpec((1, SUBLANES, S), lambda h, seg: (h, 0, 0))],
            out_specs=pl.BlockSpec((1, SUBLANES, S), lambda h, seg: (h, 0, 0)),
        ),
        out_shape=jax.ShapeDtypeStruct((H, SUBLANES, S), jnp.int32),
        compiler_params=_cparams(("parallel",)),
        name="moba_dest",
    )(seg_start, route)


def _sc_mesh():
    return plsc.VectorSubcoreMesh(core_axis_name="c", subcore_axis_name="s")


def _sc_scatter(rows, row0, n_rows, idx, n_out):
    W = rows.shape[1]
    M = idx.shape[0]
    n_win = n_rows // SC_WINDOW
    win0 = row0 // SC_WINDOW

    @pl.kernel(out_type=jax.ShapeDtypeStruct((n_out, W), rows.dtype), mesh=_sc_mesh())
    def k(x_hbm, i_hbm, o_hbm):
        def body(x_vmem, i_vmem):
            pltpu.sync_copy(x_vmem, o_hbm.at[i_vmem.at[0]])

        pltpu.emit_pipeline(
            body,
            grid=(M // SC_WINDOW,),
            in_specs=[pl.BlockSpec((SC_WINDOW, W), lambda w: (win0 + lax.rem(w, n_win), 0)),
                      pl.BlockSpec((1, SC_WINDOW), lambda w: (0, w))],
            out_specs=[],
            core_axis_name=("c", "s"),
            dimension_semantics=(pltpu.PARALLEL,),
        )(x_hbm, i_hbm)

    return k(rows, idx.reshape(1, M))


def _sc_gather(table, idx):
    M = idx.shape[0]
    W = table.shape[1]

    @pl.kernel(out_type=jax.ShapeDtypeStruct((M, W), table.dtype), mesh=_sc_mesh())
    def k(x_hbm, i_hbm, o_hbm):
        def body(i_vmem, o_vmem):
            pltpu.sync_copy(x_hbm.at[i_vmem.at[0]], o_vmem)

        pltpu.emit_pipeline(
            body,
            grid=(M // SC_WINDOW,),
            in_specs=[pl.BlockSpec((1, SC_WINDOW), lambda w: (0, w))],
            out_specs=[pl.BlockSpec((SC_WINDOW, W), lambda w: (w, 0))],
            core_axis_name=("c", "s"),
            dimension_semantics=(pltpu.PARALLEL,),
        )(i_hbm, o_hbm)

    return k(table, idx.reshape(1, M))


def _group_kernel(tile_blk_ref, q_ref, k_ref, vT_ref, out_ref, s_ref, m_ref):
    h = pl.program_id(0)
    u = pl.program_id(1)
    last = pl.num_programs(1) - 2
    BS = MOBA_BLOCK
    Dh = A_HEAD_DIM
    T = GROUP_TILES_PER_STEP
    ones_rows = jnp.ones((2 * SUBLANES, BS), BF16)
    new = lax.rem(u, 2)
    old = 1 - new

    @pl.when((h == 0) & (u == 0))
    def _():
        s_ref[...] = jnp.zeros_like(s_ref)
        m_ref[...] = jnp.zeros_like(m_ref)

    def block_start(group, c):
        j = jnp.maximum(tile_blk_ref[h * ROUTE_TILES + group * T + c], 0)
        return pl.multiple_of(j * BS, BS)

    g_new = jnp.minimum(u, last)
    g_old = jnp.maximum(u - 1, 0)
    @pl.when(tile_blk_ref[h * ROUTE_TILES + g_old * T] >= 0)
    def _():
        ms, rs = [], []
        for c in range(T):
            m = m_ref[old, c]
            p = jnp.exp2(s_ref[old, c] - m.astype(BF16))
            v_aug = jnp.concatenate([vT_ref[:, pl.ds(block_start(g_old, c), BS)], ones_rows], axis=0)
            ms.append(m)
            rs.append(jnp.dot(v_aug, p, preferred_element_type=F32))
        s_new = []
        for c in range(T):
            qs = (q_ref[c * BS:(c + 1) * BS, :] * (Dh ** -0.5 * LOG2_E)).astype(BF16)
            s_new.append(_dot_nt(k_ref[pl.ds(block_start(g_new, c), BS), :], qs))
        for c in range(T):
            l = rs[c][Dh:Dh + 1, :]
            out_ref[c * BS:(c + 1) * BS, :] = _pack_rows(rs[c][:Dh, :] / l, ms[c] + jnp.log2(l))
            sb = s_new[c].astype(BF16)
            s_ref[new, c] = sb
            m_ref[new, c] = jnp.max(sb, axis=0, keepdims=True).astype(F32)


def _group(tile_blk, q_sorted, ak, avT, h0):
    S = ak.shape[0]
    H = q_sorted.shape[0] // ROUTE_ROWS
    Dh = A_HEAD_DIM
    BS = MOBA_BLOCK
    T = GROUP_TILES_PER_STEP
    steps = ROUTE_TILES // T
    rows = T * BS
    return pl.pallas_call(
        _group_kernel,
        grid_spec=pltpu.PrefetchScalarGridSpec(
            num_scalar_prefetch=1,
            grid=(H, steps + 1),
            in_specs=[
                pl.BlockSpec((rows, LANES), lambda h, u, tb: (h * steps + jnp.minimum(u, steps - 1), 0)),
                pl.BlockSpec((S, Dh), lambda h, u, tb: (0, h0 + h)),
                pl.BlockSpec((Dh, S), lambda h, u, tb: (h0 + h, 0)),
            ],
            out_specs=pl.BlockSpec((rows, LANES), lambda h, u, tb: (h * steps + jnp.maximum(u - 1, 0), 0)),
            scratch_shapes=[pltpu.VMEM((2, T, BS, BS), BF16), pltpu.VMEM((2, T, 1, BS), F32)],
        ),
        out_shape=jax.ShapeDtypeStruct((H * ROUTE_ROWS, LANES), F32),
        compiler_params=_cparams(("arbitrary", "arbitrary")),
        name="moba_group",
    )(tile_blk, q_sorted, ak, avT)


def _combine_kernel(*refs):
    i = pl.program_id(0)
    o_ref = refs[-1]
    n_groups = (len(refs) - 1) // 2
    H = refs[n_groups].shape[0]
    BS = refs[n_groups].shape[1]
    low_half = lax.broadcasted_iota(jnp.int32, (BS, LANES), 1) < ROW_WORDS
    for head in range(n_groups * H):
        got_ref, own_ref, h = refs[head // H], refs[n_groups + head // H], head % H
        tiles = [own_ref[h]]
        for r in range(MOBA_TOPK):
            tiles.append(jnp.where(r < i, got_ref[h, r], 0.0))
        lses = [tiles[0][:, ROW_WORDS:ROW_WORDS + 1]]
        lses += [jnp.where(r < i, tiles[r + 1][:, ROW_WORDS:ROW_WORDS + 1], NEG_BIG)
                 for r in range(MOBA_TOPK)]
        top = functools.reduce(jnp.maximum, lses)
        w = [jnp.exp2(x - top) for x in lses]
        inv = 1.0 / sum(w)
        first = jnp.zeros((BS, LANES), F32)
        second = jnp.zeros((BS, LANES), F32)
        for wk, tile in zip(w, tiles):
            words = pltpu.bitcast(tile, jnp.uint32)
            first = first + wk * pltpu.bitcast(words & jnp.uint32(0xFFFF0000), F32)
            second = second + wk * pltpu.bitcast(words << 16, F32)
        o = jnp.where(low_half, first, pltpu.roll(second, ROW_WORDS, 1)) * inv
        o_ref[:, head * A_HEAD_DIM:(head + 1) * A_HEAD_DIM] = o.astype(BF16)


def _combine(gots, owns):
    H, S, _ = owns[0].shape
    BS = MOBA_BLOCK
    return pl.pallas_call(
        _combine_kernel,
        grid=(S // BS,),
        in_specs=([pl.BlockSpec((H, MOBA_TOPK, BS, LANES), lambda i: (0, 0, i, 0))] * len(gots)
                  + [pl.BlockSpec((H, BS, LANES), lambda i: (0, i, 0))] * len(owns)),
        out_specs=pl.BlockSpec((BS, A_WIDTH), lambda i: (i, 0)),
        out_shape=jax.ShapeDtypeStruct((S, A_WIDTH), BF16),
        compiler_params=_cparams(("parallel",)),
        name="moba_combine",
    )(*gots, *owns)


def _moba_routed(aq_hm, ak, avT, kmean):
    H, S, Dh = aq_hm.shape
    NB = S // MOBA_BLOCK
    hn = MOBA_HEADS_PER_GROUP
    q_rows = aq_hm.reshape(H * S, Dh)
    gots, owns = [], []
    for h0 in range(0, H, hn):
        route, cnt, own = _route(aq_hm, ak, avT, kmean, h0, hn)
        cnt = cnt[:, :, 0].astype(jnp.int32)
        seg_tiles = (cnt + MOBA_BLOCK - 1) // MOBA_BLOCK
        seg_end = jnp.cumsum(seg_tiles, axis=1)
        seg_start = (seg_end - seg_tiles) * MOBA_BLOCK
        tile_ids = jnp.arange(ROUTE_TILES, dtype=jnp.int32)
        tile_blk = jnp.sum(tile_ids[None, :, None] >= seg_end[:, None, :], axis=2).astype(jnp.int32)
        tile_blk = jnp.where(tile_blk < NB, tile_blk, -1).reshape(hn * ROUTE_TILES)
        dest = _dest(seg_start, route)[:, :MOBA_TOPK, :]
        q_sorted = _sc_scatter(q_rows, h0 * S, hn * S, dest.transpose(1, 0, 2).reshape(-1),
                               hn * ROUTE_ROWS)
        results = _group(tile_blk, q_sorted, ak, avT, h0)
        gots.append(_sc_gather(results, dest.reshape(-1)).reshape(hn, MOBA_TOPK, S, LANES))
        owns.append(own)
    return _combine(gots, owns)


def _merge_kernel(hm_ref, ha_ref, gm_ref, ga_ref, x_ref, wm_ref, wa_ref, wo_ref, gain_ref,
                  out_ref):
    ym = jnp.dot(hm_ref[...], wm_ref[...], preferred_element_type=F32)
    ya = jnp.dot(ha_ref[...], wa_ref[...], preferred_element_type=F32)
    merged = _sigmoid(gm_ref[...].astype(F32)) * ym + _sigmoid(ga_ref[...].astype(F32)) * ya
    mix = jnp.dot(merged.astype(BF16), wo_ref[...], preferred_element_type=F32)
    out_ref[...] = x_ref[...] + _rms(mix, gain_ref[...])


def _merge(hm, ha, P, x, b, wm, wa, wo, gain, tm=256):
    _, S, D = x.shape
    const = pl.Buffered(1)
    return pl.pallas_call(
        _merge_kernel,
        grid=(S // tm,),
        in_specs=[
            pl.BlockSpec((tm, M_WIDTH), lambda i: (i, 0)),
            pl.BlockSpec((tm, A_WIDTH), lambda i: (i, 0)),
            pl.BlockSpec((tm, D), lambda i: (i, COL_GM // D)),
            pl.BlockSpec((tm, D), lambda i: (i, COL_GA // D)),
            pl.BlockSpec((None, tm, D), lambda i: (b, i, 0)),
            pl.BlockSpec((M_WIDTH, D), lambda i: (0, 0), pipeline_mode=const),
            pl.BlockSpec((A_WIDTH, D), lambda i: (0, 0), pipeline_mode=const),
            pl.BlockSpec((D, D), lambda i: (0, 0), pipeline_mode=const),
            pl.BlockSpec((1, D), lambda i: (0, 0)),
        ],
        out_specs=pl.BlockSpec((tm, D), lambda i: (i, 0)),
        out_shape=jax.ShapeDtypeStruct((S, D), F32),
        compiler_params=_cparams(("parallel",)),
        name="merge",
    )(hm, ha, P, P, x, wm, wa, wo, gain)


def _ffn_kernel(x_ref, gpre_ref, wu_ref, wd_ref, gpost_ref, out_ref, hn_ref, acc_ref):
    f = pl.program_id(1)

    @pl.when(f == 0)
    def _():
        hn_ref[...] = _rms(x_ref[...], gpre_ref[...]).astype(BF16)
        acc_ref[...] = jnp.zeros_like(acc_ref)

    u = jnp.dot(hn_ref[...], wu_ref[...], preferred_element_type=F32)
    u = jnp.square(jnp.maximum(u, 0.0)).astype(BF16)
    acc_ref[...] += jnp.dot(u, wd_ref[...], preferred_element_type=F32)

    @pl.when(f == pl.num_programs(1) - 1)
    def _():
        out_ref[...] = x_ref[...] + _rms(acc_ref[...], gpost_ref[...])


def _ffn(x, gpre, wu, wd, gpost, tm=512, tf=1024):
    S, D = x.shape
    Fd = wu.shape[1]
    return pl.pallas_call(
        _ffn_kernel,
        grid=(S // tm, Fd // tf),
        in_specs=[
            pl.BlockSpec((tm, D), lambda i, f: (i, 0)),
            pl.BlockSpec((1, D), lambda i, f: (0, 0)),
            pl.BlockSpec((D, tf), lambda i, f: (0, f)),
            pl.BlockSpec((tf, D), lambda i, f: (f, 0)),
            pl.BlockSpec((1, D), lambda i, f: (0, 0)),
        ],
        out_specs=pl.BlockSpec((tm, D), lambda i, f: (i, 0)),
        out_shape=jax.ShapeDtypeStruct((S, D), F32),
        scratch_shapes=[pltpu.VMEM((tm, D), BF16), pltpu.VMEM((tm, D), F32)],
        compiler_params=_cparams(("parallel", "arbitrary")),
        name="ffn",
    )(x, gpre, wu, wd, gpost)


def _layer(x, b, pos, norm_mix_pre, w_in, conv_w, conv_b, i_bias, f_bias, mlstm_norm,
           w_branch_m, w_branch_a, w_out, norm_mix_post, norm_ffn_pre, w_up, w_down,
           norm_ffn_post):
    _, S, D = x.shape
    o = 0
    pieces = {}
    w_in = w_in.astype(BF16)
    for name, width in (("mq", M_WIDTH), ("mk", M_WIDTH), ("mv", M_WIDTH), ("mo", M_WIDTH),
                        ("mi", M_HEADS), ("mf", M_HEADS), ("aq", A_WIDTH), ("ak", A_WIDTH),
                        ("av", A_WIDTH), ("gm", D), ("ga", D)):
        pieces[name] = w_in[:, o:o + width]
        o += width
    w_all = jnp.concatenate([pieces[n] for n in ("gm", "ga", "mq", "mk", "aq", "ak", "mv", "mo", "av")],
                            axis=1)
    w_gate = jnp.concatenate(
        [pieces["mi"], pieces["mf"], jnp.zeros((D, LANES - 2 * M_HEADS), BF16)], axis=1)
    gbias = jnp.concatenate([i_bias, f_bias, jnp.zeros((LANES - 2 * M_HEADS,), F32)])[None, :]
    half = jnp.arange(0, A_HEAD_DIM, 2, dtype=F32) / A_HEAD_DIM
    inv_freq = 1.0 / (ROPE_THETA ** half)
    invf = jnp.concatenate([inv_freq, inv_freq])[None, :]

    P, gate = _proj(x, b, norm_mix_pre[None, :], w_all, w_gate)
    mq, mkT, aq, ak, avT, kmean, gc, gt = _prep(
        P, gate, pos.reshape(S, 1), invf, conv_w, conv_b[None, :], gbias)
    hm = _mlstm(mq, mkT, P, gc, gt, mlstm_norm[None, :])
    ha = _moba_routed(aq, ak, avT, kmean.reshape(S // MOBA_BLOCK, A_WIDTH))
    x1 = _merge(hm, ha, P, x, b, w_branch_m.astype(BF16), w_branch_a.astype(BF16),
                w_out.astype(BF16), norm_mix_post[None, :])
    return _ffn(x1, norm_ffn_pre[None, :], w_up.astype(BF16), w_down.astype(BF16),
                norm_ffn_post[None, :])


def kernel(x, positions, norm_mix_pre, w_in, conv_w, conv_b, i_bias, f_bias, mlstm_norm,
           w_branch_m, w_branch_a, w_out, norm_mix_post, norm_ffn_pre, w_up, w_down,
           norm_ffn_post):
    B = x.shape[0]
    depth = w_in.shape[0]
    outs = []
    def take(a, i):
        return a.reshape(a.shape[1:]) if a.shape[0] == 1 else a[i]

    for b in range(B):
        xin, bi = x, b
        for l in range(depth):
            xb = _layer(xin, bi, take(positions, b), take(norm_mix_pre, l), take(w_in, l),
                        take(conv_w, l), take(conv_b, l), take(i_bias, l), take(f_bias, l),
                        take(mlstm_norm, l), take(w_branch_m, l), take(w_branch_a, l),
                        take(w_out, l), take(norm_mix_post, l), take(norm_ffn_pre, l),
                        take(w_up, l), take(w_down, l), take(norm_ffn_post, l))
            xin, bi = xb[None], 0
        outs.append(xb)
    return outs[0][None] if B == 1 else jnp.stack(outs, axis=0)
```

```python
import functools

import jax
import jax.numpy as jnp
from jax import lax
from jax.experimental import pallas as pl
from jax.experimental.pallas import tpu as pltpu
from jax.experimental.pallas import tpu_sc as plsc

F32 = jnp.float32
BF16 = jnp.bfloat16

M_HEADS = 4
M_HEAD_DIM = 256
M_WIDTH = M_HEADS * M_HEAD_DIM
M_CHUNK = 128
CONV_WIDTH = 4
A_HEADS = 8
A_HEAD_DIM = 128
A_WIDTH = A_HEADS * A_HEAD_DIM
MOBA_BLOCK = 256
MOBA_TOPK = 3
ROPE_THETA = 10000.0
NORM_EPS = 1e-6

LANES = 128
SUBLANES = 8
VMEM_LIMIT = 56 * 1024 * 1024
NEG_BIG = -1e30
LOG2_E = 1.4426950408889634

COL_GM = 0
COL_GA = 2048
COL_MQK = 4096
COL_AQK = 6144
COL_MV = 8192
COL_MO = 9216
COL_AV = 10240


def _cparams(sem):
    return pltpu.CompilerParams(dimension_semantics=sem, vmem_limit_bytes=VMEM_LIMIT)


def _rms(x, gain):
    ms = jnp.mean(x * x, axis=-1, keepdims=True)
    return x * lax.rsqrt(ms + NORM_EPS) * gain


def _sigmoid(x):
    return 1.0 / (1.0 + jnp.exp(-x))


def _split3(x):
    hi = x.astype(BF16)
    r1 = x - hi.astype(F32)
    mid = r1.astype(BF16)
    lo = (r1 - mid.astype(F32)).astype(BF16)
    return hi, mid, lo


def _proj_kernel(x_ref, g_ref, w_ref, wg_ref, p_ref, gate_ref, xn_ref):
    @pl.when(pl.program_id(1) == 0)
    def _():
        xn = _rms(x_ref[...], g_ref[...]).astype(BF16)
        xn_ref[...] = xn
        gate_ref[...] = jnp.dot(xn, wg_ref[...], preferred_element_type=F32)

    p_ref[...] = jnp.dot(xn_ref[...], w_ref[...], preferred_element_type=F32).astype(BF16)


def _proj(x, b, gain, w_all, w_gate, tm=1024, tn=1024):
    _, S, D = x.shape
    N = w_all.shape[1]
    return pl.pallas_call(
        _proj_kernel,
        grid=(S // tm, N // tn),
        in_specs=[
            pl.BlockSpec((None, tm, D), lambda i, j: (b, i, 0)),
            pl.BlockSpec((1, D), lambda i, j: (0, 0)),
            pl.BlockSpec((D, tn), lambda i, j: (0, j)),
            pl.BlockSpec((D, LANES), lambda i, j: (0, 0)),
        ],
        out_specs=[
            pl.BlockSpec((tm, tn), lambda i, j: (i, j)),
            pl.BlockSpec((tm, LANES), lambda i, j: (i, 0)),
        ],
        out_shape=[
            jax.ShapeDtypeStruct((S, N), BF16),
            jax.ShapeDtypeStruct((S, LANES), F32),
        ],
        scratch_shapes=[pltpu.VMEM((tm, D), BF16)],
        compiler_params=_cparams(("parallel", "arbitrary")),
        name="proj",
    )(x, gain, w_all, w_gate)


PREP_ROWS = MOBA_BLOCK
PREP_COLS = 512
HALO_ROWS = 2 * SUBLANES


def _prep_kernel(pmk_ref, halo_ref, paqk_ref, pav_ref, gate_ref, pos_ref, invf_ref,
                 cw_ref, cb_ref, gb_ref,
                 mq_ref, mkT_ref, aq_ref, ak_ref, avT_ref, kmean_ref, gc_ref, gt_ref):
    i = pl.program_id(0)
    R = PREP_ROWS

    k_scale = M_HEAD_DIM ** -0.5
    for c0 in range(0, 2 * M_WIDTH, PREP_COLS):
        cs = slice(c0, c0 + PREP_COLS)
        prev = halo_ref[:, cs].astype(F32)[HALO_ROWS - SUBLANES:, :]
        prev = jnp.where(i == 0, jnp.zeros_like(prev), prev)
        ext = jnp.concatenate([pmk_ref[:, cs].astype(F32), prev], axis=0)
        acc = cw_ref[0:1, cs] * ext
        for j in range(1, CONV_WIDTH):
            acc = pltpu.roll(acc, 1, 0) + cw_ref[j:j + 1, cs] * ext
        acc = acc[0:R, :] + cb_ref[:, cs]
        y = acc * _sigmoid(acc)
        if c0 < M_WIDTH:
            mq_ref[:, cs] = y.astype(BF16)
        else:
            ks = slice(c0 - M_WIDTH, c0 - M_WIDTH + PREP_COLS)
            mkT_ref[ks, :] = (y * k_scale).T.astype(BF16)

    ang = pos_ref[...].astype(F32) * invf_ref[...]
    cos = jnp.cos(ang)
    lane = lax.broadcasted_iota(jnp.int32, (R, A_HEAD_DIM), 1)
    sin_signed = jnp.where(lane < A_HEAD_DIM // 2, -1.0, 1.0) * jnp.sin(ang)
    for h in range(2 * A_HEADS):
        hs = slice(h * A_HEAD_DIM, (h + 1) * A_HEAD_DIM)
        xh = paqk_ref[:, hs].astype(F32)
        yh = xh * cos + pltpu.roll(xh, A_HEAD_DIM // 2, 1) * sin_signed
        if h < A_HEADS:
            aq_ref[h] = yh
        else:
            ko = slice((h - A_HEADS) * A_HEAD_DIM, (h - A_HEADS + 1) * A_HEAD_DIM)
            ak_ref[:, ko] = yh.astype(BF16)
            kmean_ref[0, :, ko] = jnp.mean(yh, axis=0, keepdims=True)
    for c0 in range(0, A_WIDTH, PREP_COLS):
        avT_ref[c0:c0 + PREP_COLS, :] = pav_ref[:, c0:c0 + PREP_COLS].astype(F32).T.astype(BF16)

    g = gate_ref[...] + gb_ref[...]
    log_f = jnp.minimum(g, 0.0) - jnp.log1p(jnp.exp(-jnp.abs(g)))
    r_i = lax.broadcasted_iota(jnp.int32, (R, R), 0)
    c_i = lax.broadcasted_iota(jnp.int32, (R, R), 1)
    tri = ((r_i >= c_i) & ((r_i // M_CHUNK) == (c_i // M_CHUNK))).astype(BF16)
    hi, mid, lo = _split3(log_f)
    csum = (jnp.dot(tri, hi, preferred_element_type=F32)
            + jnp.dot(tri, mid, preferred_element_type=F32)
            + jnp.dot(tri, lo, preferred_element_type=F32))
    glane = lax.broadcasted_iota(jnp.int32, (R, LANES), 1)
    gc = jnp.where(glane < M_HEADS, g, csum)
    gc_ref[...] = gc
    gt_ref[...] = gc.T[0:SUBLANES, :]


def _prep(P, gate, pos, invf, conv_w, conv_b, gbias):
    S = P.shape[0]
    R = PREP_ROWS
    nb = S // R
    halo_blocks = R // HALO_ROWS
    return pl.pallas_call(
        _prep_kernel,
        grid=(nb,),
        in_specs=[
            pl.BlockSpec((R, 2 * M_WIDTH), lambda i: (i, COL_MQK // (2 * M_WIDTH))),
            pl.BlockSpec((HALO_ROWS, 2 * M_WIDTH),
                         lambda i: (jnp.maximum(i * halo_blocks - 1, 0), COL_MQK // (2 * M_WIDTH))),
            pl.BlockSpec((R, 2 * A_WIDTH), lambda i: (i, COL_AQK // (2 * A_WIDTH))),
            pl.BlockSpec((R, A_WIDTH), lambda i: (i, COL_AV // A_WIDTH)),
            pl.BlockSpec((R, LANES), lambda i: (i, 0)),
            pl.BlockSpec((R, 1), lambda i: (i, 0)),
            pl.BlockSpec((1, A_HEAD_DIM), lambda i: (0, 0)),
            pl.BlockSpec((CONV_WIDTH, 2 * M_WIDTH), lambda i: (0, 0)),
            pl.BlockSpec((1, 2 * M_WIDTH), lambda i: (0, 0)),
            pl.BlockSpec((1, LANES), lambda i: (0, 0)),
        ],
        out_specs=[
            pl.BlockSpec((R, M_WIDTH), lambda i: (i, 0)),
            pl.BlockSpec((M_WIDTH, R), lambda i: (0, i)),
            pl.BlockSpec((A_HEADS, R, A_HEAD_DIM), lambda i: (0, i, 0)),
            pl.BlockSpec((R, A_WIDTH), lambda i: (i, 0)),
            pl.BlockSpec((A_WIDTH, R), lambda i: (0, i)),
            pl.BlockSpec((1, 1, A_WIDTH), lambda i: (i, 0, 0)),
            pl.BlockSpec((R, LANES), lambda i: (i, 0)),
            pl.BlockSpec((SUBLANES, R), lambda i: (0, i)),
        ],
        out_shape=[
            jax.ShapeDtypeStruct((S, M_WIDTH), BF16),
            jax.ShapeDtypeStruct((M_WIDTH, S), BF16),
            jax.ShapeDtypeStruct((A_HEADS, S, A_HEAD_DIM), F32),
            jax.ShapeDtypeStruct((S, A_WIDTH), BF16),
            jax.ShapeDtypeStruct((A_WIDTH, S), BF16),
            jax.ShapeDtypeStruct((nb, 1, A_WIDTH), F32),
            jax.ShapeDtypeStruct((S, LANES), F32),
            jax.ShapeDtypeStruct((SUBLANES, S), F32),
        ],
        compiler_params=_cparams(("parallel",)),
        name="prep",
    )(P, P, P, P, gate, pos, invf, conv_w, conv_b, gbias)


M_AUG = M_HEAD_DIM + LANES
MLSTM_CHUNKS_PER_STEP = 1


def _mlstm_kernel(q_ref, kT_ref, v_ref, mo_ref, gc_ref, gt_ref, gain_ref, out_ref,
                  c_ref, m_ref):
    @pl.when(pl.program_id(0) == 0)
    def _():
        c_ref[...] = jnp.zeros_like(c_ref)
        m_ref[...] = jnp.zeros_like(m_ref)

    L = M_CHUNK
    D = M_HEAD_DIM
    row = lax.broadcasted_iota(jnp.int32, (L, L), 0)
    col = lax.broadcasted_iota(jnp.int32, (L, L), 1)
    causal = row >= col
    ones_col = (lax.broadcasted_iota(jnp.int32, (L, LANES), 1) == 0).astype(BF16)

    heads = range(M_HEADS)
    hsl = [slice(h * D, (h + 1) * D) for h in heads]

    def state_stages(rs):
        q = [q_ref[rs, hsl[h]] for h in heads]
        kT = [kT_ref[hsl[h], rs] for h in heads]
        v_aug = [jnp.concatenate([v_ref[rs, hsl[h]], ones_col], axis=1) for h in heads]
        b_c = [gc_ref[rs, M_HEADS + h:M_HEADS + h + 1] for h in heads]
        b_r = [gt_ref[M_HEADS + h:M_HEADS + h + 1, rs] for h in heads]
        u_r = [gt_ref[h:h + 1, rs] - b_r[h] for h in heads]
        f_tot = [b_r[h][:, L - 1:L] for h in heads]
        m_prev = [m_ref[h:h + 1, 0:1] for h in heads]

        s_qk = [jnp.dot(q[h], kT[h], preferred_element_type=F32) for h in heads]
        q_c = [jnp.dot(q[h], c_ref[h].astype(BF16), preferred_element_type=F32) for h in heads]

        for h in heads:
            w_r = f_tot[h] + u_r[h]
            m_loc = jnp.max(w_r, axis=1, keepdims=True)
            m_new = jnp.maximum(f_tot[h] + m_prev[h], m_loc)
            a = jnp.exp(f_tot[h] + m_prev[h] - m_new)
            e_r = jnp.exp(w_r - m_new)
            keT = (kT[h].astype(F32) * e_r).astype(BF16)
            c_ref[h] = a * c_ref[h] + jnp.dot(keT, v_aug[h], preferred_element_type=F32)
            m_ref[h:h + 1, :] = jnp.broadcast_to(m_new, (1, LANES))
        return rs, v_aug, b_c, u_r, m_prev, s_qk, q_c

    def output_stage(rs, v_aug, b_c, u_r, m_prev, s_qk, q_c):
        for h in heads:
            d_log = jnp.where(causal, b_c[h] + u_r[h], -jnp.inf)
            a_log = b_c[h] + m_prev[h]
            m_t = jnp.maximum(a_log, jnp.max(d_log, axis=1, keepdims=True))
            s_ts = s_qk[h] * jnp.exp(d_log - m_t)
            inter = jnp.exp(a_log - m_t)
            r = inter * q_c[h] + jnp.dot(s_ts.astype(BF16), v_aug[h], preferred_element_type=F32)
            num = r[:, :D]
            den = r[:, D:D + 1]
            hh = num / jnp.maximum(jnp.abs(den), jnp.exp(-m_t))
            hn = hh * lax.rsqrt(jnp.mean(hh * hh, axis=-1, keepdims=True) + NORM_EPS)
            out_ref[rs, hsl[h]] = (hn * gain_ref[:, hsl[h]]
                                   * _sigmoid(mo_ref[rs, hsl[h]].astype(F32))).astype(BF16)

    pending = [state_stages(slice(c * L, (c + 1) * L)) for c in range(MLSTM_CHUNKS_PER_STEP)]
    for vals in pending:
        output_stage(*vals)


def _mlstm(mq, mkT, P, gc, gt, gain):
    S = mq.shape[0]
    L = M_CHUNK * MLSTM_CHUNKS_PER_STEP
    return pl.pallas_call(
        _mlstm_kernel,
        grid=(S // L,),
        in_specs=[
            pl.BlockSpec((L, M_WIDTH), lambda c: (c, 0)),
            pl.BlockSpec((M_WIDTH, L), lambda c: (0, c)),
            pl.BlockSpec((L, M_WIDTH), lambda c: (c, COL_MV // M_WIDTH)),
            pl.BlockSpec((L, M_WIDTH), lambda c: (c, COL_MO // M_WIDTH)),
            pl.BlockSpec((L, LANES), lambda c: (c, 0)),
            pl.BlockSpec((SUBLANES, L), lambda c: (0, c)),
            pl.BlockSpec((1, M_WIDTH), lambda c: (0, 0)),
        ],
        out_specs=pl.BlockSpec((L, M_WIDTH), lambda c: (c, 0)),
        out_shape=jax.ShapeDtypeStruct((S, M_WIDTH), BF16),
        scratch_shapes=[
            pltpu.VMEM((M_HEADS, M_HEAD_DIM, M_AUG), F32),
            pltpu.VMEM((SUBLANES, LANES), F32),
        ],
        compiler_params=_cparams(("arbitrary",)),
        name="mlstm",
    )(mq, mkT, P, P, gc, gt, gain)


MOBA_HEADS_PER_GROUP = 4
ROUTE_TILES = 264
ROUTE_ROWS = ROUTE_TILES * MOBA_BLOCK
GROUP_TILES_PER_STEP = 24
ROW_WORDS = A_HEAD_DIM // 2
SC_WINDOW = 256


def _pack_rows(o_t, lse):
    q = o_t.shape[1]
    hi = pltpu.bitcast(o_t[:ROW_WORDS].astype(BF16).astype(F32), jnp.uint32)
    lo = pltpu.bitcast(o_t[ROW_WORDS:].astype(BF16).astype(F32), jnp.uint32)
    words = pltpu.bitcast(hi | (lo >> 16), F32)
    tail = jnp.concatenate([jnp.broadcast_to(lse, (SUBLANES, q)),
                            jnp.zeros((LANES - ROW_WORDS - SUBLANES, q), F32)], axis=0)
    return jnp.concatenate([words, tail], axis=0).T


def _dot_nt(a, b):
    return lax.dot_general(a, b, (((1,), (1,)), ((), ())), preferred_element_type=F32)


def _block_partial(k_blk, v_t_blk, qs, ones_rows, causal_mask=None):
    s = _dot_nt(k_blk, qs)
    if causal_mask is not None:
        s = jnp.where(causal_mask, s, NEG_BIG)
    sb = s.astype(BF16)
    m = jnp.max(sb, axis=0, keepdims=True)
    p = jnp.exp2(sb - m)
    r = jnp.dot(jnp.concatenate([v_t_blk, ones_rows], axis=0), p, preferred_element_type=F32)
    dh = v_t_blk.shape[0]
    l = r[dh:dh + 1, :]
    return r[:dh, :] / l, m.astype(F32) + jnp.log2(l)


def _route_kernel(q_ref, k_ref, vT_ref, km_ref, route_ref, cnt_ref, own_ref, cnt_acc):
    i = pl.program_id(1)
    BS = MOBA_BLOCK
    Dh = A_HEAD_DIM
    NB = km_ref.shape[0]
    G = q_ref.shape[0]
    blk = lax.broadcasted_iota(jnp.int32, (NB, BS), 0)
    kpos = lax.broadcasted_iota(jnp.int32, (BS, BS), 0)
    qpos = lax.broadcasted_iota(jnp.int32, (BS, BS), 1)
    earlier = (kpos < qpos).astype(BF16)
    ones_rows = jnp.ones((2 * SUBLANES, BS), BF16)

    @pl.when(i == 0)
    def _():
        cnt_acc[...] = jnp.zeros_like(cnt_acc)

    q = [q_ref[g] for g in range(G)]
    qs = [(q[g] * (Dh ** -0.5 * LOG2_E)).astype(BF16) for g in range(G)]
    parts = [_block_partial(k_ref[:, g * Dh:(g + 1) * Dh], vT_ref[g * Dh:(g + 1) * Dh, :], qs[g],
                            ones_rows, kpos <= qpos) for g in range(G)]

    gates = []
    for g in range(G):
        hs = slice(g * Dh, (g + 1) * Dh)
        km = km_ref[:, hs]
        kh = km.astype(BF16)
        kl = (km - kh.astype(F32)).astype(BF16)
        qh = q[g].astype(BF16)
        ql = (q[g] - qh.astype(F32)).astype(BF16)
        gate = _dot_nt(kh, qh) + _dot_nt(kh, ql) + _dot_nt(kl, qh)
        gates.append(jnp.where(blk < i, gate, -jnp.inf))
    picks = [[] for _ in range(G)]
    rows = [[] for _ in range(G)]
    for r in range(MOBA_TOPK):
        for g in range(G):
            mx = jnp.max(gates[g], axis=0, keepdims=True)
            idx = jnp.min(jnp.where(gates[g] == mx, blk, NB), axis=0, keepdims=True)
            idx = jnp.where(r < i, idx, -1)
            pick = blk == idx
            gates[g] = jnp.where(pick, -jnp.inf, gates[g])
            picks[g].append(pick)
            rows[g].append(idx)
    for g in range(G):
        onehot = sum(p.astype(F32) for p in picks[g])
        before = cnt_acc[g][:, 0:1] + jnp.dot(onehot.astype(BF16), earlier, preferred_element_type=F32)
        for r in range(MOBA_TOPK):
            rank = jnp.sum(jnp.where(picks[g][r], before, 0.0), axis=0, keepdims=True)
            rows[g].append(rank.astype(jnp.int32))
        rows[g].append(jnp.zeros((SUBLANES - 2 * MOBA_TOPK, BS), jnp.int32))
        route_ref[g] = jnp.concatenate(rows[g], axis=0)
        cnt_new = cnt_acc[g] + jnp.sum(onehot, axis=1, keepdims=True)
        cnt_acc[g] = cnt_new
        cnt_ref[g] = cnt_new
        own_ref[g] = _pack_rows(*parts[g])


def _route(aq_hm, ak, avT, kmean, h0, H):
    _, S, Dh = aq_hm.shape
    BS = MOBA_BLOCK
    NB = S // BS
    G = MOBA_HEADS_PER_GROUP
    W = G * Dh
    hb = h0 // G
    return pl.pallas_call(
        _route_kernel,
        grid=(H // G, NB),
        in_specs=[
            pl.BlockSpec((G, BS, Dh), lambda h, i: (hb + h, i, 0)),
            pl.BlockSpec((BS, W), lambda h, i: (i, hb + h)),
            pl.BlockSpec((W, BS), lambda h, i: (hb + h, i)),
            pl.BlockSpec((NB, W), lambda h, i: (0, hb + h)),
        ],
        out_specs=[
            pl.BlockSpec((G, SUBLANES, BS), lambda h, i: (h, 0, i)),
            pl.BlockSpec((G, NB, LANES), lambda h, i: (h, 0, 0)),
            pl.BlockSpec((G, BS, LANES), lambda h, i: (h, i, 0)),
        ],
        out_shape=[
            jax.ShapeDtypeStruct((H, SUBLANES, S), jnp.int32),
            jax.ShapeDtypeStruct((H, NB, LANES), F32),
            jax.ShapeDtypeStruct((H, S, LANES), F32),
        ],
        scratch_shapes=[pltpu.VMEM((G, NB, LANES), F32)],
        compiler_params=_cparams(("parallel", "arbitrary")),
        name="moba_route",
    )(aq_hm, ak, avT, kmean)


def _dest_kernel(seg_ref, route_ref, dest_ref):
    h = pl.program_id(0)
    NB = seg_ref.shape[1]
    r = route_ref[0]
    blk = r[0:MOBA_TOPK, :]
    rank = r[MOBA_TOPK:2 * MOBA_TOPK, :]
    base = h * ROUTE_ROWS
    dest = jnp.full(blk.shape, base + ROUTE_ROWS - 1, jnp.int32)
    for j in range(NB):
        dest = jnp.where(blk == j, base + seg_ref[h, j] + rank, dest)
    dest_ref[0] = jnp.concatenate(
        [dest, jnp.zeros((SUBLANES - MOBA_TOPK, dest.shape[1]), jnp.int32)], axis=0)


def _dest(seg_start, route):
    H, _, S = route.shape
    return pl.pallas_call(
        _dest_kernel,
        grid_spec=pltpu.PrefetchScalarGridSpec(
            num_scalar_prefetch=1,
            grid=(H,),
            in_specs=[pl.BlockSpec((1, SUBLANES, S), lambda h, seg: (h, 0, 0))],
            out_specs=pl.BlockSpec((1, SUBLANES, S), lambda h, seg: (h, 0, 0)),
        ),
        out_shape=jax.ShapeDtypeStruct((H, SUBLANES, S), jnp.int32),
        compiler_params=_cparams(("parallel",)),
        name="moba_dest",
    )(seg_start, route)


def _sc_mesh():
    return plsc.VectorSubcoreMesh(core_axis_name="c", subcore_axis_name="s")


def _sc_scatter(rows, row0, n_rows, idx, n_out):
    W = rows.shape[1]
    M = idx.shape[0]
    n_win = n_rows // SC_WINDOW
    win0 = row0 // SC_WINDOW

    @pl.kernel(out_type=jax.ShapeDtypeStruct((n_out, W), rows.dtype), mesh=_sc_mesh())
    def k(x_hbm, i_hbm, o_hbm):
        def body(x_vmem, i_vmem):
            pltpu.sync_copy(x_vmem, o_hbm.at[i_vmem.at[0]])

        pltpu.emit_pipeline(
            body,
            grid=(M // SC_WINDOW,),
            in_specs=[pl.BlockSpec((SC_WINDOW, W), lambda w: (win0 + lax.rem(w, n_win), 0)),
                      pl.BlockSpec((1, SC_WINDOW), lambda w: (0, w))],
            out_specs=[],
            core_axis_name=("c", "s"),
            dimension_semantics=(pltpu.PARALLEL,),
        )(x_hbm, i_hbm)

    return k(rows, idx.reshape(1, M))


def _sc_gather(table, idx):
    M = idx.shape[0]
    W = table.shape[1]

    @pl.kernel(out_type=jax.ShapeDtypeStruct((M, W), table.dtype), mesh=_sc_mesh())
    def k(x_hbm, i_hbm, o_hbm):
        def body(i_vmem, o_vmem):
            pltpu.sync_copy(x_hbm.at[i_vmem.at[0]], o_vmem)

        pltpu.emit_pipeline(
            body,
            grid=(M // SC_WINDOW,),
            in_specs=[pl.BlockSpec((1, SC_WINDOW), lambda w: (0, w))],
            out_specs=[pl.BlockSpec((SC_WINDOW, W), lambda w: (w, 0))],
            core_axis_name=("c", "s"),
            dimension_semantics=(pltpu.PARALLEL,),
        )(i_hbm, o_hbm)

    return k(table, idx.reshape(1, M))


def _group_kernel(tile_blk_ref, q_ref, k_ref, vT_ref, out_ref, s_ref, m_ref):
    h = pl.program_id(0)
    u = pl.program_id(1)
    last = pl.num_programs(1) - 2
    BS = MOBA_BLOCK
    Dh = A_HEAD_DIM
    T = GROUP_TILES_PER_STEP
    ones_rows = jnp.ones((2 * SUBLANES, BS), BF16)
    new = lax.rem(u, 2)
    old = 1 - new

    @pl.when((h == 0) & (u == 0))
    def _():
        s_ref[...] = jnp.zeros_like(s_ref)
        m_ref[...] = jnp.zeros_like(m_ref)

    def block_start(group, c):
        j = jnp.maximum(tile_blk_ref[h * ROUTE_TILES + group * T + c], 0)
        return pl.multiple_of(j * BS, BS)

    g_new = jnp.minimum(u, last)
    g_old = jnp.maximum(u - 1, 0)
    @pl.when(tile_blk_ref[h * ROUTE_TILES + g_old * T] >= 0)
    def _():
        ms, rs = [], []
        for c in range(T):
            m = m_ref[old, c]
            p = jnp.exp2(s_ref[old, c] - m.astype(BF16))
            v_aug = jnp.concatenate([vT_ref[:, pl.ds(block_start(g_old, c), BS)], ones_rows], axis=0)
            ms.append(m)
            rs.append(jnp.dot(v_aug, p, preferred_element_type=F32))
        s_new = []
        for c in range(T):
            qs = (q_ref[c * BS:(c + 1) * BS, :] * (Dh ** -0.5 * LOG2_E)).astype(BF16)
            s_new.append(_dot_nt(k_ref[pl.ds(block_start(g_new, c), BS), :], qs))
        for c in range(T):
            l = rs[c][Dh:Dh + 1, :]
            out_ref[c * BS:(c + 1) * BS, :] = _pack_rows(rs[c][:Dh, :] / l, ms[c] + jnp.log2(l))
            sb = s_new[c].astype(BF16)
            s_ref[new, c] = sb
            m_ref[new, c] = jnp.max(sb, axis=0, keepdims=True).astype(F32)


def _group(tile_blk, q_sorted, ak, avT, h0):
    S = ak.shape[0]
    H = q_sorted.shape[0] // ROUTE_ROWS
    Dh = A_HEAD_DIM
    BS = MOBA_BLOCK
    T = GROUP_TILES_PER_STEP
    steps = ROUTE_TILES // T
    rows = T * BS
    return pl.pallas_call(
        _group_kernel,
        grid_spec=pltpu.PrefetchScalarGridSpec(
            num_scalar_prefetch=1,
            grid=(H, steps + 1),
            in_specs=[
                pl.BlockSpec((rows, LANES), lambda h, u, tb: (h * steps + jnp.minimum(u, steps - 1), 0)),
                pl.BlockSpec((S, Dh), lambda h, u, tb: (0, h0 + h)),
                pl.BlockSpec((Dh, S), lambda h, u, tb: (h0 + h, 0)),
            ],
            out_specs=pl.BlockSpec((rows, LANES), lambda h, u, tb: (h * steps + jnp.maximum(u - 1, 0), 0)),
            scratch_shapes=[pltpu.VMEM((2, T, BS, BS), BF16), pltpu.VMEM((2, T, 1, BS), F32)],
        ),
        out_shape=jax.ShapeDtypeStruct((H * ROUTE_ROWS, LANES), F32),
        compiler_params=_cparams(("arbitrary", "arbitrary")),
        name="moba_group",
    )(tile_blk, q_sorted, ak, avT)


def _combine_kernel(*refs):
    i = pl.program_id(0)
    o_ref = refs[-1]
    n_groups = (len(refs) - 1) // 2
    H = refs[n_groups].shape[0]
    BS = refs[n_groups].shape[1]
    low_half = lax.broadcasted_iota(jnp.int32, (BS, LANES), 1) < ROW_WORDS
    for head in range(n_groups * H):
        got_ref, own_ref, h = refs[head // H], refs[n_groups + head // H], head % H
        tiles = [own_ref[h]]
        for r in range(MOBA_TOPK):
            tiles.append(jnp.where(r < i, got_ref[h, r], 0.0))
        lses = [tiles[0][:, ROW_WORDS:ROW_WORDS + 1]]
        lses += [jnp.where(r < i, tiles[r + 1][:, ROW_WORDS:ROW_WORDS + 1], NEG_BIG)
                 for r in range(MOBA_TOPK)]
        top = functools.reduce(jnp.maximum, lses)
        w = [jnp.exp2(x - top) for x in lses]
        inv = 1.0 / sum(w)
        first = jnp.zeros((BS, LANES), F32)
        second = jnp.zeros((BS, LANES), F32)
        for wk, tile in zip(w, tiles):
            words = pltpu.bitcast(tile, jnp.uint32)
            first = first + wk * pltpu.bitcast(words & jnp.uint32(0xFFFF0000), F32)
            second = second + wk * pltpu.bitcast(words << 16, F32)
        o = jnp.where(low_half, first, pltpu.roll(second, ROW_WORDS, 1)) * inv
        o_ref[:, head * A_HEAD_DIM:(head + 1) * A_HEAD_DIM] = o.astype(BF16)


def _combine(gots, owns):
    H, S, _ = owns[0].shape
    BS = MOBA_BLOCK
    return pl.pallas_call(
        _combine_kernel,
        grid=(S // BS,),
        in_specs=([pl.BlockSpec((H, MOBA_TOPK, BS, LANES), lambda i: (0, 0, i, 0))] * len(gots)
                  + [pl.BlockSpec((H, BS, LANES), lambda i: (0, i, 0))] * len(owns)),
        out_specs=pl.BlockSpec((BS, A_WIDTH), lambda i: (i, 0)),
        out_shape=jax.ShapeDtypeStruct((S, A_WIDTH), BF16),
        compiler_params=_cparams(("parallel",)),
        name="moba_combine",
    )(*gots, *owns)


def _moba_routed(aq_hm, ak, avT, kmean):
    H, S, Dh = aq_hm.shape
    NB = S // MOBA_BLOCK
    hn = MOBA_HEADS_PER_GROUP
    q_rows = aq_hm.reshape(H * S, Dh)
    gots, owns = [], []
    for h0 in range(0, H, hn):
        route, cnt, own = _route(aq_hm, ak, avT, kmean, h0, hn)
        cnt = cnt[:, :, 0].astype(jnp.int32)
        seg_tiles = (cnt + MOBA_BLOCK - 1) // MOBA_BLOCK
        seg_end = jnp.cumsum(seg_tiles, axis=1)
        seg_start = (seg_end - seg_tiles) * MOBA_BLOCK
        tile_ids = jnp.arange(ROUTE_TILES, dtype=jnp.int32)
        tile_blk = jnp.sum(tile_ids[None, :, None] >= seg_end[:, None, :], axis=2).astype(jnp.int32)
        tile_blk = jnp.where(tile_blk < NB, tile_blk, -1).reshape(hn * ROUTE_TILES)
        dest = _dest(seg_start, route)[:, :MOBA_TOPK, :]
        q_sorted = _sc_scatter(q_rows, h0 * S, hn * S, dest.transpose(1, 0, 2).reshape(-1),
                               hn * ROUTE_ROWS)
        results = _group(tile_blk, q_sorted, ak, avT, h0)
        gots.append(_sc_gather(results, dest.reshape(-1)).reshape(hn, MOBA_TOPK, S, LANES))
        owns.append(own)
    return _combine(gots, owns)


def _merge_kernel(hm_ref, ha_ref, gm_ref, ga_ref, x_ref, wm_ref, wa_ref, wo_ref, gain_ref,
                  out_ref):
    ym = jnp.dot(hm_ref[...], wm_ref[...], preferred_element_type=F32)
    ya = jnp.dot(ha_ref[...], wa_ref[...], preferred_element_type=F32)
    merged = _sigmoid(gm_ref[...].astype(F32)) * ym + _sigmoid(ga_ref[...].astype(F32)) * ya
    mix = jnp.dot(merged.astype(BF16), wo_ref[...], preferred_element_type=F32)
    out_ref[...] = x_ref[...] + _rms(mix, gain_ref[...])


def _merge(hm, ha, P, x, b, wm, wa, wo, gain, tm=256):
    _, S, D = x.shape
    const = pl.Buffered(1)
    return pl.pallas_call(
        _merge_kernel,
        grid=(S // tm,),
        in_specs=[
            pl.BlockSpec((tm, M_WIDTH), lambda i: (i, 0)),
            pl.BlockSpec((tm, A_WIDTH), lambda i: (i, 0)),
            pl.BlockSpec((tm, D), lambda i: (i, COL_GM // D)),
            pl.BlockSpec((tm, D), lambda i: (i, COL_GA // D)),
            pl.BlockSpec((None, tm, D), lambda i: (b, i, 0)),
            pl.BlockSpec((M_WIDTH, D), lambda i: (0, 0), pipeline_mode=const),
            pl.BlockSpec((A_WIDTH, D), lambda i: (0, 0), pipeline_mode=const),
            pl.BlockSpec((D, D), lambda i: (0, 0), pipeline_mode=const),
            pl.BlockSpec((1, D), lambda i: (0, 0)),
        ],
        out_specs=pl.BlockSpec((tm, D), lambda i: (i, 0)),
        out_shape=jax.ShapeDtypeStruct((S, D), F32),
        compiler_params=_cparams(("parallel",)),
        name="merge",
    )(hm, ha, P, P, x, wm, wa, wo, gain)


def _ffn_kernel(x_ref, gpre_ref, wu_ref, wd_ref, gpost_ref, out_ref, hn_ref, acc_ref):
    f = pl.program_id(1)

    @pl.when(f == 0)
    def _():
        hn_ref[...] = _rms(x_ref[...], gpre_ref[...]).astype(BF16)
        acc_ref[...] = jnp.zeros_like(acc_ref)

    u = jnp.dot(hn_ref[...], wu_ref[...], preferred_element_type=F32)
    u = jnp.square(jnp.maximum(u, 0.0)).astype(BF16)
    acc_ref[...] += jnp.dot(u, wd_ref[...], preferred_element_type=F32)

    @pl.when(f == pl.num_programs(1) - 1)
    def _():
        out_ref[...] = x_ref[...] + _rms(acc_ref[...], gpost_ref[...])


def _ffn(x, gpre, wu, wd, gpost, tm=512, tf=1024):
    S, D = x.shape
    Fd = wu.shape[1]
    return pl.pallas_call(
        _ffn_kernel,
        grid=(S // tm, Fd // tf),
        in_specs=[
            pl.BlockSpec((tm, D), lambda i, f: (i, 0)),
            pl.BlockSpec((1, D), lambda i, f: (0, 0)),
            pl.BlockSpec((D, tf), lambda i, f: (0, f)),
            pl.BlockSpec((tf, D), lambda i, f: (f, 0)),
            pl.BlockSpec((1, D), lambda i, f: (0, 0)),
        ],
        out_specs=pl.BlockSpec((tm, D), lambda i, f: (i, 0)),
        out_shape=jax.ShapeDtypeStruct((S, D), F32),
        scratch_shapes=[pltpu.VMEM((tm, D), BF16), pltpu.VMEM((tm, D), F32)],
        compiler_params=_cparams(("parallel", "arbitrary")),
        name="ffn",
    )(x, gpre, wu, wd, gpost)


def _layer(x, b, pos, norm_mix_pre, w_in, conv_w, conv_b, i_bias, f_bias, mlstm_norm,
           w_branch_m, w_branch_a, w_out, norm_mix_post, norm_ffn_pre, w_up, w_down,
           norm_ffn_post):
    _, S, D = x.shape
    o = 0
    pieces = {}
    w_in = w_in.astype(BF16)
    for name, width in (("mq", M_WIDTH), ("mk", M_WIDTH), ("mv", M_WIDTH), ("mo", M_WIDTH),
                        ("mi", M_HEADS), ("mf", M_HEADS), ("aq", A_WIDTH), ("ak", A_WIDTH),
                        ("av", A_WIDTH), ("gm", D), ("ga", D)):
        pieces[name] = w_in[:, o:o + width]
        o += width
    w_all = jnp.concatenate([pieces[n] for n in ("gm", "ga", "mq", "mk", "aq", "ak", "mv", "mo", "av")],
                            axis=1)
    w_gate = jnp.concatenate(
        [pieces["mi"], pieces["mf"], jnp.zeros((D, LANES - 2 * M_HEADS), BF16)], axis=1)
    gbias = jnp.concatenate([i_bias, f_bias, jnp.zeros((LANES - 2 * M_HEADS,), F32)])[None, :]
    half = jnp.arange(0, A_HEAD_DIM, 2, dtype=F32) / A_HEAD_DIM
    inv_freq = 1.0 / (ROPE_THETA ** half)
    invf = jnp.concatenate([inv_freq, inv_freq])[None, :]

    P, gate = _proj(x, b, norm_mix_pre[None, :], w_all, w_gate)
    mq, mkT, aq, ak, avT, kmean, gc, gt = _prep(
        P, gate, pos.reshape(S, 1), invf, conv_w, conv_b[None, :], gbias)
    hm = _mlstm(mq, mkT, P, gc, gt, mlstm_norm[None, :])
    ha = _moba_routed(aq, ak, avT, kmean.reshape(S // MOBA_BLOCK, A_WIDTH))
    x1 = _merge(hm, ha, P, x, b, w_branch_m.astype(BF16), w_branch_a.astype(BF16),
                w_out.astype(BF16), norm_mix_post[None, :])
    return _ffn(x1, norm_ffn_pre[None, :], w_up.astype(BF16), w_down.astype(BF16),
                norm_ffn_post[None, :])


def kernel(x, positions, norm_mix_pre, w_in, conv_w, conv_b, i_bias, f_bias, mlstm_norm,
           w_branch_m, w_branch_a, w_out, norm_mix_post, norm_ffn_pre, w_up, w_down,
           norm_ffn_post):
    B = x.shape[0]
    depth = w_in.shape[0]
    outs = []
    def take(a, i):
        return a.reshape(a.shape[1:]) if a.shape[0] == 1 else a[i]

    for b in range(B):
        xin, bi = x, b
        for l in range(depth):
            xb = _layer(xin, bi, take(positions, b), take(norm_mix_pre, l), take(w_in, l),
                        take(conv_w, l), take(conv_b, l), take(i_bias, l), take(f_bias, l),
                        take(mlstm_norm, l), take(w_branch_m, l), take(w_branch_a, l),
                        take(w_out, l), take(norm_mix_post, l), take(norm_ffn_pre, l),
                        take(w_up, l), take(w_down, l), take(norm_ffn_post, l))
            xin, bi = xb[None], 0
        outs.append(xb)
    return outs[0][None] if B == 1 else jnp.stack(outs, axis=0)
```

```python
import functools

import jax
import jax.numpy as jnp
from jax import lax
from jax.experimental import pallas as pl
from jax.experimental.pallas import tpu as pltpu
from jax.experimental.pallas import tpu_sc as plsc

F32 = jnp.float32
BF16 = jnp.bfloat16

M_HEADS = 4
M_HEAD_DIM = 256
M_WIDTH = M_HEADS * M_HEAD_DIM
M_CHUNK = 128
CONV_WIDTH = 4
A_HEADS = 8
A_HEAD_DIM = 128
A_WIDTH = A_HEADS * A_HEAD_DIM
MOBA_BLOCK = 256
MOBA_TOPK = 3
ROPE_THETA = 10000.0
NORM_EPS = 1e-6

LANES = 128
SUBLANES = 8
VMEM_LIMIT = 56 * 1024 * 1024
NEG_BIG = -1e30
LOG2_E = 1.4426950408889634

COL_GM = 0
COL_GA = 2048
COL_MQK = 4096
COL_AQK = 6144
COL_MV = 8192
COL_MO = 9216
COL_AV = 10240


def _cparams(sem):
    return pltpu.CompilerParams(dimension_semantics=sem, vmem_limit_bytes=VMEM_LIMIT)


def _rms(x, gain):
    ms = jnp.mean(x * x, axis=-1, keepdims=True)
    return x * lax.rsqrt(ms + NORM_EPS) * gain


def _sigmoid(x):
    return 1.0 / (1.0 + jnp.exp(-x))


def _split3(x):
    hi = x.astype(BF16)
    r1 = x - hi.astype(F32)
    mid = r1.astype(BF16)
    lo = (r1 - mid.astype(F32)).astype(BF16)
    return hi, mid, lo


def _proj_kernel(x_ref, g_ref, w_ref, wg_ref, p_ref, gate_ref, xn_ref):
    @pl.when(pl.program_id(1) == 0)
    def _():
        xn = _rms(x_ref[...], g_ref[...]).astype(BF16)
        xn_ref[...] = xn
        gate_ref[...] = jnp.dot(xn, wg_ref[...], preferred_element_type=F32)

    p_ref[...] = jnp.dot(xn_ref[...], w_ref[...], preferred_element_type=F32).astype(BF16)


def _proj(x, b, gain, w_all, w_gate, tm=1024, tn=1024):
    _, S, D = x.shape
    N = w_all.shape[1]
    return pl.pallas_call(
        _proj_kernel,
        grid=(S // tm, N // tn),
        in_specs=[
            pl.BlockSpec((None, tm, D), lambda i, j: (b, i, 0)),
            pl.BlockSpec((1, D), lambda i, j: (0, 0)),
            pl.BlockSpec((D, tn), lambda i, j: (0, j)),
            pl.BlockSpec((D, LANES), lambda i, j: (0, 0)),
        ],
        out_specs=[
            pl.BlockSpec((tm, tn), lambda i, j: (i, j)),
            pl.BlockSpec((tm, LANES), lambda i, j: (i, 0)),
        ],
        out_shape=[
            jax.ShapeDtypeStruct((S, N), BF16),
            jax.ShapeDtypeStruct((S, LANES), F32),
        ],
        scratch_shapes=[pltpu.VMEM((tm, D), BF16)],
        compiler_params=_cparams(("parallel", "arbitrary")),
        name="proj",
    )(x, gain, w_all, w_gate)


PREP_ROWS = MOBA_BLOCK
PREP_COLS = 512
HALO_ROWS = 2 * SUBLANES


def _prep_kernel(pmk_ref, halo_ref, paqk_ref, pav_ref, gate_ref, pos_ref, invf_ref,
                 cw_ref, cb_ref, gb_ref,
                 mq_ref, mkT_ref, aq_ref, ak_ref, avT_ref, kmean_ref, gc_ref, gt_ref):
    i = pl.program_id(0)
    R = PREP_ROWS

    k_scale = M_HEAD_DIM ** -0.5
    for c0 in range(0, 2 * M_WIDTH, PREP_COLS):
        cs = slice(c0, c0 + PREP_COLS)
        prev = halo_ref[:, cs].astype(F32)[HALO_ROWS - SUBLANES:, :]
        prev = jnp.where(i == 0, jnp.zeros_like(prev), prev)
        ext = jnp.concatenate([pmk_ref[:, cs].astype(F32), prev], axis=0)
        acc = cw_ref[0:1, cs] * ext
        for j in range(1, CONV_WIDTH):
            acc = pltpu.roll(acc, 1, 0) + cw_ref[j:j + 1, cs] * ext
        acc = acc[0:R, :] + cb_ref[:, cs]
        y = acc * _sigmoid(acc)
        if c0 < M_WIDTH:
            mq_ref[:, cs] = y.astype(BF16)
        else:
            ks = slice(c0 - M_WIDTH, c0 - M_WIDTH + PREP_COLS)
            mkT_ref[ks, :] = (y * k_scale).T.astype(BF16)

    ang = pos_ref[...].astype(F32) * invf_ref[...]
    cos = jnp.cos(ang)
    lane = lax.broadcasted_iota(jnp.int32, (R, A_HEAD_DIM), 1)
    sin_signed = jnp.where(lane < A_HEAD_DIM // 2, -1.0, 1.0) * jnp.sin(ang)
    for h in range(2 * A_HEADS):
        hs = slice(h * A_HEAD_DIM, (h + 1) * A_HEAD_DIM)
        xh = paqk_ref[:, hs].astype(F32)
        yh = xh * cos + pltpu.roll(xh, A_HEAD_DIM // 2, 1) * sin_signed
        if h < A_HEADS:
            aq_ref[h] = yh
        else:
            ko = slice((h - A_HEADS) * A_HEAD_DIM, (h - A_HEADS + 1) * A_HEAD_DIM)
            ak_ref[:, ko] = yh.astype(BF16)
            kmean_ref[0, :, ko] = jnp.mean(yh, axis=0, keepdims=True)
    for c0 in range(0, A_WIDTH, PREP_COLS):
        avT_ref[c0:c0 + PREP_COLS, :] = pav_ref[:, c0:c0 + PREP_COLS].astype(F32).T.astype(BF16)

    g = gate_ref[...] + gb_ref[...]
    log_f = jnp.minimum(g, 0.0) - jnp.log1p(jnp.exp(-jnp.abs(g)))
    r_i = lax.broadcasted_iota(jnp.int32, (R, R), 0)
    c_i = lax.broadcasted_iota(jnp.int32, (R, R), 1)
    tri = ((r_i >= c_i) & ((r_i // M_CHUNK) == (c_i // M_CHUNK))).astype(BF16)
    hi, mid, lo = _split3(log_f)
    csum = (jnp.dot(tri, hi, preferred_element_type=F32)
            + jnp.dot(tri, mid, preferred_element_type=F32)
            + jnp.dot(tri, lo, preferred_element_type=F32))
    glane = lax.broadcasted_iota(jnp.int32, (R, LANES), 1)
    gc = jnp.where(glane < M_HEADS, g, csum)
    gc_ref[...] = gc
    gt_ref[...] = gc.T[0:SUBLANES, :]


def _prep(P, gate, pos, invf, conv_w, conv_b, gbias):
    S = P.shape[0]
    R = PREP_ROWS
    nb = S // R
    halo_blocks = R // HALO_ROWS
    return pl.pallas_call(
        _prep_kernel,
        grid=(nb,),
        in_specs=[
            pl.BlockSpec((R, 2 * M_WIDTH), lambda i: (i, COL_MQK // (2 * M_WIDTH))),
            pl.BlockSpec((HALO_ROWS, 2 * M_WIDTH),
                         lambda i: (jnp.maximum(i * halo_blocks - 1, 0), COL_MQK // (2 * M_WIDTH))),
            pl.BlockSpec((R, 2 * A_WIDTH), lambda i: (i, COL_AQK // (2 * A_WIDTH))),
            pl.BlockSpec((R, A_WIDTH), lambda i: (i, COL_AV // A_WIDTH)),
            pl.BlockSpec((R, LANES), lambda i: (i, 0)),
            pl.BlockSpec((R, 1), lambda i: (i, 0)),
            pl.BlockSpec((1, A_HEAD_DIM), lambda i: (0, 0)),
            pl.BlockSpec((CONV_WIDTH, 2 * M_WIDTH), lambda i: (0, 0)),
            pl.BlockSpec((1, 2 * M_WIDTH), lambda i: (0, 0)),
            pl.BlockSpec((1, LANES), lambda i: (0, 0)),
        ],
        out_specs=[
            pl.BlockSpec((R, M_WIDTH), lambda i: (i, 0)),
            pl.BlockSpec((M_WIDTH, R), lambda i: (0, i)),
            pl.BlockSpec((A_HEADS, R, A_HEAD_DIM), lambda i: (0, i, 0)),
            pl.BlockSpec((R, A_WIDTH), lambda i: (i, 0)),
            pl.BlockSpec((A_WIDTH, R), lambda i: (0, i)),
            pl.BlockSpec((1, 1, A_WIDTH), lambda i: (i, 0, 0)),
            pl.BlockSpec((R, LANES), lambda i: (i, 0)),
            pl.BlockSpec((SUBLANES, R), lambda i: (0, i)),
        ],
        out_shape=[
            jax.ShapeDtypeStruct((S, M_WIDTH), BF16),
            jax.ShapeDtypeStruct((M_WIDTH, S), BF16),
            jax.ShapeDtypeStruct((A_HEADS, S, A_HEAD_DIM), F32),
            jax.ShapeDtypeStruct((S, A_WIDTH), BF16),
            jax.ShapeDtypeStruct((A_WIDTH, S), BF16),
            jax.ShapeDtypeStruct((nb, 1, A_WIDTH), F32),
            jax.ShapeDtypeStruct((S, LANES), F32),
            jax.ShapeDtypeStruct((SUBLANES, S), F32),
        ],
        compiler_params=_cparams(("parallel",)),
        name="prep",
    )(P, P, P, P, gate, pos, invf, conv_w, conv_b, gbias)


M_AUG = M_HEAD_DIM + LANES
MLSTM_CHUNKS_PER_STEP = 1


def _mlstm_kernel(q_ref, kT_ref, v_ref, mo_ref, gc_ref, gt_ref, gain_ref, out_ref,
                  c_ref, m_ref):
    @pl.when(pl.program_id(0) == 0)
    def _():
        c_ref[...] = jnp.zeros_like(c_ref)
        m_ref[...] = jnp.zeros_like(m_ref)

    L = M_CHUNK
    D = M_HEAD_DIM
    row = lax.broadcasted_iota(jnp.int32, (L, L), 0)
    col = lax.broadcasted_iota(jnp.int32, (L, L), 1)
    causal = row >= col
    ones_col = (lax.broadcasted_iota(jnp.int32, (L, LANES), 1) == 0).astype(BF16)

    heads = range(M_HEADS)
    hsl = [slice(h * D, (h + 1) * D) for h in heads]

    def state_stages(rs):
        q = [q_ref[rs, hsl[h]] for h in heads]
        kT = [kT_ref[hsl[h], rs] for h in heads]
        v_aug = [jnp.concatenate([v_ref[rs, hsl[h]], ones_col], axis=1) for h in heads]
        b_c = [gc_ref[rs, M_HEADS + h:M_HEADS + h + 1] for h in heads]
        b_r = [gt_ref[M_HEADS + h:M_HEADS + h + 1, rs] for h in heads]
        u_r = [gt_ref[h:h + 1, rs] - b_r[h] for h in heads]
        f_tot = [b_r[h][:, L - 1:L] for h in heads]
        m_prev = [m_ref[h:h + 1, 0:1] for h in heads]

        s_qk = [jnp.dot(q[h], kT[h], preferred_element_type=F32) for h in heads]
        q_c = [jnp.dot(q[h], c_ref[h].astype(BF16), preferred_element_type=F32) for h in heads]

        for h in heads:
            w_r = f_tot[h] + u_r[h]
            m_loc = jnp.max(w_r, axis=1, keepdims=True)
            m_new = jnp.maximum(f_tot[h] + m_prev[h], m_loc)
            a = jnp.exp(f_tot[h] + m_prev[h] - m_new)
            e_r = jnp.exp(w_r - m_new)
            keT = (kT[h].astype(F32) * e_r).astype(BF16)
            c_ref[h] = a * c_ref[h] + jnp.dot(keT, v_aug[h], preferred_element_type=F32)
            m_ref[h:h + 1, :] = jnp.broadcast_to(m_new, (1, LANES))
        return rs, v_aug, b_c, u_r, m_prev, s_qk, q_c

    def output_stage(rs, v_aug, b_c, u_r, m_prev, s_qk, q_c):
        for h in heads:
            d_log = jnp.where(causal, b_c[h] + u_r[h], -jnp.inf)
            a_log = b_c[h] + m_prev[h]
            m_t = jnp.maximum(a_log, jnp.max(d_log, axis=1, keepdims=True))
            s_ts = s_qk[h] * jnp.exp(d_log - m_t)
            inter = jnp.exp(a_log - m_t)
            r = inter * q_c[h] + jnp.dot(s_ts.astype(BF16), v_aug[h], preferred_element_type=F32)
            num = r[:, :D]
            den = r[:, D:D + 1]
            hh = num / jnp.maximum(jnp.abs(den), jnp.exp(-m_t))
            hn = hh * lax.rsqrt(jnp.mean(hh * hh, axis=-1, keepdims=True) + NORM_EPS)
            out_ref[rs, hsl[h]] = (hn * gain_ref[:, hsl[h]]
                                   * _sigmoid(mo_ref[rs, hsl[h]].astype(F32))).astype(BF16)

    pending = [state_stages(slice(c * L, (c + 1) * L)) for c in range(MLSTM_CHUNKS_PER_STEP)]
    for vals in pending:
        output_stage(*vals)


def _mlstm(mq, mkT, P, gc, gt, gain):
    S = mq.shape[0]
    L = M_CHUNK * MLSTM_CHUNKS_PER_STEP
    return pl.pallas_call(
        _mlstm_kernel,
        grid=(S // L,),
        in_specs=[
            pl.BlockSpec((L, M_WIDTH), lambda c: (c, 0)),
            pl.BlockSpec((M_WIDTH, L), lambda c: (0, c)),
            pl.BlockSpec((L, M_WIDTH), lambda c: (c, COL_MV // M_WIDTH)),
            pl.BlockSpec((L, M_WIDTH), lambda c: (c, COL_MO // M_WIDTH)),
            pl.BlockSpec((L, LANES), lambda c: (c, 0)),
            pl.BlockSpec((SUBLANES, L), lambda c: (0, c)),
            pl.BlockSpec((1, M_WIDTH), lambda c: (0, 0)),
        ],
        out_specs=pl.BlockSpec((L, M_WIDTH), lambda c: (c, 0)),
        out_shape=jax.ShapeDtypeStruct((S, M_WIDTH), BF16),
        scratch_shapes=[
            pltpu.VMEM((M_HEADS, M_HEAD_DIM, M_AUG), F32),
            pltpu.VMEM((SUBLANES, LANES), F32),
        ],
        compiler_params=_cparams(("arbitrary",)),
        name="mlstm",
    )(mq, mkT, P, P, gc, gt, gain)


MOBA_HEADS_PER_GROUP = 4
ROUTE_TILES = 264
ROUTE_ROWS = ROUTE_TILES * MOBA_BLOCK
GROUP_TILES_PER_STEP = 24
ROW_WORDS = A_HEAD_DIM // 2
SC_WINDOW = 256
SC_STREAMS = 4


def _pack_rows(o_t, lse):
    q = o_t.shape[1]
    hi = pltpu.bitcast(o_t[:ROW_WORDS].astype(BF16).astype(F32), jnp.uint32)
    lo = pltpu.bitcast(o_t[ROW_WORDS:].astype(BF16).astype(F32), jnp.uint32)
    words = pltpu.bitcast(hi | (lo >> 16), F32)
    tail = jnp.concatenate([jnp.broadcast_to(lse, (SUBLANES, q)),
                            jnp.zeros((LANES - ROW_WORDS - SUBLANES, q), F32)], axis=0)
    return jnp.concatenate([words, tail], axis=0).T


def _dot_nt(a, b):
    return lax.dot_general(a, b, (((1,), (1,)), ((), ())), preferred_element_type=F32)


def _block_partial(k_blk, v_t_blk, qs, ones_rows, causal_mask=None):
    s = _dot_nt(k_blk, qs)
    if causal_mask is not None:
        s = jnp.where(causal_mask, s, NEG_BIG)
    sb = s.astype(BF16)
    m = jnp.max(sb, axis=0, keepdims=True)
    p = jnp.exp2(sb - m)
    r = jnp.dot(jnp.concatenate([v_t_blk, ones_rows], axis=0), p, preferred_element_type=F32)
    dh = v_t_blk.shape[0]
    l = r[dh:dh + 1, :]
    return r[:dh, :] / l, m.astype(F32) + jnp.log2(l)


def _route_kernel(q_ref, k_ref, vT_ref, km_ref, route_ref, cnt_ref, own_ref, cnt_acc):
    i = pl.program_id(1)
    BS = MOBA_BLOCK
    Dh = A_HEAD_DIM
    NB = km_ref.shape[0]
    G = q_ref.shape[0]
    blk = lax.broadcasted_iota(jnp.int32, (NB, BS), 0)
    kpos = lax.broadcasted_iota(jnp.int32, (BS, BS), 0)
    qpos = lax.broadcasted_iota(jnp.int32, (BS, BS), 1)
    earlier = (kpos < qpos).astype(BF16)
    ones_rows = jnp.ones((2 * SUBLANES, BS), BF16)

    @pl.when(i == 0)
    def _():
        cnt_acc[...] = jnp.zeros_like(cnt_acc)

    q = [q_ref[g] for g in range(G)]
    qs = [(q[g] * (Dh ** -0.5 * LOG2_E)).astype(BF16) for g in range(G)]
    parts = [_block_partial(k_ref[:, g * Dh:(g + 1) * Dh], vT_ref[g * Dh:(g + 1) * Dh, :], qs[g],
                            ones_rows, kpos <= qpos) for g in range(G)]

    gates = []
    for g in range(G):
        hs = slice(g * Dh, (g + 1) * Dh)
        km = km_ref[:, hs]
        kh = km.astype(BF16)
        kl = (km - kh.astype(F32)).astype(BF16)
        qh = q[g].astype(BF16)
        ql = (q[g] - qh.astype(F32)).astype(BF16)
        gate = _dot_nt(kh, qh) + _dot_nt(kh, ql) + _dot_nt(kl, qh)
        gates.append(jnp.where(blk < i, gate, -jnp.inf))
    picks = [[] for _ in range(G)]
    rows = [[] for _ in range(G)]
    for r in range(MOBA_TOPK):
        for g in range(G):
            mx = jnp.max(gates[g], axis=0, keepdims=True)
            idx = jnp.min(jnp.where(gates[g] == mx, blk, NB), axis=0, keepdims=True)
            idx = jnp.where(r < i, idx, -1)
            pick = blk == idx
            gates[g] = jnp.where(pick, -jnp.inf, gates[g])
            picks[g].append(pick)
            rows[g].append(idx)
    for g in range(G):
        onehot = sum(p.astype(F32) for p in picks[g])
        before = cnt_acc[g][:, 0:1] + jnp.dot(onehot.astype(BF16), earlier, preferred_element_type=F32)
        for r in range(MOBA_TOPK):
            rank = jnp.sum(jnp.where(picks[g][r], before, 0.0), axis=0, keepdims=True)
            rows[g].append(rank.astype(jnp.int32))
        rows[g].append(jnp.zeros((SUBLANES - 2 * MOBA_TOPK, BS), jnp.int32))
        route_ref[g] = jnp.concatenate(rows[g], axis=0)
        cnt_new = cnt_acc[g] + jnp.sum(onehot, axis=1, keepdims=True)
        cnt_acc[g] = cnt_new
        cnt_ref[g] = cnt_new
        own_ref[g] = _pack_rows(*parts[g])


def _route(aq_hm, ak, avT, kmean, h0, H):
    _, S, Dh = aq_hm.shape
    BS = MOBA_BLOCK
    NB = S // BS
    G = MOBA_HEADS_PER_GROUP
    W = G * Dh
    hb = h0 // G
    return pl.pallas_call(
        _route_kernel,
        grid=(H // G, NB),
        in_specs=[
            pl.BlockSpec((G, BS, Dh), lambda h, i: (hb + h, i, 0)),
            pl.BlockSpec((BS, W), lambda h, i: (i, hb + h)),
            pl.BlockSpec((W, BS), lambda h, i: (hb + h, i)),
            pl.BlockSpec((NB, W), lambda h, i: (0, hb + h)),
        ],
        out_specs=[
            pl.BlockSpec((G, SUBLANES, BS), lambda h, i: (h, 0, i)),
            pl.BlockSpec((G, NB, LANES), lambda h, i: (h, 0, 0)),
            pl.BlockSpec((G, BS, LANES), lambda h, i: (h, i, 0)),
        ],
        out_shape=[
            jax.ShapeDtypeStruct((H, SUBLANES, S), jnp.int32),
            jax.ShapeDtypeStruct((H, NB, LANES), F32),
            jax.ShapeDtypeStruct((H, S, LANES), F32),
        ],
        scratch_shapes=[pltpu.VMEM((G, NB, LANES), F32)],
        compiler_params=_cparams(("parallel", "arbitrary")),
        name="moba_route",
    )(aq_hm, ak, avT, kmean)


def _dest_kernel(seg_ref, route_ref, dest_ref):
    h = pl.program_id(0)
    NB = seg_ref.shape[1]
    r = route_ref[0]
    blk = r[0:MOBA_TOPK, :]
    rank = r[MOBA_TOPK:2 * MOBA_TOPK, :]
    base = h * ROUTE_ROWS
    dest = jnp.full(blk.shape, base + ROUTE_ROWS - 1, jnp.int32)
    for j in range(NB):
        dest = jnp.where(blk == j, base + seg_ref[h, j] + rank, dest)
    dest_ref[0] = jnp.concatenate(
        [dest, jnp.zeros((SUBLANES - MOBA_TOPK, dest.shape[1]), jnp.int32)], axis=0)


def _dest(seg_start, route):
    H, _, S = route.shape
    return pl.pallas_call(
        _dest_kernel,
        grid_spec=pltpu.PrefetchScalarGridSpec(
            num_scalar_prefetch=1,
            grid=(H,),
            in_specs=[pl.BlockSpec((1, SUBLANES, S), lambda h, seg: (h, 0, 0))],
            out_specs=pl.BlockSpec((1, SUBLANES, S), lambda h, seg: (h, 0, 0)),
        ),
        out_shape=jax.ShapeDtypeStruct((H, SUBLANES, S), jnp.int32),
        compiler_params=_cparams(("parallel",)),
        name="moba_dest",
    )(seg_start, route)


def _sc_mesh():
    return plsc.VectorSubcoreMesh(core_axis_name="c", subcore_axis_name="s")


def _sc_scatter(rows, row0, n_rows, idx, n_out):
    W = rows.shape[1]
    M = idx.shape[0]
    n_win = n_rows // SC_WINDOW
    win0 = row0 // SC_WINDOW
    sub = SC_WINDOW // SC_STREAMS

    @pl.kernel(out_type=jax.ShapeDtypeStruct((n_out, W), rows.dtype), mesh=_sc_mesh(),
               scratch_types=[pltpu.SemaphoreType.DMA((SC_STREAMS,))])
    def k(x_hbm, i_hbm, o_hbm, sems):
        def body(x_vmem, i_vmem):
            copies = [pltpu.make_async_copy(x_vmem.at[pl.ds(j * sub, sub)],
                                            o_hbm.at[i_vmem.at[0, pl.ds(j * sub, sub)]], sems.at[j])
                      for j in range(SC_STREAMS)]
            for c in copies:
                c.start()
            for c in copies:
                c.wait()

        pltpu.emit_pipeline(
            body,
            grid=(M // SC_WINDOW,),
            in_specs=[pl.BlockSpec((SC_WINDOW, W), lambda w: (win0 + lax.rem(w, n_win), 0)),
                      pl.BlockSpec((1, SC_WINDOW), lambda w: (0, w))],
            out_specs=[],
            core_axis_name=("c", "s"),
            dimension_semantics=(pltpu.PARALLEL,),
        )(x_hbm, i_hbm)

    return k(rows, idx.reshape(1, M))


def _sc_gather(table, idx):
    M = idx.shape[0]
    W = table.shape[1]
    sub = SC_WINDOW // SC_STREAMS

    @pl.kernel(out_type=jax.ShapeDtypeStruct((M, W), table.dtype), mesh=_sc_mesh(),
               scratch_types=[pltpu.SemaphoreType.DMA((SC_STREAMS,))])
    def k(x_hbm, i_hbm, o_hbm, sems):
        def body(i_vmem, o_vmem):
            copies = [pltpu.make_async_copy(x_hbm.at[i_vmem.at[0, pl.ds(j * sub, sub)]],
                                            o_vmem.at[pl.ds(j * sub, sub)], sems.at[j])
                      for j in range(SC_STREAMS)]
            for c in copies:
                c.start()
            for c in copies:
                c.wait()

        pltpu.emit_pipeline(
            body,
            grid=(M // SC_WINDOW,),
            in_specs=[pl.BlockSpec((1, SC_WINDOW), lambda w: (0, w))],
            out_specs=[pl.BlockSpec((SC_WINDOW, W), lambda w: (w, 0))],
            core_axis_name=("c", "s"),
            dimension_semantics=(pltpu.PARALLEL,),
        )(i_hbm, o_hbm)

    return k(table, idx.reshape(1, M))


def _group_kernel(tile_blk_ref, q_ref, k_ref, vT_ref, out_ref, s_ref, m_ref):
    h = pl.program_id(0)
    u = pl.program_id(1)
    last = pl.num_programs(1) - 2
    BS = MOBA_BLOCK
    Dh = A_HEAD_DIM
    T = GROUP_TILES_PER_STEP
    ones_rows = jnp.ones((2 * SUBLANES, BS), BF16)
    new = lax.rem(u, 2)
    old = 1 - new

    @pl.when((h == 0) & (u == 0))
    def _():
        s_ref[...] = jnp.zeros_like(s_ref)
        m_ref[...] = jnp.zeros_like(m_ref)

    def block_start(group, c):
        j = jnp.maximum(tile_blk_ref[h * ROUTE_TILES + group * T + c], 0)
        return pl.multiple_of(j * BS, BS)

    g_new = jnp.minimum(u, last)
    g_old = jnp.maximum(u - 1, 0)
    @pl.when(tile_blk_ref[h * ROUTE_TILES + g_old * T] >= 0)
    def _():
        ms, rs = [], []
        for c in range(T):
            m = m_ref[old, c]
            p = jnp.exp2(s_ref[old, c] - m.astype(BF16))
            v_aug = jnp.concatenate([vT_ref[:, pl.ds(block_start(g_old, c), BS)], ones_rows], axis=0)
            ms.append(m)
            rs.append(jnp.dot(v_aug, p, preferred_element_type=F32))
        s_new = []
        for c in range(T):
            qs = (q_ref[c * BS:(c + 1) * BS, :] * (Dh ** -0.5 * LOG2_E)).astype(BF16)
            s_new.append(_dot_nt(k_ref[pl.ds(block_start(g_new, c), BS), :], qs))
        for c in range(T):
            l = rs[c][Dh:Dh + 1, :]
            out_ref[c * BS:(c + 1) * BS, :] = _pack_rows(rs[c][:Dh, :] / l, ms[c] + jnp.log2(l))
            sb = s_new[c].astype(BF16)
            s_ref[new, c] = sb
            m_ref[new, c] = jnp.max(sb, axis=0, keepdims=True).astype(F32)


def _group(tile_blk, q_sorted, ak, avT, h0):
    S = ak.shape[0]
    H = q_sorted.shape[0] // ROUTE_ROWS
    Dh = A_HEAD_DIM
    BS = MOBA_BLOCK
    T = GROUP_TILES_PER_STEP
    steps = ROUTE_TILES // T
    rows = T * BS
    return pl.pallas_call(
        _group_kernel,
        grid_spec=pltpu.PrefetchScalarGridSpec(
            num_scalar_prefetch=1,
            grid=(H, steps + 1),
            in_specs=[
                pl.BlockSpec((rows, LANES), lambda h, u, tb: (h * steps + jnp.minimum(u, steps - 1), 0)),
                pl.BlockSpec((S, Dh), lambda h, u, tb: (0, h0 + h)),
                pl.BlockSpec((Dh, S), lambda h, u, tb: (h0 + h, 0)),
            ],
            out_specs=pl.BlockSpec((rows, LANES), lambda h, u, tb: (h * steps + jnp.maximum(u - 1, 0), 0)),
            scratch_shapes=[pltpu.VMEM((2, T, BS, BS), BF16), pltpu.VMEM((2, T, 1, BS), F32)],
        ),
        out_shape=jax.ShapeDtypeStruct((H * ROUTE_ROWS, LANES), F32),
        compiler_params=_cparams(("arbitrary", "arbitrary")),
        name="moba_group",
    )(tile_blk, q_sorted, ak, avT)


def _combine_kernel(*refs):
    i = pl.program_id(0)
    o_ref = refs[-1]
    n_groups = (len(refs) - 1) // 2
    H = refs[n_groups].shape[0]
    BS = refs[n_groups].shape[1]
    low_half = lax.broadcasted_iota(jnp.int32, (BS, LANES), 1) < ROW_WORDS
    for head in range(n_groups * H):
        got_ref, own_ref, h = refs[head // H], refs[n_groups + head // H], head % H
        tiles = [own_ref[h]]
        for r in range(MOBA_TOPK):
            tiles.append(jnp.where(r < i, got_ref[h, r], 0.0))
        lses = [tiles[0][:, ROW_WORDS:ROW_WORDS + 1]]
        lses += [jnp.where(r < i, tiles[r + 1][:, ROW_WORDS:ROW_WORDS + 1], NEG_BIG)
                 for r in range(MOBA_TOPK)]
        top = functools.reduce(jnp.maximum, lses)
        w = [jnp.exp2(x - top) for x in lses]
        inv = 1.0 / sum(w)
        first = jnp.zeros((BS, LANES), F32)
        second = jnp.zeros((BS, LANES), F32)
        for wk, tile in zip(w, tiles):
            words = pltpu.bitcast(tile, jnp.uint32)
            first = first + wk * pltpu.bitcast(words & jnp.uint32(0xFFFF0000), F32)
            second = second + wk * pltpu.bitcast(words << 16, F32)
        o = jnp.where(low_half, first, pltpu.roll(second, ROW_WORDS, 1)) * inv
        o_ref[:, head * A_HEAD_DIM:(head + 1) * A_HEAD_DIM] = o.astype(BF16)


def _combine(gots, owns):
    H, S, _ = owns[0].shape
    BS = MOBA_BLOCK
    return pl.pallas_call(
        _combine_kernel,
        grid=(S // BS,),
        in_specs=([pl.BlockSpec((H, MOBA_TOPK, BS, LANES), lambda i: (0, 0, i, 0))] * len(gots)
                  + [pl.BlockSpec((H, BS, LANES), lambda i: (0, i, 0))] * len(owns)),
        out_specs=pl.BlockSpec((BS, A_WIDTH), lambda i: (i, 0)),
        out_shape=jax.ShapeDtypeStruct((S, A_WIDTH), BF16),
        compiler_params=_cparams(("parallel",)),
        name="moba_combine",
    )(*gots, *owns)


def _moba_routed(aq_hm, ak, avT, kmean):
    H, S, Dh = aq_hm.shape
    NB = S // MOBA_BLOCK
    hn = MOBA_HEADS_PER_GROUP
    q_rows = aq_hm.reshape(H * S, Dh)
    gots, owns = [], []
    for h0 in range(0, H, hn):
        route, cnt, own = _route(aq_hm, ak, avT, kmean, h0, hn)
        cnt = cnt[:, :, 0].astype(jnp.int32)
        seg_tiles = (cnt + MOBA_BLOCK - 1) // MOBA_BLOCK
        seg_end = jnp.cumsum(seg_tiles, axis=1)
        seg_start = (seg_end - seg_tiles) * MOBA_BLOCK
        tile_ids = jnp.arange(ROUTE_TILES, dtype=jnp.int32)
        tile_blk = jnp.sum(tile_ids[None, :, None] >= seg_end[:, None, :], axis=2).astype(jnp.int32)
        tile_blk = jnp.where(tile_blk < NB, tile_blk, -1).reshape(hn * ROUTE_TILES)
        dest = _dest(seg_start, route)[:, :MOBA_TOPK, :]
        q_sorted = _sc_scatter(q_rows, h0 * S, hn * S, dest.transpose(1, 0, 2).reshape(-1),
                               hn * ROUTE_ROWS)
        results = _group(tile_blk, q_sorted, ak, avT, h0)
        gots.append(_sc_gather(results, dest.reshape(-1)).reshape(hn, MOBA_TOPK, S, LANES))
        owns.append(own)
    return _combine(gots, owns)


def _merge_kernel(hm_ref, ha_ref, gm_ref, ga_ref, x_ref, wm_ref, wa_ref, wo_ref, gain_ref,
                  out_ref):
    ym = jnp.dot(hm_ref[...], wm_ref[...], preferred_element_type=F32)
    ya = jnp.dot(ha_ref[...], wa_ref[...], preferred_element_type=F32)
    merged = _sigmoid(gm_ref[...].astype(F32)) * ym + _sigmoid(ga_ref[...].astype(F32)) * ya
    mix = jnp.dot(merged.astype(BF16), wo_ref[...], preferred_element_type=F32)
    out_ref[...] = x_ref[...] + _rms(mix, gain_ref[...])


def _merge(hm, ha, P, x, b, wm, wa, wo, gain, tm=256):
    _, S, D = x.shape
    const = pl.Buffered(1)
    return pl.pallas_call(
        _merge_kernel,
        grid=(S // tm,),
        in_specs=[
            pl.BlockSpec((tm, M_WIDTH), lambda i: (i, 0)),
            pl.BlockSpec((tm, A_WIDTH), lambda i: (i, 0)),
            pl.BlockSpec((tm, D), lambda i: (i, COL_GM // D)),
            pl.BlockSpec((tm, D), lambda i: (i, COL_GA // D)),
            pl.BlockSpec((None, tm, D), lambda i: (b, i, 0)),
            pl.BlockSpec((M_WIDTH, D), lambda i: (0, 0), pipeline_mode=const),
            pl.BlockSpec((A_WIDTH, D), lambda i: (0, 0), pipeline_mode=const),
            pl.BlockSpec((D, D), lambda i: (0, 0), pipeline_mode=const),
            pl.BlockSpec((1, D), lambda i: (0, 0)),
        ],
        out_specs=pl.BlockSpec((tm, D), lambda i: (i, 0)),
        out_shape=jax.ShapeDtypeStruct((S, D), F32),
        compiler_params=_cparams(("parallel",)),
        name="merge",
    )(hm, ha, P, P, x, wm, wa, wo, gain)


def _ffn_kernel(x_ref, gpre_ref, wu_ref, wd_ref, gpost_ref, out_ref, hn_ref, acc_ref):
    f = pl.program_id(1)

    @pl.when(f == 0)
    def _():
        hn_ref[...] = _rms(x_ref[...], gpre_ref[...]).astype(BF16)
        acc_ref[...] = jnp.zeros_like(acc_ref)

    u = jnp.dot(hn_ref[...], wu_ref[...], preferred_element_type=F32)
    u = jnp.square(jnp.maximum(u, 0.0)).astype(BF16)
    acc_ref[...] += jnp.dot(u, wd_ref[...], preferred_element_type=F32)

    @pl.when(f == pl.num_programs(1) - 1)
    def _():
        out_ref[...] = x_ref[...] + _rms(acc_ref[...], gpost_ref[...])


def _ffn(x, gpre, wu, wd, gpost, tm=512, tf=1024):
    S, D = x.shape
    Fd = wu.shape[1]
    return pl.pallas_call(
        _ffn_kernel,
        grid=(S // tm, Fd // tf),
        in_specs=[
            pl.BlockSpec((tm, D), lambda i, f: (i, 0)),
            pl.BlockSpec((1, D), lambda i, f: (0, 0)),
            pl.BlockSpec((D, tf), lambda i, f: (0, f)),
            pl.BlockSpec((tf, D), lambda i, f: (f, 0)),
            pl.BlockSpec((1, D), lambda i, f: (0, 0)),
        ],
        out_specs=pl.BlockSpec((tm, D), lambda i, f: (i, 0)),
        out_shape=jax.ShapeDtypeStruct((S, D), F32),
        scratch_shapes=[pltpu.VMEM((tm, D), BF16), pltpu.VMEM((tm, D), F32)],
        compiler_params=_cparams(("parallel", "arbitrary")),
        name="ffn",
    )(x, gpre, wu, wd, gpost)


def _layer(x, b, pos, norm_mix_pre, w_in, conv_w, conv_b, i_bias, f_bias, mlstm_norm,
           w_branch_m, w_branch_a, w_out, norm_mix_post, norm_ffn_pre, w_up, w_down,
           norm_ffn_post):
    _, S, D = x.shape
    o = 0
    pieces = {}
    w_in = w_in.astype(BF16)
    for name, width in (("mq", M_WIDTH), ("mk", M_WIDTH), ("mv", M_WIDTH), ("mo", M_WIDTH),
                        ("mi", M_HEADS), ("mf", M_HEADS), ("aq", A_WIDTH), ("ak", A_WIDTH),
                        ("av", A_WIDTH), ("gm", D), ("ga", D)):
        pieces[name] = w_in[:, o:o + width]
        o += width
    w_all = jnp.concatenate([pieces[n] for n in ("gm", "ga", "mq", "mk", "aq", "ak", "mv", "mo", "av")],
                            axis=1)
    w_gate = jnp.concatenate(
        [pieces["mi"], pieces["mf"], jnp.zeros((D, LANES - 2 * M_HEADS), BF16)], axis=1)
    gbias = jnp.concatenate([i_bias, f_bias, jnp.zeros((LANES - 2 * M_HEADS,), F32)])[None, :]
    half = jnp.arange(0, A_HEAD_DIM, 2, dtype=F32) / A_HEAD_DIM
    inv_freq = 1.0 / (ROPE_THETA ** half)
    invf = jnp.concatenate([inv_freq, inv_freq])[None, :]

    P, gate = _proj(x, b, norm_mix_pre[None, :], w_all, w_gate)
    mq, mkT, aq, ak, avT, kmean, gc, gt = _prep(
        P, gate, pos.reshape(S, 1), invf, conv_w, conv_b[None, :], gbias)
    hm = _mlstm(mq, mkT, P, gc, gt, mlstm_norm[None, :])
    ha = _moba_routed(aq, ak, avT, kmean.reshape(S // MOBA_BLOCK, A_WIDTH))
    x1 = _merge(hm, ha, P, x, b, w_branch_m.astype(BF16), w_branch_a.astype(BF16),
                w_out.astype(BF16), norm_mix_post[None, :])
    return _ffn(x1, norm_ffn_pre[None, :], w_up.astype(BF16), w_down.astype(BF16),
                norm_ffn_post[None, :])


def kernel(x, positions, norm_mix_pre, w_in, conv_w, conv_b, i_bias, f_bias, mlstm_norm,
           w_branch_m, w_branch_a, w_out, norm_mix_post, norm_ffn_pre, w_up, w_down,
           norm_ffn_post):
    B = x.shape[0]
    depth = w_in.shape[0]
    outs = []
    def take(a, i):
        return a.reshape(a.shape[1:]) if a.shape[0] == 1 else a[i]

    for b in range(B):
        xin, bi = x, b
        for l in range(depth):
            xb = _layer(xin, bi, take(positions, b), take(norm_mix_pre, l), take(w_in, l),
                        take(conv_w, l), take(conv_b, l), take(i_bias, l), take(f_bias, l),
                        take(mlstm_norm, l), take(w_branch_m, l), take(w_branch_a, l),
                        take(w_out, l), take(norm_mix_post, l), take(norm_ffn_pre, l),
                        take(w_up, l), take(w_down, l), take(norm_ffn_post, l))
            xin, bi = xb[None], 0
        outs.append(xb)
    return outs[0][None] if B == 1 else jnp.stack(outs, axis=0)
```

```python
import functools

import jax
import jax.numpy as jnp
from jax import lax
from jax.experimental import pallas as pl
from jax.experimental.pallas import tpu as pltpu
from jax.experimental.pallas import tpu_sc as plsc

F32 = jnp.float32
BF16 = jnp.bfloat16

M_HEADS = 4
M_HEAD_DIM = 256
M_WIDTH = M_HEADS * M_HEAD_DIM
M_CHUNK = 128
CONV_WIDTH = 4
A_HEADS = 8
A_HEAD_DIM = 128
A_WIDTH = A_HEADS * A_HEAD_DIM
MOBA_BLOCK = 256
MOBA_TOPK = 3
ROPE_THETA = 10000.0
NORM_EPS = 1e-6

LANES = 128
SUBLANES = 8
VMEM_LIMIT = 56 * 1024 * 1024
NEG_BIG = -1e30
LOG2_E = 1.4426950408889634

COL_GM = 0
COL_GA = 2048
COL_MQK = 4096
COL_AQK = 6144
COL_MV = 8192
COL_MO = 9216
COL_AV = 10240


def _cparams(sem):
    return pltpu.CompilerParams(dimension_semantics=sem, vmem_limit_bytes=VMEM_LIMIT)


def _rms(x, gain):
    ms = jnp.mean(x * x, axis=-1, keepdims=True)
    return x * lax.rsqrt(ms + NORM_EPS) * gain


def _sigmoid(x):
    return 1.0 / (1.0 + jnp.exp(-x))


def _split3(x):
    hi = x.astype(BF16)
    r1 = x - hi.astype(F32)
    mid = r1.astype(BF16)
    lo = (r1 - mid.astype(F32)).astype(BF16)
    return hi, mid, lo


def _proj_kernel(x_ref, g_ref, w_ref, wg_ref, p_ref, gate_ref, xn_ref):
    @pl.when(pl.program_id(1) == 0)
    def _():
        xn = _rms(x_ref[...], g_ref[...]).astype(BF16)
        xn_ref[...] = xn
        gate_ref[...] = jnp.dot(xn, wg_ref[...], preferred_element_type=F32)

    p_ref[...] = jnp.dot(xn_ref[...], w_ref[...], preferred_element_type=F32).astype(BF16)


def _proj(x, b, gain, w_all, w_gate, tm=1024, tn=1024):
    _, S, D = x.shape
    N = w_all.shape[1]
    return pl.pallas_call(
        _proj_kernel,
        grid=(S // tm, N // tn),
        in_specs=[
            pl.BlockSpec((None, tm, D), lambda i, j: (b, i, 0)),
            pl.BlockSpec((1, D), lambda i, j: (0, 0)),
            pl.BlockSpec((D, tn), lambda i, j: (0, j)),
            pl.BlockSpec((D, LANES), lambda i, j: (0, 0)),
        ],
        out_specs=[
            pl.BlockSpec((tm, tn), lambda i, j: (i, j)),
            pl.BlockSpec((tm, LANES), lambda i, j: (i, 0)),
        ],
        out_shape=[
            jax.ShapeDtypeStruct((S, N), BF16),
            jax.ShapeDtypeStruct((S, LANES), F32),
        ],
        scratch_shapes=[pltpu.VMEM((tm, D), BF16)],
        compiler_params=_cparams(("parallel", "arbitrary")),
        name="proj",
    )(x, gain, w_all, w_gate)


PREP_ROWS = MOBA_BLOCK
PREP_COLS = 512
HALO_ROWS = 2 * SUBLANES


def _prep_kernel(pmk_ref, halo_ref, paqk_ref, pav_ref, gate_ref, pos_ref, invf_ref,
                 cw_ref, cb_ref, gb_ref,
                 mq_ref, mkT_ref, aq_ref, ak_ref, avT_ref, kmean_ref, gc_ref, gt_ref):
    i = pl.program_id(0)
    R = PREP_ROWS

    k_scale = M_HEAD_DIM ** -0.5
    for c0 in range(0, 2 * M_WIDTH, PREP_COLS):
        cs = slice(c0, c0 + PREP_COLS)
        prev = halo_ref[:, cs].astype(F32)[HALO_ROWS - SUBLANES:, :]
        prev = jnp.where(i == 0, jnp.zeros_like(prev), prev)
        ext = jnp.concatenate([pmk_ref[:, cs].astype(F32), prev], axis=0)
        acc = cw_ref[0:1, cs] * ext
        for j in range(1, CONV_WIDTH):
            acc = pltpu.roll(acc, 1, 0) + cw_ref[j:j + 1, cs] * ext
        acc = acc[0:R, :] + cb_ref[:, cs]
        y = acc * _sigmoid(acc)
        if c0 < M_WIDTH:
            mq_ref[:, cs] = y.astype(BF16)
        else:
            ks = slice(c0 - M_WIDTH, c0 - M_WIDTH + PREP_COLS)
            mkT_ref[ks, :] = (y * k_scale).T.astype(BF16)

    ang = pos_ref[...].astype(F32) * invf_ref[...]
    cos = jnp.cos(ang)
    lane = lax.broadcasted_iota(jnp.int32, (R, A_HEAD_DIM), 1)
    sin_signed = jnp.where(lane < A_HEAD_DIM // 2, -1.0, 1.0) * jnp.sin(ang)
    for h in range(2 * A_HEADS):
        hs = slice(h * A_HEAD_DIM, (h + 1) * A_HEAD_DIM)
        xh = paqk_ref[:, hs].astype(F32)
        yh = xh * cos + pltpu.roll(xh, A_HEAD_DIM // 2, 1) * sin_signed
        if h < A_HEADS:
            aq_ref[h] = yh
        else:
            ko = slice((h - A_HEADS) * A_HEAD_DIM, (h - A_HEADS + 1) * A_HEAD_DIM)
            ak_ref[:, ko] = yh.astype(BF16)
            kmean_ref[0, :, ko] = jnp.mean(yh, axis=0, keepdims=True)
    for c0 in range(0, A_WIDTH, PREP_COLS):
        avT_ref[c0:c0 + PREP_COLS, :] = pav_ref[:, c0:c0 + PREP_COLS].astype(F32).T.astype(BF16)

    g = gate_ref[...] + gb_ref[...]
    log_f = jnp.minimum(g, 0.0) - jnp.log1p(jnp.exp(-jnp.abs(g)))
    r_i = lax.broadcasted_iota(jnp.int32, (R, R), 0)
    c_i = lax.broadcasted_iota(jnp.int32, (R, R), 1)
    tri = ((r_i >= c_i) & ((r_i // M_CHUNK) == (c_i // M_CHUNK))).astype(BF16)
    hi, mid, lo = _split3(log_f)
    csum = (jnp.dot(tri, hi, preferred_element_type=F32)
            + jnp.dot(tri, mid, preferred_element_type=F32)
            + jnp.dot(tri, lo, preferred_element_type=F32))
    glane = lax.broadcasted_iota(jnp.int32, (R, LANES), 1)
    gc = jnp.where(glane < M_HEADS, g, csum)
    gc_ref[...] = gc
    gt_ref[...] = gc.T[0:SUBLANES, :]


def _prep(P, gate, pos, invf, conv_w, conv_b, gbias):
    S = P.shape[0]
    R = PREP_ROWS
    nb = S // R
    halo_blocks = R // HALO_ROWS
    return pl.pallas_call(
        _prep_kernel,
        grid=(nb,),
        in_specs=[
            pl.BlockSpec((R, 2 * M_WIDTH), lambda i: (i, COL_MQK // (2 * M_WIDTH))),
            pl.BlockSpec((HALO_ROWS, 2 * M_WIDTH),
                         lambda i: (jnp.maximum(i * halo_blocks - 1, 0), COL_MQK // (2 * M_WIDTH))),
            pl.BlockSpec((R, 2 * A_WIDTH), lambda i: (i, COL_AQK // (2 * A_WIDTH))),
            pl.BlockSpec((R, A_WIDTH), lambda i: (i, COL_AV // A_WIDTH)),
            pl.BlockSpec((R, LANES), lambda i: (i, 0)),
            pl.BlockSpec((R, 1), lambda i: (i, 0)),
            pl.BlockSpec((1, A_HEAD_DIM), lambda i: (0, 0)),
            pl.BlockSpec((CONV_WIDTH, 2 * M_WIDTH), lambda i: (0, 0)),
            pl.BlockSpec((1, 2 * M_WIDTH), lambda i: (0, 0)),
            pl.BlockSpec((1, LANES), lambda i: (0, 0)),
        ],
        out_specs=[
            pl.BlockSpec((R, M_WIDTH), lambda i: (i, 0)),
            pl.BlockSpec((M_WIDTH, R), lambda i: (0, i)),
            pl.BlockSpec((A_HEADS, R, A_HEAD_DIM), lambda i: (0, i, 0)),
            pl.BlockSpec((R, A_WIDTH), lambda i: (i, 0)),
            pl.BlockSpec((A_WIDTH, R), lambda i: (0, i)),
            pl.BlockSpec((1, 1, A_WIDTH), lambda i: (i, 0, 0)),
            pl.BlockSpec((R, LANES), lambda i: (i, 0)),
            pl.BlockSpec((SUBLANES, R), lambda i: (0, i)),
        ],
        out_shape=[
            jax.ShapeDtypeStruct((S, M_WIDTH), BF16),
            jax.ShapeDtypeStruct((M_WIDTH, S), BF16),
            jax.ShapeDtypeStruct((A_HEADS, S, A_HEAD_DIM), F32),
            jax.ShapeDtypeStruct((S, A_WIDTH), BF16),
            jax.ShapeDtypeStruct((A_WIDTH, S), BF16),
            jax.ShapeDtypeStruct((nb, 1, A_WIDTH), F32),
            jax.ShapeDtypeStruct((S, LANES), F32),
            jax.ShapeDtypeStruct((SUBLANES, S), F32),
        ],
        compiler_params=_cparams(("parallel",)),
        name="prep",
    )(P, P, P, P, gate, pos, invf, conv_w, conv_b, gbias)


M_AUG = M_HEAD_DIM + LANES
MLSTM_CHUNKS_PER_STEP = 1


def _mlstm_kernel(q_ref, kT_ref, v_ref, mo_ref, gc_ref, gt_ref, gain_ref, out_ref,
                  c_ref, m_ref):
    @pl.when(pl.program_id(0) == 0)
    def _():
        c_ref[...] = jnp.zeros_like(c_ref)
        m_ref[...] = jnp.zeros_like(m_ref)

    L = M_CHUNK
    D = M_HEAD_DIM
    row = lax.broadcasted_iota(jnp.int32, (L, L), 0)
    col = lax.broadcasted_iota(jnp.int32, (L, L), 1)
    causal = row >= col
    ones_col = (lax.broadcasted_iota(jnp.int32, (L, LANES), 1) == 0).astype(BF16)

    heads = range(M_HEADS)
    hsl = [slice(h * D, (h + 1) * D) for h in heads]

    def state_stages(rs):
        q = [q_ref[rs, hsl[h]] for h in heads]
        kT = [kT_ref[hsl[h], rs] for h in heads]
        v_aug = [jnp.concatenate([v_ref[rs, hsl[h]], ones_col], axis=1) for h in heads]
        b_c = [gc_ref[rs, M_HEADS + h:M_HEADS + h + 1] for h in heads]
        b_r = [gt_ref[M_HEADS + h:M_HEADS + h + 1, rs] for h in heads]
        u_r = [gt_ref[h:h + 1, rs] - b_r[h] for h in heads]
        f_tot = [b_r[h][:, L - 1:L] for h in heads]
        m_prev = [m_ref[h:h + 1, 0:1] for h in heads]

        s_qk = [jnp.dot(q[h], kT[h], preferred_element_type=F32) for h in heads]
        q_c = [jnp.dot(q[h], c_ref[h].astype(BF16), preferred_element_type=F32) for h in heads]

        for h in heads:
            w_r = f_tot[h] + u_r[h]
            m_loc = jnp.max(w_r, axis=1, keepdims=True)
            m_new = jnp.maximum(f_tot[h] + m_prev[h], m_loc)
            a = jnp.exp(f_tot[h] + m_prev[h] - m_new)
            e_r = jnp.exp(w_r - m_new)
            keT = (kT[h].astype(F32) * e_r).astype(BF16)
            c_ref[h] = a * c_ref[h] + jnp.dot(keT, v_aug[h], preferred_element_type=F32)
            m_ref[h:h + 1, :] = jnp.broadcast_to(m_new, (1, LANES))
        return rs, v_aug, b_c, u_r, m_prev, s_qk, q_c

    def output_stage(rs, v_aug, b_c, u_r, m_prev, s_qk, q_c):
        for h in heads:
            d_log = jnp.where(causal, b_c[h] + u_r[h], -jnp.inf)
            a_log = b_c[h] + m_prev[h]
            m_t = jnp.maximum(a_log, jnp.max(d_log, axis=1, keepdims=True))
            s_ts = s_qk[h] * jnp.exp(d_log - m_t)
            inter = jnp.exp(a_log - m_t)
            r = inter * q_c[h] + jnp.dot(s_ts.astype(BF16), v_aug[h], preferred_element_type=F32)
            num = r[:, :D]
            den = r[:, D:D + 1]
            hh = num / jnp.maximum(jnp.abs(den), jnp.exp(-m_t))
            hn = hh * lax.rsqrt(jnp.mean(hh * hh, axis=-1, keepdims=True) + NORM_EPS)
            out_ref[rs, hsl[h]] = (hn * gain_ref[:, hsl[h]]
                                   * _sigmoid(mo_ref[rs, hsl[h]].astype(F32))).astype(BF16)

    pending = [state_stages(slice(c * L, (c + 1) * L)) for c in range(MLSTM_CHUNKS_PER_STEP)]
    for vals in pending:
        output_stage(*vals)


def _mlstm(mq, mkT, P, gc, gt, gain):
    S = mq.shape[0]
    L = M_CHUNK * MLSTM_CHUNKS_PER_STEP
    return pl.pallas_call(
        _mlstm_kernel,
        grid=(S // L,),
        in_specs=[
            pl.BlockSpec((L, M_WIDTH), lambda c: (c, 0)),
            pl.BlockSpec((M_WIDTH, L), lambda c: (0, c)),
            pl.BlockSpec((L, M_WIDTH), lambda c: (c, COL_MV // M_WIDTH)),
            pl.BlockSpec((L, M_WIDTH), lambda c: (c, COL_MO // M_WIDTH)),
            pl.BlockSpec((L, LANES), lambda c: (c, 0)),
            pl.BlockSpec((SUBLANES, L), lambda c: (0, c)),
            pl.BlockSpec((1, M_WIDTH), lambda c: (0, 0)),
        ],
        out_specs=pl.BlockSpec((L, M_WIDTH), lambda c: (c, 0)),
        out_shape=jax.ShapeDtypeStruct((S, M_WIDTH), BF16),
        scratch_shapes=[
            pltpu.VMEM((M_HEADS, M_HEAD_DIM, M_AUG), F32),
            pltpu.VMEM((SUBLANES, LANES), F32),
        ],
        compiler_params=_cparams(("arbitrary",)),
        name="mlstm",
    )(mq, mkT, P, P, gc, gt, gain)


MOBA_HEADS_PER_GROUP = 4
ROUTE_TILES = 264
ROUTE_ROWS = ROUTE_TILES * MOBA_BLOCK
GROUP_TILES_PER_STEP = 24
ROW_WORDS = A_HEAD_DIM // 2
SC_WINDOW = 256


def _pack_rows(o_t, lse):
    q = o_t.shape[1]
    hi = pltpu.bitcast(o_t[:ROW_WORDS].astype(BF16).astype(F32), jnp.uint32)
    lo = pltpu.bitcast(o_t[ROW_WORDS:].astype(BF16).astype(F32), jnp.uint32)
    words = pltpu.bitcast(hi | (lo >> 16), F32)
    tail = jnp.concatenate([jnp.broadcast_to(lse, (SUBLANES, q)),
                            jnp.zeros((LANES - ROW_WORDS - SUBLANES, q), F32)], axis=0)
    return jnp.concatenate([words, tail], axis=0).T


def _dot_nt(a, b):
    return lax.dot_general(a, b, (((1,), (1,)), ((), ())), preferred_element_type=F32)


def _block_partial(k_blk, v_t_blk, qs, ones_rows, causal_mask=None):
    s = _dot_nt(k_blk, qs)
    if causal_mask is not None:
        s = jnp.where(causal_mask, s, NEG_BIG)
    sb = s.astype(BF16)
    m = jnp.max(sb, axis=0, keepdims=True)
    p = jnp.exp2(sb - m)
    r = jnp.dot(jnp.concatenate([v_t_blk, ones_rows], axis=0), p, preferred_element_type=F32)
    dh = v_t_blk.shape[0]
    l = r[dh:dh + 1, :]
    return r[:dh, :] / l, m.astype(F32) + jnp.log2(l)


def _own_kernel(q_ref, k_ref, vT_ref, own_ref):
    BS = MOBA_BLOCK
    Dh = A_HEAD_DIM
    G = q_ref.shape[0]
    kpos = lax.broadcasted_iota(jnp.int32, (BS, BS), 0)
    qpos = lax.broadcasted_iota(jnp.int32, (BS, BS), 1)
    ones_rows = jnp.ones((2 * SUBLANES, BS), BF16)
    qs = [(q_ref[g] * (Dh ** -0.5 * LOG2_E)).astype(BF16) for g in range(G)]
    parts = [_block_partial(k_ref[:, g * Dh:(g + 1) * Dh], vT_ref[g * Dh:(g + 1) * Dh, :], qs[g],
                            ones_rows, kpos <= qpos) for g in range(G)]
    for g in range(G):
        own_ref[g] = _pack_rows(*parts[g])


def _own(aq_hm, ak, avT):
    H, S, Dh = aq_hm.shape
    BS = MOBA_BLOCK
    G = MOBA_HEADS_PER_GROUP
    W = G * Dh
    return pl.pallas_call(
        _own_kernel,
        grid=(H // G, S // BS),
        in_specs=[
            pl.BlockSpec((G, BS, Dh), lambda h, i: (h, i, 0)),
            pl.BlockSpec((BS, W), lambda h, i: (i, h)),
            pl.BlockSpec((W, BS), lambda h, i: (h, i)),
        ],
        out_specs=pl.BlockSpec((G, BS, LANES), lambda h, i: (h, i, 0)),
        out_shape=jax.ShapeDtypeStruct((H, S, LANES), F32),
        compiler_params=_cparams(("parallel", "parallel")),
        name="moba_own",
    )(aq_hm, ak, avT)


def _route_kernel(q_ref, km_ref, route_ref, cnt_ref, cnt_acc):
    i = pl.program_id(1)
    BS = MOBA_BLOCK
    Dh = A_HEAD_DIM
    NB = km_ref.shape[0]
    G = q_ref.shape[0]
    blk = lax.broadcasted_iota(jnp.int32, (NB, BS), 0)
    kpos = lax.broadcasted_iota(jnp.int32, (BS, BS), 0)
    qpos = lax.broadcasted_iota(jnp.int32, (BS, BS), 1)
    earlier = (kpos < qpos).astype(BF16)

    @pl.when(i == 0)
    def _():
        cnt_acc[...] = jnp.zeros_like(cnt_acc)

    q = [q_ref[g] for g in range(G)]

    gates = []
    for g in range(G):
        hs = slice(g * Dh, (g + 1) * Dh)
        km = km_ref[:, hs]
        kh = km.astype(BF16)
        kl = (km - kh.astype(F32)).astype(BF16)
        qh = q[g].astype(BF16)
        ql = (q[g] - qh.astype(F32)).astype(BF16)
        gate = _dot_nt(kh, qh) + _dot_nt(kh, ql) + _dot_nt(kl, qh)
        gates.append(jnp.where(blk < i, gate, -jnp.inf))
    picks = [[] for _ in range(G)]
    rows = [[] for _ in range(G)]
    for r in range(MOBA_TOPK):
        for g in range(G):
            mx = jnp.max(gates[g], axis=0, keepdims=True)
            idx = jnp.min(jnp.where(gates[g] == mx, blk, NB), axis=0, keepdims=True)
            idx = jnp.where(r < i, idx, -1)
            pick = blk == idx
            gates[g] = jnp.where(pick, -jnp.inf, gates[g])
            picks[g].append(pick)
            rows[g].append(idx)
    for g in range(G):
        onehot = sum(p.astype(F32) for p in picks[g])
        before = cnt_acc[g][:, 0:1] + jnp.dot(onehot.astype(BF16), earlier, preferred_element_type=F32)
        for r in range(MOBA_TOPK):
            rank = jnp.sum(jnp.where(picks[g][r], before, 0.0), axis=0, keepdims=True)
            rows[g].append(rank.astype(jnp.int32))
        rows[g].append(jnp.zeros((SUBLANES - 2 * MOBA_TOPK, BS), jnp.int32))
        route_ref[g] = jnp.concatenate(rows[g], axis=0)
        cnt_new = cnt_acc[g] + jnp.sum(onehot, axis=1, keepdims=True)
        cnt_acc[g] = cnt_new
        cnt_ref[g] = cnt_new


def _route(aq_hm, kmean, h0, H):
    _, S, Dh = aq_hm.shape
    BS = MOBA_BLOCK
    NB = S // BS
    G = MOBA_HEADS_PER_GROUP
    W = G * Dh
    hb = h0 // G
    return pl.pallas_call(
        _route_kernel,
        grid=(H // G, NB),
        in_specs=[
            pl.BlockSpec((G, BS, Dh), lambda h, i: (hb + h, i, 0)),
            pl.BlockSpec((NB, W), lambda h, i: (0, hb + h)),
        ],
        out_specs=[
            pl.BlockSpec((G, SUBLANES, BS), lambda h, i: (h, 0, i)),
            pl.BlockSpec((G, NB, LANES), lambda h, i: (h, 0, 0)),
        ],
        out_shape=[
            jax.ShapeDtypeStruct((H, SUBLANES, S), jnp.int32),
            jax.ShapeDtypeStruct((H, NB, LANES), F32),
        ],
        scratch_shapes=[pltpu.VMEM((G, NB, LANES), F32)],
        compiler_params=_cparams(("parallel", "arbitrary")),
        name="moba_route",
    )(aq_hm, kmean)


def _dest_kernel(seg_ref, route_ref, dest_ref):
    h = pl.program_id(0)
    NB = seg_ref.shape[1]
    r = route_ref[0]
    blk = r[0:MOBA_TOPK, :]
    rank = r[MOBA_TOPK:2 * MOBA_TOPK, :]
    base = h * ROUTE_ROWS
    dest = jnp.full(blk.shape, base + ROUTE_ROWS - 1, jnp.int32)
    for j in range(NB):
        dest = jnp.where(blk == j, base + seg_ref[h, j] + rank, dest)
    dest_ref[0] = jnp.concatenate(
        [dest, jnp.zeros((SUBLANES - MOBA_TOPK, dest.shape[1]), jnp.int32)], axis=0)


def _dest(seg_start, route):
    H, _, S = route.shape
    return pl.pallas_call(
        _dest_kernel,
        grid_spec=pltpu.PrefetchScalarGridSpec(
            num_scalar_prefetch=1,
            grid=(H,),
            in_specs=[pl.BlockSpec((1, SUBLANES, S), lambda h, seg: (h, 0, 0))],
            out_specs=pl.BlockSpec((1, SUBLANES, S), lambda h, seg: (h, 0, 0)),
        ),
        out_shape=jax.ShapeDtypeStruct((H, SUBLANES, S), jnp.int32),
        compiler_params=_cparams(("parallel",)),
        name="moba_dest",
    )(seg_start, route)


def _sc_mesh():
    return plsc.VectorSubcoreMesh(core_axis_name="c", subcore_axis_name="s")


def _sc_scatter(rows, row0, n_rows, idx, n_out):
    W = rows.shape[1]
    M = idx.shape[0]
    n_win = n_rows // SC_WINDOW
    win0 = row0 // SC_WINDOW

    @pl.kernel(out_type=jax.ShapeDtypeStruct((n_out, W), rows.dtype), mesh=_sc_mesh())
    def k(x_hbm, i_hbm, o_hbm):
        def body(x_vmem, i_vmem):
            pltpu.sync_copy(x_vmem, o_hbm.at[i_vmem.at[0]])

        pltpu.emit_pipeline(
            body,
            grid=(M // SC_WINDOW,),
            in_specs=[pl.BlockSpec((SC_WINDOW, W), lambda w: (win0 + lax.rem(w, n_win), 0)),
                      pl.BlockSpec((1, SC_WINDOW), lambda w: (0, w))],
            out_specs=[],
            core_axis_name=("c", "s"),
            dimension_semantics=(pltpu.PARALLEL,),
        )(x_hbm, i_hbm)

    return k(rows, idx.reshape(1, M))


def _sc_gather(table, idx):
    M = idx.shape[0]
    W = table.shape[1]

    @pl.kernel(out_type=jax.ShapeDtypeStruct((M, W), table.dtype), mesh=_sc_mesh())
    def k(x_hbm, i_hbm, o_hbm):
        def body(i_vmem, o_vmem):
            pltpu.sync_copy(x_hbm.at[i_vmem.at[0]], o_vmem)

        pltpu.emit_pipeline(
            body,
            grid=(M // SC_WINDOW,),
            in_specs=[pl.BlockSpec((1, SC_WINDOW), lambda w: (0, w))],
            out_specs=[pl.BlockSpec((SC_WINDOW, W), lambda w: (w, 0))],
            core_axis_name=("c", "s"),
            dimension_semantics=(pltpu.PARALLEL,),
        )(i_hbm, o_hbm)

    return k(table, idx.reshape(1, M))


def _group_kernel(tile_blk_ref, q_ref, k_ref, vT_ref, out_ref, s_ref, m_ref):
    h = pl.program_id(0)
    u = pl.program_id(1)
    last = pl.num_programs(1) - 2
    BS = MOBA_BLOCK
    Dh = A_HEAD_DIM
    T = GROUP_TILES_PER_STEP
    ones_rows = jnp.ones((2 * SUBLANES, BS), BF16)
    new = lax.rem(u, 2)
    old = 1 - new

    @pl.when((h == 0) & (u == 0))
    def _():
        s_ref[...] = jnp.zeros_like(s_ref)
        m_ref[...] = jnp.zeros_like(m_ref)

    def block_start(group, c):
        j = jnp.maximum(tile_blk_ref[h * ROUTE_TILES + group * T + c], 0)
        return pl.multiple_of(j * BS, BS)

    g_new = jnp.minimum(u, last)
    g_old = jnp.maximum(u - 1, 0)
    @pl.when(tile_blk_ref[h * ROUTE_TILES + g_old * T] >= 0)
    def _():
        ms, rs = [], []
        for c in range(T):
            m = m_ref[old, c]
            p = jnp.exp2(s_ref[old, c] - m.astype(BF16))
            v_aug = jnp.concatenate([vT_ref[:, pl.ds(block_start(g_old, c), BS)], ones_rows], axis=0)
            ms.append(m)
            rs.append(jnp.dot(v_aug, p, preferred_element_type=F32))
        s_new = []
        for c in range(T):
            qs = (q_ref[c * BS:(c + 1) * BS, :] * (Dh ** -0.5 * LOG2_E)).astype(BF16)
            s_new.append(_dot_nt(k_ref[pl.ds(block_start(g_new, c), BS), :], qs))
        for c in range(T):
            l = rs[c][Dh:Dh + 1, :]
            out_ref[c * BS:(c + 1) * BS, :] = _pack_rows(rs[c][:Dh, :] / l, ms[c] + jnp.log2(l))
            sb = s_new[c].astype(BF16)
            s_ref[new, c] = sb
            m_ref[new, c] = jnp.max(sb, axis=0, keepdims=True).astype(F32)


def _group(tile_blk, q_sorted, ak, avT, h0):
    S = ak.shape[0]
    H = q_sorted.shape[0] // ROUTE_ROWS
    Dh = A_HEAD_DIM
    BS = MOBA_BLOCK
    T = GROUP_TILES_PER_STEP
    steps = ROUTE_TILES // T
    rows = T * BS
    return pl.pallas_call(
        _group_kernel,
        grid_spec=pltpu.PrefetchScalarGridSpec(
            num_scalar_prefetch=1,
            grid=(H, steps + 1),
            in_specs=[
                pl.BlockSpec((rows, LANES), lambda h, u, tb: (h * steps + jnp.minimum(u, steps - 1), 0)),
                pl.BlockSpec((S, Dh), lambda h, u, tb: (0, h0 + h)),
                pl.BlockSpec((Dh, S), lambda h, u, tb: (h0 + h, 0)),
            ],
            out_specs=pl.BlockSpec((rows, LANES), lambda h, u, tb: (h * steps + jnp.maximum(u - 1, 0), 0)),
            scratch_shapes=[pltpu.VMEM((2, T, BS, BS), BF16), pltpu.VMEM((2, T, 1, BS), F32)],
        ),
        out_shape=jax.ShapeDtypeStruct((H * ROUTE_ROWS, LANES), F32),
        compiler_params=_cparams(("arbitrary", "arbitrary")),
        name="moba_group",
    )(tile_blk, q_sorted, ak, avT)


def _combine_kernel(*refs):
    i = pl.program_id(0)
    own_ref, o_ref = refs[-2], refs[-1]
    n_groups = len(refs) - 2
    H = refs[0].shape[0]
    BS = own_ref.shape[1]
    low_half = lax.broadcasted_iota(jnp.int32, (BS, LANES), 1) < ROW_WORDS
    for head in range(n_groups * H):
        got_ref, h = refs[head // H], head % H
        tiles = [own_ref[head]]
        for r in range(MOBA_TOPK):
            tiles.append(jnp.where(r < i, got_ref[h, r], 0.0))
        lses = [tiles[0][:, ROW_WORDS:ROW_WORDS + 1]]
        lses += [jnp.where(r < i, tiles[r + 1][:, ROW_WORDS:ROW_WORDS + 1], NEG_BIG)
                 for r in range(MOBA_TOPK)]
        top = functools.reduce(jnp.maximum, lses)
        w = [jnp.exp2(x - top) for x in lses]
        inv = 1.0 / sum(w)
        first = jnp.zeros((BS, LANES), F32)
        second = jnp.zeros((BS, LANES), F32)
        for wk, tile in zip(w, tiles):
            words = pltpu.bitcast(tile, jnp.uint32)
            first = first + wk * pltpu.bitcast(words & jnp.uint32(0xFFFF0000), F32)
            second = second + wk * pltpu.bitcast(words << 16, F32)
        o = jnp.where(low_half, first, pltpu.roll(second, ROW_WORDS, 1)) * inv
        o_ref[:, head * A_HEAD_DIM:(head + 1) * A_HEAD_DIM] = o.astype(BF16)


def _combine(gots, own):
    H = gots[0].shape[0]
    S = own.shape[1]
    BS = MOBA_BLOCK
    return pl.pallas_call(
        _combine_kernel,
        grid=(S // BS,),
        in_specs=([pl.BlockSpec((H, MOBA_TOPK, BS, LANES), lambda i: (0, 0, i, 0))] * len(gots)
                  + [pl.BlockSpec((own.shape[0], BS, LANES), lambda i: (0, i, 0))]),
        out_specs=pl.BlockSpec((BS, A_WIDTH), lambda i: (i, 0)),
        out_shape=jax.ShapeDtypeStruct((S, A_WIDTH), BF16),
        compiler_params=_cparams(("parallel",)),
        name="moba_combine",
    )(*gots, own)


def _moba_routed(aq_hm, ak, avT, kmean):
    H, S, Dh = aq_hm.shape
    NB = S // MOBA_BLOCK
    hn = MOBA_HEADS_PER_GROUP
    q_rows = aq_hm.reshape(H * S, Dh)
    dispatched = []
    for h0 in range(0, H, hn):
        route, cnt = _route(aq_hm, kmean, h0, hn)
        cnt = cnt[:, :, 0].astype(jnp.int32)
        seg_tiles = (cnt + MOBA_BLOCK - 1) // MOBA_BLOCK
        seg_end = jnp.cumsum(seg_tiles, axis=1)
        seg_start = (seg_end - seg_tiles) * MOBA_BLOCK
        tile_ids = jnp.arange(ROUTE_TILES, dtype=jnp.int32)
        tile_blk = jnp.sum(tile_ids[None, :, None] >= seg_end[:, None, :], axis=2).astype(jnp.int32)
        tile_blk = jnp.where(tile_blk < NB, tile_blk, -1).reshape(hn * ROUTE_TILES)
        dest = _dest(seg_start, route)[:, :MOBA_TOPK, :]
        q_sorted = _sc_scatter(q_rows, h0 * S, hn * S, dest.transpose(1, 0, 2).reshape(-1),
                               hn * ROUTE_ROWS)
        dispatched.append((h0, tile_blk, dest, q_sorted))
    own = _own(aq_hm, ak, avT)
    gots = []
    for h0, tile_blk, dest, q_sorted in dispatched:
        results = _group(tile_blk, q_sorted, ak, avT, h0)
        gots.append(_sc_gather(results, dest.reshape(-1)).reshape(hn, MOBA_TOPK, S, LANES))
    return _combine(gots, own)


def _merge_kernel(hm_ref, ha_ref, gm_ref, ga_ref, x_ref, wm_ref, wa_ref, wo_ref, gain_ref,
                  out_ref):
    ym = jnp.dot(hm_ref[...], wm_ref[...], preferred_element_type=F32)
    ya = jnp.dot(ha_ref[...], wa_ref[...], preferred_element_type=F32)
    merged = _sigmoid(gm_ref[...].astype(F32)) * ym + _sigmoid(ga_ref[...].astype(F32)) * ya
    mix = jnp.dot(merged.astype(BF16), wo_ref[...], preferred_element_type=F32)
    out_ref[...] = x_ref[...] + _rms(mix, gain_ref[...])


def _merge(hm, ha, P, x, b, wm, wa, wo, gain, tm=256):
    _, S, D = x.shape
    const = pl.Buffered(1)
    return pl.pallas_call(
        _merge_kernel,
        grid=(S // tm,),
        in_specs=[
            pl.BlockSpec((tm, M_WIDTH), lambda i: (i, 0)),
            pl.BlockSpec((tm, A_WIDTH), lambda i: (i, 0)),
            pl.BlockSpec((tm, D), lambda i: (i, COL_GM // D)),
            pl.BlockSpec((tm, D), lambda i: (i, COL_GA // D)),
            pl.BlockSpec((None, tm, D), lambda i: (b, i, 0)),
            pl.BlockSpec((M_WIDTH, D), lambda i: (0, 0), pipeline_mode=const),
            pl.BlockSpec((A_WIDTH, D), lambda i: (0, 0), pipeline_mode=const),
            pl.BlockSpec((D, D), lambda i: (0, 0), pipeline_mode=const),
            pl.BlockSpec((1, D), lambda i: (0, 0)),
        ],
        out_specs=pl.BlockSpec((tm, D), lambda i: (i, 0)),
        out_shape=jax.ShapeDtypeStruct((S, D), F32),
        compiler_params=_cparams(("parallel",)),
        name="merge",
    )(hm, ha, P, P, x, wm, wa, wo, gain)


def _ffn_kernel(x_ref, gpre_ref, wu_ref, wd_ref, gpost_ref, out_ref, hn_ref, acc_ref):
    f = pl.program_id(1)

    @pl.when(f == 0)
    def _():
        hn_ref[...] = _rms(x_ref[...], gpre_ref[...]).astype(BF16)
        acc_ref[...] = jnp.zeros_like(acc_ref)

    u = jnp.dot(hn_ref[...], wu_ref[...], preferred_element_type=F32)
    u = jnp.square(jnp.maximum(u, 0.0)).astype(BF16)
    acc_ref[...] += jnp.dot(u, wd_ref[...], preferred_element_type=F32)

    @pl.when(f == pl.num_programs(1) - 1)
    def _():
        out_ref[...] = x_ref[...] + _rms(acc_ref[...], gpost_ref[...])


def _ffn(x, gpre, wu, wd, gpost, tm=512, tf=1024):
    S, D = x.shape
    Fd = wu.shape[1]
    return pl.pallas_call(
        _ffn_kernel,
        grid=(S // tm, Fd // tf),
        in_specs=[
            pl.BlockSpec((tm, D), lambda i, f: (i, 0)),
            pl.BlockSpec((1, D), lambda i, f: (0, 0)),
            pl.BlockSpec((D, tf), lambda i, f: (0, f)),
            pl.BlockSpec((tf, D), lambda i, f: (f, 0)),
            pl.BlockSpec((1, D), lambda i, f: (0, 0)),
        ],
        out_specs=pl.BlockSpec((tm, D), lambda i, f: (i, 0)),
        out_shape=jax.ShapeDtypeStruct((S, D), F32),
        scratch_shapes=[pltpu.VMEM((tm, D), BF16), pltpu.VMEM((tm, D), F32)],
        compiler_params=_cparams(("parallel", "arbitrary")),
        name="ffn",
    )(x, gpre, wu, wd, gpost)


def _layer(x, b, pos, norm_mix_pre, w_in, conv_w, conv_b, i_bias, f_bias, mlstm_norm,
           w_branch_m, w_branch_a, w_out, norm_mix_post, norm_ffn_pre, w_up, w_down,
           norm_ffn_post):
    _, S, D = x.shape
    o = 0
    pieces = {}
    w_in = w_in.astype(BF16)
    for name, width in (("mq", M_WIDTH), ("mk", M_WIDTH), ("mv", M_WIDTH), ("mo", M_WIDTH),
                        ("mi", M_HEADS), ("mf", M_HEADS), ("aq", A_WIDTH), ("ak", A_WIDTH),
                        ("av", A_WIDTH), ("gm", D), ("ga", D)):
        pieces[name] = w_in[:, o:o + width]
        o += width
    w_all = jnp.concatenate([pieces[n] for n in ("gm", "ga", "mq", "mk", "aq", "ak", "mv", "mo", "av")],
                            axis=1)
    w_gate = jnp.concatenate(
        [pieces["mi"], pieces["mf"], jnp.zeros((D, LANES - 2 * M_HEADS), BF16)], axis=1)
    gbias = jnp.concatenate([i_bias, f_bias, jnp.zeros((LANES - 2 * M_HEADS,), F32)])[None, :]
    half = jnp.arange(0, A_HEAD_DIM, 2, dtype=F32) / A_HEAD_DIM
    inv_freq = 1.0 / (ROPE_THETA ** half)
    invf = jnp.concatenate([inv_freq, inv_freq])[None, :]

    P, gate = _proj(x, b, norm_mix_pre[None, :], w_all, w_gate)
    mq, mkT, aq, ak, avT, kmean, gc, gt = _prep(
        P, gate, pos.reshape(S, 1), invf, conv_w, conv_b[None, :], gbias)
    hm = _mlstm(mq, mkT, P, gc, gt, mlstm_norm[None, :])
    ha = _moba_routed(aq, ak, avT, kmean.reshape(S // MOBA_BLOCK, A_WIDTH))
    x1 = _merge(hm, ha, P, x, b, w_branch_m.astype(BF16), w_branch_a.astype(BF16),
                w_out.astype(BF16), norm_mix_post[None, :])
    return _ffn(x1, norm_ffn_pre[None, :], w_up.astype(BF16), w_down.astype(BF16),
                norm_ffn_post[None, :])


def kernel(x, positions, norm_mix_pre, w_in, conv_w, conv_b, i_bias, f_bias, mlstm_norm,
           w_branch_m, w_branch_a, w_out, norm_mix_post, norm_ffn_pre, w_up, w_down,
           norm_ffn_post):
    B = x.shape[0]
    depth = w_in.shape[0]
    outs = []
    def take(a, i):
        return a.reshape(a.shape[1:]) if a.shape[0] == 1 else a[i]

    for b in range(B):
        xin, bi = x, b
        for l in range(depth):
            xb = _layer(xin, bi, take(positions, b), take(norm_mix_pre, l), take(w_in, l),
                        take(conv_w, l), take(conv_b, l), take(i_bias, l), take(f_bias, l),
                        take(mlstm_norm, l), take(w_branch_m, l), take(w_branch_a, l),
                        take(w_out, l), take(norm_mix_post, l), take(norm_ffn_pre, l),
                        take(w_up, l), take(w_down, l), take(norm_ffn_post, l))
            xin, bi = xb[None], 0
        outs.append(xb)
    return outs[0][None] if B == 1 else jnp.stack(outs, axis=0)
```

```python
import functools

import jax
import jax.numpy as jnp
from jax import lax
from jax.experimental import pallas as pl
from jax.experimental.pallas import tpu as pltpu
from jax.experimental.pallas import tpu_sc as plsc

F32 = jnp.float32
BF16 = jnp.bfloat16

M_HEADS = 4
M_HEAD_DIM = 256
M_WIDTH = M_HEADS * M_HEAD_DIM
M_CHUNK = 128
CONV_WIDTH = 4
A_HEADS = 8
A_HEAD_DIM = 128
A_WIDTH = A_HEADS * A_HEAD_DIM
MOBA_BLOCK = 256
MOBA_TOPK = 3
ROPE_THETA = 10000.0
NORM_EPS = 1e-6

LANES = 128
SUBLANES = 8
VMEM_LIMIT = 56 * 1024 * 1024
NEG_BIG = -1e30
LOG2_E = 1.4426950408889634

COL_GM = 0
COL_GA = 2048
COL_MQK = 4096
COL_AQK = 6144
COL_MV = 8192
COL_MO = 9216
COL_AV = 10240


def _cparams(sem):
    return pltpu.CompilerParams(dimension_semantics=sem, vmem_limit_bytes=VMEM_LIMIT)


def _rms(x, gain):
    ms = jnp.mean(x * x, axis=-1, keepdims=True)
    return x * lax.rsqrt(ms + NORM_EPS) * gain


def _sigmoid(x):
    return 1.0 / (1.0 + jnp.exp(-x))


def _split3(x):
    hi = x.astype(BF16)
    r1 = x - hi.astype(F32)
    mid = r1.astype(BF16)
    lo = (r1 - mid.astype(F32)).astype(BF16)
    return hi, mid, lo


def _proj_kernel(x_ref, g_ref, w_ref, wg_ref, p_ref, gate_ref, xn_ref):
    @pl.when(pl.program_id(1) == 0)
    def _():
        xn = _rms(x_ref[...], g_ref[...]).astype(BF16)
        xn_ref[...] = xn
        gate_ref[...] = jnp.dot(xn, wg_ref[...], preferred_element_type=F32)

    p_ref[...] = jnp.dot(xn_ref[...], w_ref[...], preferred_element_type=F32).astype(BF16)


def _proj(x, b, gain, w_all, w_gate, tm=1024, tn=1024):
    _, S, D = x.shape
    N = w_all.shape[1]
    return pl.pallas_call(
        _proj_kernel,
        grid=(S // tm, N // tn),
        in_specs=[
            pl.BlockSpec((None, tm, D), lambda i, j: (b, i, 0)),
            pl.BlockSpec((1, D), lambda i, j: (0, 0)),
            pl.BlockSpec((D, tn), lambda i, j: (0, j)),
            pl.BlockSpec((D, LANES), lambda i, j: (0, 0)),
        ],
        out_specs=[
            pl.BlockSpec((tm, tn), lambda i, j: (i, j)),
            pl.BlockSpec((tm, LANES), lambda i, j: (i, 0)),
        ],
        out_shape=[
            jax.ShapeDtypeStruct((S, N), BF16),
            jax.ShapeDtypeStruct((S, LANES), F32),
        ],
        scratch_shapes=[pltpu.VMEM((tm, D), BF16)],
        compiler_params=_cparams(("parallel", "arbitrary")),
        name="proj",
    )(x, gain, w_all, w_gate)


PREP_ROWS = MOBA_BLOCK
PREP_COLS = 512
HALO_ROWS = 2 * SUBLANES


def _prep_kernel(pmk_ref, halo_ref, paqk_ref, pav_ref, gate_ref, pos_ref, invf_ref,
                 cw_ref, cb_ref, gb_ref,
                 mq_ref, mkT_ref, aq_ref, ak_ref, avT_ref, kmean_ref, gc_ref, gt_ref):
    i = pl.program_id(0)
    R = PREP_ROWS

    k_scale = M_HEAD_DIM ** -0.5
    for c0 in range(0, 2 * M_WIDTH, PREP_COLS):
        cs = slice(c0, c0 + PREP_COLS)
        prev = halo_ref[:, cs].astype(F32)[HALO_ROWS - SUBLANES:, :]
        prev = jnp.where(i == 0, jnp.zeros_like(prev), prev)
        ext = jnp.concatenate([pmk_ref[:, cs].astype(F32), prev], axis=0)
        acc = cw_ref[0:1, cs] * ext
        for j in range(1, CONV_WIDTH):
            acc = pltpu.roll(acc, 1, 0) + cw_ref[j:j + 1, cs] * ext
        acc = acc[0:R, :] + cb_ref[:, cs]
        y = acc * _sigmoid(acc)
        if c0 < M_WIDTH:
            mq_ref[:, cs] = y.astype(BF16)
        else:
            ks = slice(c0 - M_WIDTH, c0 - M_WIDTH + PREP_COLS)
            mkT_ref[ks, :] = (y * k_scale).T.astype(BF16)

    ang = pos_ref[...].astype(F32) * invf_ref[...]
    cos = jnp.cos(ang)
    lane = lax.broadcasted_iota(jnp.int32, (R, A_HEAD_DIM), 1)
    sin_signed = jnp.where(lane < A_HEAD_DIM // 2, -1.0, 1.0) * jnp.sin(ang)
    for h in range(2 * A_HEADS):
        hs = slice(h * A_HEAD_DIM, (h + 1) * A_HEAD_DIM)
        xh = paqk_ref[:, hs].astype(F32)
        yh = xh * cos + pltpu.roll(xh, A_HEAD_DIM // 2, 1) * sin_signed
        if h < A_HEADS:
            aq_ref[h] = yh
        else:
            ko = slice((h - A_HEADS) * A_HEAD_DIM, (h - A_HEADS + 1) * A_HEAD_DIM)
            ak_ref[:, ko] = yh.astype(BF16)
            kmean_ref[0, :, ko] = jnp.mean(yh, axis=0, keepdims=True)
    for c0 in range(0, A_WIDTH, PREP_COLS):
        avT_ref[c0:c0 + PREP_COLS, :] = pav_ref[:, c0:c0 + PREP_COLS].astype(F32).T.astype(BF16)

    g = gate_ref[...] + gb_ref[...]
    log_f = jnp.minimum(g, 0.0) - jnp.log1p(jnp.exp(-jnp.abs(g)))
    r_i = lax.broadcasted_iota(jnp.int32, (R, R), 0)
    c_i = lax.broadcasted_iota(jnp.int32, (R, R), 1)
    tri = ((r_i >= c_i) & ((r_i // M_CHUNK) == (c_i // M_CHUNK))).astype(BF16)
    hi, mid, lo = _split3(log_f)
    csum = (jnp.dot(tri, hi, preferred_element_type=F32)
            + jnp.dot(tri, mid, preferred_element_type=F32)
            + jnp.dot(tri, lo, preferred_element_type=F32))
    glane = lax.broadcasted_iota(jnp.int32, (R, LANES), 1)
    gc = jnp.where(glane < M_HEADS, g, csum)
    gc_ref[...] = gc
    gt_ref[...] = gc.T[0:SUBLANES, :]


def _prep(P, gate, pos, invf, conv_w, conv_b, gbias):
    S = P.shape[0]
    R = PREP_ROWS
    nb = S // R
    halo_blocks = R // HALO_ROWS
    return pl.pallas_call(
        _prep_kernel,
        grid=(nb,),
        in_specs=[
            pl.BlockSpec((R, 2 * M_WIDTH), lambda i: (i, COL_MQK // (2 * M_WIDTH))),
            pl.BlockSpec((HALO_ROWS, 2 * M_WIDTH),
                         lambda i: (jnp.maximum(i * halo_blocks - 1, 0), COL_MQK // (2 * M_WIDTH))),
            pl.BlockSpec((R, 2 * A_WIDTH), lambda i: (i, COL_AQK // (2 * A_WIDTH))),
            pl.BlockSpec((R, A_WIDTH), lambda i: (i, COL_AV // A_WIDTH)),
            pl.BlockSpec((R, LANES), lambda i: (i, 0)),
            pl.BlockSpec((R, 1), lambda i: (i, 0)),
            pl.BlockSpec((1, A_HEAD_DIM), lambda i: (0, 0)),
            pl.BlockSpec((CONV_WIDTH, 2 * M_WIDTH), lambda i: (0, 0)),
            pl.BlockSpec((1, 2 * M_WIDTH), lambda i: (0, 0)),
            pl.BlockSpec((1, LANES), lambda i: (0, 0)),
        ],
        out_specs=[
            pl.BlockSpec((R, M_WIDTH), lambda i: (i, 0)),
            pl.BlockSpec((M_WIDTH, R), lambda i: (0, i)),
            pl.BlockSpec((A_HEADS, R, A_HEAD_DIM), lambda i: (0, i, 0)),
            pl.BlockSpec((R, A_WIDTH), lambda i: (i, 0)),
            pl.BlockSpec((A_WIDTH, R), lambda i: (0, i)),
            pl.BlockSpec((1, 1, A_WIDTH), lambda i: (i, 0, 0)),
            pl.BlockSpec((R, LANES), lambda i: (i, 0)),
            pl.BlockSpec((SUBLANES, R), lambda i: (0, i)),
        ],
        out_shape=[
            jax.ShapeDtypeStruct((S, M_WIDTH), BF16),
            jax.ShapeDtypeStruct((M_WIDTH, S), BF16),
            jax.ShapeDtypeStruct((A_HEADS, S, A_HEAD_DIM), F32),
            jax.ShapeDtypeStruct((S, A_WIDTH), BF16),
            jax.ShapeDtypeStruct((A_WIDTH, S), BF16),
            jax.ShapeDtypeStruct((nb, 1, A_WIDTH), F32),
            jax.ShapeDtypeStruct((S, LANES), F32),
            jax.ShapeDtypeStruct((SUBLANES, S), F32),
        ],
        compiler_params=_cparams(("parallel",)),
        name="prep",
    )(P, P, P, P, gate, pos, invf, conv_w, conv_b, gbias)


M_AUG = M_HEAD_DIM + LANES
MLSTM_CHUNKS_PER_STEP = 1


def _mlstm_kernel(q_ref, kT_ref, v_ref, mo_ref, gc_ref, gt_ref, gain_ref, out_ref,
                  c_ref, m_ref):
    @pl.when(pl.program_id(0) == 0)
    def _():
        c_ref[...] = jnp.zeros_like(c_ref)
        m_ref[...] = jnp.zeros_like(m_ref)

    L = M_CHUNK
    D = M_HEAD_DIM
    row = lax.broadcasted_iota(jnp.int32, (L, L), 0)
    col = lax.broadcasted_iota(jnp.int32, (L, L), 1)
    causal = row >= col
    ones_col = (lax.broadcasted_iota(jnp.int32, (L, LANES), 1) == 0).astype(BF16)

    heads = range(M_HEADS)
    hsl = [slice(h * D, (h + 1) * D) for h in heads]

    def state_stages(rs):
        q = [q_ref[rs, hsl[h]] for h in heads]
        kT = [kT_ref[hsl[h], rs] for h in heads]
        v_aug = [jnp.concatenate([v_ref[rs, hsl[h]], ones_col], axis=1) for h in heads]
        b_c = [gc_ref[rs, M_HEADS + h:M_HEADS + h + 1] for h in heads]
        b_r = [gt_ref[M_HEADS + h:M_HEADS + h + 1, rs] for h in heads]
        u_r = [gt_ref[h:h + 1, rs] - b_r[h] for h in heads]
        f_tot = [b_r[h][:, L - 1:L] for h in heads]
        m_prev = [m_ref[h:h + 1, 0:1] for h in heads]

        s_qk = [jnp.dot(q[h], kT[h], preferred_element_type=F32) for h in heads]
        q_c = [jnp.dot(q[h], c_ref[h].astype(BF16), preferred_element_type=F32) for h in heads]

        for h in heads:
            w_r = f_tot[h] + u_r[h]
            m_loc = jnp.max(w_r, axis=1, keepdims=True)
            m_new = jnp.maximum(f_tot[h] + m_prev[h], m_loc)
            a = jnp.exp(f_tot[h] + m_prev[h] - m_new)
            e_r = jnp.exp(w_r - m_new)
            keT = (kT[h].astype(F32) * e_r).astype(BF16)
            c_ref[h] = a * c_ref[h] + jnp.dot(keT, v_aug[h], preferred_element_type=F32)
            m_ref[h:h + 1, :] = jnp.broadcast_to(m_new, (1, LANES))
        return rs, v_aug, b_c, u_r, m_prev, s_qk, q_c

    def output_stage(rs, v_aug, b_c, u_r, m_prev, s_qk, q_c):
        for h in heads:
            d_log = jnp.where(causal, b_c[h] + u_r[h], -jnp.inf)
            a_log = b_c[h] + m_prev[h]
            m_t = jnp.maximum(a_log, jnp.max(d_log, axis=1, keepdims=True))
            s_ts = s_qk[h] * jnp.exp(d_log - m_t)
            inter = jnp.exp(a_log - m_t)
            r = inter * q_c[h] + jnp.dot(s_ts.astype(BF16), v_aug[h], preferred_element_type=F32)
            num = r[:, :D]
            den = r[:, D:D + 1]
            hh = num / jnp.maximum(jnp.abs(den), jnp.exp(-m_t))
            hn = hh * lax.rsqrt(jnp.mean(hh * hh, axis=-1, keepdims=True) + NORM_EPS)
            out_ref[rs, hsl[h]] = (hn * gain_ref[:, hsl[h]]
                                   * _sigmoid(mo_ref[rs, hsl[h]].astype(F32))).astype(BF16)

    pending = [state_stages(slice(c * L, (c + 1) * L)) for c in range(MLSTM_CHUNKS_PER_STEP)]
    for vals in pending:
        output_stage(*vals)


def _mlstm(mq, mkT, P, gc, gt, gain):
    S = mq.shape[0]
    L = M_CHUNK * MLSTM_CHUNKS_PER_STEP
    return pl.pallas_call(
        _mlstm_kernel,
        grid=(S // L,),
        in_specs=[
            pl.BlockSpec((L, M_WIDTH), lambda c: (c, 0)),
            pl.BlockSpec((M_WIDTH, L), lambda c: (0, c)),
            pl.BlockSpec((L, M_WIDTH), lambda c: (c, COL_MV // M_WIDTH)),
            pl.BlockSpec((L, M_WIDTH), lambda c: (c, COL_MO // M_WIDTH)),
            pl.BlockSpec((L, LANES), lambda c: (c, 0)),
            pl.BlockSpec((SUBLANES, L), lambda c: (0, c)),
            pl.BlockSpec((1, M_WIDTH), lambda c: (0, 0)),
        ],
        out_specs=pl.BlockSpec((L, M_WIDTH), lambda c: (c, 0)),
        out_shape=jax.ShapeDtypeStruct((S, M_WIDTH), BF16),
        scratch_shapes=[
            pltpu.VMEM((M_HEADS, M_HEAD_DIM, M_AUG), F32),
            pltpu.VMEM((SUBLANES, LANES), F32),
        ],
        compiler_params=_cparams(("arbitrary",)),
        name="mlstm",
    )(mq, mkT, P, P, gc, gt, gain)


MOBA_HEADS_PER_GROUP = 4
ROUTE_TILES = 264
ROUTE_ROWS = ROUTE_TILES * MOBA_BLOCK
GROUP_TILES_PER_STEP = 24
ROW_WORDS = A_HEAD_DIM // 2
SC_WINDOW = 256


def _pack_rows(o_t, lse):
    q = o_t.shape[1]
    hi = pltpu.bitcast(o_t[:ROW_WORDS].astype(BF16).astype(F32), jnp.uint32)
    lo = pltpu.bitcast(o_t[ROW_WORDS:].astype(BF16).astype(F32), jnp.uint32)
    words = pltpu.bitcast(hi | (lo >> 16), F32)
    tail = jnp.concatenate([jnp.broadcast_to(lse, (SUBLANES, q)),
                            jnp.zeros((LANES - ROW_WORDS - SUBLANES, q), F32)], axis=0)
    return jnp.concatenate([words, tail], axis=0).T


def _dot_nt(a, b):
    return lax.dot_general(a, b, (((1,), (1,)), ((), ())), preferred_element_type=F32)


def _block_partial(k_blk, v_t_blk, qs, ones_rows, causal_mask=None):
    s = _dot_nt(k_blk, qs)
    if causal_mask is not None:
        s = jnp.where(causal_mask, s, NEG_BIG)
    sb = s.astype(BF16)
    m = jnp.max(sb, axis=0, keepdims=True)
    p = jnp.exp2(sb - m)
    r = jnp.dot(jnp.concatenate([v_t_blk, ones_rows], axis=0), p, preferred_element_type=F32)
    dh = v_t_blk.shape[0]
    l = r[dh:dh + 1, :]
    return r[:dh, :] / l, m.astype(F32) + jnp.log2(l)


def _route_kernel(q_ref, k_ref, vT_ref, km_ref, route_ref, cnt_ref, own_ref, cnt_acc):
    i = pl.program_id(1)
    BS = MOBA_BLOCK
    Dh = A_HEAD_DIM
    NB = km_ref.shape[0]
    G = q_ref.shape[0]
    blk = lax.broadcasted_iota(jnp.int32, (NB, BS), 0)
    kpos = lax.broadcasted_iota(jnp.int32, (BS, BS), 0)
    qpos = lax.broadcasted_iota(jnp.int32, (BS, BS), 1)
    earlier = (kpos < qpos).astype(BF16)
    ones_rows = jnp.ones((2 * SUBLANES, BS), BF16)

    @pl.when(i == 0)
    def _():
        cnt_acc[...] = jnp.zeros_like(cnt_acc)

    q = [q_ref[g] for g in range(G)]
    qs = [(q[g] * (Dh ** -0.5 * LOG2_E)).astype(BF16) for g in range(G)]
    parts = [_block_partial(k_ref[:, g * Dh:(g + 1) * Dh], vT_ref[g * Dh:(g + 1) * Dh, :], qs[g],
                            ones_rows, kpos <= qpos) for g in range(G)]

    gates = []
    for g in range(G):
        hs = slice(g * Dh, (g + 1) * Dh)
        km = km_ref[:, hs]
        kh = km.astype(BF16)
        kl = (km - kh.astype(F32)).astype(BF16)
        qh = q[g].astype(BF16)
        ql = (q[g] - qh.astype(F32)).astype(BF16)
        gate = _dot_nt(kh, qh) + _dot_nt(kh, ql) + _dot_nt(kl, qh)
        gates.append(jnp.where(blk < i, gate, -jnp.inf))
    picks = [[] for _ in range(G)]
    rows = [[] for _ in range(G)]
    for r in range(MOBA_TOPK):
        for g in range(G):
            mx = jnp.max(gates[g], axis=0, keepdims=True)
            idx = jnp.min(jnp.where(gates[g] == mx, blk, NB), axis=0, keepdims=True)
            idx = jnp.where(r < i, idx, -1)
            pick = blk == idx
            gates[g] = jnp.where(pick, -jnp.inf, gates[g])
            picks[g].append(pick)
            rows[g].append(idx)
    for g in range(G):
        onehot = sum(p.astype(F32) for p in picks[g])
        before = cnt_acc[g][:, 0:1] + jnp.dot(onehot.astype(BF16), earlier, preferred_element_type=F32)
        for r in range(MOBA_TOPK):
            rank = jnp.sum(jnp.where(picks[g][r], before, 0.0), axis=0, keepdims=True)
            rows[g].append(rank.astype(jnp.int32))
        rows[g].append(jnp.zeros((SUBLANES - 2 * MOBA_TOPK, BS), jnp.int32))
        route_ref[g] = jnp.concatenate(rows[g], axis=0)
        cnt_new = cnt_acc[g] + jnp.sum(onehot, axis=1, keepdims=True)
        cnt_acc[g] = cnt_new
        cnt_ref[g] = cnt_new
        own_ref[g] = _pack_rows(*parts[g])


def _route(aq_hm, ak, avT, kmean, h0, H):
    _, S, Dh = aq_hm.shape
    BS = MOBA_BLOCK
    NB = S // BS
    G = MOBA_HEADS_PER_GROUP
    W = G * Dh
    hb = h0 // G
    return pl.pallas_call(
        _route_kernel,
        grid=(H // G, NB),
        in_specs=[
            pl.BlockSpec((G, BS, Dh), lambda h, i: (hb + h, i, 0)),
            pl.BlockSpec((BS, W), lambda h, i: (i, hb + h)),
            pl.BlockSpec((W, BS), lambda h, i: (hb + h, i)),
            pl.BlockSpec((NB, W), lambda h, i: (0, hb + h)),
        ],
        out_specs=[
            pl.BlockSpec((G, SUBLANES, BS), lambda h, i: (h, 0, i)),
            pl.BlockSpec((G, NB, LANES), lambda h, i: (h, 0, 0)),
            pl.BlockSpec((G, BS, LANES), lambda h, i: (h, i, 0)),
        ],
        out_shape=[
            jax.ShapeDtypeStruct((H, SUBLANES, S), jnp.int32),
            jax.ShapeDtypeStruct((H, NB, LANES), F32),
            jax.ShapeDtypeStruct((H, S, LANES), F32),
        ],
        scratch_shapes=[pltpu.VMEM((G, NB, LANES), F32)],
        compiler_params=_cparams(("parallel", "arbitrary")),
        name="moba_route",
    )(aq_hm, ak, avT, kmean)


def _dest_kernel(seg_ref, route_ref, dest_ref):
    h = pl.program_id(0)
    NB = seg_ref.shape[1]
    r = route_ref[0]
    blk = r[0:MOBA_TOPK, :]
    rank = r[MOBA_TOPK:2 * MOBA_TOPK, :]
    base = h * ROUTE_ROWS
    dest = jnp.full(blk.shape, base + ROUTE_ROWS - 1, jnp.int32)
    for j in range(NB):
        dest = jnp.where(blk == j, base + seg_ref[h, j] + rank, dest)
    dest_ref[0] = jnp.concatenate(
        [dest, jnp.zeros((SUBLANES - MOBA_TOPK, dest.shape[1]), jnp.int32)], axis=0)


def _dest(seg_start, route):
    H, _, S = route.shape
    return pl.pallas_call(
        _dest_kernel,
        grid_spec=pltpu.PrefetchScalarGridSpec(
            num_scalar_prefetch=1,
            grid=(H,),
            in_specs=[pl.BlockSpec((1, SUBLANES, S), lambda h, seg: (h, 0, 0))],
            out_specs=pl.BlockSpec((1, SUBLANES, S), lambda h, seg: (h, 0, 0)),
        ),
        out_shape=jax.ShapeDtypeStruct((H, SUBLANES, S), jnp.int32),
        compiler_params=_cparams(("parallel",)),
        name="moba_dest",
    )(seg_start, route)


def _sc_mesh():
    return plsc.VectorSubcoreMesh(core_axis_name="c", subcore_axis_name="s")


def _sc_scatter(rows, row0, n_rows, idx, n_out):
    W = rows.shape[1]
    M = idx.shape[0]
    n_win = n_rows // SC_WINDOW
    win0 = row0 // SC_WINDOW

    reps = M // n_rows

    @pl.kernel(out_type=jax.ShapeDtypeStruct((n_out, W), rows.dtype), mesh=_sc_mesh())
    def k(x_hbm, i_hbm, o_hbm):
        def body(x_vmem, *i_vmems):
            for i_vmem in i_vmems:
                pltpu.sync_copy(x_vmem, o_hbm.at[i_vmem.at[0]])

        pltpu.emit_pipeline(
            body,
            grid=(n_win,),
            in_specs=([pl.BlockSpec((SC_WINDOW, W), lambda w: (win0 + w, 0))]
                      + [pl.BlockSpec((1, SC_WINDOW), lambda w, r=r: (0, r * n_win + w)) for r in range(reps)]),
            out_specs=[],
            core_axis_name=("c", "s"),
            dimension_semantics=(pltpu.PARALLEL,),
        )(x_hbm, *([i_hbm] * reps))

    return k(rows, idx.reshape(1, M))


def _sc_gather(table, idx):
    M = idx.shape[0]
    W = table.shape[1]

    @pl.kernel(out_type=jax.ShapeDtypeStruct((M, W), table.dtype), mesh=_sc_mesh())
    def k(x_hbm, i_hbm, o_hbm):
        def body(i_vmem, o_vmem):
            pltpu.sync_copy(x_hbm.at[i_vmem.at[0]], o_vmem)

        pltpu.emit_pipeline(
            body,
            grid=(M // SC_WINDOW,),
            in_specs=[pl.BlockSpec((1, SC_WINDOW), lambda w: (0, w))],
            out_specs=[pl.BlockSpec((SC_WINDOW, W), lambda w: (w, 0))],
            core_axis_name=("c", "s"),
            dimension_semantics=(pltpu.PARALLEL,),
        )(i_hbm, o_hbm)

    return k(table, idx.reshape(1, M))


def _group_kernel(tile_blk_ref, q_ref, k_ref, vT_ref, out_ref, s_ref, m_ref):
    h = pl.program_id(0)
    u = pl.program_id(1)
    last = pl.num_programs(1) - 2
    BS = MOBA_BLOCK
    Dh = A_HEAD_DIM
    T = GROUP_TILES_PER_STEP
    ones_rows = jnp.ones((2 * SUBLANES, BS), BF16)
    new = lax.rem(u, 2)
    old = 1 - new

    @pl.when((h == 0) & (u == 0))
    def _():
        s_ref[...] = jnp.zeros_like(s_ref)
        m_ref[...] = jnp.zeros_like(m_ref)

    def block_start(group, c):
        j = jnp.maximum(tile_blk_ref[h * ROUTE_TILES + group * T + c], 0)
        return pl.multiple_of(j * BS, BS)

    g_new = jnp.minimum(u, last)
    g_old = jnp.maximum(u - 1, 0)
    @pl.when(tile_blk_ref[h * ROUTE_TILES + g_old * T] >= 0)
    def _():
        ms, rs = [], []
        for c in range(T):
            m = m_ref[old, c]
            p = jnp.exp2(s_ref[old, c] - m.astype(BF16))
            v_aug = jnp.concatenate([vT_ref[:, pl.ds(block_start(g_old, c), BS)], ones_rows], axis=0)
            ms.append(m)
            rs.append(jnp.dot(v_aug, p, preferred_element_type=F32))
        s_new = []
        for c in range(T):
            qs = (q_ref[c * BS:(c + 1) * BS, :] * (Dh ** -0.5 * LOG2_E)).astype(BF16)
            s_new.append(_dot_nt(k_ref[pl.ds(block_start(g_new, c), BS), :], qs))
        for c in range(T):
            l = rs[c][Dh:Dh + 1, :]
            out_ref[c * BS:(c + 1) * BS, :] = _pack_rows(rs[c][:Dh, :] / l, ms[c] + jnp.log2(l))
            sb = s_new[c].astype(BF16)
            s_ref[new, c] = sb
            m_ref[new, c] = jnp.max(sb, axis=0, keepdims=True).astype(F32)


def _group(tile_blk, q_sorted, ak, avT, h0):
    S = ak.shape[0]
    H = q_sorted.shape[0] // ROUTE_ROWS
    Dh = A_HEAD_DIM
    BS = MOBA_BLOCK
    T = GROUP_TILES_PER_STEP
    steps = ROUTE_TILES // T
    rows = T * BS
    return pl.pallas_call(
        _group_kernel,
        grid_spec=pltpu.PrefetchScalarGridSpec(
            num_scalar_prefetch=1,
            grid=(H, steps + 1),
            in_specs=[
                pl.BlockSpec((rows, LANES), lambda h, u, tb: (h * steps + jnp.minimum(u, steps - 1), 0)),
                pl.BlockSpec((S, Dh), lambda h, u, tb: (0, h0 + h)),
                pl.BlockSpec((Dh, S), lambda h, u, tb: (h0 + h, 0)),
            ],
            out_specs=pl.BlockSpec((rows, LANES), lambda h, u, tb: (h * steps + jnp.maximum(u - 1, 0), 0)),
            scratch_shapes=[pltpu.VMEM((2, T, BS, BS), BF16), pltpu.VMEM((2, T, 1, BS), F32)],
        ),
        out_shape=jax.ShapeDtypeStruct((H * ROUTE_ROWS, LANES), F32),
        compiler_params=_cparams(("arbitrary", "arbitrary")),
        name="moba_group",
    )(tile_blk, q_sorted, ak, avT)


def _combine_kernel(*refs):
    i = pl.program_id(0)
    o_ref = refs[-1]
    n_groups = (len(refs) - 1) // 2
    H = refs[n_groups].shape[0]
    BS = refs[n_groups].shape[1]
    low_half = lax.broadcasted_iota(jnp.int32, (BS, LANES), 1) < ROW_WORDS
    for head in range(n_groups * H):
        got_ref, own_ref, h = refs[head // H], refs[n_groups + head // H], head % H
        tiles = [own_ref[h]]
        for r in range(MOBA_TOPK):
            tiles.append(jnp.where(r < i, got_ref[h, r], 0.0))
        lses = [tiles[0][:, ROW_WORDS:ROW_WORDS + 1]]
        lses += [jnp.where(r < i, tiles[r + 1][:, ROW_WORDS:ROW_WORDS + 1], NEG_BIG)
                 for r in range(MOBA_TOPK)]
        top = functools.reduce(jnp.maximum, lses)
        w = [jnp.exp2(x - top) for x in lses]
        inv = 1.0 / sum(w)
        first = jnp.zeros((BS, LANES), F32)
        second = jnp.zeros((BS, LANES), F32)
        for wk, tile in zip(w, tiles):
            words = pltpu.bitcast(tile, jnp.uint32)
            first = first + wk * pltpu.bitcast(words & jnp.uint32(0xFFFF0000), F32)
            second = second + wk * pltpu.bitcast(words << 16, F32)
        o = jnp.where(low_half, first, pltpu.roll(second, ROW_WORDS, 1)) * inv
        o_ref[:, head * A_HEAD_DIM:(head + 1) * A_HEAD_DIM] = o.astype(BF16)


def _combine(gots, owns):
    H, S, _ = owns[0].shape
    BS = MOBA_BLOCK
    return pl.pallas_call(
        _combine_kernel,
        grid=(S // BS,),
        in_specs=([pl.BlockSpec((H, MOBA_TOPK, BS, LANES), lambda i: (0, 0, i, 0))] * len(gots)
                  + [pl.BlockSpec((H, BS, LANES), lambda i: (0, i, 0))] * len(owns)),
        out_specs=pl.BlockSpec((BS, A_WIDTH), lambda i: (i, 0)),
        out_shape=jax.ShapeDtypeStruct((S, A_WIDTH), BF16),
        compiler_params=_cparams(("parallel",)),
        name="moba_combine",
    )(*gots, *owns)


def _moba_routed(aq_hm, ak, avT, kmean):
    H, S, Dh = aq_hm.shape
    NB = S // MOBA_BLOCK
    hn = MOBA_HEADS_PER_GROUP
    q_rows = aq_hm.reshape(H * S, Dh)
    gots, owns = [], []
    for h0 in range(0, H, hn):
        route, cnt, own = _route(aq_hm, ak, avT, kmean, h0, hn)
        cnt = cnt[:, :, 0].astype(jnp.int32)
        seg_tiles = (cnt + MOBA_BLOCK - 1) // MOBA_BLOCK
        seg_end = jnp.cumsum(seg_tiles, axis=1)
        seg_start = (seg_end - seg_tiles) * MOBA_BLOCK
        tile_ids = jnp.arange(ROUTE_TILES, dtype=jnp.int32)
        tile_blk = jnp.sum(tile_ids[None, :, None] >= seg_end[:, None, :], axis=2).astype(jnp.int32)
        tile_blk = jnp.where(tile_blk < NB, tile_blk, -1).reshape(hn * ROUTE_TILES)
        dest = _dest(seg_start, route)[:, :MOBA_TOPK, :]
        q_sorted = _sc_scatter(q_rows, h0 * S, hn * S, dest.transpose(1, 0, 2).reshape(-1),
                               hn * ROUTE_ROWS)
        results = _group(tile_blk, q_sorted, ak, avT, h0)
        gots.append(_sc_gather(results, dest.reshape(-1)).reshape(hn, MOBA_TOPK, S, LANES))
        owns.append(own)
    return _combine(gots, owns)


def _merge_kernel(hm_ref, ha_ref, gm_ref, ga_ref, x_ref, wm_ref, wa_ref, wo_ref, gain_ref,
                  out_ref):
    ym = jnp.dot(hm_ref[...], wm_ref[...], preferred_element_type=F32)
    ya = jnp.dot(ha_ref[...], wa_ref[...], preferred_element_type=F32)
    merged = _sigmoid(gm_ref[...].astype(F32)) * ym + _sigmoid(ga_ref[...].astype(F32)) * ya
    mix = jnp.dot(merged.astype(BF16), wo_ref[...], preferred_element_type=F32)
    out_ref[...] = x_ref[...] + _rms(mix, gain_ref[...])


def _merge(hm, ha, P, x, b, wm, wa, wo, gain, tm=256):
    _, S, D = x.shape
    const = pl.Buffered(1)
    return pl.pallas_call(
        _merge_kernel,
        grid=(S // tm,),
        in_specs=[
            pl.BlockSpec((tm, M_WIDTH), lambda i: (i, 0)),
            pl.BlockSpec((tm, A_WIDTH), lambda i: (i, 0)),
            pl.BlockSpec((tm, D), lambda i: (i, COL_GM // D)),
            pl.BlockSpec((tm, D), lambda i: (i, COL_GA // D)),
            pl.BlockSpec((None, tm, D), lambda i: (b, i, 0)),
            pl.BlockSpec((M_WIDTH, D), lambda i: (0, 0), pipeline_mode=const),
            pl.BlockSpec((A_WIDTH, D), lambda i: (0, 0), pipeline_mode=const),
            pl.BlockSpec((D, D), lambda i: (0, 0), pipeline_mode=const),
            pl.BlockSpec((1, D), lambda i: (0, 0)),
        ],
        out_specs=pl.BlockSpec((tm, D), lambda i: (i, 0)),
        out_shape=jax.ShapeDtypeStruct((S, D), F32),
        compiler_params=_cparams(("parallel",)),
        name="merge",
    )(hm, ha, P, P, x, wm, wa, wo, gain)


def _ffn_kernel(x_ref, gpre_ref, wu_ref, wd_ref, gpost_ref, out_ref, hn_ref, acc_ref):
    f = pl.program_id(1)

    @pl.when(f == 0)
    def _():
        hn_ref[...] = _rms(x_ref[...], gpre_ref[...]).astype(BF16)
        acc_ref[...] = jnp.zeros_like(acc_ref)

    u = jnp.dot(hn_ref[...], wu_ref[...], preferred_element_type=F32)
    u = jnp.square(jnp.maximum(u, 0.0)).astype(BF16)
    acc_ref[...] += jnp.dot(u, wd_ref[...], preferred_element_type=F32)

    @pl.when(f == pl.num_programs(1) - 1)
    def _():
        out_ref[...] = x_ref[...] + _rms(acc_ref[...], gpost_ref[...])


def _ffn(x, gpre, wu, wd, gpost, tm=512, tf=1024):
    S, D = x.shape
    Fd = wu.shape[1]
    return pl.pallas_call(
        _ffn_kernel,
        grid=(S // tm, Fd // tf),
        in_specs=[
            pl.BlockSpec((tm, D), lambda i, f: (i, 0)),
            pl.BlockSpec((1, D), lambda i, f: (0, 0)),
            pl.BlockSpec((D, tf), lambda i, f: (0, f)),
            pl.BlockSpec((tf, D), lambda i, f: (f, 0)),
            pl.BlockSpec((1, D), lambda i, f: (0, 0)),
        ],
        out_specs=pl.BlockSpec((tm, D), lambda i, f: (i, 0)),
        out_shape=jax.ShapeDtypeStruct((S, D), F32),
        scratch_shapes=[pltpu.VMEM((tm, D), BF16), pltpu.VMEM((tm, D), F32)],
        compiler_params=_cparams(("parallel", "arbitrary")),
        name="ffn",
    )(x, gpre, wu, wd, gpost)


def _layer(x, b, pos, norm_mix_pre, w_in, conv_w, conv_b, i_bias, f_bias, mlstm_norm,
           w_branch_m, w_branch_a, w_out, norm_mix_post, norm_ffn_pre, w_up, w_down,
           norm_ffn_post):
    _, S, D = x.shape
    o = 0
    pieces = {}
    w_in = w_in.astype(BF16)
    for name, width in (("mq", M_WIDTH), ("mk", M_WIDTH), ("mv", M_WIDTH), ("mo", M_WIDTH),
                        ("mi", M_HEADS), ("mf", M_HEADS), ("aq", A_WIDTH), ("ak", A_WIDTH),
                        ("av", A_WIDTH), ("gm", D), ("ga", D)):
        pieces[name] = w_in[:, o:o + width]
        o += width
    w_all = jnp.concatenate([pieces[n] for n in ("gm", "ga", "mq", "mk", "aq", "ak", "mv", "mo", "av")],
                            axis=1)
    w_gate = jnp.concatenate(
        [pieces["mi"], pieces["mf"], jnp.zeros((D, LANES - 2 * M_HEADS), BF16)], axis=1)
    gbias = jnp.concatenate([i_bias, f_bias, jnp.zeros((LANES - 2 * M_HEADS,), F32)])[None, :]
    half = jnp.arange(0, A_HEAD_DIM, 2, dtype=F32) / A_HEAD_DIM
    inv_freq = 1.0 / (ROPE_THETA ** half)
    invf = jnp.concatenate([inv_freq, inv_freq])[None, :]

    P, gate = _proj(x, b, norm_mix_pre[None, :], w_all, w_gate)
    mq, mkT, aq, ak, avT, kmean, gc, gt = _prep(
        P, gate, pos.reshape(S, 1), invf, conv_w, conv_b[None, :], gbias)
    hm = _mlstm(mq, mkT, P, gc, gt, mlstm_norm[None, :])
    ha = _moba_routed(aq, ak, avT, kmean.reshape(S // MOBA_BLOCK, A_WIDTH))
    x1 = _merge(hm, ha, P, x, b, w_branch_m.astype(BF16), w_branch_a.astype(BF16),
                w_out.astype(BF16), norm_mix_post[None, :])
    return _ffn(x1, norm_ffn_pre[None, :], w_up.astype(BF16), w_down.astype(BF16),
                norm_ffn_post[None, :])


def kernel(x, positions, norm_mix_pre, w_in, conv_w, conv_b, i_bias, f_bias, mlstm_norm,
           w_branch_m, w_branch_a, w_out, norm_mix_post, norm_ffn_pre, w_up, w_down,
           norm_ffn_post):
    B = x.shape[0]
    depth = w_in.shape[0]
    outs = []
    def take(a, i):
        return a.reshape(a.shape[1:]) if a.shape[0] == 1 else a[i]

    for b in range(B):
        xin, bi = x, b
        for l in range(depth):
            xb = _layer(xin, bi, take(positions, b), take(norm_mix_pre, l), take(w_in, l),
                        take(conv_w, l), take(conv_b, l), take(i_bias, l), take(f_bias, l),
                        take(mlstm_norm, l), take(w_branch_m, l), take(w_branch_a, l),
                        take(w_out, l), take(norm_mix_post, l), take(norm_ffn_pre, l),
                        take(w_up, l), take(w_down, l), take(norm_ffn_post, l))
            xin, bi = xb[None], 0
        outs.append(xb)
    return outs[0][None] if B == 1 else jnp.stack(outs, axis=0)
```

```python
import functools

import jax
import jax.numpy as jnp
from jax import lax
from jax.experimental import pallas as pl
from jax.experimental.pallas import tpu as pltpu
from jax.experimental.pallas import tpu_sc as plsc

F32 = jnp.float32
BF16 = jnp.bfloat16

M_HEADS = 4
M_HEAD_DIM = 256
M_WIDTH = M_HEADS * M_HEAD_DIM
M_CHUNK = 128
CONV_WIDTH = 4
A_HEADS = 8
A_HEAD_DIM = 128
A_WIDTH = A_HEADS * A_HEAD_DIM
MOBA_BLOCK = 256
MOBA_TOPK = 3
ROPE_THETA = 10000.0
NORM_EPS = 1e-6

LANES = 128
SUBLANES = 8
VMEM_LIMIT = 56 * 1024 * 1024
NEG_BIG = -1e30
LOG2_E = 1.4426950408889634

COL_GM = 0
COL_GA = 2048
COL_MQK = 4096
COL_AQK = 6144
COL_MV = 8192
COL_MO = 9216
COL_AV = 10240


def _cparams(sem):
    return pltpu.CompilerParams(dimension_semantics=sem, vmem_limit_bytes=VMEM_LIMIT)


def _rms(x, gain):
    ms = jnp.mean(x * x, axis=-1, keepdims=True)
    return x * lax.rsqrt(ms + NORM_EPS) * gain


def _sigmoid(x):
    return 1.0 / (1.0 + jnp.exp(-x))


def _split3(x):
    hi = x.astype(BF16)
    r1 = x - hi.astype(F32)
    mid = r1.astype(BF16)
    lo = (r1 - mid.astype(F32)).astype(BF16)
    return hi, mid, lo


def _proj_kernel(x_ref, g_ref, w_ref, wg_ref, p_ref, gate_ref, xn_ref):
    @pl.when(pl.program_id(1) == 0)
    def _():
        xn = _rms(x_ref[...], g_ref[...]).astype(BF16)
        xn_ref[...] = xn
        gate_ref[...] = jnp.dot(xn, wg_ref[...], preferred_element_type=F32)

    p_ref[...] = jnp.dot(xn_ref[...], w_ref[...], preferred_element_type=F32).astype(BF16)


def _proj(x, b, gain, w_all, w_gate, tm=1024, tn=1024):
    _, S, D = x.shape
    N = w_all.shape[1]
    return pl.pallas_call(
        _proj_kernel,
        grid=(S // tm, N // tn),
        in_specs=[
            pl.BlockSpec((None, tm, D), lambda i, j: (b, i, 0)),
            pl.BlockSpec((1, D), lambda i, j: (0, 0)),
            pl.BlockSpec((D, tn), lambda i, j: (0, j)),
            pl.BlockSpec((D, LANES), lambda i, j: (0, 0)),
        ],
        out_specs=[
            pl.BlockSpec((tm, tn), lambda i, j: (i, j)),
            pl.BlockSpec((tm, LANES), lambda i, j: (i, 0)),
        ],
        out_shape=[
            jax.ShapeDtypeStruct((S, N), BF16),
            jax.ShapeDtypeStruct((S, LANES), F32),
        ],
        scratch_shapes=[pltpu.VMEM((tm, D), BF16)],
        compiler_params=_cparams(("parallel", "arbitrary")),
        name="proj",
    )(x, gain, w_all, w_gate)


PREP_ROWS = MOBA_BLOCK
PREP_COLS = 512
HALO_ROWS = 2 * SUBLANES


def _prep_kernel(pmk_ref, halo_ref, paqk_ref, pav_ref, gate_ref, pos_ref, invf_ref,
                 cw_ref, cb_ref, gb_ref,
                 mq_ref, mkT_ref, aq_ref, ak_ref, avT_ref, kmean_ref, gc_ref, gt_ref):
    i = pl.program_id(0)
    R = PREP_ROWS

    k_scale = M_HEAD_DIM ** -0.5
    for c0 in range(0, 2 * M_WIDTH, PREP_COLS):
        cs = slice(c0, c0 + PREP_COLS)
        prev = halo_ref[:, cs].astype(F32)[HALO_ROWS - SUBLANES:, :]
        prev = jnp.where(i == 0, jnp.zeros_like(prev), prev)
        ext = jnp.concatenate([pmk_ref[:, cs].astype(F32), prev], axis=0)
        acc = cw_ref[0:1, cs] * ext
        for j in range(1, CONV_WIDTH):
            acc = pltpu.roll(acc, 1, 0) + cw_ref[j:j + 1, cs] * ext
        acc = acc[0:R, :] + cb_ref[:, cs]
        y = acc * _sigmoid(acc)
        if c0 < M_WIDTH:
            mq_ref[:, cs] = y.astype(BF16)
        else:
            ks = slice(c0 - M_WIDTH, c0 - M_WIDTH + PREP_COLS)
            mkT_ref[ks, :] = (y * k_scale).T.astype(BF16)

    ang = pos_ref[...].astype(F32) * invf_ref[...]
    cos = jnp.cos(ang)
    lane = lax.broadcasted_iota(jnp.int32, (R, A_HEAD_DIM), 1)
    sin_signed = jnp.where(lane < A_HEAD_DIM // 2, -1.0, 1.0) * jnp.sin(ang)
    for h in range(2 * A_HEADS):
        hs = slice(h * A_HEAD_DIM, (h + 1) * A_HEAD_DIM)
        xh = paqk_ref[:, hs].astype(F32)
        yh = xh * cos + pltpu.roll(xh, A_HEAD_DIM // 2, 1) * sin_signed
        if h < A_HEADS:
            aq_ref[h] = yh
        else:
            ko = slice((h - A_HEADS) * A_HEAD_DIM, (h - A_HEADS + 1) * A_HEAD_DIM)
            ak_ref[:, ko] = yh.astype(BF16)
            kmean_ref[0, :, ko] = jnp.mean(yh, axis=0, keepdims=True)
    for c0 in range(0, A_WIDTH, PREP_COLS):
        avT_ref[c0:c0 + PREP_COLS, :] = pav_ref[:, c0:c0 + PREP_COLS].astype(F32).T.astype(BF16)

    g = gate_ref[...] + gb_ref[...]
    log_f = jnp.minimum(g, 0.0) - jnp.log1p(jnp.exp(-jnp.abs(g)))
    r_i = lax.broadcasted_iota(jnp.int32, (R, R), 0)
    c_i = lax.broadcasted_iota(jnp.int32, (R, R), 1)
    tri = ((r_i >= c_i) & ((r_i // M_CHUNK) == (c_i // M_CHUNK))).astype(BF16)
    hi, mid, lo = _split3(log_f)
    csum = (jnp.dot(tri, hi, preferred_element_type=F32)
            + jnp.dot(tri, mid, preferred_element_type=F32)
            + jnp.dot(tri, lo, preferred_element_type=F32))
    glane = lax.broadcasted_iota(jnp.int32, (R, LANES), 1)
    gc = jnp.where(glane < M_HEADS, g, csum)
    gc_ref[...] = gc
    gt_ref[...] = gc.T[0:SUBLANES, :]


def _prep(P, gate, pos, invf, conv_w, conv_b, gbias):
    S = P.shape[0]
    R = PREP_ROWS
    nb = S // R
    halo_blocks = R // HALO_ROWS
    return pl.pallas_call(
        _prep_kernel,
        grid=(nb,),
        in_specs=[
            pl.BlockSpec((R, 2 * M_WIDTH), lambda i: (i, COL_MQK // (2 * M_WIDTH))),
            pl.BlockSpec((HALO_ROWS, 2 * M_WIDTH),
                         lambda i: (jnp.maximum(i * halo_blocks - 1, 0), COL_MQK // (2 * M_WIDTH))),
            pl.BlockSpec((R, 2 * A_WIDTH), lambda i: (i, COL_AQK // (2 * A_WIDTH))),
            pl.BlockSpec((R, A_WIDTH), lambda i: (i, COL_AV // A_WIDTH)),
            pl.BlockSpec((R, LANES), lambda i: (i, 0)),
            pl.BlockSpec((R, 1), lambda i: (i, 0)),
            pl.BlockSpec((1, A_HEAD_DIM), lambda i: (0, 0)),
            pl.BlockSpec((CONV_WIDTH, 2 * M_WIDTH), lambda i: (0, 0)),
            pl.BlockSpec((1, 2 * M_WIDTH), lambda i: (0, 0)),
            pl.BlockSpec((1, LANES), lambda i: (0, 0)),
        ],
        out_specs=[
            pl.BlockSpec((R, M_WIDTH), lambda i: (i, 0)),
            pl.BlockSpec((M_WIDTH, R), lambda i: (0, i)),
            pl.BlockSpec((A_HEADS, R, A_HEAD_DIM), lambda i: (0, i, 0)),
            pl.BlockSpec((R, A_WIDTH), lambda i: (i, 0)),
            pl.BlockSpec((A_WIDTH, R), lambda i: (0, i)),
            pl.BlockSpec((1, 1, A_WIDTH), lambda i: (i, 0, 0)),
            pl.BlockSpec((R, LANES), lambda i: (i, 0)),
            pl.BlockSpec((SUBLANES, R), lambda i: (0, i)),
        ],
        out_shape=[
            jax.ShapeDtypeStruct((S, M_WIDTH), BF16),
            jax.ShapeDtypeStruct((M_WIDTH, S), BF16),
            jax.ShapeDtypeStruct((A_HEADS, S, A_HEAD_DIM), F32),
            jax.ShapeDtypeStruct((S, A_WIDTH), BF16),
            jax.ShapeDtypeStruct((A_WIDTH, S), BF16),
            jax.ShapeDtypeStruct((nb, 1, A_WIDTH), F32),
            jax.ShapeDtypeStruct((S, LANES), F32),
            jax.ShapeDtypeStruct((SUBLANES, S), F32),
        ],
        compiler_params=_cparams(("parallel",)),
        name="prep",
    )(P, P, P, P, gate, pos, invf, conv_w, conv_b, gbias)


M_AUG = M_HEAD_DIM + LANES
MLSTM_CHUNKS_PER_STEP = 4


def _mlstm_kernel(q_ref, kT_ref, v_ref, mo_ref, gc_ref, gt_ref, gain_ref, out_ref,
                  c_ref, m_ref):
    @pl.when(pl.program_id(0) == 0)
    def _():
        c_ref[...] = jnp.zeros_like(c_ref)
        m_ref[...] = jnp.zeros_like(m_ref)

    L = M_CHUNK
    D = M_HEAD_DIM
    row = lax.broadcasted_iota(jnp.int32, (L, L), 0)
    col = lax.broadcasted_iota(jnp.int32, (L, L), 1)
    causal = row >= col
    ones_col = (lax.broadcasted_iota(jnp.int32, (L, LANES), 1) == 0).astype(BF16)

    heads = range(M_HEADS)
    hsl = [slice(h * D, (h + 1) * D) for h in heads]

    def state_stages(rs):
        q = [q_ref[rs, hsl[h]] for h in heads]
        kT = [kT_ref[hsl[h], rs] for h in heads]
        v_aug = [jnp.concatenate([v_ref[rs, hsl[h]], ones_col], axis=1) for h in heads]
        b_c = [gc_ref[rs, M_HEADS + h:M_HEADS + h + 1] for h in heads]
        b_r = [gt_ref[M_HEADS + h:M_HEADS + h + 1, rs] for h in heads]
        u_r = [gt_ref[h:h + 1, rs] - b_r[h] for h in heads]
        f_tot = [b_r[h][:, L - 1:L] for h in heads]
        m_prev = [m_ref[h:h + 1, 0:1] for h in heads]

        s_qk = [jnp.dot(q[h], kT[h], preferred_element_type=F32) for h in heads]
        q_c = [jnp.dot(q[h], c_ref[h].astype(BF16), preferred_element_type=F32) for h in heads]

        for h in heads:
            w_r = f_tot[h] + u_r[h]
            m_loc = jnp.max(w_r, axis=1, keepdims=True)
            m_new = jnp.maximum(f_tot[h] + m_prev[h], m_loc)
            a = jnp.exp(f_tot[h] + m_prev[h] - m_new)
            e_r = jnp.exp(w_r - m_new)
            keT = (kT[h].astype(F32) * e_r).astype(BF16)
            c_ref[h] = a * c_ref[h] + jnp.dot(keT, v_aug[h], preferred_element_type=F32)
            m_ref[h:h + 1, :] = jnp.broadcast_to(m_new, (1, LANES))
        return rs, v_aug, b_c, u_r, m_prev, s_qk, q_c

    def output_stage(rs, v_aug, b_c, u_r, m_prev, s_qk, q_c):
        for h in heads:
            d_log = jnp.where(causal, b_c[h] + u_r[h], -jnp.inf)
            a_log = b_c[h] + m_prev[h]
            m_t = jnp.maximum(a_log, jnp.max(d_log, axis=1, keepdims=True))
            s_ts = s_qk[h] * jnp.exp(d_log - m_t)
            inter = jnp.exp(a_log - m_t)
            r = inter * q_c[h] + jnp.dot(s_ts.astype(BF16), v_aug[h], preferred_element_type=F32)
            num = r[:, :D]
            den = r[:, D:D + 1]
            hh = num / jnp.maximum(jnp.abs(den), jnp.exp(-m_t))
            hn = hh * lax.rsqrt(jnp.mean(hh * hh, axis=-1, keepdims=True) + NORM_EPS)
            out_ref[rs, hsl[h]] = (hn * gain_ref[:, hsl[h]]
                                   * _sigmoid(mo_ref[rs, hsl[h]].astype(F32))).astype(BF16)

    def chunk(c, carry):
        output_stage(*state_stages(pl.ds(pl.multiple_of(c * L, L), L)))
        return carry

    lax.fori_loop(0, MLSTM_CHUNKS_PER_STEP, chunk, 0)


def _mlstm(mq, mkT, P, gc, gt, gain):
    S = mq.shape[0]
    L = M_CHUNK * MLSTM_CHUNKS_PER_STEP
    return pl.pallas_call(
        _mlstm_kernel,
        grid=(S // L,),
        in_specs=[
            pl.BlockSpec((L, M_WIDTH), lambda c: (c, 0)),
            pl.BlockSpec((M_WIDTH, L), lambda c: (0, c)),
            pl.BlockSpec((L, M_WIDTH), lambda c: (c, COL_MV // M_WIDTH)),
            pl.BlockSpec((L, M_WIDTH), lambda c: (c, COL_MO // M_WIDTH)),
            pl.BlockSpec((L, LANES), lambda c: (c, 0)),
            pl.BlockSpec((SUBLANES, L), lambda c: (0, c)),
            pl.BlockSpec((1, M_WIDTH), lambda c: (0, 0)),
        ],
        out_specs=pl.BlockSpec((L, M_WIDTH), lambda c: (c, 0)),
        out_shape=jax.ShapeDtypeStruct((S, M_WIDTH), BF16),
        scratch_shapes=[
            pltpu.VMEM((M_HEADS, M_HEAD_DIM, M_AUG), F32),
            pltpu.VMEM((SUBLANES, LANES), F32),
        ],
        compiler_params=_cparams(("arbitrary",)),
        name="mlstm",
    )(mq, mkT, P, P, gc, gt, gain)


MOBA_HEADS_PER_GROUP = 4
ROUTE_TILES = 264
ROUTE_ROWS = ROUTE_TILES * MOBA_BLOCK
GROUP_TILES_PER_STEP = 24
ROW_WORDS = A_HEAD_DIM // 2
SC_WINDOW = 256


def _pack_rows(o_t, lse):
    q = o_t.shape[1]
    hi = pltpu.bitcast(o_t[:ROW_WORDS].astype(BF16).astype(F32), jnp.uint32)
    lo = pltpu.bitcast(o_t[ROW_WORDS:].astype(BF16).astype(F32), jnp.uint32)
    words = pltpu.bitcast(hi | (lo >> 16), F32)
    tail = jnp.concatenate([jnp.broadcast_to(lse, (SUBLANES, q)),
                            jnp.zeros((LANES - ROW_WORDS - SUBLANES, q), F32)], axis=0)
    return jnp.concatenate([words, tail], axis=0).T


def _dot_nt(a, b):
    return lax.dot_general(a, b, (((1,), (1,)), ((), ())), preferred_element_type=F32)


def _block_partial(k_blk, v_t_blk, qs, ones_rows, causal_mask=None):
    s = _dot_nt(k_blk, qs)
    if causal_mask is not None:
        s = jnp.where(causal_mask, s, NEG_BIG)
    sb = s.astype(BF16)
    m = jnp.max(sb, axis=0, keepdims=True)
    p = jnp.exp2(sb - m)
    r = jnp.dot(jnp.concatenate([v_t_blk, ones_rows], axis=0), p, preferred_element_type=F32)
    dh = v_t_blk.shape[0]
    l = r[dh:dh + 1, :]
    return r[:dh, :] / l, m.astype(F32) + jnp.log2(l)


def _route_kernel(q_ref, k_ref, vT_ref, km_ref, route_ref, cnt_ref, own_ref, cnt_acc):
    i = pl.program_id(1)
    BS = MOBA_BLOCK
    Dh = A_HEAD_DIM
    NB = km_ref.shape[0]
    G = q_ref.shape[0]
    blk = lax.broadcasted_iota(jnp.int32, (NB, BS), 0)
    kpos = lax.broadcasted_iota(jnp.int32, (BS, BS), 0)
    qpos = lax.broadcasted_iota(jnp.int32, (BS, BS), 1)
    earlier = (kpos < qpos).astype(BF16)
    ones_rows = jnp.ones((2 * SUBLANES, BS), BF16)

    @pl.when(i == 0)
    def _():
        cnt_acc[...] = jnp.zeros_like(cnt_acc)

    q = [q_ref[g] for g in range(G)]
    qs = [(q[g] * (Dh ** -0.5 * LOG2_E)).astype(BF16) for g in range(G)]
    parts = [_block_partial(k_ref[:, g * Dh:(g + 1) * Dh], vT_ref[g * Dh:(g + 1) * Dh, :], qs[g],
                            ones_rows, kpos <= qpos) for g in range(G)]

    gates = []
    for g in range(G):
        hs = slice(g * Dh, (g + 1) * Dh)
        km = km_ref[:, hs]
        kh = km.astype(BF16)
        kl = (km - kh.astype(F32)).astype(BF16)
        qh = q[g].astype(BF16)
        ql = (q[g] - qh.astype(F32)).astype(BF16)
        gate = _dot_nt(kh, qh) + _dot_nt(kh, ql) + _dot_nt(kl, qh)
        gates.append(jnp.where(blk < i, gate, -jnp.inf))
    picks = [[] for _ in range(G)]
    rows = [[] for _ in range(G)]
    for r in range(MOBA_TOPK):
        for g in range(G):
            mx = jnp.max(gates[g], axis=0, keepdims=True)
            idx = jnp.min(jnp.where(gates[g] == mx, blk, NB), axis=0, keepdims=True)
            idx = jnp.where(r < i, idx, -1)
            pick = blk == idx
            gates[g] = jnp.where(pick, -jnp.inf, gates[g])
            picks[g].append(pick)
            rows[g].append(idx)
    for g in range(G):
        onehot = sum(p.astype(F32) for p in picks[g])
        before = cnt_acc[g][:, 0:1] + jnp.dot(onehot.astype(BF16), earlier, preferred_element_type=F32)
        for r in range(MOBA_TOPK):
            rank = jnp.sum(jnp.where(picks[g][r], before, 0.0), axis=0, keepdims=True)
            rows[g].append(rank.astype(jnp.int32))
        rows[g].append(jnp.zeros((SUBLANES - 2 * MOBA_TOPK, BS), jnp.int32))
        route_ref[g] = jnp.concatenate(rows[g], axis=0)
        cnt_new = cnt_acc[g] + jnp.sum(onehot, axis=1, keepdims=True)
        cnt_acc[g] = cnt_new
        cnt_ref[g] = cnt_new
        own_ref[g] = _pack_rows(*parts[g])


def _route(aq_hm, ak, avT, kmean, h0, H):
    _, S, Dh = aq_hm.shape
    BS = MOBA_BLOCK
    NB = S // BS
    G = MOBA_HEADS_PER_GROUP
    W = G * Dh
    hb = h0 // G
    return pl.pallas_call(
        _route_kernel,
        grid=(H // G, NB),
        in_specs=[
            pl.BlockSpec((G, BS, Dh), lambda h, i: (hb + h, i, 0)),
            pl.BlockSpec((BS, W), lambda h, i: (i, hb + h)),
            pl.BlockSpec((W, BS), lambda h, i: (hb + h, i)),
            pl.BlockSpec((NB, W), lambda h, i: (0, hb + h)),
        ],
        out_specs=[
            pl.BlockSpec((G, SUBLANES, BS), lambda h, i: (h, 0, i)),
            pl.BlockSpec((G, NB, LANES), lambda h, i: (h, 0, 0)),
            pl.BlockSpec((G, BS, LANES), lambda h, i: (h, i, 0)),
        ],
        out_shape=[
            jax.ShapeDtypeStruct((H, SUBLANES, S), jnp.int32),
            jax.ShapeDtypeStruct((H, NB, LANES), F32),
            jax.ShapeDtypeStruct((H, S, LANES), F32),
        ],
        scratch_shapes=[pltpu.VMEM((G, NB, LANES), F32)],
        compiler_params=_cparams(("parallel", "arbitrary")),
        name="moba_route",
    )(aq_hm, ak, avT, kmean)


def _dest_kernel(seg_ref, route_ref, dest_ref):
    h = pl.program_id(0)
    NB = seg_ref.shape[1]
    r = route_ref[0]
    blk = r[0:MOBA_TOPK, :]
    rank = r[MOBA_TOPK:2 * MOBA_TOPK, :]
    base = h * ROUTE_ROWS
    dest = jnp.full(blk.shape, base + ROUTE_ROWS - 1, jnp.int32)
    for j in range(NB):
        dest = jnp.where(blk == j, base + seg_ref[h, j] + rank, dest)
    dest_ref[0] = jnp.concatenate(
        [dest, jnp.zeros((SUBLANES - MOBA_TOPK, dest.shape[1]), jnp.int32)], axis=0)


def _dest(seg_start, route):
    H, _, S = route.shape
    return pl.pallas_call(
        _dest_kernel,
        grid_spec=pltpu.PrefetchScalarGridSpec(
            num_scalar_prefetch=1,
            grid=(H,),
            in_specs=[pl.BlockSpec((1, SUBLANES, S), lambda h, seg: (h, 0, 0))],
            out_specs=pl.BlockSpec((1, SUBLANES, S), lambda h, seg: (h, 0, 0)),
        ),
        out_shape=jax.ShapeDtypeStruct((H, SUBLANES, S), jnp.int32),
        compiler_params=_cparams(("parallel",)),
        name="moba_dest",
    )(seg_start, route)


def _sc_mesh():
    return plsc.VectorSubcoreMesh(core_axis_name="c", subcore_axis_name="s")


def _sc_scatter(rows, row0, n_rows, idx, n_out):
    W = rows.shape[1]
    M = idx.shape[0]
    n_win = n_rows // SC_WINDOW
    win0 = row0 // SC_WINDOW

    reps = M // n_rows

    @pl.kernel(out_type=jax.ShapeDtypeStruct((n_out, W), rows.dtype), mesh=_sc_mesh())
    def k(x_hbm, i_hbm, o_hbm):
        def body(x_vmem, *i_vmems):
            for i_vmem in i_vmems:
                pltpu.sync_copy(x_vmem, o_hbm.at[i_vmem.at[0]])

        pltpu.emit_pipeline(
            body,
            grid=(n_win,),
            in_specs=([pl.BlockSpec((SC_WINDOW, W), lambda w: (win0 + w, 0))]
                      + [pl.BlockSpec((1, SC_WINDOW), lambda w, r=r: (0, r * n_win + w)) for r in range(reps)]),
            out_specs=[],
            core_axis_name=("c", "s"),
            dimension_semantics=(pltpu.PARALLEL,),
        )(x_hbm, *([i_hbm] * reps))

    return k(rows, idx.reshape(1, M))


def _sc_gather(table, idx):
    M = idx.shape[0]
    W = table.shape[1]

    @pl.kernel(out_type=jax.ShapeDtypeStruct((M, W), table.dtype), mesh=_sc_mesh())
    def k(x_hbm, i_hbm, o_hbm):
        def body(i_vmem, o_vmem):
            pltpu.sync_copy(x_hbm.at[i_vmem.at[0]], o_vmem)

        pltpu.emit_pipeline(
            body,
            grid=(M // SC_WINDOW,),
            in_specs=[pl.BlockSpec((1, SC_WINDOW), lambda w: (0, w))],
            out_specs=[pl.BlockSpec((SC_WINDOW, W), lambda w: (w, 0))],
            core_axis_name=("c", "s"),
            dimension_semantics=(pltpu.PARALLEL,),
        )(i_hbm, o_hbm)

    return k(table, idx.reshape(1, M))


def _group_kernel(tile_blk_ref, q_ref, k_ref, vT_ref, out_ref, s_ref, m_ref):
    h = pl.program_id(0)
    u = pl.program_id(1)
    last = pl.num_programs(1) - 2
    BS = MOBA_BLOCK
    Dh = A_HEAD_DIM
    T = GROUP_TILES_PER_STEP
    ones_rows = jnp.ones((2 * SUBLANES, BS), BF16)
    new = lax.rem(u, 2)
    old = 1 - new

    @pl.when((h == 0) & (u == 0))
    def _():
        s_ref[...] = jnp.zeros_like(s_ref)
        m_ref[...] = jnp.zeros_like(m_ref)

    def block_start(group, c):
        j = jnp.maximum(tile_blk_ref[h * ROUTE_TILES + group * T + c], 0)
        return pl.multiple_of(j * BS, BS)

    g_new = jnp.minimum(u, last)
    g_old = jnp.maximum(u - 1, 0)
    @pl.when(tile_blk_ref[h * ROUTE_TILES + g_old * T] >= 0)
    def _():
        ms, rs = [], []
        for c in range(T):
            m = m_ref[old, c]
            p = jnp.exp2(s_ref[old, c] - m.astype(BF16))
            v_aug = jnp.concatenate([vT_ref[:, pl.ds(block_start(g_old, c), BS)], ones_rows], axis=0)
            ms.append(m)
            rs.append(jnp.dot(v_aug, p, preferred_element_type=F32))
        s_new = []
        for c in range(T):
            qs = (q_ref[c * BS:(c + 1) * BS, :] * (Dh ** -0.5 * LOG2_E)).astype(BF16)
            s_new.append(_dot_nt(k_ref[pl.ds(block_start(g_new, c), BS), :], qs))
        for c in range(T):
            l = rs[c][Dh:Dh + 1, :]
            out_ref[c * BS:(c + 1) * BS, :] = _pack_rows(rs[c][:Dh, :] / l, ms[c] + jnp.log2(l))
            sb = s_new[c].astype(BF16)
            s_ref[new, c] = sb
            m_ref[new, c] = jnp.max(sb, axis=0, keepdims=True).astype(F32)


def _group(tile_blk, q_sorted, ak, avT, h0):
    S = ak.shape[0]
    H = q_sorted.shape[0] // ROUTE_ROWS
    Dh = A_HEAD_DIM
    BS = MOBA_BLOCK
    T = GROUP_TILES_PER_STEP
    steps = ROUTE_TILES // T
    rows = T * BS
    return pl.pallas_call(
        _group_kernel,
        grid_spec=pltpu.PrefetchScalarGridSpec(
            num_scalar_prefetch=1,
            grid=(H, steps + 1),
            in_specs=[
                pl.BlockSpec((rows, LANES), lambda h, u, tb: (h * steps + jnp.minimum(u, steps - 1), 0)),
                pl.BlockSpec((S, Dh), lambda h, u, tb: (0, h0 + h)),
                pl.BlockSpec((Dh, S), lambda h, u, tb: (h0 + h, 0)),
            ],
            out_specs=pl.BlockSpec((rows, LANES), lambda h, u, tb: (h * steps + jnp.maximum(u - 1, 0), 0)),
            scratch_shapes=[pltpu.VMEM((2, T, BS, BS), BF16), pltpu.VMEM((2, T, 1, BS), F32)],
        ),
        out_shape=jax.ShapeDtypeStruct((H * ROUTE_ROWS, LANES), F32),
        compiler_params=_cparams(("arbitrary", "arbitrary")),
        name="moba_group",
    )(tile_blk, q_sorted, ak, avT)


def _combine_kernel(*refs):
    i = pl.program_id(0)
    o_ref = refs[-1]
    n_groups = (len(refs) - 1) // 2
    H = refs[n_groups].shape[0]
    BS = refs[n_groups].shape[1]
    low_half = lax.broadcasted_iota(jnp.int32, (BS, LANES), 1) < ROW_WORDS
    for head in range(n_groups * H):
        got_ref, own_ref, h = refs[head // H], refs[n_groups + head // H], head % H
        tiles = [own_ref[h]]
        for r in range(MOBA_TOPK):
            tiles.append(jnp.where(r < i, got_ref[h, r], 0.0))
        lses = [tiles[0][:, ROW_WORDS:ROW_WORDS + 1]]
        lses += [jnp.where(r < i, tiles[r + 1][:, ROW_WORDS:ROW_WORDS + 1], NEG_BIG)
                 for r in range(MOBA_TOPK)]
        top = functools.reduce(jnp.maximum, lses)
        w = [jnp.exp2(x - top) for x in lses]
        inv = 1.0 / sum(w)
        first = jnp.zeros((BS, LANES), F32)
        second = jnp.zeros((BS, LANES), F32)
        for wk, tile in zip(w, tiles):
            words = pltpu.bitcast(tile, jnp.uint32)
            first = first + wk * pltpu.bitcast(words & jnp.uint32(0xFFFF0000), F32)
            second = second + wk * pltpu.bitcast(words << 16, F32)
        o = jnp.where(low_half, first, pltpu.roll(second, ROW_WORDS, 1)) * inv
        o_ref[:, head * A_HEAD_DIM:(head + 1) * A_HEAD_DIM] = o.astype(BF16)


def _combine(gots, owns):
    H, S, _ = owns[0].shape
    BS = MOBA_BLOCK
    return pl.pallas_call(
        _combine_kernel,
        grid=(S // BS,),
        in_specs=([pl.BlockSpec((H, MOBA_TOPK, BS, LANES), lambda i: (0, 0, i, 0))] * len(gots)
                  + [pl.BlockSpec((H, BS, LANES), lambda i: (0, i, 0))] * len(owns)),
        out_specs=pl.BlockSpec((BS, A_WIDTH), lambda i: (i, 0)),
        out_shape=jax.ShapeDtypeStruct((S, A_WIDTH), BF16),
        compiler_params=_cparams(("parallel",)),
        name="moba_combine",
    )(*gots, *owns)


def _moba_routed(aq_hm, ak, avT, kmean):
    H, S, Dh = aq_hm.shape
    NB = S // MOBA_BLOCK
    hn = MOBA_HEADS_PER_GROUP
    q_rows = aq_hm.reshape(H * S, Dh)
    gots, owns = [], []
    for h0 in range(0, H, hn):
        route, cnt, own = _route(aq_hm, ak, avT, kmean, h0, hn)
        cnt = cnt[:, :, 0].astype(jnp.int32)
        seg_tiles = (cnt + MOBA_BLOCK - 1) // MOBA_BLOCK
        seg_end = jnp.cumsum(seg_tiles, axis=1)
        seg_start = (seg_end - seg_tiles) * MOBA_BLOCK
        tile_ids = jnp.arange(ROUTE_TILES, dtype=jnp.int32)
        tile_blk = jnp.sum(tile_ids[None, :, None] >= seg_end[:, None, :], axis=2).astype(jnp.int32)
        tile_blk = jnp.where(tile_blk < NB, tile_blk, -1).reshape(hn * ROUTE_TILES)
        dest = _dest(seg_start, route)[:, :MOBA_TOPK, :]
        q_sorted = _sc_scatter(q_rows, h0 * S, hn * S, dest.transpose(1, 0, 2).reshape(-1),
                               hn * ROUTE_ROWS)
        results = _group(tile_blk, q_sorted, ak, avT, h0)
        gots.append(_sc_gather(results, dest.reshape(-1)).reshape(hn, MOBA_TOPK, S, LANES))
        owns.append(own)
    return _combine(gots, owns)


def _merge_kernel(hm_ref, ha_ref, gm_ref, ga_ref, x_ref, wm_ref, wa_ref, wo_ref, gain_ref,
                  out_ref):
    ym = jnp.dot(hm_ref[...], wm_ref[...], preferred_element_type=F32)
    ya = jnp.dot(ha_ref[...], wa_ref[...], preferred_element_type=F32)
    merged = _sigmoid(gm_ref[...].astype(F32)) * ym + _sigmoid(ga_ref[...].astype(F32)) * ya
    mix = jnp.dot(merged.astype(BF16), wo_ref[...], preferred_element_type=F32)
    out_ref[...] = x_ref[...] + _rms(mix, gain_ref[...])


def _merge(hm, ha, P, x, b, wm, wa, wo, gain, tm=256):
    _, S, D = x.shape
    const = pl.Buffered(1)
    return pl.pallas_call(
        _merge_kernel,
        grid=(S // tm,),
        in_specs=[
            pl.BlockSpec((tm, M_WIDTH), lambda i: (i, 0)),
            pl.BlockSpec((tm, A_WIDTH), lambda i: (i, 0)),
            pl.BlockSpec((tm, D), lambda i: (i, COL_GM // D)),
            pl.BlockSpec((tm, D), lambda i: (i, COL_GA // D)),
            pl.BlockSpec((None, tm, D), lambda i: (b, i, 0)),
            pl.BlockSpec((M_WIDTH, D), lambda i: (0, 0), pipeline_mode=const),
            pl.BlockSpec((A_WIDTH, D), lambda i: (0, 0), pipeline_mode=const),
            pl.BlockSpec((D, D), lambda i: (0, 0), pipeline_mode=const),
            pl.BlockSpec((1, D), lambda i: (0, 0)),
        ],
        out_specs=pl.BlockSpec((tm, D), lambda i: (i, 0)),
        out_shape=jax.ShapeDtypeStruct((S, D), F32),
        compiler_params=_cparams(("parallel",)),
        name="merge",
    )(hm, ha, P, P, x, wm, wa, wo, gain)


def _ffn_kernel(x_ref, gpre_ref, wu_ref, wd_ref, gpost_ref, out_ref, hn_ref, acc_ref):
    f = pl.program_id(1)

    @pl.when(f == 0)
    def _():
        hn_ref[...] = _rms(x_ref[...], gpre_ref[...]).astype(BF16)
        acc_ref[...] = jnp.zeros_like(acc_ref)

    u = jnp.dot(hn_ref[...], wu_ref[...], preferred_element_type=F32)
    u = jnp.square(jnp.maximum(u, 0.0)).astype(BF16)
    acc_ref[...] += jnp.dot(u, wd_ref[...], preferred_element_type=F32)

    @pl.when(f == pl.num_programs(1) - 1)
    def _():
        out_ref[...] = x_ref[...] + _rms(acc_ref[...], gpost_ref[...])


def _ffn(x, gpre, wu, wd, gpost, tm=512, tf=1024):
    S, D = x.shape
    Fd = wu.shape[1]
    return pl.pallas_call(
        _ffn_kernel,
        grid=(S // tm, Fd // tf),
        in_specs=[
            pl.BlockSpec((tm, D), lambda i, f: (i, 0)),
            pl.BlockSpec((1, D), lambda i, f: (0, 0)),
            pl.BlockSpec((D, tf), lambda i, f: (0, f)),
            pl.BlockSpec((tf, D), lambda i, f: (f, 0)),
            pl.BlockSpec((1, D), lambda i, f: (0, 0)),
        ],
        out_specs=pl.BlockSpec((tm, D), lambda i, f: (i, 0)),
        out_shape=jax.ShapeDtypeStruct((S, D), F32),
        scratch_shapes=[pltpu.VMEM((tm, D), BF16), pltpu.VMEM((tm, D), F32)],
        compiler_params=_cparams(("parallel", "arbitrary")),
        name="ffn",
    )(x, gpre, wu, wd, gpost)


def _layer(x, b, pos, norm_mix_pre, w_in, conv_w, conv_b, i_bias, f_bias, mlstm_norm,
           w_branch_m, w_branch_a, w_out, norm_mix_post, norm_ffn_pre, w_up, w_down,
           norm_ffn_post):
    _, S, D = x.shape
    o = 0
    pieces = {}
    w_in = w_in.astype(BF16)
    for name, width in (("mq", M_WIDTH), ("mk", M_WIDTH), ("mv", M_WIDTH), ("mo", M_WIDTH),
                        ("mi", M_HEADS), ("mf", M_HEADS), ("aq", A_WIDTH), ("ak", A_WIDTH),
                        ("av", A_WIDTH), ("gm", D), ("ga", D)):
        pieces[name] = w_in[:, o:o + width]
        o += width
    w_all = jnp.concatenate([pieces[n] for n in ("gm", "ga", "mq", "mk", "aq", "ak", "mv", "mo", "av")],
                            axis=1)
    w_gate = jnp.concatenate(
        [pieces["mi"], pieces["mf"], jnp.zeros((D, LANES - 2 * M_HEADS), BF16)], axis=1)
    gbias = jnp.concatenate([i_bias, f_bias, jnp.zeros((LANES - 2 * M_HEADS,), F32)])[None, :]
    half = jnp.arange(0, A_HEAD_DIM, 2, dtype=F32) / A_HEAD_DIM
    inv_freq = 1.0 / (ROPE_THETA ** half)
    invf = jnp.concatenate([inv_freq, inv_freq])[None, :]

    P, gate = _proj(x, b, norm_mix_pre[None, :], w_all, w_gate)
    mq, mkT, aq, ak, avT, kmean, gc, gt = _prep(
        P, gate, pos.reshape(S, 1), invf, conv_w, conv_b[None, :], gbias)
    hm = _mlstm(mq, mkT, P, gc, gt, mlstm_norm[None, :])
    ha = _moba_routed(aq, ak, avT, kmean.reshape(S // MOBA_BLOCK, A_WIDTH))
    x1 = _merge(hm, ha, P, x, b, w_branch_m.astype(BF16), w_branch_a.astype(BF16),
                w_out.astype(BF16), norm_mix_post[None, :])
    return _ffn(x1, norm_ffn_pre[None, :], w_up.astype(BF16), w_down.astype(BF16),
                norm_ffn_post[None, :])


def kernel(x, positions, norm_mix_pre, w_in, conv_w, conv_b, i_bias, f_bias, mlstm_norm,
           w_branch_m, w_branch_a, w_out, norm_mix_post, norm_ffn_pre, w_up, w_down,
           norm_ffn_post):
    B = x.shape[0]
    depth = w_in.shape[0]
    outs = []
    def take(a, i):
        return a.reshape(a.shape[1:]) if a.shape[0] == 1 else a[i]

    for b in range(B):
        xin, bi = x, b
        for l in range(depth):
            xb = _layer(xin, bi, take(positions, b), take(norm_mix_pre, l), take(w_in, l),
                        take(conv_w, l), take(conv_b, l), take(i_bias, l), take(f_bias, l),
                        take(mlstm_norm, l), take(w_branch_m, l), take(w_branch_a, l),
                        take(w_out, l), take(norm_mix_post, l), take(norm_ffn_pre, l),
                        take(w_up, l), take(w_down, l), take(norm_ffn_post, l))
            xin, bi = xb[None], 0
        outs.append(xb)
    return outs[0][None] if B == 1 else jnp.stack(outs, axis=0)
```

```python
import functools

import jax
import jax.numpy as jnp
from jax import lax
from jax.experimental import pallas as pl
from jax.experimental.pallas import tpu as pltpu
from jax.experimental.pallas import tpu_sc as plsc

F32 = jnp.float32
BF16 = jnp.bfloat16

M_HEADS = 4
M_HEAD_DIM = 256
M_WIDTH = M_HEADS * M_HEAD_DIM
M_CHUNK = 128
CONV_WIDTH = 4
A_HEADS = 8
A_HEAD_DIM = 128
A_WIDTH = A_HEADS * A_HEAD_DIM
MOBA_BLOCK = 256
MOBA_TOPK = 3
ROPE_THETA = 10000.0
NORM_EPS = 1e-6

LANES = 128
SUBLANES = 8
VMEM_LIMIT = 56 * 1024 * 1024
NEG_BIG = -1e30
LOG2_E = 1.4426950408889634

COL_GM = 0
COL_GA = 2048
COL_MQK = 4096
COL_AQK = 6144
COL_MV = 8192
COL_MO = 9216
COL_AV = 10240


def _cparams(sem):
    return pltpu.CompilerParams(dimension_semantics=sem, vmem_limit_bytes=VMEM_LIMIT)


def _rms(x, gain):
    ms = jnp.mean(x * x, axis=-1, keepdims=True)
    return x * lax.rsqrt(ms + NORM_EPS) * gain


def _sigmoid(x):
    return 1.0 / (1.0 + jnp.exp(-x))


def _split3(x):
    hi = x.astype(BF16)
    r1 = x - hi.astype(F32)
    mid = r1.astype(BF16)
    lo = (r1 - mid.astype(F32)).astype(BF16)
    return hi, mid, lo


def _proj_kernel(x_ref, g_ref, w_ref, wg_ref, p_ref, gate_ref, xn_ref):
    @pl.when(pl.program_id(1) == 0)
    def _():
        xn = _rms(x_ref[...], g_ref[...]).astype(BF16)
        xn_ref[...] = xn
        gate_ref[...] = jnp.dot(xn, wg_ref[...], preferred_element_type=F32)

    p_ref[...] = jnp.dot(xn_ref[...], w_ref[...], preferred_element_type=F32).astype(BF16)


def _proj(x, b, gain, w_all, w_gate, tm=1024, tn=1024):
    _, S, D = x.shape
    N = w_all.shape[1]
    return pl.pallas_call(
        _proj_kernel,
        grid=(S // tm, N // tn),
        in_specs=[
            pl.BlockSpec((None, tm, D), lambda i, j: (b, i, 0)),
            pl.BlockSpec((1, D), lambda i, j: (0, 0)),
            pl.BlockSpec((D, tn), lambda i, j: (0, j)),
            pl.BlockSpec((D, LANES), lambda i, j: (0, 0)),
        ],
        out_specs=[
            pl.BlockSpec((tm, tn), lambda i, j: (i, j)),
            pl.BlockSpec((tm, LANES), lambda i, j: (i, 0)),
        ],
        out_shape=[
            jax.ShapeDtypeStruct((S, N), BF16),
            jax.ShapeDtypeStruct((S, LANES), F32),
        ],
        scratch_shapes=[pltpu.VMEM((tm, D), BF16)],
        compiler_params=_cparams(("parallel", "arbitrary")),
        name="proj",
    )(x, gain, w_all, w_gate)


PREP_ROWS = MOBA_BLOCK
PREP_COLS = 512
HALO_ROWS = 2 * SUBLANES


def _prep_kernel(pmk_ref, halo_ref, paqk_ref, pav_ref, gate_ref, pos_ref, invf_ref,
                 cw_ref, cb_ref, gb_ref,
                 mq_ref, mkT_ref, aq_ref, ak_ref, avT_ref, kmean_ref, gc_ref, gt_ref):
    i = pl.program_id(0)
    R = PREP_ROWS

    k_scale = M_HEAD_DIM ** -0.5
    for c0 in range(0, 2 * M_WIDTH, PREP_COLS):
        cs = slice(c0, c0 + PREP_COLS)
        prev = halo_ref[:, cs].astype(F32)[HALO_ROWS - SUBLANES:, :]
        prev = jnp.where(i == 0, jnp.zeros_like(prev), prev)
        ext = jnp.concatenate([pmk_ref[:, cs].astype(F32), prev], axis=0)
        acc = cw_ref[0:1, cs] * ext
        for j in range(1, CONV_WIDTH):
            acc = pltpu.roll(acc, 1, 0) + cw_ref[j:j + 1, cs] * ext
        acc = acc[0:R, :] + cb_ref[:, cs]
        y = acc * _sigmoid(acc)
        if c0 < M_WIDTH:
            mq_ref[:, cs] = y.astype(BF16)
        else:
            ks = slice(c0 - M_WIDTH, c0 - M_WIDTH + PREP_COLS)
            mkT_ref[ks, :] = (y * k_scale).T.astype(BF16)

    ang = pos_ref[...].astype(F32) * invf_ref[...]
    cos = jnp.cos(ang)
    lane = lax.broadcasted_iota(jnp.int32, (R, A_HEAD_DIM), 1)
    sin_signed = jnp.where(lane < A_HEAD_DIM // 2, -1.0, 1.0) * jnp.sin(ang)
    for h in range(2 * A_HEADS):
        hs = slice(h * A_HEAD_DIM, (h + 1) * A_HEAD_DIM)
        xh = paqk_ref[:, hs].astype(F32)
        yh = xh * cos + pltpu.roll(xh, A_HEAD_DIM // 2, 1) * sin_signed
        if h < A_HEADS:
            aq_ref[h] = yh
        else:
            ko = slice((h - A_HEADS) * A_HEAD_DIM, (h - A_HEADS + 1) * A_HEAD_DIM)
            ak_ref[:, ko] = yh.astype(BF16)
            kmean_ref[0, :, ko] = jnp.mean(yh, axis=0, keepdims=True)
    for c0 in range(0, A_WIDTH, PREP_COLS):
        avT_ref[c0:c0 + PREP_COLS, :] = pav_ref[:, c0:c0 + PREP_COLS].astype(F32).T.astype(BF16)

    g = gate_ref[...] + gb_ref[...]
    log_f = jnp.minimum(g, 0.0) - jnp.log1p(jnp.exp(-jnp.abs(g)))
    r_i = lax.broadcasted_iota(jnp.int32, (R, R), 0)
    c_i = lax.broadcasted_iota(jnp.int32, (R, R), 1)
    tri = ((r_i >= c_i) & ((r_i // M_CHUNK) == (c_i // M_CHUNK))).astype(BF16)
    hi, mid, lo = _split3(log_f)
    csum = (jnp.dot(tri, hi, preferred_element_type=F32)
            + jnp.dot(tri, mid, preferred_element_type=F32)
            + jnp.dot(tri, lo, preferred_element_type=F32))
    glane = lax.broadcasted_iota(jnp.int32, (R, LANES), 1)
    gc = jnp.where(glane < M_HEADS, g, csum)
    gc_ref[...] = gc
    gt_ref[...] = gc.T[0:SUBLANES, :]


def _prep(P, gate, pos, invf, conv_w, conv_b, gbias):
    S = P.shape[0]
    R = PREP_ROWS
    nb = S // R
    halo_blocks = R // HALO_ROWS
    return pl.pallas_call(
        _prep_kernel,
        grid=(nb,),
        in_specs=[
            pl.BlockSpec((R, 2 * M_WIDTH), lambda i: (i, COL_MQK // (2 * M_WIDTH))),
            pl.BlockSpec((HALO_ROWS, 2 * M_WIDTH),
                         lambda i: (jnp.maximum(i * halo_blocks - 1, 0), COL_MQK // (2 * M_WIDTH))),
            pl.BlockSpec((R, 2 * A_WIDTH), lambda i: (i, COL_AQK // (2 * A_WIDTH))),
            pl.BlockSpec((R, A_WIDTH), lambda i: (i, COL_AV // A_WIDTH)),
            pl.BlockSpec((R, LANES), lambda i: (i, 0)),
            pl.BlockSpec((R, 1), lambda i: (i, 0)),
            pl.BlockSpec((1, A_HEAD_DIM), lambda i: (0, 0)),
            pl.BlockSpec((CONV_WIDTH, 2 * M_WIDTH), lambda i: (0, 0)),
            pl.BlockSpec((1, 2 * M_WIDTH), lambda i: (0, 0)),
            pl.BlockSpec((1, LANES), lambda i: (0, 0)),
        ],
        out_specs=[
            pl.BlockSpec((R, M_WIDTH), lambda i: (i, 0)),
            pl.BlockSpec((M_WIDTH, R), lambda i: (0, i)),
            pl.BlockSpec((A_HEADS, R, A_HEAD_DIM), lambda i: (0, i, 0)),
            pl.BlockSpec((R, A_WIDTH), lambda i: (i, 0)),
            pl.BlockSpec((A_WIDTH, R), lambda i: (0, i)),
            pl.BlockSpec((1, 1, A_WIDTH), lambda i: (i, 0, 0)),
            pl.BlockSpec((R, LANES), lambda i: (i, 0)),
            pl.BlockSpec((SUBLANES, R), lambda i: (0, i)),
        ],
        out_shape=[
            jax.ShapeDtypeStruct((S, M_WIDTH), BF16),
            jax.ShapeDtypeStruct((M_WIDTH, S), BF16),
            jax.ShapeDtypeStruct((A_HEADS, S, A_HEAD_DIM), F32),
            jax.ShapeDtypeStruct((S, A_WIDTH), BF16),
            jax.ShapeDtypeStruct((A_WIDTH, S), BF16),
            jax.ShapeDtypeStruct((nb, 1, A_WIDTH), F32),
            jax.ShapeDtypeStruct((S, LANES), F32),
            jax.ShapeDtypeStruct((SUBLANES, S), F32),
        ],
        compiler_params=_cparams(("parallel",)),
        name="prep",
    )(P, P, P, P, gate, pos, invf, conv_w, conv_b, gbias)


M_AUG = M_HEAD_DIM + LANES
MLSTM_CHUNKS_PER_STEP = 16


def _mlstm_kernel(q_ref, kT_ref, v_ref, mo_ref, gc_ref, gt_ref, gain_ref, out_ref,
                  c_ref, m_ref):
    @pl.when(pl.program_id(0) == 0)
    def _():
        c_ref[...] = jnp.zeros_like(c_ref)
        m_ref[...] = jnp.zeros_like(m_ref)

    L = M_CHUNK
    D = M_HEAD_DIM
    row = lax.broadcasted_iota(jnp.int32, (L, L), 0)
    col = lax.broadcasted_iota(jnp.int32, (L, L), 1)
    causal = row >= col
    ones_col = (lax.broadcasted_iota(jnp.int32, (L, LANES), 1) == 0).astype(BF16)

    heads = range(M_HEADS)
    hsl = [slice(h * D, (h + 1) * D) for h in heads]

    def state_stages(rs):
        q = [q_ref[rs, hsl[h]] for h in heads]
        kT = [kT_ref[hsl[h], rs] for h in heads]
        v_aug = [jnp.concatenate([v_ref[rs, hsl[h]], ones_col], axis=1) for h in heads]
        b_c = [gc_ref[rs, M_HEADS + h:M_HEADS + h + 1] for h in heads]
        b_r = [gt_ref[M_HEADS + h:M_HEADS + h + 1, rs] for h in heads]
        u_r = [gt_ref[h:h + 1, rs] - b_r[h] for h in heads]
        f_tot = [b_r[h][:, L - 1:L] for h in heads]
        m_prev = [m_ref[h:h + 1, 0:1] for h in heads]

        s_qk = [jnp.dot(q[h], kT[h], preferred_element_type=F32) for h in heads]
        q_c = [jnp.dot(q[h], c_ref[h].astype(BF16), preferred_element_type=F32) for h in heads]

        for h in heads:
            w_r = f_tot[h] + u_r[h]
            m_loc = jnp.max(w_r, axis=1, keepdims=True)
            m_new = jnp.maximum(f_tot[h] + m_prev[h], m_loc)
            a = jnp.exp(f_tot[h] + m_prev[h] - m_new)
            e_r = jnp.exp(w_r - m_new)
            keT = (kT[h].astype(F32) * e_r).astype(BF16)
            c_ref[h] = a * c_ref[h] + jnp.dot(keT, v_aug[h], preferred_element_type=F32)
            m_ref[h:h + 1, :] = jnp.broadcast_to(m_new, (1, LANES))
        return rs, v_aug, b_c, u_r, m_prev, s_qk, q_c

    def output_stage(rs, v_aug, b_c, u_r, m_prev, s_qk, q_c):
        for h in heads:
            d_log = jnp.where(causal, b_c[h] + u_r[h], -jnp.inf)
            a_log = b_c[h] + m_prev[h]
            m_t = jnp.maximum(a_log, jnp.max(d_log, axis=1, keepdims=True))
            s_ts = s_qk[h] * jnp.exp(d_log - m_t)
            inter = jnp.exp(a_log - m_t)
            r = inter * q_c[h] + jnp.dot(s_ts.astype(BF16), v_aug[h], preferred_element_type=F32)
            num = r[:, :D]
            den = r[:, D:D + 1]
            hh = num / jnp.maximum(jnp.abs(den), jnp.exp(-m_t))
            hn = hh * lax.rsqrt(jnp.mean(hh * hh, axis=-1, keepdims=True) + NORM_EPS)
            out_ref[rs, hsl[h]] = (hn * gain_ref[:, hsl[h]]
                                   * _sigmoid(mo_ref[rs, hsl[h]].astype(F32))).astype(BF16)

    def chunk(c, carry):
        output_stage(*state_stages(pl.ds(pl.multiple_of(c * L, L), L)))
        return carry

    lax.fori_loop(0, MLSTM_CHUNKS_PER_STEP, chunk, 0)


def _mlstm(mq, mkT, P, gc, gt, gain):
    S = mq.shape[0]
    L = M_CHUNK * MLSTM_CHUNKS_PER_STEP
    return pl.pallas_call(
        _mlstm_kernel,
        grid=(S // L,),
        in_specs=[
            pl.BlockSpec((L, M_WIDTH), lambda c: (c, 0)),
            pl.BlockSpec((M_WIDTH, L), lambda c: (0, c)),
            pl.BlockSpec((L, M_WIDTH), lambda c: (c, COL_MV // M_WIDTH)),
            pl.BlockSpec((L, M_WIDTH), lambda c: (c, COL_MO // M_WIDTH)),
            pl.BlockSpec((L, LANES), lambda c: (c, 0)),
            pl.BlockSpec((SUBLANES, L), lambda c: (0, c)),
            pl.BlockSpec((1, M_WIDTH), lambda c: (0, 0)),
        ],
        out_specs=pl.BlockSpec((L, M_WIDTH), lambda c: (c, 0)),
        out_shape=jax.ShapeDtypeStruct((S, M_WIDTH), BF16),
        scratch_shapes=[
            pltpu.VMEM((M_HEADS, M_HEAD_DIM, M_AUG), F32),
            pltpu.VMEM((SUBLANES, LANES), F32),
        ],
        compiler_params=_cparams(("arbitrary",)),
        name="mlstm",
    )(mq, mkT, P, P, gc, gt, gain)


MOBA_HEADS_PER_GROUP = 4
ROUTE_TILES = 264
ROUTE_ROWS = ROUTE_TILES * MOBA_BLOCK
GROUP_TILES_PER_STEP = 24
ROW_WORDS = A_HEAD_DIM // 2
SC_WINDOW = 256


def _pack_rows(o_t, lse):
    q = o_t.shape[1]
    hi = pltpu.bitcast(o_t[:ROW_WORDS].astype(BF16).astype(F32), jnp.uint32)
    lo = pltpu.bitcast(o_t[ROW_WORDS:].astype(BF16).astype(F32), jnp.uint32)
    words = pltpu.bitcast(hi | (lo >> 16), F32)
    tail = jnp.concatenate([jnp.broadcast_to(lse, (SUBLANES, q)),
                            jnp.zeros((LANES - ROW_WORDS - SUBLANES, q), F32)], axis=0)
    return jnp.concatenate([words, tail], axis=0).T


def _dot_nt(a, b):
    return lax.dot_general(a, b, (((1,), (1,)), ((), ())), preferred_element_type=F32)


def _block_partial(k_blk, v_t_blk, qs, ones_rows, causal_mask=None):
    s = _dot_nt(k_blk, qs)
    if causal_mask is not None:
        s = jnp.where(causal_mask, s, NEG_BIG)
    sb = s.astype(BF16)
    m = jnp.max(sb, axis=0, keepdims=True)
    p = jnp.exp2(sb - m)
    r = jnp.dot(jnp.concatenate([v_t_blk, ones_rows], axis=0), p, preferred_element_type=F32)
    dh = v_t_blk.shape[0]
    l = r[dh:dh + 1, :]
    return r[:dh, :] / l, m.astype(F32) + jnp.log2(l)


def _route_kernel(q_ref, k_ref, vT_ref, km_ref, route_ref, cnt_ref, own_ref, cnt_acc):
    i = pl.program_id(1)
    BS = MOBA_BLOCK
    Dh = A_HEAD_DIM
    NB = km_ref.shape[0]
    G = q_ref.shape[0]
    blk = lax.broadcasted_iota(jnp.int32, (NB, BS), 0)
    kpos = lax.broadcasted_iota(jnp.int32, (BS, BS), 0)
    qpos = lax.broadcasted_iota(jnp.int32, (BS, BS), 1)
    earlier = (kpos < qpos).astype(BF16)
    ones_rows = jnp.ones((2 * SUBLANES, BS), BF16)

    @pl.when(i == 0)
    def _():
        cnt_acc[...] = jnp.zeros_like(cnt_acc)

    q = [q_ref[g] for g in range(G)]
    qs = [(q[g] * (Dh ** -0.5 * LOG2_E)).astype(BF16) for g in range(G)]
    parts = [_block_partial(k_ref[:, g * Dh:(g + 1) * Dh], vT_ref[g * Dh:(g + 1) * Dh, :], qs[g],
                            ones_rows, kpos <= qpos) for g in range(G)]

    gates = []
    for g in range(G):
        hs = slice(g * Dh, (g + 1) * Dh)
        km = km_ref[:, hs]
        kh = km.astype(BF16)
        kl = (km - kh.astype(F32)).astype(BF16)
        qh = q[g].astype(BF16)
        ql = (q[g] - qh.astype(F32)).astype(BF16)
        gate = _dot_nt(kh, qh) + _dot_nt(kh, ql) + _dot_nt(kl, qh)
        gates.append(jnp.where(blk < i, gate, -jnp.inf))
    picks = [[] for _ in range(G)]
    rows = [[] for _ in range(G)]
    for r in range(MOBA_TOPK):
        for g in range(G):
            mx = jnp.max(gates[g], axis=0, keepdims=True)
            idx = jnp.min(jnp.where(gates[g] == mx, blk, NB), axis=0, keepdims=True)
            idx = jnp.where(r < i, idx, -1)
            pick = blk == idx
            gates[g] = jnp.where(pick, -jnp.inf, gates[g])
            picks[g].append(pick)
            rows[g].append(idx)
    for g in range(G):
        onehot = sum(p.astype(F32) for p in picks[g])
        before = cnt_acc[g][:, 0:1] + jnp.dot(onehot.astype(BF16), earlier, preferred_element_type=F32)
        for r in range(MOBA_TOPK):
            rank = jnp.sum(jnp.where(picks[g][r], before, 0.0), axis=0, keepdims=True)
            rows[g].append(rank.astype(jnp.int32))
        rows[g].append(jnp.zeros((SUBLANES - 2 * MOBA_TOPK, BS), jnp.int32))
        route_ref[g] = jnp.concatenate(rows[g], axis=0)
        cnt_new = cnt_acc[g] + jnp.sum(onehot, axis=1, keepdims=True)
        cnt_acc[g] = cnt_new
        cnt_ref[g] = cnt_new
        own_ref[g] = _pack_rows(*parts[g])


def _route(aq_hm, ak, avT, kmean, h0, H):
    _, S, Dh = aq_hm.shape
    BS = MOBA_BLOCK
    NB = S // BS
    G = MOBA_HEADS_PER_GROUP
    W = G * Dh
    hb = h0 // G
    return pl.pallas_call(
        _route_kernel,
        grid=(H // G, NB),
        in_specs=[
            pl.BlockSpec((G, BS, Dh), lambda h, i: (hb + h, i, 0)),
            pl.BlockSpec((BS, W), lambda h, i: (i, hb + h)),
            pl.BlockSpec((W, BS), lambda h, i: (hb + h, i)),
            pl.BlockSpec((NB, W), lambda h, i: (0, hb + h)),
        ],
        out_specs=[
            pl.BlockSpec((G, SUBLANES, BS), lambda h, i: (h, 0, i)),
            pl.BlockSpec((G, NB, LANES), lambda h, i: (h, 0, 0)),
            pl.BlockSpec((G, BS, LANES), lambda h, i: (h, i, 0)),
        ],
        out_shape=[
            jax.ShapeDtypeStruct((H, SUBLANES, S), jnp.int32),
            jax.ShapeDtypeStruct((H, NB, LANES), F32),
            jax.ShapeDtypeStruct((H, S, LANES), F32),
        ],
        scratch_shapes=[pltpu.VMEM((G, NB, LANES), F32)],
        compiler_params=_cparams(("parallel", "arbitrary")),
        name="moba_route",
    )(aq_hm, ak, avT, kmean)


def _dest_kernel(seg_ref, route_ref, dest_ref):
    h = pl.program_id(0)
    NB = seg_ref.shape[1]
    r = route_ref[0]
    blk = r[0:MOBA_TOPK, :]
    rank = r[MOBA_TOPK:2 * MOBA_TOPK, :]
    base = h * ROUTE_ROWS
    dest = jnp.full(blk.shape, base + ROUTE_ROWS - 1, jnp.int32)
    for j in range(NB):
        dest = jnp.where(blk == j, base + seg_ref[h, j] + rank, dest)
    dest_ref[0] = jnp.concatenate(
        [dest, jnp.zeros((SUBLANES - MOBA_TOPK, dest.shape[1]), jnp.int32)], axis=0)


def _dest(seg_start, route):
    H, _, S = route.shape
    return pl.pallas_call(
        _dest_kernel,
        grid_spec=pltpu.PrefetchScalarGridSpec(
            num_scalar_prefetch=1,
            grid=(H,),
            in_specs=[pl.BlockSpec((1, SUBLANES, S), lambda h, seg: (h, 0, 0))],
            out_specs=pl.BlockSpec((1, SUBLANES, S), lambda h, seg: (h, 0, 0)),
        ),
        out_shape=jax.ShapeDtypeStruct((H, SUBLANES, S), jnp.int32),
        compiler_params=_cparams(("parallel",)),
        name="moba_dest",
    )(seg_start, route)


def _sc_mesh():
    return plsc.VectorSubcoreMesh(core_axis_name="c", subcore_axis_name="s")


def _sc_scatter(rows, row0, n_rows, idx, n_out):
    W = rows.shape[1]
    M = idx.shape[0]
    n_win = n_rows // SC_WINDOW
    win0 = row0 // SC_WINDOW

    reps = M // n_rows

    @pl.kernel(out_type=jax.ShapeDtypeStruct((n_out, W), rows.dtype), mesh=_sc_mesh())
    def k(x_hbm, i_hbm, o_hbm):
        def body(x_vmem, *i_vmems):
            for i_vmem in i_vmems:
                pltpu.sync_copy(x_vmem, o_hbm.at[i_vmem.at[0]])

        pltpu.emit_pipeline(
            body,
            grid=(n_win,),
            in_specs=([pl.BlockSpec((SC_WINDOW, W), lambda w: (win0 + w, 0))]
                      + [pl.BlockSpec((1, SC_WINDOW), lambda w, r=r: (0, r * n_win + w)) for r in range(reps)]),
            out_specs=[],
            core_axis_name=("c", "s"),
            dimension_semantics=(pltpu.PARALLEL,),
        )(x_hbm, *([i_hbm] * reps))

    return k(rows, idx.reshape(1, M))


def _sc_gather(table, idx):
    M = idx.shape[0]
    W = table.shape[1]

    @pl.kernel(out_type=jax.ShapeDtypeStruct((M, W), table.dtype), mesh=_sc_mesh())
    def k(x_hbm, i_hbm, o_hbm):
        def body(i_vmem, o_vmem):
            pltpu.sync_copy(x_hbm.at[i_vmem.at[0]], o_vmem)

        pltpu.emit_pipeline(
            body,
            grid=(M // SC_WINDOW,),
            in_specs=[pl.BlockSpec((1, SC_WINDOW), lambda w: (0, w))],
            out_specs=[pl.BlockSpec((SC_WINDOW, W), lambda w: (w, 0))],
            core_axis_name=("c", "s"),
            dimension_semantics=(pltpu.PARALLEL,),
        )(i_hbm, o_hbm)

    return k(table, idx.reshape(1, M))


def _group_kernel(tile_blk_ref, q_ref, k_ref, vT_ref, out_ref, s_ref, m_ref):
    h = pl.program_id(0)
    u = pl.program_id(1)
    last = pl.num_programs(1) - 2
    BS = MOBA_BLOCK
    Dh = A_HEAD_DIM
    T = GROUP_TILES_PER_STEP
    ones_rows = jnp.ones((2 * SUBLANES, BS), BF16)
    new = lax.rem(u, 2)
    old = 1 - new

    @pl.when((h == 0) & (u == 0))
    def _():
        s_ref[...] = jnp.zeros_like(s_ref)
        m_ref[...] = jnp.zeros_like(m_ref)

    def block_start(group, c):
        j = jnp.maximum(tile_blk_ref[h * ROUTE_TILES + group * T + c], 0)
        return pl.multiple_of(j * BS, BS)

    g_new = jnp.minimum(u, last)
    g_old = jnp.maximum(u - 1, 0)
    @pl.when(tile_blk_ref[h * ROUTE_TILES + g_old * T] >= 0)
    def _():
        ms, rs = [], []
        for c in range(T):
            m = m_ref[old, c]
            p = jnp.exp2(s_ref[old, c] - m.astype(BF16))
            v_aug = jnp.concatenate([vT_ref[:, pl.ds(block_start(g_old, c), BS)], ones_rows], axis=0)
            ms.append(m)
            rs.append(jnp.dot(v_aug, p, preferred_element_type=F32))
        s_new = []
        for c in range(T):
            qs = (q_ref[c * BS:(c + 1) * BS, :] * (Dh ** -0.5 * LOG2_E)).astype(BF16)
            s_new.append(_dot_nt(k_ref[pl.ds(block_start(g_new, c), BS), :], qs))
        for c in range(T):
            l = rs[c][Dh:Dh + 1, :]
            out_ref[c * BS:(c + 1) * BS, :] = _pack_rows(rs[c][:Dh, :] / l, ms[c] + jnp.log2(l))
            sb = s_new[c].astype(BF16)
            s_ref[new, c] = sb
            m_ref[new, c] = jnp.max(sb, axis=0, keepdims=True).astype(F32)


def _group(tile_blk, q_sorted, ak, avT, h0):
    S = ak.shape[0]
    H = q_sorted.shape[0] // ROUTE_ROWS
    Dh = A_HEAD_DIM
    BS = MOBA_BLOCK
    T = GROUP_TILES_PER_STEP
    steps = ROUTE_TILES // T
    rows = T * BS
    return pl.pallas_call(
        _group_kernel,
        grid_spec=pltpu.PrefetchScalarGridSpec(
            num_scalar_prefetch=1,
            grid=(H, steps + 1),
            in_specs=[
                pl.BlockSpec((rows, LANES), lambda h, u, tb: (h * steps + jnp.minimum(u, steps - 1), 0)),
                pl.BlockSpec((S, Dh), lambda h, u, tb: (0, h0 + h)),
                pl.BlockSpec((Dh, S), lambda h, u, tb: (h0 + h, 0)),
            ],
            out_specs=pl.BlockSpec((rows, LANES), lambda h, u, tb: (h * steps + jnp.maximum(u - 1, 0), 0)),
            scratch_shapes=[pltpu.VMEM((2, T, BS, BS), BF16), pltpu.VMEM((2, T, 1, BS), F32)],
        ),
        out_shape=jax.ShapeDtypeStruct((H * ROUTE_ROWS, LANES), F32),
        compiler_params=_cparams(("arbitrary", "arbitrary")),
        name="moba_group",
    )(tile_blk, q_sorted, ak, avT)


def _combine_kernel(*refs):
    i = pl.program_id(0)
    o_ref = refs[-1]
    n_groups = (len(refs) - 1) // 2
    H = refs[n_groups].shape[0]
    BS = refs[n_groups].shape[1]
    low_half = lax.broadcasted_iota(jnp.int32, (BS, LANES), 1) < ROW_WORDS
    for head in range(n_groups * H):
        got_ref, own_ref, h = refs[head // H], refs[n_groups + head // H], head % H
        tiles = [own_ref[h]]
        for r in range(MOBA_TOPK):
            tiles.append(jnp.where(r < i, got_ref[h, r], 0.0))
        lses = [tiles[0][:, ROW_WORDS:ROW_WORDS + 1]]
        lses += [jnp.where(r < i, tiles[r + 1][:, ROW_WORDS:ROW_WORDS + 1], NEG_BIG)
                 for r in range(MOBA_TOPK)]
        top = functools.reduce(jnp.maximum, lses)
        w = [jnp.exp2(x - top) for x in lses]
        inv = 1.0 / sum(w)
        first = jnp.zeros((BS, LANES), F32)
        second = jnp.zeros((BS, LANES), F32)
        for wk, tile in zip(w, tiles):
            words = pltpu.bitcast(tile, jnp.uint32)
            first = first + wk * pltpu.bitcast(words & jnp.uint32(0xFFFF0000), F32)
            second = second + wk * pltpu.bitcast(words << 16, F32)
        o = jnp.where(low_half, first, pltpu.roll(second, ROW_WORDS, 1)) * inv
        o_ref[:, head * A_HEAD_DIM:(head + 1) * A_HEAD_DIM] = o.astype(BF16)


def _combine(gots, owns):
    H, S, _ = owns[0].shape
    BS = MOBA_BLOCK
    return pl.pallas_call(
        _combine_kernel,
        grid=(S // BS,),
        in_specs=([pl.BlockSpec((H, MOBA_TOPK, BS, LANES), lambda i: (0, 0, i, 0))] * len(gots)
                  + [pl.BlockSpec((H, BS, LANES), lambda i: (0, i, 0))] * len(owns)),
        out_specs=pl.BlockSpec((BS, A_WIDTH), lambda i: (i, 0)),
        out_shape=jax.ShapeDtypeStruct((S, A_WIDTH), BF16),
        compiler_params=_cparams(("parallel",)),
        name="moba_combine",
    )(*gots, *owns)


def _moba_routed(aq_hm, ak, avT, kmean):
    H, S, Dh = aq_hm.shape
    NB = S // MOBA_BLOCK
    hn = MOBA_HEADS_PER_GROUP
    q_rows = aq_hm.reshape(H * S, Dh)
    gots, owns = [], []
    for h0 in range(0, H, hn):
        route, cnt, own = _route(aq_hm, ak, avT, kmean, h0, hn)
        cnt = cnt[:, :, 0].astype(jnp.int32)
        seg_tiles = (cnt + MOBA_BLOCK - 1) // MOBA_BLOCK
        seg_end = jnp.cumsum(seg_tiles, axis=1)
        seg_start = (seg_end - seg_tiles) * MOBA_BLOCK
        tile_ids = jnp.arange(ROUTE_TILES, dtype=jnp.int32)
        tile_blk = jnp.sum(tile_ids[None, :, None] >= seg_end[:, None, :], axis=2).astype(jnp.int32)
        tile_blk = jnp.where(tile_blk < NB, tile_blk, -1).reshape(hn * ROUTE_TILES)
        dest = _dest(seg_start, route)[:, :MOBA_TOPK, :]
        q_sorted = _sc_scatter(q_rows, h0 * S, hn * S, dest.transpose(1, 0, 2).reshape(-1),
                               hn * ROUTE_ROWS)
        results = _group(tile_blk, q_sorted, ak, avT, h0)
        gots.append(_sc_gather(results, dest.reshape(-1)).reshape(hn, MOBA_TOPK, S, LANES))
        owns.append(own)
    return _combine(gots, owns)


def _merge_kernel(hm_ref, ha_ref, gm_ref, ga_ref, x_ref, wm_ref, wa_ref, wo_ref, gain_ref,
                  out_ref):
    ym = jnp.dot(hm_ref[...], wm_ref[...], preferred_element_type=F32)
    ya = jnp.dot(ha_ref[...], wa_ref[...], preferred_element_type=F32)
    merged = _sigmoid(gm_ref[...].astype(F32)) * ym + _sigmoid(ga_ref[...].astype(F32)) * ya
    mix = jnp.dot(merged.astype(BF16), wo_ref[...], preferred_element_type=F32)
    out_ref[...] = x_ref[...] + _rms(mix, gain_ref[...])


def _merge(hm, ha, P, x, b, wm, wa, wo, gain, tm=256):
    _, S, D = x.shape
    const = pl.Buffered(1)
    return pl.pallas_call(
        _merge_kernel,
        grid=(S // tm,),
        in_specs=[
            pl.BlockSpec((tm, M_WIDTH), lambda i: (i, 0)),
            pl.BlockSpec((tm, A_WIDTH), lambda i: (i, 0)),
            pl.BlockSpec((tm, D), lambda i: (i, COL_GM // D)),
            pl.BlockSpec((tm, D), lambda i: (i, COL_GA // D)),
            pl.BlockSpec((None, tm, D), lambda i: (b, i, 0)),
            pl.BlockSpec((M_WIDTH, D), lambda i: (0, 0), pipeline_mode=const),
            pl.BlockSpec((A_WIDTH, D), lambda i: (0, 0), pipeline_mode=const),
            pl.BlockSpec((D, D), lambda i: (0, 0), pipeline_mode=const),
            pl.BlockSpec((1, D), lambda i: (0, 0)),
        ],
        out_specs=pl.BlockSpec((tm, D), lambda i: (i, 0)),
        out_shape=jax.ShapeDtypeStruct((S, D), F32),
        compiler_params=_cparams(("parallel",)),
        name="merge",
    )(hm, ha, P, P, x, wm, wa, wo, gain)


def _ffn_kernel(x_ref, gpre_ref, wu_ref, wd_ref, gpost_ref, out_ref, hn_ref, acc_ref):
    f = pl.program_id(1)

    @pl.when(f == 0)
    def _():
        hn_ref[...] = _rms(x_ref[...], gpre_ref[...]).astype(BF16)
        acc_ref[...] = jnp.zeros_like(acc_ref)

    u = jnp.dot(hn_ref[...], wu_ref[...], preferred_element_type=F32)
    u = jnp.square(jnp.maximum(u, 0.0)).astype(BF16)
    acc_ref[...] += jnp.dot(u, wd_ref[...], preferred_element_type=F32)

    @pl.when(f == pl.num_programs(1) - 1)
    def _():
        out_ref[...] = x_ref[...] + _rms(acc_ref[...], gpost_ref[...])


def _ffn(x, gpre, wu, wd, gpost, tm=512, tf=1024):
    S, D = x.shape
    Fd = wu.shape[1]
    return pl.pallas_call(
        _ffn_kernel,
        grid=(S // tm, Fd // tf),
        in_specs=[
            pl.BlockSpec((tm, D), lambda i, f: (i, 0)),
            pl.BlockSpec((1, D), lambda i, f: (0, 0)),
            pl.BlockSpec((D, tf), lambda i, f: (0, f)),
            pl.BlockSpec((tf, D), lambda i, f: (f, 0)),
            pl.BlockSpec((1, D), lambda i, f: (0, 0)),
        ],
        out_specs=pl.BlockSpec((tm, D), lambda i, f: (i, 0)),
        out_shape=jax.ShapeDtypeStruct((S, D), F32),
        scratch_shapes=[pltpu.VMEM((tm, D), BF16), pltpu.VMEM((tm, D), F32)],
        compiler_params=_cparams(("parallel", "arbitrary")),
        name="ffn",
    )(x, gpre, wu, wd, gpost)


def _layer(x, b, pos, norm_mix_pre, w_in, conv_w, conv_b, i_bias, f_bias, mlstm_norm,
           w_branch_m, w_branch_a, w_out, norm_mix_post, norm_ffn_pre, w_up, w_down,
           norm_ffn_post):
    _, S, D = x.shape
    o = 0
    pieces = {}
    w_in = w_in.astype(BF16)
    for name, width in (("mq", M_WIDTH), ("mk", M_WIDTH), ("mv", M_WIDTH), ("mo", M_WIDTH),
                        ("mi", M_HEADS), ("mf", M_HEADS), ("aq", A_WIDTH), ("ak", A_WIDTH),
                        ("av", A_WIDTH), ("gm", D), ("ga", D)):
        pieces[name] = w_in[:, o:o + width]
        o += width
    w_all = jnp.concatenate([pieces[n] for n in ("gm", "ga", "mq", "mk", "aq", "ak", "mv", "mo", "av")],
                            axis=1)
    w_gate = jnp.concatenate(
        [pieces["mi"], pieces["mf"], jnp.zeros((D, LANES - 2 * M_HEADS), BF16)], axis=1)
    gbias = jnp.concatenate([i_bias, f_bias, jnp.zeros((LANES - 2 * M_HEADS,), F32)])[None, :]
    half = jnp.arange(0, A_HEAD_DIM, 2, dtype=F32) / A_HEAD_DIM
    inv_freq = 1.0 / (ROPE_THETA ** half)
    invf = jnp.concatenate([inv_freq, inv_freq])[None, :]

    P, gate = _proj(x, b, norm_mix_pre[None, :], w_all, w_gate)
    mq, mkT, aq, ak, avT, kmean, gc, gt = _prep(
        P, gate, pos.reshape(S, 1), invf, conv_w, conv_b[None, :], gbias)
    hm = _mlstm(mq, mkT, P, gc, gt, mlstm_norm[None, :])
    ha = _moba_routed(aq, ak, avT, kmean.reshape(S // MOBA_BLOCK, A_WIDTH))
    x1 = _merge(hm, ha, P, x, b, w_branch_m.astype(BF16), w_branch_a.astype(BF16),
                w_out.astype(BF16), norm_mix_post[None, :])
    return _ffn(x1, norm_ffn_pre[None, :], w_up.astype(BF16), w_down.astype(BF16),
                norm_ffn_post[None, :])


def kernel(x, positions, norm_mix_pre, w_in, conv_w, conv_b, i_bias, f_bias, mlstm_norm,
           w_branch_m, w_branch_a, w_out, norm_mix_post, norm_ffn_pre, w_up, w_down,
           norm_ffn_post):
    B = x.shape[0]
    depth = w_in.shape[0]
    outs = []
    def take(a, i):
        return a.reshape(a.shape[1:]) if a.shape[0] == 1 else a[i]

    for b in range(B):
        xin, bi = x, b
        for l in range(depth):
            xb = _layer(xin, bi, take(positions, b), take(norm_mix_pre, l), take(w_in, l),
                        take(conv_w, l), take(conv_b, l), take(i_bias, l), take(f_bias, l),
                        take(mlstm_norm, l), take(w_branch_m, l), take(w_branch_a, l),
                        take(w_out, l), take(norm_mix_post, l), take(norm_ffn_pre, l),
                        take(w_up, l), take(w_down, l), take(norm_ffn_post, l))
            xin, bi = xb[None], 0
        outs.append(xb)
    return outs[0][None] if B == 1 else jnp.stack(outs, axis=0)
```

```python
import functools

import jax
import jax.numpy as jnp
from jax import lax
from jax.experimental import pallas as pl
from jax.experimental.pallas import tpu as pltpu
from jax.experimental.pallas import tpu_sc as plsc

F32 = jnp.float32
BF16 = jnp.bfloat16

M_HEADS = 4
M_HEAD_DIM = 256
M_WIDTH = M_HEADS * M_HEAD_DIM
M_CHUNK = 128
CONV_WIDTH = 4
A_HEADS = 8
A_HEAD_DIM = 128
A_WIDTH = A_HEADS * A_HEAD_DIM
MOBA_BLOCK = 256
MOBA_TOPK = 3
ROPE_THETA = 10000.0
NORM_EPS = 1e-6

LANES = 128
SUBLANES = 8
VMEM_LIMIT = 56 * 1024 * 1024
NEG_BIG = -1e30
LOG2_E = 1.4426950408889634

COL_GM = 0
COL_GA = 2048
COL_MQK = 4096
COL_AQK = 6144
COL_MV = 8192
COL_MO = 9216
COL_AV = 10240


def _cparams(sem):
    return pltpu.CompilerParams(dimension_semantics=sem, vmem_limit_bytes=VMEM_LIMIT)


def _rms(x, gain):
    ms = jnp.mean(x * x, axis=-1, keepdims=True)
    return x * lax.rsqrt(ms + NORM_EPS) * gain


def _sigmoid(x):
    return 1.0 / (1.0 + jnp.exp(-x))


def _split3(x):
    hi = x.astype(BF16)
    r1 = x - hi.astype(F32)
    mid = r1.astype(BF16)
    lo = (r1 - mid.astype(F32)).astype(BF16)
    return hi, mid, lo


def _proj_kernel(x_ref, g_ref, w_ref, wg_ref, p_ref, gate_ref, xn_ref):
    @pl.when(pl.program_id(1) == 0)
    def _():
        xn = _rms(x_ref[...], g_ref[...]).astype(BF16)
        xn_ref[...] = xn
        gate_ref[...] = jnp.dot(xn, wg_ref[...], preferred_element_type=F32)

    p_ref[...] = jnp.dot(xn_ref[...], w_ref[...], preferred_element_type=F32).astype(BF16)


def _proj(x, b, gain, w_all, w_gate, tm=1024, tn=1024):
    _, S, D = x.shape
    N = w_all.shape[1]
    return pl.pallas_call(
        _proj_kernel,
        grid=(S // tm, N // tn),
        in_specs=[
            pl.BlockSpec((None, tm, D), lambda i, j: (b, i, 0)),
            pl.BlockSpec((1, D), lambda i, j: (0, 0)),
            pl.BlockSpec((D, tn), lambda i, j: (0, j)),
            pl.BlockSpec((D, LANES), lambda i, j: (0, 0)),
        ],
        out_specs=[
            pl.BlockSpec((tm, tn), lambda i, j: (i, j)),
            pl.BlockSpec((tm, LANES), lambda i, j: (i, 0)),
        ],
        out_shape=[
            jax.ShapeDtypeStruct((S, N), BF16),
            jax.ShapeDtypeStruct((S, LANES), F32),
        ],
        scratch_shapes=[pltpu.VMEM((tm, D), BF16)],
        compiler_params=_cparams(("parallel", "arbitrary")),
        name="proj",
    )(x, gain, w_all, w_gate)


PREP_ROWS = MOBA_BLOCK
PREP_COLS = 512
HALO_ROWS = 2 * SUBLANES


def _prep_kernel(pmk_ref, halo_ref, paqk_ref, pav_ref, gate_ref, pos_ref, invf_ref,
                 cw_ref, cb_ref, gb_ref,
                 mq_ref, mkT_ref, aq_ref, ak_ref, avT_ref, kmean_ref, gc_ref, gt_ref):
    i = pl.program_id(0)
    R = PREP_ROWS

    k_scale = M_HEAD_DIM ** -0.5
    for c0 in range(0, 2 * M_WIDTH, PREP_COLS):
        cs = slice(c0, c0 + PREP_COLS)
        prev = halo_ref[:, cs].astype(F32)[HALO_ROWS - SUBLANES:, :]
        prev = jnp.where(i == 0, jnp.zeros_like(prev), prev)
        ext = jnp.concatenate([pmk_ref[:, cs].astype(F32), prev], axis=0)
        acc = cw_ref[0:1, cs] * ext
        for j in range(1, CONV_WIDTH):
            acc = pltpu.roll(acc, 1, 0) + cw_ref[j:j + 1, cs] * ext
        acc = acc[0:R, :] + cb_ref[:, cs]
        y = acc * _sigmoid(acc)
        if c0 < M_WIDTH:
            mq_ref[:, cs] = y.astype(BF16)
        else:
            ks = slice(c0 - M_WIDTH, c0 - M_WIDTH + PREP_COLS)
            mkT_ref[ks, :] = (y * k_scale).T.astype(BF16)

    ang = pos_ref[...].astype(F32) * invf_ref[...]
    cos = jnp.cos(ang)
    lane = lax.broadcasted_iota(jnp.int32, (R, A_HEAD_DIM), 1)
    sin_signed = jnp.where(lane < A_HEAD_DIM // 2, -1.0, 1.0) * jnp.sin(ang)
    for h in range(2 * A_HEADS):
        hs = slice(h * A_HEAD_DIM, (h + 1) * A_HEAD_DIM)
        xh = paqk_ref[:, hs].astype(F32)
        yh = xh * cos + pltpu.roll(xh, A_HEAD_DIM // 2, 1) * sin_signed
        if h < A_HEADS:
            aq_ref[h] = yh
        else:
            ko = slice((h - A_HEADS) * A_HEAD_DIM, (h - A_HEADS + 1) * A_HEAD_DIM)
            ak_ref[:, ko] = yh.astype(BF16)
            kmean_ref[0, :, ko] = jnp.mean(yh, axis=0, keepdims=True)
    for c0 in range(0, A_WIDTH, PREP_COLS):
        avT_ref[c0:c0 + PREP_COLS, :] = pav_ref[:, c0:c0 + PREP_COLS].astype(F32).T.astype(BF16)

    g = gate_ref[...] + gb_ref[...]
    log_f = jnp.minimum(g, 0.0) - jnp.log1p(jnp.exp(-jnp.abs(g)))
    r_i = lax.broadcasted_iota(jnp.int32, (R, R), 0)
    c_i = lax.broadcasted_iota(jnp.int32, (R, R), 1)
    tri = ((r_i >= c_i) & ((r_i // M_CHUNK) == (c_i // M_CHUNK))).astype(BF16)
    hi, mid, lo = _split3(log_f)
    csum = (jnp.dot(tri, hi, preferred_element_type=F32)
            + jnp.dot(tri, mid, preferred_element_type=F32)
            + jnp.dot(tri, lo, preferred_element_type=F32))
    glane = lax.broadcasted_iota(jnp.int32, (R, LANES), 1)
    gc = jnp.where(glane < M_HEADS, g, csum)
    gc_ref[...] = gc
    gt_ref[...] = gc.T[0:SUBLANES, :]


def _prep(P, gate, pos, invf, conv_w, conv_b, gbias):
    S = P.shape[0]
    R = PREP_ROWS
    nb = S // R
    halo_blocks = R // HALO_ROWS
    return pl.pallas_call(
        _prep_kernel,
        grid=(nb,),
        in_specs=[
            pl.BlockSpec((R, 2 * M_WIDTH), lambda i: (i, COL_MQK // (2 * M_WIDTH))),
            pl.BlockSpec((HALO_ROWS, 2 * M_WIDTH),
                         lambda i: (jnp.maximum(i * halo_blocks - 1, 0), COL_MQK // (2 * M_WIDTH))),
            pl.BlockSpec((R, 2 * A_WIDTH), lambda i: (i, COL_AQK // (2 * A_WIDTH))),
            pl.BlockSpec((R, A_WIDTH), lambda i: (i, COL_AV // A_WIDTH)),
            pl.BlockSpec((R, LANES), lambda i: (i, 0)),
            pl.BlockSpec((R, 1), lambda i: (i, 0)),
            pl.BlockSpec((1, A_HEAD_DIM), lambda i: (0, 0)),
            pl.BlockSpec((CONV_WIDTH, 2 * M_WIDTH), lambda i: (0, 0)),
            pl.BlockSpec((1, 2 * M_WIDTH), lambda i: (0, 0)),
            pl.BlockSpec((1, LANES), lambda i: (0, 0)),
        ],
        out_specs=[
            pl.BlockSpec((R, M_WIDTH), lambda i: (i, 0)),
            pl.BlockSpec((M_WIDTH, R), lambda i: (0, i)),
            pl.BlockSpec((A_HEADS, R, A_HEAD_DIM), lambda i: (0, i, 0)),
            pl.BlockSpec((R, A_WIDTH), lambda i: (i, 0)),
            pl.BlockSpec((A_WIDTH, R), lambda i: (0, i)),
            pl.BlockSpec((1, 1, A_WIDTH), lambda i: (i, 0, 0)),
            pl.BlockSpec((R, LANES), lambda i: (i, 0)),
            pl.BlockSpec((SUBLANES, R), lambda i: (0, i)),
        ],
        out_shape=[
            jax.ShapeDtypeStruct((S, M_WIDTH), BF16),
            jax.ShapeDtypeStruct((M_WIDTH, S), BF16),
            jax.ShapeDtypeStruct((A_HEADS, S, A_HEAD_DIM), F32),
            jax.ShapeDtypeStruct((S, A_WIDTH), BF16),
            jax.ShapeDtypeStruct((A_WIDTH, S), BF16),
            jax.ShapeDtypeStruct((nb, 1, A_WIDTH), F32),
            jax.ShapeDtypeStruct((S, LANES), F32),
            jax.ShapeDtypeStruct((SUBLANES, S), F32),
        ],
        compiler_params=_cparams(("parallel",)),
        name="prep",
    )(P, P, P, P, gate, pos, invf, conv_w, conv_b, gbias)


M_AUG = M_HEAD_DIM + LANES
MLSTM_CHUNKS_PER_STEP = 16


def _mlstm_kernel(q_ref, kT_ref, v_ref, mo_ref, gc_ref, gt_ref, gain_ref, out_ref,
                  c_ref, m_ref):
    @pl.when(pl.program_id(0) == 0)
    def _():
        c_ref[...] = jnp.zeros_like(c_ref)
        m_ref[...] = jnp.zeros_like(m_ref)

    L = M_CHUNK
    D = M_HEAD_DIM
    row = lax.broadcasted_iota(jnp.int32, (L, L), 0)
    col = lax.broadcasted_iota(jnp.int32, (L, L), 1)
    causal = row >= col
    ones_col = (lax.broadcasted_iota(jnp.int32, (L, LANES), 1) == 0).astype(BF16)

    heads = range(M_HEADS)
    hsl = [slice(h * D, (h + 1) * D) for h in heads]

    def state_stages(rs):
        q = [q_ref[rs, hsl[h]] for h in heads]
        kT = [kT_ref[hsl[h], rs] for h in heads]
        v_aug = [jnp.concatenate([v_ref[rs, hsl[h]], ones_col], axis=1) for h in heads]
        b_c = [gc_ref[rs, M_HEADS + h:M_HEADS + h + 1] for h in heads]
        b_r = [gt_ref[M_HEADS + h:M_HEADS + h + 1, rs] for h in heads]
        u_r = [gt_ref[h:h + 1, rs] - b_r[h] for h in heads]
        f_tot = [b_r[h][:, L - 1:L] for h in heads]
        m_prev = [m_ref[h:h + 1, 0:1] for h in heads]

        s_qk = [jnp.dot(q[h], kT[h], preferred_element_type=F32) for h in heads]
        q_c = [jnp.dot(q[h], c_ref[h].astype(BF16), preferred_element_type=F32) for h in heads]

        for h in heads:
            w_r = f_tot[h] + u_r[h]
            m_loc = jnp.max(w_r, axis=1, keepdims=True)
            m_new = jnp.maximum(f_tot[h] + m_prev[h], m_loc)
            a = jnp.exp(f_tot[h] + m_prev[h] - m_new)
            e_r = jnp.exp(w_r - m_new)
            keT = (kT[h].astype(F32) * e_r).astype(BF16)
            c_ref[h] = a * c_ref[h] + jnp.dot(keT, v_aug[h], preferred_element_type=F32)
            m_ref[h:h + 1, :] = jnp.broadcast_to(m_new, (1, LANES))
        return rs, v_aug, b_c, u_r, m_prev, s_qk, q_c

    def output_stage(rs, v_aug, b_c, u_r, m_prev, s_qk, q_c):
        for h in heads:
            d_log = jnp.where(causal, b_c[h] + u_r[h], -jnp.inf)
            a_log = b_c[h] + m_prev[h]
            m_t = jnp.maximum(a_log, jnp.max(d_log, axis=1, keepdims=True))
            s_ts = s_qk[h] * jnp.exp(d_log - m_t)
            inter = jnp.exp(a_log - m_t)
            r = inter * q_c[h] + jnp.dot(s_ts.astype(BF16), v_aug[h], preferred_element_type=F32)
            num = r[:, :D]
            den = r[:, D:D + 1]
            hh = num / jnp.maximum(jnp.abs(den), jnp.exp(-m_t))
            hn = hh * lax.rsqrt(jnp.mean(hh * hh, axis=-1, keepdims=True) + NORM_EPS)
            out_ref[rs, hsl[h]] = (hn * gain_ref[:, hsl[h]]
                                   * _sigmoid(mo_ref[rs, hsl[h]].astype(F32))).astype(BF16)

    def chunk(c, carry):
        output_stage(*state_stages(pl.ds(pl.multiple_of(c * L, L), L)))
        return carry

    lax.fori_loop(0, MLSTM_CHUNKS_PER_STEP, chunk, 0)


def _mlstm(mq, mkT, P, gc, gt, gain):
    S = mq.shape[0]
    L = M_CHUNK * MLSTM_CHUNKS_PER_STEP
    return pl.pallas_call(
        _mlstm_kernel,
        grid=(S // L,),
        in_specs=[
            pl.BlockSpec((L, M_WIDTH), lambda c: (c, 0)),
            pl.BlockSpec((M_WIDTH, L), lambda c: (0, c)),
            pl.BlockSpec((L, M_WIDTH), lambda c: (c, COL_MV // M_WIDTH)),
            pl.BlockSpec((L, M_WIDTH), lambda c: (c, COL_MO // M_WIDTH)),
            pl.BlockSpec((L, LANES), lambda c: (c, 0)),
            pl.BlockSpec((SUBLANES, L), lambda c: (0, c)),
            pl.BlockSpec((1, M_WIDTH), lambda c: (0, 0)),
        ],
        out_specs=pl.BlockSpec((L, M_WIDTH), lambda c: (c, 0)),
        out_shape=jax.ShapeDtypeStruct((S, M_WIDTH), BF16),
        scratch_shapes=[
            pltpu.VMEM((M_HEADS, M_HEAD_DIM, M_AUG), F32),
            pltpu.VMEM((SUBLANES, LANES), F32),
        ],
        compiler_params=_cparams(("arbitrary",)),
        name="mlstm",
    )(mq, mkT, P, P, gc, gt, gain)


MOBA_HEADS_PER_GROUP = 4
ROUTE_TILES = 264
ROUTE_ROWS = ROUTE_TILES * MOBA_BLOCK
GROUP_TILES_PER_STEP = 24
ROW_WORDS = A_HEAD_DIM // 2
SC_WINDOW = 256


def _pack_rows(o_t, lse):
    q = o_t.shape[1]
    hi = pltpu.bitcast(o_t[:ROW_WORDS].astype(BF16).astype(F32), jnp.uint32)
    lo = pltpu.bitcast(o_t[ROW_WORDS:].astype(BF16).astype(F32), jnp.uint32)
    words = pltpu.bitcast(hi | (lo >> 16), F32)
    tail = jnp.concatenate([jnp.broadcast_to(lse, (SUBLANES, q)),
                            jnp.zeros((LANES - ROW_WORDS - SUBLANES, q), F32)], axis=0)
    return jnp.concatenate([words, tail], axis=0).T


def _dot_nt(a, b):
    return lax.dot_general(a, b, (((1,), (1,)), ((), ())), preferred_element_type=F32)


def _block_partial(k_blk, v_t_blk, qs, ones_rows, causal_mask=None):
    s = _dot_nt(k_blk, qs)
    if causal_mask is not None:
        s = jnp.where(causal_mask, s, NEG_BIG)
    sb = s.astype(BF16)
    m = jnp.max(sb, axis=0, keepdims=True)
    p = jnp.exp2(sb - m)
    r = jnp.dot(jnp.concatenate([v_t_blk, ones_rows], axis=0), p, preferred_element_type=F32)
    dh = v_t_blk.shape[0]
    l = r[dh:dh + 1, :]
    return r[:dh, :] / l, m.astype(F32) + jnp.log2(l)


def _route_kernel(q_ref, k_ref, vT_ref, km_ref, route_ref, cnt_ref, own_ref, cnt_acc):
    i = pl.program_id(1)
    BS = MOBA_BLOCK
    Dh = A_HEAD_DIM
    NB = km_ref.shape[0]
    G = q_ref.shape[0]
    blk = lax.broadcasted_iota(jnp.int32, (NB, BS), 0)
    kpos = lax.broadcasted_iota(jnp.int32, (BS, BS), 0)
    qpos = lax.broadcasted_iota(jnp.int32, (BS, BS), 1)
    earlier = (kpos < qpos).astype(BF16)
    ones_rows = jnp.ones((2 * SUBLANES, BS), BF16)

    @pl.when(i == 0)
    def _():
        cnt_acc[...] = jnp.zeros_like(cnt_acc)

    q = [q_ref[g] for g in range(G)]
    qs = [(q[g] * (Dh ** -0.5 * LOG2_E)).astype(BF16) for g in range(G)]
    parts = [_block_partial(k_ref[:, g * Dh:(g + 1) * Dh], vT_ref[g * Dh:(g + 1) * Dh, :], qs[g],
                            ones_rows, kpos <= qpos) for g in range(G)]

    gates = []
    for g in range(G):
        hs = slice(g * Dh, (g + 1) * Dh)
        km = km_ref[:, hs]
        kh = km.astype(BF16)
        kl = (km - kh.astype(F32)).astype(BF16)
        qh = q[g].astype(BF16)
        ql = (q[g] - qh.astype(F32)).astype(BF16)
        gate = _dot_nt(kh, qh) + _dot_nt(kh, ql) + _dot_nt(kl, qh)
        gates.append(jnp.where(blk < i, gate, -jnp.inf))
    picks = [[] for _ in range(G)]
    rows = [[] for _ in range(G)]
    for r in range(MOBA_TOPK):
        for g in range(G):
            mx = jnp.max(gates[g], axis=0, keepdims=True)
            idx = jnp.min(jnp.where(gates[g] == mx, blk, NB), axis=0, keepdims=True)
            idx = jnp.where(r < i, idx, -1)
            pick = blk == idx
            gates[g] = jnp.where(pick, -jnp.inf, gates[g])
            picks[g].append(pick)
            rows[g].append(idx)
    for g in range(G):
        onehot = sum(p.astype(F32) for p in picks[g])
        before = cnt_acc[g][:, 0:1] + jnp.dot(onehot.astype(BF16), earlier, preferred_element_type=F32)
        for r in range(MOBA_TOPK):
            rank = jnp.sum(jnp.where(picks[g][r], before, 0.0), axis=0, keepdims=True)
            rows[g].append(rank.astype(jnp.int32))
        rows[g].append(jnp.zeros((SUBLANES - 2 * MOBA_TOPK, BS), jnp.int32))
        route_ref[g] = jnp.concatenate(rows[g], axis=0)
        cnt_new = cnt_acc[g] + jnp.sum(onehot, axis=1, keepdims=True)
        cnt_acc[g] = cnt_new
        cnt_ref[g] = cnt_new
        own_ref[g] = _pack_rows(*parts[g])


def _route(aq_hm, ak, avT, kmean, h0, H):
    _, S, Dh = aq_hm.shape
    BS = MOBA_BLOCK
    NB = S // BS
    G = MOBA_HEADS_PER_GROUP
    W = G * Dh
    hb = h0 // G
    return pl.pallas_call(
        _route_kernel,
        grid=(H // G, NB),
        in_specs=[
            pl.BlockSpec((G, BS, Dh), lambda h, i: (hb + h, i, 0)),
            pl.BlockSpec((BS, W), lambda h, i: (i, hb + h)),
            pl.BlockSpec((W, BS), lambda h, i: (hb + h, i)),
            pl.BlockSpec((NB, W), lambda h, i: (0, hb + h)),
        ],
        out_specs=[
            pl.BlockSpec((G, SUBLANES, BS), lambda h, i: (h, 0, i)),
            pl.BlockSpec((G, NB, LANES), lambda h, i: (h, 0, 0)),
            pl.BlockSpec((G, BS, LANES), lambda h, i: (h, i, 0)),
        ],
        out_shape=[
            jax.ShapeDtypeStruct((H, SUBLANES, S), jnp.int32),
            jax.ShapeDtypeStruct((H, NB, LANES), F32),
            jax.ShapeDtypeStruct((H, S, LANES), F32),
        ],
        scratch_shapes=[pltpu.VMEM((G, NB, LANES), F32)],
        compiler_params=_cparams(("parallel", "arbitrary")),
        name="moba_route",
    )(aq_hm, ak, avT, kmean)


def _dest_kernel(seg_ref, route_ref, dest_ref):
    h = pl.program_id(0)
    NB = seg_ref.shape[1]
    r = route_ref[0]
    blk = r[0:MOBA_TOPK, :]
    rank = r[MOBA_TOPK:2 * MOBA_TOPK, :]
    base = h * ROUTE_ROWS
    dest = jnp.full(blk.shape, base + ROUTE_ROWS - 1, jnp.int32)
    for j in range(NB):
        dest = jnp.where(blk == j, base + seg_ref[h, j] + rank, dest)
    dest_ref[0] = jnp.concatenate(
        [dest, jnp.zeros((SUBLANES - MOBA_TOPK, dest.shape[1]), jnp.int32)], axis=0)


def _dest(seg_start, route):
    H, _, S = route.shape
    return pl.pallas_call(
        _dest_kernel,
        grid_spec=pltpu.PrefetchScalarGridSpec(
            num_scalar_prefetch=1,
            grid=(H,),
            in_specs=[pl.BlockSpec((1, SUBLANES, S), lambda h, seg: (h, 0, 0))],
            out_specs=pl.BlockSpec((1, SUBLANES, S), lambda h, seg: (h, 0, 0)),
        ),
        out_shape=jax.ShapeDtypeStruct((H, SUBLANES, S), jnp.int32),
        compiler_params=_cparams(("parallel",)),
        name="moba_dest",
    )(seg_start, route)


def _sc_mesh():
    return plsc.VectorSubcoreMesh(core_axis_name="c", subcore_axis_name="s")


def _sc_scatter(rows, row0, n_rows, idx, n_out):
    W = rows.shape[1]
    M = idx.shape[0]
    n_win = n_rows // SC_WINDOW
    win0 = row0 // SC_WINDOW

    reps = M // n_rows

    @pl.kernel(out_type=jax.ShapeDtypeStruct((n_out, W), rows.dtype), mesh=_sc_mesh())
    def k(x_hbm, i_hbm, o_hbm):
        def body(x_vmem, *i_vmems):
            for i_vmem in i_vmems:
                pltpu.sync_copy(x_vmem, o_hbm.at[i_vmem.at[0]])

        pltpu.emit_pipeline(
            body,
            grid=(n_win,),
            in_specs=([pl.BlockSpec((SC_WINDOW, W), lambda w: (win0 + w, 0))]
                      + [pl.BlockSpec((1, SC_WINDOW), lambda w, r=r: (0, r * n_win + w)) for r in range(reps)]),
            out_specs=[],
            core_axis_name=("c", "s"),
            dimension_semantics=(pltpu.PARALLEL,),
        )(x_hbm, *([i_hbm] * reps))

    return k(rows, idx.reshape(1, M))


def _sc_gather(table, idx):
    M = idx.shape[0]
    W = table.shape[1]

    @pl.kernel(out_type=jax.ShapeDtypeStruct((M, W), table.dtype), mesh=_sc_mesh())
    def k(x_hbm, i_hbm, o_hbm):
        def body(i_vmem, o_vmem):
            pltpu.sync_copy(x_hbm.at[i_vmem.at[0]], o_vmem)

        pltpu.emit_pipeline(
            body,
            grid=(M // SC_WINDOW,),
            in_specs=[pl.BlockSpec((1, SC_WINDOW), lambda w: (0, w))],
            out_specs=[pl.BlockSpec((SC_WINDOW, W), lambda w: (w, 0))],
            core_axis_name=("c", "s"),
            dimension_semantics=(pltpu.PARALLEL,),
        )(i_hbm, o_hbm)

    return k(table, idx.reshape(1, M))


def _group_kernel(tile_blk_ref, q_ref, k_ref, vT_ref, out_ref, s_ref, m_ref):
    h = pl.program_id(0)
    u = pl.program_id(1)
    last = pl.num_programs(1) - 2
    BS = MOBA_BLOCK
    Dh = A_HEAD_DIM
    T = GROUP_TILES_PER_STEP
    ones_rows = jnp.ones((2 * SUBLANES, BS), BF16)
    new = lax.rem(u, 2)
    old = 1 - new

    @pl.when((h == 0) & (u == 0))
    def _():
        s_ref[...] = jnp.zeros_like(s_ref)
        m_ref[...] = jnp.zeros_like(m_ref)

    def block_start(group, c):
        j = jnp.maximum(tile_blk_ref[h * ROUTE_TILES + group * T + c], 0)
        return pl.multiple_of(j * BS, BS)

    g_new = jnp.minimum(u, last)
    g_old = jnp.maximum(u - 1, 0)
    @pl.when(tile_blk_ref[h * ROUTE_TILES + g_old * T] >= 0)
    def _():
        ms, rs = [], []
        for c in range(T):
            m = m_ref[old, c]
            p = jnp.exp2(s_ref[old, c] - m.astype(BF16))
            v_aug = jnp.concatenate([vT_ref[:, pl.ds(block_start(g_old, c), BS)], ones_rows], axis=0)
            ms.append(m)
            rs.append(jnp.dot(v_aug, p, preferred_element_type=F32))
        s_new = []
        for c in range(T):
            qs = (q_ref[c * BS:(c + 1) * BS, :] * (Dh ** -0.5 * LOG2_E)).astype(BF16)
            s_new.append(_dot_nt(k_ref[pl.ds(block_start(g_new, c), BS), :], qs))
        for c in range(T):
            l = rs[c][Dh:Dh + 1, :]
            out_ref[c * BS:(c + 1) * BS, :] = _pack_rows(rs[c][:Dh, :] / l, ms[c] + jnp.log2(l))
            sb = s_new[c].astype(BF16)
            s_ref[new, c] = sb
            m_ref[new, c] = jnp.max(sb, axis=0, keepdims=True).astype(F32)


def _group(tile_blk, q_sorted, ak, avT, h0):
    S = ak.shape[0]
    H = q_sorted.shape[0] // ROUTE_ROWS
    Dh = A_HEAD_DIM
    BS = MOBA_BLOCK
    T = GROUP_TILES_PER_STEP
    steps = ROUTE_TILES // T
    rows = T * BS
    return pl.pallas_call(
        _group_kernel,
        grid_spec=pltpu.PrefetchScalarGridSpec(
            num_scalar_prefetch=1,
            grid=(H, steps + 1),
            in_specs=[
                pl.BlockSpec((rows, LANES), lambda h, u, tb: (h * steps + jnp.minimum(u, steps - 1), 0)),
                pl.BlockSpec((S, Dh), lambda h, u, tb: (0, h0 + h)),
                pl.BlockSpec((Dh, S), lambda h, u, tb: (h0 + h, 0)),
            ],
            out_specs=pl.BlockSpec((rows, LANES), lambda h, u, tb: (h * steps + jnp.maximum(u - 1, 0), 0)),
            scratch_shapes=[pltpu.VMEM((2, T, BS, BS), BF16), pltpu.VMEM((2, T, 1, BS), F32)],
        ),
        out_shape=jax.ShapeDtypeStruct((H * ROUTE_ROWS, LANES), F32),
        compiler_params=_cparams(("arbitrary", "arbitrary")),
        name="moba_group",
    )(tile_blk, q_sorted, ak, avT)


def _combine_kernel(*refs):
    i = pl.program_id(0)
    o_ref = refs[-1]
    n_groups = (len(refs) - 1) // 2
    H = refs[n_groups].shape[0]
    BS = refs[n_groups].shape[1]
    low_half = lax.broadcasted_iota(jnp.int32, (BS, LANES), 1) < ROW_WORDS
    for head in range(n_groups * H):
        got_ref, own_ref, h = refs[head // H], refs[n_groups + head // H], head % H
        tiles = [own_ref[h]]
        for r in range(MOBA_TOPK):
            tiles.append(jnp.where(r < i, got_ref[h, r], 0.0))
        lses = [tiles[0][:, ROW_WORDS:ROW_WORDS + 1]]
        lses += [jnp.where(r < i, tiles[r + 1][:, ROW_WORDS:ROW_WORDS + 1], NEG_BIG)
                 for r in range(MOBA_TOPK)]
        top = functools.reduce(jnp.maximum, lses)
        w = [jnp.exp2(x - top) for x in lses]
        inv = 1.0 / sum(w)
        first = jnp.zeros((BS, LANES), F32)
        second = jnp.zeros((BS, LANES), F32)
        for wk, tile in zip(w, tiles):
            words = pltpu.bitcast(tile, jnp.uint32)
            first = first + wk * pltpu.bitcast(words & jnp.uint32(0xFFFF0000), F32)
            second = second + wk * pltpu.bitcast(words << 16, F32)
        o = jnp.where(low_half, first, pltpu.roll(second, ROW_WORDS, 1)) * inv
        o_ref[:, head * A_HEAD_DIM:(head + 1) * A_HEAD_DIM] = o.astype(BF16)


def _combine(gots, owns):
    H, S, _ = owns[0].shape
    BS = MOBA_BLOCK
    return pl.pallas_call(
        _combine_kernel,
        grid=(S // BS,),
        in_specs=([pl.BlockSpec((H, MOBA_TOPK, BS, LANES), lambda i: (0, 0, i, 0))] * len(gots)
                  + [pl.BlockSpec((H, BS, LANES), lambda i: (0, i, 0))] * len(owns)),
        out_specs=pl.BlockSpec((BS, A_WIDTH), lambda i: (i, 0)),
        out_shape=jax.ShapeDtypeStruct((S, A_WIDTH), BF16),
        compiler_params=_cparams(("parallel",)),
        name="moba_combine",
    )(*gots, *owns)


def _moba_routed(aq_hm, ak, avT, kmean):
    H, S, Dh = aq_hm.shape
    NB = S // MOBA_BLOCK
    hn = MOBA_HEADS_PER_GROUP
    q_rows = aq_hm.reshape(H * S, Dh)
    gots, owns = [], []
    for h0 in range(0, H, hn):
        route, cnt, own = _route(aq_hm, ak, avT, kmean, h0, hn)
        cnt = cnt[:, :, 0].astype(jnp.int32)
        seg_tiles = (cnt + MOBA_BLOCK - 1) // MOBA_BLOCK
        seg_end = jnp.cumsum(seg_tiles, axis=1)
        seg_start = (seg_end - seg_tiles) * MOBA_BLOCK
        tile_ids = jnp.arange(ROUTE_TILES, dtype=jnp.int32)
        tile_blk = jnp.sum(tile_ids[None, :, None] >= seg_end[:, None, :], axis=2).astype(jnp.int32)
        tile_blk = jnp.where(tile_blk < NB, tile_blk, -1).reshape(hn * ROUTE_TILES)
        dest = _dest(seg_start, route)[:, :MOBA_TOPK, :]
        q_sorted = _sc_scatter(q_rows, h0 * S, hn * S, dest.transpose(1, 0, 2).reshape(-1),
                               hn * ROUTE_ROWS)
        results = _group(tile_blk, q_sorted, ak, avT, h0)
        gots.append(_sc_gather(results, dest.reshape(-1)).reshape(hn, MOBA_TOPK, S, LANES))
        owns.append(own)
    return _combine(gots, owns)


def _merge_kernel(hm_ref, ha_ref, gm_ref, ga_ref, x_ref, wm_ref, wa_ref, wo_ref, gain_ref, gpre_ref,
                  out_ref, hn_ref):
    ym = jnp.dot(hm_ref[...], wm_ref[...], preferred_element_type=F32)
    ya = jnp.dot(ha_ref[...], wa_ref[...], preferred_element_type=F32)
    merged = _sigmoid(gm_ref[...].astype(F32)) * ym + _sigmoid(ga_ref[...].astype(F32)) * ya
    mix = jnp.dot(merged.astype(BF16), wo_ref[...], preferred_element_type=F32)
    x1 = x_ref[...] + _rms(mix, gain_ref[...])
    out_ref[...] = x1
    hn_ref[...] = _rms(x1, gpre_ref[...]).astype(BF16)


def _merge(hm, ha, P, x, b, wm, wa, wo, gain, gpre, tm=256):
    _, S, D = x.shape
    const = pl.Buffered(1)
    return pl.pallas_call(
        _merge_kernel,
        grid=(S // tm,),
        in_specs=[
            pl.BlockSpec((tm, M_WIDTH), lambda i: (i, 0)),
            pl.BlockSpec((tm, A_WIDTH), lambda i: (i, 0)),
            pl.BlockSpec((tm, D), lambda i: (i, COL_GM // D)),
            pl.BlockSpec((tm, D), lambda i: (i, COL_GA // D)),
            pl.BlockSpec((None, tm, D), lambda i: (b, i, 0)),
            pl.BlockSpec((M_WIDTH, D), lambda i: (0, 0), pipeline_mode=const),
            pl.BlockSpec((A_WIDTH, D), lambda i: (0, 0), pipeline_mode=const),
            pl.BlockSpec((D, D), lambda i: (0, 0), pipeline_mode=const),
            pl.BlockSpec((1, D), lambda i: (0, 0)),
            pl.BlockSpec((1, D), lambda i: (0, 0)),
        ],
        out_specs=[pl.BlockSpec((tm, D), lambda i: (i, 0)), pl.BlockSpec((tm, D), lambda i: (i, 0))],
        out_shape=[jax.ShapeDtypeStruct((S, D), F32), jax.ShapeDtypeStruct((S, D), BF16)],
        compiler_params=_cparams(("parallel",)),
        name="merge",
    )(hm, ha, P, P, x, wm, wa, wo, gain, gpre)


def _ffn_kernel(x_ref, hn_ref, wu_ref, wd_ref, gpost_ref, out_ref, acc_ref):
    f = pl.program_id(1)

    @pl.when(f == 0)
    def _():
        acc_ref[...] = jnp.zeros_like(acc_ref)

    u = jnp.dot(hn_ref[...], wu_ref[...], preferred_element_type=F32)
    u = jnp.square(jnp.maximum(u, 0.0)).astype(BF16)
    acc_ref[...] += jnp.dot(u, wd_ref[...], preferred_element_type=F32)

    @pl.when(f == pl.num_programs(1) - 1)
    def _():
        out_ref[...] = x_ref[...] + _rms(acc_ref[...], gpost_ref[...])


def _ffn(x, hn, wu, wd, gpost, tm=512, tf=1024):
    S, D = x.shape
    Fd = wu.shape[1]
    return pl.pallas_call(
        _ffn_kernel,
        grid=(S // tm, Fd // tf),
        in_specs=[
            pl.BlockSpec((tm, D), lambda i, f: (i, 0)),
            pl.BlockSpec((tm, D), lambda i, f: (i, 0)),
            pl.BlockSpec((D, tf), lambda i, f: (0, f)),
            pl.BlockSpec((tf, D), lambda i, f: (f, 0)),
            pl.BlockSpec((1, D), lambda i, f: (0, 0)),
        ],
        out_specs=pl.BlockSpec((tm, D), lambda i, f: (i, 0)),
        out_shape=jax.ShapeDtypeStruct((S, D), F32),
        scratch_shapes=[pltpu.VMEM((tm, D), F32)],
        compiler_params=_cparams(("parallel", "arbitrary")),
        name="ffn",
    )(x, hn, wu, wd, gpost)


def _layer(x, b, pos, norm_mix_pre, w_in, conv_w, conv_b, i_bias, f_bias, mlstm_norm,
           w_branch_m, w_branch_a, w_out, norm_mix_post, norm_ffn_pre, w_up, w_down,
           norm_ffn_post):
    _, S, D = x.shape
    o = 0
    pieces = {}
    w_in = w_in.astype(BF16)
    for name, width in (("mq", M_WIDTH), ("mk", M_WIDTH), ("mv", M_WIDTH), ("mo", M_WIDTH),
                        ("mi", M_HEADS), ("mf", M_HEADS), ("aq", A_WIDTH), ("ak", A_WIDTH),
                        ("av", A_WIDTH), ("gm", D), ("ga", D)):
        pieces[name] = w_in[:, o:o + width]
        o += width
    w_all = jnp.concatenate([pieces[n] for n in ("gm", "ga", "mq", "mk", "aq", "ak", "mv", "mo", "av")],
                            axis=1)
    w_gate = jnp.concatenate(
        [pieces["mi"], pieces["mf"], jnp.zeros((D, LANES - 2 * M_HEADS), BF16)], axis=1)
    gbias = jnp.concatenate([i_bias, f_bias, jnp.zeros((LANES - 2 * M_HEADS,), F32)])[None, :]
    half = jnp.arange(0, A_HEAD_DIM, 2, dtype=F32) / A_HEAD_DIM
    inv_freq = 1.0 / (ROPE_THETA ** half)
    invf = jnp.concatenate([inv_freq, inv_freq])[None, :]

    P, gate = _proj(x, b, norm_mix_pre[None, :], w_all, w_gate)
    mq, mkT, aq, ak, avT, kmean, gc, gt = _prep(
        P, gate, pos.reshape(S, 1), invf, conv_w, conv_b[None, :], gbias)
    hm = _mlstm(mq, mkT, P, gc, gt, mlstm_norm[None, :])
    ha = _moba_routed(aq, ak, avT, kmean.reshape(S // MOBA_BLOCK, A_WIDTH))
    x1, hn = _merge(hm, ha, P, x, b, w_branch_m.astype(BF16), w_branch_a.astype(BF16),
                    w_out.astype(BF16), norm_mix_post[None, :], norm_ffn_pre[None, :])
    return _ffn(x1, hn, w_up.astype(BF16), w_down.astype(BF16), norm_ffn_post[None, :])


def kernel(x, positions, norm_mix_pre, w_in, conv_w, conv_b, i_bias, f_bias, mlstm_norm,
           w_branch_m, w_branch_a, w_out, norm_mix_post, norm_ffn_pre, w_up, w_down,
           norm_ffn_post):
    B = x.shape[0]
    depth = w_in.shape[0]
    outs = []
    def take(a, i):
        return a.reshape(a.shape[1:]) if a.shape[0] == 1 else a[i]

    for b in range(B):
        xin, bi = x, b
        for l in range(depth):
            xb = _layer(xin, bi, take(positions, b), take(norm_mix_pre, l), take(w_in, l),
                        take(conv_w, l), take(conv_b, l), take(i_bias, l), take(f_bias, l),
                        take(mlstm_norm, l), take(w_branch_m, l), take(w_branch_a, l),
                        take(w_out, l), take(norm_mix_post, l), take(norm_ffn_pre, l),
                        take(w_up, l), take(w_down, l), take(norm_ffn_post, l))
            xin, bi = xb[None], 0
        outs.append(xb)
    return outs[0][None] if B == 1 else jnp.stack(outs, axis=0)
```

```python
import functools

import jax
import jax.numpy as jnp
from jax import lax
from jax.experimental import pallas as pl
from jax.experimental.pallas import tpu as pltpu
from jax.experimental.pallas import tpu_sc as plsc

F32 = jnp.float32
BF16 = jnp.bfloat16

M_HEADS = 4
M_HEAD_DIM = 256
M_WIDTH = M_HEADS * M_HEAD_DIM
M_CHUNK = 128
CONV_WIDTH = 4
A_HEADS = 8
A_HEAD_DIM = 128
A_WIDTH = A_HEADS * A_HEAD_DIM
MOBA_BLOCK = 256
MOBA_TOPK = 3
ROPE_THETA = 10000.0
NORM_EPS = 1e-6

LANES = 128
SUBLANES = 8
VMEM_LIMIT = 56 * 1024 * 1024
NEG_BIG = -1e30
LOG2_E = 1.4426950408889634

COL_GM = 0
COL_GA = 2048
COL_MQK = 4096
COL_AQK = 6144
COL_MV = 8192
COL_MO = 9216
COL_AV = 10240


def _cparams(sem):
    return pltpu.CompilerParams(dimension_semantics=sem, vmem_limit_bytes=VMEM_LIMIT)


def _rms(x, gain):
    ms = jnp.mean(x * x, axis=-1, keepdims=True)
    return x * lax.rsqrt(ms + NORM_EPS) * gain


def _sigmoid(x):
    return 1.0 / (1.0 + jnp.exp(-x))


def _split3(x):
    hi = x.astype(BF16)
    r1 = x - hi.astype(F32)
    mid = r1.astype(BF16)
    lo = (r1 - mid.astype(F32)).astype(BF16)
    return hi, mid, lo


def _proj_kernel(x_ref, g_ref, w_ref, wg_ref, p_ref, gate_ref, xn_ref):
    @pl.when(pl.program_id(1) == 0)
    def _():
        xn = _rms(x_ref[...], g_ref[...]).astype(BF16)
        xn_ref[...] = xn
        gate_ref[...] = jnp.dot(xn, wg_ref[...], preferred_element_type=F32)

    p_ref[...] = jnp.dot(xn_ref[...], w_ref[...], preferred_element_type=F32).astype(BF16)


def _proj(x, b, gain, w_all, w_gate, tm=1024, tn=1024):
    _, S, D = x.shape
    N = w_all.shape[1]
    return pl.pallas_call(
        _proj_kernel,
        grid=(S // tm, N // tn),
        in_specs=[
            pl.BlockSpec((None, tm, D), lambda i, j: (b, i, 0)),
            pl.BlockSpec((1, D), lambda i, j: (0, 0)),
            pl.BlockSpec((D, tn), lambda i, j: (0, j)),
            pl.BlockSpec((D, LANES), lambda i, j: (0, 0)),
        ],
        out_specs=[
            pl.BlockSpec((tm, tn), lambda i, j: (i, j)),
            pl.BlockSpec((tm, LANES), lambda i, j: (i, 0)),
        ],
        out_shape=[
            jax.ShapeDtypeStruct((S, N), BF16),
            jax.ShapeDtypeStruct((S, LANES), F32),
        ],
        scratch_shapes=[pltpu.VMEM((tm, D), BF16)],
        compiler_params=_cparams(("parallel", "arbitrary")),
        name="proj",
    )(x, gain, w_all, w_gate)


PREP_ROWS = MOBA_BLOCK
PREP_COLS = 512
HALO_ROWS = 2 * SUBLANES


def _prep_kernel(pmk_ref, halo_ref, paqk_ref, pav_ref, gate_ref, pos_ref, invf_ref,
                 cw_ref, cb_ref, gb_ref,
                 mq_ref, mkT_ref, aq_ref, ak_ref, avT_ref, kmean_ref, gc_ref, gt_ref):
    i = pl.program_id(0)
    R = PREP_ROWS

    k_scale = M_HEAD_DIM ** -0.5
    for c0 in range(0, 2 * M_WIDTH, PREP_COLS):
        cs = slice(c0, c0 + PREP_COLS)
        prev = halo_ref[:, cs].astype(F32)[HALO_ROWS - SUBLANES:, :]
        prev = jnp.where(i == 0, jnp.zeros_like(prev), prev)
        ext = jnp.concatenate([pmk_ref[:, cs].astype(F32), prev], axis=0)
        acc = cw_ref[0:1, cs] * ext
        for j in range(1, CONV_WIDTH):
            acc = pltpu.roll(acc, 1, 0) + cw_ref[j:j + 1, cs] * ext
        acc = acc[0:R, :] + cb_ref[:, cs]
        y = acc * _sigmoid(acc)
        if c0 < M_WIDTH:
            mq_ref[:, cs] = y.astype(BF16)
        else:
            ks = slice(c0 - M_WIDTH, c0 - M_WIDTH + PREP_COLS)
            mkT_ref[ks, :] = (y * k_scale).T.astype(BF16)

    ang = pos_ref[...].astype(F32) * invf_ref[...]
    cos = jnp.cos(ang)
    lane = lax.broadcasted_iota(jnp.int32, (R, A_HEAD_DIM), 1)
    sin_signed = jnp.where(lane < A_HEAD_DIM // 2, -1.0, 1.0) * jnp.sin(ang)
    for h in range(2 * A_HEADS):
        hs = slice(h * A_HEAD_DIM, (h + 1) * A_HEAD_DIM)
        xh = paqk_ref[:, hs].astype(F32)
        yh = xh * cos + pltpu.roll(xh, A_HEAD_DIM // 2, 1) * sin_signed
        if h < A_HEADS:
            aq_ref[h] = yh
        else:
            ko = slice((h - A_HEADS) * A_HEAD_DIM, (h - A_HEADS + 1) * A_HEAD_DIM)
            ak_ref[:, ko] = yh.astype(BF16)
            kmean_ref[0, :, ko] = jnp.mean(yh, axis=0, keepdims=True)
    for c0 in range(0, A_WIDTH, PREP_COLS):
        avT_ref[c0:c0 + PREP_COLS, :] = pav_ref[:, c0:c0 + PREP_COLS].astype(F32).T.astype(BF16)

    g = gate_ref[...] + gb_ref[...]
    log_f = jnp.minimum(g, 0.0) - jnp.log1p(jnp.exp(-jnp.abs(g)))
    r_i = lax.broadcasted_iota(jnp.int32, (R, R), 0)
    c_i = lax.broadcasted_iota(jnp.int32, (R, R), 1)
    tri = ((r_i >= c_i) & ((r_i // M_CHUNK) == (c_i // M_CHUNK))).astype(BF16)
    hi, mid, lo = _split3(log_f)
    csum = (jnp.dot(tri, hi, preferred_element_type=F32)
            + jnp.dot(tri, mid, preferred_element_type=F32)
            + jnp.dot(tri, lo, preferred_element_type=F32))
    glane = lax.broadcasted_iota(jnp.int32, (R, LANES), 1)
    gc = jnp.where(glane < M_HEADS, g, csum)
    gc_ref[...] = gc
    gt_ref[...] = gc.T[0:SUBLANES, :]


def _prep(P, gate, pos, invf, conv_w, conv_b, gbias):
    S = P.shape[0]
    R = PREP_ROWS
    nb = S // R
    halo_blocks = R // HALO_ROWS
    return pl.pallas_call(
        _prep_kernel,
        grid=(nb,),
        in_specs=[
            pl.BlockSpec((R, 2 * M_WIDTH), lambda i: (i, COL_MQK // (2 * M_WIDTH))),
            pl.BlockSpec((HALO_ROWS, 2 * M_WIDTH),
                         lambda i: (jnp.maximum(i * halo_blocks - 1, 0), COL_MQK // (2 * M_WIDTH))),
            pl.BlockSpec((R, 2 * A_WIDTH), lambda i: (i, COL_AQK // (2 * A_WIDTH))),
            pl.BlockSpec((R, A_WIDTH), lambda i: (i, COL_AV // A_WIDTH)),
            pl.BlockSpec((R, LANES), lambda i: (i, 0)),
            pl.BlockSpec((R, 1), lambda i: (i, 0)),
            pl.BlockSpec((1, A_HEAD_DIM), lambda i: (0, 0)),
            pl.BlockSpec((CONV_WIDTH, 2 * M_WIDTH), lambda i: (0, 0)),
            pl.BlockSpec((1, 2 * M_WIDTH), lambda i: (0, 0)),
            pl.BlockSpec((1, LANES), lambda i: (0, 0)),
        ],
        out_specs=[
            pl.BlockSpec((R, M_WIDTH), lambda i: (i, 0)),
            pl.BlockSpec((M_WIDTH, R), lambda i: (0, i)),
            pl.BlockSpec((A_HEADS, R, A_HEAD_DIM), lambda i: (0, i, 0)),
            pl.BlockSpec((R, A_WIDTH), lambda i: (i, 0)),
            pl.BlockSpec((A_WIDTH, R), lambda i: (0, i)),
            pl.BlockSpec((1, 1, A_WIDTH), lambda i: (i, 0, 0)),
            pl.BlockSpec((R, LANES), lambda i: (i, 0)),
            pl.BlockSpec((SUBLANES, R), lambda i: (0, i)),
        ],
        out_shape=[
            jax.ShapeDtypeStruct((S, M_WIDTH), BF16),
            jax.ShapeDtypeStruct((M_WIDTH, S), BF16),
            jax.ShapeDtypeStruct((A_HEADS, S, A_HEAD_DIM), F32),
            jax.ShapeDtypeStruct((S, A_WIDTH), BF16),
            jax.ShapeDtypeStruct((A_WIDTH, S), BF16),
            jax.ShapeDtypeStruct((nb, 1, A_WIDTH), F32),
            jax.ShapeDtypeStruct((S, LANES), F32),
            jax.ShapeDtypeStruct((SUBLANES, S), F32),
        ],
        compiler_params=_cparams(("parallel",)),
        name="prep",
    )(P, P, P, P, gate, pos, invf, conv_w, conv_b, gbias)


M_AUG = M_HEAD_DIM + LANES
MLSTM_CHUNKS_PER_STEP = 16


def _mlstm_kernel(q_ref, kT_ref, v_ref, mo_ref, gc_ref, gt_ref, gain_ref, out_ref,
                  c_ref, m_ref):
    @pl.when(pl.program_id(0) == 0)
    def _():
        c_ref[...] = jnp.zeros_like(c_ref)
        m_ref[...] = jnp.zeros_like(m_ref)

    L = M_CHUNK
    D = M_HEAD_DIM
    row = lax.broadcasted_iota(jnp.int32, (L, L), 0)
    col = lax.broadcasted_iota(jnp.int32, (L, L), 1)
    causal = row >= col
    ones_col = (lax.broadcasted_iota(jnp.int32, (L, LANES), 1) == 0).astype(BF16)

    heads = range(M_HEADS)
    hsl = [slice(h * D, (h + 1) * D) for h in heads]

    def state_stages(rs):
        q = [q_ref[rs, hsl[h]] for h in heads]
        kT = [kT_ref[hsl[h], rs] for h in heads]
        v_aug = [jnp.concatenate([v_ref[rs, hsl[h]], ones_col], axis=1) for h in heads]
        b_c = [gc_ref[rs, M_HEADS + h:M_HEADS + h + 1] for h in heads]
        b_r = [gt_ref[M_HEADS + h:M_HEADS + h + 1, rs] for h in heads]
        u_r = [gt_ref[h:h + 1, rs] - b_r[h] for h in heads]
        f_tot = [b_r[h][:, L - 1:L] for h in heads]
        m_prev = [m_ref[h:h + 1, 0:1] for h in heads]

        s_qk = [jnp.dot(q[h], kT[h], preferred_element_type=F32) for h in heads]
        q_c = [jnp.dot(q[h], c_ref[h].astype(BF16), preferred_element_type=F32) for h in heads]

        for h in heads:
            w_r = f_tot[h] + u_r[h]
            m_loc = jnp.max(w_r, axis=1, keepdims=True)
            m_new = jnp.maximum(f_tot[h] + m_prev[h], m_loc)
            a = jnp.exp(f_tot[h] + m_prev[h] - m_new)
            e_r = jnp.exp(w_r - m_new)
            keT = (kT[h].astype(F32) * e_r).astype(BF16)
            c_ref[h] = a * c_ref[h] + jnp.dot(keT, v_aug[h], preferred_element_type=F32)
            m_ref[h:h + 1, :] = jnp.broadcast_to(m_new, (1, LANES))
        return rs, v_aug, b_c, u_r, m_prev, s_qk, q_c

    def output_stage(rs, v_aug, b_c, u_r, m_prev, s_qk, q_c):
        for h in heads:
            d_log = jnp.where(causal, b_c[h] + u_r[h], -jnp.inf)
            a_log = b_c[h] + m_prev[h]
            m_t = jnp.maximum(a_log, jnp.max(d_log, axis=1, keepdims=True))
            s_ts = s_qk[h] * jnp.exp(d_log - m_t)
            inter = jnp.exp(a_log - m_t)
            r = inter * q_c[h] + jnp.dot(s_ts.astype(BF16), v_aug[h], preferred_element_type=F32)
            num = r[:, :D]
            den = r[:, D:D + 1]
            hh = num / jnp.maximum(jnp.abs(den), jnp.exp(-m_t))
            hn = hh * lax.rsqrt(jnp.mean(hh * hh, axis=-1, keepdims=True) + NORM_EPS)
            out_ref[rs, hsl[h]] = (hn * gain_ref[:, hsl[h]]
                                   * _sigmoid(mo_ref[rs, hsl[h]].astype(F32))).astype(BF16)

    def chunk(c, carry):
        output_stage(*state_stages(pl.ds(pl.multiple_of(c * L, L), L)))
        return carry

    lax.fori_loop(0, MLSTM_CHUNKS_PER_STEP, chunk, 0)


def _mlstm(mq, mkT, P, gc, gt, gain):
    S = mq.shape[0]
    L = M_CHUNK * MLSTM_CHUNKS_PER_STEP
    return pl.pallas_call(
        _mlstm_kernel,
        grid=(S // L,),
        in_specs=[
            pl.BlockSpec((L, M_WIDTH), lambda c: (c, 0)),
            pl.BlockSpec((M_WIDTH, L), lambda c: (0, c)),
            pl.BlockSpec((L, M_WIDTH), lambda c: (c, COL_MV // M_WIDTH)),
            pl.BlockSpec((L, M_WIDTH), lambda c: (c, COL_MO // M_WIDTH)),
            pl.BlockSpec((L, LANES), lambda c: (c, 0)),
            pl.BlockSpec((SUBLANES, L), lambda c: (0, c)),
            pl.BlockSpec((1, M_WIDTH), lambda c: (0, 0)),
        ],
        out_specs=pl.BlockSpec((L, M_WIDTH), lambda c: (c, 0)),
        out_shape=jax.ShapeDtypeStruct((S, M_WIDTH), BF16),
        scratch_shapes=[
            pltpu.VMEM((M_HEADS, M_HEAD_DIM, M_AUG), F32),
            pltpu.VMEM((SUBLANES, LANES), F32),
        ],
        compiler_params=_cparams(("arbitrary",)),
        name="mlstm",
    )(mq, mkT, P, P, gc, gt, gain)


MOBA_HEADS_PER_GROUP = 4
ROUTE_TILES = 264
ROUTE_ROWS = ROUTE_TILES * MOBA_BLOCK
GROUP_TILES_PER_STEP = 33
ROW_WORDS = A_HEAD_DIM // 2
SC_WINDOW = 256


def _pack_rows(o_t, lse):
    q = o_t.shape[1]
    hi = pltpu.bitcast(o_t[:ROW_WORDS].astype(BF16).astype(F32), jnp.uint32)
    lo = pltpu.bitcast(o_t[ROW_WORDS:].astype(BF16).astype(F32), jnp.uint32)
    words = pltpu.bitcast(hi | (lo >> 16), F32)
    tail = jnp.concatenate([jnp.broadcast_to(lse, (SUBLANES, q)),
                            jnp.zeros((LANES - ROW_WORDS - SUBLANES, q), F32)], axis=0)
    return jnp.concatenate([words, tail], axis=0).T


def _dot_nt(a, b):
    return lax.dot_general(a, b, (((1,), (1,)), ((), ())), preferred_element_type=F32)


def _block_partial(k_blk, v_t_blk, qs, ones_rows, causal_mask=None):
    s = _dot_nt(k_blk, qs)
    if causal_mask is not None:
        s = jnp.where(causal_mask, s, NEG_BIG)
    sb = s.astype(BF16)
    m = jnp.max(sb, axis=0, keepdims=True)
    p = jnp.exp2(sb - m)
    r = jnp.dot(jnp.concatenate([v_t_blk, ones_rows], axis=0), p, preferred_element_type=F32)
    dh = v_t_blk.shape[0]
    l = r[dh:dh + 1, :]
    return r[:dh, :] / l, m.astype(F32) + jnp.log2(l)


def _route_kernel(q_ref, k_ref, vT_ref, km_ref, route_ref, cnt_ref, own_ref, cnt_acc):
    i = pl.program_id(1)
    BS = MOBA_BLOCK
    Dh = A_HEAD_DIM
    NB = km_ref.shape[0]
    G = q_ref.shape[0]
    blk = lax.broadcasted_iota(jnp.int32, (NB, BS), 0)
    kpos = lax.broadcasted_iota(jnp.int32, (BS, BS), 0)
    qpos = lax.broadcasted_iota(jnp.int32, (BS, BS), 1)
    earlier = (kpos < qpos).astype(BF16)
    ones_rows = jnp.ones((2 * SUBLANES, BS), BF16)

    @pl.when(i == 0)
    def _():
        cnt_acc[...] = jnp.zeros_like(cnt_acc)

    q = [q_ref[g] for g in range(G)]
    qs = [(q[g] * (Dh ** -0.5 * LOG2_E)).astype(BF16) for g in range(G)]
    parts = [_block_partial(k_ref[:, g * Dh:(g + 1) * Dh], vT_ref[g * Dh:(g + 1) * Dh, :], qs[g],
                            ones_rows, kpos <= qpos) for g in range(G)]

    gates = []
    for g in range(G):
        hs = slice(g * Dh, (g + 1) * Dh)
        km = km_ref[:, hs]
        kh = km.astype(BF16)
        kl = (km - kh.astype(F32)).astype(BF16)
        qh = q[g].astype(BF16)
        ql = (q[g] - qh.astype(F32)).astype(BF16)
        gate = _dot_nt(kh, qh) + _dot_nt(kh, ql) + _dot_nt(kl, qh)
        gates.append(jnp.where(blk < i, gate, -jnp.inf))
    picks = [[] for _ in range(G)]
    rows = [[] for _ in range(G)]
    for r in range(MOBA_TOPK):
        for g in range(G):
            mx = jnp.max(gates[g], axis=0, keepdims=True)
            idx = jnp.min(jnp.where(gates[g] == mx, blk, NB), axis=0, keepdims=True)
            idx = jnp.where(r < i, idx, -1)
            pick = blk == idx
            gates[g] = jnp.where(pick, -jnp.inf, gates[g])
            picks[g].append(pick)
            rows[g].append(idx)
    for g in range(G):
        onehot = sum(p.astype(F32) for p in picks[g])
        before = cnt_acc[g][:, 0:1] + jnp.dot(onehot.astype(BF16), earlier, preferred_element_type=F32)
        for r in range(MOBA_TOPK):
            rank = jnp.sum(jnp.where(picks[g][r], before, 0.0), axis=0, keepdims=True)
            rows[g].append(rank.astype(jnp.int32))
        rows[g].append(jnp.zeros((SUBLANES - 2 * MOBA_TOPK, BS), jnp.int32))
        route_ref[g] = jnp.concatenate(rows[g], axis=0)
        cnt_new = cnt_acc[g] + jnp.sum(onehot, axis=1, keepdims=True)
        cnt_acc[g] = cnt_new
        cnt_ref[g] = cnt_new
        own_ref[g] = _pack_rows(*parts[g])


def _route(aq_hm, ak, avT, kmean, h0, H):
    _, S, Dh = aq_hm.shape
    BS = MOBA_BLOCK
    NB = S // BS
    G = MOBA_HEADS_PER_GROUP
    W = G * Dh
    hb = h0 // G
    return pl.pallas_call(
        _route_kernel,
        grid=(H // G, NB),
        in_specs=[
            pl.BlockSpec((G, BS, Dh), lambda h, i: (hb + h, i, 0)),
            pl.BlockSpec((BS, W), lambda h, i: (i, hb + h)),
            pl.BlockSpec((W, BS), lambda h, i: (hb + h, i)),
            pl.BlockSpec((NB, W), lambda h, i: (0, hb + h)),
        ],
        out_specs=[
            pl.BlockSpec((G, SUBLANES, BS), lambda h, i: (h, 0, i)),
            pl.BlockSpec((G, NB, LANES), lambda h, i: (h, 0, 0)),
            pl.BlockSpec((G, BS, LANES), lambda h, i: (h, i, 0)),
        ],
        out_shape=[
            jax.ShapeDtypeStruct((H, SUBLANES, S), jnp.int32),
            jax.ShapeDtypeStruct((H, NB, LANES), F32),
            jax.ShapeDtypeStruct((H, S, LANES), F32),
        ],
        scratch_shapes=[pltpu.VMEM((G, NB, LANES), F32)],
        compiler_params=_cparams(("parallel", "arbitrary")),
        name="moba_route",
    )(aq_hm, ak, avT, kmean)


def _dest_kernel(seg_ref, route_ref, dest_ref):
    h = pl.program_id(0)
    NB = seg_ref.shape[1]
    r = route_ref[0]
    blk = r[0:MOBA_TOPK, :]
    rank = r[MOBA_TOPK:2 * MOBA_TOPK, :]
    base = h * ROUTE_ROWS
    dest = jnp.full(blk.shape, base + ROUTE_ROWS - 1, jnp.int32)
    for j in range(NB):
        dest = jnp.where(blk == j, base + seg_ref[h, j] + rank, dest)
    dest_ref[0] = jnp.concatenate(
        [dest, jnp.zeros((SUBLANES - MOBA_TOPK, dest.shape[1]), jnp.int32)], axis=0)


def _dest(seg_start, route):
    H, _, S = route.shape
    return pl.pallas_call(
        _dest_kernel,
        grid_spec=pltpu.PrefetchScalarGridSpec(
            num_scalar_prefetch=1,
            grid=(H,),
            in_specs=[pl.BlockSpec((1, SUBLANES, S), lambda h, seg: (h, 0, 0))],
            out_specs=pl.BlockSpec((1, SUBLANES, S), lambda h, seg: (h, 0, 0)),
        ),
        out_shape=jax.ShapeDtypeStruct((H, SUBLANES, S), jnp.int32),
        compiler_params=_cparams(("parallel",)),
        name="moba_dest",
    )(seg_start, route)


def _sc_mesh():
    return plsc.VectorSubcoreMesh(core_axis_name="c", subcore_axis_name="s")


def _sc_scatter(rows, row0, n_rows, idx, n_out):
    W = rows.shape[1]
    M = idx.shape[0]
    n_win = n_rows // SC_WINDOW
    win0 = row0 // SC_WINDOW

    reps = M // n_rows

    @pl.kernel(out_type=jax.ShapeDtypeStruct((n_out, W), rows.dtype), mesh=_sc_mesh())
    def k(x_hbm, i_hbm, o_hbm):
        def body(x_vmem, *i_vmems):
            for i_vmem in i_vmems:
                pltpu.sync_copy(x_vmem, o_hbm.at[i_vmem.at[0]])

        pltpu.emit_pipeline(
            body,
            grid=(n_win,),
            in_specs=([pl.BlockSpec((SC_WINDOW, W), lambda w: (win0 + w, 0))]
                      + [pl.BlockSpec((1, SC_WINDOW), lambda w, r=r: (0, r * n_win + w)) for r in range(reps)]),
            out_specs=[],
            core_axis_name=("c", "s"),
            dimension_semantics=(pltpu.PARALLEL,),
        )(x_hbm, *([i_hbm] * reps))

    return k(rows, idx.reshape(1, M))


def _sc_gather(table, idx):
    M = idx.shape[0]
    W = table.shape[1]

    @pl.kernel(out_type=jax.ShapeDtypeStruct((M, W), table.dtype), mesh=_sc_mesh())
    def k(x_hbm, i_hbm, o_hbm):
        def body(i_vmem, o_vmem):
            pltpu.sync_copy(x_hbm.at[i_vmem.at[0]], o_vmem)

        pltpu.emit_pipeline(
            body,
            grid=(M // SC_WINDOW,),
            in_specs=[pl.BlockSpec((1, SC_WINDOW), lambda w: (0, w))],
            out_specs=[pl.BlockSpec((SC_WINDOW, W), lambda w: (w, 0))],
            core_axis_name=("c", "s"),
            dimension_semantics=(pltpu.PARALLEL,),
        )(i_hbm, o_hbm)

    return k(table, idx.reshape(1, M))


def _group_kernel(tile_blk_ref, q_ref, k_ref, vT_ref, out_ref, s_ref, m_ref):
    h = pl.program_id(0)
    u = pl.program_id(1)
    last = pl.num_programs(1) - 2
    BS = MOBA_BLOCK
    Dh = A_HEAD_DIM
    T = GROUP_TILES_PER_STEP
    ones_rows = jnp.ones((2 * SUBLANES, BS), BF16)
    new = lax.rem(u, 2)
    old = 1 - new

    @pl.when((h == 0) & (u == 0))
    def _():
        s_ref[...] = jnp.zeros_like(s_ref)
        m_ref[...] = jnp.zeros_like(m_ref)

    def block_start(group, c):
        j = jnp.maximum(tile_blk_ref[h * ROUTE_TILES + group * T + c], 0)
        return pl.multiple_of(j * BS, BS)

    g_new = jnp.minimum(u, last)
    g_old = jnp.maximum(u - 1, 0)
    @pl.when(tile_blk_ref[h * ROUTE_TILES + g_old * T] >= 0)
    def _():
        ms, rs = [], []
        for c in range(T):
            m = m_ref[old, c]
            p = jnp.exp2(s_ref[old, c] - m.astype(BF16))
            v_aug = jnp.concatenate([vT_ref[:, pl.ds(block_start(g_old, c), BS)], ones_rows], axis=0)
            ms.append(m)
            rs.append(jnp.dot(v_aug, p, preferred_element_type=F32))
        s_new = []
        for c in range(T):
            qs = (q_ref[c * BS:(c + 1) * BS, :] * (Dh ** -0.5 * LOG2_E)).astype(BF16)
            s_new.append(_dot_nt(k_ref[pl.ds(block_start(g_new, c), BS), :], qs))
        for c in range(T):
            l = rs[c][Dh:Dh + 1, :]
            out_ref[c * BS:(c + 1) * BS, :] = _pack_rows(rs[c][:Dh, :] / l, ms[c] + jnp.log2(l))
            sb = s_new[c].astype(BF16)
            s_ref[new, c] = sb
            m_ref[new, c] = jnp.max(sb, axis=0, keepdims=True).astype(F32)


def _group(tile_blk, q_sorted, ak, avT, h0):
    S = ak.shape[0]
    H = q_sorted.shape[0] // ROUTE_ROWS
    Dh = A_HEAD_DIM
    BS = MOBA_BLOCK
    T = GROUP_TILES_PER_STEP
    steps = ROUTE_TILES // T
    rows = T * BS
    return pl.pallas_call(
        _group_kernel,
        grid_spec=pltpu.PrefetchScalarGridSpec(
            num_scalar_prefetch=1,
            grid=(H, steps + 1),
            in_specs=[
                pl.BlockSpec((rows, LANES), lambda h, u, tb: (h * steps + jnp.minimum(u, steps - 1), 0)),
                pl.BlockSpec((S, Dh), lambda h, u, tb: (0, h0 + h)),
                pl.BlockSpec((Dh, S), lambda h, u, tb: (h0 + h, 0)),
            ],
            out_specs=pl.BlockSpec((rows, LANES), lambda h, u, tb: (h * steps + jnp.maximum(u - 1, 0), 0)),
            scratch_shapes=[pltpu.VMEM((2, T, BS, BS), BF16), pltpu.VMEM((2, T, 1, BS), F32)],
        ),
        out_shape=jax.ShapeDtypeStruct((H * ROUTE_ROWS, LANES), F32),
        compiler_params=_cparams(("arbitrary", "arbitrary")),
        name="moba_group",
    )(tile_blk, q_sorted, ak, avT)


def _combine_kernel(*refs):
    i = pl.program_id(0)
    o_ref = refs[-1]
    n_groups = (len(refs) - 1) // 2
    H = refs[n_groups].shape[0]
    BS = refs[n_groups].shape[1]
    low_half = lax.broadcasted_iota(jnp.int32, (BS, LANES), 1) < ROW_WORDS
    for head in range(n_groups * H):
        got_ref, own_ref, h = refs[head // H], refs[n_groups + head // H], head % H
        tiles = [own_ref[h]]
        for r in range(MOBA_TOPK):
            tiles.append(jnp.where(r < i, got_ref[h, r], 0.0))
        lses = [tiles[0][:, ROW_WORDS:ROW_WORDS + 1]]
        lses += [jnp.where(r < i, tiles[r + 1][:, ROW_WORDS:ROW_WORDS + 1], NEG_BIG)
                 for r in range(MOBA_TOPK)]
        top = functools.reduce(jnp.maximum, lses)
        w = [jnp.exp2(x - top) for x in lses]
        inv = 1.0 / sum(w)
        first = jnp.zeros((BS, LANES), F32)
        second = jnp.zeros((BS, LANES), F32)
        for wk, tile in zip(w, tiles):
            words = pltpu.bitcast(tile, jnp.uint32)
            first = first + wk * pltpu.bitcast(words & jnp.uint32(0xFFFF0000), F32)
            second = second + wk * pltpu.bitcast(words << 16, F32)
        o = jnp.where(low_half, first, pltpu.roll(second, ROW_WORDS, 1)) * inv
        o_ref[:, head * A_HEAD_DIM:(head + 1) * A_HEAD_DIM] = o.astype(BF16)


def _combine(gots, owns):
    H, S, _ = owns[0].shape
    BS = MOBA_BLOCK
    return pl.pallas_call(
        _combine_kernel,
        grid=(S // BS,),
        in_specs=([pl.BlockSpec((H, MOBA_TOPK, BS, LANES), lambda i: (0, 0, i, 0))] * len(gots)
                  + [pl.BlockSpec((H, BS, LANES), lambda i: (0, i, 0))] * len(owns)),
        out_specs=pl.BlockSpec((BS, A_WIDTH), lambda i: (i, 0)),
        out_shape=jax.ShapeDtypeStruct((S, A_WIDTH), BF16),
        compiler_params=_cparams(("parallel",)),
        name="moba_combine",
    )(*gots, *owns)


def _moba_routed(aq_hm, ak, avT, kmean):
    H, S, Dh = aq_hm.shape
    NB = S // MOBA_BLOCK
    hn = MOBA_HEADS_PER_GROUP
    q_rows = aq_hm.reshape(H * S, Dh)
    gots, owns = [], []
    for h0 in range(0, H, hn):
        route, cnt, own = _route(aq_hm, ak, avT, kmean, h0, hn)
        cnt = cnt[:, :, 0].astype(jnp.int32)
        seg_tiles = (cnt + MOBA_BLOCK - 1) // MOBA_BLOCK
        seg_end = jnp.cumsum(seg_tiles, axis=1)
        seg_start = (seg_end - seg_tiles) * MOBA_BLOCK
        tile_ids = jnp.arange(ROUTE_TILES, dtype=jnp.int32)
        tile_blk = jnp.sum(tile_ids[None, :, None] >= seg_end[:, None, :], axis=2).astype(jnp.int32)
        tile_blk = jnp.where(tile_blk < NB, tile_blk, -1).reshape(hn * ROUTE_TILES)
        dest = _dest(seg_start, route)[:, :MOBA_TOPK, :]
        q_sorted = _sc_scatter(q_rows, h0 * S, hn * S, dest.transpose(1, 0, 2).reshape(-1),
                               hn * ROUTE_ROWS)
        results = _group(tile_blk, q_sorted, ak, avT, h0)
        gots.append(_sc_gather(results, dest.reshape(-1)).reshape(hn, MOBA_TOPK, S, LANES))
        owns.append(own)
    return _combine(gots, owns)


def _merge_kernel(hm_ref, ha_ref, gm_ref, ga_ref, x_ref, wm_ref, wa_ref, wo_ref, gain_ref,
                  out_ref):
    ym = jnp.dot(hm_ref[...], wm_ref[...], preferred_element_type=F32)
    ya = jnp.dot(ha_ref[...], wa_ref[...], preferred_element_type=F32)
    merged = _sigmoid(gm_ref[...].astype(F32)) * ym + _sigmoid(ga_ref[...].astype(F32)) * ya
    mix = jnp.dot(merged.astype(BF16), wo_ref[...], preferred_element_type=F32)
    out_ref[...] = x_ref[...] + _rms(mix, gain_ref[...])


def _merge(hm, ha, P, x, b, wm, wa, wo, gain, tm=256):
    _, S, D = x.shape
    const = pl.Buffered(1)
    return pl.pallas_call(
        _merge_kernel,
        grid=(S // tm,),
        in_specs=[
            pl.BlockSpec((tm, M_WIDTH), lambda i: (i, 0)),
            pl.BlockSpec((tm, A_WIDTH), lambda i: (i, 0)),
            pl.BlockSpec((tm, D), lambda i: (i, COL_GM // D)),
            pl.BlockSpec((tm, D), lambda i: (i, COL_GA // D)),
            pl.BlockSpec((None, tm, D), lambda i: (b, i, 0)),
            pl.BlockSpec((M_WIDTH, D), lambda i: (0, 0), pipeline_mode=const),
            pl.BlockSpec((A_WIDTH, D), lambda i: (0, 0), pipeline_mode=const),
            pl.BlockSpec((D, D), lambda i: (0, 0), pipeline_mode=const),
            pl.BlockSpec((1, D), lambda i: (0, 0)),
        ],
        out_specs=pl.BlockSpec((tm, D), lambda i: (i, 0)),
        out_shape=jax.ShapeDtypeStruct((S, D), F32),
        compiler_params=_cparams(("parallel",)),
        name="merge",
    )(hm, ha, P, P, x, wm, wa, wo, gain)


def _ffn_kernel(x_ref, gpre_ref, wu_ref, wd_ref, gpost_ref, out_ref, hn_ref, acc_ref):
    f = pl.program_id(1)

    @pl.when(f == 0)
    def _():
        hn_ref[...] = _rms(x_ref[...], gpre_ref[...]).astype(BF16)
        acc_ref[...] = jnp.zeros_like(acc_ref)

    u = jnp.dot(hn_ref[...], wu_ref[...], preferred_element_type=F32)
    u = jnp.square(jnp.maximum(u, 0.0)).astype(BF16)
    acc_ref[...] += jnp.dot(u, wd_ref[...], preferred_element_type=F32)

    @pl.when(f == pl.num_programs(1) - 1)
    def _():
        out_ref[...] = x_ref[...] + _rms(acc_ref[...], gpost_ref[...])


def _ffn(x, gpre, wu, wd, gpost, tm=512, tf=1024):
    S, D = x.shape
    Fd = wu.shape[1]
    return pl.pallas_call(
        _ffn_kernel,
        grid=(S // tm, Fd // tf),
        in_specs=[
            pl.BlockSpec((tm, D), lambda i, f: (i, 0)),
            pl.BlockSpec((1, D), lambda i, f: (0, 0)),
            pl.BlockSpec((D, tf), lambda i, f: (0, f)),
            pl.BlockSpec((tf, D), lambda i, f: (f, 0)),
            pl.BlockSpec((1, D), lambda i, f: (0, 0)),
        ],
        out_specs=pl.BlockSpec((tm, D), lambda i, f: (i, 0)),
        out_shape=jax.ShapeDtypeStruct((S, D), F32),
        scratch_shapes=[pltpu.VMEM((tm, D), BF16), pltpu.VMEM((tm, D), F32)],
        compiler_params=_cparams(("parallel", "arbitrary")),
        name="ffn",
    )(x, gpre, wu, wd, gpost)


def _layer(x, b, pos, norm_mix_pre, w_in, conv_w, conv_b, i_bias, f_bias, mlstm_norm,
           w_branch_m, w_branch_a, w_out, norm_mix_post, norm_ffn_pre, w_up, w_down,
           norm_ffn_post):
    _, S, D = x.shape
    o = 0
    pieces = {}
    w_in = w_in.astype(BF16)
    for name, width in (("mq", M_WIDTH), ("mk", M_WIDTH), ("mv", M_WIDTH), ("mo", M_WIDTH),
                        ("mi", M_HEADS), ("mf", M_HEADS), ("aq", A_WIDTH), ("ak", A_WIDTH),
                        ("av", A_WIDTH), ("gm", D), ("ga", D)):
        pieces[name] = w_in[:, o:o + width]
        o += width
    w_all = jnp.concatenate([pieces[n] for n in ("gm", "ga", "mq", "mk", "aq", "ak", "mv", "mo", "av")],
                            axis=1)
    w_gate = jnp.concatenate(
        [pieces["mi"], pieces["mf"], jnp.zeros((D, LANES - 2 * M_HEADS), BF16)], axis=1)
    gbias = jnp.concatenate([i_bias, f_bias, jnp.zeros((LANES - 2 * M_HEADS,), F32)])[None, :]
    half = jnp.arange(0, A_HEAD_DIM, 2, dtype=F32) / A_HEAD_DIM
    inv_freq = 1.0 / (ROPE_THETA ** half)
    invf = jnp.concatenate([inv_freq, inv_freq])[None, :]

    P, gate = _proj(x, b, norm_mix_pre[None, :], w_all, w_gate)
    mq, mkT, aq, ak, avT, kmean, gc, gt = _prep(
        P, gate, pos.reshape(S, 1), invf, conv_w, conv_b[None, :], gbias)
    hm = _mlstm(mq, mkT, P, gc, gt, mlstm_norm[None, :])
    ha = _moba_routed(aq, ak, avT, kmean.reshape(S // MOBA_BLOCK, A_WIDTH))
    x1 = _merge(hm, ha, P, x, b, w_branch_m.astype(BF16), w_branch_a.astype(BF16),
                w_out.astype(BF16), norm_mix_post[None, :])
    return _ffn(x1, norm_ffn_pre[None, :], w_up.astype(BF16), w_down.astype(BF16),
                norm_ffn_post[None, :])


def kernel(x, positions, norm_mix_pre, w_in, conv_w, conv_b, i_bias, f_bias, mlstm_norm,
           w_branch_m, w_branch_a, w_out, norm_mix_post, norm_ffn_pre, w_up, w_down,
           norm_ffn_post):
    B = x.shape[0]
    depth = w_in.shape[0]
    outs = []
    def take(a, i):
        return a.reshape(a.shape[1:]) if a.shape[0] == 1 else a[i]

    for b in range(B):
        xin, bi = x, b
        for l in range(depth):
            xb = _layer(xin, bi, take(positions, b), take(norm_mix_pre, l), take(w_in, l),
                        take(conv_w, l), take(conv_b, l), take(i_bias, l), take(f_bias, l),
                        take(mlstm_norm, l), take(w_branch_m, l), take(w_branch_a, l),
                        take(w_out, l), take(norm_mix_post, l), take(norm_ffn_pre, l),
                        take(w_up, l), take(w_down, l), take(norm_ffn_post, l))
            xin, bi = xb[None], 0
        outs.append(xb)
    return outs[0][None] if B == 1 else jnp.stack(outs, axis=0)
```

```python
import functools

import jax
import jax.numpy as jnp
from jax import lax
from jax.experimental import pallas as pl
from jax.experimental.pallas import tpu as pltpu
from jax.experimental.pallas import tpu_sc as plsc

F32 = jnp.float32
BF16 = jnp.bfloat16

M_HEADS = 4
M_HEAD_DIM = 256
M_WIDTH = M_HEADS * M_HEAD_DIM
M_CHUNK = 128
CONV_WIDTH = 4
A_HEADS = 8
A_HEAD_DIM = 128
A_WIDTH = A_HEADS * A_HEAD_DIM
MOBA_BLOCK = 256
MOBA_TOPK = 3
ROPE_THETA = 10000.0
NORM_EPS = 1e-6

LANES = 128
SUBLANES = 8
VMEM_LIMIT = 56 * 1024 * 1024
NEG_BIG = -1e30
LOG2_E = 1.4426950408889634

COL_GM = 0
COL_GA = 2048
COL_MQK = 4096
COL_AQK = 6144
COL_MV = 8192
COL_MO = 9216
COL_AV = 10240


def _cparams(sem):
    return pltpu.CompilerParams(dimension_semantics=sem, vmem_limit_bytes=VMEM_LIMIT)


def _rms(x, gain):
    ms = jnp.mean(x * x, axis=-1, keepdims=True)
    return x * lax.rsqrt(ms + NORM_EPS) * gain


def _sigmoid(x):
    return 1.0 / (1.0 + jnp.exp(-x))


def _split3(x):
    hi = x.astype(BF16)
    r1 = x - hi.astype(F32)
    mid = r1.astype(BF16)
    lo = (r1 - mid.astype(F32)).astype(BF16)
    return hi, mid, lo


def _proj_kernel(x_ref, g_ref, w_ref, wg_ref, p_ref, gate_ref, xn_ref):
    @pl.when(pl.program_id(1) == 0)
    def _():
        xn = _rms(x_ref[...], g_ref[...]).astype(BF16)
        xn_ref[...] = xn
        gate_ref[...] = jnp.dot(xn, wg_ref[...], preferred_element_type=F32)

    p_ref[...] = jnp.dot(xn_ref[...], w_ref[...], preferred_element_type=F32).astype(BF16)


def _proj(x, b, gain, w_all, w_gate, tm=512, tn=2816):
    _, S, D = x.shape
    N = w_all.shape[1]
    return pl.pallas_call(
        _proj_kernel,
        grid=(S // tm, N // tn),
        in_specs=[
            pl.BlockSpec((None, tm, D), lambda i, j: (b, i, 0)),
            pl.BlockSpec((1, D), lambda i, j: (0, 0)),
            pl.BlockSpec((D, tn), lambda i, j: (0, j)),
            pl.BlockSpec((D, LANES), lambda i, j: (0, 0)),
        ],
        out_specs=[
            pl.BlockSpec((tm, tn), lambda i, j: (i, j)),
            pl.BlockSpec((tm, LANES), lambda i, j: (i, 0)),
        ],
        out_shape=[
            jax.ShapeDtypeStruct((S, N), BF16),
            jax.ShapeDtypeStruct((S, LANES), F32),
        ],
        scratch_shapes=[pltpu.VMEM((tm, D), BF16)],
        compiler_params=_cparams(("parallel", "arbitrary")),
        name="proj",
    )(x, gain, w_all, w_gate)


PREP_ROWS = MOBA_BLOCK
PREP_COLS = 512
HALO_ROWS = 2 * SUBLANES


def _prep_kernel(pmk_ref, halo_ref, paqk_ref, pav_ref, gate_ref, pos_ref, invf_ref,
                 cw_ref, cb_ref, gb_ref,
                 mq_ref, mkT_ref, aq_ref, ak_ref, avT_ref, kmean_ref, gc_ref, gt_ref):
    i = pl.program_id(0)
    R = PREP_ROWS

    k_scale = M_HEAD_DIM ** -0.5
    for c0 in range(0, 2 * M_WIDTH, PREP_COLS):
        cs = slice(c0, c0 + PREP_COLS)
        prev = halo_ref[:, cs].astype(F32)[HALO_ROWS - SUBLANES:, :]
        prev = jnp.where(i == 0, jnp.zeros_like(prev), prev)
        ext = jnp.concatenate([pmk_ref[:, cs].astype(F32), prev], axis=0)
        acc = cw_ref[0:1, cs] * ext
        for j in range(1, CONV_WIDTH):
            acc = pltpu.roll(acc, 1, 0) + cw_ref[j:j + 1, cs] * ext
        acc = acc[0:R, :] + cb_ref[:, cs]
        y = acc * _sigmoid(acc)
        if c0 < M_WIDTH:
            mq_ref[:, cs] = y.astype(BF16)
        else:
            ks = slice(c0 - M_WIDTH, c0 - M_WIDTH + PREP_COLS)
            mkT_ref[ks, :] = (y * k_scale).T.astype(BF16)

    ang = pos_ref[...].astype(F32) * invf_ref[...]
    cos = jnp.cos(ang)
    lane = lax.broadcasted_iota(jnp.int32, (R, A_HEAD_DIM), 1)
    sin_signed = jnp.where(lane < A_HEAD_DIM // 2, -1.0, 1.0) * jnp.sin(ang)
    for h in range(2 * A_HEADS):
        hs = slice(h * A_HEAD_DIM, (h + 1) * A_HEAD_DIM)
        xh = paqk_ref[:, hs].astype(F32)
        yh = xh * cos + pltpu.roll(xh, A_HEAD_DIM // 2, 1) * sin_signed
        if h < A_HEADS:
            aq_ref[h] = yh
        else:
            ko = slice((h - A_HEADS) * A_HEAD_DIM, (h - A_HEADS + 1) * A_HEAD_DIM)
            ak_ref[:, ko] = yh.astype(BF16)
            kmean_ref[0, :, ko] = jnp.mean(yh, axis=0, keepdims=True)
    for c0 in range(0, A_WIDTH, PREP_COLS):
        avT_ref[c0:c0 + PREP_COLS, :] = pav_ref[:, c0:c0 + PREP_COLS].astype(F32).T.astype(BF16)

    g = gate_ref[...] + gb_ref[...]
    log_f = jnp.minimum(g, 0.0) - jnp.log1p(jnp.exp(-jnp.abs(g)))
    r_i = lax.broadcasted_iota(jnp.int32, (R, R), 0)
    c_i = lax.broadcasted_iota(jnp.int32, (R, R), 1)
    tri = ((r_i >= c_i) & ((r_i // M_CHUNK) == (c_i // M_CHUNK))).astype(BF16)
    hi, mid, lo = _split3(log_f)
    csum = (jnp.dot(tri, hi, preferred_element_type=F32)
            + jnp.dot(tri, mid, preferred_element_type=F32)
            + jnp.dot(tri, lo, preferred_element_type=F32))
    glane = lax.broadcasted_iota(jnp.int32, (R, LANES), 1)
    gc = jnp.where(glane < M_HEADS, g, csum)
    gc_ref[...] = gc
    gt_ref[...] = gc.T[0:SUBLANES, :]


def _prep(P, gate, pos, invf, conv_w, conv_b, gbias):
    S = P.shape[0]
    R = PREP_ROWS
    nb = S // R
    halo_blocks = R // HALO_ROWS
    return pl.pallas_call(
        _prep_kernel,
        grid=(nb,),
        in_specs=[
            pl.BlockSpec((R, 2 * M_WIDTH), lambda i: (i, COL_MQK // (2 * M_WIDTH))),
            pl.BlockSpec((HALO_ROWS, 2 * M_WIDTH),
                         lambda i: (jnp.maximum(i * halo_blocks - 1, 0), COL_MQK // (2 * M_WIDTH))),
            pl.BlockSpec((R, 2 * A_WIDTH), lambda i: (i, COL_AQK // (2 * A_WIDTH))),
            pl.BlockSpec((R, A_WIDTH), lambda i: (i, COL_AV // A_WIDTH)),
            pl.BlockSpec((R, LANES), lambda i: (i, 0)),
            pl.BlockSpec((R, 1), lambda i: (i, 0)),
            pl.BlockSpec((1, A_HEAD_DIM), lambda i: (0, 0)),
            pl.BlockSpec((CONV_WIDTH, 2 * M_WIDTH), lambda i: (0, 0)),
            pl.BlockSpec((1, 2 * M_WIDTH), lambda i: (0, 0)),
            pl.BlockSpec((1, LANES), lambda i: (0, 0)),
        ],
        out_specs=[
            pl.BlockSpec((R, M_WIDTH), lambda i: (i, 0)),
            pl.BlockSpec((M_WIDTH, R), lambda i: (0, i)),
            pl.BlockSpec((A_HEADS, R, A_HEAD_DIM), lambda i: (0, i, 0)),
            pl.BlockSpec((R, A_WIDTH), lambda i: (i, 0)),
            pl.BlockSpec((A_WIDTH, R), lambda i: (0, i)),
            pl.BlockSpec((1, 1, A_WIDTH), lambda i: (i, 0, 0)),
            pl.BlockSpec((R, LANES), lambda i: (i, 0)),
            pl.BlockSpec((SUBLANES, R), lambda i: (0, i)),
        ],
        out_shape=[
            jax.ShapeDtypeStruct((S, M_WIDTH), BF16),
            jax.ShapeDtypeStruct((M_WIDTH, S), BF16),
            jax.ShapeDtypeStruct((A_HEADS, S, A_HEAD_DIM), F32),
            jax.ShapeDtypeStruct((S, A_WIDTH), BF16),
            jax.ShapeDtypeStruct((A_WIDTH, S), BF16),
            jax.ShapeDtypeStruct((nb, 1, A_WIDTH), F32),
            jax.ShapeDtypeStruct((S, LANES), F32),
            jax.ShapeDtypeStruct((SUBLANES, S), F32),
        ],
        compiler_params=_cparams(("parallel",)),
        name="prep",
    )(P, P, P, P, gate, pos, invf, conv_w, conv_b, gbias)


M_AUG = M_HEAD_DIM + LANES
MLSTM_CHUNKS_PER_STEP = 16


def _mlstm_kernel(q_ref, kT_ref, v_ref, mo_ref, gc_ref, gt_ref, gain_ref, out_ref,
                  c_ref, m_ref):
    @pl.when(pl.program_id(0) == 0)
    def _():
        c_ref[...] = jnp.zeros_like(c_ref)
        m_ref[...] = jnp.zeros_like(m_ref)

    L = M_CHUNK
    D = M_HEAD_DIM
    row = lax.broadcasted_iota(jnp.int32, (L, L), 0)
    col = lax.broadcasted_iota(jnp.int32, (L, L), 1)
    causal = row >= col
    ones_col = (lax.broadcasted_iota(jnp.int32, (L, LANES), 1) == 0).astype(BF16)

    heads = range(M_HEADS)
    hsl = [slice(h * D, (h + 1) * D) for h in heads]

    def state_stages(rs):
        q = [q_ref[rs, hsl[h]] for h in heads]
        kT = [kT_ref[hsl[h], rs] for h in heads]
        v_aug = [jnp.concatenate([v_ref[rs, hsl[h]], ones_col], axis=1) for h in heads]
        b_c = [gc_ref[rs, M_HEADS + h:M_HEADS + h + 1] for h in heads]
        b_r = [gt_ref[M_HEADS + h:M_HEADS + h + 1, rs] for h in heads]
        u_r = [gt_ref[h:h + 1, rs] - b_r[h] for h in heads]
        f_tot = [b_r[h][:, L - 1:L] for h in heads]
        m_prev = [m_ref[h:h + 1, 0:1] for h in heads]

        s_qk = [jnp.dot(q[h], kT[h], preferred_element_type=F32) for h in heads]
        q_c = [jnp.dot(q[h], c_ref[h].astype(BF16), preferred_element_type=F32) for h in heads]

        for h in heads:
            w_r = f_tot[h] + u_r[h]
            m_loc = jnp.max(w_r, axis=1, keepdims=True)
            m_new = jnp.maximum(f_tot[h] + m_prev[h], m_loc)
            a = jnp.exp(f_tot[h] + m_prev[h] - m_new)
            e_r = jnp.exp(w_r - m_new)
            keT = (kT[h].astype(F32) * e_r).astype(BF16)
            c_ref[h] = a * c_ref[h] + jnp.dot(keT, v_aug[h], preferred_element_type=F32)
            m_ref[h:h + 1, :] = jnp.broadcast_to(m_new, (1, LANES))
        return rs, v_aug, b_c, u_r, m_prev, s_qk, q_c

    def output_stage(rs, v_aug, b_c, u_r, m_prev, s_qk, q_c):
        for h in heads:
            d_log = jnp.where(causal, b_c[h] + u_r[h], -jnp.inf)
            a_log = b_c[h] + m_prev[h]
            m_t = jnp.maximum(a_log, jnp.max(d_log, axis=1, keepdims=True))
            s_ts = s_qk[h] * jnp.exp(d_log - m_t)
            inter = jnp.exp(a_log - m_t)
            r = inter * q_c[h] + jnp.dot(s_ts.astype(BF16), v_aug[h], preferred_element_type=F32)
            num = r[:, :D]
            den = r[:, D:D + 1]
            hh = num / jnp.maximum(jnp.abs(den), jnp.exp(-m_t))
            hn = hh * lax.rsqrt(jnp.mean(hh * hh, axis=-1, keepdims=True) + NORM_EPS)
            out_ref[rs, hsl[h]] = (hn * gain_ref[:, hsl[h]]
                                   * _sigmoid(mo_ref[rs, hsl[h]].astype(F32))).astype(BF16)

    def chunk(c, carry):
        output_stage(*state_stages(pl.ds(pl.multiple_of(c * L, L), L)))
        return carry

    lax.fori_loop(0, MLSTM_CHUNKS_PER_STEP, chunk, 0)


def _mlstm(mq, mkT, P, gc, gt, gain):
    S = mq.shape[0]
    L = M_CHUNK * MLSTM_CHUNKS_PER_STEP
    return pl.pallas_call(
        _mlstm_kernel,
        grid=(S // L,),
        in_specs=[
            pl.BlockSpec((L, M_WIDTH), lambda c: (c, 0)),
            pl.BlockSpec((M_WIDTH, L), lambda c: (0, c)),
            pl.BlockSpec((L, M_WIDTH), lambda c: (c, COL_MV // M_WIDTH)),
            pl.BlockSpec((L, M_WIDTH), lambda c: (c, COL_MO // M_WIDTH)),
            pl.BlockSpec((L, LANES), lambda c: (c, 0)),
            pl.BlockSpec((SUBLANES, L), lambda c: (0, c)),
            pl.BlockSpec((1, M_WIDTH), lambda c: (0, 0)),
        ],
        out_specs=pl.BlockSpec((L, M_WIDTH), lambda c: (c, 0)),
        out_shape=jax.ShapeDtypeStruct((S, M_WIDTH), BF16),
        scratch_shapes=[
            pltpu.VMEM((M_HEADS, M_HEAD_DIM, M_AUG), F32),
            pltpu.VMEM((SUBLANES, LANES), F32),
        ],
        compiler_params=_cparams(("arbitrary",)),
        name="mlstm",
    )(mq, mkT, P, P, gc, gt, gain)


MOBA_HEADS_PER_GROUP = 4
ROUTE_TILES = 264
ROUTE_ROWS = ROUTE_TILES * MOBA_BLOCK
GROUP_TILES_PER_STEP = 33
ROW_WORDS = A_HEAD_DIM // 2
SC_WINDOW = 256


def _pack_rows(o_t, lse):
    q = o_t.shape[1]
    hi = pltpu.bitcast(o_t[:ROW_WORDS].astype(BF16).astype(F32), jnp.uint32)
    lo = pltpu.bitcast(o_t[ROW_WORDS:].astype(BF16).astype(F32), jnp.uint32)
    words = pltpu.bitcast(hi | (lo >> 16), F32)
    tail = jnp.concatenate([jnp.broadcast_to(lse, (SUBLANES, q)),
                            jnp.zeros((LANES - ROW_WORDS - SUBLANES, q), F32)], axis=0)
    return jnp.concatenate([words, tail], axis=0).T


def _dot_nt(a, b):
    return lax.dot_general(a, b, (((1,), (1,)), ((), ())), preferred_element_type=F32)


def _block_partial(k_blk, v_t_blk, qs, ones_rows, causal_mask=None):
    s = _dot_nt(k_blk, qs)
    if causal_mask is not None:
        s = jnp.where(causal_mask, s, NEG_BIG)
    sb = s.astype(BF16)
    m = jnp.max(sb, axis=0, keepdims=True)
    p = jnp.exp2(sb - m)
    r = jnp.dot(jnp.concatenate([v_t_blk, ones_rows], axis=0), p, preferred_element_type=F32)
    dh = v_t_blk.shape[0]
    l = r[dh:dh + 1, :]
    return r[:dh, :] / l, m.astype(F32) + jnp.log2(l)


def _route_kernel(q_ref, k_ref, vT_ref, km_ref, route_ref, cnt_ref, own_ref, cnt_acc):
    i = pl.program_id(1)
    BS = MOBA_BLOCK
    Dh = A_HEAD_DIM
    NB = km_ref.shape[0]
    G = q_ref.shape[0]
    blk = lax.broadcasted_iota(jnp.int32, (NB, BS), 0)
    kpos = lax.broadcasted_iota(jnp.int32, (BS, BS), 0)
    qpos = lax.broadcasted_iota(jnp.int32, (BS, BS), 1)
    earlier = (kpos < qpos).astype(BF16)
    ones_rows = jnp.ones((2 * SUBLANES, BS), BF16)

    @pl.when(i == 0)
    def _():
        cnt_acc[...] = jnp.zeros_like(cnt_acc)

    q = [q_ref[g] for g in range(G)]
    qs = [(q[g] * (Dh ** -0.5 * LOG2_E)).astype(BF16) for g in range(G)]
    parts = [_block_partial(k_ref[:, g * Dh:(g + 1) * Dh], vT_ref[g * Dh:(g + 1) * Dh, :], qs[g],
                            ones_rows, kpos <= qpos) for g in range(G)]

    gates = []
    for g in range(G):
        hs = slice(g * Dh, (g + 1) * Dh)
        km = km_ref[:, hs]
        kh = km.astype(BF16)
        kl = (km - kh.astype(F32)).astype(BF16)
        qh = q[g].astype(BF16)
        ql = (q[g] - qh.astype(F32)).astype(BF16)
        gate = _dot_nt(kh, qh) + _dot_nt(kh, ql) + _dot_nt(kl, qh)
        gates.append(jnp.where(blk < i, gate, -jnp.inf))
    picks = [[] for _ in range(G)]
    rows = [[] for _ in range(G)]
    for r in range(MOBA_TOPK):
        for g in range(G):
            mx = jnp.max(gates[g], axis=0, keepdims=True)
            idx = jnp.min(jnp.where(gates[g] == mx, blk, NB), axis=0, keepdims=True)
            idx = jnp.where(r < i, idx, -1)
            pick = blk == idx
            gates[g] = jnp.where(pick, -jnp.inf, gates[g])
            picks[g].append(pick)
            rows[g].append(idx)
    for g in range(G):
        onehot = sum(p.astype(F32) for p in picks[g])
        before = cnt_acc[g][:, 0:1] + jnp.dot(onehot.astype(BF16), earlier, preferred_element_type=F32)
        for r in range(MOBA_TOPK):
            rank = jnp.sum(jnp.where(picks[g][r], before, 0.0), axis=0, keepdims=True)
            rows[g].append(rank.astype(jnp.int32))
        rows[g].append(jnp.zeros((SUBLANES - 2 * MOBA_TOPK, BS), jnp.int32))
        route_ref[g] = jnp.concatenate(rows[g], axis=0)
        cnt_new = cnt_acc[g] + jnp.sum(onehot, axis=1, keepdims=True)
        cnt_acc[g] = cnt_new
        cnt_ref[g] = cnt_new
        own_ref[g] = _pack_rows(*parts[g])


def _route(aq_hm, ak, avT, kmean, h0, H):
    _, S, Dh = aq_hm.shape
    BS = MOBA_BLOCK
    NB = S // BS
    G = MOBA_HEADS_PER_GROUP
    W = G * Dh
    hb = h0 // G
    return pl.pallas_call(
        _route_kernel,
        grid=(H // G, NB),
        in_specs=[
            pl.BlockSpec((G, BS, Dh), lambda h, i: (hb + h, i, 0)),
            pl.BlockSpec((BS, W), lambda h, i: (i, hb + h)),
            pl.BlockSpec((W, BS), lambda h, i: (hb + h, i)),
            pl.BlockSpec((NB, W), lambda h, i: (0, hb + h)),
        ],
        out_specs=[
            pl.BlockSpec((G, SUBLANES, BS), lambda h, i: (h, 0, i)),
            pl.BlockSpec((G, NB, LANES), lambda h, i: (h, 0, 0)),
            pl.BlockSpec((G, BS, LANES), lambda h, i: (h, i, 0)),
        ],
        out_shape=[
            jax.ShapeDtypeStruct((H, SUBLANES, S), jnp.int32),
            jax.ShapeDtypeStruct((H, NB, LANES), F32),
            jax.ShapeDtypeStruct((H, S, LANES), F32),
        ],
        scratch_shapes=[pltpu.VMEM((G, NB, LANES), F32)],
        compiler_params=_cparams(("parallel", "arbitrary")),
        name="moba_route",
    )(aq_hm, ak, avT, kmean)


def _dest_kernel(seg_ref, route_ref, dest_ref):
    h = pl.program_id(0)
    NB = seg_ref.shape[1]
    r = route_ref[0]
    blk = r[0:MOBA_TOPK, :]
    rank = r[MOBA_TOPK:2 * MOBA_TOPK, :]
    base = h * ROUTE_ROWS
    dest = jnp.full(blk.shape, base + ROUTE_ROWS - 1, jnp.int32)
    for j in range(NB):
        dest = jnp.where(blk == j, base + seg_ref[h, j] + rank, dest)
    dest_ref[0] = jnp.concatenate(
        [dest, jnp.zeros((SUBLANES - MOBA_TOPK, dest.shape[1]), jnp.int32)], axis=0)


def _dest(seg_start, route):
    H, _, S = route.shape
    return pl.pallas_call(
        _dest_kernel,
        grid_spec=pltpu.PrefetchScalarGridSpec(
            num_scalar_prefetch=1,
            grid=(H,),
            in_specs=[pl.BlockSpec((1, SUBLANES, S), lambda h, seg: (h, 0, 0))],
            out_specs=pl.BlockSpec((1, SUBLANES, S), lambda h, seg: (h, 0, 0)),
        ),
        out_shape=jax.ShapeDtypeStruct((H, SUBLANES, S), jnp.int32),
        compiler_params=_cparams(("parallel",)),
        name="moba_dest",
    )(seg_start, route)


def _sc_mesh():
    return plsc.VectorSubcoreMesh(core_axis_name="c", subcore_axis_name="s")


def _sc_scatter(rows, row0, n_rows, idx, n_out):
    W = rows.shape[1]
    M = idx.shape[0]
    n_win = n_rows // SC_WINDOW
    win0 = row0 // SC_WINDOW

    reps = M // n_rows

    @pl.kernel(out_type=jax.ShapeDtypeStruct((n_out, W), rows.dtype), mesh=_sc_mesh())
    def k(x_hbm, i_hbm, o_hbm):
        def body(x_vmem, *i_vmems):
            for i_vmem in i_vmems:
                pltpu.sync_copy(x_vmem, o_hbm.at[i_vmem.at[0]])

        pltpu.emit_pipeline(
            body,
            grid=(n_win,),
            in_specs=([pl.BlockSpec((SC_WINDOW, W), lambda w: (win0 + w, 0))]
                      + [pl.BlockSpec((1, SC_WINDOW), lambda w, r=r: (0, r * n_win + w)) for r in range(reps)]),
            out_specs=[],
            core_axis_name=("c", "s"),
            dimension_semantics=(pltpu.PARALLEL,),
        )(x_hbm, *([i_hbm] * reps))

    return k(rows, idx.reshape(1, M))


def _sc_gather(table, idx):
    M = idx.shape[0]
    W = table.shape[1]

    @pl.kernel(out_type=jax.ShapeDtypeStruct((M, W), table.dtype), mesh=_sc_mesh())
    def k(x_hbm, i_hbm, o_hbm):
        def body(i_vmem, o_vmem):
            pltpu.sync_copy(x_hbm.at[i_vmem.at[0]], o_vmem)

        pltpu.emit_pipeline(
            body,
            grid=(M // SC_WINDOW,),
            in_specs=[pl.BlockSpec((1, SC_WINDOW), lambda w: (0, w))],
            out_specs=[pl.BlockSpec((SC_WINDOW, W), lambda w: (w, 0))],
            core_axis_name=("c", "s"),
            dimension_semantics=(pltpu.PARALLEL,),
        )(i_hbm, o_hbm)

    return k(table, idx.reshape(1, M))


def _group_kernel(tile_blk_ref, q_ref, k_ref, vT_ref, out_ref, s_ref, m_ref):
    h = pl.program_id(0)
    u = pl.program_id(1)
    last = pl.num_programs(1) - 2
    BS = MOBA_BLOCK
    Dh = A_HEAD_DIM
    T = GROUP_TILES_PER_STEP
    ones_rows = jnp.ones((2 * SUBLANES, BS), BF16)
    new = lax.rem(u, 2)
    old = 1 - new

    @pl.when((h == 0) & (u == 0))
    def _():
        s_ref[...] = jnp.zeros_like(s_ref)
        m_ref[...] = jnp.zeros_like(m_ref)

    def block_start(group, c):
        j = jnp.maximum(tile_blk_ref[h * ROUTE_TILES + group * T + c], 0)
        return pl.multiple_of(j * BS, BS)

    g_new = jnp.minimum(u, last)
    g_old = jnp.maximum(u - 1, 0)
    @pl.when(tile_blk_ref[h * ROUTE_TILES + g_old * T] >= 0)
    def _():
        ms, rs = [], []
        for c in range(T):
            m = m_ref[old, c]
            p = jnp.exp2(s_ref[old, c] - m.astype(BF16))
            v_aug = jnp.concatenate([vT_ref[:, pl.ds(block_start(g_old, c), BS)], ones_rows], axis=0)
            ms.append(m)
            rs.append(jnp.dot(v_aug, p, preferred_element_type=F32))
        s_new = []
        for c in range(T):
            qs = (q_ref[c * BS:(c + 1) * BS, :] * (Dh ** -0.5 * LOG2_E)).astype(BF16)
            s_new.append(_dot_nt(k_ref[pl.ds(block_start(g_new, c), BS), :], qs))
        for c in range(T):
            l = rs[c][Dh:Dh + 1, :]
            out_ref[c * BS:(c + 1) * BS, :] = _pack_rows(rs[c][:Dh, :] / l, ms[c] + jnp.log2(l))
            sb = s_new[c].astype(BF16)
            s_ref[new, c] = sb
            m_ref[new, c] = jnp.max(sb, axis=0, keepdims=True).astype(F32)


def _group(tile_blk, q_sorted, ak, avT, h0):
    S = ak.shape[0]
    H = q_sorted.shape[0] // ROUTE_ROWS
    Dh = A_HEAD_DIM
    BS = MOBA_BLOCK
    T = GROUP_TILES_PER_STEP
    steps = ROUTE_TILES // T
    rows = T * BS
    return pl.pallas_call(
        _group_kernel,
        grid_spec=pltpu.PrefetchScalarGridSpec(
            num_scalar_prefetch=1,
            grid=(H, steps + 1),
            in_specs=[
                pl.BlockSpec((rows, LANES), lambda h, u, tb: (h * steps + jnp.minimum(u, steps - 1), 0)),
                pl.BlockSpec((S, Dh), lambda h, u, tb: (0, h0 + h)),
                pl.BlockSpec((Dh, S), lambda h, u, tb: (h0 + h, 0)),
            ],
            out_specs=pl.BlockSpec((rows, LANES), lambda h, u, tb: (h * steps + jnp.maximum(u - 1, 0), 0)),
            scratch_shapes=[pltpu.VMEM((2, T, BS, BS), BF16), pltpu.VMEM((2, T, 1, BS), F32)],
        ),
        out_shape=jax.ShapeDtypeStruct((H * ROUTE_ROWS, LANES), F32),
        compiler_params=_cparams(("arbitrary", "arbitrary")),
        name="moba_group",
    )(tile_blk, q_sorted, ak, avT)


def _combine_kernel(*refs):
    i = pl.program_id(0)
    o_ref = refs[-1]
    n_groups = (len(refs) - 1) // 2
    H = refs[n_groups].shape[0]
    BS = refs[n_groups].shape[1]
    low_half = lax.broadcasted_iota(jnp.int32, (BS, LANES), 1) < ROW_WORDS
    for head in range(n_groups * H):
        got_ref, own_ref, h = refs[head // H], refs[n_groups + head // H], head % H
        tiles = [own_ref[h]]
        for r in range(MOBA_TOPK):
            tiles.append(jnp.where(r < i, got_ref[h, r], 0.0))
        lses = [tiles[0][:, ROW_WORDS:ROW_WORDS + 1]]
        lses += [jnp.where(r < i, tiles[r + 1][:, ROW_WORDS:ROW_WORDS + 1], NEG_BIG)
                 for r in range(MOBA_TOPK)]
        top = functools.reduce(jnp.maximum, lses)
        w = [jnp.exp2(x - top) for x in lses]
        inv = 1.0 / sum(w)
        first = jnp.zeros((BS, LANES), F32)
        second = jnp.zeros((BS, LANES), F32)
        for wk, tile in zip(w, tiles):
            words = pltpu.bitcast(tile, jnp.uint32)
            first = first + wk * pltpu.bitcast(words & jnp.uint32(0xFFFF0000), F32)
            second = second + wk * pltpu.bitcast(words << 16, F32)
        o = jnp.where(low_half, first, pltpu.roll(second, ROW_WORDS, 1)) * inv
        o_ref[:, head * A_HEAD_DIM:(head + 1) * A_HEAD_DIM] = o.astype(BF16)


def _combine(gots, owns):
    H, S, _ = owns[0].shape
    BS = MOBA_BLOCK
    return pl.pallas_call(
        _combine_kernel,
        grid=(S // BS,),
        in_specs=([pl.BlockSpec((H, MOBA_TOPK, BS, LANES), lambda i: (0, 0, i, 0))] * len(gots)
                  + [pl.BlockSpec((H, BS, LANES), lambda i: (0, i, 0))] * len(owns)),
        out_specs=pl.BlockSpec((BS, A_WIDTH), lambda i: (i, 0)),
        out_shape=jax.ShapeDtypeStruct((S, A_WIDTH), BF16),
        compiler_params=_cparams(("parallel",)),
        name="moba_combine",
    )(*gots, *owns)


def _moba_routed(aq_hm, ak, avT, kmean):
    H, S, Dh = aq_hm.shape
    NB = S // MOBA_BLOCK
    hn = MOBA_HEADS_PER_GROUP
    q_rows = aq_hm.reshape(H * S, Dh)
    gots, owns = [], []
    for h0 in range(0, H, hn):
        route, cnt, own = _route(aq_hm, ak, avT, kmean, h0, hn)
        cnt = cnt[:, :, 0].astype(jnp.int32)
        seg_tiles = (cnt + MOBA_BLOCK - 1) // MOBA_BLOCK
        seg_end = jnp.cumsum(seg_tiles, axis=1)
        seg_start = (seg_end - seg_tiles) * MOBA_BLOCK
        tile_ids = jnp.arange(ROUTE_TILES, dtype=jnp.int32)
        tile_blk = jnp.sum(tile_ids[None, :, None] >= seg_end[:, None, :], axis=2).astype(jnp.int32)
        tile_blk = jnp.where(tile_blk < NB, tile_blk, -1).reshape(hn * ROUTE_TILES)
        dest = _dest(seg_start, route)[:, :MOBA_TOPK, :]
        q_sorted = _sc_scatter(q_rows, h0 * S, hn * S, dest.transpose(1, 0, 2).reshape(-1),
                               hn * ROUTE_ROWS)
        results = _group(tile_blk, q_sorted, ak, avT, h0)
        gots.append(_sc_gather(results, dest.reshape(-1)).reshape(hn, MOBA_TOPK, S, LANES))
        owns.append(own)
    return _combine(gots, owns)


def _merge_kernel(hm_ref, ha_ref, gm_ref, ga_ref, x_ref, wm_ref, wa_ref, wo_ref, gain_ref,
                  out_ref):
    ym = jnp.dot(hm_ref[...], wm_ref[...], preferred_element_type=F32)
    ya = jnp.dot(ha_ref[...], wa_ref[...], preferred_element_type=F32)
    merged = _sigmoid(gm_ref[...].astype(F32)) * ym + _sigmoid(ga_ref[...].astype(F32)) * ya
    mix = jnp.dot(merged.astype(BF16), wo_ref[...], preferred_element_type=F32)
    out_ref[...] = x_ref[...] + _rms(mix, gain_ref[...])


def _merge(hm, ha, P, x, b, wm, wa, wo, gain, tm=256):
    _, S, D = x.shape
    const = pl.Buffered(1)
    return pl.pallas_call(
        _merge_kernel,
        grid=(S // tm,),
        in_specs=[
            pl.BlockSpec((tm, M_WIDTH), lambda i: (i, 0)),
            pl.BlockSpec((tm, A_WIDTH), lambda i: (i, 0)),
            pl.BlockSpec((tm, D), lambda i: (i, COL_GM // D)),
            pl.BlockSpec((tm, D), lambda i: (i, COL_GA // D)),
            pl.BlockSpec((None, tm, D), lambda i: (b, i, 0)),
            pl.BlockSpec((M_WIDTH, D), lambda i: (0, 0), pipeline_mode=const),
            pl.BlockSpec((A_WIDTH, D), lambda i: (0, 0), pipeline_mode=const),
            pl.BlockSpec((D, D), lambda i: (0, 0), pipeline_mode=const),
            pl.BlockSpec((1, D), lambda i: (0, 0)),
        ],
        out_specs=pl.BlockSpec((tm, D), lambda i: (i, 0)),
        out_shape=jax.ShapeDtypeStruct((S, D), F32),
        compiler_params=_cparams(("parallel",)),
        name="merge",
    )(hm, ha, P, P, x, wm, wa, wo, gain)


def _ffn_kernel(x_ref, gpre_ref, wu_ref, wd_ref, gpost_ref, out_ref, hn_ref, acc_ref):
    f = pl.program_id(1)

    @pl.when(f == 0)
    def _():
        hn_ref[...] = _rms(x_ref[...], gpre_ref[...]).astype(BF16)
        acc_ref[...] = jnp.zeros_like(acc_ref)

    u = jnp.dot(hn_ref[...], wu_ref[...], preferred_element_type=F32)
    u = jnp.square(jnp.maximum(u, 0.0)).astype(BF16)
    acc_ref[...] += jnp.dot(u, wd_ref[...], preferred_element_type=F32)

    @pl.when(f == pl.num_programs(1) - 1)
    def _():
        out_ref[...] = x_ref[...] + _rms(acc_ref[...], gpost_ref[...])


def _ffn(x, gpre, wu, wd, gpost, tm=512, tf=1024):
    S, D = x.shape
    Fd = wu.shape[1]
    return pl.pallas_call(
        _ffn_kernel,
        grid=(S // tm, Fd // tf),
        in_specs=[
            pl.BlockSpec((tm, D), lambda i, f: (i, 0)),
            pl.BlockSpec((1, D), lambda i, f: (0, 0)),
            pl.BlockSpec((D, tf), lambda i, f: (0, f)),
            pl.BlockSpec((tf, D), lambda i, f: (f, 0)),
            pl.BlockSpec((1, D), lambda i, f: (0, 0)),
        ],
        out_specs=pl.BlockSpec((tm, D), lambda i, f: (i, 0)),
        out_shape=jax.ShapeDtypeStruct((S, D), F32),
        scratch_shapes=[pltpu.VMEM((tm, D), BF16), pltpu.VMEM((tm, D), F32)],
        compiler_params=_cparams(("parallel", "arbitrary")),
        name="ffn",
    )(x, gpre, wu, wd, gpost)


def _layer(x, b, pos, norm_mix_pre, w_in, conv_w, conv_b, i_bias, f_bias, mlstm_norm,
           w_branch_m, w_branch_a, w_out, norm_mix_post, norm_ffn_pre, w_up, w_down,
           norm_ffn_post):
    _, S, D = x.shape
    o = 0
    pieces = {}
    w_in = w_in.astype(BF16)
    for name, width in (("mq", M_WIDTH), ("mk", M_WIDTH), ("mv", M_WIDTH), ("mo", M_WIDTH),
                        ("mi", M_HEADS), ("mf", M_HEADS), ("aq", A_WIDTH), ("ak", A_WIDTH),
                        ("av", A_WIDTH), ("gm", D), ("ga", D)):
        pieces[name] = w_in[:, o:o + width]
        o += width
    w_all = jnp.concatenate([pieces[n] for n in ("gm", "ga", "mq", "mk", "aq", "ak", "mv", "mo", "av")],
                            axis=1)
    w_gate = jnp.concatenate(
        [pieces["mi"], pieces["mf"], jnp.zeros((D, LANES - 2 * M_HEADS), BF16)], axis=1)
    gbias = jnp.concatenate([i_bias, f_bias, jnp.zeros((LANES - 2 * M_HEADS,), F32)])[None, :]
    half = jnp.arange(0, A_HEAD_DIM, 2, dtype=F32) / A_HEAD_DIM
    inv_freq = 1.0 / (ROPE_THETA ** half)
    invf = jnp.concatenate([inv_freq, inv_freq])[None, :]

    P, gate = _proj(x, b, norm_mix_pre[None, :], w_all, w_gate)
    mq, mkT, aq, ak, avT, kmean, gc, gt = _prep(
        P, gate, pos.reshape(S, 1), invf, conv_w, conv_b[None, :], gbias)
    hm = _mlstm(mq, mkT, P, gc, gt, mlstm_norm[None, :])
    ha = _moba_routed(aq, ak, avT, kmean.reshape(S // MOBA_BLOCK, A_WIDTH))
    x1 = _merge(hm, ha, P, x, b, w_branch_m.astype(BF16), w_branch_a.astype(BF16),
                w_out.astype(BF16), norm_mix_post[None, :])
    return _ffn(x1, norm_ffn_pre[None, :], w_up.astype(BF16), w_down.astype(BF16),
                norm_ffn_post[None, :])


def kernel(x, positions, norm_mix_pre, w_in, conv_w, conv_b, i_bias, f_bias, mlstm_norm,
           w_branch_m, w_branch_a, w_out, norm_mix_post, norm_ffn_pre, w_up, w_down,
           norm_ffn_post):
    B = x.shape[0]
    depth = w_in.shape[0]
    outs = []
    def take(a, i):
        return a.reshape(a.shape[1:]) if a.shape[0] == 1 else a[i]

    for b in range(B):
        xin, bi = x, b
        for l in range(depth):
            xb = _layer(xin, bi, take(positions, b), take(norm_mix_pre, l), take(w_in, l),
                        take(conv_w, l), take(conv_b, l), take(i_bias, l), take(f_bias, l),
                        take(mlstm_norm, l), take(w_branch_m, l), take(w_branch_a, l),
                        take(w_out, l), take(norm_mix_post, l), take(norm_ffn_pre, l),
                        take(w_up, l), take(w_down, l), take(norm_ffn_post, l))
            xin, bi = xb[None], 0
        outs.append(xb)
    return outs[0][None] if B == 1 else jnp.stack(outs, axis=0)
```
